```python
import math
import jax, jax.numpy as jnp
from jax import lax
import numpy as np

D_MODEL = 1024
BATCH = 8
SEQ = 8192
DEPTH = 2

CONV_DIM = D_MODEL
CONV_KERNEL = 31
SSM_EXPAND = 2
SSM_DIM = SSM_EXPAND * D_MODEL
SSM_HEAD_DIM = 64
SSM_HEADS = SSM_DIM // SSM_HEAD_DIM
SSM_GROUPS = 4
SSM_STATE = 128
SSM_CONV = 4
SSM_CHUNK = 128
SSM_BC = 2 * SSM_GROUPS * SSM_STATE
FFN_DIM = 2816
FFN_CONV = 3
DN_ALPHA = (2 * DEPTH) ** 0.25
DN_BETA = (8 * DEPTH) ** -0.25
LN_EPS = 1e-5
RMS_EPS = 1e-5
IN_SIZES = (2 * CONV_DIM, SSM_DIM, SSM_DIM + SSM_BC, SSM_HEADS, D_MODEL, D_MODEL)
IN_DIM = sum(IN_SIZES)

kernel_name = "hybrid_conformer_ssd_deepnorm"


def layer_norm(x, g, b):
    xf = x.astype(jnp.float32)
    mu = jnp.mean(xf, axis=-1, keepdims=True)
    var = jnp.mean(jnp.square(xf - mu), axis=-1, keepdims=True)
    return ((xf - mu) * lax.rsqrt(var + LN_EPS) * g.astype(jnp.float32)
            + b.astype(jnp.float32)).astype(x.dtype)


def causal_dwconv(x, w, b):
    k = w.shape[0]
    y = lax.conv_general_dilated(
        x, w[:, None, :].astype(x.dtype), window_strides=(1,),
        padding=[(k - 1, 0)], dimension_numbers=("NWC", "WIO", "NWC"),
        feature_group_count=x.shape[-1])
    return y + b.astype(x.dtype)


def split_in(u):
    idx = [int(v) for v in np.cumsum(IN_SIZES)[:-1]]
    return jnp.split(u, idx, axis=-1)


def conformer_branch(u_glu, dw_w, dw_b, ln_g, ln_b, w_out):
    a, g = jnp.split(u_glu, 2, axis=-1)
    v = a * jax.nn.sigmoid(g)
    v = causal_dwconv(v, dw_w, dw_b)
    v = jax.nn.silu(layer_norm(v, ln_g, ln_b))
    return v @ w_out


def ssd_chunked(x, dt, a_head, bm, cm):
    b, t, h, p = x.shape
    g, n = bm.shape[2], bm.shape[3]
    r = h // g
    L = SSM_CHUNK
    c = t // L
    xs = (x * dt[..., None]).reshape(b, c, L, g, r, p)
    a = (dt * a_head).reshape(b, c, L, g, r).transpose(0, 1, 3, 4, 2)
    bc = bm.reshape(b, c, L, g, n)
    cc = cm.reshape(b, c, L, g, n)
    a_cs = jnp.cumsum(a, axis=-1)
    seg = a_cs[..., :, None] - a_cs[..., None, :]
    causal = jnp.tril(jnp.ones((L, L), dtype=bool))
    decay = jnp.where(causal, jnp.exp(jnp.where(causal, seg, 0.0)), 0.0)
    cb = jnp.einsum("bclgn,bcsgn->bcgls", cc, bc)
    y_diag = jnp.einsum("bcgrls,bcsgrp->bclgrp", cb[:, :, :, None] * decay, xs)
    decay_states = jnp.exp(a_cs[..., -1:] - a_cs)
    states = jnp.einsum("bclgn,bcgrl,bclgrp->bcgrpn", bc, decay_states, xs)
    chunk_decay = jnp.exp(a_cs[..., -1])

    def step(carry, inp):
        s_c, d_c = inp
        return d_c[..., None, None] * carry + s_c, carry

    init = jnp.zeros((b, g, r, p, n), states.dtype)
    _, prev = lax.scan(step, init, (jnp.moveaxis(states, 1, 0),
                                    jnp.moveaxis(chunk_decay, 1, 0)))
    prev = jnp.moveaxis(prev, 0, 1)
    y_off = jnp.einsum("bclgn,bcgrpn,bcgrl->bclgrp", cc, prev, jnp.exp(a_cs))
    return (y_diag + y_off).reshape(b, t, h, p)


def ssd_branch(z, xbc, dt_raw, conv_w, conv_b, dt_bias, a_log, d_skip, norm_w, w_out):
    bsz, t, _ = z.shape
    xbc = jax.nn.silu(causal_dwconv(xbc, conv_w, conv_b))
    gn = SSM_GROUPS * SSM_STATE
    xs = xbc[..., :SSM_DIM]
    bm = xbc[..., SSM_DIM:SSM_DIM + gn].reshape(bsz, t, SSM_GROUPS, SSM_STATE)
    cm = xbc[..., SSM_DIM + gn:].reshape(bsz, t, SSM_GROUPS, SSM_STATE)
    dt = jax.nn.softplus(dt_raw.astype(jnp.float32) + dt_bias.astype(jnp.float32))
    a_head = -jnp.exp(a_log.astype(jnp.float32))
    xh = xs.reshape(bsz, t, SSM_HEADS, SSM_HEAD_DIM).astype(jnp.float32)
    y = ssd_chunked(xh, dt, a_head, bm.astype(jnp.float32), cm.astype(jnp.float32))
    y = y + xh * d_skip.astype(jnp.float32)[:, None]
    yg = (y.reshape(bsz, t, SSM_DIM) * jax.nn.silu(z.astype(jnp.float32)))
    yg = yg.reshape(bsz, t, SSM_GROUPS, SSM_DIM // SSM_GROUPS)
    yg = yg * lax.rsqrt(jnp.mean(jnp.square(yg), axis=-1, keepdims=True) + RMS_EPS)
    yn = (yg.reshape(bsz, t, SSM_DIM) * norm_w.astype(jnp.float32)).astype(z.dtype)
    return yn @ w_out


def conv_ffn(h, w_up, dw_w, dw_b, w_down):
    u = causal_dwconv(h @ w_up, dw_w, dw_b)
    gate, val = jnp.split(u, 2, axis=-1)
    return (jax.nn.silu(gate) * val) @ w_down


def _fwd_setup_inputs(seed: int = 0) -> dict:
    key = jax.random.key(seed)
    ks = jax.random.split(key, 32)
    f32 = jnp.float32
    nrm = lambda k, shape, s: (jax.random.normal(k, shape, f32) * s).astype(f32)
    gain = lambda k, shape: 1.0 + 0.05 * jax.random.normal(k, shape, f32)
    small = lambda k, shape: 0.02 * jax.random.normal(k, shape, f32)
    dt0 = jnp.exp(jax.random.uniform(ks[12], (DEPTH, SSM_HEADS), f32)
                  * (math.log(0.1) - math.log(0.001)) + math.log(0.001))
    return {
        "x": jax.random.normal(ks[0], (BATCH, SEQ, D_MODEL), f32),
        "ln_in_g": gain(ks[1], (D_MODEL,)),
        "ln_in_b": small(ks[2], (D_MODEL,)),
        "w_in": nrm(ks[3], (DEPTH, D_MODEL, IN_DIM), D_MODEL ** -0.5),
        "conv_dw_w": nrm(ks[4], (DEPTH, CONV_KERNEL, CONV_DIM), CONV_KERNEL ** -0.5),
        "conv_dw_b": small(ks[5], (DEPTH, CONV_DIM)),
        "conv_ln_g": gain(ks[6], (DEPTH, CONV_DIM)),
        "conv_ln_b": small(ks[7], (DEPTH, CONV_DIM)),
        "w_conv_out": nrm(ks[8], (DEPTH, CONV_DIM, D_MODEL), DN_BETA * CONV_DIM ** -0.5),
        "ssm_conv_w": nrm(ks[9], (DEPTH, SSM_CONV, SSM_DIM + SSM_BC), SSM_CONV ** -0.5),
        "ssm_conv_b": small(ks[10], (DEPTH, SSM_DIM + SSM_BC)),
        "ssm_dt_bias": dt0 + jnp.log(-jnp.expm1(-dt0)),
        "ssm_a_log": jnp.log(jax.random.uniform(ks[13], (DEPTH, SSM_HEADS), f32, 1.0, 16.0)),
        "ssm_d": gain(ks[14], (DEPTH, SSM_HEADS)),
        "ssm_norm_w": gain(ks[15], (DEPTH, SSM_DIM)),
        "w_ssm_out": nrm(ks[16], (DEPTH, SSM_DIM, D_MODEL), DN_BETA * SSM_DIM ** -0.5),
        "w_o": nrm(ks[17], (DEPTH, D_MODEL, D_MODEL), DN_BETA * D_MODEL ** -0.5),
        "ln1_g": gain(ks[18], (DEPTH, D_MODEL)),
        "ln1_b": small(ks[19], (DEPTH, D_MODEL)),
        "w_ffn_up": nrm(ks[20], (DEPTH, D_MODEL, 2 * FFN_DIM), DN_BETA * D_MODEL ** -0.5),
        "ffn_dw_w": nrm(ks[21], (DEPTH, FFN_CONV, 2 * FFN_DIM), FFN_CONV ** -0.5),
        "ffn_dw_b": small(ks[22], (DEPTH, 2 * FFN_DIM)),
        "w_ffn_down": nrm(ks[23], (DEPTH, FFN_DIM, D_MODEL), DN_BETA * FFN_DIM ** -0.5),
        "ln2_g": gain(ks[24], (DEPTH, D_MODEL)),
        "ln2_b": small(ks[25], (DEPTH, D_MODEL)),
    }


def _fwd_reference(x, ln_in_g, ln_in_b, w_in, conv_dw_w, conv_dw_b, conv_ln_g, conv_ln_b,
              w_conv_out, ssm_conv_w, ssm_conv_b, ssm_dt_bias, ssm_a_log, ssm_d,
              ssm_norm_w, w_ssm_out, w_o, ln1_g, ln1_b, w_ffn_up, ffn_dw_w, ffn_dw_b,
              w_ffn_down, ln2_g, ln2_b):
    h = layer_norm(x, ln_in_g, ln_in_b)
    for l in range(DEPTH):
        u = h @ w_in[l]
        u_glu, z, xbc, dt_raw, gate_a, gate_b = split_in(u)
        y_a = conformer_branch(u_glu, conv_dw_w[l], conv_dw_b[l], conv_ln_g[l],
                               conv_ln_b[l], w_conv_out[l])
        y_b = ssd_branch(z, xbc, dt_raw, ssm_conv_w[l], ssm_conv_b[l], ssm_dt_bias[l],
                         ssm_a_log[l], ssm_d[l], ssm_norm_w[l], w_ssm_out[l])
        mix = (jax.nn.sigmoid(gate_a) * y_a + jax.nn.sigmoid(gate_b) * y_b) @ w_o[l]
        h = layer_norm(DN_ALPHA * h + mix, ln1_g[l], ln1_b[l])
        ffn = conv_ffn(h, w_ffn_up[l], ffn_dw_w[l], ffn_dw_b[l], w_ffn_down[l])
        h = layer_norm(DN_ALPHA * h + ffn, ln2_g[l], ln2_b[l])
    return h


import jax as _jax
import jax.numpy as _jnp

TWIN_FORMAT = 'train_step'
FWD_PARAMS = ['x', 'ln_in_g', 'ln_in_b', 'w_in', 'conv_dw_w', 'conv_dw_b', 'conv_ln_g', 'conv_ln_b', 'w_conv_out', 'ssm_conv_w', 'ssm_conv_b', 'ssm_dt_bias', 'ssm_a_log', 'ssm_d', 'ssm_norm_w', 'w_ssm_out', 'w_o', 'ln1_g', 'ln1_b', 'w_ffn_up', 'ffn_dw_w', 'ffn_dw_b', 'w_ffn_down', 'ln2_g', 'ln2_b']
TWIN_WEIGHTS = ['ln_in_g', 'ln_in_b', 'w_in', 'conv_dw_w', 'conv_dw_b', 'conv_ln_g', 'conv_ln_b', 'w_conv_out', 'ssm_conv_w', 'ssm_conv_b', 'ssm_dt_bias', 'ssm_a_log', 'ssm_d', 'ssm_norm_w', 'w_ssm_out', 'w_o', 'ln1_g', 'ln1_b', 'w_ffn_up', 'ffn_dw_w', 'ffn_dw_b', 'w_ffn_down', 'ln2_g', 'ln2_b']
TWIN_DIFF_INPUT = 'x'
TWIN_INPUTS = ['x', 'ln_in_g', 'ln_in_b', 'w_in', 'conv_dw_w', 'conv_dw_b', 'conv_ln_g', 'conv_ln_b', 'w_conv_out', 'ssm_conv_w', 'ssm_conv_b', 'ssm_dt_bias', 'ssm_a_log', 'ssm_d', 'ssm_norm_w', 'w_ssm_out', 'w_o', 'ln1_g', 'ln1_b', 'w_ffn_up', 'ffn_dw_w', 'ffn_dw_b', 'w_ffn_down', 'ln2_g', 'ln2_b', 'loss_target', 'm_ln_in_g', 'm_ln_in_b', 'm_w_in', 'm_conv_dw_w', 'm_conv_dw_b', 'm_conv_ln_g', 'm_conv_ln_b', 'm_w_conv_out', 'm_ssm_conv_w', 'm_ssm_conv_b', 'm_ssm_dt_bias', 'm_ssm_a_log', 'm_ssm_d', 'm_ssm_norm_w', 'm_w_ssm_out', 'm_w_o', 'm_ln1_g', 'm_ln1_b', 'm_w_ffn_up', 'm_ffn_dw_w', 'm_ffn_dw_b', 'm_w_ffn_down', 'm_ln2_g', 'm_ln2_b', 'v_ln_in_g', 'v_ln_in_b', 'v_w_in', 'v_conv_dw_w', 'v_conv_dw_b', 'v_conv_ln_g', 'v_conv_ln_b', 'v_w_conv_out', 'v_ssm_conv_w', 'v_ssm_conv_b', 'v_ssm_dt_bias', 'v_ssm_a_log', 'v_ssm_d', 'v_ssm_norm_w', 'v_w_ssm_out', 'v_w_o', 'v_ln1_g', 'v_ln1_b', 'v_w_ffn_up', 'v_ffn_dw_w', 'v_ffn_dw_b', 'v_w_ffn_down', 'v_ln2_g', 'v_ln2_b']
TWIN_OUTPUTS = ['loss', 'grad_x', 'grad_ln_in_g', 'grad_ln_in_b', 'grad_w_in', 'grad_conv_dw_w', 'grad_conv_dw_b', 'grad_conv_ln_g', 'grad_conv_ln_b', 'grad_w_conv_out', 'grad_ssm_conv_w', 'grad_ssm_conv_b', 'grad_ssm_dt_bias', 'grad_ssm_a_log', 'grad_ssm_d', 'grad_ssm_norm_w', 'grad_w_ssm_out', 'grad_w_o', 'grad_ln1_g', 'grad_ln1_b', 'grad_w_ffn_up', 'grad_ffn_dw_w', 'grad_ffn_dw_b', 'grad_w_ffn_down', 'grad_ln2_g', 'grad_ln2_b', 'delta_ln_in_g', 'delta_ln_in_b', 'delta_w_in', 'delta_conv_dw_w', 'delta_conv_dw_b', 'delta_conv_ln_g', 'delta_conv_ln_b', 'delta_w_conv_out', 'delta_ssm_conv_w', 'delta_ssm_conv_b', 'delta_ssm_dt_bias', 'delta_ssm_a_log', 'delta_ssm_d', 'delta_ssm_norm_w', 'delta_w_ssm_out', 'delta_w_o', 'delta_ln1_g', 'delta_ln1_b', 'delta_w_ffn_up', 'delta_ffn_dw_w', 'delta_ffn_dw_b', 'delta_w_ffn_down', 'delta_ln2_g', 'delta_ln2_b', 'new_m_ln_in_g', 'new_m_ln_in_b', 'new_m_w_in', 'new_m_conv_dw_w', 'new_m_conv_dw_b', 'new_m_conv_ln_g', 'new_m_conv_ln_b', 'new_m_w_conv_out', 'new_m_ssm_conv_w', 'new_m_ssm_conv_b', 'new_m_ssm_dt_bias', 'new_m_ssm_a_log', 'new_m_ssm_d', 'new_m_ssm_norm_w', 'new_m_w_ssm_out', 'new_m_w_o', 'new_m_ln1_g', 'new_m_ln1_b', 'new_m_w_ffn_up', 'new_m_ffn_dw_w', 'new_m_ffn_dw_b', 'new_m_w_ffn_down', 'new_m_ln2_g', 'new_m_ln2_b', 'new_v_ln_in_g', 'new_v_ln_in_b', 'new_v_w_in', 'new_v_conv_dw_w', 'new_v_conv_dw_b', 'new_v_conv_ln_g', 'new_v_conv_ln_b', 'new_v_w_conv_out', 'new_v_ssm_conv_w', 'new_v_ssm_conv_b', 'new_v_ssm_dt_bias', 'new_v_ssm_a_log', 'new_v_ssm_d', 'new_v_ssm_norm_w', 'new_v_w_ssm_out', 'new_v_w_o', 'new_v_ln1_g', 'new_v_ln1_b', 'new_v_w_ffn_up', 'new_v_ffn_dw_w', 'new_v_ffn_dw_b', 'new_v_w_ffn_down', 'new_v_ln2_g', 'new_v_ln2_b']
TWIN_LEAF_KINDS = {'loss': 'loss', 'grad_x': 'grad_x', 'grad_ln_in_g': 'grad_w', 'grad_ln_in_b': 'grad_w', 'grad_w_in': 'grad_w', 'grad_conv_dw_w': 'grad_w', 'grad_conv_dw_b': 'grad_w', 'grad_conv_ln_g': 'grad_w', 'grad_conv_ln_b': 'grad_w', 'grad_w_conv_out': 'grad_w', 'grad_ssm_conv_w': 'grad_w', 'grad_ssm_conv_b': 'grad_w', 'grad_ssm_dt_bias': 'grad_w', 'grad_ssm_a_log': 'grad_w', 'grad_ssm_d': 'grad_w', 'grad_ssm_norm_w': 'grad_w', 'grad_w_ssm_out': 'grad_w', 'grad_w_o': 'grad_w', 'grad_ln1_g': 'grad_w', 'grad_ln1_b': 'grad_w', 'grad_w_ffn_up': 'grad_w', 'grad_ffn_dw_w': 'grad_w', 'grad_ffn_dw_b': 'grad_w', 'grad_w_ffn_down': 'grad_w', 'grad_ln2_g': 'grad_w', 'grad_ln2_b': 'grad_w', 'delta_ln_in_g': 'delta_w', 'delta_ln_in_b': 'delta_w', 'delta_w_in': 'delta_w', 'delta_conv_dw_w': 'delta_w', 'delta_conv_dw_b': 'delta_w', 'delta_conv_ln_g': 'delta_w', 'delta_conv_ln_b': 'delta_w', 'delta_w_conv_out': 'delta_w', 'delta_ssm_conv_w': 'delta_w', 'delta_ssm_conv_b': 'delta_w', 'delta_ssm_dt_bias': 'delta_w', 'delta_ssm_a_log': 'delta_w', 'delta_ssm_d': 'delta_w', 'delta_ssm_norm_w': 'delta_w', 'delta_w_ssm_out': 'delta_w', 'delta_w_o': 'delta_w', 'delta_ln1_g': 'delta_w', 'delta_ln1_b': 'delta_w', 'delta_w_ffn_up': 'delta_w', 'delta_ffn_dw_w': 'delta_w', 'delta_ffn_dw_b': 'delta_w', 'delta_w_ffn_down': 'delta_w', 'delta_ln2_g': 'delta_w', 'delta_ln2_b': 'delta_w', 'new_m_ln_in_g': 'new_m', 'new_m_ln_in_b': 'new_m', 'new_m_w_in': 'new_m', 'new_m_conv_dw_w': 'new_m', 'new_m_conv_dw_b': 'new_m', 'new_m_conv_ln_g': 'new_m', 'new_m_conv_ln_b': 'new_m', 'new_m_w_conv_out': 'new_m', 'new_m_ssm_conv_w': 'new_m', 'new_m_ssm_conv_b': 'new_m', 'new_m_ssm_dt_bias': 'new_m', 'new_m_ssm_a_log': 'new_m', 'new_m_ssm_d': 'new_m', 'new_m_ssm_norm_w': 'new_m', 'new_m_w_ssm_out': 'new_m', 'new_m_w_o': 'new_m', 'new_m_ln1_g': 'new_m', 'new_m_ln1_b': 'new_m', 'new_m_w_ffn_up': 'new_m', 'new_m_ffn_dw_w': 'new_m', 'new_m_ffn_dw_b': 'new_m', 'new_m_w_ffn_down': 'new_m', 'new_m_ln2_g': 'new_m', 'new_m_ln2_b': 'new_m', 'new_v_ln_in_g': 'new_v', 'new_v_ln_in_b': 'new_v', 'new_v_w_in': 'new_v', 'new_v_conv_dw_w': 'new_v', 'new_v_conv_dw_b': 'new_v', 'new_v_conv_ln_g': 'new_v', 'new_v_conv_ln_b': 'new_v', 'new_v_w_conv_out': 'new_v', 'new_v_ssm_conv_w': 'new_v', 'new_v_ssm_conv_b': 'new_v', 'new_v_ssm_dt_bias': 'new_v', 'new_v_ssm_a_log': 'new_v', 'new_v_ssm_d': 'new_v', 'new_v_ssm_norm_w': 'new_v', 'new_v_w_ssm_out': 'new_v', 'new_v_w_o': 'new_v', 'new_v_ln1_g': 'new_v', 'new_v_ln1_b': 'new_v', 'new_v_w_ffn_up': 'new_v', 'new_v_ffn_dw_w': 'new_v', 'new_v_ffn_dw_b': 'new_v', 'new_v_w_ffn_down': 'new_v', 'new_v_ln2_g': 'new_v', 'new_v_ln2_b': 'new_v'}


def _forward(args):
    return _fwd_reference(*[args[k] for k in FWD_PARAMS])


def _output_shape():
    def fwd():
        inp = _fwd_setup_inputs(0)
        return _fwd_reference(*[inp[k] for k in FWD_PARAMS])
    out = _jax.eval_shape(fwd)
    return out.shape, out.dtype

N_MICROBATCH = 1
ADAM_LR = 0.001
ADAM_B1 = 0.9
ADAM_B2 = 0.999
ADAM_EPS = 1e-08
ADAM_WD = 0.01
ADAM_STEP = 10
PER_EXAMPLE_BATCH_AXIS = {'x': 0, 'loss_target': 0}
SHARED_INPUTS = []
_WEIGHT_DTYPES = {'ln_in_g': _jnp.float32, 'ln_in_b': _jnp.float32, 'w_in': _jnp.float32, 'conv_dw_w': _jnp.float32, 'conv_dw_b': _jnp.float32, 'conv_ln_g': _jnp.float32, 'conv_ln_b': _jnp.float32, 'w_conv_out': _jnp.float32, 'ssm_conv_w': _jnp.float32, 'ssm_conv_b': _jnp.float32, 'ssm_dt_bias': _jnp.float32, 'ssm_a_log': _jnp.float32, 'ssm_d': _jnp.float32, 'ssm_norm_w': _jnp.float32, 'w_ssm_out': _jnp.float32, 'w_o': _jnp.float32, 'ln1_g': _jnp.float32, 'ln1_b': _jnp.float32, 'w_ffn_up': _jnp.float32, 'ffn_dw_w': _jnp.float32, 'ffn_dw_b': _jnp.float32, 'w_ffn_down': _jnp.float32, 'ln2_g': _jnp.float32, 'ln2_b': _jnp.float32}
MOMENT_SCALE = {'ln_in_g': 6.205893e+00, 'ln_in_b': 1.065118e+00, 'w_in': 1.374761e-02, 'conv_dw_w': 1.536615e-02, 'conv_dw_b': 7.189495e-02, 'conv_ln_g': 3.148351e-02, 'conv_ln_b': 4.417596e-02, 'w_conv_out': 4.111781e-02, 'ssm_conv_w': 1.582673e-02, 'ssm_conv_b': 2.632782e-02, 'ssm_dt_bias': 4.988232e-02, 'ssm_a_log': 7.818357e-02, 'ssm_d': 1.204980e-01, 'ssm_norm_w': 2.091870e-02, 'w_ssm_out': 5.450116e-02, 'w_o': 6.791568e-02, 'ln1_g': 6.317061e+00, 'ln1_b': 1.068734e+00, 'w_ffn_up': 1.530978e-02, 'ffn_dw_w': 7.585560e-03, 'ffn_dw_b': 1.745065e-02, 'w_ffn_down': 2.508492e-02, 'ln2_g': 4.651115e+01, 'ln2_b': 2.334619e+00}


def _to_microbatches(a, axis):
    t = _jnp.moveaxis(a, axis, 0)
    t = t.reshape((N_MICROBATCH, t.shape[0] // N_MICROBATCH) + t.shape[1:])
    return _jnp.moveaxis(t, 1, axis + 1)


def setup_inputs(seed: int = 0) -> dict:
    inp = _fwd_setup_inputs(seed)
    key = _jax.random.fold_in(_jax.random.key(seed), 7919)
    shape, _ = _output_shape()
    out = dict(inp)
    out["loss_target"] = _jax.random.normal(_jax.random.fold_in(key, 0), shape, _jnp.float32)
    for i, name in enumerate(TWIN_WEIGHTS):
        w = inp[name].astype(_jnp.float32)
        if MOMENT_SCALE is None:
            s = _jnp.sqrt(_jnp.mean(_jnp.square(w)) + 1e-30)
        else:
            s = MOMENT_SCALE[name]
        km, kv = _jax.random.split(_jax.random.fold_in(key, i + 1))
        out[name] = w
        out["m_" + name] = s * _jax.random.normal(km, w.shape, _jnp.float32)
        out["v_" + name] = (s * s) * _jax.random.uniform(kv, w.shape, _jnp.float32, 0.5, 1.5)
    if N_MICROBATCH > 1:
        for name, axis in PER_EXAMPLE_BATCH_AXIS.items():
            out[name] = _to_microbatches(out[name], axis)
    return {'x': out['x'], 'ln_in_g': out['ln_in_g'], 'ln_in_b': out['ln_in_b'], 'w_in': out['w_in'], 'conv_dw_w': out['conv_dw_w'], 'conv_dw_b': out['conv_dw_b'], 'conv_ln_g': out['conv_ln_g'], 'conv_ln_b': out['conv_ln_b'], 'w_conv_out': out['w_conv_out'], 'ssm_conv_w': out['ssm_conv_w'], 'ssm_conv_b': out['ssm_conv_b'], 'ssm_dt_bias': out['ssm_dt_bias'], 'ssm_a_log': out['ssm_a_log'], 'ssm_d': out['ssm_d'], 'ssm_norm_w': out['ssm_norm_w'], 'w_ssm_out': out['w_ssm_out'], 'w_o': out['w_o'], 'ln1_g': out['ln1_g'], 'ln1_b': out['ln1_b'], 'w_ffn_up': out['w_ffn_up'], 'ffn_dw_w': out['ffn_dw_w'], 'ffn_dw_b': out['ffn_dw_b'], 'w_ffn_down': out['w_ffn_down'], 'ln2_g': out['ln2_g'], 'ln2_b': out['ln2_b'], 'loss_target': out['loss_target'], 'm_ln_in_g': out['m_ln_in_g'], 'm_ln_in_b': out['m_ln_in_b'], 'm_w_in': out['m_w_in'], 'm_conv_dw_w': out['m_conv_dw_w'], 'm_conv_dw_b': out['m_conv_dw_b'], 'm_conv_ln_g': out['m_conv_ln_g'], 'm_conv_ln_b': out['m_conv_ln_b'], 'm_w_conv_out': out['m_w_conv_out'], 'm_ssm_conv_w': out['m_ssm_conv_w'], 'm_ssm_conv_b': out['m_ssm_conv_b'], 'm_ssm_dt_bias': out['m_ssm_dt_bias'], 'm_ssm_a_log': out['m_ssm_a_log'], 'm_ssm_d': out['m_ssm_d'], 'm_ssm_norm_w': out['m_ssm_norm_w'], 'm_w_ssm_out': out['m_w_ssm_out'], 'm_w_o': out['m_w_o'], 'm_ln1_g': out['m_ln1_g'], 'm_ln1_b': out['m_ln1_b'], 'm_w_ffn_up': out['m_w_ffn_up'], 'm_ffn_dw_w': out['m_ffn_dw_w'], 'm_ffn_dw_b': out['m_ffn_dw_b'], 'm_w_ffn_down': out['m_w_ffn_down'], 'm_ln2_g': out['m_ln2_g'], 'm_ln2_b': out['m_ln2_b'], 'v_ln_in_g': out['v_ln_in_g'], 'v_ln_in_b': out['v_ln_in_b'], 'v_w_in': out['v_w_in'], 'v_conv_dw_w': out['v_conv_dw_w'], 'v_conv_dw_b': out['v_conv_dw_b'], 'v_conv_ln_g': out['v_conv_ln_g'], 'v_conv_ln_b': out['v_conv_ln_b'], 'v_w_conv_out': out['v_w_conv_out'], 'v_ssm_conv_w': out['v_ssm_conv_w'], 'v_ssm_conv_b': out['v_ssm_conv_b'], 'v_ssm_dt_bias': out['v_ssm_dt_bias'], 'v_ssm_a_log': out['v_ssm_a_log'], 'v_ssm_d': out['v_ssm_d'], 'v_ssm_norm_w': out['v_ssm_norm_w'], 'v_w_ssm_out': out['v_w_ssm_out'], 'v_w_o': out['v_w_o'], 'v_ln1_g': out['v_ln1_g'], 'v_ln1_b': out['v_ln1_b'], 'v_w_ffn_up': out['v_w_ffn_up'], 'v_ffn_dw_w': out['v_ffn_dw_w'], 'v_ffn_dw_b': out['v_ffn_dw_b'], 'v_w_ffn_down': out['v_w_ffn_down'], 'v_ln2_g': out['v_ln2_g'], 'v_ln2_b': out['v_ln2_b']}


def _loss(weights, diff, rest, loss_target):
    with _jax.named_scope("forward"):
        args = {**rest, TWIN_DIFF_INPUT: diff, **{k: w.astype(_WEIGHT_DTYPES[k]) for k, w in weights.items()}}
        y = _forward(args)
    with _jax.named_scope("loss_head"):
        err = _jnp.square(y.astype(_jnp.float32) - loss_target)
        return 0.5 * _jnp.sum(_jnp.mean(err, axis=-1)) if err.ndim else 0.5 * err


def _adamw(w, g, m, v):
    m = ADAM_B1 * m + (1.0 - ADAM_B1) * g
    v = ADAM_B2 * v + (1.0 - ADAM_B2) * _jnp.square(g)
    m_hat = m / (1.0 - ADAM_B1 ** ADAM_STEP)
    v_hat = v / (1.0 - ADAM_B2 ** ADAM_STEP)
    delta = -ADAM_LR * (m_hat / (_jnp.sqrt(v_hat) + ADAM_EPS) + ADAM_WD * w)
    return delta, m, v


def reference(x, ln_in_g, ln_in_b, w_in, conv_dw_w, conv_dw_b, conv_ln_g, conv_ln_b, w_conv_out, ssm_conv_w, ssm_conv_b, ssm_dt_bias, ssm_a_log, ssm_d, ssm_norm_w, w_ssm_out, w_o, ln1_g, ln1_b, w_ffn_up, ffn_dw_w, ffn_dw_b, w_ffn_down, ln2_g, ln2_b, loss_target, m_ln_in_g, m_ln_in_b, m_w_in, m_conv_dw_w, m_conv_dw_b, m_conv_ln_g, m_conv_ln_b, m_w_conv_out, m_ssm_conv_w, m_ssm_conv_b, m_ssm_dt_bias, m_ssm_a_log, m_ssm_d, m_ssm_norm_w, m_w_ssm_out, m_w_o, m_ln1_g, m_ln1_b, m_w_ffn_up, m_ffn_dw_w, m_ffn_dw_b, m_w_ffn_down, m_ln2_g, m_ln2_b, v_ln_in_g, v_ln_in_b, v_w_in, v_conv_dw_w, v_conv_dw_b, v_conv_ln_g, v_conv_ln_b, v_w_conv_out, v_ssm_conv_w, v_ssm_conv_b, v_ssm_dt_bias, v_ssm_a_log, v_ssm_d, v_ssm_norm_w, v_w_ssm_out, v_w_o, v_ln1_g, v_ln1_b, v_w_ffn_up, v_ffn_dw_w, v_ffn_dw_b, v_w_ffn_down, v_ln2_g, v_ln2_b):
    given = dict(x=x, ln_in_g=ln_in_g, ln_in_b=ln_in_b, w_in=w_in, conv_dw_w=conv_dw_w, conv_dw_b=conv_dw_b, conv_ln_g=conv_ln_g, conv_ln_b=conv_ln_b, w_conv_out=w_conv_out, ssm_conv_w=ssm_conv_w, ssm_conv_b=ssm_conv_b, ssm_dt_bias=ssm_dt_bias, ssm_a_log=ssm_a_log, ssm_d=ssm_d, ssm_norm_w=ssm_norm_w, w_ssm_out=w_ssm_out, w_o=w_o, ln1_g=ln1_g, ln1_b=ln1_b, w_ffn_up=w_ffn_up, ffn_dw_w=ffn_dw_w, ffn_dw_b=ffn_dw_b, w_ffn_down=w_ffn_down, ln2_g=ln2_g, ln2_b=ln2_b, loss_target=loss_target, m_ln_in_g=m_ln_in_g, m_ln_in_b=m_ln_in_b, m_w_in=m_w_in, m_conv_dw_w=m_conv_dw_w, m_conv_dw_b=m_conv_dw_b, m_conv_ln_g=m_conv_ln_g, m_conv_ln_b=m_conv_ln_b, m_w_conv_out=m_w_conv_out, m_ssm_conv_w=m_ssm_conv_w, m_ssm_conv_b=m_ssm_conv_b, m_ssm_dt_bias=m_ssm_dt_bias, m_ssm_a_log=m_ssm_a_log, m_ssm_d=m_ssm_d, m_ssm_norm_w=m_ssm_norm_w, m_w_ssm_out=m_w_ssm_out, m_w_o=m_w_o, m_ln1_g=m_ln1_g, m_ln1_b=m_ln1_b, m_w_ffn_up=m_w_ffn_up, m_ffn_dw_w=m_ffn_dw_w, m_ffn_dw_b=m_ffn_dw_b, m_w_ffn_down=m_w_ffn_down, m_ln2_g=m_ln2_g, m_ln2_b=m_ln2_b, v_ln_in_g=v_ln_in_g, v_ln_in_b=v_ln_in_b, v_w_in=v_w_in, v_conv_dw_w=v_conv_dw_w, v_conv_dw_b=v_conv_dw_b, v_conv_ln_g=v_conv_ln_g, v_conv_ln_b=v_conv_ln_b, v_w_conv_out=v_w_conv_out, v_ssm_conv_w=v_ssm_conv_w, v_ssm_conv_b=v_ssm_conv_b, v_ssm_dt_bias=v_ssm_dt_bias, v_ssm_a_log=v_ssm_a_log, v_ssm_d=v_ssm_d, v_ssm_norm_w=v_ssm_norm_w, v_w_ssm_out=v_w_ssm_out, v_w_o=v_w_o, v_ln1_g=v_ln1_g, v_ln1_b=v_ln1_b, v_w_ffn_up=v_w_ffn_up, v_ffn_dw_w=v_ffn_dw_w, v_ffn_dw_b=v_ffn_dw_b, v_w_ffn_down=v_w_ffn_down, v_ln2_g=v_ln2_g, v_ln2_b=v_ln2_b)
    weights = {n: given[n] for n in TWIN_WEIGHTS}
    shared = {n: given[n] for n in SHARED_INPUTS}
    per_example = {n: given[n] for n in ['x']}
    grad_fn = _jax.value_and_grad(_loss, argnums=(0, 1))

    def one_microbatch(ex, loss_target):
        ex = dict(ex)
        diff = ex.pop(TWIN_DIFF_INPUT)
        return grad_fn(weights, diff, {**shared, **ex}, loss_target)

    if N_MICROBATCH == 1:
        loss, (grad_w, grad_x) = one_microbatch(per_example, given["loss_target"])
    else:
        def body(carry, xs):
            loss_sum, grad_sum = carry
            l_k, (gw_k, gx_k) = one_microbatch(xs[0], xs[1])
            with _jax.named_scope("update"):
                return (loss_sum + l_k, _jax.tree.map(_jnp.add, grad_sum, gw_k)), gx_k

        init = (_jnp.zeros((), _jnp.float32), _jax.tree.map(_jnp.zeros_like, weights))
        (loss, grad_w), grad_x = _jax.lax.scan(body, init, (per_example, given["loss_target"]))
    with _jax.named_scope("update"):
        delta_w, new_m, new_v = {}, {}, {}
        for n in TWIN_WEIGHTS:
            delta_w[n], new_m[n], new_v[n] = _adamw(weights[n], grad_w[n], given["m_" + n], given["v_" + n])
    return (loss, grad_x, *[grad_w[n] for n in TWIN_WEIGHTS], *[delta_w[n] for n in TWIN_WEIGHTS],
            *[new_m[n] for n in TWIN_WEIGHTS], *[new_v[n] for n in TWIN_WEIGHTS])
```

```python
import functools

import jax
import jax.numpy as jnp
from jax import lax
from jax.experimental import pallas as pl
from jax.experimental.pallas import tpu as pltpu

F32 = jnp.float32
BF16 = jnp.bfloat16
WIRE = jnp.bfloat16
HIGHEST = lax.Precision.HIGHEST

DEPTH = 2
SSM_STATE = 128
SSM_CHUNK = 128
SSM_GROUPS = 4
SSM_HEAD_DIM = 64
DN_ALPHA = (2 * DEPTH) ** 0.25
LN_EPS = 1e-5
RMS_EPS = 1e-5
ADAM_LR = 0.001
ADAM_B1 = 0.9
ADAM_B2 = 0.999
ADAM_EPS = 1e-08
ADAM_WD = 0.01
ADAM_STEP = 10

N_DEV = 8
LANES = 128
VMEM_LIMIT = 48 * 1024 * 1024
GLU_TILE = 512
FFN_TILE = 256
CONV_ROWS = 256

PARAMS = ['ln_in_g', 'ln_in_b', 'w_in', 'conv_dw_w', 'conv_dw_b', 'conv_ln_g', 'conv_ln_b', 'w_conv_out',
          'ssm_conv_w', 'ssm_conv_b', 'ssm_dt_bias', 'ssm_a_log', 'ssm_d', 'ssm_norm_w', 'w_ssm_out', 'w_o',
          'ln1_g', 'ln1_b', 'w_ffn_up', 'ffn_dw_w', 'ffn_dw_b', 'w_ffn_down', 'ln2_g', 'ln2_b']
SHARDED = {'w_in': (2, True), 'conv_dw_w': (2, False), 'w_conv_out': (1, True), 'ssm_conv_w': (2, False),
           'w_ssm_out': (1, True), 'w_o': (1, True), 'w_ffn_up': (2, True), 'ffn_dw_w': (2, False),
           'w_ffn_down': (1, True)}


def _sds(shape, dtype):
    return jax.ShapeDtypeStruct(tuple(shape), dtype)


def _tile(dim, pref):
    return pref if dim % pref == 0 else dim


def _pick(dim, pref, mult):
    best = None
    for t in range(mult, min(dim, pref) + 1, mult):
        if dim % t == 0:
            best = t
    return best or dim


def _cp(*sem):
    return pltpu.CompilerParams(dimension_semantics=sem, vmem_limit_bytes=VMEM_LIMIT)


def _sigmoid(x):
    return 1.0 / (1.0 + jnp.exp(-x))


def _silu_and_grad(x):
    s = _sigmoid(x)
    return x * s, s * (1.0 + x * (1.0 - s))


def _ln_stats(s):
    mu = jnp.mean(s, axis=-1, keepdims=True)
    xc = s - mu
    var = jnp.mean(xc * xc, axis=-1, keepdims=True)
    rstd = lax.rsqrt(var + LN_EPS)
    return xc * rstd, rstd


def _dot(a, b, dims):
    return lax.dot_general(a, b, (dims, ((), ())), preferred_element_type=F32)


_NN = ((1,), (0,))
_NT = ((1,), (1,))
_TN = ((0,), (0,))


def _mm(a, b, *, mode, out_dtype, name, res=None, res_scale=1.0, tm=512, tn=1408, tk=1408):
    if mode == 'nn':
        (M, K), (_, N) = a.shape, b.shape
    elif mode == 'nt':
        (M, K), (N, _) = a.shape, b.shape
    else:
        (K, M), (_, N) = a.shape, b.shape
        tm, tk = 1408, 512
    tm, tn, tk = _pick(M, tm, 8), _pick(N, tn, LANES), _pick(K, tk, LANES)
    nk = K // tk
    dims = {'nn': _NN, 'nt': _NT, 'tn': _TN}[mode]

    def body(*refs):
        if res is None:
            a_ref, b_ref, o_ref = refs[:3]
            r_ref = None
        else:
            a_ref, b_ref, r_ref, o_ref = refs[:4]
        p = _dot(a_ref[...].astype(BF16), b_ref[...].astype(BF16), dims)

        def finish(acc):
            if r_ref is not None:
                acc = acc + res_scale * r_ref[...]
            o_ref[...] = acc.astype(out_dtype)

        if nk == 1:
            finish(p)
        else:
            acc_ref = refs[-1]
            k = pl.program_id(2)

            @pl.when(k == 0)
            def _():
                acc_ref[...] = p

            @pl.when(k > 0)
            def _():
                acc_ref[...] += p

            @pl.when(k == nk - 1)
            def _():
                finish(acc_ref[...])

    if mode == 'nn':
        a_spec = pl.BlockSpec((tm, tk), lambda i, j, k: (i, k))
        b_spec = pl.BlockSpec((tk, tn), lambda i, j, k: (k, j))
    elif mode == 'nt':
        a_spec = pl.BlockSpec((tm, tk), lambda i, j, k: (i, k))
        b_spec = pl.BlockSpec((tn, tk), lambda i, j, k: (j, k))
    else:
        a_spec = pl.BlockSpec((tk, tm), lambda i, j, k: (k, i))
        b_spec = pl.BlockSpec((tk, tn), lambda i, j, k: (k, j))
    o_spec = pl.BlockSpec((tm, tn), lambda i, j, k: (i, j))
    in_specs = [a_spec, b_spec] + ([o_spec] if res is not None else [])
    args = (a, b) + ((res,) if res is not None else ())
    return pl.pallas_call(
        body, name=name, out_shape=_sds((M, N), out_dtype), grid=(M // tm, N // tn, nk),
        in_specs=in_specs, out_specs=o_spec,
        scratch_shapes=[pltpu.VMEM((tm, tn), F32)] if nk > 1 else [],
        compiler_params=_cp("parallel", "parallel", "arbitrary"))(*args)


def _mm_res_ln(a, w, res, g, b, *, name, tm=256, tk=1408):
    (M, K), (_, N) = a.shape, w.shape
    tm, tk = _pick(M, tm, 8), _pick(K, tk, LANES)
    nk = K // tk

    def body(a_ref, w_ref, r_ref, g_ref, b_ref, h_ref, s_ref, acc_ref):
        k = pl.program_id(1)
        p = _dot(a_ref[...].astype(BF16), w_ref[...].astype(BF16), _NN)

        @pl.when(k == 0)
        def _():
            acc_ref[...] = p

        @pl.when(k > 0)
        def _():
            acc_ref[...] += p

        @pl.when(k == nk - 1)
        def _():
            s = DN_ALPHA * r_ref[...] + acc_ref[...]
            xhat, _ = _ln_stats(s)
            s_ref[...] = s
            h_ref[...] = xhat * g_ref[...] + b_ref[...]

    row = pl.BlockSpec((tm, N), lambda i, k: (i, 0))
    vec = pl.BlockSpec((1, N), lambda i, k: (0, 0))
    return pl.pallas_call(
        body, name=name, out_shape=(_sds((M, N), F32), _sds((M, N), F32)), grid=(M // tm, nk),
        in_specs=[pl.BlockSpec((tm, tk), lambda i, k: (i, k)), pl.BlockSpec((tk, N), lambda i, k: (k, 0)), row, vec, vec],
        out_specs=(row, row), scratch_shapes=[pltpu.VMEM((tm, N), F32)],
        compiler_params=_cp("parallel", "arbitrary"))(a, w, res, g, b)


def _ln_fwd(x, g, b, *, name):
    T, D = x.shape
    tt = _tile(T, 512)

    def body(x_ref, g_ref, b_ref, o_ref):
        xhat, _ = _ln_stats(x_ref[...])
        o_ref[...] = xhat * g_ref[...] + b_ref[...]

    row = pl.BlockSpec((tt, D), lambda i: (i, 0))
    vec = pl.BlockSpec((1, D), lambda i: (0, 0))
    return pl.pallas_call(body, name=name, out_shape=_sds((T, D), F32), grid=(T // tt,),
                          in_specs=[row, vec, vec], out_specs=row, compiler_params=_cp("parallel"))(x, g, b)


def _ln_bwd(s, dy, g, b, *, silu, name):
    T, D = s.shape
    tt = _tile(T, 512)

    def body(s_ref, dy_ref, g_ref, b_ref, ds_ref, dg_ref, db_ref):
        i = pl.program_id(0)
        xhat, rstd = _ln_stats(s_ref[...])
        gg = g_ref[...]
        dyl = dy_ref[...]
        if silu:
            _, dsilu = _silu_and_grad(xhat * gg + b_ref[...])
            dyl = dyl * dsilu
        dxh = dyl * gg
        m1 = jnp.mean(dxh, axis=-1, keepdims=True)
        m2 = jnp.mean(dxh * xhat, axis=-1, keepdims=True)
        ds_ref[...] = rstd * (dxh - m1 - xhat * m2)

        @pl.when(i == 0)
        def _():
            dg_ref[...] = jnp.zeros_like(dg_ref)
            db_ref[...] = jnp.zeros_like(db_ref)

        dg_ref[...] += jnp.sum(dyl * xhat, axis=0, keepdims=True)
        db_ref[...] += jnp.sum(dyl, axis=0, keepdims=True)

    row = pl.BlockSpec((tt, D), lambda i: (i, 0))
    vec = pl.BlockSpec((1, D), lambda i: (0, 0))
    return pl.pallas_call(body, name=name, out_shape=(_sds((T, D), F32), _sds((1, D), F32), _sds((1, D), F32)),
                          grid=(T // tt,), in_specs=[row, row, vec, vec], out_specs=(row, vec, vec),
                          compiler_params=_cp("arbitrary"))(s, dy, g, b)


def _loss_fwd_bwd(h, tgt, *, name):
    T, D = h.shape
    tt = _tile(T, 512)

    def body(h_ref, t_ref, dh_ref, l_ref):
        i = pl.program_id(0)
        e = h_ref[...] - t_ref[...]
        dh_ref[...] = e * (1.0 / D)

        @pl.when(i == 0)
        def _():
            l_ref[...] = jnp.zeros_like(l_ref)

        part = jnp.sum(jnp.sum(e * e, axis=1, keepdims=True), axis=0, keepdims=True) * (0.5 / D)
        l_ref[...] += jnp.broadcast_to(part, l_ref.shape)

    row = pl.BlockSpec((tt, D), lambda i: (i, 0))
    one = pl.BlockSpec((8, LANES), lambda i: (0, 0))
    dh, l = pl.pallas_call(body, name=name, out_shape=(_sds((T, D), F32), _sds((8, LANES), F32)), grid=(T // tt,),
                           in_specs=[row, row], out_specs=(row, one), compiler_params=_cp("arbitrary"))(h, tgt)
    return l[0:1, 0:1], dh


def _halo_rows(k):
    return 32 if k > 9 else 8


def _rows_from(ext, start, n):
    if start % 8 == 0:
        return ext[start:start + n]
    return pltpu.roll(ext, ext.shape[0] - start, 0)[:n]


def _conv_taps(ext, w_ref, K, hb, n):
    acc = None
    for k in range(K):
        term = w_ref[k:k + 1, :] * _rows_from(ext, hb - (K - 1) + k, n)
        acc = term if acc is None else acc + term
    return acc


def _prev_spec(tt, hb, width, off):
    return pl.BlockSpec((hb, width), lambda t, c: (jnp.maximum(t * (tt // hb) - 1, 0), c + off))


def _ext_prev(i, main, halo):
    return jnp.concatenate([jnp.where(i > 0, halo, 0.0), main], axis=0)


def _glu(blk, tc):
    return blk[:, :tc] * _sigmoid(blk[:, tc:])


def _conf_fwd(u, w, b, g, beta, *, D, name):
    T = u.shape[0]
    K = w.shape[0]
    tt, hb, tc = _tile(T, CONV_ROWS), _halo_rows(K), _tile(D, GLU_TILE)
    ntile = D // tc

    def deint(blk):
        a = jnp.concatenate([blk[:, 2 * i * tc:(2 * i + 1) * tc] for i in range(ntile)], axis=1)
        gg = jnp.concatenate([blk[:, (2 * i + 1) * tc:(2 * i + 2) * tc] for i in range(ntile)], axis=1)
        return a * _sigmoid(gg)

    def body(u_ref, uh_ref, w_ref, b_ref, g_ref, beta_ref, v1_ref, v3_ref):
        i = pl.program_id(0)
        ext = _ext_prev(i, deint(u_ref[...]), deint(uh_ref[...]))
        v1 = _conv_taps(ext, w_ref, K, hb, tt) + b_ref[...]
        v1_ref[...] = v1
        xhat, _ = _ln_stats(v1)
        v2 = xhat * g_ref[...] + beta_ref[...]
        v3_ref[...] = (v2 * _sigmoid(v2)).astype(BF16)

    row = pl.BlockSpec((tt, D), lambda t: (t, 0))
    vec = pl.BlockSpec((1, D), lambda t: (0, 0))
    return pl.pallas_call(
        body, name=name, out_shape=(_sds((T, D), F32), _sds((T, D), BF16)), grid=(T // tt,),
        in_specs=[pl.BlockSpec((tt, 2 * D), lambda t: (t, 0)),
                  pl.BlockSpec((hb, 2 * D), lambda t: (jnp.maximum(t * (tt // hb) - 1, 0), 0)),
                  pl.BlockSpec((K, D), lambda t: (0, 0)), vec, vec, vec],
        out_specs=(row, row), compiler_params=_cp("parallel"))(u, u, w, b, g, beta)


def _ssm_conv_fwd(u, off, w, b, *, name):
    T = u.shape[0]
    K, C = w.shape
    tt, hb, tc = _tile(T, CONV_ROWS), _halo_rows(K), _tile(C, 512)
    ob = off // tc

    def body(x_ref, xh_ref, w_ref, b_ref, o_ref):
        ext = _ext_prev(pl.program_id(0), x_ref[...], xh_ref[...])
        pre = _conv_taps(ext, w_ref, K, hb, tt) + b_ref[...]
        o_ref[...] = pre * _sigmoid(pre)

    return pl.pallas_call(
        body, name=name, out_shape=_sds((T, C), F32), grid=(T // tt, C // tc),
        in_specs=[pl.BlockSpec((tt, tc), lambda t, c: (t, c + ob)), _prev_spec(tt, hb, tc, ob),
                  pl.BlockSpec((K, tc), lambda t, c: (0, c)), pl.BlockSpec((1, tc), lambda t, c: (0, c))],
        out_specs=pl.BlockSpec((tt, tc), lambda t, c: (t, c)), compiler_params=_cp("parallel", "parallel"))(u, u, w, b)


def _ssm_act_bwd(u, off, w, b, dout, *, name):
    T = u.shape[0]
    K, C = w.shape
    tt, hb, tc = _tile(T, CONV_ROWS), _halo_rows(K), _tile(C, 512)
    ob = off // tc

    def body(x_ref, xh_ref, w_ref, b_ref, d_ref, o_ref):
        ext = _ext_prev(pl.program_id(0), x_ref[...], xh_ref[...])
        pre = _conv_taps(ext, w_ref, K, hb, tt) + b_ref[...]
        _, dsilu = _silu_and_grad(pre)
        o_ref[...] = d_ref[...] * dsilu

    blk = pl.BlockSpec((tt, tc), lambda t, c: (t, c))
    return pl.pallas_call(
        body, name=name, out_shape=_sds((T, C), F32), grid=(T // tt, C // tc),
        in_specs=[pl.BlockSpec((tt, tc), lambda t, c: (t, c + ob)), _prev_spec(tt, hb, tc, ob),
                  pl.BlockSpec((K, tc), lambda t, c: (0, c)), pl.BlockSpec((1, tc), lambda t, c: (0, c)), blk],
        out_specs=blk, compiler_params=_cp("parallel", "parallel"))(u, u, w, b, dout)


def _ffn_conv_fwd(uf, w, b, *, name):
    T, C2 = uf.shape
    K = w.shape[0]
    FF = C2 // 2
    tt, hb, tc = _tile(T, CONV_ROWS), _halo_rows(K), _tile(FF, FFN_TILE)

    def body(x_ref, xh_ref, w_ref, b_ref, o_ref):
        ext = _ext_prev(pl.program_id(0), x_ref[...], xh_ref[...])
        pre = _conv_taps(ext, w_ref, K, hb, tt) + b_ref[...]
        gate, val = pre[:, :tc], pre[:, tc:]
        o_ref[...] = (gate * _sigmoid(gate) * val).astype(BF16)

    return pl.pallas_call(
        body, name=name, out_shape=_sds((T, FF), BF16), grid=(T // tt, FF // tc),
        in_specs=[pl.BlockSpec((tt, 2 * tc), lambda t, c: (t, c)), _prev_spec(tt, hb, 2 * tc, 0),
                  pl.BlockSpec((K, 2 * tc), lambda t, c: (0, c)), pl.BlockSpec((1, 2 * tc), lambda t, c: (0, c))],
        out_specs=pl.BlockSpec((tt, tc), lambda t, c: (t, c)), compiler_params=_cp("parallel", "parallel"))(uf, uf, w, b)


def _ffn_act_bwd(uf, w, b, dact, *, name):
    T, C2 = uf.shape
    K = w.shape[0]
    FF = C2 // 2
    tt, hb, tc = _tile(T, CONV_ROWS), _halo_rows(K), _tile(FF, FFN_TILE)

    def body(x_ref, xh_ref, w_ref, b_ref, d_ref, o_ref):
        ext = _ext_prev(pl.program_id(0), x_ref[...], xh_ref[...])
        pre = _conv_taps(ext, w_ref, K, hb, tt) + b_ref[...]
        gate, val = pre[:, :tc], pre[:, tc:]
        sg, dsg = _silu_and_grad(gate)
        d = d_ref[...]
        o_ref[...] = jnp.concatenate([d * val * dsg, d * sg], axis=1)

    wide = pl.BlockSpec((tt, 2 * tc), lambda t, c: (t, c))
    return pl.pallas_call(
        body, name=name, out_shape=_sds((T, C2), F32), grid=(T // tt, FF // tc),
        in_specs=[wide, _prev_spec(tt, hb, 2 * tc, 0), pl.BlockSpec((K, 2 * tc), lambda t, c: (0, c)),
                  pl.BlockSpec((1, 2 * tc), lambda t, c: (0, c)), pl.BlockSpec((tt, tc), lambda t, c: (t, c))],
        out_specs=wide, compiler_params=_cp("parallel", "parallel"))(uf, uf, w, b, dact)


def _dwconv_bwd(dpre, x, off, w, *, glu, name):
    T, C = dpre.shape
    K = w.shape[0]
    tt, hb = _tile(T, CONV_ROWS), _halo_rows(K)
    tc = _tile(C, GLU_TILE if glu else 512)
    xw = 2 * tc if glu else tc
    ob = 0 if glu else off // tc
    nt = T // tt

    def body(d_ref, dn_ref, x_ref, xh_ref, w_ref, dx_ref, dw_ref, db_ref):
        t = pl.program_id(1)
        d = d_ref[...]
        extn = jnp.concatenate([d, jnp.where(t < nt - 1, dn_ref[...], 0.0)], axis=0)
        dxin = None
        for k in range(K):
            term = w_ref[k:k + 1, :] * _rows_from(extn, (K - 1) - k, tt)
            dxin = term if dxin is None else dxin + term
        xb, xh = x_ref[...], xh_ref[...]
        if glu:
            a, gg = xb[:, :tc], xb[:, tc:]
            sg = _sigmoid(gg)
            dx_ref[...] = jnp.concatenate([dxin * sg, dxin * a * sg * (1.0 - sg)], axis=1).astype(BF16)
            ext = _ext_prev(t, a * sg, _glu(xh, tc))
        else:
            dx_ref[...] = dxin.astype(BF16)
            ext = _ext_prev(t, xb, xh)

        @pl.when(t == 0)
        def _():
            dw_ref[...] = jnp.zeros_like(dw_ref)
            db_ref[...] = jnp.zeros_like(db_ref)

        for k in range(K):
            dw_ref[k:k + 1, :] += jnp.sum(d * _rows_from(ext, hb - (K - 1) + k, tt), axis=0, keepdims=True)
        db_ref[...] += jnp.sum(d, axis=0, keepdims=True)

    return pl.pallas_call(
        body, name=name,
        out_shape=(_sds((T, 2 * C if glu else C), BF16), _sds((K, C), F32), _sds((1, C), F32)),
        grid=(C // tc, nt),
        in_specs=[pl.BlockSpec((tt, tc), lambda c, t: (t, c)),
                  pl.BlockSpec((hb, tc), lambda c, t: (jnp.minimum((t + 1) * (tt // hb), T // hb - 1), c)),
                  pl.BlockSpec((tt, xw), lambda c, t: (t, c + ob)),
                  pl.BlockSpec((hb, xw), lambda c, t: (jnp.maximum(t * (tt // hb) - 1, 0), c + ob)),
                  pl.BlockSpec((K, tc), lambda c, t: (0, c))],
        out_specs=(pl.BlockSpec((tt, xw), lambda c, t: (t, c)), pl.BlockSpec((K, tc), lambda c, t: (0, c)),
                   pl.BlockSpec((1, tc), lambda c, t: (0, c))),
        compiler_params=_cp("parallel", "arbitrary"))(dpre, dpre, x, x, w)


def _softplus(x):
    t = jnp.exp(-jnp.abs(x))
    u = 1.0 + t
    log1p = jnp.where(u == 1.0, t, jnp.log(u) * t / jnp.where(u == 1.0, 1.0, u - 1.0))
    return jnp.maximum(x, 0.0) + log1p


def _dt_fwd(udt, bias, *, H, name):
    T = udt.shape[0]
    tt = _tile(T, 1024)

    def body(u_ref, b_ref, o_ref):
        lane = lax.broadcasted_iota(jnp.int32, (tt, LANES), 1)
        o_ref[...] = jnp.where(lane < H, _softplus(u_ref[...] + b_ref[...]), 0.0)

    row = pl.BlockSpec((tt, LANES), lambda i: (i, 0))
    return pl.pallas_call(body, name=name, out_shape=_sds((T, LANES), F32), grid=(T // tt,),
                          in_specs=[row, pl.BlockSpec((1, LANES), lambda i: (0, 0))], out_specs=row,
                          compiler_params=_cp("parallel"))(udt, bias)


def _dt_bwd(da, ddtx, dt, udt, bias, a_col, *, H, name):
    T = da.shape[0]
    tt = _tile(T, 1024)

    def body(da_ref, dx_ref, dt_ref, u_ref, b_ref, a_ref, draw_ref, dbias_ref, dalog_ref):
        i = pl.program_id(0)
        lane = lax.broadcasted_iota(jnp.int32, (tt, LANES), 1)
        dav = da_ref[...]
        ddt = dav * a_ref[...] + dx_ref[...]
        draw = jnp.where(lane < H, ddt * _sigmoid(u_ref[...] + b_ref[...]), 0.0)
        draw_ref[...] = draw.astype(BF16)

        @pl.when(i == 0)
        def _():
            dbias_ref[...] = jnp.zeros_like(dbias_ref)
            dalog_ref[...] = jnp.zeros_like(dalog_ref)

        dbias_ref[...] += jnp.sum(draw, axis=0, keepdims=True)
        dalog_ref[...] += jnp.sum(dav * dt_ref[...], axis=0, keepdims=True) * a_ref[...]

    row = pl.BlockSpec((tt, LANES), lambda i: (i, 0))
    vec = pl.BlockSpec((1, LANES), lambda i: (0, 0))
    return pl.pallas_call(body, name=name,
                          out_shape=(_sds((T, LANES), BF16), _sds((1, LANES), F32), _sds((1, LANES), F32)),
                          grid=(T // tt,), in_specs=[row, row, row, row, vec, vec], out_specs=(row, vec, vec),
                          compiler_params=_cp("arbitrary"))(da, ddtx, dt, udt, bias, a_col)


def _ssd_consts():
    L = SSM_CHUNK
    r = lax.broadcasted_iota(jnp.int32, (L, L), 0)
    c = lax.broadcasted_iota(jnp.int32, (L, L), 1)
    return r, c


def _ssd_chunk_decays(dtc_ref, dtr_ref, acol_ref, arow_ref, cs_ref, csr_ref, r, c):
    L = SSM_CHUNK
    dtc = dtc_ref[...]
    tril = (r >= c).astype(F32)
    triu = (r <= c).astype(F32)
    cs_ref[...] = jnp.dot(tril, dtc * acol_ref[...], precision=HIGHEST, preferred_element_type=F32)
    csr_ref[...] = jnp.dot(dtr_ref[...] * arow_ref[...], triu, precision=HIGHEST, preferred_element_type=F32)
    cs = cs_ref[...]
    cs_last = cs_ref[L - 1:L, :]
    return dtc, cs, jnp.exp(cs), jnp.exp(cs_last - cs), jnp.exp(cs_last), triu


def _ssd_fwd(xbc, dt, dtT, a_col, a_row, d_col, *, H, name):
    T, XBC = xbc.shape
    L, N, G, P = SSM_CHUNK, SSM_STATE, SSM_GROUPS, SSM_HEAD_DIM
    SX = H * P
    HR = dtT.shape[0]
    nc = T // L
    heads_per_group = H // G

    def body(x_ref, dtc_ref, dtr_ref, acol_ref, arow_ref, d_ref, y_ref, hst_ref, state, cs_ref, csr_ref):
        ci = pl.program_id(0)

        @pl.when(ci == 0)
        def _():
            state[...] = jnp.zeros_like(state)

        hst_ref[0] = state[...]
        r, c = _ssd_consts()
        tri = r >= c
        lane_lo = c < P
        row_lo = r < P
        lane1_lo = lax.broadcasted_iota(jnp.int32, (1, LANES), 1) < P
        dtc, cs, e, ds, cd, _ = _ssd_chunk_decays(dtc_ref, dtr_ref, acol_ref, arow_ref, cs_ref, csr_ref, r, c)
        dsk = d_ref[...]

        def sel(arr, h0):
            return jnp.where(lane_lo, arr[:, h0:h0 + 1], arr[:, h0 + 1:h0 + 2])

        for g in range(G):
            Bg = x_ref[:, SX + g * N:SX + (g + 1) * N].astype(BF16)
            Cg = x_ref[:, SX + G * N + g * N:SX + G * N + (g + 1) * N].astype(BF16)
            Gm = _dot(Cg, Bg, _NT)
            for j in range(g * heads_per_group // 2, (g + 1) * heads_per_group // 2):
                h0 = 2 * j
                sl = slice(2 * P * j, 2 * P * (j + 1))
                x2 = x_ref[:, sl]
                X2 = x2 * sel(dtc, h0)
                X2b = X2.astype(BF16)
                ys = []
                for h in (h0, h0 + 1):
                    seg = cs[:, h:h + 1] - csr_ref[h:h + 1, :]
                    Dm = jnp.where(tri, jnp.exp(jnp.where(tri, seg, 0.0)), 0.0)
                    ys.append(_dot((Gm * Dm).astype(BF16), X2b, _NN))
                H2 = state[sl, :]
                R2 = _dot(Cg, H2.astype(BF16), _NT)
                dsk2 = jnp.where(lane1_lo, dsk[:, h0:h0 + 1], dsk[:, h0 + 1:h0 + 2])
                y_ref[:, sl] = jnp.where(lane_lo, ys[0], ys[1]) + sel(e, h0) * R2 + x2 * dsk2
                S2 = _dot((X2 * sel(ds, h0)).astype(BF16), Bg, _TN)
                state[sl, :] = jnp.where(row_lo, cd[:, h0:h0 + 1], cd[:, h0 + 1:h0 + 2]) * H2 + S2

    vec = pl.BlockSpec((1, LANES), lambda i: (0, 0))
    return pl.pallas_call(
        body, name=name, out_shape=(_sds((T, SX), F32), _sds((nc, SX, N), F32)), grid=(nc,),
        in_specs=[pl.BlockSpec((L, XBC), lambda i: (i, 0)), pl.BlockSpec((L, LANES), lambda i: (i, 0)),
                  pl.BlockSpec((HR, L), lambda i: (0, i)), vec, pl.BlockSpec((HR, 1), lambda i: (0, 0)), vec],
        out_specs=(pl.BlockSpec((L, SX), lambda i: (i, 0)), pl.BlockSpec((1, SX, N), lambda i: (i, 0, 0))),
        scratch_shapes=[pltpu.VMEM((SX, N), F32), pltpu.VMEM((L, LANES), F32), pltpu.VMEM((HR, L), F32)],
        compiler_params=_cp("arbitrary"))(xbc, dt, dtT, a_col, a_row, d_col)


def _ssd_bwd(xbc, dt, dtT, a_col, a_row, d_col, hst, dy, *, H, name):
    T, XBC = xbc.shape
    L, N, G, P = SSM_CHUNK, SSM_STATE, SSM_GROUPS, SSM_HEAD_DIM
    SX = H * P
    HR = dtT.shape[0]
    nc = T // L
    heads_per_group = H // G

    def body(x_ref, dtc_ref, dtr_ref, acol_ref, arow_ref, d_ref, hst_ref, dy_ref,
             dx_ref, da_ref, ddtx_ref, dd_ref, dstate, cs_ref, csr_ref):
        ci = pl.program_id(0)

        @pl.when(ci == 0)
        def _():
            dstate[...] = jnp.zeros_like(dstate)
            dd_ref[...] = jnp.zeros_like(dd_ref)

        r, c = _ssd_consts()
        tri = r >= c
        lane_lo = c < P
        row_lo = r < P
        lane1 = lax.broadcasted_iota(jnp.int32, (1, LANES), 1)
        rowc = lax.broadcasted_iota(jnp.int32, (L, 1), 0)
        dtc, cs, e, ds, cd, triu = _ssd_chunk_decays(dtc_ref, dtr_ref, acol_ref, arow_ref, cs_ref, csr_ref, r, c)
        triu_b = triu.astype(BF16)
        dsk = d_ref[...]
        last_row = rowc == L - 1

        def sel(arr, h0):
            return jnp.where(lane_lo, arr[:, h0:h0 + 1], arr[:, h0 + 1:h0 + 2])

        def half(mask_lo, lo):
            return mask_lo if lo else jnp.logical_not(mask_lo)

        def lsum(v, lo):
            return jnp.sum(jnp.where(half(lane_lo, lo), v, 0.0), axis=1, keepdims=True)

        dcs_acc = jnp.zeros((L, LANES), F32)
        dadir_acc = jnp.zeros((L, LANES), F32)
        ddtx_acc = jnp.zeros((L, LANES), F32)
        dd_acc = jnp.zeros((1, LANES), F32)
        for g in range(G):
            bsl = slice(SX + g * N, SX + (g + 1) * N)
            csl = slice(SX + G * N + g * N, SX + G * N + (g + 1) * N)
            Bg = x_ref[:, bsl].astype(BF16)
            Cg = x_ref[:, csl].astype(BF16)
            Gm = _dot(Cg, Bg, _NT)
            dG = jnp.zeros((L, L), F32)
            dBg = jnp.zeros((L, N), F32)
            dCg = jnp.zeros((L, N), F32)
            for j in range(g * heads_per_group // 2, (g + 1) * heads_per_group // 2):
                h0 = 2 * j
                sl = slice(2 * P * j, 2 * P * (j + 1))
                x2 = x_ref[:, sl]
                dy2 = dy_ref[:, sl]
                dt2, e2, ds2 = sel(dtc, h0), sel(e, h0), sel(ds, h0)
                X2 = x2 * dt2
                X2b = X2.astype(BF16)
                H2 = hst_ref[0, sl, :]
                H2b = H2.astype(BF16)
                dHn = dstate[sl, :]
                dHnb = dHn.astype(BF16)
                R2 = _dot(Cg, H2b, _NT)
                dR2b = (e2 * dy2).astype(BF16)
                de2 = dy2 * R2
                dCg = dCg + _dot(dR2b, H2b, _NN)
                dHr = _dot(dR2b, Cg, _TN)
                rs = jnp.sum(dHn * H2, axis=1, keepdims=True)
                dXd = _dot(Bg, dHnb, _NT)
                dBg = dBg + _dot((X2 * ds2).astype(BF16), dHnb, _NN)
                dds2 = dXd * X2
                dX2 = ds2 * dXd
                for lo, h in ((True, h0), (False, h0 + 1)):
                    ds_h, cd_h = ds[:, h:h + 1], cd[:, h:h + 1]
                    dcd = jnp.sum(jnp.where(half(rowc < P, lo), rs, 0.0), axis=0, keepdims=True)
                    t1 = lsum(dds2, lo) * ds_h
                    tail = jnp.sum(t1, axis=0, keepdims=True) + dcd * cd_h
                    dcs_h = lsum(de2, lo) * e[:, h:h + 1] - t1 + jnp.where(last_row, tail, 0.0)
                    seg = cs[:, h:h + 1] - csr_ref[h:h + 1, :]
                    Dm = jnp.where(tri, jnp.exp(jnp.where(tri, seg, 0.0)), 0.0)
                    Mh = Gm * Dm
                    dyh = jnp.where(half(lane_lo, lo), dy2, 0.0).astype(BF16)
                    dM = _dot(dyh, X2b, _NT)
                    dG = dG + dM * Dm
                    dX2 = dX2 + _dot(Mh.astype(BF16), dyh, _TN)
                    W = _dot(triu_b, (dM * Mh).astype(BF16), _NN)
                    da_q = jnp.sum(jnp.where(c < r, W, 0.0), axis=1, keepdims=True)
                    oh = (lane1 == h).astype(F32)
                    dcs_acc = dcs_acc + dcs_h * oh
                    dadir_acc = dadir_acc + da_q * oh
                dxx = dX2 * x2
                q = jnp.sum(dy2 * x2, axis=0, keepdims=True)
                for lo, h in ((True, h0), (False, h0 + 1)):
                    oh = (lane1 == h).astype(F32)
                    ddtx_acc = ddtx_acc + lsum(dxx, lo) * oh
                    dd_acc = dd_acc + jnp.sum(jnp.where(half(lane1 < P, lo), q, 0.0), axis=1, keepdims=True) * oh
                dsk2 = jnp.where(lane1 < P, dsk[:, h0:h0 + 1], dsk[:, h0 + 1:h0 + 2])
                dx_ref[:, sl] = dX2 * dt2 + dy2 * dsk2
                dstate[sl, :] = jnp.where(row_lo, cd[:, h0:h0 + 1], cd[:, h0 + 1:h0 + 2]) * dHn + dHr
            dGb = dG.astype(BF16)
            dx_ref[:, bsl] = dBg + _dot(dGb, Cg, _TN)
            dx_ref[:, csl] = dCg + _dot(dGb, Bg, _NN)
        da_ref[...] = jnp.dot(triu, dcs_acc, precision=HIGHEST, preferred_element_type=F32) + dadir_acc
        ddtx_ref[...] = ddtx_acc
        dd_ref[...] += dd_acc

    vec = pl.BlockSpec((1, LANES), lambda i: (0, 0))
    rev = lambda i: (nc - 1 - i, 0)
    return pl.pallas_call(
        body, name=name,
        out_shape=(_sds((T, XBC), F32), _sds((T, LANES), F32), _sds((T, LANES), F32), _sds((1, LANES), F32)),
        grid=(nc,),
        in_specs=[pl.BlockSpec((L, XBC), rev), pl.BlockSpec((L, LANES), rev),
                  pl.BlockSpec((HR, L), lambda i: (0, nc - 1 - i)), vec, pl.BlockSpec((HR, 1), lambda i: (0, 0)), vec,
                  pl.BlockSpec((1, SX, N), lambda i: (nc - 1 - i, 0, 0)), pl.BlockSpec((L, SX), rev)],
        out_specs=(pl.BlockSpec((L, XBC), rev), pl.BlockSpec((L, LANES), rev), pl.BlockSpec((L, LANES), rev), vec),
        scratch_shapes=[pltpu.VMEM((SX, N), F32), pltpu.VMEM((L, LANES), F32), pltpu.VMEM((HR, L), F32)],
        compiler_params=_cp("arbitrary"))(xbc, dt, dtT, a_col, a_row, d_col, hst, dy)


def _rms_fwd(y, u, zoff, w, *, name):
    T, SX = y.shape
    tt = _tile(T, 256)
    gs = SX // SSM_GROUPS
    zb = zoff // SX

    def body(y_ref, z_ref, w_ref, o_ref):
        for g in range(SSM_GROUPS):
            sl = slice(g * gs, (g + 1) * gs)
            z = z_ref[:, sl]
            yg = y_ref[:, sl] * (z * _sigmoid(z))
            rstd = lax.rsqrt(jnp.mean(yg * yg, axis=-1, keepdims=True) + RMS_EPS)
            o_ref[:, sl] = (yg * rstd * w_ref[:, sl]).astype(BF16)

    row = pl.BlockSpec((tt, SX), lambda i: (i, 0))
    return pl.pallas_call(body, name=name, out_shape=_sds((T, SX), BF16), grid=(T // tt,),
                          in_specs=[row, pl.BlockSpec((tt, SX), lambda i: (i, zb)), pl.BlockSpec((1, SX), lambda i: (0, 0))],
                          out_specs=row, compiler_params=_cp("parallel"))(y, u, w)


def _rms_bwd(y, u, zoff, w, dyn, *, name):
    T, SX = y.shape
    tt = _tile(T, 256)
    gs = SX // SSM_GROUPS
    zb = zoff // SX

    def body(y_ref, z_ref, w_ref, d_ref, dy_ref, dz_ref, dw_ref):
        i = pl.program_id(0)

        @pl.when(i == 0)
        def _():
            dw_ref[...] = jnp.zeros_like(dw_ref)

        for g in range(SSM_GROUPS):
            sl = slice(g * gs, (g + 1) * gs)
            z, yv, d = z_ref[:, sl], y_ref[:, sl], d_ref[:, sl]
            sz, dsz = _silu_and_grad(z)
            yg = yv * sz
            rstd = lax.rsqrt(jnp.mean(yg * yg, axis=-1, keepdims=True) + RMS_EPS)
            t = yg * rstd
            dw_ref[:, sl] += jnp.sum(d * t, axis=0, keepdims=True)
            dt_ = d * w_ref[:, sl]
            dyg = rstd * (dt_ - t * jnp.mean(dt_ * t, axis=-1, keepdims=True))
            dy_ref[:, sl] = dyg * sz
            dz_ref[:, sl] = (dyg * yv * dsz).astype(BF16)

    row = pl.BlockSpec((tt, SX), lambda i: (i, 0))
    vec = pl.BlockSpec((1, SX), lambda i: (0, 0))
    return pl.pallas_call(body, name=name, out_shape=(_sds((T, SX), F32), _sds((T, SX), BF16), _sds((1, SX), F32)),
                          grid=(T // tt,), in_specs=[row, pl.BlockSpec((tt, SX), lambda i: (i, zb)), vec, row],
                          out_specs=(row, row, vec), compiler_params=_cp("arbitrary"))(y, u, w, dyn)


def _gate_fwd(u, goff, ya, yb, *, name):
    T, D = ya.shape
    tt = _tile(T, 512)
    gb = goff // D

    def body(ga_ref, gb_ref, ya_ref, yb_ref, o_ref):
        o_ref[...] = (_sigmoid(ga_ref[...]) * ya_ref[...] + _sigmoid(gb_ref[...]) * yb_ref[...]).astype(BF16)

    row = pl.BlockSpec((tt, D), lambda i: (i, 0))
    return pl.pallas_call(body, name=name, out_shape=_sds((T, D), BF16), grid=(T // tt,),
                          in_specs=[pl.BlockSpec((tt, D), lambda i: (i, gb)), pl.BlockSpec((tt, D), lambda i: (i, gb + 1)), row, row],
                          out_specs=row, compiler_params=_cp("parallel"))(u, u, ya, yb)


def _gate_bwd(u, goff, ya, yb, dm, *, name):
    T, D = ya.shape
    tt = _tile(T, 512)
    gb = goff // D

    def body(ga_ref, gb_ref, ya_ref, yb_ref, dm_ref, dya_ref, dyb_ref, dg_ref):
        d = dm_ref[...]
        sa, sb = _sigmoid(ga_ref[...]), _sigmoid(gb_ref[...])
        dya_ref[...] = (d * sa).astype(BF16)
        dyb_ref[...] = (d * sb).astype(BF16)
        dg_ref[...] = jnp.concatenate([d * ya_ref[...] * sa * (1.0 - sa), d * yb_ref[...] * sb * (1.0 - sb)], axis=1).astype(BF16)

    row = pl.BlockSpec((tt, D), lambda i: (i, 0))
    return pl.pallas_call(body, name=name, out_shape=(_sds((T, D), BF16), _sds((T, D), BF16), _sds((T, 2 * D), BF16)),
                          grid=(T // tt,),
                          in_specs=[pl.BlockSpec((tt, D), lambda i: (i, gb)), pl.BlockSpec((tt, D), lambda i: (i, gb + 1)), row, row, row],
                          out_specs=(row, row, pl.BlockSpec((tt, 2 * D), lambda i: (i, 0))),
                          compiler_params=_cp("parallel"))(u, u, ya, yb, dm)


def _adamw(w, g, m, v, *, name):
    R, C = w.shape
    tr = _pick(R, 256, 8)

    def body(w_ref, g_ref, m_ref, v_ref, d_ref, nm_ref, nv_ref):
        gg = g_ref[...]
        nm = ADAM_B1 * m_ref[...] + (1.0 - ADAM_B1) * gg
        nv = ADAM_B2 * v_ref[...] + (1.0 - ADAM_B2) * (gg * gg)
        m_hat = nm / (1.0 - ADAM_B1 ** ADAM_STEP)
        v_hat = nv / (1.0 - ADAM_B2 ** ADAM_STEP)
        d_ref[...] = -ADAM_LR * (m_hat / (jnp.sqrt(v_hat) + ADAM_EPS) + ADAM_WD * w_ref[...])
        nm_ref[...] = nm
        nv_ref[...] = nv

    blk = pl.BlockSpec((tr, C), lambda i: (i, 0))
    out = _sds((R, C), F32)
    return pl.pallas_call(body, name=name, out_shape=(out, out, out), grid=(R // tr,), in_specs=[blk] * 4,
                          out_specs=(blk,) * 3, compiler_params=_cp("parallel"))(w, g, m, v)


def _sum_slots(x, *, name):
    n, R, C = x.shape
    tr = _tile(R, 512)

    def body(x_ref, o_ref):
        acc = x_ref[0].astype(F32)
        for k in range(1, n):
            acc = acc + x_ref[k].astype(F32)
        o_ref[...] = acc

    return pl.pallas_call(body, name=name, out_shape=_sds((R, C), F32), grid=(R // tr,),
                          in_specs=[pl.BlockSpec((n, tr, C), lambda i: (0, i, 0))],
                          out_specs=pl.BlockSpec((tr, C), lambda i: (i, 0)), compiler_params=_cp("parallel"))(x)


def _exchange(x, *, scatter, name):
    R, C = x.shape[-2:]

    def body(x_ref, o_ref, send_sems, recv_sems, local_sem):
        mx, my, mc = lax.axis_index("x"), lax.axis_index("y"), lax.axis_index("c")
        me = 4 * mx + 2 * my + mc

        def src(d):
            return x_ref.at[d] if scatter else x_ref

        local = pltpu.make_async_copy(src(me), o_ref.at[me], local_sem)
        local.start()
        sends, recvs = [], []
        for k in range(1, N_DEV):
            px = 1 - mx if k & 4 else mx
            py = 1 - my if k & 2 else my
            pc = 1 - mc if k & 1 else mc
            peer = 4 * px + 2 * py + pc
            sends.append(pltpu.make_async_remote_copy(
                src_ref=src(peer), dst_ref=o_ref.at[me], send_sem=send_sems.at[k - 1], recv_sem=recv_sems.at[k - 1],
                device_id=(px, py, pc), device_id_type=pl.DeviceIdType.MESH))
            recvs.append(pltpu.make_async_remote_copy(
                src_ref=src(peer), dst_ref=o_ref.at[peer], send_sem=send_sems.at[k - 1], recv_sem=recv_sems.at[k - 1],
                device_id=(px, py, pc), device_id_type=pl.DeviceIdType.MESH))
        for cp in sends:
            cp.start()
        for cp in recvs:
            cp.wait_recv()
        for cp in sends:
            cp.wait_send()
        local.wait()

    any_spec = pl.BlockSpec(memory_space=pl.ANY)
    return pl.pallas_call(
        body, name=name, out_shape=_sds((N_DEV, R, C), x.dtype), in_specs=[any_spec], out_specs=any_spec,
        scratch_shapes=[pltpu.SemaphoreType.DMA((N_DEV - 1,)), pltpu.SemaphoreType.DMA((N_DEV - 1,)),
                        pltpu.SemaphoreType.DMA(())])(x)


def _to_rows(flat, row_mult):
    n = flat.shape[-1]
    per = row_mult * LANES
    pad = (-n) % per
    flat = jnp.pad(flat, [(0, 0)] * (flat.ndim - 1) + [(0, pad)])
    return flat.reshape(flat.shape[:-1] + ((n + pad) // LANES, LANES))


def _gather_weights(p):
    parts, metas = [], []
    for name, (axis, as_bf16) in SHARDED.items():
        w = p[name]
        if as_bf16:
            flat = w.astype(WIRE).reshape(-1)
        else:
            flat = lax.bitcast_convert_type(w, WIRE).reshape(-1)
        parts.append(flat)
        metas.append((name, axis, as_bf16, w.shape, flat.shape[0]))
    got = _exchange(_to_rows(jnp.concatenate(parts), 512), scatter=False, name="gather_weights")
    got = got.reshape(N_DEV, -1)
    full, off = {}, 0
    for name, axis, as_bf16, shape, n in metas:
        g = got[:, off:off + n]
        off += n
        if as_bf16:
            g = g.reshape((N_DEV,) + shape)
        else:
            g = lax.bitcast_convert_type(g.reshape((N_DEV,) + shape + (2,)), F32)
        g = jnp.moveaxis(g, 0, axis)
        full[name] = g.reshape(shape[:axis] + (N_DEV * shape[axis],) + shape[axis + 1:])
    return full


def _scatter_grads(grads, p):
    parts, metas = [], []
    for name, (axis, _) in SHARDED.items():
        g = grads[name]
        shard_shape = p[name].shape
        g = g.reshape(g.shape[:axis] + (N_DEV, shard_shape[axis]) + g.shape[axis + 1:])
        g = jnp.moveaxis(g, axis, 0).reshape(N_DEV, -1)
        parts.append(g.astype(WIRE))
        metas.append((name, shard_shape, g.shape[1]))
    got = _exchange(_to_rows(jnp.concatenate(parts, axis=1), 512), scatter=True, name="scatter_grads")
    summed = _sum_slots(got, name="sum_grads").reshape(-1)
    out, off = {}, 0
    for name, shard_shape, n in metas:
        out[name] = summed[off:off + n].reshape(shard_shape)
        off += n
    return out


def _allreduce_small(vals):
    metas = [(k, v.shape, v.size) for k, v in vals.items()]
    flat = jnp.concatenate([v.reshape(-1) for v in vals.values()])
    got = _exchange(_to_rows(flat, 8), scatter=False, name="gather_small_grads")
    summed = _sum_slots(got, name="sum_small_grads").reshape(-1)
    out, off = {}, 0
    for k, shape, n in metas:
        out[k] = summed[off:off + n].reshape(shape)
        off += n
    return out


def _interleave(a, g, tile):
    C = a.shape[-1]
    n = C // tile
    lead = a.shape[:-1]
    return jnp.concatenate([a.reshape(lead + (n, 1, tile)), g.reshape(lead + (n, 1, tile))], axis=-2).reshape(lead + (2 * C,))


def _deinterleave(w, tile):
    C = w.shape[-1] // 2
    n = C // tile
    lead = w.shape[:-1]
    r = w.reshape(lead + (n, 2, tile))
    return r[..., 0, :].reshape(lead + (C,)), r[..., 1, :].reshape(lead + (C,))


def _row(v):
    return v.reshape(1, -1).astype(F32)


def _lanes(v):
    return jnp.pad(v.astype(F32), (0, LANES - v.shape[0])).reshape(1, LANES)


def _layer_weights(p, full, l, dims):
    D, SX, XBC, H, FF = dims
    gt, ft = _tile(D, GLU_TILE), _tile(FF, FFN_TILE)
    w_in = full['w_in'][l]
    o = 0
    seg = {}
    for nm, n in (('a', D), ('g', D), ('z', SX), ('xbc', XBC), ('dt', H), ('ga', D), ('gb', D)):
        seg[nm] = w_in[:, o:o + n]
        o += n
    w_main = jnp.concatenate([_interleave(seg['a'], seg['g'], gt), seg['z'], seg['xbc'], seg['ga'], seg['gb']], axis=1)
    w_dt = jnp.pad(seg['dt'], ((0, 0), (0, LANES - H)))
    up = full['w_ffn_up'][l]
    fw = full['ffn_dw_w'][l]
    fb = p['ffn_dw_b'][l]
    a_head = -jnp.exp(p['ssm_a_log'][l].astype(F32))
    hr = -(-H // 8) * 8
    return dict(
        w_main=w_main, w_dt=w_dt,
        conv_w=full['conv_dw_w'][l], conv_b=_row(p['conv_dw_b'][l]), conv_g=_row(p['conv_ln_g'][l]), conv_beta=_row(p['conv_ln_b'][l]),
        w_conv_out=full['w_conv_out'][l],
        ssm_w=full['ssm_conv_w'][l], ssm_b=_row(p['ssm_conv_b'][l]),
        dt_bias=_lanes(p['ssm_dt_bias'][l]), a_col=_lanes(a_head),
        a_row=jnp.pad(a_head, (0, hr - H)).reshape(hr, 1), d_col=_lanes(p['ssm_d'][l]),
        norm_w=_row(p['ssm_norm_w'][l]), w_ssm_out=full['w_ssm_out'][l], w_o=full['w_o'][l],
        ln1_g=_row(p['ln1_g'][l]), ln1_b=_row(p['ln1_b'][l]),
        w_up=_interleave(up[:, :FF], up[:, FF:], ft),
        ffn_w=_interleave(fw[:, :FF], fw[:, FF:], ft), ffn_b=_row(_interleave(fb[:FF], fb[FF:], ft)),
        w_down=full['w_ffn_down'][l], ln2_g=_row(p['ln2_g'][l]), ln2_b=_row(p['ln2_b'][l]))


def _layer_fwd(h, W, l, dims):
    D, SX, XBC, H, FF = dims
    o_z, o_xbc, o_gate = 2 * D, 2 * D + SX, 2 * D + SX + XBC
    hr = W['a_row'].shape[0]
    t = f"l{l}"
    u = _mm(h, W['w_main'], mode='nn', out_dtype=F32, name=t + "_in_proj")
    udt = _mm(h, W['w_dt'], mode='nn', out_dtype=F32, name=t + "_dt_proj")
    v1, v3 = _conf_fwd(u, W['conv_w'], W['conv_b'], W['conv_g'], W['conv_beta'], D=D, name=t + "_conf_fwd")
    ya = _mm(v3, W['w_conv_out'], mode='nn', out_dtype=F32, name=t + "_conv_out")
    xbc = _ssm_conv_fwd(u, o_xbc, W['ssm_w'], W['ssm_b'], name=t + "_ssm_conv")
    dt = _dt_fwd(udt, W['dt_bias'], H=H, name=t + "_dt")
    dtT = jnp.pad(dt[:, :H].T, ((0, hr - H), (0, 0)))
    y, hst = _ssd_fwd(xbc, dt, dtT, W['a_col'], W['a_row'], W['d_col'], H=H, name=t + "_ssd_fwd")
    yn = _rms_fwd(y, u, o_z, W['norm_w'], name=t + "_rms_fwd")
    yb = _mm(yn, W['w_ssm_out'], mode='nn', out_dtype=F32, name=t + "_ssm_out")
    m = _gate_fwd(u, o_gate, ya, yb, name=t + "_gate_fwd")
    h1, s1 = _mm_res_ln(m, W['w_o'], h, W['ln1_g'], W['ln1_b'], name=t + "_mix_ln1")
    uf = _mm(h1, W['w_up'], mode='nn', out_dtype=F32, name=t + "_ffn_up")
    act = _ffn_conv_fwd(uf, W['ffn_w'], W['ffn_b'], name=t + "_ffn_conv")
    h2, s2 = _mm_res_ln(act, W['w_down'], h1, W['ln2_g'], W['ln2_b'], name=t + "_ffn_down_ln2")
    saved = dict(h=h, u=u, udt=udt, v1=v1, v3=v3, ya=ya, xbc=xbc, dt=dt, dtT=dtT, y=y, hst=hst, yn=yn, yb=yb, m=m,
                 h1=h1, s1=s1, uf=uf, act=act, s2=s2)
    return h2, saved


def _layer_bwd(dh2, W, S, l, dims):
    D, SX, XBC, H, FF = dims
    o_z, o_xbc, o_gate = 2 * D, 2 * D + SX, 2 * D + SX + XBC
    t = f"l{l}"
    g = {}
    ds2, g['ln2_g'], g['ln2_b'] = _ln_bwd(S['s2'], dh2, W['ln2_g'], W['ln2_b'], silu=False, name=t + "_ln2_bwd")
    g['w_down'] = _mm(S['act'], ds2, mode='tn', out_dtype=F32, name=t + "_dw_down")
    dact = _mm(ds2, W['w_down'], mode='nt', out_dtype=F32, name=t + "_dact")
    dpre = _ffn_act_bwd(S['uf'], W['ffn_w'], W['ffn_b'], dact, name=t + "_ffn_act_bwd")
    duf, g['ffn_w'], g['ffn_b'] = _dwconv_bwd(dpre, S['uf'], 0, W['ffn_w'], glu=False, name=t + "_ffn_conv_bwd")
    g['w_up'] = _mm(S['h1'], duf, mode='tn', out_dtype=F32, name=t + "_dw_up")
    dh1 = _mm(duf, W['w_up'], mode='nt', out_dtype=F32, res=ds2, res_scale=DN_ALPHA, name=t + "_dh1")
    ds1, g['ln1_g'], g['ln1_b'] = _ln_bwd(S['s1'], dh1, W['ln1_g'], W['ln1_b'], silu=False, name=t + "_ln1_bwd")
    g['w_o'] = _mm(S['m'], ds1, mode='tn', out_dtype=F32, name=t + "_dw_o")
    dm = _mm(ds1, W['w_o'], mode='nt', out_dtype=F32, name=t + "_dm")
    dya, dyb, dgate = _gate_bwd(S['u'], o_gate, S['ya'], S['yb'], dm, name=t + "_gate_bwd")
    g['w_conv_out'] = _mm(S['v3'], dya, mode='tn', out_dtype=F32, name=t + "_dw_conv_out")
    dv3 = _mm(dya, W['w_conv_out'], mode='nt', out_dtype=F32, name=t + "_dv3")
    dv1, g['conv_g'], g['conv_beta'] = _ln_bwd(S['v1'], dv3, W['conv_g'], W['conv_beta'], silu=True, name=t + "_conv_ln_bwd")
    dglu, g['conv_w'], g['conv_b'] = _dwconv_bwd(dv1, S['u'], 0, W['conv_w'], glu=True, name=t + "_conf_conv_bwd")
    g['w_ssm_out'] = _mm(S['yn'], dyb, mode='tn', out_dtype=F32, name=t + "_dw_ssm_out")
    dyn = _mm(dyb, W['w_ssm_out'], mode='nt', out_dtype=F32, name=t + "_dyn")
    dy, dz, g['norm_w'] = _rms_bwd(S['y'], S['u'], o_z, W['norm_w'], dyn, name=t + "_rms_bwd")
    dxbc_c, da, ddtx, g['d'] = _ssd_bwd(S['xbc'], S['dt'], S['dtT'], W['a_col'], W['a_row'], W['d_col'], S['hst'], dy,
                                        H=H, name=t + "_ssd_bwd")
    ddt_raw, g['dt_bias'], g['a_log'] = _dt_bwd(da, ddtx, S['dt'], S['udt'], W['dt_bias'], W['a_col'], H=H, name=t + "_dt_bwd")
    dpre_x = _ssm_act_bwd(S['u'], o_xbc, W['ssm_w'], W['ssm_b'], dxbc_c, name=t + "_ssm_act_bwd")
    dxbc, g['ssm_w'], g['ssm_b'] = _dwconv_bwd(dpre_x, S['u'], o_xbc, W['ssm_w'], glu=False, name=t + "_ssm_conv_bwd")
    du = jnp.concatenate([dglu, dz, dxbc, dgate], axis=1)
    g['w_main'] = _mm(S['h'], du, mode='tn', out_dtype=F32, name=t + "_dw_main")
    g['w_dt'] = _mm(S['h'], ddt_raw, mode='tn', out_dtype=F32, name=t + "_dw_dt")
    dh = _mm(du, W['w_main'], mode='nt', out_dtype=F32, res=ds1, res_scale=DN_ALPHA, name=t + "_dh_main")
    dh = _mm(ddt_raw, W['w_dt'], mode='nt', out_dtype=F32, res=dh, res_scale=1.0, name=t + "_dh_dt")
    return dh, g


def _layer_grads_to_params(g, dims):
    D, SX, XBC, H, FF = dims
    gt, ft = _tile(D, GLU_TILE), _tile(FF, FFN_TILE)
    wm = g['w_main']
    da, dg = _deinterleave(wm[:, :2 * D], gt)
    rest = wm[:, 2 * D:]
    w_in = jnp.concatenate([da, dg, rest[:, :SX + XBC], g['w_dt'][:, :H], rest[:, SX + XBC:]], axis=1)
    fwg, fwv = _deinterleave(g['ffn_w'], ft)
    fbg, fbv = _deinterleave(g['ffn_b'], ft)
    upg, upv = _deinterleave(g['w_up'], ft)
    return dict(
        w_in=w_in, conv_dw_w=g['conv_w'], conv_dw_b=g['conv_b'][0], conv_ln_g=g['conv_g'][0], conv_ln_b=g['conv_beta'][0],
        w_conv_out=g['w_conv_out'], ssm_conv_w=g['ssm_w'], ssm_conv_b=g['ssm_b'][0],
        ssm_dt_bias=g['dt_bias'][0, :H], ssm_a_log=g['a_log'][0, :H], ssm_d=g['d'][0, :H], ssm_norm_w=g['norm_w'][0],
        w_ssm_out=g['w_ssm_out'], w_o=g['w_o'], ln1_g=g['ln1_g'][0], ln1_b=g['ln1_b'][0],
        w_ffn_up=jnp.concatenate([upg, upv], axis=1), ffn_dw_w=jnp.concatenate([fwg, fwv], axis=1),
        ffn_dw_b=jnp.concatenate([fbg, fbv], axis=1)[0], w_ffn_down=g['w_down'], ln2_g=g['ln2_g'][0], ln2_b=g['ln2_b'][0])


def kernel(x, ln_in_g, ln_in_b, w_in, conv_dw_w, conv_dw_b, conv_ln_g, conv_ln_b, w_conv_out, ssm_conv_w, ssm_conv_b, ssm_dt_bias, ssm_a_log, ssm_d, ssm_norm_w, w_ssm_out, w_o, ln1_g, ln1_b, w_ffn_up, ffn_dw_w, ffn_dw_b, w_ffn_down, ln2_g, ln2_b, loss_target, m_ln_in_g, m_ln_in_b, m_w_in, m_conv_dw_w, m_conv_dw_b, m_conv_ln_g, m_conv_ln_b, m_w_conv_out, m_ssm_conv_w, m_ssm_conv_b, m_ssm_dt_bias, m_ssm_a_log, m_ssm_d, m_ssm_norm_w, m_w_ssm_out, m_w_o, m_ln1_g, m_ln1_b, m_w_ffn_up, m_ffn_dw_w, m_ffn_dw_b, m_w_ffn_down, m_ln2_g, m_ln2_b, v_ln_in_g, v_ln_in_b, v_w_in, v_conv_dw_w, v_conv_dw_b, v_conv_ln_g, v_conv_ln_b, v_w_conv_out, v_ssm_conv_w, v_ssm_conv_b, v_ssm_dt_bias, v_ssm_a_log, v_ssm_d, v_ssm_norm_w, v_w_ssm_out, v_w_o, v_ln1_g, v_ln1_b, v_w_ffn_up, v_ffn_dw_w, v_ffn_dw_b, v_w_ffn_down, v_ln2_g, v_ln2_b):
    weights = (ln_in_g, ln_in_b, w_in, conv_dw_w, conv_dw_b, conv_ln_g, conv_ln_b, w_conv_out, ssm_conv_w, ssm_conv_b,
               ssm_dt_bias, ssm_a_log, ssm_d, ssm_norm_w, w_ssm_out, w_o, ln1_g, ln1_b, w_ffn_up, ffn_dw_w, ffn_dw_b,
               w_ffn_down, ln2_g, ln2_b)
    moments_m = (m_ln_in_g, m_ln_in_b, m_w_in, m_conv_dw_w, m_conv_dw_b, m_conv_ln_g, m_conv_ln_b, m_w_conv_out,
                 m_ssm_conv_w, m_ssm_conv_b, m_ssm_dt_bias, m_ssm_a_log, m_ssm_d, m_ssm_norm_w, m_w_ssm_out, m_w_o,
                 m_ln1_g, m_ln1_b, m_w_ffn_up, m_ffn_dw_w, m_ffn_dw_b, m_w_ffn_down, m_ln2_g, m_ln2_b)
    moments_v = (v_ln_in_g, v_ln_in_b, v_w_in, v_conv_dw_w, v_conv_dw_b, v_conv_ln_g, v_conv_ln_b, v_w_conv_out,
                 v_ssm_conv_w, v_ssm_conv_b, v_ssm_dt_bias, v_ssm_a_log, v_ssm_d, v_ssm_norm_w, v_w_ssm_out, v_w_o,
                 v_ln1_g, v_ln1_b, v_w_ffn_up, v_ffn_dw_w, v_ffn_dw_b, v_w_ffn_down, v_ln2_g, v_ln2_b)
    p = dict(zip(PARAMS, weights))
    pm = dict(zip(PARAMS, moments_m))
    pv = dict(zip(PARAMS, moments_v))

    T, D = x.shape[1], x.shape[2]
    SX = w_ssm_out.shape[1] * N_DEV
    XBC = ssm_conv_b.shape[-1]
    H = ssm_d.shape[-1]
    FF = ffn_dw_b.shape[-1] // 2
    dims = (D, SX, XBC, H, FF)
    depth = w_in.shape[0]

    full = _gather_weights(p)
    layers = [_layer_weights(p, full, l, dims) for l in range(depth)]

    xs = x.reshape(T, D)
    h = _ln_fwd(xs, _row(ln_in_g), _row(ln_in_b), name="ln_in_fwd")
    saved = []
    for l in range(depth):
        h, s = _layer_fwd(h, layers[l], l, dims)
        saved.append(s)
    loss_part, dh = _loss_fwd_bwd(h, loss_target.reshape(T, D), name="loss")

    layer_grads = [None] * depth
    for l in reversed(range(depth)):
        dh, g = _layer_bwd(dh, layers[l], saved[l], l, dims)
        layer_grads[l] = _layer_grads_to_params(g, dims)
    grad_x, dg_in, db_in = _ln_bwd(xs, dh, _row(ln_in_g), _row(ln_in_b), silu=False, name="ln_in_bwd")

    local = {k: jnp.stack([layer_grads[l][k] for l in range(depth)]) for k in layer_grads[0]}
    local['ln_in_g'], local['ln_in_b'] = dg_in[0], db_in[0]

    grads = _scatter_grads(local, p)
    small = {k: local[k] for k in PARAMS if k not in SHARDED}
    small['loss'] = loss_part
    small = _allreduce_small(small)
    loss = small.pop('loss').reshape(())
    grads.update(small)

    delta, new_m, new_v = {}, {}, {}
    big = [k for k in PARAMS if k in SHARDED and SHARDED[k][1]]
    for k in big:
        shp = p[k].shape
        two_d = (shp[0] * shp[1], shp[2])
        d_, m_, v_ = _adamw(p[k].reshape(two_d), grads[k].reshape(two_d), pm[k].reshape(two_d), pv[k].reshape(two_d), name="adamw_" + k)
        delta[k], new_m[k], new_v[k] = d_.reshape(shp), m_.reshape(shp), v_.reshape(shp)
    rest = [k for k in PARAMS if k not in big]
    flat = lambda d: _to_rows(jnp.concatenate([d[k].reshape(-1) for k in rest]), 8)
    d_, m_, v_ = _adamw(flat(p), flat(grads), flat(pm), flat(pv), name="adamw_small")
    off = 0
    for k in rest:
        n, shp = p[k].size, p[k].shape
        delta[k] = d_.reshape(-1)[off:off + n].reshape(shp)
        new_m[k] = m_.reshape(-1)[off:off + n].reshape(shp)
        new_v[k] = v_.reshape(-1)[off:off + n].reshape(shp)
        off += n

    return (loss, grad_x.reshape(x.shape), *[grads[k] for k in PARAMS], *[delta[k] for k in PARAMS],
            *[new_m[k] for k in PARAMS], *[new_v[k] for k in PARAMS])
```

```python
import math

import jax
import jax.numpy as jnp
from jax import lax
from jax.experimental import pallas as pl
from jax.experimental.pallas import tpu as pltpu

F32 = jnp.float32
BF16 = jnp.bfloat16
WIRE = jnp.bfloat16
HIGHEST = lax.Precision.HIGHEST

DEPTH = 2
SSM_STATE = 128
SSM_CHUNK = 128
SSM_GROUPS = 4
SSM_HEAD_DIM = 64
DN_ALPHA = (2 * DEPTH) ** 0.25
LN_EPS = 1e-5
RMS_EPS = 1e-5
ADAM_LR = 0.001
ADAM_B1 = 0.9
ADAM_B2 = 0.999
ADAM_EPS = 1e-08
ADAM_WD = 0.01
ADAM_STEP = 10

N_DEV = 8
LANES = 128
VMEM_LIMIT = 48 * 1024 * 1024
GLU_TILE = 512
FFN_TILE = 1408
CONV_ROWS = 256

PARAMS = ['ln_in_g', 'ln_in_b', 'w_in', 'conv_dw_w', 'conv_dw_b', 'conv_ln_g', 'conv_ln_b', 'w_conv_out',
          'ssm_conv_w', 'ssm_conv_b', 'ssm_dt_bias', 'ssm_a_log', 'ssm_d', 'ssm_norm_w', 'w_ssm_out', 'w_o',
          'ln1_g', 'ln1_b', 'w_ffn_up', 'ffn_dw_w', 'ffn_dw_b', 'w_ffn_down', 'ln2_g', 'ln2_b']
SHARDED = {'w_in': (2, True), 'conv_dw_w': (2, False), 'w_conv_out': (1, True), 'ssm_conv_w': (2, False),
           'w_ssm_out': (1, True), 'w_o': (1, True), 'w_ffn_up': (2, True), 'ffn_dw_w': (2, False),
           'w_ffn_down': (1, True)}


def _sds(shape, dtype):
    return jax.ShapeDtypeStruct(tuple(shape), dtype)


def _tile(dim, pref):
    return pref if dim % pref == 0 else dim


def _pick(dim, pref, mult):
    best = None
    for t in range(mult, min(dim, pref) + 1, mult):
        if dim % t == 0:
            best = t
    return best or dim


def _cp(*sem):
    return pltpu.CompilerParams(dimension_semantics=sem, vmem_limit_bytes=VMEM_LIMIT)


def _sigmoid(x):
    return 1.0 / (1.0 + jnp.exp(-x))


def _silu_and_grad(x):
    s = _sigmoid(x)
    return x * s, s * (1.0 + x * (1.0 - s))


def _ln_stats(s):
    mu = jnp.mean(s, axis=-1, keepdims=True)
    xc = s - mu
    var = jnp.mean(xc * xc, axis=-1, keepdims=True)
    rstd = lax.rsqrt(var + LN_EPS)
    return xc * rstd, rstd


def _dot(a, b, dims):
    return lax.dot_general(a, b, (dims, ((), ())), preferred_element_type=F32)


_NN = ((1,), (0,))
_NT = ((1,), (1,))
_TN = ((0,), (0,))


def _mm(a, b, *, mode, out_dtype, name, res=None, res_scale=1.0, tm=1024, tn=1408, tk=1408):
    if mode == 'nn':
        (M, K), (_, N) = a.shape, b.shape
    elif mode == 'nt':
        (M, K), (N, _) = a.shape, b.shape
    else:
        (K, M), (_, N) = a.shape, b.shape
        tm, tk = 1408, 512
    tm, tn, tk = _pick(M, tm, 8), _pick(N, tn, LANES), _pick(K, tk, LANES)
    nk = K // tk
    dims = {'nn': _NN, 'nt': _NT, 'tn': _TN}[mode]

    def body(*refs):
        if res is None:
            a_ref, b_ref, o_ref = refs[:3]
            r_ref = None
        else:
            a_ref, b_ref, r_ref, o_ref = refs[:4]
        p = _dot(a_ref[...].astype(BF16), b_ref[...].astype(BF16), dims)

        def finish(acc):
            if r_ref is not None:
                acc = acc + res_scale * r_ref[...]
            o_ref[...] = acc.astype(out_dtype)

        if nk == 1:
            finish(p)
        else:
            acc_ref = refs[-1]
            k = pl.program_id(2)

            @pl.when(k == 0)
            def _():
                acc_ref[...] = p

            @pl.when(k > 0)
            def _():
                acc_ref[...] += p

            @pl.when(k == nk - 1)
            def _():
                finish(acc_ref[...])

    if mode == 'nn':
        a_spec = pl.BlockSpec((tm, tk), lambda i, j, k: (i, k))
        b_spec = pl.BlockSpec((tk, tn), lambda i, j, k: (k, j))
    elif mode == 'nt':
        a_spec = pl.BlockSpec((tm, tk), lambda i, j, k: (i, k))
        b_spec = pl.BlockSpec((tn, tk), lambda i, j, k: (j, k))
    else:
        a_spec = pl.BlockSpec((tk, tm), lambda i, j, k: (k, i))
        b_spec = pl.BlockSpec((tk, tn), lambda i, j, k: (k, j))
    o_spec = pl.BlockSpec((tm, tn), lambda i, j, k: (i, j))
    in_specs = [a_spec, b_spec] + ([o_spec] if res is not None else [])
    args = (a, b) + ((res,) if res is not None else ())
    return pl.pallas_call(
        body, name=name, out_shape=_sds((M, N), out_dtype), grid=(M // tm, N // tn, nk),
        in_specs=in_specs, out_specs=o_spec,
        scratch_shapes=[pltpu.VMEM((tm, tn), F32)] if nk > 1 else [],
        compiler_params=_cp("parallel", "parallel", "arbitrary"))(*args)


def _mm_res_ln(a, w, res, g, b, *, name, tm=512, tk=1408):
    (M, K), (_, N) = a.shape, w.shape
    tm, tk = _pick(M, tm, 8), _pick(K, tk, LANES)
    nk = K // tk

    def body(a_ref, w_ref, r_ref, g_ref, b_ref, h_ref, s_ref, acc_ref):
        k = pl.program_id(1)
        p = _dot(a_ref[...].astype(BF16), w_ref[...].astype(BF16), _NN)

        @pl.when(k == 0)
        def _():
            acc_ref[...] = p

        @pl.when(k > 0)
        def _():
            acc_ref[...] += p

        @pl.when(k == nk - 1)
        def _():
            s = DN_ALPHA * r_ref[...] + acc_ref[...]
            xhat, _ = _ln_stats(s)
            s_ref[...] = s
            h_ref[...] = xhat * g_ref[...] + b_ref[...]

    row = pl.BlockSpec((tm, N), lambda i, k: (i, 0))
    vec = pl.BlockSpec((1, N), lambda i, k: (0, 0))
    return pl.pallas_call(
        body, name=name, out_shape=(_sds((M, N), F32), _sds((M, N), F32)), grid=(M // tm, nk),
        in_specs=[pl.BlockSpec((tm, tk), lambda i, k: (i, k)), pl.BlockSpec((tk, N), lambda i, k: (k, 0)), row, vec, vec],
        out_specs=(row, row), scratch_shapes=[pltpu.VMEM((tm, N), F32)],
        compiler_params=_cp("parallel", "arbitrary"))(a, w, res, g, b)


def _ln_fwd(x, g, b, *, name):
    T, D = x.shape
    tt = _tile(T, 512)

    def body(x_ref, g_ref, b_ref, o_ref):
        xhat, _ = _ln_stats(x_ref[...])
        o_ref[...] = xhat * g_ref[...] + b_ref[...]

    row = pl.BlockSpec((tt, D), lambda i: (i, 0))
    vec = pl.BlockSpec((1, D), lambda i: (0, 0))
    return pl.pallas_call(body, name=name, out_shape=_sds((T, D), F32), grid=(T // tt,),
                          in_specs=[row, vec, vec], out_specs=row, compiler_params=_cp("parallel"))(x, g, b)


def _ln_bwd(s, dy, g, b, *, silu, name):
    T, D = s.shape
    tt = _tile(T, 512)

    def body(s_ref, dy_ref, g_ref, b_ref, ds_ref, dg_ref, db_ref):
        i = pl.program_id(0)
        xhat, rstd = _ln_stats(s_ref[...])
        gg = g_ref[...]
        dyl = dy_ref[...]
        if silu:
            _, dsilu = _silu_and_grad(xhat * gg + b_ref[...])
            dyl = dyl * dsilu
        dxh = dyl * gg
        m1 = jnp.mean(dxh, axis=-1, keepdims=True)
        m2 = jnp.mean(dxh * xhat, axis=-1, keepdims=True)
        ds_ref[...] = rstd * (dxh - m1 - xhat * m2)

        @pl.when(i == 0)
        def _():
            dg_ref[...] = jnp.zeros_like(dg_ref)
            db_ref[...] = jnp.zeros_like(db_ref)

        dg_ref[...] += jnp.sum(dyl * xhat, axis=0, keepdims=True)
        db_ref[...] += jnp.sum(dyl, axis=0, keepdims=True)

    row = pl.BlockSpec((tt, D), lambda i: (i, 0))
    vec = pl.BlockSpec((1, D), lambda i: (0, 0))
    return pl.pallas_call(body, name=name, out_shape=(_sds((T, D), F32), _sds((1, D), F32), _sds((1, D), F32)),
                          grid=(T // tt,), in_specs=[row, row, vec, vec], out_specs=(row, vec, vec),
                          compiler_params=_cp("arbitrary"))(s, dy, g, b)


def _loss_fwd_bwd(h, tgt, *, name):
    T, D = h.shape
    tt = _tile(T, 512)

    def body(h_ref, t_ref, dh_ref, l_ref):
        i = pl.program_id(0)
        e = h_ref[...] - t_ref[...]
        dh_ref[...] = e * (1.0 / D)

        @pl.when(i == 0)
        def _():
            l_ref[...] = jnp.zeros_like(l_ref)

        part = jnp.sum(jnp.sum(e * e, axis=1, keepdims=True), axis=0, keepdims=True) * (0.5 / D)
        l_ref[...] += jnp.broadcast_to(part, l_ref.shape)

    row = pl.BlockSpec((tt, D), lambda i: (i, 0))
    one = pl.BlockSpec((8, LANES), lambda i: (0, 0))
    dh, l = pl.pallas_call(body, name=name, out_shape=(_sds((T, D), F32), _sds((8, LANES), F32)), grid=(T // tt,),
                           in_specs=[row, row], out_specs=(row, one), compiler_params=_cp("arbitrary"))(h, tgt)
    return l[0:1, 0:1], dh


def _halo_rows(k):
    return 32 if k > 9 else 8


def _shifted(ext, K, first):
    rolled = {0: ext}
    out = []
    for k in range(K):
        r = (first + k) % 8
        if r not in rolled:
            rolled[r] = pltpu.roll(ext, ext.shape[0] - r, 0)
        out.append((rolled[r], first + k - r))
    return out


def _taps(ext, w_ref, sl, K, first, n, reverse=False):
    acc = None
    for k, (z, base) in enumerate(_shifted(ext, K, first)):
        kw = K - 1 - k if reverse else k
        term = w_ref[kw:kw + 1, sl] * z[base:base + n]
        acc = term if acc is None else acc + term
    return acc


def _tap_sums(d, ext, dw_ref, sl, K, first):
    n = d.shape[0]
    for k, (z, base) in enumerate(_shifted(ext, K, first)):
        dw_ref[k:k + 1, sl] += jnp.sum(d * z[base:base + n], axis=0, keepdims=True)


def _strip_width(width):
    return LANES if width % LANES == 0 else width


def _for_strips(width, fn):
    sw = _strip_width(width)

    def step(s, carry):
        fn(pl.ds(pl.multiple_of(s * sw, sw), sw), s)
        return carry

    lax.fori_loop(0, width // sw, step, 0)


def _prev_rows(tt, hb):
    return lambda t: jnp.maximum(t * (tt // hb) - 1, 0)


def _next_rows(tt, hb, T):
    return lambda t: jnp.minimum((t + 1) * (tt // hb), T // hb - 1)


def _conf_fwd(u, w, b, g, beta, *, D, name):
    T = u.shape[0]
    K = w.shape[0]
    tt, hb, tc = _tile(T, CONV_ROWS), _halo_rows(K), _tile(D, GLU_TILE)
    sw = _strip_width(tc)
    per_tile = tc // sw
    prev = _prev_rows(tt, hb)

    def body(u_ref, uh_ref, w_ref, b_ref, g_ref, beta_ref, v1_ref, v3_ref):
        i = pl.program_id(0)

        def strip(sl, s):
            a0 = (s // per_tile) * (2 * tc) + (s % per_tile) * sw
            a_sl, g_sl = pl.ds(pl.multiple_of(a0, sw), sw), pl.ds(pl.multiple_of(a0 + tc, sw), sw)
            halo = jnp.where(i > 0, uh_ref[:, a_sl] * _sigmoid(uh_ref[:, g_sl]), 0.0)
            ext = jnp.concatenate([halo, u_ref[:, a_sl] * _sigmoid(u_ref[:, g_sl])], axis=0)
            v1_ref[:, sl] = _taps(ext, w_ref, sl, K, hb - (K - 1), tt) + b_ref[:, sl]

        _for_strips(D, strip)
        xhat, _ = _ln_stats(v1_ref[...])
        v2 = xhat * g_ref[...] + beta_ref[...]
        v3_ref[...] = (v2 * _sigmoid(v2)).astype(BF16)

    row = pl.BlockSpec((tt, D), lambda t: (t, 0))
    vec = pl.BlockSpec((1, D), lambda t: (0, 0))
    return pl.pallas_call(
        body, name=name, out_shape=(_sds((T, D), F32), _sds((T, D), BF16)), grid=(T // tt,),
        in_specs=[pl.BlockSpec((tt, 2 * D), lambda t: (t, 0)), pl.BlockSpec((hb, 2 * D), lambda t: (prev(t), 0)),
                  pl.BlockSpec((K, D), lambda t: (0, 0)), vec, vec, vec],
        out_specs=(row, row), compiler_params=_cp("parallel"))(u, u, w, b, g, beta)


def _conv_act_fwd(x, off, w, b, *, ffn, name):
    T = x.shape[0]
    K, Cw = w.shape
    tt, hb = _tile(T, CONV_ROWS), _halo_rows(K)
    tc = _tile(Cw // 2, FFN_TILE) if ffn else _pick(math.gcd(Cw, off), 1536, LANES)
    xw = 2 * tc if ffn else tc
    assert off % xw == 0
    ob = off // xw
    prev = _prev_rows(tt, hb)

    def body(x_ref, xh_ref, w_ref, b_ref, o_ref):
        first = pl.program_id(0) == 0

        def pre_of(sl):
            ext = jnp.concatenate([jnp.where(first, 0.0, xh_ref[:, sl]), x_ref[:, sl]], axis=0)
            return _taps(ext, w_ref, sl, K, hb - (K - 1), tt) + b_ref[:, sl]

        def strip(sl, s):
            if ffn:
                gate = pre_of(sl)
                val = pre_of(pl.ds(pl.multiple_of(tc + s * sw, sw), sw))
                o_ref[:, sl] = (gate * _sigmoid(gate) * val).astype(BF16)
            else:
                pre = pre_of(sl)
                o_ref[:, sl] = pre * _sigmoid(pre)

        _for_strips(tc, strip)

    sw = _strip_width(tc)
    return pl.pallas_call(
        body, name=name, out_shape=_sds((T, Cw // 2), BF16) if ffn else _sds((T, Cw), F32), grid=(T // tt, Cw // xw),
        in_specs=[pl.BlockSpec((tt, xw), lambda t, c: (t, c + ob)), pl.BlockSpec((hb, xw), lambda t, c: (prev(t), c + ob)),
                  pl.BlockSpec((K, xw), lambda t, c: (0, c)), pl.BlockSpec((1, xw), lambda t, c: (0, c))],
        out_specs=pl.BlockSpec((tt, tc), lambda t, c: (t, c)),
        compiler_params=_cp("parallel", "parallel"))(x, x, w, b)


def _conv_act_bwd(x, off, w, b, dout, *, ffn, name):
    T = x.shape[0]
    K, Cw = w.shape
    tt, hb = _tile(T, CONV_ROWS), _halo_rows(K)
    tc = _tile(Cw // 2, FFN_TILE) if ffn else _pick(math.gcd(Cw, off), 1536, LANES)
    xw = 2 * tc if ffn else tc
    assert off % xw == 0
    ob = off // xw
    nt = T // tt
    prev, nxt = _prev_rows(tt, hb), _next_rows(tt, hb, T)

    sw = _strip_width(tc)

    def body(x_ref, xp_ref, xn_ref, d_ref, dn_ref, w_ref, b_ref, dx_ref, dw_ref, db_ref):
        t = pl.program_id(1)

        @pl.when(t == 0)
        def _():
            dw_ref[...] = jnp.zeros_like(dw_ref)
            db_ref[...] = jnp.zeros_like(db_ref)

        def ext_of(sl):
            return jnp.concatenate([jnp.where(t == 0, 0.0, xp_ref[:, sl]), x_ref[:, sl], xn_ref[:, sl]], axis=0)

        def pre_of(ext, sl):
            return _taps(ext, w_ref, sl, K, hb - (K - 1), tt + hb) + b_ref[:, sl]

        def finish(ext, dpre, sl):
            dx_ref[:, sl] = _taps(dpre, w_ref, sl, K, 0, tt, reverse=True).astype(BF16)
            dp = dpre[0:tt]
            _tap_sums(dp, ext, dw_ref, sl, K, hb - (K - 1))
            db_ref[:, sl] += jnp.sum(dp, axis=0, keepdims=True)

        def strip(sl, s):
            d = jnp.concatenate([d_ref[:, sl], jnp.where(t == nt - 1, 0.0, dn_ref[:, sl])], axis=0)
            if ffn:
                vsl = pl.ds(pl.multiple_of(tc + s * sw, sw), sw)
                eg, ev = ext_of(sl), ext_of(vsl)
                sg, dsg = _silu_and_grad(pre_of(eg, sl))
                val = pre_of(ev, vsl)
                finish(eg, d * val * dsg, sl)
                finish(ev, d * sg, vsl)
            else:
                ext = ext_of(sl)
                finish(ext, d * _silu_and_grad(pre_of(ext, sl))[1], sl)

        _for_strips(tc, strip)

    return pl.pallas_call(
        body, name=name, out_shape=(_sds((T, Cw), BF16), _sds((K, Cw), F32), _sds((1, Cw), F32)),
        grid=(Cw // xw, nt),
        in_specs=[pl.BlockSpec((tt, xw), lambda c, t: (t, c + ob)), pl.BlockSpec((hb, xw), lambda c, t: (prev(t), c + ob)),
                  pl.BlockSpec((hb, xw), lambda c, t: (nxt(t), c + ob)),
                  pl.BlockSpec((tt, tc), lambda c, t: (t, c)), pl.BlockSpec((hb, tc), lambda c, t: (nxt(t), c)),
                  pl.BlockSpec((K, xw), lambda c, t: (0, c)), pl.BlockSpec((1, xw), lambda c, t: (0, c))],
        out_specs=(pl.BlockSpec((tt, xw), lambda c, t: (t, c)), pl.BlockSpec((K, xw), lambda c, t: (0, c)),
                   pl.BlockSpec((1, xw), lambda c, t: (0, c))),
        compiler_params=_cp("parallel", "arbitrary"))(x, x, x, dout, dout, w, b)


def _glu_conv_bwd(dpre, u, w, *, name):
    T, C = dpre.shape
    K = w.shape[0]
    tt, hb, tc = _tile(T, CONV_ROWS), _halo_rows(K), _tile(C, GLU_TILE)
    nt = T // tt
    prev, nxt = _prev_rows(tt, hb), _next_rows(tt, hb, T)

    sw = _strip_width(tc)

    def body(d_ref, dn_ref, x_ref, xh_ref, w_ref, dx_ref, dw_ref, db_ref):
        t = pl.program_id(1)

        @pl.when(t == 0)
        def _():
            dw_ref[...] = jnp.zeros_like(dw_ref)
            db_ref[...] = jnp.zeros_like(db_ref)

        def strip(sl, s):
            gsl = pl.ds(pl.multiple_of(tc + s * sw, sw), sw)
            d = d_ref[:, sl]
            dext = jnp.concatenate([d, jnp.where(t == nt - 1, 0.0, dn_ref[:, sl])], axis=0)
            dxin = _taps(dext, w_ref, sl, K, 0, tt, reverse=True)
            a, sg = x_ref[:, sl], _sigmoid(x_ref[:, gsl])
            v0 = a * sg
            dx_ref[:, sl] = (dxin * sg).astype(BF16)
            dx_ref[:, gsl] = (dxin * v0 * (1.0 - sg)).astype(BF16)
            halo = jnp.where(t == 0, 0.0, xh_ref[:, sl] * _sigmoid(xh_ref[:, gsl]))
            _tap_sums(d, jnp.concatenate([halo, v0], axis=0), dw_ref, sl, K, hb - (K - 1))
            db_ref[:, sl] += jnp.sum(d, axis=0, keepdims=True)

        _for_strips(tc, strip)

    return pl.pallas_call(
        body, name=name, out_shape=(_sds((T, 2 * C), BF16), _sds((K, C), F32), _sds((1, C), F32)), grid=(C // tc, nt),
        in_specs=[pl.BlockSpec((tt, tc), lambda c, t: (t, c)), pl.BlockSpec((hb, tc), lambda c, t: (nxt(t), c)),
                  pl.BlockSpec((tt, 2 * tc), lambda c, t: (t, c)), pl.BlockSpec((hb, 2 * tc), lambda c, t: (prev(t), c)),
                  pl.BlockSpec((K, tc), lambda c, t: (0, c))],
        out_specs=(pl.BlockSpec((tt, 2 * tc), lambda c, t: (t, c)), pl.BlockSpec((K, tc), lambda c, t: (0, c)),
                   pl.BlockSpec((1, tc), lambda c, t: (0, c))),
        compiler_params=_cp("parallel", "arbitrary"))(dpre, dpre, u, u, w)


def _softplus(x):
    t = jnp.exp(-jnp.abs(x))
    u = 1.0 + t
    log1p = jnp.where(u == 1.0, t, jnp.log(u) * t / jnp.where(u == 1.0, 1.0, u - 1.0))
    return jnp.maximum(x, 0.0) + log1p


def _dt_fwd(udt, bias, *, H, name):
    T = udt.shape[0]
    tt = _tile(T, 1024)

    def body(u_ref, b_ref, o_ref):
        lane = lax.broadcasted_iota(jnp.int32, (tt, LANES), 1)
        o_ref[...] = jnp.where(lane < H, _softplus(u_ref[...] + b_ref[...]), 0.0)

    row = pl.BlockSpec((tt, LANES), lambda i: (i, 0))
    return pl.pallas_call(body, name=name, out_shape=_sds((T, LANES), F32), grid=(T // tt,),
                          in_specs=[row, pl.BlockSpec((1, LANES), lambda i: (0, 0))], out_specs=row,
                          compiler_params=_cp("parallel"))(udt, bias)


def _dt_bwd(da, ddtx, dt, udt, bias, a_col, *, H, name):
    T = da.shape[0]
    tt = _tile(T, 1024)

    def body(da_ref, dx_ref, dt_ref, u_ref, b_ref, a_ref, draw_ref, dbias_ref, dalog_ref):
        i = pl.program_id(0)
        lane = lax.broadcasted_iota(jnp.int32, (tt, LANES), 1)
        dav = da_ref[...]
        ddt = dav * a_ref[...] + dx_ref[...]
        draw = jnp.where(lane < H, ddt * _sigmoid(u_ref[...] + b_ref[...]), 0.0)
        draw_ref[...] = draw.astype(BF16)

        @pl.when(i == 0)
        def _():
            dbias_ref[...] = jnp.zeros_like(dbias_ref)
            dalog_ref[...] = jnp.zeros_like(dalog_ref)

        dbias_ref[...] += jnp.sum(draw, axis=0, keepdims=True)
        dalog_ref[...] += jnp.sum(dav * dt_ref[...], axis=0, keepdims=True) * a_ref[...]

    row = pl.BlockSpec((tt, LANES), lambda i: (i, 0))
    vec = pl.BlockSpec((1, LANES), lambda i: (0, 0))
    return pl.pallas_call(body, name=name,
                          out_shape=(_sds((T, LANES), BF16), _sds((1, LANES), F32), _sds((1, LANES), F32)),
                          grid=(T // tt,), in_specs=[row, row, row, row, vec, vec], out_specs=(row, vec, vec),
                          compiler_params=_cp("arbitrary"))(da, ddtx, dt, udt, bias, a_col)


def _ssd_consts():
    L = SSM_CHUNK
    r = lax.broadcasted_iota(jnp.int32, (L, L), 0)
    c = lax.broadcasted_iota(jnp.int32, (L, L), 1)
    return r, c


def _ssd_chunk_decays(dtc_ref, dtr_ref, acol_ref, arow_ref, cs_ref, csr_ref, r, c):
    L = SSM_CHUNK
    dtc = dtc_ref[...]
    tril = (r >= c).astype(F32)
    triu = (r <= c).astype(F32)
    cs_ref[...] = jnp.dot(tril, dtc * acol_ref[...], precision=HIGHEST, preferred_element_type=F32)
    csr_ref[...] = jnp.dot(dtr_ref[...] * arow_ref[...], triu, precision=HIGHEST, preferred_element_type=F32)
    cs = cs_ref[...]
    cs_last = cs_ref[L - 1:L, :]
    return dtc, cs, jnp.exp(cs), jnp.exp(cs_last - cs), jnp.exp(cs_last), triu


def _ssd_fwd(xbc, dt, dtT, a_col, a_row, d_col, *, H, name):
    T, XBC = xbc.shape
    L, N, G, P = SSM_CHUNK, SSM_STATE, SSM_GROUPS, SSM_HEAD_DIM
    SX = H * P
    HR = dtT.shape[0]
    nc = T // L
    heads_per_group = H // G

    def body(x_ref, dtc_ref, dtr_ref, acol_ref, arow_ref, d_ref, y_ref, hst_ref, state, cs_ref, csr_ref):
        ci = pl.program_id(0)

        @pl.when(ci == 0)
        def _():
            state[...] = jnp.zeros_like(state)

        hst_ref[0] = state[...]
        r, c = _ssd_consts()
        tri = r >= c
        lane_lo = c < P
        row_lo = r < P
        lane1_lo = lax.broadcasted_iota(jnp.int32, (1, LANES), 1) < P
        dtc, cs, e, ds, cd, _ = _ssd_chunk_decays(dtc_ref, dtr_ref, acol_ref, arow_ref, cs_ref, csr_ref, r, c)
        dsk = d_ref[...]

        def sel(arr, h0):
            return jnp.where(lane_lo, arr[:, h0:h0 + 1], arr[:, h0 + 1:h0 + 2])

        for g in range(G):
            Bg = x_ref[:, SX + g * N:SX + (g + 1) * N].astype(BF16)
            Cg = x_ref[:, SX + G * N + g * N:SX + G * N + (g + 1) * N].astype(BF16)
            Gm = _dot(Cg, Bg, _NT)
            for j in range(g * heads_per_group // 2, (g + 1) * heads_per_group // 2):
                h0 = 2 * j
                sl = slice(2 * P * j, 2 * P * (j + 1))
                x2 = x_ref[:, sl]
                X2 = x2 * sel(dtc, h0)
                X2b = X2.astype(BF16)
                ys = []
                for h in (h0, h0 + 1):
                    seg = cs[:, h:h + 1] - csr_ref[h:h + 1, :]
                    Dm = jnp.where(tri, jnp.exp(jnp.where(tri, seg, 0.0)), 0.0)
                    ys.append(_dot((Gm * Dm).astype(BF16), X2b, _NN))
                H2 = state[sl, :]
                R2 = _dot(Cg, H2.astype(BF16), _NT)
                dsk2 = jnp.where(lane1_lo, dsk[:, h0:h0 + 1], dsk[:, h0 + 1:h0 + 2])
                y_ref[:, sl] = jnp.where(lane_lo, ys[0], ys[1]) + sel(e, h0) * R2 + x2 * dsk2
                S2 = _dot((X2 * sel(ds, h0)).astype(BF16), Bg, _TN)
                state[sl, :] = jnp.where(row_lo, cd[:, h0:h0 + 1], cd[:, h0 + 1:h0 + 2]) * H2 + S2

    vec = pl.BlockSpec((1, LANES), lambda i: (0, 0))
    return pl.pallas_call(
        body, name=name, out_shape=(_sds((T, SX), F32), _sds((nc, SX, N), F32)), grid=(nc,),
        in_specs=[pl.BlockSpec((L, XBC), lambda i: (i, 0)), pl.BlockSpec((L, LANES), lambda i: (i, 0)),
                  pl.BlockSpec((HR, L), lambda i: (0, i)), vec, pl.BlockSpec((HR, 1), lambda i: (0, 0)), vec],
        out_specs=(pl.BlockSpec((L, SX), lambda i: (i, 0)), pl.BlockSpec((1, SX, N), lambda i: (i, 0, 0))),
        scratch_shapes=[pltpu.VMEM((SX, N), F32), pltpu.VMEM((L, LANES), F32), pltpu.VMEM((HR, L), F32)],
        compiler_params=_cp("arbitrary"))(xbc, dt, dtT, a_col, a_row, d_col)


def _ssd_bwd(xbc, dt, dtT, a_col, a_row, d_col, hst, dy, *, H, name):
    T, XBC = xbc.shape
    L, N, G, P = SSM_CHUNK, SSM_STATE, SSM_GROUPS, SSM_HEAD_DIM
    SX = H * P
    HR = dtT.shape[0]
    nc = T // L
    heads_per_group = H // G

    def body(x_ref, dtc_ref, dtr_ref, acol_ref, arow_ref, d_ref, hst_ref, dy_ref,
             dx_ref, da_ref, ddtx_ref, dd_ref, dstate, cs_ref, csr_ref):
        ci = pl.program_id(0)

        @pl.when(ci == 0)
        def _():
            dstate[...] = jnp.zeros_like(dstate)
            dd_ref[...] = jnp.zeros_like(dd_ref)

        r, c = _ssd_consts()
        tri = r >= c
        lane_lo = c < P
        row_lo = r < P
        lane1 = lax.broadcasted_iota(jnp.int32, (1, LANES), 1)
        rowc = lax.broadcasted_iota(jnp.int32, (L, 1), 0)
        dtc, cs, e, ds, cd, triu = _ssd_chunk_decays(dtc_ref, dtr_ref, acol_ref, arow_ref, cs_ref, csr_ref, r, c)
        triu_b = triu.astype(BF16)
        dsk = d_ref[...]
        last_row = rowc == L - 1

        def sel(arr, h0):
            return jnp.where(lane_lo, arr[:, h0:h0 + 1], arr[:, h0 + 1:h0 + 2])

        def half(mask_lo, lo):
            return mask_lo if lo else jnp.logical_not(mask_lo)

        def lsum(v, lo):
            return jnp.sum(jnp.where(half(lane_lo, lo), v, 0.0), axis=1, keepdims=True)

        dcs_acc = jnp.zeros((L, LANES), F32)
        dadir_acc = jnp.zeros((L, LANES), F32)
        ddtx_acc = jnp.zeros((L, LANES), F32)
        dd_acc = jnp.zeros((1, LANES), F32)
        for g in range(G):
            bsl = slice(SX + g * N, SX + (g + 1) * N)
            csl = slice(SX + G * N + g * N, SX + G * N + (g + 1) * N)
            Bg = x_ref[:, bsl].astype(BF16)
            Cg = x_ref[:, csl].astype(BF16)
            Gm = _dot(Cg, Bg, _NT)
            dG = jnp.zeros((L, L), F32)
            dBg = jnp.zeros((L, N), F32)
            dCg = jnp.zeros((L, N), F32)
            for j in range(g * heads_per_group // 2, (g + 1) * heads_per_group // 2):
                h0 = 2 * j
                sl = slice(2 * P * j, 2 * P * (j + 1))
                x2 = x_ref[:, sl]
                dy2 = dy_ref[:, sl]
                dt2, e2, ds2 = sel(dtc, h0), sel(e, h0), sel(ds, h0)
                X2 = x2 * dt2
                X2b = X2.astype(BF16)
                H2 = hst_ref[0, sl, :]
                H2b = H2.astype(BF16)
                dHn = dstate[sl, :]
                dHnb = dHn.astype(BF16)
                R2 = _dot(Cg, H2b, _NT)
                dR2b = (e2 * dy2).astype(BF16)
                de2 = dy2 * R2
                dCg = dCg + _dot(dR2b, H2b, _NN)
                dHr = _dot(dR2b, Cg, _TN)
                rs = jnp.sum(dHn * H2, axis=1, keepdims=True)
                dXd = _dot(Bg, dHnb, _NT)
                dBg = dBg + _dot((X2 * ds2).astype(BF16), dHnb, _NN)
                dds2 = dXd * X2
                dX2 = ds2 * dXd
                for lo, h in ((True, h0), (False, h0 + 1)):
                    ds_h, cd_h = ds[:, h:h + 1], cd[:, h:h + 1]
                    dcd = jnp.sum(jnp.where(half(rowc < P, lo), rs, 0.0), axis=0, keepdims=True)
                    t1 = lsum(dds2, lo) * ds_h
                    tail = jnp.sum(t1, axis=0, keepdims=True) + dcd * cd_h
                    dcs_h = lsum(de2, lo) * e[:, h:h + 1] - t1 + jnp.where(last_row, tail, 0.0)
                    seg = cs[:, h:h + 1] - csr_ref[h:h + 1, :]
                    Dm = jnp.where(tri, jnp.exp(jnp.where(tri, seg, 0.0)), 0.0)
                    Mh = Gm * Dm
                    dyh = jnp.where(half(lane_lo, lo), dy2, 0.0).astype(BF16)
                    dM = _dot(dyh, X2b, _NT)
                    dG = dG + dM * Dm
                    dX2 = dX2 + _dot(Mh.astype(BF16), dyh, _TN)
                    W = _dot(triu_b, (dM * Mh).astype(BF16), _NN)
                    da_q = jnp.sum(jnp.where(c < r, W, 0.0), axis=1, keepdims=True)
                    oh = (lane1 == h).astype(F32)
                    dcs_acc = dcs_acc + dcs_h * oh
                    dadir_acc = dadir_acc + da_q * oh
                dxx = dX2 * x2
                q = jnp.sum(dy2 * x2, axis=0, keepdims=True)
                for lo, h in ((True, h0), (False, h0 + 1)):
                    oh = (lane1 == h).astype(F32)
                    ddtx_acc = ddtx_acc + lsum(dxx, lo) * oh
                    dd_acc = dd_acc + jnp.sum(jnp.where(half(lane1 < P, lo), q, 0.0), axis=1, keepdims=True) * oh
                dsk2 = jnp.where(lane1 < P, dsk[:, h0:h0 + 1], dsk[:, h0 + 1:h0 + 2])
                dx_ref[:, sl] = dX2 * dt2 + dy2 * dsk2
                dstate[sl, :] = jnp.where(row_lo, cd[:, h0:h0 + 1], cd[:, h0 + 1:h0 + 2]) * dHn + dHr
            dGb = dG.astype(BF16)
            dx_ref[:, bsl] = dBg + _dot(dGb, Cg, _TN)
            dx_ref[:, csl] = dCg + _dot(dGb, Bg, _NN)
        da_ref[...] = jnp.dot(triu, dcs_acc, precision=HIGHEST, preferred_element_type=F32) + dadir_acc
        ddtx_ref[...] = ddtx_acc
        dd_ref[...] += dd_acc

    vec = pl.BlockSpec((1, LANES), lambda i: (0, 0))
    rev = lambda i: (nc - 1 - i, 0)
    return pl.pallas_call(
        body, name=name,
        out_shape=(_sds((T, XBC), F32), _sds((T, LANES), F32), _sds((T, LANES), F32), _sds((1, LANES), F32)),
        grid=(nc,),
        in_specs=[pl.BlockSpec((L, XBC), rev), pl.BlockSpec((L, LANES), rev),
                  pl.BlockSpec((HR, L), lambda i: (0, nc - 1 - i)), vec, pl.BlockSpec((HR, 1), lambda i: (0, 0)), vec,
                  pl.BlockSpec((1, SX, N), lambda i: (nc - 1 - i, 0, 0)), pl.BlockSpec((L, SX), rev)],
        out_specs=(pl.BlockSpec((L, XBC), rev), pl.BlockSpec((L, LANES), rev), pl.BlockSpec((L, LANES), rev), vec),
        scratch_shapes=[pltpu.VMEM((SX, N), F32), pltpu.VMEM((L, LANES), F32), pltpu.VMEM((HR, L), F32)],
        compiler_params=_cp("arbitrary"))(xbc, dt, dtT, a_col, a_row, d_col, hst, dy)


def _rms_fwd(y, u, zoff, w, *, name):
    T, SX = y.shape
    tt = _tile(T, 256)
    gs = SX // SSM_GROUPS
    zb = zoff // SX

    def body(y_ref, z_ref, w_ref, o_ref):
        for g in range(SSM_GROUPS):
            sl = slice(g * gs, (g + 1) * gs)
            z = z_ref[:, sl]
            yg = y_ref[:, sl] * (z * _sigmoid(z))
            rstd = lax.rsqrt(jnp.mean(yg * yg, axis=-1, keepdims=True) + RMS_EPS)
            o_ref[:, sl] = (yg * rstd * w_ref[:, sl]).astype(BF16)

    row = pl.BlockSpec((tt, SX), lambda i: (i, 0))
    return pl.pallas_call(body, name=name, out_shape=_sds((T, SX), BF16), grid=(T // tt,),
                          in_specs=[row, pl.BlockSpec((tt, SX), lambda i: (i, zb)), pl.BlockSpec((1, SX), lambda i: (0, 0))],
                          out_specs=row, compiler_params=_cp("parallel"))(y, u, w)


def _rms_bwd(y, u, zoff, w, dyn, *, name):
    T, SX = y.shape
    tt = _tile(T, 256)
    gs = SX // SSM_GROUPS
    zb = zoff // SX

    def body(y_ref, z_ref, w_ref, d_ref, dy_ref, dz_ref, dw_ref):
        i = pl.program_id(0)

        @pl.when(i == 0)
        def _():
            dw_ref[...] = jnp.zeros_like(dw_ref)

        for g in range(SSM_GROUPS):
            sl = slice(g * gs, (g + 1) * gs)
            z, yv, d = z_ref[:, sl], y_ref[:, sl], d_ref[:, sl]
            sz, dsz = _silu_and_grad(z)
            yg = yv * sz
            rstd = lax.rsqrt(jnp.mean(yg * yg, axis=-1, keepdims=True) + RMS_EPS)
            t = yg * rstd
            dw_ref[:, sl] += jnp.sum(d * t, axis=0, keepdims=True)
            dt_ = d * w_ref[:, sl]
            dyg = rstd * (dt_ - t * jnp.mean(dt_ * t, axis=-1, keepdims=True))
            dy_ref[:, sl] = dyg * sz
            dz_ref[:, sl] = (dyg * yv * dsz).astype(BF16)

    row = pl.BlockSpec((tt, SX), lambda i: (i, 0))
    vec = pl.BlockSpec((1, SX), lambda i: (0, 0))
    return pl.pallas_call(body, name=name, out_shape=(_sds((T, SX), F32), _sds((T, SX), BF16), _sds((1, SX), F32)),
                          grid=(T // tt,), in_specs=[row, pl.BlockSpec((tt, SX), lambda i: (i, zb)), vec, row],
                          out_specs=(row, row, vec), compiler_params=_cp("arbitrary"))(y, u, w, dyn)


def _gate_fwd(u, goff, ya, yb, *, name):
    T, D = ya.shape
    tt = _tile(T, 512)
    gb = goff // D

    def body(ga_ref, gb_ref, ya_ref, yb_ref, o_ref):
        o_ref[...] = (_sigmoid(ga_ref[...]) * ya_ref[...] + _sigmoid(gb_ref[...]) * yb_ref[...]).astype(BF16)

    row = pl.BlockSpec((tt, D), lambda i: (i, 0))
    return pl.pallas_call(body, name=name, out_shape=_sds((T, D), BF16), grid=(T // tt,),
                          in_specs=[pl.BlockSpec((tt, D), lambda i: (i, gb)), pl.BlockSpec((tt, D), lambda i: (i, gb + 1)), row, row],
                          out_specs=row, compiler_params=_cp("parallel"))(u, u, ya, yb)


def _gate_bwd(u, goff, ya, yb, dm, *, name):
    T, D = ya.shape
    tt = _tile(T, 512)
    gb = goff // D

    def body(ga_ref, gb_ref, ya_ref, yb_ref, dm_ref, dya_ref, dyb_ref, dg_ref):
        d = dm_ref[...]
        sa, sb = _sigmoid(ga_ref[...]), _sigmoid(gb_ref[...])
        dya_ref[...] = (d * sa).astype(BF16)
        dyb_ref[...] = (d * sb).astype(BF16)
        dg_ref[...] = jnp.concatenate([d * ya_ref[...] * sa * (1.0 - sa), d * yb_ref[...] * sb * (1.0 - sb)], axis=1).astype(BF16)

    row = pl.BlockSpec((tt, D), lambda i: (i, 0))
    return pl.pallas_call(body, name=name, out_shape=(_sds((T, D), BF16), _sds((T, D), BF16), _sds((T, 2 * D), BF16)),
                          grid=(T // tt,),
                          in_specs=[pl.BlockSpec((tt, D), lambda i: (i, gb)), pl.BlockSpec((tt, D), lambda i: (i, gb + 1)), row, row, row],
                          out_specs=(row, row, pl.BlockSpec((tt, 2 * D), lambda i: (i, 0))),
                          compiler_params=_cp("parallel"))(u, u, ya, yb, dm)


def _adamw_math(w, gg, m, v):
    nm = ADAM_B1 * m + (1.0 - ADAM_B1) * gg
    nv = ADAM_B2 * v + (1.0 - ADAM_B2) * (gg * gg)
    m_hat = nm / (1.0 - ADAM_B1 ** ADAM_STEP)
    v_hat = nv / (1.0 - ADAM_B2 ** ADAM_STEP)
    return -ADAM_LR * (m_hat / (jnp.sqrt(v_hat) + ADAM_EPS) + ADAM_WD * w), nm, nv


def _adamw(w, g, m, v, *, name):
    R, C = w.shape
    tr = _pick(R, 256, 8)

    def body(w_ref, g_ref, m_ref, v_ref, d_ref, nm_ref, nv_ref):
        d_ref[...], nm_ref[...], nv_ref[...] = _adamw_math(w_ref[...], g_ref[...], m_ref[...], v_ref[...])

    blk = pl.BlockSpec((tr, C), lambda i: (i, 0))
    out = _sds((R, C), F32)
    return pl.pallas_call(body, name=name, out_shape=(out, out, out), grid=(R // tr,), in_specs=[blk] * 4,
                          out_specs=(blk,) * 3, compiler_params=_cp("parallel"))(w, g, m, v)


def _sum_adamw(parts, w, m, v, *, name):
    shape = w.shape
    C = shape[-1]
    R = w.size // C
    tr = _pick(R, 256, 8)

    def body(p_ref, w_ref, m_ref, v_ref, g_ref, d_ref, nm_ref, nv_ref):
        gg = p_ref[0].astype(F32)
        for k in range(1, N_DEV):
            gg = gg + p_ref[k].astype(F32)
        g_ref[...] = gg
        d_ref[...], nm_ref[...], nv_ref[...] = _adamw_math(w_ref[...], gg, m_ref[...], v_ref[...])

    blk = pl.BlockSpec((tr, C), lambda i: (i, 0))
    out = _sds((R, C), F32)
    res = pl.pallas_call(body, name=name, out_shape=(out,) * 4, grid=(R // tr,),
                         in_specs=[pl.BlockSpec((N_DEV, tr, C), lambda i: (0, i, 0)), blk, blk, blk],
                         out_specs=(blk,) * 4, compiler_params=_cp("parallel"))(
        parts.reshape(N_DEV, R, C), w.reshape(R, C), m.reshape(R, C), v.reshape(R, C))
    return tuple(r.reshape(shape) for r in res)


def _sum_slots(x, *, name):
    n, R, C = x.shape
    tr = _tile(R, 512)

    def body(x_ref, o_ref):
        acc = x_ref[0].astype(F32)
        for k in range(1, n):
            acc = acc + x_ref[k].astype(F32)
        o_ref[...] = acc

    return pl.pallas_call(body, name=name, out_shape=_sds((R, C), F32), grid=(R // tr,),
                          in_specs=[pl.BlockSpec((n, tr, C), lambda i: (0, i, 0))],
                          out_specs=pl.BlockSpec((tr, C), lambda i: (i, 0)), compiler_params=_cp("parallel"))(x)


def _exchange(xs, *, scatter, name):
    n = len(xs)

    def body(*refs):
        x_refs, o_refs = refs[:n], refs[n:2 * n]
        send_sems, recv_sems, local_sems = refs[2 * n:]
        mx, my, mc = lax.axis_index("x"), lax.axis_index("y"), lax.axis_index("c")
        me = 4 * mx + 2 * my + mc

        def src(i, d):
            return x_refs[i].at[d] if scatter else x_refs[i]

        locals_ = [pltpu.make_async_copy(src(i, me), o_refs[i].at[me], local_sems.at[i]) for i in range(n)]
        for cp in locals_:
            cp.start()
        sends, recvs = [], []
        for k in range(1, N_DEV):
            px = 1 - mx if k & 4 else mx
            py = 1 - my if k & 2 else my
            pc = 1 - mc if k & 1 else mc
            peer = 4 * px + 2 * py + pc
            for i in range(n):
                common = dict(send_sem=send_sems.at[k - 1, i], recv_sem=recv_sems.at[k - 1, i],
                              device_id=(px, py, pc), device_id_type=pl.DeviceIdType.MESH)
                sends.append(pltpu.make_async_remote_copy(src_ref=src(i, peer), dst_ref=o_refs[i].at[me], **common))
                recvs.append(pltpu.make_async_remote_copy(src_ref=src(i, peer), dst_ref=o_refs[i].at[peer], **common))
        for cp in sends:
            cp.start()
        for cp in recvs:
            cp.wait_recv()
        for cp in sends:
            cp.wait_send()
        for cp in locals_:
            cp.wait()

    any_spec = pl.BlockSpec(memory_space=pl.ANY)
    out_shape = tuple(_sds(x.shape if scatter else (N_DEV,) + x.shape, x.dtype) for x in xs)
    return pl.pallas_call(
        body, name=name, out_shape=out_shape, in_specs=[any_spec] * n, out_specs=(any_spec,) * n,
        scratch_shapes=[pltpu.SemaphoreType.DMA((N_DEV - 1, n)), pltpu.SemaphoreType.DMA((N_DEV - 1, n)),
                        pltpu.SemaphoreType.DMA((n,))])(*xs)


def _to_rows(flat, row_mult):
    n = flat.shape[-1]
    per = row_mult * LANES
    pad = (-n) % per
    flat = jnp.pad(flat, [(0, 0)] * (flat.ndim - 1) + [(0, pad)])
    return flat.reshape(flat.shape[:-1] + ((n + pad) // LANES, LANES))


def _gather_weights(p):
    names = list(SHARDED)
    got = _exchange([p[k].astype(WIRE) if SHARDED[k][1] else p[k] for k in names], scatter=False, name="gather_weights")
    full = {}
    for k, g in zip(names, got):
        axis, shape = SHARDED[k][0], p[k].shape
        g = jnp.moveaxis(g, 0, axis)
        full[k] = g.reshape(shape[:axis] + (N_DEV * shape[axis],) + shape[axis + 1:])
    return full


def _scatter_grads(grads, p):
    names = list(SHARDED)
    parts = []
    for k in names:
        axis, g = SHARDED[k][0], grads[k]
        g = g.reshape(g.shape[:axis] + (N_DEV, p[k].shape[axis]) + g.shape[axis + 1:])
        parts.append(jnp.moveaxis(g, axis, 0).astype(WIRE))
    return dict(zip(names, _exchange(parts, scatter=True, name="scatter_grads")))


def _allreduce_small(vals):
    metas = [(k, v.shape, v.size) for k, v in vals.items()]
    flat = jnp.concatenate([v.reshape(-1) for v in vals.values()])
    got, = _exchange([_to_rows(flat, 8)], scatter=False, name="gather_small_grads")
    summed = _sum_slots(got, name="sum_small_grads").reshape(-1)
    out, off = {}, 0
    for k, shape, n in metas:
        out[k] = summed[off:off + n].reshape(shape)
        off += n
    return out


def _interleave(a, g, tile):
    C = a.shape[-1]
    n = C // tile
    lead = a.shape[:-1]
    return jnp.concatenate([a.reshape(lead + (n, 1, tile)), g.reshape(lead + (n, 1, tile))], axis=-2).reshape(lead + (2 * C,))


def _deinterleave(w, tile):
    C = w.shape[-1] // 2
    n = C // tile
    lead = w.shape[:-1]
    r = w.reshape(lead + (n, 2, tile))
    return r[..., 0, :].reshape(lead + (C,)), r[..., 1, :].reshape(lead + (C,))


def _row(v):
    return v.reshape(1, -1).astype(F32)


def _lanes(v):
    return jnp.pad(v.astype(F32), (0, LANES - v.shape[0])).reshape(1, LANES)


def _layer_weights(p, full, l, dims):
    D, SX, XBC, H, FF = dims
    gt, ft = _tile(D, GLU_TILE), _tile(FF, FFN_TILE)
    w_in = full['w_in'][l]
    o = 0
    seg = {}
    for nm, n in (('a', D), ('g', D), ('z', SX), ('xbc', XBC), ('dt', H), ('ga', D), ('gb', D)):
        seg[nm] = w_in[:, o:o + n]
        o += n
    w_main = jnp.concatenate([_interleave(seg['a'], seg['g'], gt), seg['z'], seg['xbc'], seg['ga'], seg['gb']], axis=1)
    w_dt = jnp.pad(seg['dt'], ((0, 0), (0, LANES - H)))
    up = full['w_ffn_up'][l]
    fw = full['ffn_dw_w'][l]
    fb = p['ffn_dw_b'][l]
    a_head = -jnp.exp(p['ssm_a_log'][l].astype(F32))
    hr = -(-H // 8) * 8
    return dict(
        w_main=w_main, w_dt=w_dt,
        conv_w=full['conv_dw_w'][l], conv_b=_row(p['conv_dw_b'][l]), conv_g=_row(p['conv_ln_g'][l]), conv_beta=_row(p['conv_ln_b'][l]),
        w_conv_out=full['w_conv_out'][l],
        ssm_w=full['ssm_conv_w'][l], ssm_b=_row(p['ssm_conv_b'][l]),
        dt_bias=_lanes(p['ssm_dt_bias'][l]), a_col=_lanes(a_head),
        a_row=jnp.pad(a_head, (0, hr - H)).reshape(hr, 1), d_col=_lanes(p['ssm_d'][l]),
        norm_w=_row(p['ssm_norm_w'][l]), w_ssm_out=full['w_ssm_out'][l], w_o=full['w_o'][l],
        ln1_g=_row(p['ln1_g'][l]), ln1_b=_row(p['ln1_b'][l]),
        w_up=_interleave(up[:, :FF], up[:, FF:], ft),
        ffn_w=_interleave(fw[:, :FF], fw[:, FF:], ft), ffn_b=_row(_interleave(fb[:FF], fb[FF:], ft)),
        w_down=full['w_ffn_down'][l], ln2_g=_row(p['ln2_g'][l]), ln2_b=_row(p['ln2_b'][l]))


def _layer_fwd(h, W, l, dims):
    D, SX, XBC, H, FF = dims
    o_z, o_xbc, o_gate = 2 * D, 2 * D + SX, 2 * D + SX + XBC
    hr = W['a_row'].shape[0]
    t = f"l{l}"
    u = _mm(h, W['w_main'], mode='nn', out_dtype=F32, name=t + "_in_proj")
    udt = _mm(h, W['w_dt'], mode='nn', out_dtype=F32, name=t + "_dt_proj")
    v1, v3 = _conf_fwd(u, W['conv_w'], W['conv_b'], W['conv_g'], W['conv_beta'], D=D, name=t + "_conf_fwd")
    ya = _mm(v3, W['w_conv_out'], mode='nn', out_dtype=F32, name=t + "_conv_out")
    xbc = _conv_act_fwd(u, o_xbc, W['ssm_w'], W['ssm_b'], ffn=False, name=t + "_ssm_conv")
    dt = _dt_fwd(udt, W['dt_bias'], H=H, name=t + "_dt")
    dtT = jnp.pad(dt[:, :H].T, ((0, hr - H), (0, 0)))
    y, hst = _ssd_fwd(xbc, dt, dtT, W['a_col'], W['a_row'], W['d_col'], H=H, name=t + "_ssd_fwd")
    yn = _rms_fwd(y, u, o_z, W['norm_w'], name=t + "_rms_fwd")
    yb = _mm(yn, W['w_ssm_out'], mode='nn', out_dtype=F32, name=t + "_ssm_out")
    m = _gate_fwd(u, o_gate, ya, yb, name=t + "_gate_fwd")
    h1, s1 = _mm_res_ln(m, W['w_o'], h, W['ln1_g'], W['ln1_b'], name=t + "_mix_ln1")
    uf = _mm(h1, W['w_up'], mode='nn', out_dtype=F32, name=t + "_ffn_up")
    act = _conv_act_fwd(uf, 0, W['ffn_w'], W['ffn_b'], ffn=True, name=t + "_ffn_conv")
    h2, s2 = _mm_res_ln(act, W['w_down'], h1, W['ln2_g'], W['ln2_b'], name=t + "_ffn_down_ln2")
    saved = dict(h=h, u=u, udt=udt, v1=v1, v3=v3, ya=ya, xbc=xbc, dt=dt, dtT=dtT, y=y, hst=hst, yn=yn, yb=yb, m=m,
                 h1=h1, s1=s1, uf=uf, act=act, s2=s2)
    return h2, saved


def _layer_bwd(dh2, W, S, l, dims):
    D, SX, XBC, H, FF = dims
    o_z, o_xbc, o_gate = 2 * D, 2 * D + SX, 2 * D + SX + XBC
    t = f"l{l}"
    g = {}
    ds2, g['ln2_g'], g['ln2_b'] = _ln_bwd(S['s2'], dh2, W['ln2_g'], W['ln2_b'], silu=False, name=t + "_ln2_bwd")
    g['w_down'] = _mm(S['act'], ds2, mode='tn', out_dtype=F32, name=t + "_dw_down")
    dact = _mm(ds2, W['w_down'], mode='nt', out_dtype=F32, name=t + "_dact")
    duf, g['ffn_w'], g['ffn_b'] = _conv_act_bwd(S['uf'], 0, W['ffn_w'], W['ffn_b'], dact, ffn=True, name=t + "_ffn_conv_bwd")
    g['w_up'] = _mm(S['h1'], duf, mode='tn', out_dtype=F32, name=t + "_dw_up")
    dh1 = _mm(duf, W['w_up'], mode='nt', out_dtype=F32, res=ds2, res_scale=DN_ALPHA, name=t + "_dh1")
    ds1, g['ln1_g'], g['ln1_b'] = _ln_bwd(S['s1'], dh1, W['ln1_g'], W['ln1_b'], silu=False, name=t + "_ln1_bwd")
    g['w_o'] = _mm(S['m'], ds1, mode='tn', out_dtype=F32, name=t + "_dw_o")
    dm = _mm(ds1, W['w_o'], mode='nt', out_dtype=F32, name=t + "_dm")
    dya, dyb, dgate = _gate_bwd(S['u'], o_gate, S['ya'], S['yb'], dm, name=t + "_gate_bwd")
    g['w_conv_out'] = _mm(S['v3'], dya, mode='tn', out_dtype=F32, name=t + "_dw_conv_out")
    dv3 = _mm(dya, W['w_conv_out'], mode='nt', out_dtype=F32, name=t + "_dv3")
    dv1, g['conv_g'], g['conv_beta'] = _ln_bwd(S['v1'], dv3, W['conv_g'], W['conv_beta'], silu=True, name=t + "_conv_ln_bwd")
    dglu, g['conv_w'], g['conv_b'] = _glu_conv_bwd(dv1, S['u'], W['conv_w'], name=t + "_conf_conv_bwd")
    g['w_ssm_out'] = _mm(S['yn'], dyb, mode='tn', out_dtype=F32, name=t + "_dw_ssm_out")
    dyn = _mm(dyb, W['w_ssm_out'], mode='nt', out_dtype=F32, name=t + "_dyn")
    dy, dz, g['norm_w'] = _rms_bwd(S['y'], S['u'], o_z, W['norm_w'], dyn, name=t + "_rms_bwd")
    dxbc_c, da, ddtx, g['d'] = _ssd_bwd(S['xbc'], S['dt'], S['dtT'], W['a_col'], W['a_row'], W['d_col'], S['hst'], dy,
                                        H=H, name=t + "_ssd_bwd")
    ddt_raw, g['dt_bias'], g['a_log'] = _dt_bwd(da, ddtx, S['dt'], S['udt'], W['dt_bias'], W['a_col'], H=H, name=t + "_dt_bwd")
    dxbc, g['ssm_w'], g['ssm_b'] = _conv_act_bwd(S['u'], o_xbc, W['ssm_w'], W['ssm_b'], dxbc_c, ffn=False, name=t + "_ssm_conv_bwd")
    du = jnp.concatenate([dglu, dz, dxbc, dgate], axis=1)
    g['w_main'] = _mm(S['h'], du, mode='tn', out_dtype=F32, name=t + "_dw_main")
    g['w_dt'] = _mm(S['h'], ddt_raw, mode='tn', out_dtype=F32, name=t + "_dw_dt")
    dh = _mm(du, W['w_main'], mode='nt', out_dtype=F32, res=ds1, res_scale=DN_ALPHA, name=t + "_dh_main")
    dh = _mm(ddt_raw, W['w_dt'], mode='nt', out_dtype=F32, res=dh, res_scale=1.0, name=t + "_dh_dt")
    return dh, g


def _layer_grads_to_params(g, dims):
    D, SX, XBC, H, FF = dims
    gt, ft = _tile(D, GLU_TILE), _tile(FF, FFN_TILE)
    wm = g['w_main']
    da, dg = _deinterleave(wm[:, :2 * D], gt)
    rest = wm[:, 2 * D:]
    w_in = jnp.concatenate([da, dg, rest[:, :SX + XBC], g['w_dt'][:, :H], rest[:, SX + XBC:]], axis=1)
    fwg, fwv = _deinterleave(g['ffn_w'], ft)
    fbg, fbv = _deinterleave(g['ffn_b'], ft)
    upg, upv = _deinterleave(g['w_up'], ft)
    return dict(
        w_in=w_in, conv_dw_w=g['conv_w'], conv_dw_b=g['conv_b'][0], conv_ln_g=g['conv_g'][0], conv_ln_b=g['conv_beta'][0],
        w_conv_out=g['w_conv_out'], ssm_conv_w=g['ssm_w'], ssm_conv_b=g['ssm_b'][0],
        ssm_dt_bias=g['dt_bias'][0, :H], ssm_a_log=g['a_log'][0, :H], ssm_d=g['d'][0, :H], ssm_norm_w=g['norm_w'][0],
        w_ssm_out=g['w_ssm_out'], w_o=g['w_o'], ln1_g=g['ln1_g'][0], ln1_b=g['ln1_b'][0],
        w_ffn_up=jnp.concatenate([upg, upv], axis=1), ffn_dw_w=jnp.concatenate([fwg, fwv], axis=1),
        ffn_dw_b=jnp.concatenate([fbg, fbv], axis=1)[0], w_ffn_down=g['w_down'], ln2_g=g['ln2_g'][0], ln2_b=g['ln2_b'][0])


def kernel(x, ln_in_g, ln_in_b, w_in, conv_dw_w, conv_dw_b, conv_ln_g, conv_ln_b, w_conv_out, ssm_conv_w, ssm_conv_b, ssm_dt_bias, ssm_a_log, ssm_d, ssm_norm_w, w_ssm_out, w_o, ln1_g, ln1_b, w_ffn_up, ffn_dw_w, ffn_dw_b, w_ffn_down, ln2_g, ln2_b, loss_target, m_ln_in_g, m_ln_in_b, m_w_in, m_conv_dw_w, m_conv_dw_b, m_conv_ln_g, m_conv_ln_b, m_w_conv_out, m_ssm_conv_w, m_ssm_conv_b, m_ssm_dt_bias, m_ssm_a_log, m_ssm_d, m_ssm_norm_w, m_w_ssm_out, m_w_o, m_ln1_g, m_ln1_b, m_w_ffn_up, m_ffn_dw_w, m_ffn_dw_b, m_w_ffn_down, m_ln2_g, m_ln2_b, v_ln_in_g, v_ln_in_b, v_w_in, v_conv_dw_w, v_conv_dw_b, v_conv_ln_g, v_conv_ln_b, v_w_conv_out, v_ssm_conv_w, v_ssm_conv_b, v_ssm_dt_bias, v_ssm_a_log, v_ssm_d, v_ssm_norm_w, v_w_ssm_out, v_w_o, v_ln1_g, v_ln1_b, v_w_ffn_up, v_ffn_dw_w, v_ffn_dw_b, v_w_ffn_down, v_ln2_g, v_ln2_b):
    weights = (ln_in_g, ln_in_b, w_in, conv_dw_w, conv_dw_b, conv_ln_g, conv_ln_b, w_conv_out, ssm_conv_w, ssm_conv_b,
               ssm_dt_bias, ssm_a_log, ssm_d, ssm_norm_w, w_ssm_out, w_o, ln1_g, ln1_b, w_ffn_up, ffn_dw_w, ffn_dw_b,
               w_ffn_down, ln2_g, ln2_b)
    moments_m = (m_ln_in_g, m_ln_in_b, m_w_in, m_conv_dw_w, m_conv_dw_b, m_conv_ln_g, m_conv_ln_b, m_w_conv_out,
                 m_ssm_conv_w, m_ssm_conv_b, m_ssm_dt_bias, m_ssm_a_log, m_ssm_d, m_ssm_norm_w, m_w_ssm_out, m_w_o,
                 m_ln1_g, m_ln1_b, m_w_ffn_up, m_ffn_dw_w, m_ffn_dw_b, m_w_ffn_down, m_ln2_g, m_ln2_b)
    moments_v = (v_ln_in_g, v_ln_in_b, v_w_in, v_conv_dw_w, v_conv_dw_b, v_conv_ln_g, v_conv_ln_b, v_w_conv_out,
                 v_ssm_conv_w, v_ssm_conv_b, v_ssm_dt_bias, v_ssm_a_log, v_ssm_d, v_ssm_norm_w, v_w_ssm_out, v_w_o,
                 v_ln1_g, v_ln1_b, v_w_ffn_up, v_ffn_dw_w, v_ffn_dw_b, v_w_ffn_down, v_ln2_g, v_ln2_b)
    p = dict(zip(PARAMS, weights))
    pm = dict(zip(PARAMS, moments_m))
    pv = dict(zip(PARAMS, moments_v))

    T, D = x.shape[1], x.shape[2]
    SX = w_ssm_out.shape[1] * N_DEV
    XBC = ssm_conv_b.shape[-1]
    H = ssm_d.shape[-1]
    FF = ffn_dw_b.shape[-1] // 2
    dims = (D, SX, XBC, H, FF)
    depth = w_in.shape[0]

    full = _gather_weights(p)
    layers = [_layer_weights(p, full, l, dims) for l in range(depth)]

    xs = x.reshape(T, D)
    h = _ln_fwd(xs, _row(ln_in_g), _row(ln_in_b), name="ln_in_fwd")
    saved = []
    for l in range(depth):
        h, s = _layer_fwd(h, layers[l], l, dims)
        saved.append(s)
    loss_part, dh = _loss_fwd_bwd(h, loss_target.reshape(T, D), name="loss")

    layer_grads = [None] * depth
    for l in reversed(range(depth)):
        dh, g = _layer_bwd(dh, layers[l], saved[l], l, dims)
        layer_grads[l] = _layer_grads_to_params(g, dims)
    grad_x, dg_in, db_in = _ln_bwd(xs, dh, _row(ln_in_g), _row(ln_in_b), silu=False, name="ln_in_bwd")

    local = {k: jnp.stack([layer_grads[l][k] for l in range(depth)]) for k in layer_grads[0]}
    local['ln_in_g'], local['ln_in_b'] = dg_in[0], db_in[0]

    parts = _scatter_grads(local, p)
    small = {k: local[k] for k in PARAMS if k not in SHARDED}
    small['loss'] = loss_part
    grads = _allreduce_small(small)
    loss = grads.pop('loss').reshape(())

    delta, new_m, new_v = {}, {}, {}
    for k in SHARDED:
        grads[k], delta[k], new_m[k], new_v[k] = _sum_adamw(parts[k], p[k], pm[k], pv[k], name="sum_adamw_" + k)
    rest = [k for k in PARAMS if k not in SHARDED]
    flat = lambda d: _to_rows(jnp.concatenate([d[k].reshape(-1) for k in rest]), 8)
    d_, m_, v_ = _adamw(flat(p), flat(grads), flat(pm), flat(pv), name="adamw_small")
    off = 0
    for k in rest:
        n, shp = p[k].size, p[k].shape
        delta[k] = d_.reshape(-1)[off:off + n].reshape(shp)
        new_m[k] = m_.reshape(-1)[off:off + n].reshape(shp)
        new_v[k] = v_.reshape(-1)[off:off + n].reshape(shp)
        off += n

    return (loss, grad_x.reshape(x.shape), *[grads[k] for k in PARAMS], *[delta[k] for k in PARAMS],
            *[new_m[k] for k in PARAMS], *[new_v[k] for k in PARAMS])
```

```python
import math

import jax
import jax.numpy as jnp
from jax import lax
from jax.experimental import pallas as pl
from jax.experimental.pallas import tpu as pltpu

F32 = jnp.float32
BF16 = jnp.bfloat16
WIRE = jnp.bfloat16
HIGHEST = lax.Precision.HIGHEST

DEPTH = 2
SSM_STATE = 128
SSM_CHUNK = 128
SSM_GROUPS = 4
SSM_HEAD_DIM = 64
DN_ALPHA = (2 * DEPTH) ** 0.25
LN_EPS = 1e-5
RMS_EPS = 1e-5
ADAM_LR = 0.001
ADAM_B1 = 0.9
ADAM_B2 = 0.999
ADAM_EPS = 1e-08
ADAM_WD = 0.01
ADAM_STEP = 10

N_DEV = 8
LANES = 128
VMEM_LIMIT = 48 * 1024 * 1024
GLU_TILE = 512
FFN_TILE = 1408
CONV_ROWS = 256

PARAMS = ['ln_in_g', 'ln_in_b', 'w_in', 'conv_dw_w', 'conv_dw_b', 'conv_ln_g', 'conv_ln_b', 'w_conv_out',
          'ssm_conv_w', 'ssm_conv_b', 'ssm_dt_bias', 'ssm_a_log', 'ssm_d', 'ssm_norm_w', 'w_ssm_out', 'w_o',
          'ln1_g', 'ln1_b', 'w_ffn_up', 'ffn_dw_w', 'ffn_dw_b', 'w_ffn_down', 'ln2_g', 'ln2_b']
SHARDED = {'w_in': (2, True), 'conv_dw_w': (2, False), 'w_conv_out': (1, True), 'ssm_conv_w': (2, False),
           'w_ssm_out': (1, True), 'w_o': (1, True), 'w_ffn_up': (2, True), 'ffn_dw_w': (2, False),
           'w_ffn_down': (1, True)}


def _sds(shape, dtype):
    return jax.ShapeDtypeStruct(tuple(shape), dtype)


def _tile(dim, pref):
    return pref if dim % pref == 0 else dim


def _pick(dim, pref, mult):
    best = None
    for t in range(mult, min(dim, pref) + 1, mult):
        if dim % t == 0:
            best = t
    return best or dim


def _cp(*sem):
    return pltpu.CompilerParams(dimension_semantics=sem, vmem_limit_bytes=VMEM_LIMIT)


def _sigmoid(x):
    return 1.0 / (1.0 + jnp.exp(-x))


def _silu_and_grad(x):
    s = _sigmoid(x)
    return x * s, s * (1.0 + x * (1.0 - s))


def _ln_stats(s):
    mu = jnp.mean(s, axis=-1, keepdims=True)
    xc = s - mu
    var = jnp.mean(xc * xc, axis=-1, keepdims=True)
    rstd = lax.rsqrt(var + LN_EPS)
    return xc * rstd, rstd


def _dot(a, b, dims):
    return lax.dot_general(a, b, (dims, ((), ())), preferred_element_type=F32)


_NN = ((1,), (0,))
_NT = ((1,), (1,))
_TN = ((0,), (0,))


def _mm(a, b, *, mode, out_dtype, name, res=None, res_scale=1.0, tm=1024, tn=1408, tk=1408):
    if mode == 'nn':
        (M, K), (_, N) = a.shape, b.shape
    elif mode == 'nt':
        (M, K), (N, _) = a.shape, b.shape
    else:
        (K, M), (_, N) = a.shape, b.shape
        tm, tk = 1408, 512
    tm, tn, tk = _pick(M, tm, 8), _pick(N, tn, LANES), _pick(K, tk, LANES)
    nk = K // tk
    dims = {'nn': _NN, 'nt': _NT, 'tn': _TN}[mode]

    def body(*refs):
        if res is None:
            a_ref, b_ref, o_ref = refs[:3]
            r_ref = None
        else:
            a_ref, b_ref, r_ref, o_ref = refs[:4]
        p = _dot(a_ref[...].astype(BF16), b_ref[...].astype(BF16), dims)

        def finish(acc):
            if r_ref is not None:
                acc = acc + res_scale * r_ref[...]
            o_ref[...] = acc.astype(out_dtype)

        if nk == 1:
            finish(p)
        else:
            acc_ref = refs[-1]
            k = pl.program_id(2)

            @pl.when(k == 0)
            def _():
                acc_ref[...] = p

            @pl.when(k > 0)
            def _():
                acc_ref[...] += p

            @pl.when(k == nk - 1)
            def _():
                finish(acc_ref[...])

    if mode == 'nn':
        a_spec = pl.BlockSpec((tm, tk), lambda i, j, k: (i, k))
        b_spec = pl.BlockSpec((tk, tn), lambda i, j, k: (k, j))
    elif mode == 'nt':
        a_spec = pl.BlockSpec((tm, tk), lambda i, j, k: (i, k))
        b_spec = pl.BlockSpec((tn, tk), lambda i, j, k: (j, k))
    else:
        a_spec = pl.BlockSpec((tk, tm), lambda i, j, k: (k, i))
        b_spec = pl.BlockSpec((tk, tn), lambda i, j, k: (k, j))
    o_spec = pl.BlockSpec((tm, tn), lambda i, j, k: (i, j))
    in_specs = [a_spec, b_spec] + ([o_spec] if res is not None else [])
    args = (a, b) + ((res,) if res is not None else ())
    return pl.pallas_call(
        body, name=name, out_shape=_sds((M, N), out_dtype), grid=(M // tm, N // tn, nk),
        in_specs=in_specs, out_specs=o_spec,
        scratch_shapes=[pltpu.VMEM((tm, tn), F32)] if nk > 1 else [],
        compiler_params=_cp("parallel", "parallel", "arbitrary"))(*args)


def _mm_res_ln(a, w, res, g, b, *, name, tm=512, tk=1408):
    (M, K), (_, N) = a.shape, w.shape
    tm, tk = _pick(M, tm, 8), _pick(K, tk, LANES)
    nk = K // tk

    def body(a_ref, w_ref, r_ref, g_ref, b_ref, h_ref, s_ref, acc_ref):
        k = pl.program_id(1)
        p = _dot(a_ref[...].astype(BF16), w_ref[...].astype(BF16), _NN)

        @pl.when(k == 0)
        def _():
            acc_ref[...] = p

        @pl.when(k > 0)
        def _():
            acc_ref[...] += p

        @pl.when(k == nk - 1)
        def _():
            s = DN_ALPHA * r_ref[...] + acc_ref[...]
            xhat, _ = _ln_stats(s)
            s_ref[...] = s
            h_ref[...] = xhat * g_ref[...] + b_ref[...]

    row = pl.BlockSpec((tm, N), lambda i, k: (i, 0))
    vec = pl.BlockSpec((1, N), lambda i, k: (0, 0))
    return pl.pallas_call(
        body, name=name, out_shape=(_sds((M, N), F32), _sds((M, N), F32)), grid=(M // tm, nk),
        in_specs=[pl.BlockSpec((tm, tk), lambda i, k: (i, k)), pl.BlockSpec((tk, N), lambda i, k: (k, 0)), row, vec, vec],
        out_specs=(row, row), scratch_shapes=[pltpu.VMEM((tm, N), F32)],
        compiler_params=_cp("parallel", "arbitrary"))(a, w, res, g, b)


def _ln_fwd(x, g, b, *, name):
    T, D = x.shape
    tt = _tile(T, 512)

    def body(x_ref, g_ref, b_ref, o_ref):
        xhat, _ = _ln_stats(x_ref[...])
        o_ref[...] = xhat * g_ref[...] + b_ref[...]

    row = pl.BlockSpec((tt, D), lambda i: (i, 0))
    vec = pl.BlockSpec((1, D), lambda i: (0, 0))
    return pl.pallas_call(body, name=name, out_shape=_sds((T, D), F32), grid=(T // tt,),
                          in_specs=[row, vec, vec], out_specs=row, compiler_params=_cp("parallel"))(x, g, b)


def _ln_bwd(s, dy, g, b, *, silu, name):
    T, D = s.shape
    tt = _tile(T, 512)

    def body(s_ref, dy_ref, g_ref, b_ref, ds_ref, dg_ref, db_ref):
        i = pl.program_id(0)
        xhat, rstd = _ln_stats(s_ref[...])
        gg = g_ref[...]
        dyl = dy_ref[...]
        if silu:
            _, dsilu = _silu_and_grad(xhat * gg + b_ref[...])
            dyl = dyl * dsilu
        dxh = dyl * gg
        m1 = jnp.mean(dxh, axis=-1, keepdims=True)
        m2 = jnp.mean(dxh * xhat, axis=-1, keepdims=True)
        ds_ref[...] = rstd * (dxh - m1 - xhat * m2)

        @pl.when(i == 0)
        def _():
            dg_ref[...] = jnp.zeros_like(dg_ref)
            db_ref[...] = jnp.zeros_like(db_ref)

        dg_ref[...] += jnp.sum(dyl * xhat, axis=0, keepdims=True)
        db_ref[...] += jnp.sum(dyl, axis=0, keepdims=True)

    row = pl.BlockSpec((tt, D), lambda i: (i, 0))
    vec = pl.BlockSpec((1, D), lambda i: (0, 0))
    return pl.pallas_call(body, name=name, out_shape=(_sds((T, D), F32), _sds((1, D), F32), _sds((1, D), F32)),
                          grid=(T // tt,), in_specs=[row, row, vec, vec], out_specs=(row, vec, vec),
                          compiler_params=_cp("arbitrary"))(s, dy, g, b)


def _loss_fwd_bwd(h, tgt, *, name):
    T, D = h.shape
    tt = _tile(T, 512)

    def body(h_ref, t_ref, dh_ref, l_ref):
        i = pl.program_id(0)
        e = h_ref[...] - t_ref[...]
        dh_ref[...] = e * (1.0 / D)

        @pl.when(i == 0)
        def _():
            l_ref[...] = jnp.zeros_like(l_ref)

        part = jnp.sum(jnp.sum(e * e, axis=1, keepdims=True), axis=0, keepdims=True) * (0.5 / D)
        l_ref[...] += jnp.broadcast_to(part, l_ref.shape)

    row = pl.BlockSpec((tt, D), lambda i: (i, 0))
    one = pl.BlockSpec((8, LANES), lambda i: (0, 0))
    dh, l = pl.pallas_call(body, name=name, out_shape=(_sds((T, D), F32), _sds((8, LANES), F32)), grid=(T // tt,),
                           in_specs=[row, row], out_specs=(row, one), compiler_params=_cp("arbitrary"))(h, tgt)
    return l[0:1, 0:1], dh


def _halo_rows(k):
    return 32 if k > 9 else 8


def _shifted(ext, K, first):
    rolled = {0: ext}
    out = []
    for k in range(K):
        r = (first + k) % 8
        if r not in rolled:
            rolled[r] = pltpu.roll(ext, ext.shape[0] - r, 0)
        out.append((rolled[r], first + k - r))
    return out


def _taps(ext, w_ref, sl, K, first, n, reverse=False):
    acc = None
    for k, (z, base) in enumerate(_shifted(ext, K, first)):
        kw = K - 1 - k if reverse else k
        term = w_ref[kw:kw + 1, sl] * z[base:base + n]
        acc = term if acc is None else acc + term
    return acc


def _tap_sums(d, ext, dw_ref, sl, K, first):
    n = d.shape[0]
    for k, (z, base) in enumerate(_shifted(ext, K, first)):
        dw_ref[k:k + 1, sl] += jnp.sum(d * z[base:base + n], axis=0, keepdims=True)


def _strip_width(width):
    return LANES if width % LANES == 0 else width


def _for_strips(width, fn):
    sw = _strip_width(width)

    def step(s, carry):
        fn(pl.ds(pl.multiple_of(s * sw, sw), sw), s)
        return carry

    lax.fori_loop(0, width // sw, step, 0)


def _prev_rows(tt, hb):
    return lambda t: jnp.maximum(t * (tt // hb) - 1, 0)


def _next_rows(tt, hb, T):
    return lambda t: jnp.minimum((t + 1) * (tt // hb), T // hb - 1)


def _conf_fwd(u, w, b, g, beta, *, D, name):
    T = u.shape[0]
    K = w.shape[0]
    tt, hb, tc = _tile(T, CONV_ROWS), _halo_rows(K), _tile(D, GLU_TILE)
    sw = _strip_width(tc)
    per_tile = tc // sw
    prev = _prev_rows(tt, hb)

    def body(u_ref, uh_ref, w_ref, b_ref, g_ref, beta_ref, v1_ref, v3_ref):
        i = pl.program_id(0)

        def strip(sl, s):
            a0 = (s // per_tile) * (2 * tc) + (s % per_tile) * sw
            a_sl, g_sl = pl.ds(pl.multiple_of(a0, sw), sw), pl.ds(pl.multiple_of(a0 + tc, sw), sw)
            halo = jnp.where(i > 0, uh_ref[:, a_sl] * _sigmoid(uh_ref[:, g_sl]), 0.0)
            ext = jnp.concatenate([halo, u_ref[:, a_sl] * _sigmoid(u_ref[:, g_sl])], axis=0)
            v1_ref[:, sl] = _taps(ext, w_ref, sl, K, hb - (K - 1), tt) + b_ref[:, sl]

        _for_strips(D, strip)
        xhat, _ = _ln_stats(v1_ref[...])
        v2 = xhat * g_ref[...] + beta_ref[...]
        v3_ref[...] = (v2 * _sigmoid(v2)).astype(BF16)

    row = pl.BlockSpec((tt, D), lambda t: (t, 0))
    vec = pl.BlockSpec((1, D), lambda t: (0, 0))
    return pl.pallas_call(
        body, name=name, out_shape=(_sds((T, D), F32), _sds((T, D), BF16)), grid=(T // tt,),
        in_specs=[pl.BlockSpec((tt, 2 * D), lambda t: (t, 0)), pl.BlockSpec((hb, 2 * D), lambda t: (prev(t), 0)),
                  pl.BlockSpec((K, D), lambda t: (0, 0)), vec, vec, vec],
        out_specs=(row, row), compiler_params=_cp("parallel"))(u, u, w, b, g, beta)


def _conv_act_fwd(x, off, w, b, *, ffn, name):
    T = x.shape[0]
    K, Cw = w.shape
    tt, hb = _tile(T, CONV_ROWS), _halo_rows(K)
    tc = _tile(Cw // 2, FFN_TILE) if ffn else _pick(math.gcd(Cw, off), 1536, LANES)
    xw = 2 * tc if ffn else tc
    assert off % xw == 0
    ob = off // xw
    prev = _prev_rows(tt, hb)

    def body(x_ref, xh_ref, w_ref, b_ref, o_ref):
        first = pl.program_id(0) == 0

        def pre_of(sl):
            ext = jnp.concatenate([jnp.where(first, 0.0, xh_ref[:, sl]), x_ref[:, sl]], axis=0)
            return _taps(ext, w_ref, sl, K, hb - (K - 1), tt) + b_ref[:, sl]

        def strip(sl, s):
            if ffn:
                gate = pre_of(sl)
                val = pre_of(pl.ds(pl.multiple_of(tc + s * sw, sw), sw))
                o_ref[:, sl] = (gate * _sigmoid(gate) * val).astype(BF16)
            else:
                pre = pre_of(sl)
                o_ref[:, sl] = pre * _sigmoid(pre)

        _for_strips(tc, strip)

    sw = _strip_width(tc)
    return pl.pallas_call(
        body, name=name, out_shape=_sds((T, Cw // 2), BF16) if ffn else _sds((T, Cw), F32), grid=(T // tt, Cw // xw),
        in_specs=[pl.BlockSpec((tt, xw), lambda t, c: (t, c + ob)), pl.BlockSpec((hb, xw), lambda t, c: (prev(t), c + ob)),
                  pl.BlockSpec((K, xw), lambda t, c: (0, c)), pl.BlockSpec((1, xw), lambda t, c: (0, c))],
        out_specs=pl.BlockSpec((tt, tc), lambda t, c: (t, c)),
        compiler_params=_cp("parallel", "parallel"))(x, x, w, b)


def _conv_act_bwd(x, off, w, b, dout, *, ffn, name):
    T = x.shape[0]
    K, Cw = w.shape
    tt, hb = _tile(T, CONV_ROWS), _halo_rows(K)
    tc = _tile(Cw // 2, FFN_TILE) if ffn else _pick(math.gcd(Cw, off), 1536, LANES)
    xw = 2 * tc if ffn else tc
    assert off % xw == 0
    ob = off // xw
    nt = T // tt
    prev, nxt = _prev_rows(tt, hb), _next_rows(tt, hb, T)

    sw = _strip_width(tc)

    def body(x_ref, xp_ref, xn_ref, d_ref, dn_ref, w_ref, b_ref, dx_ref, dw_ref, db_ref):
        t = pl.program_id(1)

        @pl.when(t == 0)
        def _():
            dw_ref[...] = jnp.zeros_like(dw_ref)
            db_ref[...] = jnp.zeros_like(db_ref)

        def ext_of(sl):
            return jnp.concatenate([jnp.where(t == 0, 0.0, xp_ref[:, sl]), x_ref[:, sl], xn_ref[:, sl]], axis=0)

        def pre_of(ext, sl):
            return _taps(ext, w_ref, sl, K, hb - (K - 1), tt + hb) + b_ref[:, sl]

        def finish(ext, dpre, sl):
            dx_ref[:, sl] = _taps(dpre, w_ref, sl, K, 0, tt, reverse=True).astype(BF16)
            dp = dpre[0:tt]
            _tap_sums(dp, ext, dw_ref, sl, K, hb - (K - 1))
            db_ref[:, sl] += jnp.sum(dp, axis=0, keepdims=True)

        def strip(sl, s):
            d = jnp.concatenate([d_ref[:, sl], jnp.where(t == nt - 1, 0.0, dn_ref[:, sl])], axis=0)
            if ffn:
                vsl = pl.ds(pl.multiple_of(tc + s * sw, sw), sw)
                eg, ev = ext_of(sl), ext_of(vsl)
                sg, dsg = _silu_and_grad(pre_of(eg, sl))
                val = pre_of(ev, vsl)
                finish(eg, d * val * dsg, sl)
                finish(ev, d * sg, vsl)
            else:
                ext = ext_of(sl)
                finish(ext, d * _silu_and_grad(pre_of(ext, sl))[1], sl)

        _for_strips(tc, strip)

    return pl.pallas_call(
        body, name=name, out_shape=(_sds((T, Cw), BF16), _sds((K, Cw), F32), _sds((1, Cw), F32)),
        grid=(Cw // xw, nt),
        in_specs=[pl.BlockSpec((tt, xw), lambda c, t: (t, c + ob)), pl.BlockSpec((hb, xw), lambda c, t: (prev(t), c + ob)),
                  pl.BlockSpec((hb, xw), lambda c, t: (nxt(t), c + ob)),
                  pl.BlockSpec((tt, tc), lambda c, t: (t, c)), pl.BlockSpec((hb, tc), lambda c, t: (nxt(t), c)),
                  pl.BlockSpec((K, xw), lambda c, t: (0, c)), pl.BlockSpec((1, xw), lambda c, t: (0, c))],
        out_specs=(pl.BlockSpec((tt, xw), lambda c, t: (t, c)), pl.BlockSpec((K, xw), lambda c, t: (0, c)),
                   pl.BlockSpec((1, xw), lambda c, t: (0, c))),
        compiler_params=_cp("parallel", "arbitrary"))(x, x, x, dout, dout, w, b)


def _glu_conv_bwd(dpre, u, w, *, name):
    T, C = dpre.shape
    K = w.shape[0]
    tt, hb, tc = _tile(T, CONV_ROWS), _halo_rows(K), _tile(C, GLU_TILE)
    nt = T // tt
    prev, nxt = _prev_rows(tt, hb), _next_rows(tt, hb, T)

    sw = _strip_width(tc)

    def body(d_ref, dn_ref, x_ref, xh_ref, w_ref, dx_ref, dw_ref, db_ref):
        t = pl.program_id(1)

        @pl.when(t == 0)
        def _():
            dw_ref[...] = jnp.zeros_like(dw_ref)
            db_ref[...] = jnp.zeros_like(db_ref)

        def strip(sl, s):
            gsl = pl.ds(pl.multiple_of(tc + s * sw, sw), sw)
            d = d_ref[:, sl]
            dext = jnp.concatenate([d, jnp.where(t == nt - 1, 0.0, dn_ref[:, sl])], axis=0)
            dxin = _taps(dext, w_ref, sl, K, 0, tt, reverse=True)
            a, sg = x_ref[:, sl], _sigmoid(x_ref[:, gsl])
            v0 = a * sg
            dx_ref[:, sl] = (dxin * sg).astype(BF16)
            dx_ref[:, gsl] = (dxin * v0 * (1.0 - sg)).astype(BF16)
            halo = jnp.where(t == 0, 0.0, xh_ref[:, sl] * _sigmoid(xh_ref[:, gsl]))
            _tap_sums(d, jnp.concatenate([halo, v0], axis=0), dw_ref, sl, K, hb - (K - 1))
            db_ref[:, sl] += jnp.sum(d, axis=0, keepdims=True)

        _for_strips(tc, strip)

    return pl.pallas_call(
        body, name=name, out_shape=(_sds((T, 2 * C), BF16), _sds((K, C), F32), _sds((1, C), F32)), grid=(C // tc, nt),
        in_specs=[pl.BlockSpec((tt, tc), lambda c, t: (t, c)), pl.BlockSpec((hb, tc), lambda c, t: (nxt(t), c)),
                  pl.BlockSpec((tt, 2 * tc), lambda c, t: (t, c)), pl.BlockSpec((hb, 2 * tc), lambda c, t: (prev(t), c)),
                  pl.BlockSpec((K, tc), lambda c, t: (0, c))],
        out_specs=(pl.BlockSpec((tt, 2 * tc), lambda c, t: (t, c)), pl.BlockSpec((K, tc), lambda c, t: (0, c)),
                   pl.BlockSpec((1, tc), lambda c, t: (0, c))),
        compiler_params=_cp("parallel", "arbitrary"))(dpre, dpre, u, u, w)


def _softplus(x):
    t = jnp.exp(-jnp.abs(x))
    u = 1.0 + t
    log1p = jnp.where(u == 1.0, t, jnp.log(u) * t / jnp.where(u == 1.0, 1.0, u - 1.0))
    return jnp.maximum(x, 0.0) + log1p


def _dt_fwd(udt, bias, *, H, name):
    T = udt.shape[0]
    tt = _tile(T, 1024)

    def body(u_ref, b_ref, o_ref):
        lane = lax.broadcasted_iota(jnp.int32, (tt, LANES), 1)
        o_ref[...] = jnp.where(lane < H, _softplus(u_ref[...] + b_ref[...]), 0.0)

    row = pl.BlockSpec((tt, LANES), lambda i: (i, 0))
    return pl.pallas_call(body, name=name, out_shape=_sds((T, LANES), F32), grid=(T // tt,),
                          in_specs=[row, pl.BlockSpec((1, LANES), lambda i: (0, 0))], out_specs=row,
                          compiler_params=_cp("parallel"))(udt, bias)


def _dt_bwd(da, ddtx, dt, udt, bias, a_col, *, H, name):
    T = da.shape[0]
    tt = _tile(T, 1024)

    def body(da_ref, dx_ref, dt_ref, u_ref, b_ref, a_ref, draw_ref, dbias_ref, dalog_ref):
        i = pl.program_id(0)
        lane = lax.broadcasted_iota(jnp.int32, (tt, LANES), 1)
        dav = da_ref[...]
        ddt = dav * a_ref[...] + dx_ref[...]
        draw = jnp.where(lane < H, ddt * _sigmoid(u_ref[...] + b_ref[...]), 0.0)
        draw_ref[...] = draw.astype(BF16)

        @pl.when(i == 0)
        def _():
            dbias_ref[...] = jnp.zeros_like(dbias_ref)
            dalog_ref[...] = jnp.zeros_like(dalog_ref)

        dbias_ref[...] += jnp.sum(draw, axis=0, keepdims=True)
        dalog_ref[...] += jnp.sum(dav * dt_ref[...], axis=0, keepdims=True) * a_ref[...]

    row = pl.BlockSpec((tt, LANES), lambda i: (i, 0))
    vec = pl.BlockSpec((1, LANES), lambda i: (0, 0))
    return pl.pallas_call(body, name=name,
                          out_shape=(_sds((T, LANES), BF16), _sds((1, LANES), F32), _sds((1, LANES), F32)),
                          grid=(T // tt,), in_specs=[row, row, row, row, vec, vec], out_specs=(row, vec, vec),
                          compiler_params=_cp("arbitrary"))(da, ddtx, dt, udt, bias, a_col)


def _ssd_consts():
    L = SSM_CHUNK
    r = lax.broadcasted_iota(jnp.int32, (L, L), 0)
    c = lax.broadcasted_iota(jnp.int32, (L, L), 1)
    return r, c


def _ssd_chunk_decays(dtc_ref, dtr_ref, acol_ref, arow_ref, cs_ref, csr_ref, r, c):
    L = SSM_CHUNK
    dtc = dtc_ref[...]
    tril = (r >= c).astype(F32)
    triu = (r <= c).astype(F32)
    cs_ref[...] = jnp.dot(tril, dtc * acol_ref[...], precision=HIGHEST, preferred_element_type=F32)
    csr_ref[...] = jnp.dot(dtr_ref[...] * arow_ref[...], triu, precision=HIGHEST, preferred_element_type=F32)
    cs = cs_ref[...]
    cs_last = cs_ref[L - 1:L, :]
    return dtc, cs, jnp.exp(cs), jnp.exp(cs_last - cs), jnp.exp(cs_last), triu


def _ssd_fwd(xbc, dt, dtT, a_col, a_row, d_col, *, H, name):
    T, XBC = xbc.shape
    L, N, G, P = SSM_CHUNK, SSM_STATE, SSM_GROUPS, SSM_HEAD_DIM
    SX = H * P
    HR = dtT.shape[0]
    nc = T // L
    heads_per_group = H // G

    def body(x_ref, dtc_ref, dtr_ref, acol_ref, arow_ref, d_ref, y_ref, hst_ref, state, cs_ref, csr_ref):
        ci = pl.program_id(0)

        @pl.when(ci == 0)
        def _():
            state[...] = jnp.zeros_like(state)

        hst_ref[0] = state[...]
        r, c = _ssd_consts()
        tri = r >= c
        lane_lo = c < P
        row_lo = r < P
        lane1_lo = lax.broadcasted_iota(jnp.int32, (1, LANES), 1) < P
        dtc, cs, e, ds, cd, _ = _ssd_chunk_decays(dtc_ref, dtr_ref, acol_ref, arow_ref, cs_ref, csr_ref, r, c)
        dsk = d_ref[...]

        def sel(arr, h0):
            return jnp.where(lane_lo, arr[:, h0:h0 + 1], arr[:, h0 + 1:h0 + 2])

        for g in range(G):
            Bg = x_ref[:, SX + g * N:SX + (g + 1) * N].astype(BF16)
            Cg = x_ref[:, SX + G * N + g * N:SX + G * N + (g + 1) * N].astype(BF16)
            Gm = _dot(Cg, Bg, _NT)
            for j in range(g * heads_per_group // 2, (g + 1) * heads_per_group // 2):
                h0 = 2 * j
                sl = slice(2 * P * j, 2 * P * (j + 1))
                x2 = x_ref[:, sl]
                X2 = x2 * sel(dtc, h0)
                X2b = X2.astype(BF16)
                ys = []
                for h in (h0, h0 + 1):
                    seg = cs[:, h:h + 1] - csr_ref[h:h + 1, :]
                    Dm = jnp.where(tri, jnp.exp(jnp.where(tri, seg, 0.0)), 0.0)
                    ys.append(_dot((Gm * Dm).astype(BF16), X2b, _NN))
                H2 = state[sl, :]
                R2 = _dot(Cg, H2.astype(BF16), _NT)
                dsk2 = jnp.where(lane1_lo, dsk[:, h0:h0 + 1], dsk[:, h0 + 1:h0 + 2])
                y_ref[:, sl] = jnp.where(lane_lo, ys[0], ys[1]) + sel(e, h0) * R2 + x2 * dsk2
                S2 = _dot((X2 * sel(ds, h0)).astype(BF16), Bg, _TN)
                state[sl, :] = jnp.where(row_lo, cd[:, h0:h0 + 1], cd[:, h0 + 1:h0 + 2]) * H2 + S2

    vec = pl.BlockSpec((1, LANES), lambda i: (0, 0))
    return pl.pallas_call(
        body, name=name, out_shape=(_sds((T, SX), F32), _sds((nc, SX, N), F32)), grid=(nc,),
        in_specs=[pl.BlockSpec((L, XBC), lambda i: (i, 0)), pl.BlockSpec((L, LANES), lambda i: (i, 0)),
                  pl.BlockSpec((HR, L), lambda i: (0, i)), vec, pl.BlockSpec((HR, 1), lambda i: (0, 0)), vec],
        out_specs=(pl.BlockSpec((L, SX), lambda i: (i, 0)), pl.BlockSpec((1, SX, N), lambda i: (i, 0, 0))),
        scratch_shapes=[pltpu.VMEM((SX, N), F32), pltpu.VMEM((L, LANES), F32), pltpu.VMEM((HR, L), F32)],
        compiler_params=_cp("arbitrary"))(xbc, dt, dtT, a_col, a_row, d_col)


def _ssd_bwd(xbc, dt, dtT, a_col, a_row, d_col, hst, dy, *, H, name):
    T, XBC = xbc.shape
    L, N, G, P = SSM_CHUNK, SSM_STATE, SSM_GROUPS, SSM_HEAD_DIM
    SX = H * P
    HR = dtT.shape[0]
    nc = T // L
    heads_per_group = H // G

    def body(x_ref, dtc_ref, dtr_ref, acol_ref, arow_ref, d_ref, hst_ref, dy_ref,
             dx_ref, da_ref, ddtx_ref, dd_ref, dstate, cs_ref, csr_ref):
        ci = pl.program_id(0)

        @pl.when(ci == 0)
        def _():
            dstate[...] = jnp.zeros_like(dstate)
            dd_ref[...] = jnp.zeros_like(dd_ref)

        r, c = _ssd_consts()
        tri = r >= c
        lane_lo = c < P
        row_lo = r < P
        lane1 = lax.broadcasted_iota(jnp.int32, (1, LANES), 1)
        rowc = lax.broadcasted_iota(jnp.int32, (L, 1), 0)
        dtc, cs, e, ds, cd, triu = _ssd_chunk_decays(dtc_ref, dtr_ref, acol_ref, arow_ref, cs_ref, csr_ref, r, c)
        triu_b = triu.astype(BF16)
        dsk = d_ref[...]
        last_row = rowc == L - 1

        def sel(arr, h0):
            return jnp.where(lane_lo, arr[:, h0:h0 + 1], arr[:, h0 + 1:h0 + 2])

        def half(mask_lo, lo):
            return mask_lo if lo else jnp.logical_not(mask_lo)

        def lsum(v, lo):
            return jnp.sum(jnp.where(half(lane_lo, lo), v, 0.0), axis=1, keepdims=True)

        dcs_acc = jnp.zeros((L, LANES), F32)
        dadir_acc = jnp.zeros((L, LANES), F32)
        ddtx_acc = jnp.zeros((L, LANES), F32)
        dd_acc = jnp.zeros((1, LANES), F32)
        for g in range(G):
            bsl = slice(SX + g * N, SX + (g + 1) * N)
            csl = slice(SX + G * N + g * N, SX + G * N + (g + 1) * N)
            Bg = x_ref[:, bsl].astype(BF16)
            Cg = x_ref[:, csl].astype(BF16)
            Gm = _dot(Cg, Bg, _NT)
            dG = jnp.zeros((L, L), F32)
            dBg = jnp.zeros((L, N), F32)
            dCg = jnp.zeros((L, N), F32)
            for j in range(g * heads_per_group // 2, (g + 1) * heads_per_group // 2):
                h0 = 2 * j
                sl = slice(2 * P * j, 2 * P * (j + 1))
                x2 = x_ref[:, sl]
                dy2 = dy_ref[:, sl]
                dt2, e2, ds2 = sel(dtc, h0), sel(e, h0), sel(ds, h0)
                X2 = x2 * dt2
                X2b = X2.astype(BF16)
                H2 = hst_ref[0, sl, :]
                H2b = H2.astype(BF16)
                dHn = dstate[sl, :]
                dHnb = dHn.astype(BF16)
                R2 = _dot(Cg, H2b, _NT)
                dR2b = (e2 * dy2).astype(BF16)
                de2 = dy2 * R2
                dCg = dCg + _dot(dR2b, H2b, _NN)
                dHr = _dot(dR2b, Cg, _TN)
                rs = jnp.sum(dHn * H2, axis=1, keepdims=True)
                dXd = _dot(Bg, dHnb, _NT)
                dBg = dBg + _dot((X2 * ds2).astype(BF16), dHnb, _NN)
                dds2 = dXd * X2
                dX2 = ds2 * dXd
                for lo, h in ((True, h0), (False, h0 + 1)):
                    ds_h, cd_h = ds[:, h:h + 1], cd[:, h:h + 1]
                    dcd = jnp.sum(jnp.where(half(rowc < P, lo), rs, 0.0), axis=0, keepdims=True)
                    t1 = lsum(dds2, lo) * ds_h
                    tail = jnp.sum(t1, axis=0, keepdims=True) + dcd * cd_h
                    dcs_h = lsum(de2, lo) * e[:, h:h + 1] - t1 + jnp.where(last_row, tail, 0.0)
                    seg = cs[:, h:h + 1] - csr_ref[h:h + 1, :]
                    Dm = jnp.where(tri, jnp.exp(jnp.where(tri, seg, 0.0)), 0.0)
                    Mh = Gm * Dm
                    dyh = jnp.where(half(lane_lo, lo), dy2, 0.0).astype(BF16)
                    dM = _dot(dyh, X2b, _NT)
                    dG = dG + dM * Dm
                    dX2 = dX2 + _dot(Mh.astype(BF16), dyh, _TN)
                    W = _dot(triu_b, (dM * Mh).astype(BF16), _NN)
                    da_q = jnp.sum(jnp.where(c < r, W, 0.0), axis=1, keepdims=True)
                    oh = (lane1 == h).astype(F32)
                    dcs_acc = dcs_acc + dcs_h * oh
                    dadir_acc = dadir_acc + da_q * oh
                dxx = dX2 * x2
                q = jnp.sum(dy2 * x2, axis=0, keepdims=True)
                for lo, h in ((True, h0), (False, h0 + 1)):
                    oh = (lane1 == h).astype(F32)
                    ddtx_acc = ddtx_acc + lsum(dxx, lo) * oh
                    dd_acc = dd_acc + jnp.sum(jnp.where(half(lane1 < P, lo), q, 0.0), axis=1, keepdims=True) * oh
                dsk2 = jnp.where(lane1 < P, dsk[:, h0:h0 + 1], dsk[:, h0 + 1:h0 + 2])
                dx_ref[:, sl] = dX2 * dt2 + dy2 * dsk2
                dstate[sl, :] = jnp.where(row_lo, cd[:, h0:h0 + 1], cd[:, h0 + 1:h0 + 2]) * dHn + dHr
            dGb = dG.astype(BF16)
            dx_ref[:, bsl] = dBg + _dot(dGb, Cg, _TN)
            dx_ref[:, csl] = dCg + _dot(dGb, Bg, _NN)
        da_ref[...] = jnp.dot(triu, dcs_acc, precision=HIGHEST, preferred_element_type=F32) + dadir_acc
        ddtx_ref[...] = ddtx_acc
        dd_ref[...] += dd_acc

    vec = pl.BlockSpec((1, LANES), lambda i: (0, 0))
    rev = lambda i: (nc - 1 - i, 0)
    return pl.pallas_call(
        body, name=name,
        out_shape=(_sds((T, XBC), F32), _sds((T, LANES), F32), _sds((T, LANES), F32), _sds((1, LANES), F32)),
        grid=(nc,),
        in_specs=[pl.BlockSpec((L, XBC), rev), pl.BlockSpec((L, LANES), rev),
                  pl.BlockSpec((HR, L), lambda i: (0, nc - 1 - i)), vec, pl.BlockSpec((HR, 1), lambda i: (0, 0)), vec,
                  pl.BlockSpec((1, SX, N), lambda i: (nc - 1 - i, 0, 0)), pl.BlockSpec((L, SX), rev)],
        out_specs=(pl.BlockSpec((L, XBC), rev), pl.BlockSpec((L, LANES), rev), pl.BlockSpec((L, LANES), rev), vec),
        scratch_shapes=[pltpu.VMEM((SX, N), F32), pltpu.VMEM((L, LANES), F32), pltpu.VMEM((HR, L), F32)],
        compiler_params=_cp("arbitrary"))(xbc, dt, dtT, a_col, a_row, d_col, hst, dy)


def _rms_fwd(y, u, zoff, w, *, name):
    T, SX = y.shape
    tt = _tile(T, 256)
    gs = SX // SSM_GROUPS
    zb = zoff // SX

    def body(y_ref, z_ref, w_ref, o_ref):
        for g in range(SSM_GROUPS):
            sl = slice(g * gs, (g + 1) * gs)
            z = z_ref[:, sl]
            yg = y_ref[:, sl] * (z * _sigmoid(z))
            rstd = lax.rsqrt(jnp.mean(yg * yg, axis=-1, keepdims=True) + RMS_EPS)
            o_ref[:, sl] = (yg * rstd * w_ref[:, sl]).astype(BF16)

    row = pl.BlockSpec((tt, SX), lambda i: (i, 0))
    return pl.pallas_call(body, name=name, out_shape=_sds((T, SX), BF16), grid=(T // tt,),
                          in_specs=[row, pl.BlockSpec((tt, SX), lambda i: (i, zb)), pl.BlockSpec((1, SX), lambda i: (0, 0))],
                          out_specs=row, compiler_params=_cp("parallel"))(y, u, w)


def _rms_bwd(y, u, zoff, w, dyn, *, name):
    T, SX = y.shape
    tt = _tile(T, 256)
    gs = SX // SSM_GROUPS
    zb = zoff // SX

    def body(y_ref, z_ref, w_ref, d_ref, dy_ref, dz_ref, dw_ref):
        i = pl.program_id(0)

        @pl.when(i == 0)
        def _():
            dw_ref[...] = jnp.zeros_like(dw_ref)

        for g in range(SSM_GROUPS):
            sl = slice(g * gs, (g + 1) * gs)
            z, yv, d = z_ref[:, sl], y_ref[:, sl], d_ref[:, sl]
            sz, dsz = _silu_and_grad(z)
            yg = yv * sz
            rstd = lax.rsqrt(jnp.mean(yg * yg, axis=-1, keepdims=True) + RMS_EPS)
            t = yg * rstd
            dw_ref[:, sl] += jnp.sum(d * t, axis=0, keepdims=True)
            dt_ = d * w_ref[:, sl]
            dyg = rstd * (dt_ - t * jnp.mean(dt_ * t, axis=-1, keepdims=True))
            dy_ref[:, sl] = dyg * sz
            dz_ref[:, sl] = (dyg * yv * dsz).astype(BF16)

    row = pl.BlockSpec((tt, SX), lambda i: (i, 0))
    vec = pl.BlockSpec((1, SX), lambda i: (0, 0))
    return pl.pallas_call(body, name=name, out_shape=(_sds((T, SX), F32), _sds((T, SX), BF16), _sds((1, SX), F32)),
                          grid=(T // tt,), in_specs=[row, pl.BlockSpec((tt, SX), lambda i: (i, zb)), vec, row],
                          out_specs=(row, row, vec), compiler_params=_cp("arbitrary"))(y, u, w, dyn)


def _gate_fwd(u, goff, ya, yb, *, name):
    T, D = ya.shape
    tt = _tile(T, 512)
    gb = goff // D

    def body(ga_ref, gb_ref, ya_ref, yb_ref, o_ref):
        o_ref[...] = (_sigmoid(ga_ref[...]) * ya_ref[...] + _sigmoid(gb_ref[...]) * yb_ref[...]).astype(BF16)

    row = pl.BlockSpec((tt, D), lambda i: (i, 0))
    return pl.pallas_call(body, name=name, out_shape=_sds((T, D), BF16), grid=(T // tt,),
                          in_specs=[pl.BlockSpec((tt, D), lambda i: (i, gb)), pl.BlockSpec((tt, D), lambda i: (i, gb + 1)), row, row],
                          out_specs=row, compiler_params=_cp("parallel"))(u, u, ya, yb)


def _gate_bwd(u, goff, ya, yb, dm, *, name):
    T, D = ya.shape
    tt = _tile(T, 512)
    gb = goff // D

    def body(ga_ref, gb_ref, ya_ref, yb_ref, dm_ref, dya_ref, dyb_ref, dg_ref):
        d = dm_ref[...]
        sa, sb = _sigmoid(ga_ref[...]), _sigmoid(gb_ref[...])
        dya_ref[...] = (d * sa).astype(BF16)
        dyb_ref[...] = (d * sb).astype(BF16)
        dg_ref[...] = jnp.concatenate([d * ya_ref[...] * sa * (1.0 - sa), d * yb_ref[...] * sb * (1.0 - sb)], axis=1).astype(BF16)

    row = pl.BlockSpec((tt, D), lambda i: (i, 0))
    return pl.pallas_call(body, name=name, out_shape=(_sds((T, D), BF16), _sds((T, D), BF16), _sds((T, 2 * D), BF16)),
                          grid=(T // tt,),
                          in_specs=[pl.BlockSpec((tt, D), lambda i: (i, gb)), pl.BlockSpec((tt, D), lambda i: (i, gb + 1)), row, row, row],
                          out_specs=(row, row, pl.BlockSpec((tt, 2 * D), lambda i: (i, 0))),
                          compiler_params=_cp("parallel"))(u, u, ya, yb, dm)


def _adamw_math(w, gg, m, v):
    nm = ADAM_B1 * m + (1.0 - ADAM_B1) * gg
    nv = ADAM_B2 * v + (1.0 - ADAM_B2) * (gg * gg)
    m_hat = nm / (1.0 - ADAM_B1 ** ADAM_STEP)
    v_hat = nv / (1.0 - ADAM_B2 ** ADAM_STEP)
    return -ADAM_LR * (m_hat / (jnp.sqrt(v_hat) + ADAM_EPS) + ADAM_WD * w), nm, nv


def _adamw(w, g, m, v, *, name):
    R, C = w.shape
    tr = _pick(R, 256, 8)

    def body(w_ref, g_ref, m_ref, v_ref, d_ref, nm_ref, nv_ref):
        d_ref[...], nm_ref[...], nv_ref[...] = _adamw_math(w_ref[...], g_ref[...], m_ref[...], v_ref[...])

    blk = pl.BlockSpec((tr, C), lambda i: (i, 0))
    out = _sds((R, C), F32)
    return pl.pallas_call(body, name=name, out_shape=(out, out, out), grid=(R // tr,), in_specs=[blk] * 4,
                          out_specs=(blk,) * 3, compiler_params=_cp("parallel"))(w, g, m, v)


def _sum_adamw(parts, w, m, v, l, *, name):
    shape = w.shape[1:]
    C = shape[-1]
    R = w[0].size // C
    tr = _pick(R, 256, 8)
    if tr % 8:
        w, m, v, l = w[l:l + 1], m[l:l + 1], v[l:l + 1], 0
    lb = l * (R // tr)

    def body(p_ref, w_ref, m_ref, v_ref, g_ref, d_ref, nm_ref, nv_ref):
        gg = p_ref[0].astype(F32)
        for k in range(1, N_DEV):
            gg = gg + p_ref[k].astype(F32)
        g_ref[...] = gg
        d_ref[...], nm_ref[...], nv_ref[...] = _adamw_math(w_ref[...], gg, m_ref[...], v_ref[...])

    blk = pl.BlockSpec((tr, C), lambda i: (i, 0))
    lblk = pl.BlockSpec((tr, C), lambda i: (i + lb, 0))
    out = _sds((R, C), F32)
    stacked = (w.shape[0] * R, C)
    res = pl.pallas_call(body, name=name, out_shape=(out,) * 4, grid=(R // tr,),
                         in_specs=[pl.BlockSpec((N_DEV, tr, C), lambda i: (0, i, 0)), lblk, lblk, lblk],
                         out_specs=(blk,) * 4, compiler_params=_cp("parallel"))(
        parts.reshape(N_DEV, R, C), w.reshape(stacked), m.reshape(stacked), v.reshape(stacked))
    return tuple(r.reshape(shape) for r in res)


def _sum_slots(x, *, name):
    n, R, C = x.shape
    tr = _tile(R, 512)

    def body(x_ref, o_ref):
        acc = x_ref[0].astype(F32)
        for k in range(1, n):
            acc = acc + x_ref[k].astype(F32)
        o_ref[...] = acc

    return pl.pallas_call(body, name=name, out_shape=_sds((R, C), F32), grid=(R // tr,),
                          in_specs=[pl.BlockSpec((n, tr, C), lambda i: (0, i, 0))],
                          out_specs=pl.BlockSpec((tr, C), lambda i: (i, 0)), compiler_params=_cp("parallel"))(x)


def _exchange(xs, *, scatter, name):
    n = len(xs)

    def body(*refs):
        x_refs, o_refs = refs[:n], refs[n:2 * n]
        send_sems, recv_sems, local_sems = refs[2 * n:]
        mx, my, mc = lax.axis_index("x"), lax.axis_index("y"), lax.axis_index("c")
        me = 4 * mx + 2 * my + mc

        def src(i, d):
            return x_refs[i].at[d] if scatter else x_refs[i]

        locals_ = [pltpu.make_async_copy(src(i, me), o_refs[i].at[me], local_sems.at[i]) for i in range(n)]
        for cp in locals_:
            cp.start()
        sends, recvs = [], []
        for k in range(1, N_DEV):
            px = 1 - mx if k & 4 else mx
            py = 1 - my if k & 2 else my
            pc = 1 - mc if k & 1 else mc
            peer = 4 * px + 2 * py + pc
            for i in range(n):
                common = dict(send_sem=send_sems.at[k - 1, i], recv_sem=recv_sems.at[k - 1, i],
                              device_id=(px, py, pc), device_id_type=pl.DeviceIdType.MESH)
                sends.append(pltpu.make_async_remote_copy(src_ref=src(i, peer), dst_ref=o_refs[i].at[me], **common))
                recvs.append(pltpu.make_async_remote_copy(src_ref=src(i, peer), dst_ref=o_refs[i].at[peer], **common))
        for cp in sends:
            cp.start()
        for cp in recvs:
            cp.wait_recv()
        for cp in sends:
            cp.wait_send()
        for cp in locals_:
            cp.wait()

    any_spec = pl.BlockSpec(memory_space=pl.ANY)
    out_shape = tuple(_sds(x.shape if scatter else (N_DEV,) + x.shape, x.dtype) for x in xs)
    return pl.pallas_call(
        body, name=name, out_shape=out_shape, in_specs=[any_spec] * n, out_specs=(any_spec,) * n,
        scratch_shapes=[pltpu.SemaphoreType.DMA((N_DEV - 1, n)), pltpu.SemaphoreType.DMA((N_DEV - 1, n)),
                        pltpu.SemaphoreType.DMA((n,))])(*xs)


def _to_rows(flat, row_mult):
    n = flat.shape[-1]
    per = row_mult * LANES
    pad = (-n) % per
    flat = jnp.pad(flat, [(0, 0)] * (flat.ndim - 1) + [(0, pad)])
    return flat.reshape(flat.shape[:-1] + ((n + pad) // LANES, LANES))


def _peers():
    mx, my, mc = lax.axis_index("x"), lax.axis_index("y"), lax.axis_index("c")
    out = []
    for k in range(1, N_DEV):
        px = 1 - mx if k & 4 else mx
        py = 1 - my if k & 2 else my
        pc = 1 - mc if k & 1 else mc
        out.append(((px, py, pc), 4 * px + 2 * py + pc))
    return 4 * mx + 2 * my + mc, out


_HBM = pl.BlockSpec(memory_space=pltpu.HBM)
_SEM = pl.BlockSpec(memory_space=pltpu.SEMAPHORE)


def _exchange_start(xs, *, scatter, name):
    n = len(xs)

    def body(*refs):
        x_refs, land_refs, send_sems, recv_sems, token = refs[:n], refs[n:2 * n], refs[2 * n], refs[2 * n + 1], refs[-1]
        me, peers = _peers()
        for k, (dev, peer) in enumerate(peers):
            for i in range(n):
                pltpu.make_async_remote_copy(
                    src_ref=x_refs[i].at[peer] if scatter else x_refs[i], dst_ref=land_refs[i].at[me],
                    send_sem=send_sems.at[k * n + i], recv_sem=recv_sems.at[k * n + i],
                    device_id=dev, device_id_type=pl.DeviceIdType.MESH).start()
        token[...] = jnp.zeros_like(token)

    land_shapes = [x.shape if scatter else (N_DEV,) + x.shape for x in xs]
    lands = [pltpu.with_memory_space_constraint(lax.empty(s, x.dtype), pltpu.HBM) for s, x in zip(land_shapes, xs)]
    srcs = [pltpu.with_memory_space_constraint(x, pltpu.HBM) for x in xs]
    out = pl.pallas_call(
        body, name=name,
        out_shape=(pltpu.SemaphoreType.DMA(((N_DEV - 1) * n,)), pltpu.SemaphoreType.DMA(((N_DEV - 1) * n,)),
                   *[pltpu.HBM(x.shape, x.dtype) for x in xs], *[pltpu.HBM(s, x.dtype) for s, x in zip(land_shapes, xs)],
                   _sds((8, LANES), F32)),
        in_specs=(_HBM,) * (2 * n), out_specs=(_SEM, _SEM) + (_HBM,) * (2 * n) + (pl.BlockSpec(memory_space=pltpu.VMEM),),
        input_output_aliases={i: 2 + i for i in range(2 * n)},
        compiler_params=pltpu.CompilerParams(has_side_effects=pltpu.SideEffectType.DATAFLOW_SIDE_EFFECTING))(*srcs, *lands)
    return (out[0], out[1], list(out[2:2 + n]), list(out[2 + n:2 + 2 * n])), out[-1]


def _exchange_wait(handle, after, *, scatter, name):
    send_sems, recv_sems, srcs, lands = handle
    n = len(srcs)

    def body(*refs):
        x_refs, land_refs, send_sems, recv_sems = refs[:n], refs[n:2 * n], refs[2 * n], refs[2 * n + 1]
        me, peers = _peers()
        for k, (dev, peer) in enumerate(peers):
            for i in range(n):
                cp = pltpu.make_async_remote_copy(
                    src_ref=x_refs[i].at[peer] if scatter else x_refs[i], dst_ref=land_refs[i].at[peer],
                    send_sem=send_sems.at[k * n + i], recv_sem=recv_sems.at[k * n + i],
                    device_id=dev, device_id_type=pl.DeviceIdType.MESH)
                cp.wait_send()
                cp.wait_recv()

    out = pl.pallas_call(
        body, name=name, out_shape=tuple(pltpu.HBM(a.shape, a.dtype) for a in srcs + lands),
        in_specs=(_HBM,) * (2 * n) + (_SEM, _SEM, pl.BlockSpec(memory_space=pl.ANY)), out_specs=(_HBM,) * (2 * n),
        input_output_aliases={i: i for i in range(2 * n)},
        compiler_params=pltpu.CompilerParams(has_side_effects=pltpu.SideEffectType.DATAFLOW_SIDE_EFFECTING))(
        *srcs, *lands, send_sems, recv_sems, after)
    return list(out[n:])


def _with_own_slot(lands, xs, *, scatter):
    me = 4 * lax.axis_index("x") + 2 * lax.axis_index("y") + lax.axis_index("c")
    out = []
    for land, x in zip(lands, xs):
        own = lax.dynamic_index_in_dim(x, me, 0, keepdims=True) if scatter else x[None]
        out.append(lax.dynamic_update_index_in_dim(land, own, me, 0))
    return out


def _wire_shards(p, l):
    return [p[k][l].astype(WIRE) if SHARDED[k][1] else p[k][l] for k in SHARDED]


def _full_weights(got, p):
    full = {}
    for k, g in zip(SHARDED, got):
        axis, shape = SHARDED[k][0] - 1, p[k].shape[1:]
        g = jnp.moveaxis(g, 0, axis)
        full[k] = g.reshape(shape[:axis] + (N_DEV * shape[axis],) + shape[axis + 1:])
    return full


def _grad_parts(grads, p):
    parts = []
    for k in SHARDED:
        axis, g = SHARDED[k][0] - 1, grads[k]
        g = g.reshape(g.shape[:axis] + (N_DEV, p[k].shape[1:][axis]) + g.shape[axis + 1:])
        parts.append(jnp.moveaxis(g, axis, 0).astype(WIRE))
    return parts


def _allreduce_small(vals):
    metas = [(k, v.shape, v.size) for k, v in vals.items()]
    flat = jnp.concatenate([v.reshape(-1) for v in vals.values()])
    got, = _exchange([_to_rows(flat, 8)], scatter=False, name="gather_small_grads")
    summed = _sum_slots(got, name="sum_small_grads").reshape(-1)
    out, off = {}, 0
    for k, shape, n in metas:
        out[k] = summed[off:off + n].reshape(shape)
        off += n
    return out


def _interleave(a, g, tile):
    C = a.shape[-1]
    n = C // tile
    lead = a.shape[:-1]
    return jnp.concatenate([a.reshape(lead + (n, 1, tile)), g.reshape(lead + (n, 1, tile))], axis=-2).reshape(lead + (2 * C,))


def _deinterleave(w, tile):
    C = w.shape[-1] // 2
    n = C // tile
    lead = w.shape[:-1]
    r = w.reshape(lead + (n, 2, tile))
    return r[..., 0, :].reshape(lead + (C,)), r[..., 1, :].reshape(lead + (C,))


def _row(v):
    return v.reshape(1, -1).astype(F32)


def _lanes(v):
    return jnp.pad(v.astype(F32), (0, LANES - v.shape[0])).reshape(1, LANES)


def _layer_weights(p, full, l, dims):
    D, SX, XBC, H, FF = dims
    gt, ft = _tile(D, GLU_TILE), _tile(FF, FFN_TILE)
    w_in = full['w_in']
    o = 0
    seg = {}
    for nm, n in (('a', D), ('g', D), ('z', SX), ('xbc', XBC), ('dt', H), ('ga', D), ('gb', D)):
        seg[nm] = w_in[:, o:o + n]
        o += n
    w_main = jnp.concatenate([_interleave(seg['a'], seg['g'], gt), seg['z'], seg['xbc'], seg['ga'], seg['gb']], axis=1)
    w_dt = jnp.pad(seg['dt'], ((0, 0), (0, LANES - H)))
    up = full['w_ffn_up']
    fw = full['ffn_dw_w']
    fb = p['ffn_dw_b'][l]
    a_head = -jnp.exp(p['ssm_a_log'][l].astype(F32))
    hr = -(-H // 8) * 8
    return dict(
        w_main=w_main, w_dt=w_dt,
        conv_w=full['conv_dw_w'], conv_b=_row(p['conv_dw_b'][l]), conv_g=_row(p['conv_ln_g'][l]), conv_beta=_row(p['conv_ln_b'][l]),
        w_conv_out=full['w_conv_out'],
        ssm_w=full['ssm_conv_w'], ssm_b=_row(p['ssm_conv_b'][l]),
        dt_bias=_lanes(p['ssm_dt_bias'][l]), a_col=_lanes(a_head),
        a_row=jnp.pad(a_head, (0, hr - H)).reshape(hr, 1), d_col=_lanes(p['ssm_d'][l]),
        norm_w=_row(p['ssm_norm_w'][l]), w_ssm_out=full['w_ssm_out'], w_o=full['w_o'],
        ln1_g=_row(p['ln1_g'][l]), ln1_b=_row(p['ln1_b'][l]),
        w_up=_interleave(up[:, :FF], up[:, FF:], ft),
        ffn_w=_interleave(fw[:, :FF], fw[:, FF:], ft), ffn_b=_row(_interleave(fb[:FF], fb[FF:], ft)),
        w_down=full['w_ffn_down'], ln2_g=_row(p['ln2_g'][l]), ln2_b=_row(p['ln2_b'][l]))


def _layer_fwd(h, W, l, dims):
    D, SX, XBC, H, FF = dims
    o_z, o_xbc, o_gate = 2 * D, 2 * D + SX, 2 * D + SX + XBC
    hr = W['a_row'].shape[0]
    t = f"l{l}"
    u = _mm(h, W['w_main'], mode='nn', out_dtype=F32, name=t + "_in_proj")
    udt = _mm(h, W['w_dt'], mode='nn', out_dtype=F32, name=t + "_dt_proj")
    v1, v3 = _conf_fwd(u, W['conv_w'], W['conv_b'], W['conv_g'], W['conv_beta'], D=D, name=t + "_conf_fwd")
    ya = _mm(v3, W['w_conv_out'], mode='nn', out_dtype=F32, name=t + "_conv_out")
    xbc = _conv_act_fwd(u, o_xbc, W['ssm_w'], W['ssm_b'], ffn=False, name=t + "_ssm_conv")
    dt = _dt_fwd(udt, W['dt_bias'], H=H, name=t + "_dt")
    dtT = jnp.pad(dt[:, :H].T, ((0, hr - H), (0, 0)))
    y, hst = _ssd_fwd(xbc, dt, dtT, W['a_col'], W['a_row'], W['d_col'], H=H, name=t + "_ssd_fwd")
    yn = _rms_fwd(y, u, o_z, W['norm_w'], name=t + "_rms_fwd")
    yb = _mm(yn, W['w_ssm_out'], mode='nn', out_dtype=F32, name=t + "_ssm_out")
    m = _gate_fwd(u, o_gate, ya, yb, name=t + "_gate_fwd")
    h1, s1 = _mm_res_ln(m, W['w_o'], h, W['ln1_g'], W['ln1_b'], name=t + "_mix_ln1")
    uf = _mm(h1, W['w_up'], mode='nn', out_dtype=F32, name=t + "_ffn_up")
    act = _conv_act_fwd(uf, 0, W['ffn_w'], W['ffn_b'], ffn=True, name=t + "_ffn_conv")
    h2, s2 = _mm_res_ln(act, W['w_down'], h1, W['ln2_g'], W['ln2_b'], name=t + "_ffn_down_ln2")
    saved = dict(h=h, u=u, udt=udt, v1=v1, v3=v3, ya=ya, xbc=xbc, dt=dt, dtT=dtT, y=y, hst=hst, yn=yn, yb=yb, m=m,
                 h1=h1, s1=s1, uf=uf, act=act, s2=s2)
    return h2, saved


def _layer_bwd(dh2, W, S, l, dims):
    D, SX, XBC, H, FF = dims
    o_z, o_xbc, o_gate = 2 * D, 2 * D + SX, 2 * D + SX + XBC
    t = f"l{l}"
    g = {}
    ds2, g['ln2_g'], g['ln2_b'] = _ln_bwd(S['s2'], dh2, W['ln2_g'], W['ln2_b'], silu=False, name=t + "_ln2_bwd")
    g['w_down'] = _mm(S['act'], ds2, mode='tn', out_dtype=F32, name=t + "_dw_down")
    dact = _mm(ds2, W['w_down'], mode='nt', out_dtype=F32, name=t + "_dact")
    duf, g['ffn_w'], g['ffn_b'] = _conv_act_bwd(S['uf'], 0, W['ffn_w'], W['ffn_b'], dact, ffn=True, name=t + "_ffn_conv_bwd")
    g['w_up'] = _mm(S['h1'], duf, mode='tn', out_dtype=F32, name=t + "_dw_up")
    dh1 = _mm(duf, W['w_up'], mode='nt', out_dtype=F32, res=ds2, res_scale=DN_ALPHA, name=t + "_dh1")
    ds1, g['ln1_g'], g['ln1_b'] = _ln_bwd(S['s1'], dh1, W['ln1_g'], W['ln1_b'], silu=False, name=t + "_ln1_bwd")
    g['w_o'] = _mm(S['m'], ds1, mode='tn', out_dtype=F32, name=t + "_dw_o")
    dm = _mm(ds1, W['w_o'], mode='nt', out_dtype=F32, name=t + "_dm")
    dya, dyb, dgate = _gate_bwd(S['u'], o_gate, S['ya'], S['yb'], dm, name=t + "_gate_bwd")
    g['w_conv_out'] = _mm(S['v3'], dya, mode='tn', out_dtype=F32, name=t + "_dw_conv_out")
    dv3 = _mm(dya, W['w_conv_out'], mode='nt', out_dtype=F32, name=t + "_dv3")
    dv1, g['conv_g'], g['conv_beta'] = _ln_bwd(S['v1'], dv3, W['conv_g'], W['conv_beta'], silu=True, name=t + "_conv_ln_bwd")
    dglu, g['conv_w'], g['conv_b'] = _glu_conv_bwd(dv1, S['u'], W['conv_w'], name=t + "_conf_conv_bwd")
    g['w_ssm_out'] = _mm(S['yn'], dyb, mode='tn', out_dtype=F32, name=t + "_dw_ssm_out")
    dyn = _mm(dyb, W['w_ssm_out'], mode='nt', out_dtype=F32, name=t + "_dyn")
    dy, dz, g['norm_w'] = _rms_bwd(S['y'], S['u'], o_z, W['norm_w'], dyn, name=t + "_rms_bwd")
    dxbc_c, da, ddtx, g['d'] = _ssd_bwd(S['xbc'], S['dt'], S['dtT'], W['a_col'], W['a_row'], W['d_col'], S['hst'], dy,
                                        H=H, name=t + "_ssd_bwd")
    ddt_raw, g['dt_bias'], g['a_log'] = _dt_bwd(da, ddtx, S['dt'], S['udt'], W['dt_bias'], W['a_col'], H=H, name=t + "_dt_bwd")
    dxbc, g['ssm_w'], g['ssm_b'] = _conv_act_bwd(S['u'], o_xbc, W['ssm_w'], W['ssm_b'], dxbc_c, ffn=False, name=t + "_ssm_conv_bwd")
    du = jnp.concatenate([dglu, dz, dxbc, dgate], axis=1)
    g['w_main'] = _mm(S['h'], du, mode='tn', out_dtype=F32, name=t + "_dw_main")
    g['w_dt'] = _mm(S['h'], ddt_raw, mode='tn', out_dtype=F32, name=t + "_dw_dt")
    dh = _mm(du, W['w_main'], mode='nt', out_dtype=F32, res=ds1, res_scale=DN_ALPHA, name=t + "_dh_main")
    dh = _mm(ddt_raw, W['w_dt'], mode='nt', out_dtype=F32, res=dh, res_scale=1.0, name=t + "_dh_dt")
    return dh, g


def _layer_grads_to_params(g, dims):
    D, SX, XBC, H, FF = dims
    gt, ft = _tile(D, GLU_TILE), _tile(FF, FFN_TILE)
    wm = g['w_main']
    da, dg = _deinterleave(wm[:, :2 * D], gt)
    rest = wm[:, 2 * D:]
    w_in = jnp.concatenate([da, dg, rest[:, :SX + XBC], g['w_dt'][:, :H], rest[:, SX + XBC:]], axis=1)
    fwg, fwv = _deinterleave(g['ffn_w'], ft)
    fbg, fbv = _deinterleave(g['ffn_b'], ft)
    upg, upv = _deinterleave(g['w_up'], ft)
    return dict(
        w_in=w_in, conv_dw_w=g['conv_w'], conv_dw_b=g['conv_b'][0], conv_ln_g=g['conv_g'][0], conv_ln_b=g['conv_beta'][0],
        w_conv_out=g['w_conv_out'], ssm_conv_w=g['ssm_w'], ssm_conv_b=g['ssm_b'][0],
        ssm_dt_bias=g['dt_bias'][0, :H], ssm_a_log=g['a_log'][0, :H], ssm_d=g['d'][0, :H], ssm_norm_w=g['norm_w'][0],
        w_ssm_out=g['w_ssm_out'], w_o=g['w_o'], ln1_g=g['ln1_g'][0], ln1_b=g['ln1_b'][0],
        w_ffn_up=jnp.concatenate([upg, upv], axis=1), ffn_dw_w=jnp.concatenate([fwg, fwv], axis=1),
        ffn_dw_b=jnp.concatenate([fbg, fbv], axis=1)[0], w_ffn_down=g['w_down'], ln2_g=g['ln2_g'][0], ln2_b=g['ln2_b'][0])


def kernel(x, ln_in_g, ln_in_b, w_in, conv_dw_w, conv_dw_b, conv_ln_g, conv_ln_b, w_conv_out, ssm_conv_w, ssm_conv_b, ssm_dt_bias, ssm_a_log, ssm_d, ssm_norm_w, w_ssm_out, w_o, ln1_g, ln1_b, w_ffn_up, ffn_dw_w, ffn_dw_b, w_ffn_down, ln2_g, ln2_b, loss_target, m_ln_in_g, m_ln_in_b, m_w_in, m_conv_dw_w, m_conv_dw_b, m_conv_ln_g, m_conv_ln_b, m_w_conv_out, m_ssm_conv_w, m_ssm_conv_b, m_ssm_dt_bias, m_ssm_a_log, m_ssm_d, m_ssm_norm_w, m_w_ssm_out, m_w_o, m_ln1_g, m_ln1_b, m_w_ffn_up, m_ffn_dw_w, m_ffn_dw_b, m_w_ffn_down, m_ln2_g, m_ln2_b, v_ln_in_g, v_ln_in_b, v_w_in, v_conv_dw_w, v_conv_dw_b, v_conv_ln_g, v_conv_ln_b, v_w_conv_out, v_ssm_conv_w, v_ssm_conv_b, v_ssm_dt_bias, v_ssm_a_log, v_ssm_d, v_ssm_norm_w, v_w_ssm_out, v_w_o, v_ln1_g, v_ln1_b, v_w_ffn_up, v_ffn_dw_w, v_ffn_dw_b, v_w_ffn_down, v_ln2_g, v_ln2_b):
    weights = (ln_in_g, ln_in_b, w_in, conv_dw_w, conv_dw_b, conv_ln_g, conv_ln_b, w_conv_out, ssm_conv_w, ssm_conv_b,
               ssm_dt_bias, ssm_a_log, ssm_d, ssm_norm_w, w_ssm_out, w_o, ln1_g, ln1_b, w_ffn_up, ffn_dw_w, ffn_dw_b,
               w_ffn_down, ln2_g, ln2_b)
    moments_m = (m_ln_in_g, m_ln_in_b, m_w_in, m_conv_dw_w, m_conv_dw_b, m_conv_ln_g, m_conv_ln_b, m_w_conv_out,
                 m_ssm_conv_w, m_ssm_conv_b, m_ssm_dt_bias, m_ssm_a_log, m_ssm_d, m_ssm_norm_w, m_w_ssm_out, m_w_o,
                 m_ln1_g, m_ln1_b, m_w_ffn_up, m_ffn_dw_w, m_ffn_dw_b, m_w_ffn_down, m_ln2_g, m_ln2_b)
    moments_v = (v_ln_in_g, v_ln_in_b, v_w_in, v_conv_dw_w, v_conv_dw_b, v_conv_ln_g, v_conv_ln_b, v_w_conv_out,
                 v_ssm_conv_w, v_ssm_conv_b, v_ssm_dt_bias, v_ssm_a_log, v_ssm_d, v_ssm_norm_w, v_w_ssm_out, v_w_o,
                 v_ln1_g, v_ln1_b, v_w_ffn_up, v_ffn_dw_w, v_ffn_dw_b, v_w_ffn_down, v_ln2_g, v_ln2_b)
    p = dict(zip(PARAMS, weights))
    pm = dict(zip(PARAMS, moments_m))
    pv = dict(zip(PARAMS, moments_v))

    T, D = x.shape[1], x.shape[2]
    SX = w_ssm_out.shape[1] * N_DEV
    XBC = ssm_conv_b.shape[-1]
    H = ssm_d.shape[-1]
    FF = ffn_dw_b.shape[-1] // 2
    dims = (D, SX, XBC, H, FF)
    depth = w_in.shape[0]

    got = _exchange(_wire_shards(p, 0), scatter=False, name="gather_l0")
    pending, tok = [], jnp.zeros((8, LANES), F32)
    for l in range(1, depth):
        shards, got = lax.optimization_barrier((_wire_shards(p, l), got))
        handle, t = _exchange_start(shards, scatter=False, name=f"gather_l{l}_start")
        pending.append((handle, shards))
        tok = tok + t
    layers = [_layer_weights(p, _full_weights(got, p), 0, dims)]

    xs = x.reshape(T, D)
    h = _ln_fwd(xs, _row(ln_in_g) + tok[0:1, 0:1], _row(ln_in_b), name="ln_in_fwd")
    saved = []
    for l in range(depth):
        if l > 0:
            handle, shards = pending[l - 1]
            got = _with_own_slot(_exchange_wait(handle, h, scatter=False, name=f"gather_l{l}_wait"), shards, scatter=False)
            layers.append(_layer_weights(p, _full_weights(got, p), l, dims))
        h, s = _layer_fwd(h, layers[l], l, dims)
        saved.append(s)
    loss_part, dh = _loss_fwd_bwd(h, loss_target.reshape(T, D), name="loss")

    layer_grads = [None] * depth
    pending, tok = [], jnp.zeros((8, LANES), F32)
    for l in reversed(range(depth)):
        W = dict(layers[l], ln2_g=layers[l]['ln2_g'] + tok[0:1, 0:1])
        dh, g = _layer_bwd(dh, W, saved[l], l, dims)
        layer_grads[l] = _layer_grads_to_params(g, dims)
        if l > 0:
            parts = _grad_parts(layer_grads[l], p)
            handle, tok = _exchange_start(parts, scatter=True, name=f"scatter_l{l}_start")
            pending.append((l, handle, parts))
    grad_x, dg_in, db_in = _ln_bwd(xs, dh, _row(ln_in_g), _row(ln_in_b), silu=False, name="ln_in_bwd")

    arrived = {}
    after = grad_x
    for l, handle, parts in pending:
        lands = _exchange_wait(handle, after, scatter=True, name=f"scatter_l{l}_wait")
        arrived[l] = _with_own_slot(lands, parts, scatter=True)
        after = arrived[l][0]
    parts0, after = lax.optimization_barrier((_grad_parts(layer_grads[0], p), after))
    arrived[0] = _exchange(parts0, scatter=True, name="scatter_l0")
    small = {k: jnp.stack([layer_grads[l][k] for l in range(depth)]) for k in layer_grads[0] if k not in SHARDED}
    small['ln_in_g'], small['ln_in_b'], small['loss'] = dg_in[0], db_in[0], loss_part
    grads = _allreduce_small(small)
    loss = grads.pop('loss').reshape(())

    delta, new_m, new_v = {}, {}, {}
    for i, k in enumerate(SHARDED):
        per_layer = [_sum_adamw(arrived[l][i], p[k], pm[k], pv[k], l, name=f"sum_adamw_l{l}_{k}") for l in range(depth)]
        grads[k], delta[k], new_m[k], new_v[k] = (jnp.stack([r[j] for r in per_layer]) for j in range(4))
    rest = [k for k in PARAMS if k not in SHARDED]
    flat = lambda d: _to_rows(jnp.concatenate([d[k].reshape(-1) for k in rest]), 8)
    d_, m_, v_ = _adamw(flat(p), flat(grads), flat(pm), flat(pv), name="adamw_small")
    off = 0
    for k in rest:
        n, shp = p[k].size, p[k].shape
        delta[k] = d_.reshape(-1)[off:off + n].reshape(shp)
        new_m[k] = m_.reshape(-1)[off:off + n].reshape(shp)
        new_v[k] = v_.reshape(-1)[off:off + n].reshape(shp)
        off += n

    return (loss, grad_x.reshape(x.shape), *[grads[k] for k in PARAMS], *[delta[k] for k in PARAMS],
            *[new_m[k] for k in PARAMS], *[new_v[k] for k in PARAMS])
```

```python
import math

import jax
import jax.numpy as jnp
from jax import lax
from jax.experimental import pallas as pl
from jax.experimental.pallas import tpu as pltpu

F32 = jnp.float32
BF16 = jnp.bfloat16
WIRE = jnp.bfloat16
HIGHEST = lax.Precision.HIGHEST

DEPTH = 2
SSM_STATE = 128
SSM_CHUNK = 128
SSM_GROUPS = 4
SSM_HEAD_DIM = 64
DN_ALPHA = (2 * DEPTH) ** 0.25
LN_EPS = 1e-5
RMS_EPS = 1e-5
ADAM_LR = 0.001
ADAM_B1 = 0.9
ADAM_B2 = 0.999
ADAM_EPS = 1e-08
ADAM_WD = 0.01
ADAM_STEP = 10

N_DEV = 8
LANES = 128
VMEM_LIMIT = 48 * 1024 * 1024
GLU_TILE = 512
FFN_TILE = 1408
CONV_ROWS = 256

PARAMS = ['ln_in_g', 'ln_in_b', 'w_in', 'conv_dw_w', 'conv_dw_b', 'conv_ln_g', 'conv_ln_b', 'w_conv_out',
          'ssm_conv_w', 'ssm_conv_b', 'ssm_dt_bias', 'ssm_a_log', 'ssm_d', 'ssm_norm_w', 'w_ssm_out', 'w_o',
          'ln1_g', 'ln1_b', 'w_ffn_up', 'ffn_dw_w', 'ffn_dw_b', 'w_ffn_down', 'ln2_g', 'ln2_b']
SHARDED = {'w_in': (2, True), 'conv_dw_w': (2, False), 'w_conv_out': (1, True), 'ssm_conv_w': (2, False),
           'w_ssm_out': (1, True), 'w_o': (1, True), 'w_ffn_up': (2, True), 'ffn_dw_w': (2, False),
           'w_ffn_down': (1, True)}


def _sds(shape, dtype):
    return jax.ShapeDtypeStruct(tuple(shape), dtype)


def _tile(dim, pref):
    return pref if dim % pref == 0 else dim


def _pick(dim, pref, mult):
    best = None
    for t in range(mult, min(dim, pref) + 1, mult):
        if dim % t == 0:
            best = t
    return best or dim


def _cp(*sem):
    return pltpu.CompilerParams(dimension_semantics=sem, vmem_limit_bytes=VMEM_LIMIT)


def _sigmoid(x):
    return 1.0 / (1.0 + jnp.exp(-x))


def _silu_and_grad(x):
    s = _sigmoid(x)
    return x * s, s * (1.0 + x * (1.0 - s))


def _ln_stats(s):
    mu = jnp.mean(s, axis=-1, keepdims=True)
    xc = s - mu
    var = jnp.mean(xc * xc, axis=-1, keepdims=True)
    rstd = lax.rsqrt(var + LN_EPS)
    return xc * rstd, rstd


def _dot(a, b, dims):
    return lax.dot_general(a, b, (dims, ((), ())), preferred_element_type=F32)


_NN = ((1,), (0,))
_NT = ((1,), (1,))
_TN = ((0,), (0,))


def _mm(a, b, *, mode, out_dtype, name, res=None, res_scale=1.0, tm=1024, tn=1408, tk=1408):
    if mode == 'nn':
        (M, K), (_, N) = a.shape, b.shape
    elif mode == 'nt':
        (M, K), (N, _) = a.shape, b.shape
    else:
        (K, M), (_, N) = a.shape, b.shape
        tm, tk = 1408, 512
    tm, tn, tk = _pick(M, tm, 8), _pick(N, tn, LANES), _pick(K, tk, LANES)
    nk = K // tk
    dims = {'nn': _NN, 'nt': _NT, 'tn': _TN}[mode]

    def body(*refs):
        if res is None:
            a_ref, b_ref, o_ref = refs[:3]
            r_ref = None
        else:
            a_ref, b_ref, r_ref, o_ref = refs[:4]
        p = _dot(a_ref[...].astype(BF16), b_ref[...].astype(BF16), dims)

        def finish(acc):
            if r_ref is not None:
                acc = acc + res_scale * r_ref[...]
            o_ref[...] = acc.astype(out_dtype)

        if nk == 1:
            finish(p)
        else:
            acc_ref = refs[-1]
            k = pl.program_id(2)

            @pl.when(k == 0)
            def _():
                acc_ref[...] = p

            @pl.when(k > 0)
            def _():
                acc_ref[...] += p

            @pl.when(k == nk - 1)
            def _():
                finish(acc_ref[...])

    if mode == 'nn':
        a_spec = pl.BlockSpec((tm, tk), lambda i, j, k: (i, k))
        b_spec = pl.BlockSpec((tk, tn), lambda i, j, k: (k, j))
    elif mode == 'nt':
        a_spec = pl.BlockSpec((tm, tk), lambda i, j, k: (i, k))
        b_spec = pl.BlockSpec((tn, tk), lambda i, j, k: (j, k))
    else:
        a_spec = pl.BlockSpec((tk, tm), lambda i, j, k: (k, i))
        b_spec = pl.BlockSpec((tk, tn), lambda i, j, k: (k, j))
    o_spec = pl.BlockSpec((tm, tn), lambda i, j, k: (i, j))
    in_specs = [a_spec, b_spec] + ([o_spec] if res is not None else [])
    args = (a, b) + ((res,) if res is not None else ())
    return pl.pallas_call(
        body, name=name, out_shape=_sds((M, N), out_dtype), grid=(M // tm, N // tn, nk),
        in_specs=in_specs, out_specs=o_spec,
        scratch_shapes=[pltpu.VMEM((tm, tn), F32)] if nk > 1 else [],
        compiler_params=_cp("parallel", "parallel", "arbitrary"))(*args)


def _mm_res_ln(a, w, res, g, b, *, name, tm=512, tk=1408):
    (M, K), (_, N) = a.shape, w.shape
    tm, tk = _pick(M, tm, 8), _pick(K, tk, LANES)
    nk = K // tk

    def body(a_ref, w_ref, r_ref, g_ref, b_ref, h_ref, s_ref, acc_ref):
        k = pl.program_id(1)
        p = _dot(a_ref[...].astype(BF16), w_ref[...].astype(BF16), _NN)

        @pl.when(k == 0)
        def _():
            acc_ref[...] = p

        @pl.when(k > 0)
        def _():
            acc_ref[...] += p

        @pl.when(k == nk - 1)
        def _():
            s = DN_ALPHA * r_ref[...] + acc_ref[...]
            xhat, _ = _ln_stats(s)
            s_ref[...] = s
            h_ref[...] = xhat * g_ref[...] + b_ref[...]

    row = pl.BlockSpec((tm, N), lambda i, k: (i, 0))
    vec = pl.BlockSpec((1, N), lambda i, k: (0, 0))
    return pl.pallas_call(
        body, name=name, out_shape=(_sds((M, N), F32), _sds((M, N), F32)), grid=(M // tm, nk),
        in_specs=[pl.BlockSpec((tm, tk), lambda i, k: (i, k)), pl.BlockSpec((tk, N), lambda i, k: (k, 0)), row, vec, vec],
        out_specs=(row, row), scratch_shapes=[pltpu.VMEM((tm, N), F32)],
        compiler_params=_cp("parallel", "arbitrary"))(a, w, res, g, b)


def _ln_fwd(x, g, b, *, name):
    T, D = x.shape
    tt = _tile(T, 512)

    def body(x_ref, g_ref, b_ref, o_ref):
        xhat, _ = _ln_stats(x_ref[...])
        o_ref[...] = xhat * g_ref[...] + b_ref[...]

    row = pl.BlockSpec((tt, D), lambda i: (i, 0))
    vec = pl.BlockSpec((1, D), lambda i: (0, 0))
    return pl.pallas_call(body, name=name, out_shape=_sds((T, D), F32), grid=(T // tt,),
                          in_specs=[row, vec, vec], out_specs=row, compiler_params=_cp("parallel"))(x, g, b)


def _ln_bwd(s, dy, g, b, *, silu, name):
    T, D = s.shape
    tt = _tile(T, 512)

    def body(s_ref, dy_ref, g_ref, b_ref, ds_ref, dg_ref, db_ref):
        i = pl.program_id(0)
        xhat, rstd = _ln_stats(s_ref[...])
        gg = g_ref[...]
        dyl = dy_ref[...]
        if silu:
            _, dsilu = _silu_and_grad(xhat * gg + b_ref[...])
            dyl = dyl * dsilu
        dxh = dyl * gg
        m1 = jnp.mean(dxh, axis=-1, keepdims=True)
        m2 = jnp.mean(dxh * xhat, axis=-1, keepdims=True)
        ds_ref[...] = rstd * (dxh - m1 - xhat * m2)

        @pl.when(i == 0)
        def _():
            dg_ref[...] = jnp.zeros_like(dg_ref)
            db_ref[...] = jnp.zeros_like(db_ref)

        dg_ref[...] += jnp.sum(dyl * xhat, axis=0, keepdims=True)
        db_ref[...] += jnp.sum(dyl, axis=0, keepdims=True)

    row = pl.BlockSpec((tt, D), lambda i: (i, 0))
    vec = pl.BlockSpec((1, D), lambda i: (0, 0))
    return pl.pallas_call(body, name=name, out_shape=(_sds((T, D), F32), _sds((1, D), F32), _sds((1, D), F32)),
                          grid=(T // tt,), in_specs=[row, row, vec, vec], out_specs=(row, vec, vec),
                          compiler_params=_cp("arbitrary"))(s, dy, g, b)


def _loss_fwd_bwd(h, tgt, *, name):
    T, D = h.shape
    tt = _tile(T, 512)

    def body(h_ref, t_ref, dh_ref, l_ref):
        i = pl.program_id(0)
        e = h_ref[...] - t_ref[...]
        dh_ref[...] = e * (1.0 / D)

        @pl.when(i == 0)
        def _():
            l_ref[...] = jnp.zeros_like(l_ref)

        part = jnp.sum(jnp.sum(e * e, axis=1, keepdims=True), axis=0, keepdims=True) * (0.5 / D)
        l_ref[...] += jnp.broadcast_to(part, l_ref.shape)

    row = pl.BlockSpec((tt, D), lambda i: (i, 0))
    one = pl.BlockSpec((8, LANES), lambda i: (0, 0))
    dh, l = pl.pallas_call(body, name=name, out_shape=(_sds((T, D), F32), _sds((8, LANES), F32)), grid=(T // tt,),
                           in_specs=[row, row], out_specs=(row, one), compiler_params=_cp("arbitrary"))(h, tgt)
    return l[0:1, 0:1], dh


def _halo_rows(k):
    return 32 if k > 9 else 8


def _shifted(ext, K, first):
    rolled = {0: ext}
    out = []
    for k in range(K):
        r = (first + k) % 8
        if r not in rolled:
            rolled[r] = pltpu.roll(ext, ext.shape[0] - r, 0)
        out.append((rolled[r], first + k - r))
    return out


def _taps(ext, w_ref, sl, K, first, n, reverse=False):
    acc = None
    for k, (z, base) in enumerate(_shifted(ext, K, first)):
        kw = K - 1 - k if reverse else k
        term = w_ref[kw:kw + 1, sl] * z[base:base + n]
        acc = term if acc is None else acc + term
    return acc


def _tap_sums(d, ext, dw_ref, sl, K, first):
    n = d.shape[0]
    for k, (z, base) in enumerate(_shifted(ext, K, first)):
        dw_ref[k:k + 1, sl] += jnp.sum(d * z[base:base + n], axis=0, keepdims=True)


def _strip_width(width):
    return LANES if width % LANES == 0 else width


def _for_strips(width, fn):
    sw = _strip_width(width)

    def step(s, carry):
        fn(pl.ds(pl.multiple_of(s * sw, sw), sw), s)
        return carry

    lax.fori_loop(0, width // sw, step, 0)


def _prev_rows(tt, hb):
    return lambda t: jnp.maximum(t * (tt // hb) - 1, 0)


def _next_rows(tt, hb, T):
    return lambda t: jnp.minimum((t + 1) * (tt // hb), T // hb - 1)


def _conf_fwd(u, w, b, g, beta, *, D, name):
    T = u.shape[0]
    K = w.shape[0]
    tt, hb, tc = _tile(T, CONV_ROWS), _halo_rows(K), _tile(D, GLU_TILE)
    sw = _strip_width(tc)
    per_tile = tc // sw
    prev = _prev_rows(tt, hb)

    def body(u_ref, uh_ref, w_ref, b_ref, g_ref, beta_ref, v1_ref, v3_ref):
        i = pl.program_id(0)

        def strip(sl, s):
            a0 = (s // per_tile) * (2 * tc) + (s % per_tile) * sw
            a_sl, g_sl = pl.ds(pl.multiple_of(a0, sw), sw), pl.ds(pl.multiple_of(a0 + tc, sw), sw)
            halo = jnp.where(i > 0, uh_ref[:, a_sl] * _sigmoid(uh_ref[:, g_sl]), 0.0)
            ext = jnp.concatenate([halo, u_ref[:, a_sl] * _sigmoid(u_ref[:, g_sl])], axis=0)
            v1_ref[:, sl] = _taps(ext, w_ref, sl, K, hb - (K - 1), tt) + b_ref[:, sl]

        _for_strips(D, strip)
        xhat, _ = _ln_stats(v1_ref[...])
        v2 = xhat * g_ref[...] + beta_ref[...]
        v3_ref[...] = (v2 * _sigmoid(v2)).astype(BF16)

    row = pl.BlockSpec((tt, D), lambda t: (t, 0))
    vec = pl.BlockSpec((1, D), lambda t: (0, 0))
    return pl.pallas_call(
        body, name=name, out_shape=(_sds((T, D), F32), _sds((T, D), BF16)), grid=(T // tt,),
        in_specs=[pl.BlockSpec((tt, 2 * D), lambda t: (t, 0)), pl.BlockSpec((hb, 2 * D), lambda t: (prev(t), 0)),
                  pl.BlockSpec((K, D), lambda t: (0, 0)), vec, vec, vec],
        out_specs=(row, row), compiler_params=_cp("parallel"))(u, u, w, b, g, beta)


def _conv_act_fwd(x, off, w, b, *, ffn, name):
    T = x.shape[0]
    K, Cw = w.shape
    tt, hb = _tile(T, CONV_ROWS), _halo_rows(K)
    tc = _tile(Cw // 2, FFN_TILE) if ffn else _pick(math.gcd(Cw, off), 1536, LANES)
    xw = 2 * tc if ffn else tc
    assert off % xw == 0
    ob = off // xw
    prev = _prev_rows(tt, hb)

    def body(x_ref, xh_ref, w_ref, b_ref, o_ref):
        first = pl.program_id(0) == 0

        def pre_of(sl):
            ext = jnp.concatenate([jnp.where(first, 0.0, xh_ref[:, sl]), x_ref[:, sl]], axis=0)
            return _taps(ext, w_ref, sl, K, hb - (K - 1), tt) + b_ref[:, sl]

        def strip(sl, s):
            if ffn:
                gate = pre_of(sl)
                val = pre_of(pl.ds(pl.multiple_of(tc + s * sw, sw), sw))
                o_ref[:, sl] = (gate * _sigmoid(gate) * val).astype(BF16)
            else:
                pre = pre_of(sl)
                o_ref[:, sl] = pre * _sigmoid(pre)

        _for_strips(tc, strip)

    sw = _strip_width(tc)
    return pl.pallas_call(
        body, name=name, out_shape=_sds((T, Cw // 2), BF16) if ffn else _sds((T, Cw), F32), grid=(T // tt, Cw // xw),
        in_specs=[pl.BlockSpec((tt, xw), lambda t, c: (t, c + ob)), pl.BlockSpec((hb, xw), lambda t, c: (prev(t), c + ob)),
                  pl.BlockSpec((K, xw), lambda t, c: (0, c)), pl.BlockSpec((1, xw), lambda t, c: (0, c))],
        out_specs=pl.BlockSpec((tt, tc), lambda t, c: (t, c)),
        compiler_params=_cp("parallel", "parallel"))(x, x, w, b)


def _conv_act_bwd(x, off, w, b, dout, *, ffn, name):
    T = x.shape[0]
    K, Cw = w.shape
    tt, hb = _tile(T, CONV_ROWS), _halo_rows(K)
    tc = _tile(Cw // 2, FFN_TILE) if ffn else _pick(math.gcd(Cw, off), 1536, LANES)
    xw = 2 * tc if ffn else tc
    assert off % xw == 0
    ob = off // xw
    nt = T // tt
    prev, nxt = _prev_rows(tt, hb), _next_rows(tt, hb, T)

    sw = _strip_width(tc)

    def body(x_ref, xp_ref, xn_ref, d_ref, dn_ref, w_ref, b_ref, dx_ref, dw_ref, db_ref):
        t = pl.program_id(1)

        @pl.when(t == 0)
        def _():
            dw_ref[...] = jnp.zeros_like(dw_ref)
            db_ref[...] = jnp.zeros_like(db_ref)

        def ext_of(sl):
            return jnp.concatenate([jnp.where(t == 0, 0.0, xp_ref[:, sl]), x_ref[:, sl], xn_ref[:, sl]], axis=0)

        def pre_of(ext, sl):
            return _taps(ext, w_ref, sl, K, hb - (K - 1), tt + hb) + b_ref[:, sl]

        def finish(ext, dpre, sl):
            dx_ref[:, sl] = _taps(dpre, w_ref, sl, K, 0, tt, reverse=True).astype(BF16)
            dp = dpre[0:tt]
            _tap_sums(dp, ext, dw_ref, sl, K, hb - (K - 1))
            db_ref[:, sl] += jnp.sum(dp, axis=0, keepdims=True)

        def strip(sl, s):
            d = jnp.concatenate([d_ref[:, sl], jnp.where(t == nt - 1, 0.0, dn_ref[:, sl])], axis=0)
            if ffn:
                vsl = pl.ds(pl.multiple_of(tc + s * sw, sw), sw)
                eg, ev = ext_of(sl), ext_of(vsl)
                sg, dsg = _silu_and_grad(pre_of(eg, sl))
                val = pre_of(ev, vsl)
                finish(eg, d * val * dsg, sl)
                finish(ev, d * sg, vsl)
            else:
                ext = ext_of(sl)
                finish(ext, d * _silu_and_grad(pre_of(ext, sl))[1], sl)

        _for_strips(tc, strip)

    return pl.pallas_call(
        body, name=name, out_shape=(_sds((T, Cw), BF16), _sds((K, Cw), F32), _sds((1, Cw), F32)),
        grid=(Cw // xw, nt),
        in_specs=[pl.BlockSpec((tt, xw), lambda c, t: (t, c + ob)), pl.BlockSpec((hb, xw), lambda c, t: (prev(t), c + ob)),
                  pl.BlockSpec((hb, xw), lambda c, t: (nxt(t), c + ob)),
                  pl.BlockSpec((tt, tc), lambda c, t: (t, c)), pl.BlockSpec((hb, tc), lambda c, t: (nxt(t), c)),
                  pl.BlockSpec((K, xw), lambda c, t: (0, c)), pl.BlockSpec((1, xw), lambda c, t: (0, c))],
        out_specs=(pl.BlockSpec((tt, xw), lambda c, t: (t, c)), pl.BlockSpec((K, xw), lambda c, t: (0, c)),
                   pl.BlockSpec((1, xw), lambda c, t: (0, c))),
        compiler_params=_cp("parallel", "arbitrary"))(x, x, x, dout, dout, w, b)


def _glu_conv_bwd(dpre, u, w, *, name):
    T, C = dpre.shape
    K = w.shape[0]
    tt, hb, tc = _tile(T, CONV_ROWS), _halo_rows(K), _tile(C, GLU_TILE)
    nt = T // tt
    prev, nxt = _prev_rows(tt, hb), _next_rows(tt, hb, T)

    sw = _strip_width(tc)

    def body(d_ref, dn_ref, x_ref, xh_ref, w_ref, dx_ref, dw_ref, db_ref):
        t = pl.program_id(1)

        @pl.when(t == 0)
        def _():
            dw_ref[...] = jnp.zeros_like(dw_ref)
            db_ref[...] = jnp.zeros_like(db_ref)

        def strip(sl, s):
            gsl = pl.ds(pl.multiple_of(tc + s * sw, sw), sw)
            d = d_ref[:, sl]
            dext = jnp.concatenate([d, jnp.where(t == nt - 1, 0.0, dn_ref[:, sl])], axis=0)
            dxin = _taps(dext, w_ref, sl, K, 0, tt, reverse=True)
            a, sg = x_ref[:, sl], _sigmoid(x_ref[:, gsl])
            v0 = a * sg
            dx_ref[:, sl] = (dxin * sg).astype(BF16)
            dx_ref[:, gsl] = (dxin * v0 * (1.0 - sg)).astype(BF16)
            halo = jnp.where(t == 0, 0.0, xh_ref[:, sl] * _sigmoid(xh_ref[:, gsl]))
            _tap_sums(d, jnp.concatenate([halo, v0], axis=0), dw_ref, sl, K, hb - (K - 1))
            db_ref[:, sl] += jnp.sum(d, axis=0, keepdims=True)

        _for_strips(tc, strip)

    return pl.pallas_call(
        body, name=name, out_shape=(_sds((T, 2 * C), BF16), _sds((K, C), F32), _sds((1, C), F32)), grid=(C // tc, nt),
        in_specs=[pl.BlockSpec((tt, tc), lambda c, t: (t, c)), pl.BlockSpec((hb, tc), lambda c, t: (nxt(t), c)),
                  pl.BlockSpec((tt, 2 * tc), lambda c, t: (t, c)), pl.BlockSpec((hb, 2 * tc), lambda c, t: (prev(t), c)),
                  pl.BlockSpec((K, tc), lambda c, t: (0, c))],
        out_specs=(pl.BlockSpec((tt, 2 * tc), lambda c, t: (t, c)), pl.BlockSpec((K, tc), lambda c, t: (0, c)),
                   pl.BlockSpec((1, tc), lambda c, t: (0, c))),
        compiler_params=_cp("parallel", "arbitrary"))(dpre, dpre, u, u, w)


def _softplus(x):
    t = jnp.exp(-jnp.abs(x))
    u = 1.0 + t
    log1p = jnp.where(u == 1.0, t, jnp.log(u) * t / jnp.where(u == 1.0, 1.0, u - 1.0))
    return jnp.maximum(x, 0.0) + log1p


def _dt_fwd(udt, bias, *, H, name):
    T = udt.shape[0]
    tt = _tile(T, 1024)

    def body(u_ref, b_ref, o_ref):
        lane = lax.broadcasted_iota(jnp.int32, (tt, LANES), 1)
        o_ref[...] = jnp.where(lane < H, _softplus(u_ref[...] + b_ref[...]), 0.0)

    row = pl.BlockSpec((tt, LANES), lambda i: (i, 0))
    return pl.pallas_call(body, name=name, out_shape=_sds((T, LANES), F32), grid=(T // tt,),
                          in_specs=[row, pl.BlockSpec((1, LANES), lambda i: (0, 0))], out_specs=row,
                          compiler_params=_cp("parallel"))(udt, bias)


def _dt_bwd(da, ddtx, dt, udt, bias, a_col, *, H, name):
    T = da.shape[0]
    tt = _tile(T, 1024)

    def body(da_ref, dx_ref, dt_ref, u_ref, b_ref, a_ref, draw_ref, dbias_ref, dalog_ref):
        i = pl.program_id(0)
        lane = lax.broadcasted_iota(jnp.int32, (tt, LANES), 1)
        dav = da_ref[...]
        ddt = dav * a_ref[...] + dx_ref[...]
        draw = jnp.where(lane < H, ddt * _sigmoid(u_ref[...] + b_ref[...]), 0.0)
        draw_ref[...] = draw.astype(BF16)

        @pl.when(i == 0)
        def _():
            dbias_ref[...] = jnp.zeros_like(dbias_ref)
            dalog_ref[...] = jnp.zeros_like(dalog_ref)

        dbias_ref[...] += jnp.sum(draw, axis=0, keepdims=True)
        dalog_ref[...] += jnp.sum(dav * dt_ref[...], axis=0, keepdims=True) * a_ref[...]

    row = pl.BlockSpec((tt, LANES), lambda i: (i, 0))
    vec = pl.BlockSpec((1, LANES), lambda i: (0, 0))
    return pl.pallas_call(body, name=name,
                          out_shape=(_sds((T, LANES), BF16), _sds((1, LANES), F32), _sds((1, LANES), F32)),
                          grid=(T // tt,), in_specs=[row, row, row, row, vec, vec], out_specs=(row, vec, vec),
                          compiler_params=_cp("arbitrary"))(da, ddtx, dt, udt, bias, a_col)


def _ssd_consts():
    L = SSM_CHUNK
    r = lax.broadcasted_iota(jnp.int32, (L, L), 0)
    c = lax.broadcasted_iota(jnp.int32, (L, L), 1)
    return r, c


def _ssd_chunk_decays(dtc_ref, dtr_ref, acol_ref, arow_ref, cs_ref, csr_ref, r, c):
    L = SSM_CHUNK
    dtc = dtc_ref[...]
    tril = (r >= c).astype(F32)
    triu = (r <= c).astype(F32)
    cs_ref[...] = jnp.dot(tril, dtc * acol_ref[...], precision=HIGHEST, preferred_element_type=F32)
    csr_ref[...] = jnp.dot(dtr_ref[...] * arow_ref[...], triu, precision=HIGHEST, preferred_element_type=F32)
    cs = cs_ref[...]
    cs_last = cs_ref[L - 1:L, :]
    return dtc, cs, jnp.exp(cs), jnp.exp(cs_last - cs), jnp.exp(cs_last), triu


def _ssd_fwd(xbc, dt, dtT, a_col, a_row, d_col, *, H, name):
    T, XBC = xbc.shape
    L, N, G, P = SSM_CHUNK, SSM_STATE, SSM_GROUPS, SSM_HEAD_DIM
    SX = H * P
    HR = dtT.shape[0]
    nc = T // L
    heads_per_group = H // G

    def body(x_ref, dtc_ref, dtr_ref, acol_ref, arow_ref, d_ref, y_ref, hst_ref, state, cs_ref, csr_ref):
        ci = pl.program_id(0)

        @pl.when(ci == 0)
        def _():
            state[...] = jnp.zeros_like(state)

        hst_ref[0] = state[...]
        r, c = _ssd_consts()
        tri = r >= c
        lane_lo = c < P
        row_lo = r < P
        lane1_lo = lax.broadcasted_iota(jnp.int32, (1, LANES), 1) < P
        dtc, cs, e, ds, cd, _ = _ssd_chunk_decays(dtc_ref, dtr_ref, acol_ref, arow_ref, cs_ref, csr_ref, r, c)
        dsk = d_ref[...]

        def sel(arr, h0):
            return jnp.where(lane_lo, arr[:, h0:h0 + 1], arr[:, h0 + 1:h0 + 2])

        for g in range(G):
            Bg = x_ref[:, SX + g * N:SX + (g + 1) * N].astype(BF16)
            Cg = x_ref[:, SX + G * N + g * N:SX + G * N + (g + 1) * N].astype(BF16)
            Gm = _dot(Cg, Bg, _NT)
            st = []
            for j in range(g * heads_per_group // 2, (g + 1) * heads_per_group // 2):
                h0 = 2 * j
                sl = slice(2 * P * j, 2 * P * (j + 1))
                x2 = x_ref[:, sl]
                X2 = x2 * sel(dtc, h0)
                H2 = state[sl, :]
                st.append(dict(h0=h0, sl=sl, x2=x2, X2=X2, H2=H2, R2=_dot(Cg, H2.astype(BF16), _NT),
                               S2=_dot((X2 * sel(ds, h0)).astype(BF16), Bg, _TN)))
            for s in st:
                ms = []
                for h in (s['h0'], s['h0'] + 1):
                    seg = cs[:, h:h + 1] - csr_ref[h:h + 1, :]
                    ms.append((Gm * jnp.where(tri, jnp.exp(jnp.where(tri, seg, 0.0)), 0.0)).astype(BF16))
                xst = jnp.concatenate([jnp.where(lane_lo, s['X2'], 0.0), jnp.where(lane_lo, 0.0, s['X2'])], axis=0).astype(BF16)
                s['yd'] = _dot(jnp.concatenate(ms, axis=1), xst, _NN)
            for s in st:
                h0, sl = s['h0'], s['sl']
                dsk2 = jnp.where(lane1_lo, dsk[:, h0:h0 + 1], dsk[:, h0 + 1:h0 + 2])
                y_ref[:, sl] = s['yd'] + sel(e, h0) * s['R2'] + s['x2'] * dsk2
                state[sl, :] = jnp.where(row_lo, cd[:, h0:h0 + 1], cd[:, h0 + 1:h0 + 2]) * s['H2'] + s['S2']

    vec = pl.BlockSpec((1, LANES), lambda i: (0, 0))
    return pl.pallas_call(
        body, name=name, out_shape=(_sds((T, SX), F32), _sds((nc, SX, N), F32)), grid=(nc,),
        in_specs=[pl.BlockSpec((L, XBC), lambda i: (i, 0)), pl.BlockSpec((L, LANES), lambda i: (i, 0)),
                  pl.BlockSpec((HR, L), lambda i: (0, i)), vec, pl.BlockSpec((HR, 1), lambda i: (0, 0)), vec],
        out_specs=(pl.BlockSpec((L, SX), lambda i: (i, 0)), pl.BlockSpec((1, SX, N), lambda i: (i, 0, 0))),
        scratch_shapes=[pltpu.VMEM((SX, N), F32), pltpu.VMEM((L, LANES), F32), pltpu.VMEM((HR, L), F32)],
        compiler_params=_cp("arbitrary"))(xbc, dt, dtT, a_col, a_row, d_col)


def _ssd_bwd(xbc, dt, dtT, a_col, a_row, d_col, hst, dy, *, H, name):
    T, XBC = xbc.shape
    L, N, G, P = SSM_CHUNK, SSM_STATE, SSM_GROUPS, SSM_HEAD_DIM
    SX = H * P
    HR = dtT.shape[0]
    nc = T // L
    heads_per_group = H // G

    def body(x_ref, dtc_ref, dtr_ref, acol_ref, arow_ref, d_ref, hst_ref, dy_ref,
             dx_ref, da_ref, ddtx_ref, dd_ref, dstate, cs_ref, csr_ref):
        ci = pl.program_id(0)

        @pl.when(ci == 0)
        def _():
            dstate[...] = jnp.zeros_like(dstate)
            dd_ref[...] = jnp.zeros_like(dd_ref)

        r, c = _ssd_consts()
        tri = r >= c
        lane_lo = c < P
        row_lo = r < P
        lane1 = lax.broadcasted_iota(jnp.int32, (1, LANES), 1)
        rowc = lax.broadcasted_iota(jnp.int32, (L, 1), 0)
        dtc, cs, e, ds, cd, triu = _ssd_chunk_decays(dtc_ref, dtr_ref, acol_ref, arow_ref, cs_ref, csr_ref, r, c)
        triu_b = triu.astype(BF16)
        dsk = d_ref[...]
        last_row = rowc == L - 1

        def sel(arr, h0):
            return jnp.where(lane_lo, arr[:, h0:h0 + 1], arr[:, h0 + 1:h0 + 2])

        triT = r <= c

        def halves(v, axis):
            return jnp.concatenate([jnp.where(lane_lo, v, 0.0), jnp.where(lane_lo, 0.0, v)], axis=axis)

        def head_sum(v, h):
            lo = jnp.sum(jnp.where(lane1 < P, v, 0.0), axis=1, keepdims=True) * (lane1 == h).astype(F32)
            hi = jnp.sum(jnp.where(lane1 < P, 0.0, v), axis=1, keepdims=True) * (lane1 == h + 1).astype(F32)
            return lo + hi

        GW = heads_per_group * P
        gl = lax.broadcasted_iota(jnp.int32, (GW, LANES), 0)
        gc = lax.broadcasted_iota(jnp.int32, (GW, LANES), 1)
        wl = lax.broadcasted_iota(jnp.int32, (heads_per_group * L, LANES), 0)
        wc = lax.broadcasted_iota(jnp.int32, (heads_per_group * L, LANES), 1)
        wide_r = lax.broadcasted_iota(jnp.int32, (L, heads_per_group * L), 0)
        wide_c = lax.broadcasted_iota(jnp.int32, (L, heads_per_group * L), 1)
        below_diag = (wide_c % L) < wide_r
        sums = jnp.zeros((3 * L, LANES), F32)
        da_q = jnp.zeros((L, LANES), F32)
        dcd_acc = jnp.zeros((1, LANES), F32)
        dd_acc = jnp.zeros((1, LANES), F32)
        for g in range(G):
            bsl = slice(SX + g * N, SX + (g + 1) * N)
            csl = slice(SX + G * N + g * N, SX + G * N + (g + 1) * N)
            Bg = x_ref[:, bsl].astype(BF16)
            Cg = x_ref[:, csl].astype(BF16)
            Gm = _dot(Cg, Bg, _NT)
            GmT = _dot(Bg, Cg, _NT)
            st = []
            for j in range(g * heads_per_group // 2, (g + 1) * heads_per_group // 2):
                h0 = 2 * j
                sl = slice(2 * P * j, 2 * P * (j + 1))
                s = dict(h0=h0, sl=sl, x2=x_ref[:, sl], dy2=dy_ref[:, sl], H2=hst_ref[0, sl, :], dHn=dstate[sl, :])
                s['dt2'], s['e2'], s['ds2'] = sel(dtc, h0), sel(e, h0), sel(ds, h0)
                s['X2'] = s['x2'] * s['dt2']
                s['H2b'], s['dHnb'] = s['H2'].astype(BF16), s['dHn'].astype(BF16)
                st.append(s)
            for s in st:
                s['R2'] = _dot(Cg, s['H2b'], _NT)
                s['dXd'] = _dot(Bg, s['dHnb'], _NT)
                s['dM2'] = _dot(s['dy2'].astype(BF16), halves(s['X2'], 0).astype(BF16), _NT)
            dG = jnp.zeros((L, L), F32)
            qs = []
            for s in st:
                mts = []
                for i, h in enumerate((s['h0'], s['h0'] + 1)):
                    z = cs[:, h:h + 1] - csr_ref[h:h + 1, :]
                    Dm = jnp.where(tri, jnp.exp(jnp.where(tri, z, 0.0)), 0.0)
                    DmT = jnp.where(triT, jnp.exp(jnp.where(triT, -z, 0.0)), 0.0)
                    dM = s['dM2'][:, i * L:(i + 1) * L]
                    dG = dG + dM * Dm
                    qs.append((dM * (Gm * Dm)).astype(BF16))
                    mts.append((GmT * DmT).astype(BF16))
                s['dXm'] = _dot(jnp.concatenate(mts, axis=1), halves(s['dy2'], 0).astype(BF16), _NN)
            Wg = _dot(triu_b, jnp.concatenate(qs, axis=1), _NN)
            place = (wc == g * heads_per_group + wl // L).astype(BF16)
            da_q = da_q + _dot(jnp.where(below_diag, Wg, 0.0).astype(BF16), place, _NN)
            dBg = jnp.zeros((L, N), F32)
            dCg = jnp.zeros((L, N), F32)
            des, ddss, dxxs = [], [], []
            for s in st:
                h0, sl, x2, dy2, X2 = s['h0'], s['sl'], s['x2'], s['dy2'], s['X2']
                dR2b = (s['e2'] * dy2).astype(BF16)
                dCg = dCg + _dot(dR2b, s['H2b'], _NN)
                dHr = _dot(dR2b, Cg, _TN)
                dBg = dBg + _dot((X2 * s['ds2']).astype(BF16), s['dHnb'], _NN)
                dX2 = s['ds2'] * s['dXd'] + s['dXm']
                des.append(dy2 * s['R2'])
                ddss.append(s['dXd'] * X2)
                dxxs.append(dX2 * x2)
                prod = s['dHn'] * s['H2']
                for i, h in enumerate((h0, h0 + 1)):
                    rows = jnp.sum(prod[i * P:(i + 1) * P], axis=0, keepdims=True)
                    dcd_acc = dcd_acc + jnp.sum(rows, axis=1, keepdims=True) * (lane1 == h).astype(F32)
                dd_acc = dd_acc + head_sum(jnp.sum(dy2 * x2, axis=0, keepdims=True), h0)
                dsk2 = jnp.where(lane1 < P, dsk[:, h0:h0 + 1], dsk[:, h0 + 1:h0 + 2])
                dx_ref[:, sl] = dX2 * s['dt2'] + dy2 * dsk2
                dstate[sl, :] = jnp.where(row_lo, cd[:, h0:h0 + 1], cd[:, h0 + 1:h0 + 2]) * s['dHn'] + dHr
            stack = jnp.concatenate([jnp.concatenate(v, axis=1) for v in (des, ddss, dxxs)], axis=0).astype(BF16)
            sums = sums + _dot(stack, (gc == g * heads_per_group + gl // P).astype(BF16), _NN)
            dGb = dG.astype(BF16)
            dx_ref[:, bsl] = dBg + _dot(dGb, Cg, _TN)
            dx_ref[:, csl] = dCg + _dot(dGb, Bg, _NN)
        t1 = sums[L:2 * L] * ds
        tail = jnp.sum(t1, axis=0, keepdims=True) + dcd_acc * cd
        dcs = sums[0:L] * e - t1 + jnp.where(last_row, tail, 0.0)
        da_ref[...] = jnp.dot(triu, dcs, precision=HIGHEST, preferred_element_type=F32) + da_q
        ddtx_ref[...] = sums[2 * L:3 * L]
        dd_ref[...] += dd_acc

    vec = pl.BlockSpec((1, LANES), lambda i: (0, 0))
    rev = lambda i: (nc - 1 - i, 0)
    return pl.pallas_call(
        body, name=name,
        out_shape=(_sds((T, XBC), F32), _sds((T, LANES), F32), _sds((T, LANES), F32), _sds((1, LANES), F32)),
        grid=(nc,),
        in_specs=[pl.BlockSpec((L, XBC), rev), pl.BlockSpec((L, LANES), rev),
                  pl.BlockSpec((HR, L), lambda i: (0, nc - 1 - i)), vec, pl.BlockSpec((HR, 1), lambda i: (0, 0)), vec,
                  pl.BlockSpec((1, SX, N), lambda i: (nc - 1 - i, 0, 0)), pl.BlockSpec((L, SX), rev)],
        out_specs=(pl.BlockSpec((L, XBC), rev), pl.BlockSpec((L, LANES), rev), pl.BlockSpec((L, LANES), rev), vec),
        scratch_shapes=[pltpu.VMEM((SX, N), F32), pltpu.VMEM((L, LANES), F32), pltpu.VMEM((HR, L), F32)],
        compiler_params=_cp("arbitrary"))(xbc, dt, dtT, a_col, a_row, d_col, hst, dy)


def _rms_fwd(y, u, zoff, w, *, name):
    T, SX = y.shape
    tt = _tile(T, 256)
    gs = SX // SSM_GROUPS
    zb = zoff // SX

    def body(y_ref, z_ref, w_ref, o_ref):
        for g in range(SSM_GROUPS):
            sl = slice(g * gs, (g + 1) * gs)
            z = z_ref[:, sl]
            yg = y_ref[:, sl] * (z * _sigmoid(z))
            rstd = lax.rsqrt(jnp.mean(yg * yg, axis=-1, keepdims=True) + RMS_EPS)
            o_ref[:, sl] = (yg * rstd * w_ref[:, sl]).astype(BF16)

    row = pl.BlockSpec((tt, SX), lambda i: (i, 0))
    return pl.pallas_call(body, name=name, out_shape=_sds((T, SX), BF16), grid=(T // tt,),
                          in_specs=[row, pl.BlockSpec((tt, SX), lambda i: (i, zb)), pl.BlockSpec((1, SX), lambda i: (0, 0))],
                          out_specs=row, compiler_params=_cp("parallel"))(y, u, w)


def _rms_bwd(y, u, zoff, w, dyn, *, name):
    T, SX = y.shape
    tt = _tile(T, 256)
    gs = SX // SSM_GROUPS
    zb = zoff // SX

    def body(y_ref, z_ref, w_ref, d_ref, dy_ref, dz_ref, dw_ref):
        i = pl.program_id(0)

        @pl.when(i == 0)
        def _():
            dw_ref[...] = jnp.zeros_like(dw_ref)

        for g in range(SSM_GROUPS):
            sl = slice(g * gs, (g + 1) * gs)
            z, yv, d = z_ref[:, sl], y_ref[:, sl], d_ref[:, sl]
            sz, dsz = _silu_and_grad(z)
            yg = yv * sz
            rstd = lax.rsqrt(jnp.mean(yg * yg, axis=-1, keepdims=True) + RMS_EPS)
            t = yg * rstd
            dw_ref[:, sl] += jnp.sum(d * t, axis=0, keepdims=True)
            dt_ = d * w_ref[:, sl]
            dyg = rstd * (dt_ - t * jnp.mean(dt_ * t, axis=-1, keepdims=True))
            dy_ref[:, sl] = dyg * sz
            dz_ref[:, sl] = (dyg * yv * dsz).astype(BF16)

    row = pl.BlockSpec((tt, SX), lambda i: (i, 0))
    vec = pl.BlockSpec((1, SX), lambda i: (0, 0))
    return pl.pallas_call(body, name=name, out_shape=(_sds((T, SX), F32), _sds((T, SX), BF16), _sds((1, SX), F32)),
                          grid=(T // tt,), in_specs=[row, pl.BlockSpec((tt, SX), lambda i: (i, zb)), vec, row],
                          out_specs=(row, row, vec), compiler_params=_cp("arbitrary"))(y, u, w, dyn)


def _gate_fwd(u, goff, ya, yb, *, name):
    T, D = ya.shape
    tt = _tile(T, 512)
    gb = goff // D

    def body(ga_ref, gb_ref, ya_ref, yb_ref, o_ref):
        o_ref[...] = (_sigmoid(ga_ref[...]) * ya_ref[...] + _sigmoid(gb_ref[...]) * yb_ref[...]).astype(BF16)

    row = pl.BlockSpec((tt, D), lambda i: (i, 0))
    return pl.pallas_call(body, name=name, out_shape=_sds((T, D), BF16), grid=(T // tt,),
                          in_specs=[pl.BlockSpec((tt, D), lambda i: (i, gb)), pl.BlockSpec((tt, D), lambda i: (i, gb + 1)), row, row],
                          out_specs=row, compiler_params=_cp("parallel"))(u, u, ya, yb)


def _gate_bwd(u, goff, ya, yb, dm, *, name):
    T, D = ya.shape
    tt = _tile(T, 512)
    gb = goff // D

    def body(ga_ref, gb_ref, ya_ref, yb_ref, dm_ref, dya_ref, dyb_ref, dg_ref):
        d = dm_ref[...]
        sa, sb = _sigmoid(ga_ref[...]), _sigmoid(gb_ref[...])
        dya_ref[...] = (d * sa).astype(BF16)
        dyb_ref[...] = (d * sb).astype(BF16)
        dg_ref[...] = jnp.concatenate([d * ya_ref[...] * sa * (1.0 - sa), d * yb_ref[...] * sb * (1.0 - sb)], axis=1).astype(BF16)

    row = pl.BlockSpec((tt, D), lambda i: (i, 0))
    return pl.pallas_call(body, name=name, out_shape=(_sds((T, D), BF16), _sds((T, D), BF16), _sds((T, 2 * D), BF16)),
                          grid=(T // tt,),
                          in_specs=[pl.BlockSpec((tt, D), lambda i: (i, gb)), pl.BlockSpec((tt, D), lambda i: (i, gb + 1)), row, row, row],
                          out_specs=(row, row, pl.BlockSpec((tt, 2 * D), lambda i: (i, 0))),
                          compiler_params=_cp("parallel"))(u, u, ya, yb, dm)


def _adamw_math(w, gg, m, v):
    nm = ADAM_B1 * m + (1.0 - ADAM_B1) * gg
    nv = ADAM_B2 * v + (1.0 - ADAM_B2) * (gg * gg)
    m_hat = nm / (1.0 - ADAM_B1 ** ADAM_STEP)
    v_hat = nv / (1.0 - ADAM_B2 ** ADAM_STEP)
    return -ADAM_LR * (m_hat / (jnp.sqrt(v_hat) + ADAM_EPS) + ADAM_WD * w), nm, nv


def _adamw(w, g, m, v, *, name):
    R, C = w.shape
    tr = _pick(R, 256, 8)

    def body(w_ref, g_ref, m_ref, v_ref, d_ref, nm_ref, nv_ref):
        d_ref[...], nm_ref[...], nv_ref[...] = _adamw_math(w_ref[...], g_ref[...], m_ref[...], v_ref[...])

    blk = pl.BlockSpec((tr, C), lambda i: (i, 0))
    out = _sds((R, C), F32)
    return pl.pallas_call(body, name=name, out_shape=(out, out, out), grid=(R // tr,), in_specs=[blk] * 4,
                          out_specs=(blk,) * 3, compiler_params=_cp("parallel"))(w, g, m, v)


def _sum_adamw(parts, w, m, v, l, *, name):
    shape = w.shape[1:]
    C = shape[-1]
    R = w[0].size // C
    tr = _pick(R, 256, 8)
    if tr % 8:
        w, m, v, l = w[l:l + 1], m[l:l + 1], v[l:l + 1], 0
    lb = l * (R // tr)

    def body(p_ref, w_ref, m_ref, v_ref, g_ref, d_ref, nm_ref, nv_ref):
        gg = p_ref[0].astype(F32)
        for k in range(1, N_DEV):
            gg = gg + p_ref[k].astype(F32)
        g_ref[...] = gg
        d_ref[...], nm_ref[...], nv_ref[...] = _adamw_math(w_ref[...], gg, m_ref[...], v_ref[...])

    blk = pl.BlockSpec((tr, C), lambda i: (i, 0))
    lblk = pl.BlockSpec((tr, C), lambda i: (i + lb, 0))
    out = _sds((R, C), F32)
    stacked = (w.shape[0] * R, C)
    res = pl.pallas_call(body, name=name, out_shape=(out,) * 4, grid=(R // tr,),
                         in_specs=[pl.BlockSpec((N_DEV, tr, C), lambda i: (0, i, 0)), lblk, lblk, lblk],
                         out_specs=(blk,) * 4, compiler_params=_cp("parallel"))(
        parts.reshape(N_DEV, R, C), w.reshape(stacked), m.reshape(stacked), v.reshape(stacked))
    return tuple(r.reshape(shape) for r in res)


def _sum_slots(x, *, name):
    n, R, C = x.shape
    tr = _tile(R, 512)

    def body(x_ref, o_ref):
        acc = x_ref[0].astype(F32)
        for k in range(1, n):
            acc = acc + x_ref[k].astype(F32)
        o_ref[...] = acc

    return pl.pallas_call(body, name=name, out_shape=_sds((R, C), F32), grid=(R // tr,),
                          in_specs=[pl.BlockSpec((n, tr, C), lambda i: (0, i, 0))],
                          out_specs=pl.BlockSpec((tr, C), lambda i: (i, 0)), compiler_params=_cp("parallel"))(x)


def _exchange(xs, *, scatter, name):
    n = len(xs)

    def body(*refs):
        x_refs, o_refs = refs[:n], refs[n:2 * n]
        send_sems, recv_sems, local_sems = refs[2 * n:]
        mx, my, mc = lax.axis_index("x"), lax.axis_index("y"), lax.axis_index("c")
        me = 4 * mx + 2 * my + mc

        def src(i, d):
            return x_refs[i].at[d] if scatter else x_refs[i]

        locals_ = [pltpu.make_async_copy(src(i, me), o_refs[i].at[me], local_sems.at[i]) for i in range(n)]
        for cp in locals_:
            cp.start()
        sends, recvs = [], []
        for k in range(1, N_DEV):
            px = 1 - mx if k & 4 else mx
            py = 1 - my if k & 2 else my
            pc = 1 - mc if k & 1 else mc
            peer = 4 * px + 2 * py + pc
            for i in range(n):
                common = dict(send_sem=send_sems.at[k - 1, i], recv_sem=recv_sems.at[k - 1, i],
                              device_id=(px, py, pc), device_id_type=pl.DeviceIdType.MESH)
                sends.append(pltpu.make_async_remote_copy(src_ref=src(i, peer), dst_ref=o_refs[i].at[me], **common))
                recvs.append(pltpu.make_async_remote_copy(src_ref=src(i, peer), dst_ref=o_refs[i].at[peer], **common))
        for cp in sends:
            cp.start()
        for cp in recvs:
            cp.wait_recv()
        for cp in sends:
            cp.wait_send()
        for cp in locals_:
            cp.wait()

    any_spec = pl.BlockSpec(memory_space=pl.ANY)
    out_shape = tuple(_sds(x.shape if scatter else (N_DEV,) + x.shape, x.dtype) for x in xs)
    return pl.pallas_call(
        body, name=name, out_shape=out_shape, in_specs=[any_spec] * n, out_specs=(any_spec,) * n,
        scratch_shapes=[pltpu.SemaphoreType.DMA((N_DEV - 1, n)), pltpu.SemaphoreType.DMA((N_DEV - 1, n)),
                        pltpu.SemaphoreType.DMA((n,))])(*xs)


def _to_rows(flat, row_mult):
    n = flat.shape[-1]
    per = row_mult * LANES
    pad = (-n) % per
    flat = jnp.pad(flat, [(0, 0)] * (flat.ndim - 1) + [(0, pad)])
    return flat.reshape(flat.shape[:-1] + ((n + pad) // LANES, LANES))


def _peers():
    mx, my, mc = lax.axis_index("x"), lax.axis_index("y"), lax.axis_index("c")
    out = []
    for k in range(1, N_DEV):
        px = 1 - mx if k & 4 else mx
        py = 1 - my if k & 2 else my
        pc = 1 - mc if k & 1 else mc
        out.append(((px, py, pc), 4 * px + 2 * py + pc))
    return 4 * mx + 2 * my + mc, out


_HBM = pl.BlockSpec(memory_space=pltpu.HBM)
_SEM = pl.BlockSpec(memory_space=pltpu.SEMAPHORE)


def _exchange_start(xs, *, scatter, name):
    n = len(xs)

    def body(*refs):
        x_refs, land_refs, send_sems, recv_sems, token = refs[:n], refs[n:2 * n], refs[2 * n], refs[2 * n + 1], refs[-1]
        me, peers = _peers()
        for k, (dev, peer) in enumerate(peers):
            for i in range(n):
                pltpu.make_async_remote_copy(
                    src_ref=x_refs[i].at[peer] if scatter else x_refs[i], dst_ref=land_refs[i].at[me],
                    send_sem=send_sems.at[k * n + i], recv_sem=recv_sems.at[k * n + i],
                    device_id=dev, device_id_type=pl.DeviceIdType.MESH).start()
        token[...] = jnp.zeros_like(token)

    land_shapes = [x.shape if scatter else (N_DEV,) + x.shape for x in xs]
    lands = [pltpu.with_memory_space_constraint(lax.empty(s, x.dtype), pltpu.HBM) for s, x in zip(land_shapes, xs)]
    srcs = [pltpu.with_memory_space_constraint(x, pltpu.HBM) for x in xs]
    out = pl.pallas_call(
        body, name=name,
        out_shape=(pltpu.SemaphoreType.DMA(((N_DEV - 1) * n,)), pltpu.SemaphoreType.DMA(((N_DEV - 1) * n,)),
                   *[pltpu.HBM(x.shape, x.dtype) for x in xs], *[pltpu.HBM(s, x.dtype) for s, x in zip(land_shapes, xs)],
                   _sds((8, LANES), F32)),
        in_specs=(_HBM,) * (2 * n), out_specs=(_SEM, _SEM) + (_HBM,) * (2 * n) + (pl.BlockSpec(memory_space=pltpu.VMEM),),
        input_output_aliases={i: 2 + i for i in range(2 * n)},
        compiler_params=pltpu.CompilerParams(has_side_effects=pltpu.SideEffectType.DATAFLOW_SIDE_EFFECTING))(*srcs, *lands)
    return (out[0], out[1], list(out[2:2 + n]), list(out[2 + n:2 + 2 * n])), out[-1]


def _exchange_wait(handle, after, *, scatter, name):
    send_sems, recv_sems, srcs, lands = handle
    n = len(srcs)

    def body(*refs):
        x_refs, land_refs, send_sems, recv_sems = refs[:n], refs[n:2 * n], refs[2 * n], refs[2 * n + 1]
        me, peers = _peers()
        for k, (dev, peer) in enumerate(peers):
            for i in range(n):
                cp = pltpu.make_async_remote_copy(
                    src_ref=x_refs[i].at[peer] if scatter else x_refs[i], dst_ref=land_refs[i].at[peer],
                    send_sem=send_sems.at[k * n + i], recv_sem=recv_sems.at[k * n + i],
                    device_id=dev, device_id_type=pl.DeviceIdType.MESH)
                cp.wait_send()
                cp.wait_recv()

    out = pl.pallas_call(
        body, name=name, out_shape=tuple(pltpu.HBM(a.shape, a.dtype) for a in srcs + lands),
        in_specs=(_HBM,) * (2 * n) + (_SEM, _SEM, pl.BlockSpec(memory_space=pl.ANY)), out_specs=(_HBM,) * (2 * n),
        input_output_aliases={i: i for i in range(2 * n)},
        compiler_params=pltpu.CompilerParams(has_side_effects=pltpu.SideEffectType.DATAFLOW_SIDE_EFFECTING))(
        *srcs, *lands, send_sems, recv_sems, after)
    return list(out[n:])


def _with_own_slot(lands, xs, *, scatter):
    me = 4 * lax.axis_index("x") + 2 * lax.axis_index("y") + lax.axis_index("c")
    out = []
    for land, x in zip(lands, xs):
        own = lax.dynamic_index_in_dim(x, me, 0, keepdims=True) if scatter else x[None]
        out.append(lax.dynamic_update_index_in_dim(land, own, me, 0))
    return out


def _wire_shards(p, l):
    return [p[k][l].astype(WIRE) if SHARDED[k][1] else p[k][l] for k in SHARDED]


def _full_weights(got, p):
    full = {}
    for k, g in zip(SHARDED, got):
        axis, shape = SHARDED[k][0] - 1, p[k].shape[1:]
        g = jnp.moveaxis(g, 0, axis)
        full[k] = g.reshape(shape[:axis] + (N_DEV * shape[axis],) + shape[axis + 1:])
    return full


def _grad_parts(grads, p):
    parts = []
    for k in SHARDED:
        axis, g = SHARDED[k][0] - 1, grads[k]
        g = g.reshape(g.shape[:axis] + (N_DEV, p[k].shape[1:][axis]) + g.shape[axis + 1:])
        parts.append(jnp.moveaxis(g, axis, 0).astype(WIRE))
    return parts


def _allreduce_small(vals):
    metas = [(k, v.shape, v.size) for k, v in vals.items()]
    flat = jnp.concatenate([v.reshape(-1) for v in vals.values()])
    got, = _exchange([_to_rows(flat, 8)], scatter=False, name="gather_small_grads")
    summed = _sum_slots(got, name="sum_small_grads").reshape(-1)
    out, off = {}, 0
    for k, shape, n in metas:
        out[k] = summed[off:off + n].reshape(shape)
        off += n
    return out


def _interleave(a, g, tile):
    C = a.shape[-1]
    n = C // tile
    lead = a.shape[:-1]
    return jnp.concatenate([a.reshape(lead + (n, 1, tile)), g.reshape(lead + (n, 1, tile))], axis=-2).reshape(lead + (2 * C,))


def _deinterleave(w, tile):
    C = w.shape[-1] // 2
    n = C // tile
    lead = w.shape[:-1]
    r = w.reshape(lead + (n, 2, tile))
    return r[..., 0, :].reshape(lead + (C,)), r[..., 1, :].reshape(lead + (C,))


def _row(v):
    return v.reshape(1, -1).astype(F32)


def _lanes(v):
    return jnp.pad(v.astype(F32), (0, LANES - v.shape[0])).reshape(1, LANES)


def _layer_weights(p, full, l, dims):
    D, SX, XBC, H, FF = dims
    gt, ft = _tile(D, GLU_TILE), _tile(FF, FFN_TILE)
    w_in = full['w_in']
    o = 0
    seg = {}
    for nm, n in (('a', D), ('g', D), ('z', SX), ('xbc', XBC), ('dt', H), ('ga', D), ('gb', D)):
        seg[nm] = w_in[:, o:o + n]
        o += n
    w_main = jnp.concatenate([_interleave(seg['a'], seg['g'], gt), seg['z'], seg['xbc'], seg['ga'], seg['gb']], axis=1)
    w_dt = jnp.pad(seg['dt'], ((0, 0), (0, LANES - H)))
    up = full['w_ffn_up']
    fw = full['ffn_dw_w']
    fb = p['ffn_dw_b'][l]
    a_head = -jnp.exp(p['ssm_a_log'][l].astype(F32))
    hr = -(-H // 8) * 8
    return dict(
        w_main=w_main, w_dt=w_dt,
        conv_w=full['conv_dw_w'], conv_b=_row(p['conv_dw_b'][l]), conv_g=_row(p['conv_ln_g'][l]), conv_beta=_row(p['conv_ln_b'][l]),
        w_conv_out=full['w_conv_out'],
        ssm_w=full['ssm_conv_w'], ssm_b=_row(p['ssm_conv_b'][l]),
        dt_bias=_lanes(p['ssm_dt_bias'][l]), a_col=_lanes(a_head),
        a_row=jnp.pad(a_head, (0, hr - H)).reshape(hr, 1), d_col=_lanes(p['ssm_d'][l]),
        norm_w=_row(p['ssm_norm_w'][l]), w_ssm_out=full['w_ssm_out'], w_o=full['w_o'],
        ln1_g=_row(p['ln1_g'][l]), ln1_b=_row(p['ln1_b'][l]),
        w_up=_interleave(up[:, :FF], up[:, FF:], ft),
        ffn_w=_interleave(fw[:, :FF], fw[:, FF:], ft), ffn_b=_row(_interleave(fb[:FF], fb[FF:], ft)),
        w_down=full['w_ffn_down'], ln2_g=_row(p['ln2_g'][l]), ln2_b=_row(p['ln2_b'][l]))


def _layer_fwd(h, W, l, dims):
    D, SX, XBC, H, FF = dims
    o_z, o_xbc, o_gate = 2 * D, 2 * D + SX, 2 * D + SX + XBC
    hr = W['a_row'].shape[0]
    t = f"l{l}"
    u = _mm(h, W['w_main'], mode='nn', out_dtype=F32, name=t + "_in_proj")
    udt = _mm(h, W['w_dt'], mode='nn', out_dtype=F32, name=t + "_dt_proj")
    v1, v3 = _conf_fwd(u, W['conv_w'], W['conv_b'], W['conv_g'], W['conv_beta'], D=D, name=t + "_conf_fwd")
    ya = _mm(v3, W['w_conv_out'], mode='nn', out_dtype=F32, name=t + "_conv_out")
    xbc = _conv_act_fwd(u, o_xbc, W['ssm_w'], W['ssm_b'], ffn=False, name=t + "_ssm_conv")
    dt = _dt_fwd(udt, W['dt_bias'], H=H, name=t + "_dt")
    dtT = jnp.pad(dt[:, :H].T, ((0, hr - H), (0, 0)))
    y, hst = _ssd_fwd(xbc, dt, dtT, W['a_col'], W['a_row'], W['d_col'], H=H, name=t + "_ssd_fwd")
    yn = _rms_fwd(y, u, o_z, W['norm_w'], name=t + "_rms_fwd")
    yb = _mm(yn, W['w_ssm_out'], mode='nn', out_dtype=F32, name=t + "_ssm_out")
    m = _gate_fwd(u, o_gate, ya, yb, name=t + "_gate_fwd")
    h1, s1 = _mm_res_ln(m, W['w_o'], h, W['ln1_g'], W['ln1_b'], name=t + "_mix_ln1")
    uf = _mm(h1, W['w_up'], mode='nn', out_dtype=F32, name=t + "_ffn_up")
    act = _conv_act_fwd(uf, 0, W['ffn_w'], W['ffn_b'], ffn=True, name=t + "_ffn_conv")
    h2, s2 = _mm_res_ln(act, W['w_down'], h1, W['ln2_g'], W['ln2_b'], name=t + "_ffn_down_ln2")
    saved = dict(h=h, u=u, udt=udt, v1=v1, v3=v3, ya=ya, xbc=xbc, dt=dt, dtT=dtT, y=y, hst=hst, yn=yn, yb=yb, m=m,
                 h1=h1, s1=s1, uf=uf, act=act, s2=s2)
    return h2, saved


def _layer_bwd(dh2, W, S, l, dims):
    D, SX, XBC, H, FF = dims
    o_z, o_xbc, o_gate = 2 * D, 2 * D + SX, 2 * D + SX + XBC
    t = f"l{l}"
    g = {}
    ds2, g['ln2_g'], g['ln2_b'] = _ln_bwd(S['s2'], dh2, W['ln2_g'], W['ln2_b'], silu=False, name=t + "_ln2_bwd")
    g['w_down'] = _mm(S['act'], ds2, mode='tn', out_dtype=F32, name=t + "_dw_down")
    dact = _mm(ds2, W['w_down'], mode='nt', out_dtype=F32, name=t + "_dact")
    duf, g['ffn_w'], g['ffn_b'] = _conv_act_bwd(S['uf'], 0, W['ffn_w'], W['ffn_b'], dact, ffn=True, name=t + "_ffn_conv_bwd")
    g['w_up'] = _mm(S['h1'], duf, mode='tn', out_dtype=F32, name=t + "_dw_up")
    dh1 = _mm(duf, W['w_up'], mode='nt', out_dtype=F32, res=ds2, res_scale=DN_ALPHA, name=t + "_dh1")
    ds1, g['ln1_g'], g['ln1_b'] = _ln_bwd(S['s1'], dh1, W['ln1_g'], W['ln1_b'], silu=False, name=t + "_ln1_bwd")
    g['w_o'] = _mm(S['m'], ds1, mode='tn', out_dtype=F32, name=t + "_dw_o")
    dm = _mm(ds1, W['w_o'], mode='nt', out_dtype=F32, name=t + "_dm")
    dya, dyb, dgate = _gate_bwd(S['u'], o_gate, S['ya'], S['yb'], dm, name=t + "_gate_bwd")
    g['w_conv_out'] = _mm(S['v3'], dya, mode='tn', out_dtype=F32, name=t + "_dw_conv_out")
    dv3 = _mm(dya, W['w_conv_out'], mode='nt', out_dtype=F32, name=t + "_dv3")
    dv1, g['conv_g'], g['conv_beta'] = _ln_bwd(S['v1'], dv3, W['conv_g'], W['conv_beta'], silu=True, name=t + "_conv_ln_bwd")
    dglu, g['conv_w'], g['conv_b'] = _glu_conv_bwd(dv1, S['u'], W['conv_w'], name=t + "_conf_conv_bwd")
    g['w_ssm_out'] = _mm(S['yn'], dyb, mode='tn', out_dtype=F32, name=t + "_dw_ssm_out")
    dyn = _mm(dyb, W['w_ssm_out'], mode='nt', out_dtype=F32, name=t + "_dyn")
    dy, dz, g['norm_w'] = _rms_bwd(S['y'], S['u'], o_z, W['norm_w'], dyn, name=t + "_rms_bwd")
    dxbc_c, da, ddtx, g['d'] = _ssd_bwd(S['xbc'], S['dt'], S['dtT'], W['a_col'], W['a_row'], W['d_col'], S['hst'], dy,
                                        H=H, name=t + "_ssd_bwd")
    ddt_raw, g['dt_bias'], g['a_log'] = _dt_bwd(da, ddtx, S['dt'], S['udt'], W['dt_bias'], W['a_col'], H=H, name=t + "_dt_bwd")
    dxbc, g['ssm_w'], g['ssm_b'] = _conv_act_bwd(S['u'], o_xbc, W['ssm_w'], W['ssm_b'], dxbc_c, ffn=False, name=t + "_ssm_conv_bwd")
    du = jnp.concatenate([dglu, dz, dxbc, dgate], axis=1)
    g['w_main'] = _mm(S['h'], du, mode='tn', out_dtype=F32, name=t + "_dw_main")
    g['w_dt'] = _mm(S['h'], ddt_raw, mode='tn', out_dtype=F32, name=t + "_dw_dt")
    dh = _mm(du, W['w_main'], mode='nt', out_dtype=F32, res=ds1, res_scale=DN_ALPHA, name=t + "_dh_main")
    dh = _mm(ddt_raw, W['w_dt'], mode='nt', out_dtype=F32, res=dh, res_scale=1.0, name=t + "_dh_dt")
    return dh, g


def _layer_grads_to_params(g, dims):
    D, SX, XBC, H, FF = dims
    gt, ft = _tile(D, GLU_TILE), _tile(FF, FFN_TILE)
    wm = g['w_main']
    da, dg = _deinterleave(wm[:, :2 * D], gt)
    rest = wm[:, 2 * D:]
    w_in = jnp.concatenate([da, dg, rest[:, :SX + XBC], g['w_dt'][:, :H], rest[:, SX + XBC:]], axis=1)
    fwg, fwv = _deinterleave(g['ffn_w'], ft)
    fbg, fbv = _deinterleave(g['ffn_b'], ft)
    upg, upv = _deinterleave(g['w_up'], ft)
    return dict(
        w_in=w_in, conv_dw_w=g['conv_w'], conv_dw_b=g['conv_b'][0], conv_ln_g=g['conv_g'][0], conv_ln_b=g['conv_beta'][0],
        w_conv_out=g['w_conv_out'], ssm_conv_w=g['ssm_w'], ssm_conv_b=g['ssm_b'][0],
        ssm_dt_bias=g['dt_bias'][0, :H], ssm_a_log=g['a_log'][0, :H], ssm_d=g['d'][0, :H], ssm_norm_w=g['norm_w'][0],
        w_ssm_out=g['w_ssm_out'], w_o=g['w_o'], ln1_g=g['ln1_g'][0], ln1_b=g['ln1_b'][0],
        w_ffn_up=jnp.concatenate([upg, upv], axis=1), ffn_dw_w=jnp.concatenate([fwg, fwv], axis=1),
        ffn_dw_b=jnp.concatenate([fbg, fbv], axis=1)[0], w_ffn_down=g['w_down'], ln2_g=g['ln2_g'][0], ln2_b=g['ln2_b'][0])


def kernel(x, ln_in_g, ln_in_b, w_in, conv_dw_w, conv_dw_b, conv_ln_g, conv_ln_b, w_conv_out, ssm_conv_w, ssm_conv_b, ssm_dt_bias, ssm_a_log, ssm_d, ssm_norm_w, w_ssm_out, w_o, ln1_g, ln1_b, w_ffn_up, ffn_dw_w, ffn_dw_b, w_ffn_down, ln2_g, ln2_b, loss_target, m_ln_in_g, m_ln_in_b, m_w_in, m_conv_dw_w, m_conv_dw_b, m_conv_ln_g, m_conv_ln_b, m_w_conv_out, m_ssm_conv_w, m_ssm_conv_b, m_ssm_dt_bias, m_ssm_a_log, m_ssm_d, m_ssm_norm_w, m_w_ssm_out, m_w_o, m_ln1_g, m_ln1_b, m_w_ffn_up, m_ffn_dw_w, m_ffn_dw_b, m_w_ffn_down, m_ln2_g, m_ln2_b, v_ln_in_g, v_ln_in_b, v_w_in, v_conv_dw_w, v_conv_dw_b, v_conv_ln_g, v_conv_ln_b, v_w_conv_out, v_ssm_conv_w, v_ssm_conv_b, v_ssm_dt_bias, v_ssm_a_log, v_ssm_d, v_ssm_norm_w, v_w_ssm_out, v_w_o, v_ln1_g, v_ln1_b, v_w_ffn_up, v_ffn_dw_w, v_ffn_dw_b, v_w_ffn_down, v_ln2_g, v_ln2_b):
    weights = (ln_in_g, ln_in_b, w_in, conv_dw_w, conv_dw_b, conv_ln_g, conv_ln_b, w_conv_out, ssm_conv_w, ssm_conv_b,
               ssm_dt_bias, ssm_a_log, ssm_d, ssm_norm_w, w_ssm_out, w_o, ln1_g, ln1_b, w_ffn_up, ffn_dw_w, ffn_dw_b,
               w_ffn_down, ln2_g, ln2_b)
    moments_m = (m_ln_in_g, m_ln_in_b, m_w_in, m_conv_dw_w, m_conv_dw_b, m_conv_ln_g, m_conv_ln_b, m_w_conv_out,
                 m_ssm_conv_w, m_ssm_conv_b, m_ssm_dt_bias, m_ssm_a_log, m_ssm_d, m_ssm_norm_w, m_w_ssm_out, m_w_o,
                 m_ln1_g, m_ln1_b, m_w_ffn_up, m_ffn_dw_w, m_ffn_dw_b, m_w_ffn_down, m_ln2_g, m_ln2_b)
    moments_v = (v_ln_in_g, v_ln_in_b, v_w_in, v_conv_dw_w, v_conv_dw_b, v_conv_ln_g, v_conv_ln_b, v_w_conv_out,
                 v_ssm_conv_w, v_ssm_conv_b, v_ssm_dt_bias, v_ssm_a_log, v_ssm_d, v_ssm_norm_w, v_w_ssm_out, v_w_o,
                 v_ln1_g, v_ln1_b, v_w_ffn_up, v_ffn_dw_w, v_ffn_dw_b, v_w_ffn_down, v_ln2_g, v_ln2_b)
    p = dict(zip(PARAMS, weights))
    pm = dict(zip(PARAMS, moments_m))
    pv = dict(zip(PARAMS, moments_v))

    T, D = x.shape[1], x.shape[2]
    SX = w_ssm_out.shape[1] * N_DEV
    XBC = ssm_conv_b.shape[-1]
    H = ssm_d.shape[-1]
    FF = ffn_dw_b.shape[-1] // 2
    dims = (D, SX, XBC, H, FF)
    depth = w_in.shape[0]

    got = _exchange(_wire_shards(p, 0), scatter=False, name="gather_l0")
    pending, tok = [], jnp.zeros((8, LANES), F32)
    for l in range(1, depth):
        shards, got = lax.optimization_barrier((_wire_shards(p, l), got))
        handle, t = _exchange_start(shards, scatter=False, name=f"gather_l{l}_start")
        pending.append((handle, shards))
        tok = tok + t
    layers = [_layer_weights(p, _full_weights(got, p), 0, dims)]

    xs = x.reshape(T, D)
    h = _ln_fwd(xs, _row(ln_in_g) + tok[0:1, 0:1], _row(ln_in_b), name="ln_in_fwd")
    saved = []
    for l in range(depth):
        if l > 0:
            handle, shards = pending[l - 1]
            got = _with_own_slot(_exchange_wait(handle, h, scatter=False, name=f"gather_l{l}_wait"), shards, scatter=False)
            layers.append(_layer_weights(p, _full_weights(got, p), l, dims))
        h, s = _layer_fwd(h, layers[l], l, dims)
        saved.append(s)
    loss_part, dh = _loss_fwd_bwd(h, loss_target.reshape(T, D), name="loss")

    layer_grads = [None] * depth
    pending, tok = [], jnp.zeros((8, LANES), F32)
    for l in reversed(range(depth)):
        W = dict(layers[l], ln2_g=layers[l]['ln2_g'] + tok[0:1, 0:1])
        dh, g = _layer_bwd(dh, W, saved[l], l, dims)
        layer_grads[l] = _layer_grads_to_params(g, dims)
        if l > 0:
            parts = _grad_parts(layer_grads[l], p)
            handle, tok = _exchange_start(parts, scatter=True, name=f"scatter_l{l}_start")
            pending.append((l, handle, parts))
    grad_x, dg_in, db_in = _ln_bwd(xs, dh, _row(ln_in_g), _row(ln_in_b), silu=False, name="ln_in_bwd")

    arrived = {}
    after = grad_x
    for l, handle, parts in pending:
        lands = _exchange_wait(handle, after, scatter=True, name=f"scatter_l{l}_wait")
        arrived[l] = _with_own_slot(lands, parts, scatter=True)
        after = arrived[l][0]
    parts0, after = lax.optimization_barrier((_grad_parts(layer_grads[0], p), after))
    arrived[0] = _exchange(parts0, scatter=True, name="scatter_l0")
    small = {k: jnp.stack([layer_grads[l][k] for l in range(depth)]) for k in layer_grads[0] if k not in SHARDED}
    small['ln_in_g'], small['ln_in_b'], small['loss'] = dg_in[0], db_in[0], loss_part
    grads = _allreduce_small(small)
    loss = grads.pop('loss').reshape(())

    delta, new_m, new_v = {}, {}, {}
    for i, k in enumerate(SHARDED):
        per_layer = [_sum_adamw(arrived[l][i], p[k], pm[k], pv[k], l, name=f"sum_adamw_l{l}_{k}") for l in range(depth)]
        grads[k], delta[k], new_m[k], new_v[k] = (jnp.stack([r[j] for r in per_layer]) for j in range(4))
    rest = [k for k in PARAMS if k not in SHARDED]
    flat = lambda d: _to_rows(jnp.concatenate([d[k].reshape(-1) for k in rest]), 8)
    d_, m_, v_ = _adamw(flat(p), flat(grads), flat(pm), flat(pv), name="adamw_small")
    off = 0
    for k in rest:
        n, shp = p[k].size, p[k].shape
        delta[k] = d_.reshape(-1)[off:off + n].reshape(shp)
        new_m[k] = m_.reshape(-1)[off:off + n].reshape(shp)
        new_v[k] = v_.reshape(-1)[off:off + n].reshape(shp)
        off += n

    return (loss, grad_x.reshape(x.shape), *[grads[k] for k in PARAMS], *[delta[k] for k in PARAMS],
            *[new_m[k] for k in PARAMS], *[new_v[k] for k in PARAMS])
```

```python
import math

import jax
import jax.numpy as jnp
from jax import lax
from jax.experimental import pallas as pl
from jax.experimental.pallas import tpu as pltpu

F32 = jnp.float32
BF16 = jnp.bfloat16
WIRE = jnp.bfloat16
HIGHEST = lax.Precision.HIGHEST

DEPTH = 2
SSM_STATE = 128
SSM_CHUNK = 128
SSM_GROUPS = 4
SSM_HEAD_DIM = 64
DN_ALPHA = (2 * DEPTH) ** 0.25
LN_EPS = 1e-5
RMS_EPS = 1e-5
ADAM_LR = 0.001
ADAM_B1 = 0.9
ADAM_B2 = 0.999
ADAM_EPS = 1e-08
ADAM_WD = 0.01
ADAM_STEP = 10

N_DEV = 8
LANES = 128
VMEM_LIMIT = 48 * 1024 * 1024
GLU_TILE = 512
FFN_TILE = 1408
CONV_ROWS = 256

PARAMS = ['ln_in_g', 'ln_in_b', 'w_in', 'conv_dw_w', 'conv_dw_b', 'conv_ln_g', 'conv_ln_b', 'w_conv_out',
          'ssm_conv_w', 'ssm_conv_b', 'ssm_dt_bias', 'ssm_a_log', 'ssm_d', 'ssm_norm_w', 'w_ssm_out', 'w_o',
          'ln1_g', 'ln1_b', 'w_ffn_up', 'ffn_dw_w', 'ffn_dw_b', 'w_ffn_down', 'ln2_g', 'ln2_b']
SHARDED = {'w_in': (2, True), 'conv_dw_w': (2, False), 'w_conv_out': (1, True), 'ssm_conv_w': (2, False),
           'w_ssm_out': (1, True), 'w_o': (1, True), 'w_ffn_up': (2, True), 'ffn_dw_w': (2, False),
           'w_ffn_down': (1, True)}


def _sds(shape, dtype):
    return jax.ShapeDtypeStruct(tuple(shape), dtype)


def _tile(dim, pref):
    return pref if dim % pref == 0 else dim


def _pick(dim, pref, mult):
    best = None
    for t in range(mult, min(dim, pref) + 1, mult):
        if dim % t == 0:
            best = t
    return best or dim


def _cp(*sem):
    return pltpu.CompilerParams(dimension_semantics=sem, vmem_limit_bytes=VMEM_LIMIT)


def _f32(x):
    return x.astype(F32)


def _sigmoid(x):
    return 1.0 / (1.0 + jnp.exp(-x))


def _silu_and_grad(x):
    s = _sigmoid(x)
    return x * s, s * (1.0 + x * (1.0 - s))


def _ln_stats(s):
    mu = jnp.mean(s, axis=-1, keepdims=True)
    xc = s - mu
    var = jnp.mean(xc * xc, axis=-1, keepdims=True)
    rstd = lax.rsqrt(var + LN_EPS)
    return xc * rstd, rstd


def _dot(a, b, dims):
    return lax.dot_general(a, b, (dims, ((), ())), preferred_element_type=F32)


_NN = ((1,), (0,))
_NT = ((1,), (1,))
_TN = ((0,), (0,))


def _mm(a, b, *, mode, out_dtype, name, res=None, res_scale=1.0, tm=1024, tn=1408, tk=1408):
    if mode == 'nn':
        (M, K), (_, N) = a.shape, b.shape
    elif mode == 'nt':
        (M, K), (N, _) = a.shape, b.shape
    else:
        (K, M), (_, N) = a.shape, b.shape
        tm, tk = 1408, 1024
    tm, tn, tk = _pick(M, tm, 8), _pick(N, tn, LANES), _pick(K, tk, LANES)
    nk = K // tk
    dims = {'nn': _NN, 'nt': _NT, 'tn': _TN}[mode]

    def body(*refs):
        if res is None:
            a_ref, b_ref, o_ref = refs[:3]
            r_ref = None
        else:
            a_ref, b_ref, r_ref, o_ref = refs[:4]
        p = _dot(a_ref[...].astype(BF16), b_ref[...].astype(BF16), dims)

        def finish(acc):
            if r_ref is not None:
                acc = acc + res_scale * r_ref[...]
            o_ref[...] = acc.astype(out_dtype)

        if nk == 1:
            finish(p)
        else:
            acc_ref = refs[-1]
            k = pl.program_id(2)

            @pl.when(k == 0)
            def _():
                acc_ref[...] = p

            @pl.when(k > 0)
            def _():
                acc_ref[...] += p

            @pl.when(k == nk - 1)
            def _():
                finish(acc_ref[...])

    if mode == 'nn':
        a_spec = pl.BlockSpec((tm, tk), lambda i, j, k: (i, k))
        b_spec = pl.BlockSpec((tk, tn), lambda i, j, k: (k, j))
    elif mode == 'nt':
        a_spec = pl.BlockSpec((tm, tk), lambda i, j, k: (i, k))
        b_spec = pl.BlockSpec((tn, tk), lambda i, j, k: (j, k))
    else:
        a_spec = pl.BlockSpec((tk, tm), lambda i, j, k: (k, i))
        b_spec = pl.BlockSpec((tk, tn), lambda i, j, k: (k, j))
    o_spec = pl.BlockSpec((tm, tn), lambda i, j, k: (i, j))
    in_specs = [a_spec, b_spec] + ([o_spec] if res is not None else [])
    args = (a, b) + ((res,) if res is not None else ())
    return pl.pallas_call(
        body, name=name, out_shape=_sds((M, N), out_dtype), grid=(M // tm, N // tn, nk),
        in_specs=in_specs, out_specs=o_spec,
        scratch_shapes=[pltpu.VMEM((tm, tn), F32)] if nk > 1 else [],
        compiler_params=_cp("parallel", "parallel", "arbitrary"))(*args)


def _mm_nt_cat(a_list, b_list, res, res_scale, *, name, tm=1024, tk=512):
    M, N = a_list[0].shape[0], b_list[0].shape[0]
    tm = _pick(M, tm, 8)
    tks = [_pick(a.shape[1], tk, LANES) for a in a_list]
    steps = [a.shape[1] // t for a, t in zip(a_list, tks)]
    offs = [sum(steps[:i]) for i in range(len(steps))]
    nk, n = sum(steps), len(a_list)

    def body(*refs):
        a_refs, b_refs, r_ref, o_ref, acc_ref = refs[:n], refs[n:2 * n], refs[2 * n], refs[2 * n + 1], refs[2 * n + 2]
        k = pl.program_id(1)

        @pl.when(k == 0)
        def _():
            acc_ref[...] = res_scale * r_ref[...]

        for i in range(n):
            @pl.when((k >= offs[i]) & (k < offs[i] + steps[i]))
            def _(i=i):
                acc_ref[...] += _dot(a_refs[i][...].astype(BF16), b_refs[i][...].astype(BF16), _NT)

        @pl.when(k == nk - 1)
        def _():
            o_ref[...] = acc_ref[...]

    def kmap(i):
        return lambda m, k: jnp.clip(k - offs[i], 0, steps[i] - 1)

    a_specs = [pl.BlockSpec((tm, tks[i]), lambda m, k, i=i: (m, kmap(i)(m, k))) for i in range(n)]
    b_specs = [pl.BlockSpec((N, tks[i]), lambda m, k, i=i: (0, kmap(i)(m, k))) for i in range(n)]
    row = pl.BlockSpec((tm, N), lambda m, k: (m, 0))
    return pl.pallas_call(
        body, name=name, out_shape=_sds((M, N), F32), grid=(M // tm, nk), in_specs=a_specs + b_specs + [row],
        out_specs=row, scratch_shapes=[pltpu.VMEM((tm, N), F32)],
        compiler_params=_cp("parallel", "arbitrary"))(*a_list, *b_list, res)


def _mm_res_ln(a, w, res, g, b, *, name, tm=512, tk=1408):
    (M, K), (_, N) = a.shape, w.shape
    tm, tk = _pick(M, tm, 8), _pick(K, tk, LANES)
    nk = K // tk

    def body(a_ref, w_ref, r_ref, g_ref, b_ref, h_ref, s_ref, hb_ref, acc_ref):
        k = pl.program_id(1)
        p = _dot(a_ref[...].astype(BF16), w_ref[...].astype(BF16), _NN)

        @pl.when(k == 0)
        def _():
            acc_ref[...] = p

        @pl.when(k > 0)
        def _():
            acc_ref[...] += p

        @pl.when(k == nk - 1)
        def _():
            s = DN_ALPHA * r_ref[...] + acc_ref[...]
            xhat, _ = _ln_stats(s)
            s_ref[...] = s
            h = xhat * g_ref[...] + b_ref[...]
            h_ref[...] = h
            hb_ref[...] = h.astype(BF16)

    row = pl.BlockSpec((tm, N), lambda i, k: (i, 0))
    vec = pl.BlockSpec((1, N), lambda i, k: (0, 0))
    return pl.pallas_call(
        body, name=name, out_shape=(_sds((M, N), F32), _sds((M, N), F32), _sds((M, N), BF16)), grid=(M // tm, nk),
        in_specs=[pl.BlockSpec((tm, tk), lambda i, k: (i, k)), pl.BlockSpec((tk, N), lambda i, k: (k, 0)), row, vec, vec],
        out_specs=(row, row, row), scratch_shapes=[pltpu.VMEM((tm, N), F32)],
        compiler_params=_cp("parallel", "arbitrary"))(a, w, res, g, b)


def _ln_fwd(x, g, b, *, name):
    T, D = x.shape
    tt = _tile(T, 512)

    def body(x_ref, g_ref, b_ref, o_ref, ob_ref):
        xhat, _ = _ln_stats(x_ref[...])
        h = xhat * g_ref[...] + b_ref[...]
        o_ref[...] = h
        ob_ref[...] = h.astype(BF16)

    row = pl.BlockSpec((tt, D), lambda i: (i, 0))
    vec = pl.BlockSpec((1, D), lambda i: (0, 0))
    return pl.pallas_call(body, name=name, out_shape=(_sds((T, D), F32), _sds((T, D), BF16)), grid=(T // tt,),
                          in_specs=[row, vec, vec], out_specs=(row, row), compiler_params=_cp("parallel"))(x, g, b)


def _ln_bwd(s, dy, g, b, *, silu, name):
    T, D = s.shape
    tt = _tile(T, 512)

    def body(s_ref, dy_ref, g_ref, b_ref, ds_ref, dg_ref, db_ref):
        i = pl.program_id(0)
        xhat, rstd = _ln_stats(s_ref[...])
        gg = g_ref[...]
        dyl = dy_ref[...]
        if silu:
            _, dsilu = _silu_and_grad(xhat * gg + b_ref[...])
            dyl = dyl * dsilu
        dxh = dyl * gg
        m1 = jnp.mean(dxh, axis=-1, keepdims=True)
        m2 = jnp.mean(dxh * xhat, axis=-1, keepdims=True)
        ds_ref[...] = rstd * (dxh - m1 - xhat * m2)

        @pl.when(i == 0)
        def _():
            dg_ref[...] = jnp.zeros_like(dg_ref)
            db_ref[...] = jnp.zeros_like(db_ref)

        dg_ref[...] += jnp.sum(dyl * xhat, axis=0, keepdims=True)
        db_ref[...] += jnp.sum(dyl, axis=0, keepdims=True)

    row = pl.BlockSpec((tt, D), lambda i: (i, 0))
    vec = pl.BlockSpec((1, D), lambda i: (0, 0))
    return pl.pallas_call(body, name=name, out_shape=(_sds((T, D), F32), _sds((1, D), F32), _sds((1, D), F32)),
                          grid=(T // tt,), in_specs=[row, row, vec, vec], out_specs=(row, vec, vec),
                          compiler_params=_cp("arbitrary"))(s, dy, g, b)


def _loss_fwd_bwd(h, tgt, *, name):
    T, D = h.shape
    tt = _tile(T, 512)

    def body(h_ref, t_ref, dh_ref, l_ref):
        i = pl.program_id(0)
        e = h_ref[...] - t_ref[...]
        dh_ref[...] = e * (1.0 / D)

        @pl.when(i == 0)
        def _():
            l_ref[...] = jnp.zeros_like(l_ref)

        part = jnp.sum(jnp.sum(e * e, axis=1, keepdims=True), axis=0, keepdims=True) * (0.5 / D)
        l_ref[...] += jnp.broadcast_to(part, l_ref.shape)

    row = pl.BlockSpec((tt, D), lambda i: (i, 0))
    one = pl.BlockSpec((8, LANES), lambda i: (0, 0))
    dh, l = pl.pallas_call(body, name=name, out_shape=(_sds((T, D), F32), _sds((8, LANES), F32)), grid=(T // tt,),
                           in_specs=[row, row], out_specs=(row, one), compiler_params=_cp("arbitrary"))(h, tgt)
    return l[0:1, 0:1], dh


def _halo_rows(k):
    return 32 if k > 17 else 16


def _shifted(ext, K, first):
    rolled = {0: ext}
    out = []
    for k in range(K):
        r = (first + k) % 8
        if r not in rolled:
            rolled[r] = pltpu.roll(ext, ext.shape[0] - r, 0)
        out.append((rolled[r], first + k - r))
    return out


def _taps(ext, w_ref, sl, K, first, n, reverse=False):
    acc = None
    for k, (z, base) in enumerate(_shifted(ext, K, first)):
        kw = K - 1 - k if reverse else k
        term = w_ref[kw:kw + 1, sl] * z[base:base + n]
        acc = term if acc is None else acc + term
    return acc


def _tap_sums(d, ext, dw_ref, sl, K, first):
    n = d.shape[0]
    for k, (z, base) in enumerate(_shifted(ext, K, first)):
        dw_ref[k:k + 1, sl] += jnp.sum(d * z[base:base + n], axis=0, keepdims=True)


def _strip_width(width):
    return LANES if width % LANES == 0 else width


def _for_strips(width, fn):
    sw = _strip_width(width)

    def step(s, carry):
        fn(pl.ds(pl.multiple_of(s * sw, sw), sw), s)
        return carry

    lax.fori_loop(0, width // sw, step, 0)


def _prev_rows(tt, hb):
    return lambda t: jnp.maximum(t * (tt // hb) - 1, 0)


def _next_rows(tt, hb, T):
    return lambda t: jnp.minimum((t + 1) * (tt // hb), T // hb - 1)


def _conf_fwd(u, w, b, g, beta, *, D, name):
    T = u.shape[0]
    K = w.shape[0]
    tt, hb, tc = _tile(T, CONV_ROWS), _halo_rows(K), _tile(D, GLU_TILE)
    sw = _strip_width(tc)
    per_tile = tc // sw
    prev = _prev_rows(tt, hb)

    def body(u_ref, uh_ref, w_ref, b_ref, g_ref, beta_ref, v1_ref, v3_ref):
        i = pl.program_id(0)

        def strip(sl, s):
            a0 = (s // per_tile) * (2 * tc) + (s % per_tile) * sw
            a_sl, g_sl = pl.ds(pl.multiple_of(a0, sw), sw), pl.ds(pl.multiple_of(a0 + tc, sw), sw)
            halo = jnp.where(i > 0, _f32(uh_ref[:, a_sl]) * _sigmoid(_f32(uh_ref[:, g_sl])), 0.0)
            ext = jnp.concatenate([halo, _f32(u_ref[:, a_sl]) * _sigmoid(_f32(u_ref[:, g_sl]))], axis=0)
            v1_ref[:, sl] = _taps(ext, w_ref, sl, K, hb - (K - 1), tt) + b_ref[:, sl]

        _for_strips(D, strip)
        xhat, _ = _ln_stats(v1_ref[...])
        v2 = xhat * g_ref[...] + beta_ref[...]
        v3_ref[...] = (v2 * _sigmoid(v2)).astype(BF16)

    row = pl.BlockSpec((tt, D), lambda t: (t, 0))
    vec = pl.BlockSpec((1, D), lambda t: (0, 0))
    return pl.pallas_call(
        body, name=name, out_shape=(_sds((T, D), F32), _sds((T, D), BF16)), grid=(T // tt,),
        in_specs=[pl.BlockSpec((tt, 2 * D), lambda t: (t, 0)), pl.BlockSpec((hb, 2 * D), lambda t: (prev(t), 0)),
                  pl.BlockSpec((K, D), lambda t: (0, 0)), vec, vec, vec],
        out_specs=(row, row), compiler_params=_cp("parallel"))(u, u, w, b, g, beta)


def _conv_act_fwd(x, off, w, b, *, ffn, name):
    T = x.shape[0]
    K, Cw = w.shape
    tt, hb = _tile(T, CONV_ROWS), _halo_rows(K)
    tc = _tile(Cw // 2, FFN_TILE) if ffn else _pick(math.gcd(Cw, off), 1536, LANES)
    xw = 2 * tc if ffn else tc
    assert off % xw == 0
    ob = off // xw
    prev = _prev_rows(tt, hb)

    def body(x_ref, xh_ref, w_ref, b_ref, o_ref):
        first = pl.program_id(0) == 0

        def pre_of(sl):
            ext = jnp.concatenate([jnp.where(first, 0.0, _f32(xh_ref[:, sl])), _f32(x_ref[:, sl])], axis=0)
            return _taps(ext, w_ref, sl, K, hb - (K - 1), tt) + b_ref[:, sl]

        def strip(sl, s):
            if ffn:
                gate = pre_of(sl)
                val = pre_of(pl.ds(pl.multiple_of(tc + s * sw, sw), sw))
                o_ref[:, sl] = (gate * _sigmoid(gate) * val).astype(BF16)
            else:
                pre = pre_of(sl)
                o_ref[:, sl] = pre * _sigmoid(pre)

        _for_strips(tc, strip)

    sw = _strip_width(tc)
    return pl.pallas_call(
        body, name=name, out_shape=_sds((T, Cw // 2), BF16) if ffn else _sds((T, Cw), F32), grid=(T // tt, Cw // xw),
        in_specs=[pl.BlockSpec((tt, xw), lambda t, c: (t, c + ob)), pl.BlockSpec((hb, xw), lambda t, c: (prev(t), c + ob)),
                  pl.BlockSpec((K, xw), lambda t, c: (0, c)), pl.BlockSpec((1, xw), lambda t, c: (0, c))],
        out_specs=pl.BlockSpec((tt, tc), lambda t, c: (t, c)),
        compiler_params=_cp("parallel", "parallel"))(x, x, w, b)


def _conv_act_bwd(x, off, w, b, dout, *, ffn, name):
    T = x.shape[0]
    K, Cw = w.shape
    tt, hb = _tile(T, CONV_ROWS), _halo_rows(K)
    tc = _tile(Cw // 2, FFN_TILE) if ffn else _pick(math.gcd(Cw, off), 1536, LANES)
    xw = 2 * tc if ffn else tc
    assert off % xw == 0
    ob = off // xw
    nt = T // tt
    prev, nxt = _prev_rows(tt, hb), _next_rows(tt, hb, T)

    sw = _strip_width(tc)

    def body(x_ref, xp_ref, xn_ref, d_ref, dn_ref, w_ref, b_ref, dx_ref, dw_ref, db_ref):
        t = pl.program_id(1)

        @pl.when(t == 0)
        def _():
            dw_ref[...] = jnp.zeros_like(dw_ref)
            db_ref[...] = jnp.zeros_like(db_ref)

        def ext_of(sl):
            return jnp.concatenate([jnp.where(t == 0, 0.0, _f32(xp_ref[:, sl])), _f32(x_ref[:, sl]), _f32(xn_ref[:, sl])], axis=0)

        def pre_of(ext, sl):
            return _taps(ext, w_ref, sl, K, hb - (K - 1), tt + hb) + b_ref[:, sl]

        def finish(ext, dpre, sl):
            dx_ref[:, sl] = _taps(dpre, w_ref, sl, K, 0, tt, reverse=True).astype(BF16)
            dp = dpre[0:tt]
            _tap_sums(dp, ext, dw_ref, sl, K, hb - (K - 1))
            db_ref[:, sl] += jnp.sum(dp, axis=0, keepdims=True)

        def strip(sl, s):
            d = jnp.concatenate([d_ref[:, sl], jnp.where(t == nt - 1, 0.0, dn_ref[:, sl])], axis=0)
            if ffn:
                vsl = pl.ds(pl.multiple_of(tc + s * sw, sw), sw)
                eg, ev = ext_of(sl), ext_of(vsl)
                sg, dsg = _silu_and_grad(pre_of(eg, sl))
                val = pre_of(ev, vsl)
                finish(eg, d * val * dsg, sl)
                finish(ev, d * sg, vsl)
            else:
                ext = ext_of(sl)
                finish(ext, d * _silu_and_grad(pre_of(ext, sl))[1], sl)

        _for_strips(tc, strip)

    return pl.pallas_call(
        body, name=name, out_shape=(_sds((T, Cw), BF16), _sds((K, Cw), F32), _sds((1, Cw), F32)),
        grid=(Cw // xw, nt),
        in_specs=[pl.BlockSpec((tt, xw), lambda c, t: (t, c + ob)), pl.BlockSpec((hb, xw), lambda c, t: (prev(t), c + ob)),
                  pl.BlockSpec((hb, xw), lambda c, t: (nxt(t), c + ob)),
                  pl.BlockSpec((tt, tc), lambda c, t: (t, c)), pl.BlockSpec((hb, tc), lambda c, t: (nxt(t), c)),
                  pl.BlockSpec((K, xw), lambda c, t: (0, c)), pl.BlockSpec((1, xw), lambda c, t: (0, c))],
        out_specs=(pl.BlockSpec((tt, xw), lambda c, t: (t, c)), pl.BlockSpec((K, xw), lambda c, t: (0, c)),
                   pl.BlockSpec((1, xw), lambda c, t: (0, c))),
        compiler_params=_cp("parallel", "arbitrary"))(x, x, x, dout, dout, w, b)


def _glu_conv_bwd(dpre, u, w, *, name):
    T, C = dpre.shape
    K = w.shape[0]
    tt, hb, tc = _tile(T, CONV_ROWS), _halo_rows(K), _tile(C, GLU_TILE)
    nt = T // tt
    prev, nxt = _prev_rows(tt, hb), _next_rows(tt, hb, T)

    sw = _strip_width(tc)

    def body(d_ref, dn_ref, x_ref, xh_ref, w_ref, dx_ref, dw_ref, db_ref):
        t = pl.program_id(1)

        @pl.when(t == 0)
        def _():
            dw_ref[...] = jnp.zeros_like(dw_ref)
            db_ref[...] = jnp.zeros_like(db_ref)

        def strip(sl, s):
            gsl = pl.ds(pl.multiple_of(tc + s * sw, sw), sw)
            d = d_ref[:, sl]
            dext = jnp.concatenate([d, jnp.where(t == nt - 1, 0.0, dn_ref[:, sl])], axis=0)
            dxin = _taps(dext, w_ref, sl, K, 0, tt, reverse=True)
            a, sg = _f32(x_ref[:, sl]), _sigmoid(_f32(x_ref[:, gsl]))
            v0 = a * sg
            dx_ref[:, sl] = (dxin * sg).astype(BF16)
            dx_ref[:, gsl] = (dxin * v0 * (1.0 - sg)).astype(BF16)
            halo = jnp.where(t == 0, 0.0, _f32(xh_ref[:, sl]) * _sigmoid(_f32(xh_ref[:, gsl])))
            _tap_sums(d, jnp.concatenate([halo, v0], axis=0), dw_ref, sl, K, hb - (K - 1))
            db_ref[:, sl] += jnp.sum(d, axis=0, keepdims=True)

        _for_strips(tc, strip)

    return pl.pallas_call(
        body, name=name, out_shape=(_sds((T, 2 * C), BF16), _sds((K, C), F32), _sds((1, C), F32)), grid=(C // tc, nt),
        in_specs=[pl.BlockSpec((tt, tc), lambda c, t: (t, c)), pl.BlockSpec((hb, tc), lambda c, t: (nxt(t), c)),
                  pl.BlockSpec((tt, 2 * tc), lambda c, t: (t, c)), pl.BlockSpec((hb, 2 * tc), lambda c, t: (prev(t), c)),
                  pl.BlockSpec((K, tc), lambda c, t: (0, c))],
        out_specs=(pl.BlockSpec((tt, 2 * tc), lambda c, t: (t, c)), pl.BlockSpec((K, tc), lambda c, t: (0, c)),
                   pl.BlockSpec((1, tc), lambda c, t: (0, c))),
        compiler_params=_cp("parallel", "arbitrary"))(dpre, dpre, u, u, w)


def _softplus(x):
    t = jnp.exp(-jnp.abs(x))
    u = 1.0 + t
    log1p = jnp.where(u == 1.0, t, jnp.log(u) * t / jnp.where(u == 1.0, 1.0, u - 1.0))
    return jnp.maximum(x, 0.0) + log1p


def _dt_fwd(udt, bias, *, H, name):
    T = udt.shape[0]
    tt = _tile(T, 1024)

    def body(u_ref, b_ref, o_ref):
        lane = lax.broadcasted_iota(jnp.int32, (tt, LANES), 1)
        o_ref[...] = jnp.where(lane < H, _softplus(u_ref[...] + b_ref[...]), 0.0)

    row = pl.BlockSpec((tt, LANES), lambda i: (i, 0))
    return pl.pallas_call(body, name=name, out_shape=_sds((T, LANES), F32), grid=(T // tt,),
                          in_specs=[row, pl.BlockSpec((1, LANES), lambda i: (0, 0))], out_specs=row,
                          compiler_params=_cp("parallel"))(udt, bias)


def _dt_bwd(da, ddtx, dt, udt, bias, a_col, *, H, name):
    T = da.shape[0]
    tt = _tile(T, 1024)

    def body(da_ref, dx_ref, dt_ref, u_ref, b_ref, a_ref, draw_ref, dbias_ref, dalog_ref):
        i = pl.program_id(0)
        lane = lax.broadcasted_iota(jnp.int32, (tt, LANES), 1)
        dav = da_ref[...]
        ddt = dav * a_ref[...] + dx_ref[...]
        draw = jnp.where(lane < H, ddt * _sigmoid(u_ref[...] + b_ref[...]), 0.0)
        draw_ref[...] = draw.astype(BF16)

        @pl.when(i == 0)
        def _():
            dbias_ref[...] = jnp.zeros_like(dbias_ref)
            dalog_ref[...] = jnp.zeros_like(dalog_ref)

        dbias_ref[...] += jnp.sum(draw, axis=0, keepdims=True)
        dalog_ref[...] += jnp.sum(dav * dt_ref[...], axis=0, keepdims=True) * a_ref[...]

    row = pl.BlockSpec((tt, LANES), lambda i: (i, 0))
    vec = pl.BlockSpec((1, LANES), lambda i: (0, 0))
    return pl.pallas_call(body, name=name,
                          out_shape=(_sds((T, LANES), BF16), _sds((1, LANES), F32), _sds((1, LANES), F32)),
                          grid=(T // tt,), in_specs=[row, row, row, row, vec, vec], out_specs=(row, vec, vec),
                          compiler_params=_cp("arbitrary"))(da, ddtx, dt, udt, bias, a_col)


def _ssd_consts():
    L = SSM_CHUNK
    r = lax.broadcasted_iota(jnp.int32, (L, L), 0)
    c = lax.broadcasted_iota(jnp.int32, (L, L), 1)
    return r, c


def _ssd_chunk_decays(dtc_ref, dtr_ref, acol_ref, arow_ref, cs_ref, csr_ref, r, c):
    L = SSM_CHUNK
    dtc = dtc_ref[...]
    tril = (r >= c).astype(F32)
    triu = (r <= c).astype(F32)
    cs_ref[...] = jnp.dot(tril, dtc * acol_ref[...], precision=HIGHEST, preferred_element_type=F32)
    csr_ref[...] = jnp.dot(dtr_ref[...] * arow_ref[...], triu, precision=HIGHEST, preferred_element_type=F32)
    cs = cs_ref[...]
    cs_last = cs_ref[L - 1:L, :]
    return dtc, cs, jnp.exp(cs), jnp.exp(cs_last - cs), jnp.exp(cs_last), triu


def _ssd_fwd(xbc, dt, dtT, a_col, a_row, d_col, *, H, name):
    T, XBC = xbc.shape
    L, N, G, P = SSM_CHUNK, SSM_STATE, SSM_GROUPS, SSM_HEAD_DIM
    SX = H * P
    HR = dtT.shape[0]
    nc = T // L
    heads_per_group = H // G

    def body(x_ref, dtc_ref, dtr_ref, acol_ref, arow_ref, d_ref, y_ref, hst_ref, state, cs_ref, csr_ref):
        ci = pl.program_id(0)

        @pl.when(ci == 0)
        def _():
            state[...] = jnp.zeros_like(state)

        hst_ref[0] = state[...]
        r, c = _ssd_consts()
        tri = r >= c
        lane_lo = c < P
        row_lo = r < P
        lane1_lo = lax.broadcasted_iota(jnp.int32, (1, LANES), 1) < P
        dtc, cs, e, ds, cd, _ = _ssd_chunk_decays(dtc_ref, dtr_ref, acol_ref, arow_ref, cs_ref, csr_ref, r, c)
        dsk = d_ref[...]

        def sel(arr, h0):
            return jnp.where(lane_lo, arr[:, h0:h0 + 1], arr[:, h0 + 1:h0 + 2])

        for g in range(G):
            Bg = x_ref[:, SX + g * N:SX + (g + 1) * N].astype(BF16)
            Cg = x_ref[:, SX + G * N + g * N:SX + G * N + (g + 1) * N].astype(BF16)
            Gm = _dot(Cg, Bg, _NT)
            st = []
            for j in range(g * heads_per_group // 2, (g + 1) * heads_per_group // 2):
                h0 = 2 * j
                sl = slice(2 * P * j, 2 * P * (j + 1))
                x2 = x_ref[:, sl]
                X2 = x2 * sel(dtc, h0)
                H2 = state[sl, :]
                st.append(dict(h0=h0, sl=sl, x2=x2, X2=X2, H2=H2, R2=_dot(Cg, H2.astype(BF16), _NT),
                               S2=_dot((X2 * sel(ds, h0)).astype(BF16), Bg, _TN)))
            for s in st:
                ms = []
                for h in (s['h0'], s['h0'] + 1):
                    seg = cs[:, h:h + 1] - csr_ref[h:h + 1, :]
                    ms.append((Gm * jnp.where(tri, jnp.exp(jnp.where(tri, seg, 0.0)), 0.0)).astype(BF16))
                xst = jnp.concatenate([jnp.where(lane_lo, s['X2'], 0.0), jnp.where(lane_lo, 0.0, s['X2'])], axis=0).astype(BF16)
                s['yd'] = _dot(jnp.concatenate(ms, axis=1), xst, _NN)
            for s in st:
                h0, sl = s['h0'], s['sl']
                dsk2 = jnp.where(lane1_lo, dsk[:, h0:h0 + 1], dsk[:, h0 + 1:h0 + 2])
                y_ref[:, sl] = s['yd'] + sel(e, h0) * s['R2'] + s['x2'] * dsk2
                state[sl, :] = jnp.where(row_lo, cd[:, h0:h0 + 1], cd[:, h0 + 1:h0 + 2]) * s['H2'] + s['S2']

    vec = pl.BlockSpec((1, LANES), lambda i: (0, 0))
    return pl.pallas_call(
        body, name=name, out_shape=(_sds((T, SX), F32), _sds((nc, SX, N), F32)), grid=(nc,),
        in_specs=[pl.BlockSpec((L, XBC), lambda i: (i, 0)), pl.BlockSpec((L, LANES), lambda i: (i, 0)),
                  pl.BlockSpec((HR, L), lambda i: (0, i)), vec, pl.BlockSpec((HR, 1), lambda i: (0, 0)), vec],
        out_specs=(pl.BlockSpec((L, SX), lambda i: (i, 0)), pl.BlockSpec((1, SX, N), lambda i: (i, 0, 0))),
        scratch_shapes=[pltpu.VMEM((SX, N), F32), pltpu.VMEM((L, LANES), F32), pltpu.VMEM((HR, L), F32)],
        compiler_params=_cp("arbitrary"))(xbc, dt, dtT, a_col, a_row, d_col)


def _ssd_bwd(xbc, dt, dtT, a_col, a_row, d_col, hst, dy, *, H, name):
    T, XBC = xbc.shape
    L, N, G, P = SSM_CHUNK, SSM_STATE, SSM_GROUPS, SSM_HEAD_DIM
    SX = H * P
    HR = dtT.shape[0]
    nc = T // L
    heads_per_group = H // G

    def body(x_ref, dtc_ref, dtr_ref, acol_ref, arow_ref, d_ref, hst_ref, dy_ref,
             dx_ref, da_ref, ddtx_ref, dd_ref, dstate, cs_ref, csr_ref):
        ci = pl.program_id(0)

        @pl.when(ci == 0)
        def _():
            dstate[...] = jnp.zeros_like(dstate)
            dd_ref[...] = jnp.zeros_like(dd_ref)

        r, c = _ssd_consts()
        tri = r >= c
        lane_lo = c < P
        row_lo = r < P
        lane1 = lax.broadcasted_iota(jnp.int32, (1, LANES), 1)
        rowc = lax.broadcasted_iota(jnp.int32, (L, 1), 0)
        dtc, cs, e, ds, cd, triu = _ssd_chunk_decays(dtc_ref, dtr_ref, acol_ref, arow_ref, cs_ref, csr_ref, r, c)
        triu_b = triu.astype(BF16)
        dsk = d_ref[...]
        last_row = rowc == L - 1

        def sel(arr, h0):
            return jnp.where(lane_lo, arr[:, h0:h0 + 1], arr[:, h0 + 1:h0 + 2])

        triT = r <= c

        def halves(v, axis):
            return jnp.concatenate([jnp.where(lane_lo, v, 0.0), jnp.where(lane_lo, 0.0, v)], axis=axis)

        def head_sum(v, h):
            lo = jnp.sum(jnp.where(lane1 < P, v, 0.0), axis=1, keepdims=True) * (lane1 == h).astype(F32)
            hi = jnp.sum(jnp.where(lane1 < P, 0.0, v), axis=1, keepdims=True) * (lane1 == h + 1).astype(F32)
            return lo + hi

        GW = heads_per_group * P
        gl = lax.broadcasted_iota(jnp.int32, (GW, LANES), 0)
        gc = lax.broadcasted_iota(jnp.int32, (GW, LANES), 1)
        wl = lax.broadcasted_iota(jnp.int32, (heads_per_group * L, LANES), 0)
        wc = lax.broadcasted_iota(jnp.int32, (heads_per_group * L, LANES), 1)
        wide_r = lax.broadcasted_iota(jnp.int32, (L, heads_per_group * L), 0)
        wide_c = lax.broadcasted_iota(jnp.int32, (L, heads_per_group * L), 1)
        below_diag = (wide_c % L) < wide_r
        sums = jnp.zeros((3 * L, LANES), F32)
        da_q = jnp.zeros((L, LANES), F32)
        dcd_acc = jnp.zeros((1, LANES), F32)
        dd_acc = jnp.zeros((1, LANES), F32)
        for g in range(G):
            bsl = slice(SX + g * N, SX + (g + 1) * N)
            csl = slice(SX + G * N + g * N, SX + G * N + (g + 1) * N)
            Bg = x_ref[:, bsl].astype(BF16)
            Cg = x_ref[:, csl].astype(BF16)
            Gm = _dot(Cg, Bg, _NT)
            GmT = _dot(Bg, Cg, _NT)
            st = []
            for j in range(g * heads_per_group // 2, (g + 1) * heads_per_group // 2):
                h0 = 2 * j
                sl = slice(2 * P * j, 2 * P * (j + 1))
                s = dict(h0=h0, sl=sl, x2=x_ref[:, sl], dy2=dy_ref[:, sl], H2=hst_ref[0, sl, :], dHn=dstate[sl, :])
                s['dt2'], s['e2'], s['ds2'] = sel(dtc, h0), sel(e, h0), sel(ds, h0)
                s['X2'] = s['x2'] * s['dt2']
                s['H2b'], s['dHnb'] = s['H2'].astype(BF16), s['dHn'].astype(BF16)
                st.append(s)
            for s in st:
                s['R2'] = _dot(Cg, s['H2b'], _NT)
                s['dXd'] = _dot(Bg, s['dHnb'], _NT)
                s['dM2'] = _dot(s['dy2'].astype(BF16), halves(s['X2'], 0).astype(BF16), _NT)
            dG = jnp.zeros((L, L), F32)
            qs = []
            for s in st:
                mts = []
                for i, h in enumerate((s['h0'], s['h0'] + 1)):
                    z = cs[:, h:h + 1] - csr_ref[h:h + 1, :]
                    Dm = jnp.where(tri, jnp.exp(jnp.where(tri, z, 0.0)), 0.0)
                    DmT = jnp.where(triT, jnp.exp(jnp.where(triT, -z, 0.0)), 0.0)
                    dM = s['dM2'][:, i * L:(i + 1) * L]
                    dG = dG + dM * Dm
                    qs.append((dM * (Gm * Dm)).astype(BF16))
                    mts.append((GmT * DmT).astype(BF16))
                s['dXm'] = _dot(jnp.concatenate(mts, axis=1), halves(s['dy2'], 0).astype(BF16), _NN)
            Wg = _dot(triu_b, jnp.concatenate(qs, axis=1), _NN)
            place = (wc == g * heads_per_group + wl // L).astype(BF16)
            da_q = da_q + _dot(jnp.where(below_diag, Wg, 0.0).astype(BF16), place, _NN)
            dBg = jnp.zeros((L, N), F32)
            dCg = jnp.zeros((L, N), F32)
            des, ddss, dxxs = [], [], []
            for s in st:
                h0, sl, x2, dy2, X2 = s['h0'], s['sl'], s['x2'], s['dy2'], s['X2']
                dR2b = (s['e2'] * dy2).astype(BF16)
                dCg = dCg + _dot(dR2b, s['H2b'], _NN)
                dHr = _dot(dR2b, Cg, _TN)
                dBg = dBg + _dot((X2 * s['ds2']).astype(BF16), s['dHnb'], _NN)
                dX2 = s['ds2'] * s['dXd'] + s['dXm']
                des.append(dy2 * s['R2'])
                ddss.append(s['dXd'] * X2)
                dxxs.append(dX2 * x2)
                prod = s['dHn'] * s['H2']
                for i, h in enumerate((h0, h0 + 1)):
                    rows = jnp.sum(prod[i * P:(i + 1) * P], axis=0, keepdims=True)
                    dcd_acc = dcd_acc + jnp.sum(rows, axis=1, keepdims=True) * (lane1 == h).astype(F32)
                dd_acc = dd_acc + head_sum(jnp.sum(dy2 * x2, axis=0, keepdims=True), h0)
                dsk2 = jnp.where(lane1 < P, dsk[:, h0:h0 + 1], dsk[:, h0 + 1:h0 + 2])
                dx_ref[:, sl] = dX2 * s['dt2'] + dy2 * dsk2
                dstate[sl, :] = jnp.where(row_lo, cd[:, h0:h0 + 1], cd[:, h0 + 1:h0 + 2]) * s['dHn'] + dHr
            stack = jnp.concatenate([jnp.concatenate(v, axis=1) for v in (des, ddss, dxxs)], axis=0).astype(BF16)
            sums = sums + _dot(stack, (gc == g * heads_per_group + gl // P).astype(BF16), _NN)
            dGb = dG.astype(BF16)
            dx_ref[:, bsl] = dBg + _dot(dGb, Cg, _TN)
            dx_ref[:, csl] = dCg + _dot(dGb, Bg, _NN)
        t1 = sums[L:2 * L] * ds
        tail = jnp.sum(t1, axis=0, keepdims=True) + dcd_acc * cd
        dcs = sums[0:L] * e - t1 + jnp.where(last_row, tail, 0.0)
        da_ref[...] = jnp.dot(triu, dcs, precision=HIGHEST, preferred_element_type=F32) + da_q
        ddtx_ref[...] = sums[2 * L:3 * L]
        dd_ref[...] += dd_acc

    vec = pl.BlockSpec((1, LANES), lambda i: (0, 0))
    rev = lambda i: (nc - 1 - i, 0)
    return pl.pallas_call(
        body, name=name,
        out_shape=(_sds((T, XBC), F32), _sds((T, LANES), F32), _sds((T, LANES), F32), _sds((1, LANES), F32)),
        grid=(nc,),
        in_specs=[pl.BlockSpec((L, XBC), rev), pl.BlockSpec((L, LANES), rev),
                  pl.BlockSpec((HR, L), lambda i: (0, nc - 1 - i)), vec, pl.BlockSpec((HR, 1), lambda i: (0, 0)), vec,
                  pl.BlockSpec((1, SX, N), lambda i: (nc - 1 - i, 0, 0)), pl.BlockSpec((L, SX), rev)],
        out_specs=(pl.BlockSpec((L, XBC), rev), pl.BlockSpec((L, LANES), rev), pl.BlockSpec((L, LANES), rev), vec),
        scratch_shapes=[pltpu.VMEM((SX, N), F32), pltpu.VMEM((L, LANES), F32), pltpu.VMEM((HR, L), F32)],
        compiler_params=_cp("arbitrary"))(xbc, dt, dtT, a_col, a_row, d_col, hst, dy)


def _rms_fwd(y, u, zoff, w, *, name):
    T, SX = y.shape
    tt = _tile(T, 256)
    gs = SX // SSM_GROUPS
    zb = zoff // SX

    def body(y_ref, z_ref, w_ref, o_ref):
        for g in range(SSM_GROUPS):
            sl = slice(g * gs, (g + 1) * gs)
            z = _f32(z_ref[:, sl])
            yg = y_ref[:, sl] * (z * _sigmoid(z))
            rstd = lax.rsqrt(jnp.mean(yg * yg, axis=-1, keepdims=True) + RMS_EPS)
            o_ref[:, sl] = (yg * rstd * w_ref[:, sl]).astype(BF16)

    row = pl.BlockSpec((tt, SX), lambda i: (i, 0))
    return pl.pallas_call(body, name=name, out_shape=_sds((T, SX), BF16), grid=(T // tt,),
                          in_specs=[row, pl.BlockSpec((tt, SX), lambda i: (i, zb)), pl.BlockSpec((1, SX), lambda i: (0, 0))],
                          out_specs=row, compiler_params=_cp("parallel"))(y, u, w)


def _rms_bwd(y, u, zoff, w, dyn, *, name):
    T, SX = y.shape
    tt = _tile(T, 256)
    gs = SX // SSM_GROUPS
    zb = zoff // SX

    def body(y_ref, z_ref, w_ref, d_ref, dy_ref, dz_ref, dw_ref):
        i = pl.program_id(0)

        @pl.when(i == 0)
        def _():
            dw_ref[...] = jnp.zeros_like(dw_ref)

        for g in range(SSM_GROUPS):
            sl = slice(g * gs, (g + 1) * gs)
            z, yv, d = _f32(z_ref[:, sl]), y_ref[:, sl], d_ref[:, sl]
            sz, dsz = _silu_and_grad(z)
            yg = yv * sz
            rstd = lax.rsqrt(jnp.mean(yg * yg, axis=-1, keepdims=True) + RMS_EPS)
            t = yg * rstd
            dw_ref[:, sl] += jnp.sum(d * t, axis=0, keepdims=True)
            dt_ = d * w_ref[:, sl]
            dyg = rstd * (dt_ - t * jnp.mean(dt_ * t, axis=-1, keepdims=True))
            dy_ref[:, sl] = dyg * sz
            dz_ref[:, sl] = (dyg * yv * dsz).astype(BF16)

    row = pl.BlockSpec((tt, SX), lambda i: (i, 0))
    vec = pl.BlockSpec((1, SX), lambda i: (0, 0))
    return pl.pallas_call(body, name=name, out_shape=(_sds((T, SX), F32), _sds((T, SX), BF16), _sds((1, SX), F32)),
                          grid=(T // tt,), in_specs=[row, pl.BlockSpec((tt, SX), lambda i: (i, zb)), vec, row],
                          out_specs=(row, row, vec), compiler_params=_cp("arbitrary"))(y, u, w, dyn)


def _gate_fwd(u, goff, ya, yb, *, name):
    T, D = ya.shape
    tt = _tile(T, 512)
    gb = goff // D

    def body(ga_ref, gb_ref, ya_ref, yb_ref, o_ref):
        o_ref[...] = (_sigmoid(_f32(ga_ref[...])) * ya_ref[...] + _sigmoid(_f32(gb_ref[...])) * yb_ref[...]).astype(BF16)

    row = pl.BlockSpec((tt, D), lambda i: (i, 0))
    return pl.pallas_call(body, name=name, out_shape=_sds((T, D), BF16), grid=(T // tt,),
                          in_specs=[pl.BlockSpec((tt, D), lambda i: (i, gb)), pl.BlockSpec((tt, D), lambda i: (i, gb + 1)), row, row],
                          out_specs=row, compiler_params=_cp("parallel"))(u, u, ya, yb)


def _gate_bwd(u, goff, ya, yb, dm, *, name):
    T, D = ya.shape
    tt = _tile(T, 512)
    gb = goff // D

    def body(ga_ref, gb_ref, ya_ref, yb_ref, dm_ref, dya_ref, dyb_ref, dg_ref):
        d = dm_ref[...]
        sa, sb = _sigmoid(_f32(ga_ref[...])), _sigmoid(_f32(gb_ref[...]))
        dya_ref[...] = (d * sa).astype(BF16)
        dyb_ref[...] = (d * sb).astype(BF16)
        dg_ref[...] = jnp.concatenate([d * ya_ref[...] * sa * (1.0 - sa), d * yb_ref[...] * sb * (1.0 - sb)], axis=1).astype(BF16)

    row = pl.BlockSpec((tt, D), lambda i: (i, 0))
    return pl.pallas_call(body, name=name, out_shape=(_sds((T, D), BF16), _sds((T, D), BF16), _sds((T, 2 * D), BF16)),
                          grid=(T // tt,),
                          in_specs=[pl.BlockSpec((tt, D), lambda i: (i, gb)), pl.BlockSpec((tt, D), lambda i: (i, gb + 1)), row, row, row],
                          out_specs=(row, row, pl.BlockSpec((tt, 2 * D), lambda i: (i, 0))),
                          compiler_params=_cp("parallel"))(u, u, ya, yb, dm)


def _adamw_math(w, gg, m, v):
    nm = ADAM_B1 * m + (1.0 - ADAM_B1) * gg
    nv = ADAM_B2 * v + (1.0 - ADAM_B2) * (gg * gg)
    m_hat = nm / (1.0 - ADAM_B1 ** ADAM_STEP)
    v_hat = nv / (1.0 - ADAM_B2 ** ADAM_STEP)
    return -ADAM_LR * (m_hat / (jnp.sqrt(v_hat) + ADAM_EPS) + ADAM_WD * w), nm, nv


def _adamw(w, g, m, v, *, name):
    R, C = w.shape
    tr = _pick(R, 256, 8)

    def body(w_ref, g_ref, m_ref, v_ref, d_ref, nm_ref, nv_ref):
        d_ref[...], nm_ref[...], nv_ref[...] = _adamw_math(w_ref[...], g_ref[...], m_ref[...], v_ref[...])

    blk = pl.BlockSpec((tr, C), lambda i: (i, 0))
    out = _sds((R, C), F32)
    return pl.pallas_call(body, name=name, out_shape=(out, out, out), grid=(R // tr,), in_specs=[blk] * 4,
                          out_specs=(blk,) * 3, compiler_params=_cp("parallel"))(w, g, m, v)


def _sum_adamw(parts, w, m, v, l, *, name):
    shape = w.shape[1:]
    C = shape[-1]
    R = w[0].size // C
    tr = _pick(R, 256, 8)
    if tr % 8:
        w, m, v, l = w[l:l + 1], m[l:l + 1], v[l:l + 1], 0
    lb = l * (R // tr)

    def body(p_ref, w_ref, m_ref, v_ref, g_ref, d_ref, nm_ref, nv_ref):
        gg = p_ref[0].astype(F32)
        for k in range(1, N_DEV):
            gg = gg + p_ref[k].astype(F32)
        g_ref[...] = gg
        d_ref[...], nm_ref[...], nv_ref[...] = _adamw_math(w_ref[...], gg, m_ref[...], v_ref[...])

    blk = pl.BlockSpec((tr, C), lambda i: (i, 0))
    lblk = pl.BlockSpec((tr, C), lambda i: (i + lb, 0))
    out = _sds((R, C), F32)
    stacked = (w.shape[0] * R, C)
    res = pl.pallas_call(body, name=name, out_shape=(out,) * 4, grid=(R // tr,),
                         in_specs=[pl.BlockSpec((N_DEV, tr, C), lambda i: (0, i, 0)), lblk, lblk, lblk],
                         out_specs=(blk,) * 4, compiler_params=_cp("parallel"))(
        parts.reshape(N_DEV, R, C), w.reshape(stacked), m.reshape(stacked), v.reshape(stacked))
    return tuple(r.reshape(shape) for r in res)


def _sum_slots(x, *, name):
    n, R, C = x.shape
    tr = _tile(R, 512)

    def body(x_ref, o_ref):
        acc = x_ref[0].astype(F32)
        for k in range(1, n):
            acc = acc + x_ref[k].astype(F32)
        o_ref[...] = acc

    return pl.pallas_call(body, name=name, out_shape=_sds((R, C), F32), grid=(R // tr,),
                          in_specs=[pl.BlockSpec((n, tr, C), lambda i: (0, i, 0))],
                          out_specs=pl.BlockSpec((tr, C), lambda i: (i, 0)), compiler_params=_cp("parallel"))(x)


def _exchange(xs, *, scatter, name):
    n = len(xs)

    def body(*refs):
        x_refs, o_refs = refs[:n], refs[n:2 * n]
        send_sems, recv_sems, local_sems = refs[2 * n:]
        mx, my, mc = lax.axis_index("x"), lax.axis_index("y"), lax.axis_index("c")
        me = 4 * mx + 2 * my + mc

        def src(i, d):
            return x_refs[i].at[d] if scatter else x_refs[i]

        locals_ = [pltpu.make_async_copy(src(i, me), o_refs[i].at[me], local_sems.at[i]) for i in range(n)]
        for cp in locals_:
            cp.start()
        sends, recvs = [], []
        for k in range(1, N_DEV):
            px = 1 - mx if k & 4 else mx
            py = 1 - my if k & 2 else my
            pc = 1 - mc if k & 1 else mc
            peer = 4 * px + 2 * py + pc
            for i in range(n):
                common = dict(send_sem=send_sems.at[k - 1, i], recv_sem=recv_sems.at[k - 1, i],
                              device_id=(px, py, pc), device_id_type=pl.DeviceIdType.MESH)
                sends.append(pltpu.make_async_remote_copy(src_ref=src(i, peer), dst_ref=o_refs[i].at[me], **common))
                recvs.append(pltpu.make_async_remote_copy(src_ref=src(i, peer), dst_ref=o_refs[i].at[peer], **common))
        for cp in sends:
            cp.start()
        for cp in recvs:
            cp.wait_recv()
        for cp in sends:
            cp.wait_send()
        for cp in locals_:
            cp.wait()

    any_spec = pl.BlockSpec(memory_space=pl.ANY)
    out_shape = tuple(_sds(x.shape if scatter else (N_DEV,) + x.shape, x.dtype) for x in xs)
    return pl.pallas_call(
        body, name=name, out_shape=out_shape, in_specs=[any_spec] * n, out_specs=(any_spec,) * n,
        scratch_shapes=[pltpu.SemaphoreType.DMA((N_DEV - 1, n)), pltpu.SemaphoreType.DMA((N_DEV - 1, n)),
                        pltpu.SemaphoreType.DMA((n,))])(*xs)


def _to_rows(flat, row_mult):
    n = flat.shape[-1]
    per = row_mult * LANES
    pad = (-n) % per
    flat = jnp.pad(flat, [(0, 0)] * (flat.ndim - 1) + [(0, pad)])
    return flat.reshape(flat.shape[:-1] + ((n + pad) // LANES, LANES))


def _peers():
    mx, my, mc = lax.axis_index("x"), lax.axis_index("y"), lax.axis_index("c")
    out = []
    for k in range(1, N_DEV):
        px = 1 - mx if k & 4 else mx
        py = 1 - my if k & 2 else my
        pc = 1 - mc if k & 1 else mc
        out.append(((px, py, pc), 4 * px + 2 * py + pc))
    return 4 * mx + 2 * my + mc, out


_HBM = pl.BlockSpec(memory_space=pltpu.HBM)
_SEM = pl.BlockSpec(memory_space=pltpu.SEMAPHORE)


def _exchange_start(xs, *, scatter, name):
    n = len(xs)

    def body(*refs):
        x_refs, land_refs, send_sems, recv_sems, token = refs[:n], refs[n:2 * n], refs[2 * n], refs[2 * n + 1], refs[-1]
        me, peers = _peers()
        for k, (dev, peer) in enumerate(peers):
            for i in range(n):
                pltpu.make_async_remote_copy(
                    src_ref=x_refs[i].at[peer] if scatter else x_refs[i], dst_ref=land_refs[i].at[me],
                    send_sem=send_sems.at[k * n + i], recv_sem=recv_sems.at[k * n + i],
                    device_id=dev, device_id_type=pl.DeviceIdType.MESH).start()
        token[...] = jnp.zeros_like(token)

    land_shapes = [x.shape if scatter else (N_DEV,) + x.shape for x in xs]
    lands = [pltpu.with_memory_space_constraint(lax.empty(s, x.dtype), pltpu.HBM) for s, x in zip(land_shapes, xs)]
    srcs = [pltpu.with_memory_space_constraint(x, pltpu.HBM) for x in xs]
    out = pl.pallas_call(
        body, name=name,
        out_shape=(pltpu.SemaphoreType.DMA(((N_DEV - 1) * n,)), pltpu.SemaphoreType.DMA(((N_DEV - 1) * n,)),
                   *[pltpu.HBM(x.shape, x.dtype) for x in xs], *[pltpu.HBM(s, x.dtype) for s, x in zip(land_shapes, xs)],
                   _sds((8, LANES), F32)),
        in_specs=(_HBM,) * (2 * n), out_specs=(_SEM, _SEM) + (_HBM,) * (2 * n) + (pl.BlockSpec(memory_space=pltpu.VMEM),),
        input_output_aliases={i: 2 + i for i in range(2 * n)},
        compiler_params=pltpu.CompilerParams(has_side_effects=pltpu.SideEffectType.DATAFLOW_SIDE_EFFECTING))(*srcs, *lands)
    return (out[0], out[1], list(out[2:2 + n]), list(out[2 + n:2 + 2 * n])), out[-1]


def _exchange_wait(handle, after, *, scatter, name):
    send_sems, recv_sems, srcs, lands = handle
    n = len(srcs)

    def body(*refs):
        x_refs, land_refs, send_sems, recv_sems = refs[:n], refs[n:2 * n], refs[2 * n], refs[2 * n + 1]
        me, peers = _peers()
        for k, (dev, peer) in enumerate(peers):
            for i in range(n):
                cp = pltpu.make_async_remote_copy(
                    src_ref=x_refs[i].at[peer] if scatter else x_refs[i], dst_ref=land_refs[i].at[peer],
                    send_sem=send_sems.at[k * n + i], recv_sem=recv_sems.at[k * n + i],
                    device_id=dev, device_id_type=pl.DeviceIdType.MESH)
                cp.wait_send()
                cp.wait_recv()

    out = pl.pallas_call(
        body, name=name, out_shape=tuple(pltpu.HBM(a.shape, a.dtype) for a in srcs + lands),
        in_specs=(_HBM,) * (2 * n) + (_SEM, _SEM, pl.BlockSpec(memory_space=pl.ANY)), out_specs=(_HBM,) * (2 * n),
        input_output_aliases={i: i for i in range(2 * n)},
        compiler_params=pltpu.CompilerParams(has_side_effects=pltpu.SideEffectType.DATAFLOW_SIDE_EFFECTING))(
        *srcs, *lands, send_sems, recv_sems, after)
    return list(out[n:])


def _with_own_slot(lands, xs, *, scatter):
    me = 4 * lax.axis_index("x") + 2 * lax.axis_index("y") + lax.axis_index("c")
    out = []
    for land, x in zip(lands, xs):
        own = lax.dynamic_index_in_dim(x, me, 0, keepdims=True) if scatter else x[None]
        out.append(lax.dynamic_update_index_in_dim(land, own, me, 0))
    return out


def _wire_shards(p, l):
    return [p[k][l].astype(WIRE) if SHARDED[k][1] else p[k][l] for k in SHARDED]


def _full_weights(got, p):
    full = {}
    for k, g in zip(SHARDED, got):
        axis, shape = SHARDED[k][0] - 1, p[k].shape[1:]
        g = jnp.moveaxis(g, 0, axis)
        full[k] = g.reshape(shape[:axis] + (N_DEV * shape[axis],) + shape[axis + 1:])
    return full


def _grad_parts(grads, p):
    parts = []
    for k in SHARDED:
        axis, g = SHARDED[k][0] - 1, grads[k]
        g = g.reshape(g.shape[:axis] + (N_DEV, p[k].shape[1:][axis]) + g.shape[axis + 1:])
        parts.append(jnp.moveaxis(g, axis, 0).astype(WIRE))
    return parts


def _allreduce_small(vals):
    metas = [(k, v.shape, v.size) for k, v in vals.items()]
    flat = jnp.concatenate([v.reshape(-1) for v in vals.values()])
    got, = _exchange([_to_rows(flat, 8)], scatter=False, name="gather_small_grads")
    summed = _sum_slots(got, name="sum_small_grads").reshape(-1)
    out, off = {}, 0
    for k, shape, n in metas:
        out[k] = summed[off:off + n].reshape(shape)
        off += n
    return out


def _interleave(a, g, tile):
    C = a.shape[-1]
    n = C // tile
    lead = a.shape[:-1]
    return jnp.concatenate([a.reshape(lead + (n, 1, tile)), g.reshape(lead + (n, 1, tile))], axis=-2).reshape(lead + (2 * C,))


def _deinterleave(w, tile):
    C = w.shape[-1] // 2
    n = C // tile
    lead = w.shape[:-1]
    r = w.reshape(lead + (n, 2, tile))
    return r[..., 0, :].reshape(lead + (C,)), r[..., 1, :].reshape(lead + (C,))


def _row(v):
    return v.reshape(1, -1).astype(F32)


def _lanes(v):
    return jnp.pad(v.astype(F32), (0, LANES - v.shape[0])).reshape(1, LANES)


def _layer_weights(p, full, l, dims):
    D, SX, XBC, H, FF = dims
    gt, ft = _tile(D, GLU_TILE), _tile(FF, FFN_TILE)
    w_in = full['w_in']
    o = 0
    seg = {}
    for nm, n in (('a', D), ('g', D), ('z', SX), ('xbc', XBC), ('dt', H), ('ga', D), ('gb', D)):
        seg[nm] = w_in[:, o:o + n]
        o += n
    w_pieces = [_interleave(seg['a'], seg['g'], gt), seg['z'], seg['xbc'], jnp.concatenate([seg['ga'], seg['gb']], axis=1)]
    w_main = jnp.concatenate(w_pieces, axis=1)
    w_dt = jnp.pad(seg['dt'], ((0, 0), (0, LANES - H)))
    up = full['w_ffn_up']
    fw = full['ffn_dw_w']
    fb = p['ffn_dw_b'][l]
    a_head = -jnp.exp(p['ssm_a_log'][l].astype(F32))
    hr = -(-H // 8) * 8
    return dict(
        w_main=w_main, w_pieces=w_pieces, w_dt=w_dt,
        conv_w=full['conv_dw_w'], conv_b=_row(p['conv_dw_b'][l]), conv_g=_row(p['conv_ln_g'][l]), conv_beta=_row(p['conv_ln_b'][l]),
        w_conv_out=full['w_conv_out'],
        ssm_w=full['ssm_conv_w'], ssm_b=_row(p['ssm_conv_b'][l]),
        dt_bias=_lanes(p['ssm_dt_bias'][l]), a_col=_lanes(a_head),
        a_row=jnp.pad(a_head, (0, hr - H)).reshape(hr, 1), d_col=_lanes(p['ssm_d'][l]),
        norm_w=_row(p['ssm_norm_w'][l]), w_ssm_out=full['w_ssm_out'], w_o=full['w_o'],
        ln1_g=_row(p['ln1_g'][l]), ln1_b=_row(p['ln1_b'][l]),
        w_up=_interleave(up[:, :FF], up[:, FF:], ft),
        ffn_w=_interleave(fw[:, :FF], fw[:, FF:], ft), ffn_b=_row(_interleave(fb[:FF], fb[FF:], ft)),
        w_down=full['w_ffn_down'], ln2_g=_row(p['ln2_g'][l]), ln2_b=_row(p['ln2_b'][l]))


def _layer_fwd(h, hb, W, l, dims):
    D, SX, XBC, H, FF = dims
    o_z, o_xbc, o_gate = 2 * D, 2 * D + SX, 2 * D + SX + XBC
    hr = W['a_row'].shape[0]
    t = f"l{l}"
    u = _mm(hb, W['w_main'], mode='nn', out_dtype=BF16, name=t + "_in_proj")
    udt = _mm(hb, W['w_dt'], mode='nn', out_dtype=F32, name=t + "_dt_proj")
    v1, v3 = _conf_fwd(u, W['conv_w'], W['conv_b'], W['conv_g'], W['conv_beta'], D=D, name=t + "_conf_fwd")
    ya = _mm(v3, W['w_conv_out'], mode='nn', out_dtype=F32, name=t + "_conv_out")
    xbc = _conv_act_fwd(u, o_xbc, W['ssm_w'], W['ssm_b'], ffn=False, name=t + "_ssm_conv")
    dt = _dt_fwd(udt, W['dt_bias'], H=H, name=t + "_dt")
    dtT = jnp.pad(dt[:, :H].T, ((0, hr - H), (0, 0)))
    y, hst = _ssd_fwd(xbc, dt, dtT, W['a_col'], W['a_row'], W['d_col'], H=H, name=t + "_ssd_fwd")
    yn = _rms_fwd(y, u, o_z, W['norm_w'], name=t + "_rms_fwd")
    yb = _mm(yn, W['w_ssm_out'], mode='nn', out_dtype=F32, name=t + "_ssm_out")
    m = _gate_fwd(u, o_gate, ya, yb, name=t + "_gate_fwd")
    h1, s1, h1b = _mm_res_ln(m, W['w_o'], h, W['ln1_g'], W['ln1_b'], name=t + "_mix_ln1")
    uf = _mm(h1b, W['w_up'], mode='nn', out_dtype=BF16, name=t + "_ffn_up")
    act = _conv_act_fwd(uf, 0, W['ffn_w'], W['ffn_b'], ffn=True, name=t + "_ffn_conv")
    h2, s2, h2b = _mm_res_ln(act, W['w_down'], h1, W['ln2_g'], W['ln2_b'], name=t + "_ffn_down_ln2")
    saved = dict(hb=hb, u=u, udt=udt, v1=v1, v3=v3, ya=ya, xbc=xbc, dt=dt, dtT=dtT, y=y, hst=hst, yn=yn, yb=yb, m=m,
                 h1b=h1b, s1=s1, uf=uf, act=act, s2=s2)
    return h2, h2b, saved


def _layer_bwd(dh2, W, S, l, dims):
    D, SX, XBC, H, FF = dims
    o_z, o_xbc, o_gate = 2 * D, 2 * D + SX, 2 * D + SX + XBC
    t = f"l{l}"
    g = {}
    ds2, g['ln2_g'], g['ln2_b'] = _ln_bwd(S['s2'], dh2, W['ln2_g'], W['ln2_b'], silu=False, name=t + "_ln2_bwd")
    g['w_down'] = _mm(S['act'], ds2, mode='tn', out_dtype=F32, name=t + "_dw_down")
    dact = _mm(ds2, W['w_down'], mode='nt', out_dtype=F32, name=t + "_dact")
    duf, g['ffn_w'], g['ffn_b'] = _conv_act_bwd(S['uf'], 0, W['ffn_w'], W['ffn_b'], dact, ffn=True, name=t + "_ffn_conv_bwd")
    g['w_up'] = _mm(S['h1b'], duf, mode='tn', out_dtype=F32, name=t + "_dw_up")
    dh1 = _mm(duf, W['w_up'], mode='nt', out_dtype=F32, res=ds2, res_scale=DN_ALPHA, name=t + "_dh1")
    ds1, g['ln1_g'], g['ln1_b'] = _ln_bwd(S['s1'], dh1, W['ln1_g'], W['ln1_b'], silu=False, name=t + "_ln1_bwd")
    g['w_o'] = _mm(S['m'], ds1, mode='tn', out_dtype=F32, name=t + "_dw_o")
    dm = _mm(ds1, W['w_o'], mode='nt', out_dtype=F32, name=t + "_dm")
    dya, dyb, dgate = _gate_bwd(S['u'], o_gate, S['ya'], S['yb'], dm, name=t + "_gate_bwd")
    g['w_conv_out'] = _mm(S['v3'], dya, mode='tn', out_dtype=F32, name=t + "_dw_conv_out")
    dv3 = _mm(dya, W['w_conv_out'], mode='nt', out_dtype=F32, name=t + "_dv3")
    dv1, g['conv_g'], g['conv_beta'] = _ln_bwd(S['v1'], dv3, W['conv_g'], W['conv_beta'], silu=True, name=t + "_conv_ln_bwd")
    dglu, g['conv_w'], g['conv_b'] = _glu_conv_bwd(dv1, S['u'], W['conv_w'], name=t + "_conf_conv_bwd")
    g['w_ssm_out'] = _mm(S['yn'], dyb, mode='tn', out_dtype=F32, name=t + "_dw_ssm_out")
    dyn = _mm(dyb, W['w_ssm_out'], mode='nt', out_dtype=F32, name=t + "_dyn")
    dy, dz, g['norm_w'] = _rms_bwd(S['y'], S['u'], o_z, W['norm_w'], dyn, name=t + "_rms_bwd")
    dxbc_c, da, ddtx, g['d'] = _ssd_bwd(S['xbc'], S['dt'], S['dtT'], W['a_col'], W['a_row'], W['d_col'], S['hst'], dy,
                                        H=H, name=t + "_ssd_bwd")
    ddt_raw, g['dt_bias'], g['a_log'] = _dt_bwd(da, ddtx, S['dt'], S['udt'], W['dt_bias'], W['a_col'], H=H, name=t + "_dt_bwd")
    dxbc, g['ssm_w'], g['ssm_b'] = _conv_act_bwd(S['u'], o_xbc, W['ssm_w'], W['ssm_b'], dxbc_c, ffn=False, name=t + "_ssm_conv_bwd")
    du = [dglu, dz, dxbc, dgate]
    g['w_pieces'] = [_mm(S['hb'], d, mode='tn', out_dtype=F32, name=f"{t}_dw_in{i}") for i, d in enumerate(du)]
    g['w_dt'] = _mm(S['hb'], ddt_raw, mode='tn', out_dtype=F32, name=t + "_dw_dt")
    dh = _mm_nt_cat(du + [ddt_raw], W['w_pieces'] + [W['w_dt']], ds1, DN_ALPHA, name=t + "_dh_in")
    return dh, g


def _layer_grads_to_params(g, dims):
    D, SX, XBC, H, FF = dims
    gt, ft = _tile(D, GLU_TILE), _tile(FF, FFN_TILE)
    glu, dz, dxbc, dgate = g['w_pieces']
    da, dg = _deinterleave(glu, gt)
    w_in = jnp.concatenate([da, dg, dz, dxbc, g['w_dt'][:, :H], dgate], axis=1)
    fwg, fwv = _deinterleave(g['ffn_w'], ft)
    fbg, fbv = _deinterleave(g['ffn_b'], ft)
    upg, upv = _deinterleave(g['w_up'], ft)
    return dict(
        w_in=w_in, conv_dw_w=g['conv_w'], conv_dw_b=g['conv_b'][0], conv_ln_g=g['conv_g'][0], conv_ln_b=g['conv_beta'][0],
        w_conv_out=g['w_conv_out'], ssm_conv_w=g['ssm_w'], ssm_conv_b=g['ssm_b'][0],
        ssm_dt_bias=g['dt_bias'][0, :H], ssm_a_log=g['a_log'][0, :H], ssm_d=g['d'][0, :H], ssm_norm_w=g['norm_w'][0],
        w_ssm_out=g['w_ssm_out'], w_o=g['w_o'], ln1_g=g['ln1_g'][0], ln1_b=g['ln1_b'][0],
        w_ffn_up=jnp.concatenate([upg, upv], axis=1), ffn_dw_w=jnp.concatenate([fwg, fwv], axis=1),
        ffn_dw_b=jnp.concatenate([fbg, fbv], axis=1)[0], w_ffn_down=g['w_down'], ln2_g=g['ln2_g'][0], ln2_b=g['ln2_b'][0])


def kernel(x, ln_in_g, ln_in_b, w_in, conv_dw_w, conv_dw_b, conv_ln_g, conv_ln_b, w_conv_out, ssm_conv_w, ssm_conv_b, ssm_dt_bias, ssm_a_log, ssm_d, ssm_norm_w, w_ssm_out, w_o, ln1_g, ln1_b, w_ffn_up, ffn_dw_w, ffn_dw_b, w_ffn_down, ln2_g, ln2_b, loss_target, m_ln_in_g, m_ln_in_b, m_w_in, m_conv_dw_w, m_conv_dw_b, m_conv_ln_g, m_conv_ln_b, m_w_conv_out, m_ssm_conv_w, m_ssm_conv_b, m_ssm_dt_bias, m_ssm_a_log, m_ssm_d, m_ssm_norm_w, m_w_ssm_out, m_w_o, m_ln1_g, m_ln1_b, m_w_ffn_up, m_ffn_dw_w, m_ffn_dw_b, m_w_ffn_down, m_ln2_g, m_ln2_b, v_ln_in_g, v_ln_in_b, v_w_in, v_conv_dw_w, v_conv_dw_b, v_conv_ln_g, v_conv_ln_b, v_w_conv_out, v_ssm_conv_w, v_ssm_conv_b, v_ssm_dt_bias, v_ssm_a_log, v_ssm_d, v_ssm_norm_w, v_w_ssm_out, v_w_o, v_ln1_g, v_ln1_b, v_w_ffn_up, v_ffn_dw_w, v_ffn_dw_b, v_w_ffn_down, v_ln2_g, v_ln2_b):
    weights = (ln_in_g, ln_in_b, w_in, conv_dw_w, conv_dw_b, conv_ln_g, conv_ln_b, w_conv_out, ssm_conv_w, ssm_conv_b,
               ssm_dt_bias, ssm_a_log, ssm_d, ssm_norm_w, w_ssm_out, w_o, ln1_g, ln1_b, w_ffn_up, ffn_dw_w, ffn_dw_b,
               w_ffn_down, ln2_g, ln2_b)
    moments_m = (m_ln_in_g, m_ln_in_b, m_w_in, m_conv_dw_w, m_conv_dw_b, m_conv_ln_g, m_conv_ln_b, m_w_conv_out,
                 m_ssm_conv_w, m_ssm_conv_b, m_ssm_dt_bias, m_ssm_a_log, m_ssm_d, m_ssm_norm_w, m_w_ssm_out, m_w_o,
                 m_ln1_g, m_ln1_b, m_w_ffn_up, m_ffn_dw_w, m_ffn_dw_b, m_w_ffn_down, m_ln2_g, m_ln2_b)
    moments_v = (v_ln_in_g, v_ln_in_b, v_w_in, v_conv_dw_w, v_conv_dw_b, v_conv_ln_g, v_conv_ln_b, v_w_conv_out,
                 v_ssm_conv_w, v_ssm_conv_b, v_ssm_dt_bias, v_ssm_a_log, v_ssm_d, v_ssm_norm_w, v_w_ssm_out, v_w_o,
                 v_ln1_g, v_ln1_b, v_w_ffn_up, v_ffn_dw_w, v_ffn_dw_b, v_w_ffn_down, v_ln2_g, v_ln2_b)
    p = dict(zip(PARAMS, weights))
    pm = dict(zip(PARAMS, moments_m))
    pv = dict(zip(PARAMS, moments_v))

    T, D = x.shape[1], x.shape[2]
    SX = w_ssm_out.shape[1] * N_DEV
    XBC = ssm_conv_b.shape[-1]
    H = ssm_d.shape[-1]
    FF = ffn_dw_b.shape[-1] // 2
    dims = (D, SX, XBC, H, FF)
    depth = w_in.shape[0]

    got = _exchange(_wire_shards(p, 0), scatter=False, name="gather_l0")
    pending, tok = [], jnp.zeros((8, LANES), F32)
    for l in range(1, depth):
        shards, got = lax.optimization_barrier((_wire_shards(p, l), got))
        handle, t = _exchange_start(shards, scatter=False, name=f"gather_l{l}_start")
        pending.append((handle, shards))
        tok = tok + t
    layers = [_layer_weights(p, _full_weights(got, p), 0, dims)]

    xs = x.reshape(T, D)
    h, hb = _ln_fwd(xs, _row(ln_in_g) + tok[0:1, 0:1], _row(ln_in_b), name="ln_in_fwd")
    saved = []
    for l in range(depth):
        if l > 0:
            handle, shards = pending[l - 1]
            got = _with_own_slot(_exchange_wait(handle, h, scatter=False, name=f"gather_l{l}_wait"), shards, scatter=False)
            layers.append(_layer_weights(p, _full_weights(got, p), l, dims))
        h, hb, s = _layer_fwd(h, hb, layers[l], l, dims)
        saved.append(s)
    loss_part, dh = _loss_fwd_bwd(h, loss_target.reshape(T, D), name="loss")

    layer_grads = [None] * depth
    pending, tok = [], jnp.zeros((8, LANES), F32)
    for l in reversed(range(depth)):
        W = dict(layers[l], ln2_g=layers[l]['ln2_g'] + tok[0:1, 0:1])
        dh, g = _layer_bwd(dh, W, saved[l], l, dims)
        layer_grads[l] = _layer_grads_to_params(g, dims)
        if l > 0:
            parts = _grad_parts(layer_grads[l], p)
            handle, tok = _exchange_start(parts, scatter=True, name=f"scatter_l{l}_start")
            pending.append((l, handle, parts))
    grad_x, dg_in, db_in = _ln_bwd(xs, dh, _row(ln_in_g), _row(ln_in_b), silu=False, name="ln_in_bwd")

    arrived = {}
    after = grad_x
    for l, handle, parts in pending:
        lands = _exchange_wait(handle, after, scatter=True, name=f"scatter_l{l}_wait")
        arrived[l] = _with_own_slot(lands, parts, scatter=True)
        after = arrived[l][0]
    parts0, after = lax.optimization_barrier((_grad_parts(layer_grads[0], p), after))
    arrived[0] = _exchange(parts0, scatter=True, name="scatter_l0")
    small = {k: jnp.stack([layer_grads[l][k] for l in range(depth)]) for k in layer_grads[0] if k not in SHARDED}
    small['ln_in_g'], small['ln_in_b'], small['loss'] = dg_in[0], db_in[0], loss_part
    grads = _allreduce_small(small)
    loss = grads.pop('loss').reshape(())

    delta, new_m, new_v = {}, {}, {}
    for i, k in enumerate(SHARDED):
        per_layer = [_sum_adamw(arrived[l][i], p[k], pm[k], pv[k], l, name=f"sum_adamw_l{l}_{k}") for l in range(depth)]
        grads[k], delta[k], new_m[k], new_v[k] = (jnp.stack([r[j] for r in per_layer]) for j in range(4))
    rest = [k for k in PARAMS if k not in SHARDED]
    flat = lambda d: _to_rows(jnp.concatenate([d[k].reshape(-1) for k in rest]), 8)
    d_, m_, v_ = _adamw(flat(p), flat(grads), flat(pm), flat(pv), name="adamw_small")
    off = 0
    for k in rest:
        n, shp = p[k].size, p[k].shape
        delta[k] = d_.reshape(-1)[off:off + n].reshape(shp)
        new_m[k] = m_.reshape(-1)[off:off + n].reshape(shp)
        new_v[k] = v_.reshape(-1)[off:off + n].reshape(shp)
        off += n

    return (loss, grad_x.reshape(x.shape), *[grads[k] for k in PARAMS], *[delta[k] for k in PARAMS],
            *[new_m[k] for k in PARAMS], *[new_v[k] for k in PARAMS])
```

```python
import math

import jax
import jax.numpy as jnp
from jax import lax
from jax.experimental import pallas as pl
from jax.experimental.pallas import tpu as pltpu

F32 = jnp.float32
BF16 = jnp.bfloat16
WIRE = jnp.bfloat16
HIGHEST = lax.Precision.HIGHEST

DEPTH = 2
SSM_STATE = 128
SSM_CHUNK = 128
SSM_GROUPS = 4
SSM_HEAD_DIM = 64
DN_ALPHA = (2 * DEPTH) ** 0.25
LN_EPS = 1e-5
RMS_EPS = 1e-5
ADAM_LR = 0.001
ADAM_B1 = 0.9
ADAM_B2 = 0.999
ADAM_EPS = 1e-08
ADAM_WD = 0.01
ADAM_STEP = 10

N_DEV = 8
LANES = 128
VMEM_LIMIT = 48 * 1024 * 1024
GLU_TILE = 1024
FFN_TILE = 2816
CONV_ROWS = 256

PARAMS = ['ln_in_g', 'ln_in_b', 'w_in', 'conv_dw_w', 'conv_dw_b', 'conv_ln_g', 'conv_ln_b', 'w_conv_out',
          'ssm_conv_w', 'ssm_conv_b', 'ssm_dt_bias', 'ssm_a_log', 'ssm_d', 'ssm_norm_w', 'w_ssm_out', 'w_o',
          'ln1_g', 'ln1_b', 'w_ffn_up', 'ffn_dw_w', 'ffn_dw_b', 'w_ffn_down', 'ln2_g', 'ln2_b']
SHARDED = {'w_in': (2, True), 'conv_dw_w': (2, False), 'w_conv_out': (1, True), 'ssm_conv_w': (2, False),
           'w_ssm_out': (1, True), 'w_o': (1, True), 'w_ffn_up': (2, True), 'ffn_dw_w': (2, False),
           'w_ffn_down': (1, True)}


def _sds(shape, dtype):
    return jax.ShapeDtypeStruct(tuple(shape), dtype)


def _tile(dim, pref):
    return pref if dim % pref == 0 else dim


def _pick(dim, pref, mult):
    best = None
    for t in range(mult, min(dim, pref) + 1, mult):
        if dim % t == 0:
            best = t
    return best or dim


def _cp(*sem):
    return pltpu.CompilerParams(dimension_semantics=sem, vmem_limit_bytes=VMEM_LIMIT)


def _f32(x):
    return x.astype(F32)


def _sigmoid(x):
    return 1.0 / (1.0 + jnp.exp(-x))


def _silu_and_grad(x):
    s = _sigmoid(x)
    return x * s, s * (1.0 + x * (1.0 - s))


def _ln_stats(s):
    mu = jnp.mean(s, axis=-1, keepdims=True)
    xc = s - mu
    var = jnp.mean(xc * xc, axis=-1, keepdims=True)
    rstd = lax.rsqrt(var + LN_EPS)
    return xc * rstd, rstd


def _dot(a, b, dims):
    return lax.dot_general(a, b, (dims, ((), ())), preferred_element_type=F32)


_NN = ((1,), (0,))
_NT = ((1,), (1,))
_TN = ((0,), (0,))


def _mm(a, b, *, mode, out_dtype, name, res=None, res_scale=1.0, tm=1024, tn=1408, tk=1408):
    if mode == 'nn':
        (M, K), (_, N) = a.shape, b.shape
    elif mode == 'nt':
        (M, K), (N, _) = a.shape, b.shape
    else:
        (K, M), (_, N) = a.shape, b.shape
        tm, tk = 1408, 1024
    tm, tn, tk = _pick(M, tm, 8), _pick(N, tn, LANES), _pick(K, tk, LANES)
    nk = K // tk
    dims = {'nn': _NN, 'nt': _NT, 'tn': _TN}[mode]

    def body(*refs):
        if res is None:
            a_ref, b_ref, o_ref = refs[:3]
            r_ref = None
        else:
            a_ref, b_ref, r_ref, o_ref = refs[:4]
        p = _dot(a_ref[...].astype(BF16), b_ref[...].astype(BF16), dims)

        def finish(acc):
            if r_ref is not None:
                acc = acc + res_scale * r_ref[...]
            o_ref[...] = acc.astype(out_dtype)

        if nk == 1:
            finish(p)
        else:
            acc_ref = refs[-1]
            k = pl.program_id(2)

            @pl.when(k == 0)
            def _():
                acc_ref[...] = p

            @pl.when(k > 0)
            def _():
                acc_ref[...] += p

            @pl.when(k == nk - 1)
            def _():
                finish(acc_ref[...])

    if mode == 'nn':
        a_spec = pl.BlockSpec((tm, tk), lambda i, j, k: (i, k))
        b_spec = pl.BlockSpec((tk, tn), lambda i, j, k: (k, j))
    elif mode == 'nt':
        a_spec = pl.BlockSpec((tm, tk), lambda i, j, k: (i, k))
        b_spec = pl.BlockSpec((tn, tk), lambda i, j, k: (j, k))
    else:
        a_spec = pl.BlockSpec((tk, tm), lambda i, j, k: (k, i))
        b_spec = pl.BlockSpec((tk, tn), lambda i, j, k: (k, j))
    o_spec = pl.BlockSpec((tm, tn), lambda i, j, k: (i, j))
    in_specs = [a_spec, b_spec] + ([o_spec] if res is not None else [])
    args = (a, b) + ((res,) if res is not None else ())
    return pl.pallas_call(
        body, name=name, out_shape=_sds((M, N), out_dtype), grid=(M // tm, N // tn, nk),
        in_specs=in_specs, out_specs=o_spec,
        scratch_shapes=[pltpu.VMEM((tm, tn), F32)] if nk > 1 else [],
        compiler_params=_cp("parallel", "parallel", "arbitrary"))(*args)


def _mm_nt_cat(a_list, b_list, res, res_scale, *, name, tm=1024, tk=512):
    M, N = a_list[0].shape[0], b_list[0].shape[0]
    tm = _pick(M, tm, 8)
    tks = [_pick(a.shape[1], tk, LANES) for a in a_list]
    steps = [a.shape[1] // t for a, t in zip(a_list, tks)]
    offs = [sum(steps[:i]) for i in range(len(steps))]
    nk, n = sum(steps), len(a_list)

    def body(*refs):
        a_refs, b_refs, r_ref, o_ref, acc_ref = refs[:n], refs[n:2 * n], refs[2 * n], refs[2 * n + 1], refs[2 * n + 2]
        k = pl.program_id(1)

        @pl.when(k == 0)
        def _():
            acc_ref[...] = res_scale * r_ref[...]

        for i in range(n):
            @pl.when((k >= offs[i]) & (k < offs[i] + steps[i]))
            def _(i=i):
                acc_ref[...] += _dot(a_refs[i][...].astype(BF16), b_refs[i][...].astype(BF16), _NT)

        @pl.when(k == nk - 1)
        def _():
            o_ref[...] = acc_ref[...]

    def kmap(i):
        return lambda m, k: jnp.clip(k - offs[i], 0, steps[i] - 1)

    a_specs = [pl.BlockSpec((tm, tks[i]), lambda m, k, i=i: (m, kmap(i)(m, k))) for i in range(n)]
    b_specs = [pl.BlockSpec((N, tks[i]), lambda m, k, i=i: (0, kmap(i)(m, k))) for i in range(n)]
    row = pl.BlockSpec((tm, N), lambda m, k: (m, 0))
    return pl.pallas_call(
        body, name=name, out_shape=_sds((M, N), F32), grid=(M // tm, nk), in_specs=a_specs + b_specs + [row],
        out_specs=row, scratch_shapes=[pltpu.VMEM((tm, N), F32)],
        compiler_params=_cp("parallel", "arbitrary"))(*a_list, *b_list, res)


def _mm_res_ln(a, w, res, g, b, *, name, tm=512, tk=1408):
    (M, K), (_, N) = a.shape, w.shape
    tm, tk = _pick(M, tm, 8), _pick(K, tk, LANES)
    nk = K // tk

    def body(a_ref, w_ref, r_ref, g_ref, b_ref, h_ref, s_ref, hb_ref, acc_ref):
        k = pl.program_id(1)
        p = _dot(a_ref[...].astype(BF16), w_ref[...].astype(BF16), _NN)

        @pl.when(k == 0)
        def _():
            acc_ref[...] = p

        @pl.when(k > 0)
        def _():
            acc_ref[...] += p

        @pl.when(k == nk - 1)
        def _():
            s = DN_ALPHA * r_ref[...] + acc_ref[...]
            xhat, _ = _ln_stats(s)
            s_ref[...] = s
            h = xhat * g_ref[...] + b_ref[...]
            h_ref[...] = h
            hb_ref[...] = h.astype(BF16)

    row = pl.BlockSpec((tm, N), lambda i, k: (i, 0))
    vec = pl.BlockSpec((1, N), lambda i, k: (0, 0))
    return pl.pallas_call(
        body, name=name, out_shape=(_sds((M, N), F32), _sds((M, N), F32), _sds((M, N), BF16)), grid=(M // tm, nk),
        in_specs=[pl.BlockSpec((tm, tk), lambda i, k: (i, k)), pl.BlockSpec((tk, N), lambda i, k: (k, 0)), row, vec, vec],
        out_specs=(row, row, row), scratch_shapes=[pltpu.VMEM((tm, N), F32)],
        compiler_params=_cp("parallel", "arbitrary"))(a, w, res, g, b)


def _ln_fwd(x, g, b, *, name):
    T, D = x.shape
    tt = _tile(T, 512)

    def body(x_ref, g_ref, b_ref, o_ref, ob_ref):
        xhat, _ = _ln_stats(x_ref[...])
        h = xhat * g_ref[...] + b_ref[...]
        o_ref[...] = h
        ob_ref[...] = h.astype(BF16)

    row = pl.BlockSpec((tt, D), lambda i: (i, 0))
    vec = pl.BlockSpec((1, D), lambda i: (0, 0))
    return pl.pallas_call(body, name=name, out_shape=(_sds((T, D), F32), _sds((T, D), BF16)), grid=(T // tt,),
                          in_specs=[row, vec, vec], out_specs=(row, row), compiler_params=_cp("parallel"))(x, g, b)


def _ln_bwd(s, dy, g, b, *, silu, name):
    T, D = s.shape
    tt = _tile(T, 512)

    def body(s_ref, dy_ref, g_ref, b_ref, ds_ref, dg_ref, db_ref):
        i = pl.program_id(0)
        xhat, rstd = _ln_stats(s_ref[...])
        gg = g_ref[...]
        dyl = dy_ref[...]
        if silu:
            _, dsilu = _silu_and_grad(xhat * gg + b_ref[...])
            dyl = dyl * dsilu
        dxh = dyl * gg
        m1 = jnp.mean(dxh, axis=-1, keepdims=True)
        m2 = jnp.mean(dxh * xhat, axis=-1, keepdims=True)
        ds_ref[...] = rstd * (dxh - m1 - xhat * m2)

        @pl.when(i == 0)
        def _():
            dg_ref[...] = jnp.zeros_like(dg_ref)
            db_ref[...] = jnp.zeros_like(db_ref)

        dg_ref[...] += jnp.sum(dyl * xhat, axis=0, keepdims=True)
        db_ref[...] += jnp.sum(dyl, axis=0, keepdims=True)

    row = pl.BlockSpec((tt, D), lambda i: (i, 0))
    vec = pl.BlockSpec((1, D), lambda i: (0, 0))
    return pl.pallas_call(body, name=name, out_shape=(_sds((T, D), F32), _sds((1, D), F32), _sds((1, D), F32)),
                          grid=(T // tt,), in_specs=[row, row, vec, vec], out_specs=(row, vec, vec),
                          compiler_params=_cp("arbitrary"))(s, dy, g, b)


def _loss_fwd_bwd(h, tgt, *, name):
    T, D = h.shape
    tt = _tile(T, 512)

    def body(h_ref, t_ref, dh_ref, l_ref):
        i = pl.program_id(0)
        e = h_ref[...] - t_ref[...]
        dh_ref[...] = e * (1.0 / D)

        @pl.when(i == 0)
        def _():
            l_ref[...] = jnp.zeros_like(l_ref)

        part = jnp.sum(jnp.sum(e * e, axis=1, keepdims=True), axis=0, keepdims=True) * (0.5 / D)
        l_ref[...] += jnp.broadcast_to(part, l_ref.shape)

    row = pl.BlockSpec((tt, D), lambda i: (i, 0))
    one = pl.BlockSpec((8, LANES), lambda i: (0, 0))
    dh, l = pl.pallas_call(body, name=name, out_shape=(_sds((T, D), F32), _sds((8, LANES), F32)), grid=(T // tt,),
                           in_specs=[row, row], out_specs=(row, one), compiler_params=_cp("arbitrary"))(h, tgt)
    return l[0:1, 0:1], dh


def _halo_rows(k):
    return 32 if k > 17 else 16


def _shifted(ext, K, first):
    rolled = {0: ext}
    out = []
    for k in range(K):
        r = (first + k) % 8
        if r not in rolled:
            rolled[r] = pltpu.roll(ext, ext.shape[0] - r, 0)
        out.append((rolled[r], first + k - r))
    return out


def _taps(ext, w_ref, sl, K, first, n, reverse=False):
    acc = None
    for k, (z, base) in enumerate(_shifted(ext, K, first)):
        kw = K - 1 - k if reverse else k
        term = w_ref[kw:kw + 1, sl] * z[base:base + n]
        acc = term if acc is None else acc + term
    return acc


def _tap_sums(d, ext, dw_ref, sl, K, first):
    n = d.shape[0]
    for k, (z, base) in enumerate(_shifted(ext, K, first)):
        dw_ref[k:k + 1, sl] += jnp.sum(d * z[base:base + n], axis=0, keepdims=True)


def _strip_width(width):
    return LANES if width % LANES == 0 else width


def _for_strips(width, fn):
    sw = _strip_width(width)

    def step(s, carry):
        fn(pl.ds(pl.multiple_of(s * sw, sw), sw), s)
        return carry

    lax.fori_loop(0, width // sw, step, 0)


def _prev_rows(tt, hb):
    return lambda t: jnp.maximum(t * (tt // hb) - 1, 0)


def _next_rows(tt, hb, T):
    return lambda t: jnp.minimum((t + 1) * (tt // hb), T // hb - 1)


def _conf_fwd(u, w, b, g, beta, *, D, name):
    T = u.shape[0]
    K = w.shape[0]
    tt, hb, tc = _tile(T, CONV_ROWS), _halo_rows(K), _tile(D, GLU_TILE)
    sw = _strip_width(tc)
    per_tile = tc // sw
    prev = _prev_rows(tt, hb)

    def body(u_ref, uh_ref, w_ref, b_ref, g_ref, beta_ref, v1_ref, v3_ref):
        i = pl.program_id(0)

        def strip(sl, s):
            a0 = (s // per_tile) * (2 * tc) + (s % per_tile) * sw
            a_sl, g_sl = pl.ds(pl.multiple_of(a0, sw), sw), pl.ds(pl.multiple_of(a0 + tc, sw), sw)
            halo = jnp.where(i > 0, _f32(uh_ref[:, a_sl]) * _sigmoid(_f32(uh_ref[:, g_sl])), 0.0)
            ext = jnp.concatenate([halo, _f32(u_ref[:, a_sl]) * _sigmoid(_f32(u_ref[:, g_sl]))], axis=0)
            v1_ref[:, sl] = _taps(ext, w_ref, sl, K, hb - (K - 1), tt) + b_ref[:, sl]

        _for_strips(D, strip)
        xhat, _ = _ln_stats(v1_ref[...])
        v2 = xhat * g_ref[...] + beta_ref[...]
        v3_ref[...] = (v2 * _sigmoid(v2)).astype(BF16)

    row = pl.BlockSpec((tt, D), lambda t: (t, 0))
    vec = pl.BlockSpec((1, D), lambda t: (0, 0))
    return pl.pallas_call(
        body, name=name, out_shape=(_sds((T, D), F32), _sds((T, D), BF16)), grid=(T // tt,),
        in_specs=[pl.BlockSpec((tt, 2 * D), lambda t: (t, 0)), pl.BlockSpec((hb, 2 * D), lambda t: (prev(t), 0)),
                  pl.BlockSpec((K, D), lambda t: (0, 0)), vec, vec, vec],
        out_specs=(row, row), compiler_params=_cp("parallel"))(u, u, w, b, g, beta)


def _conv_act_fwd(x, off, w, b, *, ffn, name):
    T = x.shape[0]
    K, Cw = w.shape
    tt, hb = _tile(T, CONV_ROWS), _halo_rows(K)
    tc = _tile(Cw // 2, FFN_TILE) if ffn else _pick(math.gcd(Cw, off), 1536, LANES)
    xw = 2 * tc if ffn else tc
    assert off % xw == 0
    ob = off // xw
    prev = _prev_rows(tt, hb)

    def body(x_ref, xh_ref, w_ref, b_ref, o_ref):
        first = pl.program_id(0) == 0

        def pre_of(sl):
            ext = jnp.concatenate([jnp.where(first, 0.0, _f32(xh_ref[:, sl])), _f32(x_ref[:, sl])], axis=0)
            return _taps(ext, w_ref, sl, K, hb - (K - 1), tt) + b_ref[:, sl]

        def strip(sl, s):
            if ffn:
                gate = pre_of(sl)
                val = pre_of(pl.ds(pl.multiple_of(tc + s * sw, sw), sw))
                o_ref[:, sl] = (gate * _sigmoid(gate) * val).astype(BF16)
            else:
                pre = pre_of(sl)
                o_ref[:, sl] = pre * _sigmoid(pre)

        _for_strips(tc, strip)

    sw = _strip_width(tc)
    return pl.pallas_call(
        body, name=name, out_shape=_sds((T, Cw // 2), BF16) if ffn else _sds((T, Cw), F32), grid=(T // tt, Cw // xw),
        in_specs=[pl.BlockSpec((tt, xw), lambda t, c: (t, c + ob)), pl.BlockSpec((hb, xw), lambda t, c: (prev(t), c + ob)),
                  pl.BlockSpec((K, xw), lambda t, c: (0, c)), pl.BlockSpec((1, xw), lambda t, c: (0, c))],
        out_specs=pl.BlockSpec((tt, tc), lambda t, c: (t, c)),
        compiler_params=_cp("parallel", "parallel"))(x, x, w, b)


def _conv_act_bwd(x, off, w, b, dout, *, ffn, name):
    T = x.shape[0]
    K, Cw = w.shape
    tt, hb = _tile(T, CONV_ROWS), _halo_rows(K)
    tc = _tile(Cw // 2, FFN_TILE) if ffn else _pick(math.gcd(Cw, off), 1536, LANES)
    xw = 2 * tc if ffn else tc
    assert off % xw == 0
    ob = off // xw
    nt = T // tt
    prev, nxt = _prev_rows(tt, hb), _next_rows(tt, hb, T)

    sw = _strip_width(tc)

    def body(x_ref, xp_ref, xn_ref, d_ref, dn_ref, w_ref, b_ref, dx_ref, dw_ref, db_ref):
        t = pl.program_id(1)

        @pl.when(t == 0)
        def _():
            dw_ref[...] = jnp.zeros_like(dw_ref)
            db_ref[...] = jnp.zeros_like(db_ref)

        def ext_of(sl):
            return jnp.concatenate([jnp.where(t == 0, 0.0, _f32(xp_ref[:, sl])), _f32(x_ref[:, sl]), _f32(xn_ref[:, sl])], axis=0)

        def pre_of(ext, sl):
            return _taps(ext, w_ref, sl, K, hb - (K - 1), tt + hb) + b_ref[:, sl]

        def finish(ext, dpre, sl):
            dx_ref[:, sl] = _taps(dpre, w_ref, sl, K, 0, tt, reverse=True).astype(BF16)
            dp = dpre[0:tt]
            _tap_sums(dp, ext, dw_ref, sl, K, hb - (K - 1))
            db_ref[:, sl] += jnp.sum(dp, axis=0, keepdims=True)

        def strip(sl, s):
            d = jnp.concatenate([d_ref[:, sl], jnp.where(t == nt - 1, 0.0, dn_ref[:, sl])], axis=0)
            if ffn:
                vsl = pl.ds(pl.multiple_of(tc + s * sw, sw), sw)
                eg, ev = ext_of(sl), ext_of(vsl)
                sg, dsg = _silu_and_grad(pre_of(eg, sl))
                val = pre_of(ev, vsl)
                finish(eg, d * val * dsg, sl)
                finish(ev, d * sg, vsl)
            else:
                ext = ext_of(sl)
                finish(ext, d * _silu_and_grad(pre_of(ext, sl))[1], sl)

        _for_strips(tc, strip)

    return pl.pallas_call(
        body, name=name, out_shape=(_sds((T, Cw), BF16), _sds((K, Cw), F32), _sds((1, Cw), F32)),
        grid=(Cw // xw, nt),
        in_specs=[pl.BlockSpec((tt, xw), lambda c, t: (t, c + ob)), pl.BlockSpec((hb, xw), lambda c, t: (prev(t), c + ob)),
                  pl.BlockSpec((hb, xw), lambda c, t: (nxt(t), c + ob)),
                  pl.BlockSpec((tt, tc), lambda c, t: (t, c)), pl.BlockSpec((hb, tc), lambda c, t: (nxt(t), c)),
                  pl.BlockSpec((K, xw), lambda c, t: (0, c)), pl.BlockSpec((1, xw), lambda c, t: (0, c))],
        out_specs=(pl.BlockSpec((tt, xw), lambda c, t: (t, c)), pl.BlockSpec((K, xw), lambda c, t: (0, c)),
                   pl.BlockSpec((1, xw), lambda c, t: (0, c))),
        compiler_params=_cp("parallel", "arbitrary"))(x, x, x, dout, dout, w, b)


def _glu_conv_bwd(dpre, u, w, *, name):
    T, C = dpre.shape
    K = w.shape[0]
    tt, hb, tc = _tile(T, CONV_ROWS), _halo_rows(K), _tile(C, GLU_TILE)
    nt = T // tt
    prev, nxt = _prev_rows(tt, hb), _next_rows(tt, hb, T)

    sw = _strip_width(tc)

    def body(d_ref, dn_ref, x_ref, xh_ref, w_ref, dx_ref, dw_ref, db_ref):
        t = pl.program_id(1)

        @pl.when(t == 0)
        def _():
            dw_ref[...] = jnp.zeros_like(dw_ref)
            db_ref[...] = jnp.zeros_like(db_ref)

        def strip(sl, s):
            gsl = pl.ds(pl.multiple_of(tc + s * sw, sw), sw)
            d = d_ref[:, sl]
            dext = jnp.concatenate([d, jnp.where(t == nt - 1, 0.0, dn_ref[:, sl])], axis=0)
            dxin = _taps(dext, w_ref, sl, K, 0, tt, reverse=True)
            a, sg = _f32(x_ref[:, sl]), _sigmoid(_f32(x_ref[:, gsl]))
            v0 = a * sg
            dx_ref[:, sl] = (dxin * sg).astype(BF16)
            dx_ref[:, gsl] = (dxin * v0 * (1.0 - sg)).astype(BF16)
            halo = jnp.where(t == 0, 0.0, _f32(xh_ref[:, sl]) * _sigmoid(_f32(xh_ref[:, gsl])))
            _tap_sums(d, jnp.concatenate([halo, v0], axis=0), dw_ref, sl, K, hb - (K - 1))
            db_ref[:, sl] += jnp.sum(d, axis=0, keepdims=True)

        _for_strips(tc, strip)

    return pl.pallas_call(
        body, name=name, out_shape=(_sds((T, 2 * C), BF16), _sds((K, C), F32), _sds((1, C), F32)), grid=(C // tc, nt),
        in_specs=[pl.BlockSpec((tt, tc), lambda c, t: (t, c)), pl.BlockSpec((hb, tc), lambda c, t: (nxt(t), c)),
                  pl.BlockSpec((tt, 2 * tc), lambda c, t: (t, c)), pl.BlockSpec((hb, 2 * tc), lambda c, t: (prev(t), c)),
                  pl.BlockSpec((K, tc), lambda c, t: (0, c))],
        out_specs=(pl.BlockSpec((tt, 2 * tc), lambda c, t: (t, c)), pl.BlockSpec((K, tc), lambda c, t: (0, c)),
                   pl.BlockSpec((1, tc), lambda c, t: (0, c))),
        compiler_params=_cp("parallel", "arbitrary"))(dpre, dpre, u, u, w)


def _softplus(x):
    t = jnp.exp(-jnp.abs(x))
    u = 1.0 + t
    log1p = jnp.where(u == 1.0, t, jnp.log(u) * t / jnp.where(u == 1.0, 1.0, u - 1.0))
    return jnp.maximum(x, 0.0) + log1p


def _dt_fwd(udt, bias, *, H, name):
    T = udt.shape[0]
    tt = _tile(T, 1024)

    def body(u_ref, b_ref, o_ref):
        lane = lax.broadcasted_iota(jnp.int32, (tt, LANES), 1)
        o_ref[...] = jnp.where(lane < H, _softplus(u_ref[...] + b_ref[...]), 0.0)

    row = pl.BlockSpec((tt, LANES), lambda i: (i, 0))
    return pl.pallas_call(body, name=name, out_shape=_sds((T, LANES), F32), grid=(T // tt,),
                          in_specs=[row, pl.BlockSpec((1, LANES), lambda i: (0, 0))], out_specs=row,
                          compiler_params=_cp("parallel"))(udt, bias)


def _dt_bwd(da, ddtx, dt, udt, bias, a_col, *, H, name):
    T = da.shape[0]
    tt = _tile(T, 1024)

    def body(da_ref, dx_ref, dt_ref, u_ref, b_ref, a_ref, draw_ref, dbias_ref, dalog_ref):
        i = pl.program_id(0)
        lane = lax.broadcasted_iota(jnp.int32, (tt, LANES), 1)
        dav = da_ref[...]
        ddt = dav * a_ref[...] + dx_ref[...]
        draw = jnp.where(lane < H, ddt * _sigmoid(u_ref[...] + b_ref[...]), 0.0)
        draw_ref[...] = draw.astype(BF16)

        @pl.when(i == 0)
        def _():
            dbias_ref[...] = jnp.zeros_like(dbias_ref)
            dalog_ref[...] = jnp.zeros_like(dalog_ref)

        dbias_ref[...] += jnp.sum(draw, axis=0, keepdims=True)
        dalog_ref[...] += jnp.sum(dav * dt_ref[...], axis=0, keepdims=True) * a_ref[...]

    row = pl.BlockSpec((tt, LANES), lambda i: (i, 0))
    vec = pl.BlockSpec((1, LANES), lambda i: (0, 0))
    return pl.pallas_call(body, name=name,
                          out_shape=(_sds((T, LANES), BF16), _sds((1, LANES), F32), _sds((1, LANES), F32)),
                          grid=(T // tt,), in_specs=[row, row, row, row, vec, vec], out_specs=(row, vec, vec),
                          compiler_params=_cp("arbitrary"))(da, ddtx, dt, udt, bias, a_col)


def _ssd_consts():
    L = SSM_CHUNK
    r = lax.broadcasted_iota(jnp.int32, (L, L), 0)
    c = lax.broadcasted_iota(jnp.int32, (L, L), 1)
    return r, c


def _ssd_chunk_decays(dtc_ref, dtr_ref, acol_ref, arow_ref, cs_ref, csr_ref, r, c):
    L = SSM_CHUNK
    dtc = dtc_ref[...]
    tril = (r >= c).astype(F32)
    triu = (r <= c).astype(F32)
    cs_ref[...] = jnp.dot(tril, dtc * acol_ref[...], precision=HIGHEST, preferred_element_type=F32)
    csr_ref[...] = jnp.dot(dtr_ref[...] * arow_ref[...], triu, precision=HIGHEST, preferred_element_type=F32)
    cs = cs_ref[...]
    cs_last = cs_ref[L - 1:L, :]
    return dtc, cs, jnp.exp(cs), jnp.exp(cs_last - cs), jnp.exp(cs_last), triu


def _ssd_fwd(xbc, dt, dtT, a_col, a_row, d_col, *, H, name):
    T, XBC = xbc.shape
    L, N, G, P = SSM_CHUNK, SSM_STATE, SSM_GROUPS, SSM_HEAD_DIM
    SX = H * P
    HR = dtT.shape[0]
    nc = T // L
    heads_per_group = H // G

    def body(x_ref, dtc_ref, dtr_ref, acol_ref, arow_ref, d_ref, y_ref, hst_ref, state, cs_ref, csr_ref):
        ci = pl.program_id(0)

        @pl.when(ci == 0)
        def _():
            state[...] = jnp.zeros_like(state)

        hst_ref[0] = state[...]
        r, c = _ssd_consts()
        tri = r >= c
        lane_lo = c < P
        row_lo = r < P
        lane1_lo = lax.broadcasted_iota(jnp.int32, (1, LANES), 1) < P
        dtc, cs, e, ds, cd, _ = _ssd_chunk_decays(dtc_ref, dtr_ref, acol_ref, arow_ref, cs_ref, csr_ref, r, c)
        dsk = d_ref[...]

        def sel(arr, h0):
            return jnp.where(lane_lo, arr[:, h0:h0 + 1], arr[:, h0 + 1:h0 + 2])

        for g in range(G):
            Bg = x_ref[:, SX + g * N:SX + (g + 1) * N].astype(BF16)
            Cg = x_ref[:, SX + G * N + g * N:SX + G * N + (g + 1) * N].astype(BF16)
            Gm = _dot(Cg, Bg, _NT)
            st = []
            for j in range(g * heads_per_group // 2, (g + 1) * heads_per_group // 2):
                h0 = 2 * j
                sl = slice(2 * P * j, 2 * P * (j + 1))
                x2 = x_ref[:, sl]
                X2 = x2 * sel(dtc, h0)
                H2 = state[sl, :]
                st.append(dict(h0=h0, sl=sl, x2=x2, X2=X2, H2=H2, R2=_dot(Cg, H2.astype(BF16), _NT),
                               S2=_dot((X2 * sel(ds, h0)).astype(BF16), Bg, _TN)))
            for s in st:
                ms = []
                for h in (s['h0'], s['h0'] + 1):
                    seg = cs[:, h:h + 1] - csr_ref[h:h + 1, :]
                    ms.append((Gm * jnp.where(tri, jnp.exp(jnp.where(tri, seg, 0.0)), 0.0)).astype(BF16))
                xst = jnp.concatenate([jnp.where(lane_lo, s['X2'], 0.0), jnp.where(lane_lo, 0.0, s['X2'])], axis=0).astype(BF16)
                s['yd'] = _dot(jnp.concatenate(ms, axis=1), xst, _NN)
            for s in st:
                h0, sl = s['h0'], s['sl']
                dsk2 = jnp.where(lane1_lo, dsk[:, h0:h0 + 1], dsk[:, h0 + 1:h0 + 2])
                y_ref[:, sl] = s['yd'] + sel(e, h0) * s['R2'] + s['x2'] * dsk2
                state[sl, :] = jnp.where(row_lo, cd[:, h0:h0 + 1], cd[:, h0 + 1:h0 + 2]) * s['H2'] + s['S2']

    vec = pl.BlockSpec((1, LANES), lambda i: (0, 0))
    return pl.pallas_call(
        body, name=name, out_shape=(_sds((T, SX), F32), _sds((nc, SX, N), F32)), grid=(nc,),
        in_specs=[pl.BlockSpec((L, XBC), lambda i: (i, 0)), pl.BlockSpec((L, LANES), lambda i: (i, 0)),
                  pl.BlockSpec((HR, L), lambda i: (0, i)), vec, pl.BlockSpec((HR, 1), lambda i: (0, 0)), vec],
        out_specs=(pl.BlockSpec((L, SX), lambda i: (i, 0)), pl.BlockSpec((1, SX, N), lambda i: (i, 0, 0))),
        scratch_shapes=[pltpu.VMEM((SX, N), F32), pltpu.VMEM((L, LANES), F32), pltpu.VMEM((HR, L), F32)],
        compiler_params=_cp("arbitrary"))(xbc, dt, dtT, a_col, a_row, d_col)


def _ssd_bwd(xbc, dt, dtT, a_col, a_row, d_col, hst, dy, *, H, name):
    T, XBC = xbc.shape
    L, N, G, P = SSM_CHUNK, SSM_STATE, SSM_GROUPS, SSM_HEAD_DIM
    SX = H * P
    HR = dtT.shape[0]
    nc = T // L
    heads_per_group = H // G

    def body(x_ref, dtc_ref, dtr_ref, acol_ref, arow_ref, d_ref, hst_ref, dy_ref,
             dx_ref, da_ref, ddtx_ref, dd_ref, dstate, cs_ref, csr_ref):
        ci = pl.program_id(0)

        @pl.when(ci == 0)
        def _():
            dstate[...] = jnp.zeros_like(dstate)
            dd_ref[...] = jnp.zeros_like(dd_ref)

        r, c = _ssd_consts()
        tri = r >= c
        lane_lo = c < P
        row_lo = r < P
        lane1 = lax.broadcasted_iota(jnp.int32, (1, LANES), 1)
        rowc = lax.broadcasted_iota(jnp.int32, (L, 1), 0)
        dtc, cs, e, ds, cd, triu = _ssd_chunk_decays(dtc_ref, dtr_ref, acol_ref, arow_ref, cs_ref, csr_ref, r, c)
        triu_b = triu.astype(BF16)
        dsk = d_ref[...]
        last_row = rowc == L - 1

        def sel(arr, h0):
            return jnp.where(lane_lo, arr[:, h0:h0 + 1], arr[:, h0 + 1:h0 + 2])

        triT = r <= c

        def halves(v, axis):
            return jnp.concatenate([jnp.where(lane_lo, v, 0.0), jnp.where(lane_lo, 0.0, v)], axis=axis)

        def head_sum(v, h):
            lo = jnp.sum(jnp.where(lane1 < P, v, 0.0), axis=1, keepdims=True) * (lane1 == h).astype(F32)
            hi = jnp.sum(jnp.where(lane1 < P, 0.0, v), axis=1, keepdims=True) * (lane1 == h + 1).astype(F32)
            return lo + hi

        GW = heads_per_group * P
        gl = lax.broadcasted_iota(jnp.int32, (GW, LANES), 0)
        gc = lax.broadcasted_iota(jnp.int32, (GW, LANES), 1)
        wl = lax.broadcasted_iota(jnp.int32, (heads_per_group * L, LANES), 0)
        wc = lax.broadcasted_iota(jnp.int32, (heads_per_group * L, LANES), 1)
        wide_r = lax.broadcasted_iota(jnp.int32, (L, heads_per_group * L), 0)
        wide_c = lax.broadcasted_iota(jnp.int32, (L, heads_per_group * L), 1)
        below_diag = (wide_c % L) < wide_r
        sums = jnp.zeros((3 * L, LANES), F32)
        da_q = jnp.zeros((L, LANES), F32)
        dcd_acc = jnp.zeros((1, LANES), F32)
        dd_acc = jnp.zeros((1, LANES), F32)
        for g in range(G):
            bsl = slice(SX + g * N, SX + (g + 1) * N)
            csl = slice(SX + G * N + g * N, SX + G * N + (g + 1) * N)
            Bg = x_ref[:, bsl].astype(BF16)
            Cg = x_ref[:, csl].astype(BF16)
            Gm = _dot(Cg, Bg, _NT)
            GmT = _dot(Bg, Cg, _NT)
            st = []
            for j in range(g * heads_per_group // 2, (g + 1) * heads_per_group // 2):
                h0 = 2 * j
                sl = slice(2 * P * j, 2 * P * (j + 1))
                s = dict(h0=h0, sl=sl, x2=x_ref[:, sl], dy2=dy_ref[:, sl], H2=hst_ref[0, sl, :], dHn=dstate[sl, :])
                s['dt2'], s['e2'], s['ds2'] = sel(dtc, h0), sel(e, h0), sel(ds, h0)
                s['X2'] = s['x2'] * s['dt2']
                s['H2b'], s['dHnb'] = s['H2'].astype(BF16), s['dHn'].astype(BF16)
                st.append(s)
            for s in st:
                s['R2'] = _dot(Cg, s['H2b'], _NT)
                s['dXd'] = _dot(Bg, s['dHnb'], _NT)
                s['dM2'] = _dot(s['dy2'].astype(BF16), halves(s['X2'], 0).astype(BF16), _NT)
            dG = jnp.zeros((L, L), F32)
            qs = []
            for s in st:
                mts = []
                for i, h in enumerate((s['h0'], s['h0'] + 1)):
                    z = cs[:, h:h + 1] - csr_ref[h:h + 1, :]
                    Dm = jnp.where(tri, jnp.exp(jnp.where(tri, z, 0.0)), 0.0)
                    DmT = jnp.where(triT, jnp.exp(jnp.where(triT, -z, 0.0)), 0.0)
                    dM = s['dM2'][:, i * L:(i + 1) * L]
                    dG = dG + dM * Dm
                    qs.append((dM * (Gm * Dm)).astype(BF16))
                    mts.append((GmT * DmT).astype(BF16))
                s['dXm'] = _dot(jnp.concatenate(mts, axis=1), halves(s['dy2'], 0).astype(BF16), _NN)
            Wg = _dot(triu_b, jnp.concatenate(qs, axis=1), _NN)
            place = (wc == g * heads_per_group + wl // L).astype(BF16)
            da_q = da_q + _dot(jnp.where(below_diag, Wg, 0.0).astype(BF16), place, _NN)
            dBg = jnp.zeros((L, N), F32)
            dCg = jnp.zeros((L, N), F32)
            des, ddss, dxxs = [], [], []
            for s in st:
                h0, sl, x2, dy2, X2 = s['h0'], s['sl'], s['x2'], s['dy2'], s['X2']
                dR2b = (s['e2'] * dy2).astype(BF16)
                dCg = dCg + _dot(dR2b, s['H2b'], _NN)
                dHr = _dot(dR2b, Cg, _TN)
                dBg = dBg + _dot((X2 * s['ds2']).astype(BF16), s['dHnb'], _NN)
                dX2 = s['ds2'] * s['dXd'] + s['dXm']
                des.append(dy2 * s['R2'])
                ddss.append(s['dXd'] * X2)
                dxxs.append(dX2 * x2)
                prod = s['dHn'] * s['H2']
                for i, h in enumerate((h0, h0 + 1)):
                    rows = jnp.sum(prod[i * P:(i + 1) * P], axis=0, keepdims=True)
                    dcd_acc = dcd_acc + jnp.sum(rows, axis=1, keepdims=True) * (lane1 == h).astype(F32)
                dd_acc = dd_acc + head_sum(jnp.sum(dy2 * x2, axis=0, keepdims=True), h0)
                dsk2 = jnp.where(lane1 < P, dsk[:, h0:h0 + 1], dsk[:, h0 + 1:h0 + 2])
                dx_ref[:, sl] = dX2 * s['dt2'] + dy2 * dsk2
                dstate[sl, :] = jnp.where(row_lo, cd[:, h0:h0 + 1], cd[:, h0 + 1:h0 + 2]) * s['dHn'] + dHr
            stack = jnp.concatenate([jnp.concatenate(v, axis=1) for v in (des, ddss, dxxs)], axis=0).astype(BF16)
            sums = sums + _dot(stack, (gc == g * heads_per_group + gl // P).astype(BF16), _NN)
            dGb = dG.astype(BF16)
            dx_ref[:, bsl] = dBg + _dot(dGb, Cg, _TN)
            dx_ref[:, csl] = dCg + _dot(dGb, Bg, _NN)
        t1 = sums[L:2 * L] * ds
        tail = jnp.sum(t1, axis=0, keepdims=True) + dcd_acc * cd
        dcs = sums[0:L] * e - t1 + jnp.where(last_row, tail, 0.0)
        da_ref[...] = jnp.dot(triu, dcs, precision=HIGHEST, preferred_element_type=F32) + da_q
        ddtx_ref[...] = sums[2 * L:3 * L]
        dd_ref[...] += dd_acc

    vec = pl.BlockSpec((1, LANES), lambda i: (0, 0))
    rev = lambda i: (nc - 1 - i, 0)
    return pl.pallas_call(
        body, name=name,
        out_shape=(_sds((T, XBC), F32), _sds((T, LANES), F32), _sds((T, LANES), F32), _sds((1, LANES), F32)),
        grid=(nc,),
        in_specs=[pl.BlockSpec((L, XBC), rev), pl.BlockSpec((L, LANES), rev),
                  pl.BlockSpec((HR, L), lambda i: (0, nc - 1 - i)), vec, pl.BlockSpec((HR, 1), lambda i: (0, 0)), vec,
                  pl.BlockSpec((1, SX, N), lambda i: (nc - 1 - i, 0, 0)), pl.BlockSpec((L, SX), rev)],
        out_specs=(pl.BlockSpec((L, XBC), rev), pl.BlockSpec((L, LANES), rev), pl.BlockSpec((L, LANES), rev), vec),
        scratch_shapes=[pltpu.VMEM((SX, N), F32), pltpu.VMEM((L, LANES), F32), pltpu.VMEM((HR, L), F32)],
        compiler_params=_cp("arbitrary"))(xbc, dt, dtT, a_col, a_row, d_col, hst, dy)


def _rms_fwd(y, u, zoff, w, *, name):
    T, SX = y.shape
    tt = _tile(T, 256)
    gs = SX // SSM_GROUPS
    zb = zoff // SX

    def body(y_ref, z_ref, w_ref, o_ref):
        for g in range(SSM_GROUPS):
            sl = slice(g * gs, (g + 1) * gs)
            z = _f32(z_ref[:, sl])
            yg = y_ref[:, sl] * (z * _sigmoid(z))
            rstd = lax.rsqrt(jnp.mean(yg * yg, axis=-1, keepdims=True) + RMS_EPS)
            o_ref[:, sl] = (yg * rstd * w_ref[:, sl]).astype(BF16)

    row = pl.BlockSpec((tt, SX), lambda i: (i, 0))
    return pl.pallas_call(body, name=name, out_shape=_sds((T, SX), BF16), grid=(T // tt,),
                          in_specs=[row, pl.BlockSpec((tt, SX), lambda i: (i, zb)), pl.BlockSpec((1, SX), lambda i: (0, 0))],
                          out_specs=row, compiler_params=_cp("parallel"))(y, u, w)


def _rms_bwd(y, u, zoff, w, dyn, *, name):
    T, SX = y.shape
    tt = _tile(T, 256)
    gs = SX // SSM_GROUPS
    zb = zoff // SX

    def body(y_ref, z_ref, w_ref, d_ref, dy_ref, dz_ref, dw_ref):
        i = pl.program_id(0)

        @pl.when(i == 0)
        def _():
            dw_ref[...] = jnp.zeros_like(dw_ref)

        for g in range(SSM_GROUPS):
            sl = slice(g * gs, (g + 1) * gs)
            z, yv, d = _f32(z_ref[:, sl]), y_ref[:, sl], d_ref[:, sl]
            sz, dsz = _silu_and_grad(z)
            yg = yv * sz
            rstd = lax.rsqrt(jnp.mean(yg * yg, axis=-1, keepdims=True) + RMS_EPS)
            t = yg * rstd
            dw_ref[:, sl] += jnp.sum(d * t, axis=0, keepdims=True)
            dt_ = d * w_ref[:, sl]
            dyg = rstd * (dt_ - t * jnp.mean(dt_ * t, axis=-1, keepdims=True))
            dy_ref[:, sl] = dyg * sz
            dz_ref[:, sl] = (dyg * yv * dsz).astype(BF16)

    row = pl.BlockSpec((tt, SX), lambda i: (i, 0))
    vec = pl.BlockSpec((1, SX), lambda i: (0, 0))
    return pl.pallas_call(body, name=name, out_shape=(_sds((T, SX), F32), _sds((T, SX), BF16), _sds((1, SX), F32)),
                          grid=(T // tt,), in_specs=[row, pl.BlockSpec((tt, SX), lambda i: (i, zb)), vec, row],
                          out_specs=(row, row, vec), compiler_params=_cp("arbitrary"))(y, u, w, dyn)


def _gate_fwd(u, goff, ya, yb, *, name):
    T, D = ya.shape
    tt = _tile(T, 512)
    gb = goff // D

    def body(ga_ref, gb_ref, ya_ref, yb_ref, o_ref):
        o_ref[...] = (_sigmoid(_f32(ga_ref[...])) * ya_ref[...] + _sigmoid(_f32(gb_ref[...])) * yb_ref[...]).astype(BF16)

    row = pl.BlockSpec((tt, D), lambda i: (i, 0))
    return pl.pallas_call(body, name=name, out_shape=_sds((T, D), BF16), grid=(T // tt,),
                          in_specs=[pl.BlockSpec((tt, D), lambda i: (i, gb)), pl.BlockSpec((tt, D), lambda i: (i, gb + 1)), row, row],
                          out_specs=row, compiler_params=_cp("parallel"))(u, u, ya, yb)


def _gate_bwd(u, goff, ya, yb, dm, *, name):
    T, D = ya.shape
    tt = _tile(T, 512)
    gb = goff // D

    def body(ga_ref, gb_ref, ya_ref, yb_ref, dm_ref, dya_ref, dyb_ref, dg_ref):
        d = dm_ref[...]
        sa, sb = _sigmoid(_f32(ga_ref[...])), _sigmoid(_f32(gb_ref[...]))
        dya_ref[...] = (d * sa).astype(BF16)
        dyb_ref[...] = (d * sb).astype(BF16)
        dg_ref[...] = jnp.concatenate([d * ya_ref[...] * sa * (1.0 - sa), d * yb_ref[...] * sb * (1.0 - sb)], axis=1).astype(BF16)

    row = pl.BlockSpec((tt, D), lambda i: (i, 0))
    return pl.pallas_call(body, name=name, out_shape=(_sds((T, D), BF16), _sds((T, D), BF16), _sds((T, 2 * D), BF16)),
                          grid=(T // tt,),
                          in_specs=[pl.BlockSpec((tt, D), lambda i: (i, gb)), pl.BlockSpec((tt, D), lambda i: (i, gb + 1)), row, row, row],
                          out_specs=(row, row, pl.BlockSpec((tt, 2 * D), lambda i: (i, 0))),
                          compiler_params=_cp("parallel"))(u, u, ya, yb, dm)


def _adamw_math(w, gg, m, v):
    nm = ADAM_B1 * m + (1.0 - ADAM_B1) * gg
    nv = ADAM_B2 * v + (1.0 - ADAM_B2) * (gg * gg)
    m_hat = nm / (1.0 - ADAM_B1 ** ADAM_STEP)
    v_hat = nv / (1.0 - ADAM_B2 ** ADAM_STEP)
    return -ADAM_LR * (m_hat / (jnp.sqrt(v_hat) + ADAM_EPS) + ADAM_WD * w), nm, nv


def _adamw(w, g, m, v, *, name):
    R, C = w.shape
    tr = _pick(R, 256, 8)

    def body(w_ref, g_ref, m_ref, v_ref, d_ref, nm_ref, nv_ref):
        d_ref[...], nm_ref[...], nv_ref[...] = _adamw_math(w_ref[...], g_ref[...], m_ref[...], v_ref[...])

    blk = pl.BlockSpec((tr, C), lambda i: (i, 0))
    out = _sds((R, C), F32)
    return pl.pallas_call(body, name=name, out_shape=(out, out, out), grid=(R // tr,), in_specs=[blk] * 4,
                          out_specs=(blk,) * 3, compiler_params=_cp("parallel"))(w, g, m, v)


def _sum_adamw(parts, w, m, v, l, *, name):
    shape = w.shape[1:]
    C = shape[-1]
    R = w[0].size // C
    tr = _pick(R, 256, 8)
    if tr % 8:
        w, m, v, l = w[l:l + 1], m[l:l + 1], v[l:l + 1], 0
    lb = l * (R // tr)

    def body(p_ref, w_ref, m_ref, v_ref, g_ref, d_ref, nm_ref, nv_ref):
        gg = p_ref[0].astype(F32)
        for k in range(1, N_DEV):
            gg = gg + p_ref[k].astype(F32)
        g_ref[...] = gg
        d_ref[...], nm_ref[...], nv_ref[...] = _adamw_math(w_ref[...], gg, m_ref[...], v_ref[...])

    blk = pl.BlockSpec((tr, C), lambda i: (i, 0))
    lblk = pl.BlockSpec((tr, C), lambda i: (i + lb, 0))
    out = _sds((R, C), F32)
    stacked = (w.shape[0] * R, C)
    res = pl.pallas_call(body, name=name, out_shape=(out,) * 4, grid=(R // tr,),
                         in_specs=[pl.BlockSpec((N_DEV, tr, C), lambda i: (0, i, 0)), lblk, lblk, lblk],
                         out_specs=(blk,) * 4, compiler_params=_cp("parallel"))(
        parts.reshape(N_DEV, R, C), w.reshape(stacked), m.reshape(stacked), v.reshape(stacked))
    return tuple(r.reshape(shape) for r in res)


def _sum_slots(x, *, name):
    n, R, C = x.shape
    tr = _tile(R, 512)

    def body(x_ref, o_ref):
        acc = x_ref[0].astype(F32)
        for k in range(1, n):
            acc = acc + x_ref[k].astype(F32)
        o_ref[...] = acc

    return pl.pallas_call(body, name=name, out_shape=_sds((R, C), F32), grid=(R // tr,),
                          in_specs=[pl.BlockSpec((n, tr, C), lambda i: (0, i, 0))],
                          out_specs=pl.BlockSpec((tr, C), lambda i: (i, 0)), compiler_params=_cp("parallel"))(x)


def _exchange(xs, *, scatter, name):
    n = len(xs)

    def body(*refs):
        x_refs, o_refs = refs[:n], refs[n:2 * n]
        send_sems, recv_sems, local_sems = refs[2 * n:]
        mx, my, mc = lax.axis_index("x"), lax.axis_index("y"), lax.axis_index("c")
        me = 4 * mx + 2 * my + mc

        def src(i, d):
            return x_refs[i].at[d] if scatter else x_refs[i]

        locals_ = [pltpu.make_async_copy(src(i, me), o_refs[i].at[me], local_sems.at[i]) for i in range(n)]
        for cp in locals_:
            cp.start()
        sends, recvs = [], []
        for k in range(1, N_DEV):
            px = 1 - mx if k & 4 else mx
            py = 1 - my if k & 2 else my
            pc = 1 - mc if k & 1 else mc
            peer = 4 * px + 2 * py + pc
            for i in range(n):
                common = dict(send_sem=send_sems.at[k - 1, i], recv_sem=recv_sems.at[k - 1, i],
                              device_id=(px, py, pc), device_id_type=pl.DeviceIdType.MESH)
                sends.append(pltpu.make_async_remote_copy(src_ref=src(i, peer), dst_ref=o_refs[i].at[me], **common))
                recvs.append(pltpu.make_async_remote_copy(src_ref=src(i, peer), dst_ref=o_refs[i].at[peer], **common))
        for cp in sends:
            cp.start()
        for cp in recvs:
            cp.wait_recv()
        for cp in sends:
            cp.wait_send()
        for cp in locals_:
            cp.wait()

    any_spec = pl.BlockSpec(memory_space=pl.ANY)
    out_shape = tuple(_sds(x.shape if scatter else (N_DEV,) + x.shape, x.dtype) for x in xs)
    return pl.pallas_call(
        body, name=name, out_shape=out_shape, in_specs=[any_spec] * n, out_specs=(any_spec,) * n,
        scratch_shapes=[pltpu.SemaphoreType.DMA((N_DEV - 1, n)), pltpu.SemaphoreType.DMA((N_DEV - 1, n)),
                        pltpu.SemaphoreType.DMA((n,))])(*xs)


def _to_rows(flat, row_mult):
    n = flat.shape[-1]
    per = row_mult * LANES
    pad = (-n) % per
    flat = jnp.pad(flat, [(0, 0)] * (flat.ndim - 1) + [(0, pad)])
    return flat.reshape(flat.shape[:-1] + ((n + pad) // LANES, LANES))


def _peers():
    mx, my, mc = lax.axis_index("x"), lax.axis_index("y"), lax.axis_index("c")
    out = []
    for k in range(1, N_DEV):
        px = 1 - mx if k & 4 else mx
        py = 1 - my if k & 2 else my
        pc = 1 - mc if k & 1 else mc
        out.append(((px, py, pc), 4 * px + 2 * py + pc))
    return 4 * mx + 2 * my + mc, out


_HBM = pl.BlockSpec(memory_space=pltpu.HBM)
_SEM = pl.BlockSpec(memory_space=pltpu.SEMAPHORE)


def _exchange_start(xs, *, scatter, name):
    n = len(xs)

    def body(*refs):
        x_refs, land_refs, send_sems, recv_sems, token = refs[:n], refs[n:2 * n], refs[2 * n], refs[2 * n + 1], refs[-1]
        me, peers = _peers()
        for k, (dev, peer) in enumerate(peers):
            for i in range(n):
                pltpu.make_async_remote_copy(
                    src_ref=x_refs[i].at[peer] if scatter else x_refs[i], dst_ref=land_refs[i].at[me],
                    send_sem=send_sems.at[k * n + i], recv_sem=recv_sems.at[k * n + i],
                    device_id=dev, device_id_type=pl.DeviceIdType.MESH).start()
        token[...] = jnp.zeros_like(token)

    land_shapes = [x.shape if scatter else (N_DEV,) + x.shape for x in xs]
    lands = [pltpu.with_memory_space_constraint(lax.empty(s, x.dtype), pltpu.HBM) for s, x in zip(land_shapes, xs)]
    srcs = [pltpu.with_memory_space_constraint(x, pltpu.HBM) for x in xs]
    out = pl.pallas_call(
        body, name=name,
        out_shape=(pltpu.SemaphoreType.DMA(((N_DEV - 1) * n,)), pltpu.SemaphoreType.DMA(((N_DEV - 1) * n,)),
                   *[pltpu.HBM(x.shape, x.dtype) for x in xs], *[pltpu.HBM(s, x.dtype) for s, x in zip(land_shapes, xs)],
                   _sds((8, LANES), F32)),
        in_specs=(_HBM,) * (2 * n), out_specs=(_SEM, _SEM) + (_HBM,) * (2 * n) + (pl.BlockSpec(memory_space=pltpu.VMEM),),
        input_output_aliases={i: 2 + i for i in range(2 * n)},
        compiler_params=pltpu.CompilerParams(has_side_effects=pltpu.SideEffectType.DATAFLOW_SIDE_EFFECTING))(*srcs, *lands)
    return (out[0], out[1], list(out[2:2 + n]), list(out[2 + n:2 + 2 * n])), out[-1]


def _exchange_wait(handle, after, *, scatter, name):
    send_sems, recv_sems, srcs, lands = handle
    n = len(srcs)

    def body(*refs):
        x_refs, land_refs, send_sems, recv_sems = refs[:n], refs[n:2 * n], refs[2 * n], refs[2 * n + 1]
        me, peers = _peers()
        for k, (dev, peer) in enumerate(peers):
            for i in range(n):
                cp = pltpu.make_async_remote_copy(
                    src_ref=x_refs[i].at[peer] if scatter else x_refs[i], dst_ref=land_refs[i].at[peer],
                    send_sem=send_sems.at[k * n + i], recv_sem=recv_sems.at[k * n + i],
                    device_id=dev, device_id_type=pl.DeviceIdType.MESH)
                cp.wait_send()
                cp.wait_recv()

    out = pl.pallas_call(
        body, name=name, out_shape=tuple(pltpu.HBM(a.shape, a.dtype) for a in srcs + lands),
        in_specs=(_HBM,) * (2 * n) + (_SEM, _SEM, pl.BlockSpec(memory_space=pl.ANY)), out_specs=(_HBM,) * (2 * n),
        input_output_aliases={i: i for i in range(2 * n)},
        compiler_params=pltpu.CompilerParams(has_side_effects=pltpu.SideEffectType.DATAFLOW_SIDE_EFFECTING))(
        *srcs, *lands, send_sems, recv_sems, after)
    return list(out[n:])


def _with_own_slot(lands, xs, *, scatter):
    me = 4 * lax.axis_index("x") + 2 * lax.axis_index("y") + lax.axis_index("c")
    out = []
    for land, x in zip(lands, xs):
        own = lax.dynamic_index_in_dim(x, me, 0, keepdims=True) if scatter else x[None]
        out.append(lax.dynamic_update_index_in_dim(land, own, me, 0))
    return out


def _wire_shards(p, l, names):
    return [p[k][l].astype(WIRE) if SHARDED[k][1] else p[k][l] for k in names]


def _full_weights(got, p, names):
    full = {}
    for k, g in zip(names, got):
        axis, shape = SHARDED[k][0] - 1, p[k].shape[1:]
        g = jnp.moveaxis(g, 0, axis)
        full[k] = g.reshape(shape[:axis] + (N_DEV * shape[axis],) + shape[axis + 1:])
    return full


def _grad_parts(grads, p, names):
    parts = []
    for k in names:
        axis, g = SHARDED[k][0] - 1, grads[k]
        g = g.reshape(g.shape[:axis] + (N_DEV, p[k].shape[1:][axis]) + g.shape[axis + 1:])
        parts.append(jnp.moveaxis(g, axis, 0).astype(WIRE))
    return parts


def _allreduce_small(vals):
    metas = [(k, v.shape, v.size) for k, v in vals.items()]
    flat = jnp.concatenate([v.reshape(-1) for v in vals.values()])
    got, = _exchange([_to_rows(flat, 8)], scatter=False, name="gather_small_grads")
    summed = _sum_slots(got, name="sum_small_grads").reshape(-1)
    out, off = {}, 0
    for k, shape, n in metas:
        out[k] = summed[off:off + n].reshape(shape)
        off += n
    return out


def _row(v):
    return v.reshape(1, -1).astype(F32)


def _lanes(v):
    return jnp.pad(v.astype(F32), (0, LANES - v.shape[0])).reshape(1, LANES)


EARLY = ('w_in', 'conv_dw_w', 'ssm_conv_w')
LATE = tuple(k for k in SHARDED if k not in EARLY)
FFN = ('w_ffn_up', 'ffn_dw_w', 'w_ffn_down')
NOT_FFN = tuple(k for k in SHARDED if k not in FFN)


def _layer_weights(p, full, l, dims):
    D, SX, XBC, H, FF = dims
    assert _tile(D, GLU_TILE) == D and _tile(FF, FFN_TILE) == FF
    W = {}
    if 'w_in' in full:
        w_in = full['w_in']
        o_dt = 2 * D + SX + XBC
        w_pieces = [w_in[:, :2 * D], w_in[:, 2 * D:2 * D + SX], w_in[:, 2 * D + SX:o_dt], w_in[:, o_dt + H:]]
        a_head = -jnp.exp(p['ssm_a_log'][l].astype(F32))
        hr = -(-H // 8) * 8
        W.update(
            w_main=jnp.concatenate(w_pieces, axis=1), w_pieces=w_pieces,
            w_dt=jnp.pad(w_in[:, o_dt:o_dt + H], ((0, 0), (0, LANES - H))),
            conv_w=full['conv_dw_w'], conv_b=_row(p['conv_dw_b'][l]), conv_g=_row(p['conv_ln_g'][l]),
            conv_beta=_row(p['conv_ln_b'][l]), ssm_w=full['ssm_conv_w'], ssm_b=_row(p['ssm_conv_b'][l]),
            dt_bias=_lanes(p['ssm_dt_bias'][l]), a_col=_lanes(a_head),
            a_row=jnp.pad(a_head, (0, hr - H)).reshape(hr, 1), d_col=_lanes(p['ssm_d'][l]),
            norm_w=_row(p['ssm_norm_w'][l]), ln1_g=_row(p['ln1_g'][l]), ln1_b=_row(p['ln1_b'][l]),
            ffn_b=_row(p['ffn_dw_b'][l]), ln2_g=_row(p['ln2_g'][l]), ln2_b=_row(p['ln2_b'][l]))
    if 'w_o' in full:
        W.update(w_conv_out=full['w_conv_out'], w_ssm_out=full['w_ssm_out'], w_o=full['w_o'], w_up=full['w_ffn_up'],
                 ffn_w=full['ffn_dw_w'], w_down=full['w_ffn_down'])
    return W


def _layer_fwd(h, hb, W, late, l, dims):
    D, SX, XBC, H, FF = dims
    o_z, o_xbc, o_gate = 2 * D, 2 * D + SX, 2 * D + SX + XBC
    hr = W['a_row'].shape[0]
    t = f"l{l}"
    u = _mm(hb, W['w_main'], mode='nn', out_dtype=BF16, name=t + "_in_proj")
    udt = _mm(hb, W['w_dt'], mode='nn', out_dtype=F32, name=t + "_dt_proj")
    v1, v3 = _conf_fwd(u, W['conv_w'], W['conv_b'], W['conv_g'], W['conv_beta'], D=D, name=t + "_conf_fwd")
    xbc = _conv_act_fwd(u, o_xbc, W['ssm_w'], W['ssm_b'], ffn=False, name=t + "_ssm_conv")
    dt = _dt_fwd(udt, W['dt_bias'], H=H, name=t + "_dt")
    dtT = jnp.pad(dt[:, :H].T, ((0, hr - H), (0, 0)))
    y, hst = _ssd_fwd(xbc, dt, dtT, W['a_col'], W['a_row'], W['d_col'], H=H, name=t + "_ssd_fwd")
    yn = _rms_fwd(y, u, o_z, W['norm_w'], name=t + "_rms_fwd")
    W = {**W, **late(yn)}
    ya = _mm(v3, W['w_conv_out'], mode='nn', out_dtype=F32, name=t + "_conv_out")
    yb = _mm(yn, W['w_ssm_out'], mode='nn', out_dtype=F32, name=t + "_ssm_out")
    m = _gate_fwd(u, o_gate, ya, yb, name=t + "_gate_fwd")
    h1, s1, h1b = _mm_res_ln(m, W['w_o'], h, W['ln1_g'], W['ln1_b'], name=t + "_mix_ln1")
    uf = _mm(h1b, W['w_up'], mode='nn', out_dtype=BF16, name=t + "_ffn_up")
    act = _conv_act_fwd(uf, 0, W['ffn_w'], W['ffn_b'], ffn=True, name=t + "_ffn_conv")
    h2, s2, h2b = _mm_res_ln(act, W['w_down'], h1, W['ln2_g'], W['ln2_b'], name=t + "_ffn_down_ln2")
    saved = dict(hb=hb, u=u, udt=udt, v1=v1, v3=v3, ya=ya, xbc=xbc, dt=dt, dtT=dtT, y=y, hst=hst, yn=yn, yb=yb, m=m,
                 h1b=h1b, s1=s1, uf=uf, act=act, s2=s2)
    return h2, h2b, saved, W


def _layer_bwd(dh2, W, S, l, dims, on_ffn_grads=None, on_all_grads=None):
    D, SX, XBC, H, FF = dims
    o_z, o_xbc, o_gate = 2 * D, 2 * D + SX, 2 * D + SX + XBC
    t = f"l{l}"
    g = {}
    ds2, g['ln2_g'], g['ln2_b'] = _ln_bwd(S['s2'], dh2, W['ln2_g'], W['ln2_b'], silu=False, name=t + "_ln2_bwd")
    g['w_down'] = _mm(S['act'], ds2, mode='tn', out_dtype=F32, name=t + "_dw_down")
    dact = _mm(ds2, W['w_down'], mode='nt', out_dtype=F32, name=t + "_dact")
    duf, g['ffn_w'], g['ffn_b'] = _conv_act_bwd(S['uf'], 0, W['ffn_w'], W['ffn_b'], dact, ffn=True, name=t + "_ffn_conv_bwd")
    g['w_up'] = _mm(S['h1b'], duf, mode='tn', out_dtype=F32, name=t + "_dw_up")
    dh1 = _mm(duf, W['w_up'], mode='nt', out_dtype=F32, res=ds2, res_scale=DN_ALPHA, name=t + "_dh1")
    ln1_g = W['ln1_g'] if on_ffn_grads is None else W['ln1_g'] + on_ffn_grads(g)[0:1, 0:1]
    ds1, g['ln1_g'], g['ln1_b'] = _ln_bwd(S['s1'], dh1, ln1_g, W['ln1_b'], silu=False, name=t + "_ln1_bwd")
    g['w_o'] = _mm(S['m'], ds1, mode='tn', out_dtype=F32, name=t + "_dw_o")
    dm = _mm(ds1, W['w_o'], mode='nt', out_dtype=F32, name=t + "_dm")
    dya, dyb, dgate = _gate_bwd(S['u'], o_gate, S['ya'], S['yb'], dm, name=t + "_gate_bwd")
    g['w_conv_out'] = _mm(S['v3'], dya, mode='tn', out_dtype=F32, name=t + "_dw_conv_out")
    dv3 = _mm(dya, W['w_conv_out'], mode='nt', out_dtype=F32, name=t + "_dv3")
    dv1, g['conv_g'], g['conv_beta'] = _ln_bwd(S['v1'], dv3, W['conv_g'], W['conv_beta'], silu=True, name=t + "_conv_ln_bwd")
    dglu, g['conv_w'], g['conv_b'] = _glu_conv_bwd(dv1, S['u'], W['conv_w'], name=t + "_conf_conv_bwd")
    g['w_ssm_out'] = _mm(S['yn'], dyb, mode='tn', out_dtype=F32, name=t + "_dw_ssm_out")
    dyn = _mm(dyb, W['w_ssm_out'], mode='nt', out_dtype=F32, name=t + "_dyn")
    dy, dz, g['norm_w'] = _rms_bwd(S['y'], S['u'], o_z, W['norm_w'], dyn, name=t + "_rms_bwd")
    dxbc_c, da, ddtx, g['d'] = _ssd_bwd(S['xbc'], S['dt'], S['dtT'], W['a_col'], W['a_row'], W['d_col'], S['hst'], dy,
                                        H=H, name=t + "_ssd_bwd")
    ddt_raw, g['dt_bias'], g['a_log'] = _dt_bwd(da, ddtx, S['dt'], S['udt'], W['dt_bias'], W['a_col'], H=H, name=t + "_dt_bwd")
    dxbc, g['ssm_w'], g['ssm_b'] = _conv_act_bwd(S['u'], o_xbc, W['ssm_w'], W['ssm_b'], dxbc_c, ffn=False, name=t + "_ssm_conv_bwd")
    du = [dglu, dz, dxbc, dgate]
    g['w_pieces'] = [_mm(S['hb'], d, mode='tn', out_dtype=F32, name=f"{t}_dw_in{i}") for i, d in enumerate(du)]
    g['w_dt'] = _mm(S['hb'], ddt_raw, mode='tn', out_dtype=F32, name=t + "_dw_dt")
    w_dt = W['w_dt'] if on_all_grads is None else W['w_dt'] + on_all_grads(g)[0:1, 0:1].astype(W['w_dt'].dtype)
    dh = _mm_nt_cat(du + [ddt_raw], W['w_pieces'] + [w_dt], ds1, DN_ALPHA, name=t + "_dh_in")
    return dh, g


def _layer_grads_to_params(g, dims):
    D, SX, XBC, H, FF = dims
    glu, dz, dxbc, dgate = g['w_pieces']
    w_in = jnp.concatenate([glu, dz, dxbc, g['w_dt'][:, :H], dgate], axis=1)
    return dict(
        w_in=w_in, conv_dw_w=g['conv_w'], conv_dw_b=g['conv_b'][0], conv_ln_g=g['conv_g'][0], conv_ln_b=g['conv_beta'][0],
        w_conv_out=g['w_conv_out'], ssm_conv_w=g['ssm_w'], ssm_conv_b=g['ssm_b'][0],
        ssm_dt_bias=g['dt_bias'][0, :H], ssm_a_log=g['a_log'][0, :H], ssm_d=g['d'][0, :H], ssm_norm_w=g['norm_w'][0],
        w_ssm_out=g['w_ssm_out'], w_o=g['w_o'], ln1_g=g['ln1_g'][0], ln1_b=g['ln1_b'][0],
        w_ffn_up=g['w_up'], ffn_dw_w=g['ffn_w'], ffn_dw_b=g['ffn_b'][0], w_ffn_down=g['w_down'], ln2_g=g['ln2_g'][0], ln2_b=g['ln2_b'][0])


def kernel(x, ln_in_g, ln_in_b, w_in, conv_dw_w, conv_dw_b, conv_ln_g, conv_ln_b, w_conv_out, ssm_conv_w, ssm_conv_b, ssm_dt_bias, ssm_a_log, ssm_d, ssm_norm_w, w_ssm_out, w_o, ln1_g, ln1_b, w_ffn_up, ffn_dw_w, ffn_dw_b, w_ffn_down, ln2_g, ln2_b, loss_target, m_ln_in_g, m_ln_in_b, m_w_in, m_conv_dw_w, m_conv_dw_b, m_conv_ln_g, m_conv_ln_b, m_w_conv_out, m_ssm_conv_w, m_ssm_conv_b, m_ssm_dt_bias, m_ssm_a_log, m_ssm_d, m_ssm_norm_w, m_w_ssm_out, m_w_o, m_ln1_g, m_ln1_b, m_w_ffn_up, m_ffn_dw_w, m_ffn_dw_b, m_w_ffn_down, m_ln2_g, m_ln2_b, v_ln_in_g, v_ln_in_b, v_w_in, v_conv_dw_w, v_conv_dw_b, v_conv_ln_g, v_conv_ln_b, v_w_conv_out, v_ssm_conv_w, v_ssm_conv_b, v_ssm_dt_bias, v_ssm_a_log, v_ssm_d, v_ssm_norm_w, v_w_ssm_out, v_w_o, v_ln1_g, v_ln1_b, v_w_ffn_up, v_ffn_dw_w, v_ffn_dw_b, v_w_ffn_down, v_ln2_g, v_ln2_b):
    weights = (ln_in_g, ln_in_b, w_in, conv_dw_w, conv_dw_b, conv_ln_g, conv_ln_b, w_conv_out, ssm_conv_w, ssm_conv_b,
               ssm_dt_bias, ssm_a_log, ssm_d, ssm_norm_w, w_ssm_out, w_o, ln1_g, ln1_b, w_ffn_up, ffn_dw_w, ffn_dw_b,
               w_ffn_down, ln2_g, ln2_b)
    moments_m = (m_ln_in_g, m_ln_in_b, m_w_in, m_conv_dw_w, m_conv_dw_b, m_conv_ln_g, m_conv_ln_b, m_w_conv_out,
                 m_ssm_conv_w, m_ssm_conv_b, m_ssm_dt_bias, m_ssm_a_log, m_ssm_d, m_ssm_norm_w, m_w_ssm_out, m_w_o,
                 m_ln1_g, m_ln1_b, m_w_ffn_up, m_ffn_dw_w, m_ffn_dw_b, m_w_ffn_down, m_ln2_g, m_ln2_b)
    moments_v = (v_ln_in_g, v_ln_in_b, v_w_in, v_conv_dw_w, v_conv_dw_b, v_conv_ln_g, v_conv_ln_b, v_w_conv_out,
                 v_ssm_conv_w, v_ssm_conv_b, v_ssm_dt_bias, v_ssm_a_log, v_ssm_d, v_ssm_norm_w, v_w_ssm_out, v_w_o,
                 v_ln1_g, v_ln1_b, v_w_ffn_up, v_ffn_dw_w, v_ffn_dw_b, v_w_ffn_down, v_ln2_g, v_ln2_b)
    p = dict(zip(PARAMS, weights))
    pm = dict(zip(PARAMS, moments_m))
    pv = dict(zip(PARAMS, moments_v))

    T, D = x.shape[1], x.shape[2]
    SX = w_ssm_out.shape[1] * N_DEV
    XBC = ssm_conv_b.shape[-1]
    H = ssm_d.shape[-1]
    FF = ffn_dw_b.shape[-1] // 2
    dims = (D, SX, XBC, H, FF)
    depth = w_in.shape[0]

    ALL = tuple(SHARDED)
    got = _exchange(_wire_shards(p, 0, EARLY), scatter=False, name="gather_l0_first")
    rest_shards, got = lax.optimization_barrier((_wire_shards(p, 0, LATE), got))
    rest_handle, tok = _exchange_start(rest_shards, scatter=False, name="gather_l0_rest_start")
    first = _layer_weights(p, _full_weights(got, p, EARLY), 0, dims)
    in_flight = {}

    def start_next_gather(l, behind):
        shards, _ = lax.optimization_barrier((_wire_shards(p, l, ALL), behind))
        in_flight[l], t = _exchange_start(shards, scatter=False, name=f"gather_l{l}_start")
        in_flight[l] = (in_flight[l], shards)
        return t

    def late_weights(l):
        def late(x):
            if l > 0:
                return {}
            got = _with_own_slot(_exchange_wait(rest_handle, x, scatter=False, name="gather_l0_rest_wait"), rest_shards, scatter=False)
            W = _layer_weights(p, _full_weights(got, p, LATE), 0, dims)
            if depth > 1:
                t = start_next_gather(1, got)
                W['w_conv_out'] = W['w_conv_out'] + t[0:1, 0:1].astype(W['w_conv_out'].dtype)
            return W
        return late

    xs = x.reshape(T, D)
    h, hb = _ln_fwd(xs, _row(ln_in_g) + tok[0:1, 0:1], _row(ln_in_b), name="ln_in_fwd")
    saved, layers = [], []
    for l in range(depth):
        if l > 0:
            handle, shards = in_flight.pop(l)
            got = _with_own_slot(_exchange_wait(handle, h, scatter=False, name=f"gather_l{l}_wait"), shards, scatter=False)
            first = _layer_weights(p, _full_weights(got, p, ALL), l, dims)
            if l + 1 < depth:
                t = start_next_gather(l + 1, got)
                first['w_dt'] = first['w_dt'] + t[0:1, 0:1].astype(first['w_dt'].dtype)
        h, hb, s, W = _layer_fwd(h, hb, first, late_weights(l), l, dims)
        saved.append(s)
        layers.append(W)
    loss_part, dh = _loss_fwd_bwd(h, loss_target.reshape(T, D), name="loss")

    layer_grads = [None] * depth
    flying, arrived = [], {}

    def wait_flying(after):
        while flying:
            l_, names, handle, parts = flying.pop(0)
            lands = _with_own_slot(_exchange_wait(handle, after, scatter=True, name=f"scatter_l{l_}_{names[0]}_wait"), parts, scatter=True)
            arrived.update({(l_, k): a for k, a in zip(names, lands)})
            after = lands[0]
        return after

    def start_scatter(l, names, grads_now, after):
        parts, _ = lax.optimization_barrier((_grad_parts(grads_now, p, names), wait_flying(after)))
        handle, t = _exchange_start(parts, scatter=True, name=f"scatter_l{l}_{names[0]}_start")
        flying.append((l, names, handle, parts))
        return t

    def on_ffn_grads(g):
        return start_scatter(0, FFN, dict(w_ffn_up=g['w_up'], ffn_dw_w=g['ffn_w'], w_ffn_down=g['w_down']), g['w_up'])

    def on_all_grads(g):
        layer_grads[0] = _layer_grads_to_params(g, dims)
        return start_scatter(0, NOT_FFN, layer_grads[0], g['w_dt'])

    tok = jnp.zeros((8, LANES), F32)
    for l in reversed(range(depth)):
        W = dict(layers[l], ln2_g=layers[l]['ln2_g'] + tok[0:1, 0:1])
        if l > 0:
            dh, g = _layer_bwd(dh, W, saved[l], l, dims)
            layer_grads[l] = _layer_grads_to_params(g, dims)
            tok = start_scatter(l, ALL, layer_grads[l], dh)
        else:
            dh, g = _layer_bwd(dh, W, saved[l], l, dims, on_ffn_grads, on_all_grads)
    grad_x, dg_in, db_in = _ln_bwd(xs, dh, _row(ln_in_g), _row(ln_in_b), silu=False, name="ln_in_bwd")
    wait_flying(grad_x)

    small = {k: jnp.stack([layer_grads[l][k] for l in range(depth)]) for k in layer_grads[0] if k not in SHARDED}
    small['ln_in_g'], small['ln_in_b'], small['loss'] = dg_in[0], db_in[0], loss_part
    grads = _allreduce_small(small)
    loss = grads.pop('loss').reshape(())

    delta, new_m, new_v = {}, {}, {}
    for k in SHARDED:
        per_layer = [_sum_adamw(arrived[(l, k)], p[k], pm[k], pv[k], l, name=f"sum_adamw_l{l}_{k}") for l in range(depth)]
        grads[k], delta[k], new_m[k], new_v[k] = (jnp.stack([r[j] for r in per_layer]) for j in range(4))
    rest = [k for k in PARAMS if k not in SHARDED]
    flat = lambda d: _to_rows(jnp.concatenate([d[k].reshape(-1) for k in rest]), 8)
    d_, m_, v_ = _adamw(flat(p), flat(grads), flat(pm), flat(pv), name="adamw_small")
    off = 0
    for k in rest:
        n, shp = p[k].size, p[k].shape
        delta[k] = d_.reshape(-1)[off:off + n].reshape(shp)
        new_m[k] = m_.reshape(-1)[off:off + n].reshape(shp)
        new_v[k] = v_.reshape(-1)[off:off + n].reshape(shp)
        off += n

    return (loss, grad_x.reshape(x.shape), *[grads[k] for k in PARAMS], *[delta[k] for k in PARAMS],
            *[new_m[k] for k in PARAMS], *[new_v[k] for k in PARAMS])
```

```python
import math

import jax
import jax.numpy as jnp
from jax import lax
from jax.experimental import pallas as pl
from jax.experimental.pallas import tpu as pltpu

F32 = jnp.float32
BF16 = jnp.bfloat16
WIRE = jnp.bfloat16
HIGHEST = lax.Precision.HIGHEST

DEPTH = 2
SSM_STATE = 128
SSM_CHUNK = 128
SSM_GROUPS = 4
SSM_HEAD_DIM = 64
DN_ALPHA = (2 * DEPTH) ** 0.25
LN_EPS = 1e-5
RMS_EPS = 1e-5
ADAM_LR = 0.001
ADAM_B1 = 0.9
ADAM_B2 = 0.999
ADAM_EPS = 1e-08
ADAM_WD = 0.01
ADAM_STEP = 10

N_DEV = 8
LANES = 128
VMEM_LIMIT = 48 * 1024 * 1024
GLU_TILE = 1024
FFN_TILE = 2816
CONV_ROWS = 256

PARAMS = ['ln_in_g', 'ln_in_b', 'w_in', 'conv_dw_w', 'conv_dw_b', 'conv_ln_g', 'conv_ln_b', 'w_conv_out',
          'ssm_conv_w', 'ssm_conv_b', 'ssm_dt_bias', 'ssm_a_log', 'ssm_d', 'ssm_norm_w', 'w_ssm_out', 'w_o',
          'ln1_g', 'ln1_b', 'w_ffn_up', 'ffn_dw_w', 'ffn_dw_b', 'w_ffn_down', 'ln2_g', 'ln2_b']
SHARDED = {'w_in': (2, True), 'conv_dw_w': (2, False), 'w_conv_out': (1, True), 'ssm_conv_w': (2, False),
           'w_ssm_out': (1, True), 'w_o': (1, True), 'w_ffn_up': (2, True), 'ffn_dw_w': (2, False),
           'w_ffn_down': (1, True)}


def _sds(shape, dtype):
    return jax.ShapeDtypeStruct(tuple(shape), dtype)


def _tile(dim, pref):
    return pref if dim % pref == 0 else dim


def _pick(dim, pref, mult):
    best = None
    for t in range(mult, min(dim, pref) + 1, mult):
        if dim % t == 0:
            best = t
    return best or dim


def _cp(*sem):
    return pltpu.CompilerParams(dimension_semantics=sem, vmem_limit_bytes=VMEM_LIMIT)


def _f32(x):
    return x.astype(F32)


def _sigmoid(x):
    return 1.0 / (1.0 + jnp.exp(-x))


def _silu_and_grad(x):
    s = _sigmoid(x)
    return x * s, s * (1.0 + x * (1.0 - s))


def _ln_stats(s):
    mu = jnp.mean(s, axis=-1, keepdims=True)
    xc = s - mu
    var = jnp.mean(xc * xc, axis=-1, keepdims=True)
    rstd = lax.rsqrt(var + LN_EPS)
    return xc * rstd, rstd


def _dot(a, b, dims):
    return lax.dot_general(a, b, (dims, ((), ())), preferred_element_type=F32)


_NN = ((1,), (0,))
_NT = ((1,), (1,))
_TN = ((0,), (0,))


def _mm(a, b, *, mode, out_dtype, name, res=None, res_scale=1.0, tm=1024, tn=1408, tk=1408):
    if mode == 'nn':
        (M, K), (_, N) = a.shape, b.shape
    elif mode == 'nt':
        (M, K), (N, _) = a.shape, b.shape
    else:
        (K, M), (_, N) = a.shape, b.shape
        tm, tk = 1408, 1024
    tm, tn, tk = _pick(M, tm, 8), _pick(N, tn, LANES), _pick(K, tk, LANES)
    nk = K // tk
    dims = {'nn': _NN, 'nt': _NT, 'tn': _TN}[mode]

    def body(*refs):
        if res is None:
            a_ref, b_ref, o_ref = refs[:3]
            r_ref = None
        else:
            a_ref, b_ref, r_ref, o_ref = refs[:4]
        p = _dot(a_ref[...].astype(BF16), b_ref[...].astype(BF16), dims)

        def finish(acc):
            if r_ref is not None:
                acc = acc + res_scale * r_ref[...]
            o_ref[...] = acc.astype(out_dtype)

        if nk == 1:
            finish(p)
        else:
            acc_ref = refs[-1]
            k = pl.program_id(2)

            @pl.when(k == 0)
            def _():
                acc_ref[...] = p

            @pl.when(k > 0)
            def _():
                acc_ref[...] += p

            @pl.when(k == nk - 1)
            def _():
                finish(acc_ref[...])

    if mode == 'nn':
        a_spec = pl.BlockSpec((tm, tk), lambda i, j, k: (i, k))
        b_spec = pl.BlockSpec((tk, tn), lambda i, j, k: (k, j))
    elif mode == 'nt':
        a_spec = pl.BlockSpec((tm, tk), lambda i, j, k: (i, k))
        b_spec = pl.BlockSpec((tn, tk), lambda i, j, k: (j, k))
    else:
        a_spec = pl.BlockSpec((tk, tm), lambda i, j, k: (k, i))
        b_spec = pl.BlockSpec((tk, tn), lambda i, j, k: (k, j))
    o_spec = pl.BlockSpec((tm, tn), lambda i, j, k: (i, j))
    in_specs = [a_spec, b_spec] + ([o_spec] if res is not None else [])
    args = (a, b) + ((res,) if res is not None else ())
    return pl.pallas_call(
        body, name=name, out_shape=_sds((M, N), out_dtype), grid=(M // tm, N // tn, nk),
        in_specs=in_specs, out_specs=o_spec,
        scratch_shapes=[pltpu.VMEM((tm, tn), F32)] if nk > 1 else [],
        compiler_params=_cp("parallel", "parallel", "arbitrary"))(*args)


def _mm_nt_cat(a_list, b_list, res, res_scale, *, name, tm=1024, tk=512):
    M, N = a_list[0].shape[0], b_list[0].shape[0]
    tm = _pick(M, tm, 8)
    tks = [_pick(a.shape[1], tk, LANES) for a in a_list]
    steps = [a.shape[1] // t for a, t in zip(a_list, tks)]
    offs = [sum(steps[:i]) for i in range(len(steps))]
    nk, n = sum(steps), len(a_list)

    def body(*refs):
        a_refs, b_refs, r_ref, o_ref, acc_ref = refs[:n], refs[n:2 * n], refs[2 * n], refs[2 * n + 1], refs[2 * n + 2]
        k = pl.program_id(1)

        @pl.when(k == 0)
        def _():
            acc_ref[...] = res_scale * r_ref[...]

        for i in range(n):
            @pl.when((k >= offs[i]) & (k < offs[i] + steps[i]))
            def _(i=i):
                acc_ref[...] += _dot(a_refs[i][...].astype(BF16), b_refs[i][...].astype(BF16), _NT)

        @pl.when(k == nk - 1)
        def _():
            o_ref[...] = acc_ref[...]

    def kmap(i):
        return lambda m, k: jnp.clip(k - offs[i], 0, steps[i] - 1)

    a_specs = [pl.BlockSpec((tm, tks[i]), lambda m, k, i=i: (m, kmap(i)(m, k))) for i in range(n)]
    b_specs = [pl.BlockSpec((N, tks[i]), lambda m, k, i=i: (0, kmap(i)(m, k))) for i in range(n)]
    row = pl.BlockSpec((tm, N), lambda m, k: (m, 0))
    return pl.pallas_call(
        body, name=name, out_shape=_sds((M, N), F32), grid=(M // tm, nk), in_specs=a_specs + b_specs + [row],
        out_specs=row, scratch_shapes=[pltpu.VMEM((tm, N), F32)],
        compiler_params=_cp("parallel", "arbitrary"))(*a_list, *b_list, res)


def _mm_res_ln(a, w, res, g, b, *, name, tm=512, tk=1408):
    (M, K), (_, N) = a.shape, w.shape
    tm, tk = _pick(M, tm, 8), _pick(K, tk, LANES)
    nk = K // tk

    def body(a_ref, w_ref, r_ref, g_ref, b_ref, h_ref, s_ref, hb_ref, acc_ref):
        k = pl.program_id(1)
        p = _dot(a_ref[...].astype(BF16), w_ref[...].astype(BF16), _NN)

        @pl.when(k == 0)
        def _():
            acc_ref[...] = p

        @pl.when(k > 0)
        def _():
            acc_ref[...] += p

        @pl.when(k == nk - 1)
        def _():
            s = DN_ALPHA * r_ref[...] + acc_ref[...]
            xhat, _ = _ln_stats(s)
            s_ref[...] = s
            h = xhat * g_ref[...] + b_ref[...]
            h_ref[...] = h
            hb_ref[...] = h.astype(BF16)

    row = pl.BlockSpec((tm, N), lambda i, k: (i, 0))
    vec = pl.BlockSpec((1, N), lambda i, k: (0, 0))
    return pl.pallas_call(
        body, name=name, out_shape=(_sds((M, N), F32), _sds((M, N), F32), _sds((M, N), BF16)), grid=(M // tm, nk),
        in_specs=[pl.BlockSpec((tm, tk), lambda i, k: (i, k)), pl.BlockSpec((tk, N), lambda i, k: (k, 0)), row, vec, vec],
        out_specs=(row, row, row), scratch_shapes=[pltpu.VMEM((tm, N), F32)],
        compiler_params=_cp("parallel", "arbitrary"))(a, w, res, g, b)


def _ln_fwd(x, g, b, *, name):
    T, D = x.shape
    tt = _tile(T, 512)

    def body(x_ref, g_ref, b_ref, o_ref, ob_ref):
        xhat, _ = _ln_stats(x_ref[...])
        h = xhat * g_ref[...] + b_ref[...]
        o_ref[...] = h
        ob_ref[...] = h.astype(BF16)

    row = pl.BlockSpec((tt, D), lambda i: (i, 0))
    vec = pl.BlockSpec((1, D), lambda i: (0, 0))
    return pl.pallas_call(body, name=name, out_shape=(_sds((T, D), F32), _sds((T, D), BF16)), grid=(T // tt,),
                          in_specs=[row, vec, vec], out_specs=(row, row), compiler_params=_cp("parallel"))(x, g, b)


def _ln_bwd(s, dy, g, b, *, silu, name):
    T, D = s.shape
    tt = _tile(T, 512)

    def body(s_ref, dy_ref, g_ref, b_ref, ds_ref, dg_ref, db_ref):
        i = pl.program_id(0)
        xhat, rstd = _ln_stats(s_ref[...])
        gg = g_ref[...]
        dyl = dy_ref[...]
        if silu:
            _, dsilu = _silu_and_grad(xhat * gg + b_ref[...])
            dyl = dyl * dsilu
        dxh = dyl * gg
        m1 = jnp.mean(dxh, axis=-1, keepdims=True)
        m2 = jnp.mean(dxh * xhat, axis=-1, keepdims=True)
        ds_ref[...] = rstd * (dxh - m1 - xhat * m2)

        @pl.when(i == 0)
        def _():
            dg_ref[...] = jnp.zeros_like(dg_ref)
            db_ref[...] = jnp.zeros_like(db_ref)

        dg_ref[...] += jnp.sum(dyl * xhat, axis=0, keepdims=True)
        db_ref[...] += jnp.sum(dyl, axis=0, keepdims=True)

    row = pl.BlockSpec((tt, D), lambda i: (i, 0))
    vec = pl.BlockSpec((1, D), lambda i: (0, 0))
    return pl.pallas_call(body, name=name, out_shape=(_sds((T, D), F32), _sds((1, D), F32), _sds((1, D), F32)),
                          grid=(T // tt,), in_specs=[row, row, vec, vec], out_specs=(row, vec, vec),
                          compiler_params=_cp("arbitrary"))(s, dy, g, b)


def _loss_fwd_bwd(h, tgt, *, name):
    T, D = h.shape
    tt = _tile(T, 512)

    def body(h_ref, t_ref, dh_ref, l_ref):
        i = pl.program_id(0)
        e = h_ref[...] - t_ref[...]
        dh_ref[...] = e * (1.0 / D)

        @pl.when(i == 0)
        def _():
            l_ref[...] = jnp.zeros_like(l_ref)

        part = jnp.sum(jnp.sum(e * e, axis=1, keepdims=True), axis=0, keepdims=True) * (0.5 / D)
        l_ref[...] += jnp.broadcast_to(part, l_ref.shape)

    row = pl.BlockSpec((tt, D), lambda i: (i, 0))
    one = pl.BlockSpec((8, LANES), lambda i: (0, 0))
    dh, l = pl.pallas_call(body, name=name, out_shape=(_sds((T, D), F32), _sds((8, LANES), F32)), grid=(T // tt,),
                           in_specs=[row, row], out_specs=(row, one), compiler_params=_cp("arbitrary"))(h, tgt)
    return l[0:1, 0:1], dh


def _halo_rows(k):
    return 32 if k > 17 else 16


def _shifted(ext, K, first):
    rolled = {0: ext}
    out = []
    for k in range(K):
        r = (first + k) % 8
        if r not in rolled:
            rolled[r] = pltpu.roll(ext, ext.shape[0] - r, 0)
        out.append((rolled[r], first + k - r))
    return out


def _taps(ext, w_ref, sl, K, first, n, reverse=False):
    acc = None
    for k, (z, base) in enumerate(_shifted(ext, K, first)):
        kw = K - 1 - k if reverse else k
        term = w_ref[kw:kw + 1, sl] * z[base:base + n]
        acc = term if acc is None else acc + term
    return acc


def _tap_sums(d, ext, dw_ref, sl, K, first):
    n = d.shape[0]
    for k, (z, base) in enumerate(_shifted(ext, K, first)):
        dw_ref[k:k + 1, sl] += jnp.sum(d * z[base:base + n], axis=0, keepdims=True)


def _strip_width(width):
    return LANES if width % LANES == 0 else width


def _for_strips(width, fn):
    sw = _strip_width(width)

    def step(s, carry):
        fn(pl.ds(pl.multiple_of(s * sw, sw), sw), s)
        return carry

    lax.fori_loop(0, width // sw, step, 0)


def _prev_rows(tt, hb):
    return lambda t: jnp.maximum(t * (tt // hb) - 1, 0)


def _next_rows(tt, hb, T):
    return lambda t: jnp.minimum((t + 1) * (tt // hb), T // hb - 1)


def _conf_fwd(u, w, b, g, beta, *, D, name):
    T = u.shape[0]
    K = w.shape[0]
    tt, hb, tc = _tile(T, CONV_ROWS), _halo_rows(K), _tile(D, GLU_TILE)
    sw = _strip_width(tc)
    per_tile = tc // sw
    prev = _prev_rows(tt, hb)

    def body(u_ref, uh_ref, w_ref, b_ref, g_ref, beta_ref, v1_ref, v3_ref):
        i = pl.program_id(0)

        def strip(sl, s):
            a0 = (s // per_tile) * (2 * tc) + (s % per_tile) * sw
            a_sl, g_sl = pl.ds(pl.multiple_of(a0, sw), sw), pl.ds(pl.multiple_of(a0 + tc, sw), sw)
            halo = jnp.where(i > 0, _f32(uh_ref[:, a_sl]) * _sigmoid(_f32(uh_ref[:, g_sl])), 0.0)
            ext = jnp.concatenate([halo, _f32(u_ref[:, a_sl]) * _sigmoid(_f32(u_ref[:, g_sl]))], axis=0)
            v1_ref[:, sl] = _taps(ext, w_ref, sl, K, hb - (K - 1), tt) + b_ref[:, sl]

        _for_strips(D, strip)
        xhat, _ = _ln_stats(v1_ref[...])
        v2 = xhat * g_ref[...] + beta_ref[...]
        v3_ref[...] = (v2 * _sigmoid(v2)).astype(BF16)

    row = pl.BlockSpec((tt, D), lambda t: (t, 0))
    vec = pl.BlockSpec((1, D), lambda t: (0, 0))
    return pl.pallas_call(
        body, name=name, out_shape=(_sds((T, D), F32), _sds((T, D), BF16)), grid=(T // tt,),
        in_specs=[pl.BlockSpec((tt, 2 * D), lambda t: (t, 0)), pl.BlockSpec((hb, 2 * D), lambda t: (prev(t), 0)),
                  pl.BlockSpec((K, D), lambda t: (0, 0)), vec, vec, vec],
        out_specs=(row, row), compiler_params=_cp("parallel"))(u, u, w, b, g, beta)


def _conv_act_fwd(x, off, w, b, *, ffn, name):
    T = x.shape[0]
    K, Cw = w.shape
    tt, hb = _tile(T, CONV_ROWS), _halo_rows(K)
    tc = _tile(Cw // 2, FFN_TILE) if ffn else _pick(math.gcd(Cw, off), 1536, LANES)
    xw = 2 * tc if ffn else tc
    assert off % xw == 0
    ob = off // xw
    prev = _prev_rows(tt, hb)

    def body(x_ref, xh_ref, w_ref, b_ref, o_ref):
        first = pl.program_id(0) == 0

        def pre_of(sl):
            ext = jnp.concatenate([jnp.where(first, 0.0, _f32(xh_ref[:, sl])), _f32(x_ref[:, sl])], axis=0)
            return _taps(ext, w_ref, sl, K, hb - (K - 1), tt) + b_ref[:, sl]

        def strip(sl, s):
            if ffn:
                gate = pre_of(sl)
                val = pre_of(pl.ds(pl.multiple_of(tc + s * sw, sw), sw))
                o_ref[:, sl] = (gate * _sigmoid(gate) * val).astype(BF16)
            else:
                pre = pre_of(sl)
                o_ref[:, sl] = pre * _sigmoid(pre)

        _for_strips(tc, strip)

    sw = _strip_width(tc)
    return pl.pallas_call(
        body, name=name, out_shape=_sds((T, Cw // 2), BF16) if ffn else _sds((T, Cw), F32), grid=(T // tt, Cw // xw),
        in_specs=[pl.BlockSpec((tt, xw), lambda t, c: (t, c + ob)), pl.BlockSpec((hb, xw), lambda t, c: (prev(t), c + ob)),
                  pl.BlockSpec((K, xw), lambda t, c: (0, c)), pl.BlockSpec((1, xw), lambda t, c: (0, c))],
        out_specs=pl.BlockSpec((tt, tc), lambda t, c: (t, c)),
        compiler_params=_cp("parallel", "parallel"))(x, x, w, b)


def _conv_act_bwd(x, off, w, b, dout, *, ffn, name):
    T = x.shape[0]
    K, Cw = w.shape
    tt, hb = _tile(T, CONV_ROWS), _halo_rows(K)
    tc = _tile(Cw // 2, FFN_TILE) if ffn else _pick(math.gcd(Cw, off), 1536, LANES)
    xw = 2 * tc if ffn else tc
    assert off % xw == 0
    ob = off // xw
    nt = T // tt
    prev, nxt = _prev_rows(tt, hb), _next_rows(tt, hb, T)

    sw = _strip_width(tc)

    def body(x_ref, xp_ref, xn_ref, d_ref, dn_ref, w_ref, b_ref, dx_ref, dw_ref, db_ref):
        t = pl.program_id(1)

        @pl.when(t == 0)
        def _():
            dw_ref[...] = jnp.zeros_like(dw_ref)
            db_ref[...] = jnp.zeros_like(db_ref)

        def ext_of(sl):
            return jnp.concatenate([jnp.where(t == 0, 0.0, _f32(xp_ref[:, sl])), _f32(x_ref[:, sl]), _f32(xn_ref[:, sl])], axis=0)

        def pre_of(ext, sl):
            return _taps(ext, w_ref, sl, K, hb - (K - 1), tt + hb) + b_ref[:, sl]

        def finish(ext, dpre, sl):
            dx_ref[:, sl] = _taps(dpre, w_ref, sl, K, 0, tt, reverse=True).astype(BF16)
            dp = dpre[0:tt]
            _tap_sums(dp, ext, dw_ref, sl, K, hb - (K - 1))
            db_ref[:, sl] += jnp.sum(dp, axis=0, keepdims=True)

        def strip(sl, s):
            d = jnp.concatenate([d_ref[:, sl], jnp.where(t == nt - 1, 0.0, dn_ref[:, sl])], axis=0)
            if ffn:
                vsl = pl.ds(pl.multiple_of(tc + s * sw, sw), sw)
                eg, ev = ext_of(sl), ext_of(vsl)
                sg, dsg = _silu_and_grad(pre_of(eg, sl))
                val = pre_of(ev, vsl)
                finish(eg, d * val * dsg, sl)
                finish(ev, d * sg, vsl)
            else:
                ext = ext_of(sl)
                finish(ext, d * _silu_and_grad(pre_of(ext, sl))[1], sl)

        _for_strips(tc, strip)

    return pl.pallas_call(
        body, name=name, out_shape=(_sds((T, Cw), BF16), _sds((K, Cw), F32), _sds((1, Cw), F32)),
        grid=(Cw // xw, nt),
        in_specs=[pl.BlockSpec((tt, xw), lambda c, t: (t, c + ob)), pl.BlockSpec((hb, xw), lambda c, t: (prev(t), c + ob)),
                  pl.BlockSpec((hb, xw), lambda c, t: (nxt(t), c + ob)),
                  pl.BlockSpec((tt, tc), lambda c, t: (t, c)), pl.BlockSpec((hb, tc), lambda c, t: (nxt(t), c)),
                  pl.BlockSpec((K, xw), lambda c, t: (0, c)), pl.BlockSpec((1, xw), lambda c, t: (0, c))],
        out_specs=(pl.BlockSpec((tt, xw), lambda c, t: (t, c)), pl.BlockSpec((K, xw), lambda c, t: (0, c)),
                   pl.BlockSpec((1, xw), lambda c, t: (0, c))),
        compiler_params=_cp("parallel", "arbitrary"))(x, x, x, dout, dout, w, b)


def _glu_conv_bwd(dpre, u, w, *, name):
    T, C = dpre.shape
    K = w.shape[0]
    tt, hb, tc = _tile(T, CONV_ROWS), _halo_rows(K), _tile(C, GLU_TILE)
    nt = T // tt
    prev, nxt = _prev_rows(tt, hb), _next_rows(tt, hb, T)

    sw = _strip_width(tc)

    def body(d_ref, dn_ref, x_ref, xh_ref, w_ref, dx_ref, dw_ref, db_ref):
        t = pl.program_id(1)

        @pl.when(t == 0)
        def _():
            dw_ref[...] = jnp.zeros_like(dw_ref)
            db_ref[...] = jnp.zeros_like(db_ref)

        def strip(sl, s):
            gsl = pl.ds(pl.multiple_of(tc + s * sw, sw), sw)
            d = d_ref[:, sl]
            dext = jnp.concatenate([d, jnp.where(t == nt - 1, 0.0, dn_ref[:, sl])], axis=0)
            dxin = _taps(dext, w_ref, sl, K, 0, tt, reverse=True)
            a, sg = _f32(x_ref[:, sl]), _sigmoid(_f32(x_ref[:, gsl]))
            v0 = a * sg
            dx_ref[:, sl] = (dxin * sg).astype(BF16)
            dx_ref[:, gsl] = (dxin * v0 * (1.0 - sg)).astype(BF16)
            halo = jnp.where(t == 0, 0.0, _f32(xh_ref[:, sl]) * _sigmoid(_f32(xh_ref[:, gsl])))
            _tap_sums(d, jnp.concatenate([halo, v0], axis=0), dw_ref, sl, K, hb - (K - 1))
            db_ref[:, sl] += jnp.sum(d, axis=0, keepdims=True)

        _for_strips(tc, strip)

    return pl.pallas_call(
        body, name=name, out_shape=(_sds((T, 2 * C), BF16), _sds((K, C), F32), _sds((1, C), F32)), grid=(C // tc, nt),
        in_specs=[pl.BlockSpec((tt, tc), lambda c, t: (t, c)), pl.BlockSpec((hb, tc), lambda c, t: (nxt(t), c)),
                  pl.BlockSpec((tt, 2 * tc), lambda c, t: (t, c)), pl.BlockSpec((hb, 2 * tc), lambda c, t: (prev(t), c)),
                  pl.BlockSpec((K, tc), lambda c, t: (0, c))],
        out_specs=(pl.BlockSpec((tt, 2 * tc), lambda c, t: (t, c)), pl.BlockSpec((K, tc), lambda c, t: (0, c)),
                   pl.BlockSpec((1, tc), lambda c, t: (0, c))),
        compiler_params=_cp("parallel", "arbitrary"))(dpre, dpre, u, u, w)


def _softplus(x):
    t = jnp.exp(-jnp.abs(x))
    u = 1.0 + t
    log1p = jnp.where(u == 1.0, t, jnp.log(u) * t / jnp.where(u == 1.0, 1.0, u - 1.0))
    return jnp.maximum(x, 0.0) + log1p


def _dt_fwd(udt, bias, *, H, name):
    T = udt.shape[0]
    tt = _tile(T, 1024)

    def body(u_ref, b_ref, o_ref):
        lane = lax.broadcasted_iota(jnp.int32, (tt, LANES), 1)
        o_ref[...] = jnp.where(lane < H, _softplus(u_ref[...] + b_ref[...]), 0.0)

    row = pl.BlockSpec((tt, LANES), lambda i: (i, 0))
    return pl.pallas_call(body, name=name, out_shape=_sds((T, LANES), F32), grid=(T // tt,),
                          in_specs=[row, pl.BlockSpec((1, LANES), lambda i: (0, 0))], out_specs=row,
                          compiler_params=_cp("parallel"))(udt, bias)


def _dt_bwd(da, ddtx, dt, udt, bias, a_col, *, H, name):
    T = da.shape[0]
    tt = _tile(T, 1024)

    def body(da_ref, dx_ref, dt_ref, u_ref, b_ref, a_ref, draw_ref, dbias_ref, dalog_ref):
        i = pl.program_id(0)
        lane = lax.broadcasted_iota(jnp.int32, (tt, LANES), 1)
        dav = da_ref[...]
        ddt = dav * a_ref[...] + dx_ref[...]
        draw = jnp.where(lane < H, ddt * _sigmoid(u_ref[...] + b_ref[...]), 0.0)
        draw_ref[...] = draw.astype(BF16)

        @pl.when(i == 0)
        def _():
            dbias_ref[...] = jnp.zeros_like(dbias_ref)
            dalog_ref[...] = jnp.zeros_like(dalog_ref)

        dbias_ref[...] += jnp.sum(draw, axis=0, keepdims=True)
        dalog_ref[...] += jnp.sum(dav * dt_ref[...], axis=0, keepdims=True) * a_ref[...]

    row = pl.BlockSpec((tt, LANES), lambda i: (i, 0))
    vec = pl.BlockSpec((1, LANES), lambda i: (0, 0))
    return pl.pallas_call(body, name=name,
                          out_shape=(_sds((T, LANES), BF16), _sds((1, LANES), F32), _sds((1, LANES), F32)),
                          grid=(T // tt,), in_specs=[row, row, row, row, vec, vec], out_specs=(row, vec, vec),
                          compiler_params=_cp("arbitrary"))(da, ddtx, dt, udt, bias, a_col)


def _ssd_consts():
    L = SSM_CHUNK
    r = lax.broadcasted_iota(jnp.int32, (L, L), 0)
    c = lax.broadcasted_iota(jnp.int32, (L, L), 1)
    return r, c


def _ssd_chunk_decays(dtc_ref, dtr_ref, acol_ref, arow_ref, cs_ref, csr_ref, r, c):
    L = SSM_CHUNK
    dtc = dtc_ref[...]
    tril = (r >= c).astype(F32)
    triu = (r <= c).astype(F32)
    cs_ref[...] = jnp.dot(tril, dtc * acol_ref[...], precision=HIGHEST, preferred_element_type=F32)
    csr_ref[...] = jnp.dot(dtr_ref[...] * arow_ref[...], triu, precision=HIGHEST, preferred_element_type=F32)
    cs = cs_ref[...]
    cs_last = cs_ref[L - 1:L, :]
    return dtc, cs, jnp.exp(cs), jnp.exp(cs_last - cs), jnp.exp(cs_last), triu


def _head_spread(arrs, g, heads_per_group):
    P = SSM_HEAD_DIM
    gw = heads_per_group * P
    head = lax.broadcasted_iota(jnp.int32, (LANES, gw), 0)
    lane = lax.broadcasted_iota(jnp.int32, (LANES, gw), 1)
    spread = (head == g * heads_per_group + lane // P).astype(BF16)
    out = _dot(jnp.concatenate(arrs, axis=0).astype(BF16), spread, _NN)
    L = arrs[0].shape[0]
    return [out[i * L:(i + 1) * L] for i in range(len(arrs))]


def _ssd_fwd(xbc, dt, dtT, a_col, a_row, d_col, *, H, name):
    T, XBC = xbc.shape
    L, N, G, P = SSM_CHUNK, SSM_STATE, SSM_GROUPS, SSM_HEAD_DIM
    SX = H * P
    HR = dtT.shape[0]
    nc = T // L
    heads_per_group = H // G

    def body(x_ref, dtc_ref, dtr_ref, acol_ref, arow_ref, d_ref, y_ref, hst_ref, state, cs_ref, csr_ref):
        ci = pl.program_id(0)

        @pl.when(ci == 0)
        def _():
            state[...] = jnp.zeros_like(state)

        hst_ref[0] = state[...]
        r, c = _ssd_consts()
        tri = r >= c
        lane_lo = c < P
        row_lo = r < P
        lane1_lo = lax.broadcasted_iota(jnp.int32, (1, LANES), 1) < P
        dtc, cs, e, ds, cd, _ = _ssd_chunk_decays(dtc_ref, dtr_ref, acol_ref, arow_ref, cs_ref, csr_ref, r, c)
        dsk = d_ref[...]

        for g in range(G):
            Bg = x_ref[:, SX + g * N:SX + (g + 1) * N].astype(BF16)
            Cg = x_ref[:, SX + G * N + g * N:SX + G * N + (g + 1) * N].astype(BF16)
            Gm = _dot(Cg, Bg, _NT)
            dt_g, e_g, ds_g = _head_spread([dtc, e, ds], g, heads_per_group)
            st = []
            for j in range(g * heads_per_group // 2, (g + 1) * heads_per_group // 2):
                h0 = 2 * j
                sl = slice(2 * P * j, 2 * P * (j + 1))
                gl = slice(sl.start - g * heads_per_group * P, sl.stop - g * heads_per_group * P)
                x2 = x_ref[:, sl]
                X2 = x2 * dt_g[:, gl]
                H2 = state[sl, :]
                st.append(dict(h0=h0, sl=sl, x2=x2, X2=X2, H2=H2, e2=e_g[:, gl], R2=_dot(Cg, H2.astype(BF16), _NT),
                               S2=_dot((X2 * ds_g[:, gl]).astype(BF16), Bg, _TN)))
            for s in st:
                ms = []
                for h in (s['h0'], s['h0'] + 1):
                    seg = cs[:, h:h + 1] - csr_ref[h:h + 1, :]
                    ms.append((Gm * jnp.where(tri, jnp.exp(jnp.where(tri, seg, 0.0)), 0.0)).astype(BF16))
                xst = jnp.concatenate([jnp.where(lane_lo, s['X2'], 0.0), jnp.where(lane_lo, 0.0, s['X2'])], axis=0).astype(BF16)
                s['yd'] = _dot(jnp.concatenate(ms, axis=1), xst, _NN)
            for s in st:
                h0, sl = s['h0'], s['sl']
                dsk2 = jnp.where(lane1_lo, dsk[:, h0:h0 + 1], dsk[:, h0 + 1:h0 + 2])
                y_ref[:, sl] = s['yd'] + s['e2'] * s['R2'] + s['x2'] * dsk2
                state[sl, :] = jnp.where(row_lo, cd[:, h0:h0 + 1], cd[:, h0 + 1:h0 + 2]) * s['H2'] + s['S2']

    vec = pl.BlockSpec((1, LANES), lambda i: (0, 0))
    return pl.pallas_call(
        body, name=name, out_shape=(_sds((T, SX), F32), _sds((nc, SX, N), F32)), grid=(nc,),
        in_specs=[pl.BlockSpec((L, XBC), lambda i: (i, 0)), pl.BlockSpec((L, LANES), lambda i: (i, 0)),
                  pl.BlockSpec((HR, L), lambda i: (0, i)), vec, pl.BlockSpec((HR, 1), lambda i: (0, 0)), vec],
        out_specs=(pl.BlockSpec((L, SX), lambda i: (i, 0)), pl.BlockSpec((1, SX, N), lambda i: (i, 0, 0))),
        scratch_shapes=[pltpu.VMEM((SX, N), F32), pltpu.VMEM((L, LANES), F32), pltpu.VMEM((HR, L), F32)],
        compiler_params=_cp("arbitrary"))(xbc, dt, dtT, a_col, a_row, d_col)


def _ssd_bwd(xbc, dt, dtT, a_col, a_row, d_col, hst, dy, *, H, name):
    T, XBC = xbc.shape
    L, N, G, P = SSM_CHUNK, SSM_STATE, SSM_GROUPS, SSM_HEAD_DIM
    SX = H * P
    HR = dtT.shape[0]
    nc = T // L
    heads_per_group = H // G

    def body(x_ref, dtc_ref, dtr_ref, acol_ref, arow_ref, d_ref, hst_ref, dy_ref,
             dx_ref, da_ref, ddtx_ref, dd_ref, dstate, cs_ref, csr_ref):
        ci = pl.program_id(0)

        @pl.when(ci == 0)
        def _():
            dstate[...] = jnp.zeros_like(dstate)
            dd_ref[...] = jnp.zeros_like(dd_ref)

        r, c = _ssd_consts()
        tri = r >= c
        lane_lo = c < P
        row_lo = r < P
        lane1 = lax.broadcasted_iota(jnp.int32, (1, LANES), 1)
        rowc = lax.broadcasted_iota(jnp.int32, (L, 1), 0)
        dtc, cs, e, ds, cd, triu = _ssd_chunk_decays(dtc_ref, dtr_ref, acol_ref, arow_ref, cs_ref, csr_ref, r, c)
        triu_b = triu.astype(BF16)
        dsk = d_ref[...]
        last_row = rowc == L - 1

        triT = r <= c

        def halves(v, axis):
            return jnp.concatenate([jnp.where(lane_lo, v, 0.0), jnp.where(lane_lo, 0.0, v)], axis=axis)

        def head_sum(v, h):
            lo = jnp.sum(jnp.where(lane1 < P, v, 0.0), axis=1, keepdims=True) * (lane1 == h).astype(F32)
            hi = jnp.sum(jnp.where(lane1 < P, 0.0, v), axis=1, keepdims=True) * (lane1 == h + 1).astype(F32)
            return lo + hi

        GW = heads_per_group * P
        gl = lax.broadcasted_iota(jnp.int32, (GW, LANES), 0)
        gc = lax.broadcasted_iota(jnp.int32, (GW, LANES), 1)
        wl = lax.broadcasted_iota(jnp.int32, (heads_per_group * L, LANES), 0)
        wc = lax.broadcasted_iota(jnp.int32, (heads_per_group * L, LANES), 1)
        wide_r = lax.broadcasted_iota(jnp.int32, (L, heads_per_group * L), 0)
        wide_c = lax.broadcasted_iota(jnp.int32, (L, heads_per_group * L), 1)
        below_diag = (wide_c % L) < wide_r
        sums = jnp.zeros((3 * L, LANES), F32)
        da_q = jnp.zeros((L, LANES), F32)
        dcd_acc = jnp.zeros((1, LANES), F32)
        dd_acc = jnp.zeros((1, LANES), F32)
        for g in range(G):
            bsl = slice(SX + g * N, SX + (g + 1) * N)
            csl = slice(SX + G * N + g * N, SX + G * N + (g + 1) * N)
            Bg = x_ref[:, bsl].astype(BF16)
            Cg = x_ref[:, csl].astype(BF16)
            Gm = _dot(Cg, Bg, _NT)
            GmT = _dot(Bg, Cg, _NT)
            dt_g, e_g, ds_g = _head_spread([dtc, e, ds], g, heads_per_group)
            st = []
            for j in range(g * heads_per_group // 2, (g + 1) * heads_per_group // 2):
                h0 = 2 * j
                sl = slice(2 * P * j, 2 * P * (j + 1))
                s = dict(h0=h0, sl=sl, x2=x_ref[:, sl], dy2=dy_ref[:, sl], H2=hst_ref[0, sl, :], dHn=dstate[sl, :])
                gsl = slice(sl.start - g * heads_per_group * P, sl.stop - g * heads_per_group * P)
                s['dt2'], s['e2'], s['ds2'] = dt_g[:, gsl], e_g[:, gsl], ds_g[:, gsl]
                s['X2'] = s['x2'] * s['dt2']
                s['H2b'], s['dHnb'] = s['H2'].astype(BF16), s['dHn'].astype(BF16)
                st.append(s)
            for s in st:
                s['R2'] = _dot(Cg, s['H2b'], _NT)
                s['dXd'] = _dot(Bg, s['dHnb'], _NT)
                s['dM2'] = _dot(s['dy2'].astype(BF16), halves(s['X2'], 0).astype(BF16), _NT)
            dG = jnp.zeros((L, L), F32)
            qs = []
            for s in st:
                mts = []
                for i, h in enumerate((s['h0'], s['h0'] + 1)):
                    z = cs[:, h:h + 1] - csr_ref[h:h + 1, :]
                    Dm = jnp.where(tri, jnp.exp(jnp.where(tri, z, 0.0)), 0.0)
                    DmT = jnp.where(triT, jnp.exp(jnp.where(triT, -z, 0.0)), 0.0)
                    dM = s['dM2'][:, i * L:(i + 1) * L]
                    dG = dG + dM * Dm
                    qs.append((dM * (Gm * Dm)).astype(BF16))
                    mts.append((GmT * DmT).astype(BF16))
                s['dXm'] = _dot(jnp.concatenate(mts, axis=1), halves(s['dy2'], 0).astype(BF16), _NN)
            Wg = _dot(triu_b, jnp.concatenate(qs, axis=1), _NN)
            place = (wc == g * heads_per_group + wl // L).astype(BF16)
            da_q = da_q + _dot(jnp.where(below_diag, Wg, 0.0).astype(BF16), place, _NN)
            dBg = jnp.zeros((L, N), F32)
            dCg = jnp.zeros((L, N), F32)
            des, ddss, dxxs = [], [], []
            for s in st:
                h0, sl, x2, dy2, X2 = s['h0'], s['sl'], s['x2'], s['dy2'], s['X2']
                dR2b = (s['e2'] * dy2).astype(BF16)
                dCg = dCg + _dot(dR2b, s['H2b'], _NN)
                dHr = _dot(dR2b, Cg, _TN)
                dBg = dBg + _dot((X2 * s['ds2']).astype(BF16), s['dHnb'], _NN)
                dX2 = s['ds2'] * s['dXd'] + s['dXm']
                des.append(dy2 * s['R2'])
                ddss.append(s['dXd'] * X2)
                dxxs.append(dX2 * x2)
                prod = s['dHn'] * s['H2']
                for i, h in enumerate((h0, h0 + 1)):
                    rows = jnp.sum(prod[i * P:(i + 1) * P], axis=0, keepdims=True)
                    dcd_acc = dcd_acc + jnp.sum(rows, axis=1, keepdims=True) * (lane1 == h).astype(F32)
                dd_acc = dd_acc + head_sum(jnp.sum(dy2 * x2, axis=0, keepdims=True), h0)
                dsk2 = jnp.where(lane1 < P, dsk[:, h0:h0 + 1], dsk[:, h0 + 1:h0 + 2])
                dx_ref[:, sl] = dX2 * s['dt2'] + dy2 * dsk2
                dstate[sl, :] = jnp.where(row_lo, cd[:, h0:h0 + 1], cd[:, h0 + 1:h0 + 2]) * s['dHn'] + dHr
            stack = jnp.concatenate([jnp.concatenate(v, axis=1) for v in (des, ddss, dxxs)], axis=0).astype(BF16)
            sums = sums + _dot(stack, (gc == g * heads_per_group + gl // P).astype(BF16), _NN)
            dGb = dG.astype(BF16)
            dx_ref[:, bsl] = dBg + _dot(dGb, Cg, _TN)
            dx_ref[:, csl] = dCg + _dot(dGb, Bg, _NN)
        t1 = sums[L:2 * L] * ds
        tail = jnp.sum(t1, axis=0, keepdims=True) + dcd_acc * cd
        dcs = sums[0:L] * e - t1 + jnp.where(last_row, tail, 0.0)
        da_ref[...] = jnp.dot(triu, dcs, precision=HIGHEST, preferred_element_type=F32) + da_q
        ddtx_ref[...] = sums[2 * L:3 * L]
        dd_ref[...] += dd_acc

    vec = pl.BlockSpec((1, LANES), lambda i: (0, 0))
    rev = lambda i: (nc - 1 - i, 0)
    return pl.pallas_call(
        body, name=name,
        out_shape=(_sds((T, XBC), F32), _sds((T, LANES), F32), _sds((T, LANES), F32), _sds((1, LANES), F32)),
        grid=(nc,),
        in_specs=[pl.BlockSpec((L, XBC), rev), pl.BlockSpec((L, LANES), rev),
                  pl.BlockSpec((HR, L), lambda i: (0, nc - 1 - i)), vec, pl.BlockSpec((HR, 1), lambda i: (0, 0)), vec,
                  pl.BlockSpec((1, SX, N), lambda i: (nc - 1 - i, 0, 0)), pl.BlockSpec((L, SX), rev)],
        out_specs=(pl.BlockSpec((L, XBC), rev), pl.BlockSpec((L, LANES), rev), pl.BlockSpec((L, LANES), rev), vec),
        scratch_shapes=[pltpu.VMEM((SX, N), F32), pltpu.VMEM((L, LANES), F32), pltpu.VMEM((HR, L), F32)],
        compiler_params=_cp("arbitrary"))(xbc, dt, dtT, a_col, a_row, d_col, hst, dy)


def _rms_fwd(y, u, zoff, w, *, name):
    T, SX = y.shape
    tt = _tile(T, 256)
    gs = SX // SSM_GROUPS
    zb = zoff // SX

    def body(y_ref, z_ref, w_ref, o_ref):
        for g in range(SSM_GROUPS):
            sl = slice(g * gs, (g + 1) * gs)
            z = _f32(z_ref[:, sl])
            yg = y_ref[:, sl] * (z * _sigmoid(z))
            rstd = lax.rsqrt(jnp.mean(yg * yg, axis=-1, keepdims=True) + RMS_EPS)
            o_ref[:, sl] = (yg * rstd * w_ref[:, sl]).astype(BF16)

    row = pl.BlockSpec((tt, SX), lambda i: (i, 0))
    return pl.pallas_call(body, name=name, out_shape=_sds((T, SX), BF16), grid=(T // tt,),
                          in_specs=[row, pl.BlockSpec((tt, SX), lambda i: (i, zb)), pl.BlockSpec((1, SX), lambda i: (0, 0))],
                          out_specs=row, compiler_params=_cp("parallel"))(y, u, w)


def _rms_bwd(y, u, zoff, w, dyn, *, name):
    T, SX = y.shape
    tt = _tile(T, 256)
    gs = SX // SSM_GROUPS
    zb = zoff // SX

    def body(y_ref, z_ref, w_ref, d_ref, dy_ref, dz_ref, dw_ref):
        i = pl.program_id(0)

        @pl.when(i == 0)
        def _():
            dw_ref[...] = jnp.zeros_like(dw_ref)

        for g in range(SSM_GROUPS):
            sl = slice(g * gs, (g + 1) * gs)
            z, yv, d = _f32(z_ref[:, sl]), y_ref[:, sl], d_ref[:, sl]
            sz, dsz = _silu_and_grad(z)
            yg = yv * sz
            rstd = lax.rsqrt(jnp.mean(yg * yg, axis=-1, keepdims=True) + RMS_EPS)
            t = yg * rstd
            dw_ref[:, sl] += jnp.sum(d * t, axis=0, keepdims=True)
            dt_ = d * w_ref[:, sl]
            dyg = rstd * (dt_ - t * jnp.mean(dt_ * t, axis=-1, keepdims=True))
            dy_ref[:, sl] = dyg * sz
            dz_ref[:, sl] = (dyg * yv * dsz).astype(BF16)

    row = pl.BlockSpec((tt, SX), lambda i: (i, 0))
    vec = pl.BlockSpec((1, SX), lambda i: (0, 0))
    return pl.pallas_call(body, name=name, out_shape=(_sds((T, SX), F32), _sds((T, SX), BF16), _sds((1, SX), F32)),
                          grid=(T // tt,), in_specs=[row, pl.BlockSpec((tt, SX), lambda i: (i, zb)), vec, row],
                          out_specs=(row, row, vec), compiler_params=_cp("arbitrary"))(y, u, w, dyn)


def _gate_fwd(u, goff, ya, yb, *, name):
    T, D = ya.shape
    tt = _tile(T, 512)
    gb = goff // D

    def body(ga_ref, gb_ref, ya_ref, yb_ref, o_ref):
        o_ref[...] = (_sigmoid(_f32(ga_ref[...])) * ya_ref[...] + _sigmoid(_f32(gb_ref[...])) * yb_ref[...]).astype(BF16)

    row = pl.BlockSpec((tt, D), lambda i: (i, 0))
    return pl.pallas_call(body, name=name, out_shape=_sds((T, D), BF16), grid=(T // tt,),
                          in_specs=[pl.BlockSpec((tt, D), lambda i: (i, gb)), pl.BlockSpec((tt, D), lambda i: (i, gb + 1)), row, row],
                          out_specs=row, compiler_params=_cp("parallel"))(u, u, ya, yb)


def _gate_bwd(u, goff, ya, yb, dm, *, name):
    T, D = ya.shape
    tt = _tile(T, 512)
    gb = goff // D

    def body(ga_ref, gb_ref, ya_ref, yb_ref, dm_ref, dya_ref, dyb_ref, dg_ref):
        d = dm_ref[...]
        sa, sb = _sigmoid(_f32(ga_ref[...])), _sigmoid(_f32(gb_ref[...]))
        dya_ref[...] = (d * sa).astype(BF16)
        dyb_ref[...] = (d * sb).astype(BF16)
        dg_ref[...] = jnp.concatenate([d * ya_ref[...] * sa * (1.0 - sa), d * yb_ref[...] * sb * (1.0 - sb)], axis=1).astype(BF16)

    row = pl.BlockSpec((tt, D), lambda i: (i, 0))
    return pl.pallas_call(body, name=name, out_shape=(_sds((T, D), BF16), _sds((T, D), BF16), _sds((T, 2 * D), BF16)),
                          grid=(T // tt,),
                          in_specs=[pl.BlockSpec((tt, D), lambda i: (i, gb)), pl.BlockSpec((tt, D), lambda i: (i, gb + 1)), row, row, row],
                          out_specs=(row, row, pl.BlockSpec((tt, 2 * D), lambda i: (i, 0))),
                          compiler_params=_cp("parallel"))(u, u, ya, yb, dm)


def _adamw_math(w, gg, m, v):
    nm = ADAM_B1 * m + (1.0 - ADAM_B1) * gg
    nv = ADAM_B2 * v + (1.0 - ADAM_B2) * (gg * gg)
    m_hat = nm / (1.0 - ADAM_B1 ** ADAM_STEP)
    v_hat = nv / (1.0 - ADAM_B2 ** ADAM_STEP)
    return -ADAM_LR * (m_hat / (jnp.sqrt(v_hat) + ADAM_EPS) + ADAM_WD * w), nm, nv


def _adamw(w, g, m, v, *, name):
    R, C = w.shape
    tr = _pick(R, 256, 8)

    def body(w_ref, g_ref, m_ref, v_ref, d_ref, nm_ref, nv_ref):
        d_ref[...], nm_ref[...], nv_ref[...] = _adamw_math(w_ref[...], g_ref[...], m_ref[...], v_ref[...])

    blk = pl.BlockSpec((tr, C), lambda i: (i, 0))
    out = _sds((R, C), F32)
    return pl.pallas_call(body, name=name, out_shape=(out, out, out), grid=(R // tr,), in_specs=[blk] * 4,
                          out_specs=(blk,) * 3, compiler_params=_cp("parallel"))(w, g, m, v)


def _sum_adamw(parts, w, m, v, l, *, name):
    shape = w.shape[1:]
    C = shape[-1]
    R = w[0].size // C
    tr = _pick(R, 256, 8)
    if tr % 8:
        w, m, v, l = w[l:l + 1], m[l:l + 1], v[l:l + 1], 0
    lb = l * (R // tr)

    def body(p_ref, w_ref, m_ref, v_ref, g_ref, d_ref, nm_ref, nv_ref):
        gg = p_ref[0].astype(F32)
        for k in range(1, N_DEV):
            gg = gg + p_ref[k].astype(F32)
        g_ref[...] = gg
        d_ref[...], nm_ref[...], nv_ref[...] = _adamw_math(w_ref[...], gg, m_ref[...], v_ref[...])

    blk = pl.BlockSpec((tr, C), lambda i: (i, 0))
    lblk = pl.BlockSpec((tr, C), lambda i: (i + lb, 0))
    out = _sds((R, C), F32)
    stacked = (w.shape[0] * R, C)
    res = pl.pallas_call(body, name=name, out_shape=(out,) * 4, grid=(R // tr,),
                         in_specs=[pl.BlockSpec((N_DEV, tr, C), lambda i: (0, i, 0)), lblk, lblk, lblk],
                         out_specs=(blk,) * 4, compiler_params=_cp("parallel"))(
        parts.reshape(N_DEV, R, C), w.reshape(stacked), m.reshape(stacked), v.reshape(stacked))
    return tuple(r.reshape(shape) for r in res)


def _sum_slots(x, *, name):
    n, R, C = x.shape
    tr = _tile(R, 512)

    def body(x_ref, o_ref):
        acc = x_ref[0].astype(F32)
        for k in range(1, n):
            acc = acc + x_ref[k].astype(F32)
        o_ref[...] = acc

    return pl.pallas_call(body, name=name, out_shape=_sds((R, C), F32), grid=(R // tr,),
                          in_specs=[pl.BlockSpec((n, tr, C), lambda i: (0, i, 0))],
                          out_specs=pl.BlockSpec((tr, C), lambda i: (i, 0)), compiler_params=_cp("parallel"))(x)


def _exchange(xs, *, scatter, name):
    n = len(xs)

    def body(*refs):
        x_refs, o_refs = refs[:n], refs[n:2 * n]
        send_sems, recv_sems, local_sems = refs[2 * n:]
        mx, my, mc = lax.axis_index("x"), lax.axis_index("y"), lax.axis_index("c")
        me = 4 * mx + 2 * my + mc

        def src(i, d):
            return x_refs[i].at[d] if scatter else x_refs[i]

        locals_ = [pltpu.make_async_copy(src(i, me), o_refs[i].at[me], local_sems.at[i]) for i in range(n)]
        for cp in locals_:
            cp.start()
        sends, recvs = [], []
        for k in range(1, N_DEV):
            px = 1 - mx if k & 4 else mx
            py = 1 - my if k & 2 else my
            pc = 1 - mc if k & 1 else mc
            peer = 4 * px + 2 * py + pc
            for i in range(n):
                common = dict(send_sem=send_sems.at[k - 1, i], recv_sem=recv_sems.at[k - 1, i],
                              device_id=(px, py, pc), device_id_type=pl.DeviceIdType.MESH)
                sends.append(pltpu.make_async_remote_copy(src_ref=src(i, peer), dst_ref=o_refs[i].at[me], **common))
                recvs.append(pltpu.make_async_remote_copy(src_ref=src(i, peer), dst_ref=o_refs[i].at[peer], **common))
        for cp in sends:
            cp.start()
        for cp in recvs:
            cp.wait_recv()
        for cp in sends:
            cp.wait_send()
        for cp in locals_:
            cp.wait()

    any_spec = pl.BlockSpec(memory_space=pl.ANY)
    out_shape = tuple(_sds(x.shape if scatter else (N_DEV,) + x.shape, x.dtype) for x in xs)
    return pl.pallas_call(
        body, name=name, out_shape=out_shape, in_specs=[any_spec] * n, out_specs=(any_spec,) * n,
        scratch_shapes=[pltpu.SemaphoreType.DMA((N_DEV - 1, n)), pltpu.SemaphoreType.DMA((N_DEV - 1, n)),
                        pltpu.SemaphoreType.DMA((n,))])(*xs)


def _to_rows(flat, row_mult):
    n = flat.shape[-1]
    per = row_mult * LANES
    pad = (-n) % per
    flat = jnp.pad(flat, [(0, 0)] * (flat.ndim - 1) + [(0, pad)])
    return flat.reshape(flat.shape[:-1] + ((n + pad) // LANES, LANES))


def _peers():
    mx, my, mc = lax.axis_index("x"), lax.axis_index("y"), lax.axis_index("c")
    out = []
    for k in range(1, N_DEV):
        px = 1 - mx if k & 4 else mx
        py = 1 - my if k & 2 else my
        pc = 1 - mc if k & 1 else mc
        out.append(((px, py, pc), 4 * px + 2 * py + pc))
    return 4 * mx + 2 * my + mc, out


_HBM = pl.BlockSpec(memory_space=pltpu.HBM)
_SEM = pl.BlockSpec(memory_space=pltpu.SEMAPHORE)


def _exchange_start(xs, *, scatter, name):
    n = len(xs)

    def body(*refs):
        x_refs, land_refs, send_sems, recv_sems, token = refs[:n], refs[n:2 * n], refs[2 * n], refs[2 * n + 1], refs[-1]
        me, peers = _peers()
        for k, (dev, peer) in enumerate(peers):
            for i in range(n):
                pltpu.make_async_remote_copy(
                    src_ref=x_refs[i].at[peer] if scatter else x_refs[i], dst_ref=land_refs[i].at[me],
                    send_sem=send_sems.at[k * n + i], recv_sem=recv_sems.at[k * n + i],
                    device_id=dev, device_id_type=pl.DeviceIdType.MESH).start()
        token[...] = jnp.zeros_like(token)

    land_shapes = [x.shape if scatter else (N_DEV,) + x.shape for x in xs]
    lands = [pltpu.with_memory_space_constraint(lax.empty(s, x.dtype), pltpu.HBM) for s, x in zip(land_shapes, xs)]
    srcs = [pltpu.with_memory_space_constraint(x, pltpu.HBM) for x in xs]
    out = pl.pallas_call(
        body, name=name,
        out_shape=(pltpu.SemaphoreType.DMA(((N_DEV - 1) * n,)), pltpu.SemaphoreType.DMA(((N_DEV - 1) * n,)),
                   *[pltpu.HBM(x.shape, x.dtype) for x in xs], *[pltpu.HBM(s, x.dtype) for s, x in zip(land_shapes, xs)],
                   _sds((8, LANES), F32)),
        in_specs=(_HBM,) * (2 * n), out_specs=(_SEM, _SEM) + (_HBM,) * (2 * n) + (pl.BlockSpec(memory_space=pltpu.VMEM),),
        input_output_aliases={i: 2 + i for i in range(2 * n)},
        compiler_params=pltpu.CompilerParams(has_side_effects=pltpu.SideEffectType.DATAFLOW_SIDE_EFFECTING))(*srcs, *lands)
    return (out[0], out[1], list(out[2:2 + n]), list(out[2 + n:2 + 2 * n])), out[-1]


def _exchange_wait(handle, after, *, scatter, name):
    send_sems, recv_sems, srcs, lands = handle
    n = len(srcs)

    def body(*refs):
        x_refs, land_refs, send_sems, recv_sems = refs[:n], refs[n:2 * n], refs[2 * n], refs[2 * n + 1]
        me, peers = _peers()
        for k, (dev, peer) in enumerate(peers):
            for i in range(n):
                cp = pltpu.make_async_remote_copy(
                    src_ref=x_refs[i].at[peer] if scatter else x_refs[i], dst_ref=land_refs[i].at[peer],
                    send_sem=send_sems.at[k * n + i], recv_sem=recv_sems.at[k * n + i],
                    device_id=dev, device_id_type=pl.DeviceIdType.MESH)
                cp.wait_send()
                cp.wait_recv()

    out = pl.pallas_call(
        body, name=name, out_shape=tuple(pltpu.HBM(a.shape, a.dtype) for a in srcs + lands),
        in_specs=(_HBM,) * (2 * n) + (_SEM, _SEM, pl.BlockSpec(memory_space=pl.ANY)), out_specs=(_HBM,) * (2 * n),
        input_output_aliases={i: i for i in range(2 * n)},
        compiler_params=pltpu.CompilerParams(has_side_effects=pltpu.SideEffectType.DATAFLOW_SIDE_EFFECTING))(
        *srcs, *lands, send_sems, recv_sems, after)
    return list(out[n:])


def _with_own_slot(lands, xs, *, scatter):
    me = 4 * lax.axis_index("x") + 2 * lax.axis_index("y") + lax.axis_index("c")
    out = []
    for land, x in zip(lands, xs):
        own = lax.dynamic_index_in_dim(x, me, 0, keepdims=True) if scatter else x[None]
        out.append(lax.dynamic_update_index_in_dim(land, own, me, 0))
    return out


def _wire_shards(p, l, names):
    return [p[k][l].astype(WIRE) if SHARDED[k][1] else p[k][l] for k in names]


def _full_weights(got, p, names):
    full = {}
    for k, g in zip(names, got):
        axis, shape = SHARDED[k][0] - 1, p[k].shape[1:]
        g = jnp.moveaxis(g, 0, axis)
        full[k] = g.reshape(shape[:axis] + (N_DEV * shape[axis],) + shape[axis + 1:])
    return full


def _grad_parts(grads, p, names):
    parts = []
    for k in names:
        axis, g = SHARDED[k][0] - 1, grads[k]
        g = g.reshape(g.shape[:axis] + (N_DEV, p[k].shape[1:][axis]) + g.shape[axis + 1:])
        parts.append(jnp.moveaxis(g, axis, 0).astype(WIRE))
    return parts


def _allreduce_small(vals):
    metas = [(k, v.shape, v.size) for k, v in vals.items()]
    flat = jnp.concatenate([v.reshape(-1) for v in vals.values()])
    got, = _exchange([_to_rows(flat, 8)], scatter=False, name="gather_small_grads")
    summed = _sum_slots(got, name="sum_small_grads").reshape(-1)
    out, off = {}, 0
    for k, shape, n in metas:
        out[k] = summed[off:off + n].reshape(shape)
        off += n
    return out


def _row(v):
    return v.reshape(1, -1).astype(F32)


def _lanes(v):
    return jnp.pad(v.astype(F32), (0, LANES - v.shape[0])).reshape(1, LANES)


EARLY = ('w_in', 'conv_dw_w', 'ssm_conv_w')
LATE = tuple(k for k in SHARDED if k not in EARLY)
FFN = ('w_ffn_up', 'ffn_dw_w', 'w_ffn_down')
NOT_FFN = tuple(k for k in SHARDED if k not in FFN)


def _layer_weights(p, full, l, dims):
    D, SX, XBC, H, FF = dims
    assert _tile(D, GLU_TILE) == D and _tile(FF, FFN_TILE) == FF
    W = {}
    if 'w_in' in full:
        w_in = full['w_in']
        o_dt = 2 * D + SX + XBC
        w_pieces = [w_in[:, :2 * D], w_in[:, 2 * D:2 * D + SX], w_in[:, 2 * D + SX:o_dt], w_in[:, o_dt + H:]]
        a_head = -jnp.exp(p['ssm_a_log'][l].astype(F32))
        hr = -(-H // 8) * 8
        W.update(
            w_main=jnp.concatenate(w_pieces, axis=1), w_pieces=w_pieces,
            w_dt=jnp.pad(w_in[:, o_dt:o_dt + H], ((0, 0), (0, LANES - H))),
            conv_w=full['conv_dw_w'], conv_b=_row(p['conv_dw_b'][l]), conv_g=_row(p['conv_ln_g'][l]),
            conv_beta=_row(p['conv_ln_b'][l]), ssm_w=full['ssm_conv_w'], ssm_b=_row(p['ssm_conv_b'][l]),
            dt_bias=_lanes(p['ssm_dt_bias'][l]), a_col=_lanes(a_head),
            a_row=jnp.pad(a_head, (0, hr - H)).reshape(hr, 1), d_col=_lanes(p['ssm_d'][l]),
            norm_w=_row(p['ssm_norm_w'][l]), ln1_g=_row(p['ln1_g'][l]), ln1_b=_row(p['ln1_b'][l]),
            ffn_b=_row(p['ffn_dw_b'][l]), ln2_g=_row(p['ln2_g'][l]), ln2_b=_row(p['ln2_b'][l]))
    if 'w_o' in full:
        W.update(w_conv_out=full['w_conv_out'], w_ssm_out=full['w_ssm_out'], w_o=full['w_o'], w_up=full['w_ffn_up'],
                 ffn_w=full['ffn_dw_w'], w_down=full['w_ffn_down'])
    return W


def _layer_fwd(h, hb, W, late, l, dims):
    D, SX, XBC, H, FF = dims
    o_z, o_xbc, o_gate = 2 * D, 2 * D + SX, 2 * D + SX + XBC
    hr = W['a_row'].shape[0]
    t = f"l{l}"
    u = _mm(hb, W['w_main'], mode='nn', out_dtype=BF16, name=t + "_in_proj")
    udt = _mm(hb, W['w_dt'], mode='nn', out_dtype=F32, name=t + "_dt_proj")
    v1, v3 = _conf_fwd(u, W['conv_w'], W['conv_b'], W['conv_g'], W['conv_beta'], D=D, name=t + "_conf_fwd")
    xbc = _conv_act_fwd(u, o_xbc, W['ssm_w'], W['ssm_b'], ffn=False, name=t + "_ssm_conv")
    dt = _dt_fwd(udt, W['dt_bias'], H=H, name=t + "_dt")
    dtT = jnp.pad(dt[:, :H].T, ((0, hr - H), (0, 0)))
    y, hst = _ssd_fwd(xbc, dt, dtT, W['a_col'], W['a_row'], W['d_col'], H=H, name=t + "_ssd_fwd")
    yn = _rms_fwd(y, u, o_z, W['norm_w'], name=t + "_rms_fwd")
    W = {**W, **late(yn)}
    ya = _mm(v3, W['w_conv_out'], mode='nn', out_dtype=F32, name=t + "_conv_out")
    yb = _mm(yn, W['w_ssm_out'], mode='nn', out_dtype=F32, name=t + "_ssm_out")
    m = _gate_fwd(u, o_gate, ya, yb, name=t + "_gate_fwd")
    h1, s1, h1b = _mm_res_ln(m, W['w_o'], h, W['ln1_g'], W['ln1_b'], name=t + "_mix_ln1")
    uf = _mm(h1b, W['w_up'], mode='nn', out_dtype=BF16, name=t + "_ffn_up")
    act = _conv_act_fwd(uf, 0, W['ffn_w'], W['ffn_b'], ffn=True, name=t + "_ffn_conv")
    h2, s2, h2b = _mm_res_ln(act, W['w_down'], h1, W['ln2_g'], W['ln2_b'], name=t + "_ffn_down_ln2")
    saved = dict(hb=hb, u=u, udt=udt, v1=v1, v3=v3, ya=ya, xbc=xbc, dt=dt, dtT=dtT, y=y, hst=hst, yn=yn, yb=yb, m=m,
                 h1b=h1b, s1=s1, uf=uf, act=act, s2=s2)
    return h2, h2b, saved, W


def _layer_bwd(dh2, W, S, l, dims, on_ffn_grads=None, on_all_grads=None):
    D, SX, XBC, H, FF = dims
    o_z, o_xbc, o_gate = 2 * D, 2 * D + SX, 2 * D + SX + XBC
    t = f"l{l}"
    g = {}
    ds2, g['ln2_g'], g['ln2_b'] = _ln_bwd(S['s2'], dh2, W['ln2_g'], W['ln2_b'], silu=False, name=t + "_ln2_bwd")
    g['w_down'] = _mm(S['act'], ds2, mode='tn', out_dtype=WIRE, name=t + "_dw_down")
    dact = _mm(ds2, W['w_down'], mode='nt', out_dtype=F32, name=t + "_dact")
    duf, g['ffn_w'], g['ffn_b'] = _conv_act_bwd(S['uf'], 0, W['ffn_w'], W['ffn_b'], dact, ffn=True, name=t + "_ffn_conv_bwd")
    g['w_up'] = _mm(S['h1b'], duf, mode='tn', out_dtype=WIRE, name=t + "_dw_up")
    dh1 = _mm(duf, W['w_up'], mode='nt', out_dtype=F32, res=ds2, res_scale=DN_ALPHA, name=t + "_dh1")
    ln1_g = W['ln1_g'] if on_ffn_grads is None else W['ln1_g'] + on_ffn_grads(g)[0:1, 0:1]
    ds1, g['ln1_g'], g['ln1_b'] = _ln_bwd(S['s1'], dh1, ln1_g, W['ln1_b'], silu=False, name=t + "_ln1_bwd")
    g['w_o'] = _mm(S['m'], ds1, mode='tn', out_dtype=WIRE, name=t + "_dw_o")
    dm = _mm(ds1, W['w_o'], mode='nt', out_dtype=F32, name=t + "_dm")
    dya, dyb, dgate = _gate_bwd(S['u'], o_gate, S['ya'], S['yb'], dm, name=t + "_gate_bwd")
    g['w_conv_out'] = _mm(S['v3'], dya, mode='tn', out_dtype=WIRE, name=t + "_dw_conv_out")
    dv3 = _mm(dya, W['w_conv_out'], mode='nt', out_dtype=F32, name=t + "_dv3")
    dv1, g['conv_g'], g['conv_beta'] = _ln_bwd(S['v1'], dv3, W['conv_g'], W['conv_beta'], silu=True, name=t + "_conv_ln_bwd")
    dglu, g['conv_w'], g['conv_b'] = _glu_conv_bwd(dv1, S['u'], W['conv_w'], name=t + "_conf_conv_bwd")
    g['w_ssm_out'] = _mm(S['yn'], dyb, mode='tn', out_dtype=WIRE, name=t + "_dw_ssm_out")
    dyn = _mm(dyb, W['w_ssm_out'], mode='nt', out_dtype=F32, name=t + "_dyn")
    dy, dz, g['norm_w'] = _rms_bwd(S['y'], S['u'], o_z, W['norm_w'], dyn, name=t + "_rms_bwd")
    dxbc_c, da, ddtx, g['d'] = _ssd_bwd(S['xbc'], S['dt'], S['dtT'], W['a_col'], W['a_row'], W['d_col'], S['hst'], dy,
                                        H=H, name=t + "_ssd_bwd")
    ddt_raw, g['dt_bias'], g['a_log'] = _dt_bwd(da, ddtx, S['dt'], S['udt'], W['dt_bias'], W['a_col'], H=H, name=t + "_dt_bwd")
    dxbc, g['ssm_w'], g['ssm_b'] = _conv_act_bwd(S['u'], o_xbc, W['ssm_w'], W['ssm_b'], dxbc_c, ffn=False, name=t + "_ssm_conv_bwd")
    du = [dglu, dz, dxbc, dgate]
    g['w_pieces'] = [_mm(S['hb'], d, mode='tn', out_dtype=WIRE, name=f"{t}_dw_in{i}") for i, d in enumerate(du)]
    g['w_dt'] = _mm(S['hb'], ddt_raw, mode='tn', out_dtype=WIRE, name=t + "_dw_dt")
    w_dt = W['w_dt'] if on_all_grads is None else W['w_dt'] + on_all_grads(g)[0:1, 0:1].astype(W['w_dt'].dtype)
    dh = _mm_nt_cat(du + [ddt_raw], W['w_pieces'] + [w_dt], ds1, DN_ALPHA, name=t + "_dh_in")
    return dh, g


def _layer_grads_to_params(g, dims):
    D, SX, XBC, H, FF = dims
    glu, dz, dxbc, dgate = g['w_pieces']
    w_in = jnp.concatenate([glu, dz, dxbc, g['w_dt'][:, :H], dgate], axis=1)
    return dict(
        w_in=w_in, conv_dw_w=g['conv_w'], conv_dw_b=g['conv_b'][0], conv_ln_g=g['conv_g'][0], conv_ln_b=g['conv_beta'][0],
        w_conv_out=g['w_conv_out'], ssm_conv_w=g['ssm_w'], ssm_conv_b=g['ssm_b'][0],
        ssm_dt_bias=g['dt_bias'][0, :H], ssm_a_log=g['a_log'][0, :H], ssm_d=g['d'][0, :H], ssm_norm_w=g['norm_w'][0],
        w_ssm_out=g['w_ssm_out'], w_o=g['w_o'], ln1_g=g['ln1_g'][0], ln1_b=g['ln1_b'][0],
        w_ffn_up=g['w_up'], ffn_dw_w=g['ffn_w'], ffn_dw_b=g['ffn_b'][0], w_ffn_down=g['w_down'], ln2_g=g['ln2_g'][0], ln2_b=g['ln2_b'][0])


def kernel(x, ln_in_g, ln_in_b, w_in, conv_dw_w, conv_dw_b, conv_ln_g, conv_ln_b, w_conv_out, ssm_conv_w, ssm_conv_b, ssm_dt_bias, ssm_a_log, ssm_d, ssm_norm_w, w_ssm_out, w_o, ln1_g, ln1_b, w_ffn_up, ffn_dw_w, ffn_dw_b, w_ffn_down, ln2_g, ln2_b, loss_target, m_ln_in_g, m_ln_in_b, m_w_in, m_conv_dw_w, m_conv_dw_b, m_conv_ln_g, m_conv_ln_b, m_w_conv_out, m_ssm_conv_w, m_ssm_conv_b, m_ssm_dt_bias, m_ssm_a_log, m_ssm_d, m_ssm_norm_w, m_w_ssm_out, m_w_o, m_ln1_g, m_ln1_b, m_w_ffn_up, m_ffn_dw_w, m_ffn_dw_b, m_w_ffn_down, m_ln2_g, m_ln2_b, v_ln_in_g, v_ln_in_b, v_w_in, v_conv_dw_w, v_conv_dw_b, v_conv_ln_g, v_conv_ln_b, v_w_conv_out, v_ssm_conv_w, v_ssm_conv_b, v_ssm_dt_bias, v_ssm_a_log, v_ssm_d, v_ssm_norm_w, v_w_ssm_out, v_w_o, v_ln1_g, v_ln1_b, v_w_ffn_up, v_ffn_dw_w, v_ffn_dw_b, v_w_ffn_down, v_ln2_g, v_ln2_b):
    weights = (ln_in_g, ln_in_b, w_in, conv_dw_w, conv_dw_b, conv_ln_g, conv_ln_b, w_conv_out, ssm_conv_w, ssm_conv_b,
               ssm_dt_bias, ssm_a_log, ssm_d, ssm_norm_w, w_ssm_out, w_o, ln1_g, ln1_b, w_ffn_up, ffn_dw_w, ffn_dw_b,
               w_ffn_down, ln2_g, ln2_b)
    moments_m = (m_ln_in_g, m_ln_in_b, m_w_in, m_conv_dw_w, m_conv_dw_b, m_conv_ln_g, m_conv_ln_b, m_w_conv_out,
                 m_ssm_conv_w, m_ssm_conv_b, m_ssm_dt_bias, m_ssm_a_log, m_ssm_d, m_ssm_norm_w, m_w_ssm_out, m_w_o,
                 m_ln1_g, m_ln1_b, m_w_ffn_up, m_ffn_dw_w, m_ffn_dw_b, m_w_ffn_down, m_ln2_g, m_ln2_b)
    moments_v = (v_ln_in_g, v_ln_in_b, v_w_in, v_conv_dw_w, v_conv_dw_b, v_conv_ln_g, v_conv_ln_b, v_w_conv_out,
                 v_ssm_conv_w, v_ssm_conv_b, v_ssm_dt_bias, v_ssm_a_log, v_ssm_d, v_ssm_norm_w, v_w_ssm_out, v_w_o,
                 v_ln1_g, v_ln1_b, v_w_ffn_up, v_ffn_dw_w, v_ffn_dw_b, v_w_ffn_down, v_ln2_g, v_ln2_b)
    p = dict(zip(PARAMS, weights))
    pm = dict(zip(PARAMS, moments_m))
    pv = dict(zip(PARAMS, moments_v))

    T, D = x.shape[1], x.shape[2]
    SX = w_ssm_out.shape[1] * N_DEV
    XBC = ssm_conv_b.shape[-1]
    H = ssm_d.shape[-1]
    FF = ffn_dw_b.shape[-1] // 2
    dims = (D, SX, XBC, H, FF)
    depth = w_in.shape[0]

    ALL = tuple(SHARDED)
    got = _exchange(_wire_shards(p, 0, EARLY), scatter=False, name="gather_l0_first")
    rest_shards, got = lax.optimization_barrier((_wire_shards(p, 0, LATE), got))
    rest_handle, tok = _exchange_start(rest_shards, scatter=False, name="gather_l0_rest_start")
    first = _layer_weights(p, _full_weights(got, p, EARLY), 0, dims)
    in_flight = {}

    def start_next_gather(l, behind):
        shards, _ = lax.optimization_barrier((_wire_shards(p, l, ALL), behind))
        in_flight[l], t = _exchange_start(shards, scatter=False, name=f"gather_l{l}_start")
        in_flight[l] = (in_flight[l], shards)
        return t

    def late_weights(l):
        def late(x):
            if l > 0:
                return {}
            got = _with_own_slot(_exchange_wait(rest_handle, x, scatter=False, name="gather_l0_rest_wait"), rest_shards, scatter=False)
            W = _layer_weights(p, _full_weights(got, p, LATE), 0, dims)
            if depth > 1:
                t = start_next_gather(1, got)
                W['w_conv_out'] = W['w_conv_out'] + t[0:1, 0:1].astype(W['w_conv_out'].dtype)
            return W
        return late

    xs = x.reshape(T, D)
    h, hb = _ln_fwd(xs, _row(ln_in_g) + tok[0:1, 0:1], _row(ln_in_b), name="ln_in_fwd")
    saved, layers = [], []
    for l in range(depth):
        if l > 0:
            handle, shards = in_flight.pop(l)
            got = _with_own_slot(_exchange_wait(handle, h, scatter=False, name=f"gather_l{l}_wait"), shards, scatter=False)
            first = _layer_weights(p, _full_weights(got, p, ALL), l, dims)
            if l + 1 < depth:
                t = start_next_gather(l + 1, got)
                first['w_dt'] = first['w_dt'] + t[0:1, 0:1].astype(first['w_dt'].dtype)
        h, hb, s, W = _layer_fwd(h, hb, first, late_weights(l), l, dims)
        saved.append(s)
        layers.append(W)
    loss_part, dh = _loss_fwd_bwd(h, loss_target.reshape(T, D), name="loss")

    layer_grads = [None] * depth
    flying, arrived = [], {}

    def wait_flying(after):
        while flying:
            l_, names, handle, parts = flying.pop(0)
            lands = _with_own_slot(_exchange_wait(handle, after, scatter=True, name=f"scatter_l{l_}_{names[0]}_wait"), parts, scatter=True)
            arrived.update({(l_, k): a for k, a in zip(names, lands)})
            after = lands[0]
        return after

    def start_scatter(l, names, grads_now, after):
        parts, _ = lax.optimization_barrier((_grad_parts(grads_now, p, names), wait_flying(after)))
        handle, t = _exchange_start(parts, scatter=True, name=f"scatter_l{l}_{names[0]}_start")
        flying.append((l, names, handle, parts))
        return t

    def on_ffn_grads(g):
        return start_scatter(0, FFN, dict(w_ffn_up=g['w_up'], ffn_dw_w=g['ffn_w'], w_ffn_down=g['w_down']), g['w_up'])

    def on_all_grads(g):
        layer_grads[0] = _layer_grads_to_params(g, dims)
        return start_scatter(0, NOT_FFN, layer_grads[0], g['w_dt'])

    tok = jnp.zeros((8, LANES), F32)
    for l in reversed(range(depth)):
        W = dict(layers[l], ln2_g=layers[l]['ln2_g'] + tok[0:1, 0:1])
        if l > 0:
            dh, g = _layer_bwd(dh, W, saved[l], l, dims)
            layer_grads[l] = _layer_grads_to_params(g, dims)
            tok = start_scatter(l, ALL, layer_grads[l], dh)
        else:
            dh, g = _layer_bwd(dh, W, saved[l], l, dims, on_ffn_grads, on_all_grads)
    grad_x, dg_in, db_in = _ln_bwd(xs, dh, _row(ln_in_g), _row(ln_in_b), silu=False, name="ln_in_bwd")
    wait_flying(grad_x)

    small = {k: jnp.stack([layer_grads[l][k] for l in range(depth)]) for k in layer_grads[0] if k not in SHARDED}
    small['ln_in_g'], small['ln_in_b'], small['loss'] = dg_in[0], db_in[0], loss_part
    grads = _allreduce_small(small)
    loss = grads.pop('loss').reshape(())

    delta, new_m, new_v = {}, {}, {}
    for k in SHARDED:
        per_layer = [_sum_adamw(arrived[(l, k)], p[k], pm[k], pv[k], l, name=f"sum_adamw_l{l}_{k}") for l in range(depth)]
        grads[k], delta[k], new_m[k], new_v[k] = (jnp.stack([r[j] for r in per_layer]) for j in range(4))
    rest = [k for k in PARAMS if k not in SHARDED]
    flat = lambda d: _to_rows(jnp.concatenate([d[k].reshape(-1) for k in rest]), 8)
    d_, m_, v_ = _adamw(flat(p), flat(grads), flat(pm), flat(pv), name="adamw_small")
    off = 0
    for k in rest:
        n, shp = p[k].size, p[k].shape
        delta[k] = d_.reshape(-1)[off:off + n].reshape(shp)
        new_m[k] = m_.reshape(-1)[off:off + n].reshape(shp)
        new_v[k] = v_.reshape(-1)[off:off + n].reshape(shp)
        off += n

    return (loss, grad_x.reshape(x.shape), *[grads[k] for k in PARAMS], *[delta[k] for k in PARAMS],
            *[new_m[k] for k in PARAMS], *[new_v[k] for k in PARAMS])
```

```python
import math

import jax
import jax.numpy as jnp
from jax import lax
from jax.experimental import pallas as pl
from jax.experimental.pallas import tpu as pltpu

F32 = jnp.float32
BF16 = jnp.bfloat16
WIRE = jnp.bfloat16
HIGHEST = lax.Precision.HIGHEST

DEPTH = 2
SSM_STATE = 128
SSM_CHUNK = 128
SSM_GROUPS = 4
SSM_HEAD_DIM = 64
DN_ALPHA = (2 * DEPTH) ** 0.25
LN_EPS = 1e-5
RMS_EPS = 1e-5
ADAM_LR = 0.001
ADAM_B1 = 0.9
ADAM_B2 = 0.999
ADAM_EPS = 1e-08
ADAM_WD = 0.01
ADAM_STEP = 10

N_DEV = 8
LANES = 128
VMEM_LIMIT = 48 * 1024 * 1024
WHOLE_K_VMEM = 40 * 1024 * 1024
GLU_TILE = 1024
FFN_TILE = 2816
CONV_ROWS = 256

PARAMS = ['ln_in_g', 'ln_in_b', 'w_in', 'conv_dw_w', 'conv_dw_b', 'conv_ln_g', 'conv_ln_b', 'w_conv_out',
          'ssm_conv_w', 'ssm_conv_b', 'ssm_dt_bias', 'ssm_a_log', 'ssm_d', 'ssm_norm_w', 'w_ssm_out', 'w_o',
          'ln1_g', 'ln1_b', 'w_ffn_up', 'ffn_dw_w', 'ffn_dw_b', 'w_ffn_down', 'ln2_g', 'ln2_b']
SHARDED = {'w_in': (2, True), 'conv_dw_w': (2, False), 'w_conv_out': (1, True), 'ssm_conv_w': (2, False),
           'w_ssm_out': (1, True), 'w_o': (1, True), 'w_ffn_up': (2, True), 'ffn_dw_w': (2, False),
           'w_ffn_down': (1, True)}


def _sds(shape, dtype):
    return jax.ShapeDtypeStruct(tuple(shape), dtype)


def _tile(dim, pref):
    return pref if dim % pref == 0 else dim


def _pick(dim, pref, mult):
    best = None
    for t in range(mult, min(dim, pref) + 1, mult):
        if dim % t == 0:
            best = t
    return best or dim


def _cp(*sem):
    return pltpu.CompilerParams(dimension_semantics=sem, vmem_limit_bytes=VMEM_LIMIT)


def _f32(x):
    return x.astype(F32)


def _sigmoid(x):
    return 1.0 / (1.0 + jnp.exp(-x))


def _silu_and_grad(x):
    s = _sigmoid(x)
    return x * s, s * (1.0 + x * (1.0 - s))


def _ln_stats(s):
    mu = jnp.mean(s, axis=-1, keepdims=True)
    xc = s - mu
    var = jnp.mean(xc * xc, axis=-1, keepdims=True)
    rstd = lax.rsqrt(var + LN_EPS)
    return xc * rstd, rstd


def _dot(a, b, dims):
    return lax.dot_general(a, b, (dims, ((), ())), preferred_element_type=F32)


_NN = ((1,), (0,))
_NT = ((1,), (1,))
_TN = ((0,), (0,))


def _mm(a, b, *, mode, out_dtype, name, res=None, res_scale=1.0, tm=1024, tn=1408, tk=1408):
    if mode == 'nn':
        (M, K), (_, N) = a.shape, b.shape
    elif mode == 'nt':
        (M, K), (N, _) = a.shape, b.shape
    else:
        (K, M), (_, N) = a.shape, b.shape
        tm, tk = 1408, 1024
    tm, tn, tk = _pick(M, tm, 8), _pick(N, tn, LANES), _pick(K, tk, LANES)
    if mode != 'tn':
        for rows in (tm, _pick(M, tm // 2, 8), _pick(M, tm // 4, 8)):
            blocks = rows * K * a.dtype.itemsize + tn * K * b.dtype.itemsize + rows * tn * (4 + (4 if res is not None else 0))
            if 2 * blocks <= WHOLE_K_VMEM:
                tm, tk = rows, K
                break
    nk = K // tk
    dims = {'nn': _NN, 'nt': _NT, 'tn': _TN}[mode]

    def body(*refs):
        if res is None:
            a_ref, b_ref, o_ref = refs[:3]
            r_ref = None
        else:
            a_ref, b_ref, r_ref, o_ref = refs[:4]
        p = _dot(a_ref[...].astype(BF16), b_ref[...].astype(BF16), dims)

        def finish(acc):
            if r_ref is not None:
                acc = acc + res_scale * r_ref[...]
            o_ref[...] = acc.astype(out_dtype)

        if nk == 1:
            finish(p)
        else:
            acc_ref = refs[-1]
            k = pl.program_id(2)

            @pl.when(k == 0)
            def _():
                acc_ref[...] = p

            @pl.when(k > 0)
            def _():
                acc_ref[...] += p

            @pl.when(k == nk - 1)
            def _():
                finish(acc_ref[...])

    if mode == 'nn':
        a_spec = pl.BlockSpec((tm, tk), lambda i, j, k: (i, k))
        b_spec = pl.BlockSpec((tk, tn), lambda i, j, k: (k, j))
    elif mode == 'nt':
        a_spec = pl.BlockSpec((tm, tk), lambda i, j, k: (i, k))
        b_spec = pl.BlockSpec((tn, tk), lambda i, j, k: (j, k))
    else:
        a_spec = pl.BlockSpec((tk, tm), lambda i, j, k: (k, i))
        b_spec = pl.BlockSpec((tk, tn), lambda i, j, k: (k, j))
    o_spec = pl.BlockSpec((tm, tn), lambda i, j, k: (i, j))
    in_specs = [a_spec, b_spec] + ([o_spec] if res is not None else [])
    args = (a, b) + ((res,) if res is not None else ())
    return pl.pallas_call(
        body, name=name, out_shape=_sds((M, N), out_dtype), grid=(M // tm, N // tn, nk),
        in_specs=in_specs, out_specs=o_spec,
        scratch_shapes=[pltpu.VMEM((tm, tn), F32)] if nk > 1 else [],
        compiler_params=_cp("parallel", "parallel", "arbitrary"))(*args)


def _mm_nt_cat(a_list, b_list, res, res_scale, *, name, tm=512, tk=1024):
    M, N = a_list[0].shape[0], b_list[0].shape[0]
    tm = _pick(M, tm, 8)
    tks = [_pick(a.shape[1], tk, LANES) for a in a_list]
    steps = [a.shape[1] // t for a, t in zip(a_list, tks)]
    offs = [sum(steps[:i]) for i in range(len(steps))]
    nk, n = sum(steps), len(a_list)

    def body(*refs):
        a_refs, b_refs, r_ref, o_ref, acc_ref = refs[:n], refs[n:2 * n], refs[2 * n], refs[2 * n + 1], refs[2 * n + 2]
        k = pl.program_id(1)

        @pl.when(k == 0)
        def _():
            acc_ref[...] = res_scale * r_ref[...]

        for i in range(n):
            @pl.when((k >= offs[i]) & (k < offs[i] + steps[i]))
            def _(i=i):
                acc_ref[...] += _dot(a_refs[i][...].astype(BF16), b_refs[i][...].astype(BF16), _NT)

        @pl.when(k == nk - 1)
        def _():
            o_ref[...] = acc_ref[...]

    def kmap(i):
        return lambda m, k: jnp.clip(k - offs[i], 0, steps[i] - 1)

    a_specs = [pl.BlockSpec((tm, tks[i]), lambda m, k, i=i: (m, kmap(i)(m, k))) for i in range(n)]
    b_specs = [pl.BlockSpec((N, tks[i]), lambda m, k, i=i: (0, kmap(i)(m, k))) for i in range(n)]
    row = pl.BlockSpec((tm, N), lambda m, k: (m, 0))
    return pl.pallas_call(
        body, name=name, out_shape=_sds((M, N), F32), grid=(M // tm, nk), in_specs=a_specs + b_specs + [row],
        out_specs=row, scratch_shapes=[pltpu.VMEM((tm, N), F32)],
        compiler_params=_cp("parallel", "arbitrary"))(*a_list, *b_list, res)


def _mm_res_ln(a, w, res, g, b, *, name, tm=512, tk=1408):
    (M, K), (_, N) = a.shape, w.shape
    tm, tk = _pick(M, tm, 8), _pick(K, tk, LANES)
    if 2 * (tm * K * a.dtype.itemsize + K * N * w.dtype.itemsize + tm * N * 14) <= WHOLE_K_VMEM:
        tk = K
    nk = K // tk

    def body(a_ref, w_ref, r_ref, g_ref, b_ref, h_ref, s_ref, hb_ref, acc_ref):
        k = pl.program_id(1)
        p = _dot(a_ref[...].astype(BF16), w_ref[...].astype(BF16), _NN)

        def finish(acc):
            s = DN_ALPHA * r_ref[...] + acc
            xhat, _ = _ln_stats(s)
            s_ref[...] = s
            h = xhat * g_ref[...] + b_ref[...]
            h_ref[...] = h
            hb_ref[...] = h.astype(BF16)

        if nk == 1:
            finish(p)
            return

        @pl.when(k == 0)
        def _():
            acc_ref[...] = p

        @pl.when(k > 0)
        def _():
            acc_ref[...] += p

        @pl.when(k == nk - 1)
        def _():
            finish(acc_ref[...])

    row = pl.BlockSpec((tm, N), lambda i, k: (i, 0))
    vec = pl.BlockSpec((1, N), lambda i, k: (0, 0))
    return pl.pallas_call(
        body, name=name, out_shape=(_sds((M, N), F32), _sds((M, N), F32), _sds((M, N), BF16)), grid=(M // tm, nk),
        in_specs=[pl.BlockSpec((tm, tk), lambda i, k: (i, k)), pl.BlockSpec((tk, N), lambda i, k: (k, 0)), row, vec, vec],
        out_specs=(row, row, row), scratch_shapes=[pltpu.VMEM((tm, N), F32)],
        compiler_params=_cp("parallel", "arbitrary"))(a, w, res, g, b)


def _ln_fwd(x, g, b, *, name):
    T, D = x.shape
    tt = _tile(T, 512)

    def body(x_ref, g_ref, b_ref, o_ref, ob_ref):
        xhat, _ = _ln_stats(x_ref[...])
        h = xhat * g_ref[...] + b_ref[...]
        o_ref[...] = h
        ob_ref[...] = h.astype(BF16)

    row = pl.BlockSpec((tt, D), lambda i: (i, 0))
    vec = pl.BlockSpec((1, D), lambda i: (0, 0))
    return pl.pallas_call(body, name=name, out_shape=(_sds((T, D), F32), _sds((T, D), BF16)), grid=(T // tt,),
                          in_specs=[row, vec, vec], out_specs=(row, row), compiler_params=_cp("parallel"))(x, g, b)


def _ln_bwd(s, dy, g, b, *, silu, name):
    T, D = s.shape
    tt = _tile(T, 512)

    def body(s_ref, dy_ref, g_ref, b_ref, ds_ref, dg_ref, db_ref):
        i = pl.program_id(0)
        xhat, rstd = _ln_stats(s_ref[...])
        gg = g_ref[...]
        dyl = dy_ref[...]
        if silu:
            _, dsilu = _silu_and_grad(xhat * gg + b_ref[...])
            dyl = dyl * dsilu
        dxh = dyl * gg
        m1 = jnp.mean(dxh, axis=-1, keepdims=True)
        m2 = jnp.mean(dxh * xhat, axis=-1, keepdims=True)
        ds_ref[...] = rstd * (dxh - m1 - xhat * m2)

        @pl.when(i == 0)
        def _():
            dg_ref[...] = jnp.zeros_like(dg_ref)
            db_ref[...] = jnp.zeros_like(db_ref)

        dg_ref[...] += jnp.sum(dyl * xhat, axis=0, keepdims=True)
        db_ref[...] += jnp.sum(dyl, axis=0, keepdims=True)

    row = pl.BlockSpec((tt, D), lambda i: (i, 0))
    vec = pl.BlockSpec((1, D), lambda i: (0, 0))
    return pl.pallas_call(body, name=name, out_shape=(_sds((T, D), F32), _sds((1, D), F32), _sds((1, D), F32)),
                          grid=(T // tt,), in_specs=[row, row, vec, vec], out_specs=(row, vec, vec),
                          compiler_params=_cp("arbitrary"))(s, dy, g, b)


def _loss_fwd_bwd(h, tgt, *, name):
    T, D = h.shape
    tt = _tile(T, 512)

    def body(h_ref, t_ref, dh_ref, l_ref):
        i = pl.program_id(0)
        e = h_ref[...] - t_ref[...]
        dh_ref[...] = e * (1.0 / D)

        @pl.when(i == 0)
        def _():
            l_ref[...] = jnp.zeros_like(l_ref)

        part = jnp.sum(jnp.sum(e * e, axis=1, keepdims=True), axis=0, keepdims=True) * (0.5 / D)
        l_ref[...] += jnp.broadcast_to(part, l_ref.shape)

    row = pl.BlockSpec((tt, D), lambda i: (i, 0))
    one = pl.BlockSpec((8, LANES), lambda i: (0, 0))
    dh, l = pl.pallas_call(body, name=name, out_shape=(_sds((T, D), F32), _sds((8, LANES), F32)), grid=(T // tt,),
                           in_specs=[row, row], out_specs=(row, one), compiler_params=_cp("arbitrary"))(h, tgt)
    return l[0:1, 0:1], dh


def _halo_rows(k):
    return 32 if k > 17 else 16


def _shifted(ext, K, first):
    rolled = {0: ext}
    out = []
    for k in range(K):
        r = (first + k) % 8
        if r not in rolled:
            rolled[r] = pltpu.roll(ext, ext.shape[0] - r, 0)
        out.append((rolled[r], first + k - r))
    return out


def _taps(ext, w_ref, sl, K, first, n, reverse=False):
    acc = None
    for k, (z, base) in enumerate(_shifted(ext, K, first)):
        kw = K - 1 - k if reverse else k
        term = w_ref[kw:kw + 1, sl] * z[base:base + n]
        acc = term if acc is None else acc + term
    return acc


def _tap_sums(d, ext, dw_ref, sl, K, first):
    n = d.shape[0]
    for k, (z, base) in enumerate(_shifted(ext, K, first)):
        dw_ref[k:k + 1, sl] += jnp.sum(d * z[base:base + n], axis=0, keepdims=True)


def _strip_width(width):
    return LANES if width % LANES == 0 else width


def _for_strips(width, fn):
    sw = _strip_width(width)

    def step(s, carry):
        fn(pl.ds(pl.multiple_of(s * sw, sw), sw), s)
        return carry

    lax.fori_loop(0, width // sw, step, 0)


def _prev_rows(tt, hb):
    return lambda t: jnp.maximum(t * (tt // hb) - 1, 0)


def _next_rows(tt, hb, T):
    return lambda t: jnp.minimum((t + 1) * (tt // hb), T // hb - 1)


def _conf_fwd(u, w, b, g, beta, *, D, name):
    T = u.shape[0]
    K = w.shape[0]
    tt, hb, tc = _tile(T, CONV_ROWS), _halo_rows(K), _tile(D, GLU_TILE)
    sw = _strip_width(tc)
    per_tile = tc // sw
    prev = _prev_rows(tt, hb)

    def body(u_ref, uh_ref, w_ref, b_ref, g_ref, beta_ref, v1_ref, v3_ref):
        i = pl.program_id(0)

        def strip(sl, s):
            a0 = (s // per_tile) * (2 * tc) + (s % per_tile) * sw
            a_sl, g_sl = pl.ds(pl.multiple_of(a0, sw), sw), pl.ds(pl.multiple_of(a0 + tc, sw), sw)
            halo = jnp.where(i > 0, _f32(uh_ref[:, a_sl]) * _sigmoid(_f32(uh_ref[:, g_sl])), 0.0)
            ext = jnp.concatenate([halo, _f32(u_ref[:, a_sl]) * _sigmoid(_f32(u_ref[:, g_sl]))], axis=0)
            v1_ref[:, sl] = _taps(ext, w_ref, sl, K, hb - (K - 1), tt) + b_ref[:, sl]

        _for_strips(D, strip)
        xhat, _ = _ln_stats(v1_ref[...])
        v2 = xhat * g_ref[...] + beta_ref[...]
        v3_ref[...] = (v2 * _sigmoid(v2)).astype(BF16)

    row = pl.BlockSpec((tt, D), lambda t: (t, 0))
    vec = pl.BlockSpec((1, D), lambda t: (0, 0))
    return pl.pallas_call(
        body, name=name, out_shape=(_sds((T, D), F32), _sds((T, D), BF16)), grid=(T // tt,),
        in_specs=[pl.BlockSpec((tt, 2 * D), lambda t: (t, 0)), pl.BlockSpec((hb, 2 * D), lambda t: (prev(t), 0)),
                  pl.BlockSpec((K, D), lambda t: (0, 0)), vec, vec, vec],
        out_specs=(row, row), compiler_params=_cp("parallel"))(u, u, w, b, g, beta)


def _conv_act_fwd(x, off, w, b, *, ffn, name):
    T = x.shape[0]
    K, Cw = w.shape
    tt, hb = _tile(T, CONV_ROWS), _halo_rows(K)
    tc = _tile(Cw // 2, FFN_TILE) if ffn else _pick(math.gcd(Cw, off), 1536, LANES)
    xw = 2 * tc if ffn else tc
    assert off % xw == 0
    ob = off // xw
    prev = _prev_rows(tt, hb)

    def body(x_ref, xh_ref, w_ref, b_ref, o_ref):
        first = pl.program_id(0) == 0

        def pre_of(sl):
            ext = jnp.concatenate([jnp.where(first, 0.0, _f32(xh_ref[:, sl])), _f32(x_ref[:, sl])], axis=0)
            return _taps(ext, w_ref, sl, K, hb - (K - 1), tt) + b_ref[:, sl]

        def strip(sl, s):
            if ffn:
                gate = pre_of(sl)
                val = pre_of(pl.ds(pl.multiple_of(tc + s * sw, sw), sw))
                o_ref[:, sl] = (gate * _sigmoid(gate) * val).astype(BF16)
            else:
                pre = pre_of(sl)
                o_ref[:, sl] = pre * _sigmoid(pre)

        _for_strips(tc, strip)

    sw = _strip_width(tc)
    return pl.pallas_call(
        body, name=name, out_shape=_sds((T, Cw // 2), BF16) if ffn else _sds((T, Cw), F32), grid=(T // tt, Cw // xw),
        in_specs=[pl.BlockSpec((tt, xw), lambda t, c: (t, c + ob)), pl.BlockSpec((hb, xw), lambda t, c: (prev(t), c + ob)),
                  pl.BlockSpec((K, xw), lambda t, c: (0, c)), pl.BlockSpec((1, xw), lambda t, c: (0, c))],
        out_specs=pl.BlockSpec((tt, tc), lambda t, c: (t, c)),
        compiler_params=_cp("parallel", "parallel"))(x, x, w, b)


def _conv_act_bwd(x, off, w, b, dout, *, ffn, name):
    T = x.shape[0]
    K, Cw = w.shape
    tt, hb = _tile(T, CONV_ROWS), _halo_rows(K)
    tc = _tile(Cw // 2, FFN_TILE) if ffn else _pick(math.gcd(Cw, off), 1536, LANES)
    xw = 2 * tc if ffn else tc
    assert off % xw == 0
    ob = off // xw
    nt = T // tt
    prev, nxt = _prev_rows(tt, hb), _next_rows(tt, hb, T)

    sw = _strip_width(tc)

    def body(x_ref, xp_ref, xn_ref, d_ref, dn_ref, w_ref, b_ref, dx_ref, dw_ref, db_ref):
        t = pl.program_id(1)

        @pl.when(t == 0)
        def _():
            dw_ref[...] = jnp.zeros_like(dw_ref)
            db_ref[...] = jnp.zeros_like(db_ref)

        def ext_of(sl):
            return jnp.concatenate([jnp.where(t == 0, 0.0, _f32(xp_ref[:, sl])), _f32(x_ref[:, sl]), _f32(xn_ref[:, sl])], axis=0)

        def pre_of(ext, sl):
            return _taps(ext, w_ref, sl, K, hb - (K - 1), tt + hb) + b_ref[:, sl]

        def finish(ext, dpre, sl):
            dx_ref[:, sl] = _taps(dpre, w_ref, sl, K, 0, tt, reverse=True).astype(BF16)
            dp = dpre[0:tt]
            _tap_sums(dp, ext, dw_ref, sl, K, hb - (K - 1))
            db_ref[:, sl] += jnp.sum(dp, axis=0, keepdims=True)

        def strip(sl, s):
            d = jnp.concatenate([d_ref[:, sl], jnp.where(t == nt - 1, 0.0, dn_ref[:, sl])], axis=0)
            if ffn:
                vsl = pl.ds(pl.multiple_of(tc + s * sw, sw), sw)
                eg, ev = ext_of(sl), ext_of(vsl)
                sg, dsg = _silu_and_grad(pre_of(eg, sl))
                val = pre_of(ev, vsl)
                finish(eg, d * val * dsg, sl)
                finish(ev, d * sg, vsl)
            else:
                ext = ext_of(sl)
                finish(ext, d * _silu_and_grad(pre_of(ext, sl))[1], sl)

        _for_strips(tc, strip)

    return pl.pallas_call(
        body, name=name, out_shape=(_sds((T, Cw), BF16), _sds((K, Cw), F32), _sds((1, Cw), F32)),
        grid=(Cw // xw, nt),
        in_specs=[pl.BlockSpec((tt, xw), lambda c, t: (t, c + ob)), pl.BlockSpec((hb, xw), lambda c, t: (prev(t), c + ob)),
                  pl.BlockSpec((hb, xw), lambda c, t: (nxt(t), c + ob)),
                  pl.BlockSpec((tt, tc), lambda c, t: (t, c)), pl.BlockSpec((hb, tc), lambda c, t: (nxt(t), c)),
                  pl.BlockSpec((K, xw), lambda c, t: (0, c)), pl.BlockSpec((1, xw), lambda c, t: (0, c))],
        out_specs=(pl.BlockSpec((tt, xw), lambda c, t: (t, c)), pl.BlockSpec((K, xw), lambda c, t: (0, c)),
                   pl.BlockSpec((1, xw), lambda c, t: (0, c))),
        compiler_params=_cp("parallel", "arbitrary"))(x, x, x, dout, dout, w, b)


def _glu_conv_bwd(dpre, u, w, *, name):
    T, C = dpre.shape
    K = w.shape[0]
    tt, hb, tc = _tile(T, CONV_ROWS), _halo_rows(K), _tile(C, GLU_TILE)
    nt = T // tt
    prev, nxt = _prev_rows(tt, hb), _next_rows(tt, hb, T)

    sw = _strip_width(tc)

    def body(d_ref, dn_ref, x_ref, xh_ref, w_ref, dx_ref, dw_ref, db_ref):
        t = pl.program_id(1)

        @pl.when(t == 0)
        def _():
            dw_ref[...] = jnp.zeros_like(dw_ref)
            db_ref[...] = jnp.zeros_like(db_ref)

        def strip(sl, s):
            gsl = pl.ds(pl.multiple_of(tc + s * sw, sw), sw)
            d = d_ref[:, sl]
            dext = jnp.concatenate([d, jnp.where(t == nt - 1, 0.0, dn_ref[:, sl])], axis=0)
            dxin = _taps(dext, w_ref, sl, K, 0, tt, reverse=True)
            a, sg = _f32(x_ref[:, sl]), _sigmoid(_f32(x_ref[:, gsl]))
            v0 = a * sg
            dx_ref[:, sl] = (dxin * sg).astype(BF16)
            dx_ref[:, gsl] = (dxin * v0 * (1.0 - sg)).astype(BF16)
            halo = jnp.where(t == 0, 0.0, _f32(xh_ref[:, sl]) * _sigmoid(_f32(xh_ref[:, gsl])))
            _tap_sums(d, jnp.concatenate([halo, v0], axis=0), dw_ref, sl, K, hb - (K - 1))
            db_ref[:, sl] += jnp.sum(d, axis=0, keepdims=True)

        _for_strips(tc, strip)

    return pl.pallas_call(
        body, name=name, out_shape=(_sds((T, 2 * C), BF16), _sds((K, C), F32), _sds((1, C), F32)), grid=(C // tc, nt),
        in_specs=[pl.BlockSpec((tt, tc), lambda c, t: (t, c)), pl.BlockSpec((hb, tc), lambda c, t: (nxt(t), c)),
                  pl.BlockSpec((tt, 2 * tc), lambda c, t: (t, c)), pl.BlockSpec((hb, 2 * tc), lambda c, t: (prev(t), c)),
                  pl.BlockSpec((K, tc), lambda c, t: (0, c))],
        out_specs=(pl.BlockSpec((tt, 2 * tc), lambda c, t: (t, c)), pl.BlockSpec((K, tc), lambda c, t: (0, c)),
                   pl.BlockSpec((1, tc), lambda c, t: (0, c))),
        compiler_params=_cp("parallel", "arbitrary"))(dpre, dpre, u, u, w)


def _softplus(x):
    t = jnp.exp(-jnp.abs(x))
    u = 1.0 + t
    log1p = jnp.where(u == 1.0, t, jnp.log(u) * t / jnp.where(u == 1.0, 1.0, u - 1.0))
    return jnp.maximum(x, 0.0) + log1p


def _dt_fwd(udt, bias, *, H, name):
    T = udt.shape[0]
    tt = _tile(T, 1024)

    def body(u_ref, b_ref, o_ref):
        lane = lax.broadcasted_iota(jnp.int32, (tt, LANES), 1)
        o_ref[...] = jnp.where(lane < H, _softplus(u_ref[...] + b_ref[...]), 0.0)

    row = pl.BlockSpec((tt, LANES), lambda i: (i, 0))
    return pl.pallas_call(body, name=name, out_shape=_sds((T, LANES), F32), grid=(T // tt,),
                          in_specs=[row, pl.BlockSpec((1, LANES), lambda i: (0, 0))], out_specs=row,
                          compiler_params=_cp("parallel"))(udt, bias)


def _dt_bwd(da, ddtx, dt, udt, bias, a_col, *, H, name):
    T = da.shape[0]
    tt = _tile(T, 1024)

    def body(da_ref, dx_ref, dt_ref, u_ref, b_ref, a_ref, draw_ref, dbias_ref, dalog_ref):
        i = pl.program_id(0)
        lane = lax.broadcasted_iota(jnp.int32, (tt, LANES), 1)
        dav = da_ref[...]
        ddt = dav * a_ref[...] + dx_ref[...]
        draw = jnp.where(lane < H, ddt * _sigmoid(u_ref[...] + b_ref[...]), 0.0)
        draw_ref[...] = draw.astype(BF16)

        @pl.when(i == 0)
        def _():
            dbias_ref[...] = jnp.zeros_like(dbias_ref)
            dalog_ref[...] = jnp.zeros_like(dalog_ref)

        dbias_ref[...] += jnp.sum(draw, axis=0, keepdims=True)
        dalog_ref[...] += jnp.sum(dav * dt_ref[...], axis=0, keepdims=True) * a_ref[...]

    row = pl.BlockSpec((tt, LANES), lambda i: (i, 0))
    vec = pl.BlockSpec((1, LANES), lambda i: (0, 0))
    return pl.pallas_call(body, name=name,
                          out_shape=(_sds((T, LANES), BF16), _sds((1, LANES), F32), _sds((1, LANES), F32)),
                          grid=(T // tt,), in_specs=[row, row, row, row, vec, vec], out_specs=(row, vec, vec),
                          compiler_params=_cp("arbitrary"))(da, ddtx, dt, udt, bias, a_col)


def _ssd_consts():
    L = SSM_CHUNK
    r = lax.broadcasted_iota(jnp.int32, (L, L), 0)
    c = lax.broadcasted_iota(jnp.int32, (L, L), 1)
    return r, c


def _ssd_chunk_decays(dtc_ref, dtr_ref, acol_ref, arow_ref, cs_ref, csr_ref, r, c):
    L = SSM_CHUNK
    dtc = dtc_ref[...]
    tril = (r >= c).astype(F32)
    triu = (r <= c).astype(F32)
    cs_ref[...] = jnp.dot(tril, dtc * acol_ref[...], precision=HIGHEST, preferred_element_type=F32)
    csr_ref[...] = jnp.dot(dtr_ref[...] * arow_ref[...], triu, precision=HIGHEST, preferred_element_type=F32)
    cs = cs_ref[...]
    cs_last = cs_ref[L - 1:L, :]
    return dtc, cs, jnp.exp(cs), jnp.exp(cs_last - cs), jnp.exp(cs_last), triu


def _head_spread(arrs, g, heads_per_group):
    P = SSM_HEAD_DIM
    gw = heads_per_group * P
    head = lax.broadcasted_iota(jnp.int32, (LANES, gw), 0)
    lane = lax.broadcasted_iota(jnp.int32, (LANES, gw), 1)
    spread = (head == g * heads_per_group + lane // P).astype(BF16)
    out = _dot(jnp.concatenate(arrs, axis=0).astype(BF16), spread, _NN)
    L = arrs[0].shape[0]
    return [out[i * L:(i + 1) * L] for i in range(len(arrs))]


def _ssd_fwd(xbc, dt, dtT, a_col, a_row, d_col, *, H, name):
    T, XBC = xbc.shape
    L, N, G, P = SSM_CHUNK, SSM_STATE, SSM_GROUPS, SSM_HEAD_DIM
    SX = H * P
    HR = dtT.shape[0]
    nc = T // L
    heads_per_group = H // G

    def body(x_ref, dtc_ref, dtr_ref, acol_ref, arow_ref, d_ref, y_ref, hst_ref, state, cs_ref, csr_ref):
        ci = pl.program_id(0)

        @pl.when(ci == 0)
        def _():
            state[...] = jnp.zeros_like(state)

        hst_ref[0] = state[...]
        r, c = _ssd_consts()
        tri = r >= c
        lane_lo = c < P
        row_lo = r < P
        lane1_lo = lax.broadcasted_iota(jnp.int32, (1, LANES), 1) < P
        dtc, cs, e, ds, cd, _ = _ssd_chunk_decays(dtc_ref, dtr_ref, acol_ref, arow_ref, cs_ref, csr_ref, r, c)
        dsk = d_ref[...]

        for g in range(G):
            Bg = x_ref[:, SX + g * N:SX + (g + 1) * N].astype(BF16)
            Cg = x_ref[:, SX + G * N + g * N:SX + G * N + (g + 1) * N].astype(BF16)
            Gm = _dot(Cg, Bg, _NT)
            dt_g, e_g, ds_g = _head_spread([dtc, e, ds], g, heads_per_group)
            st = []
            for j in range(g * heads_per_group // 2, (g + 1) * heads_per_group // 2):
                h0 = 2 * j
                sl = slice(2 * P * j, 2 * P * (j + 1))
                gl = slice(sl.start - g * heads_per_group * P, sl.stop - g * heads_per_group * P)
                x2 = x_ref[:, sl]
                X2 = x2 * dt_g[:, gl]
                H2 = state[sl, :]
                st.append(dict(h0=h0, sl=sl, x2=x2, X2=X2, H2=H2, e2=e_g[:, gl], R2=_dot(Cg, H2.astype(BF16), _NT),
                               S2=_dot((X2 * ds_g[:, gl]).astype(BF16), Bg, _TN)))
            for s in st:
                ms = []
                for h in (s['h0'], s['h0'] + 1):
                    seg = cs[:, h:h + 1] - csr_ref[h:h + 1, :]
                    ms.append((Gm * jnp.where(tri, jnp.exp(jnp.where(tri, seg, 0.0)), 0.0)).astype(BF16))
                xst = jnp.concatenate([jnp.where(lane_lo, s['X2'], 0.0), jnp.where(lane_lo, 0.0, s['X2'])], axis=0).astype(BF16)
                s['yd'] = _dot(jnp.concatenate(ms, axis=1), xst, _NN)
            for s in st:
                h0, sl = s['h0'], s['sl']
                dsk2 = jnp.where(lane1_lo, dsk[:, h0:h0 + 1], dsk[:, h0 + 1:h0 + 2])
                y_ref[:, sl] = s['yd'] + s['e2'] * s['R2'] + s['x2'] * dsk2
                state[sl, :] = jnp.where(row_lo, cd[:, h0:h0 + 1], cd[:, h0 + 1:h0 + 2]) * s['H2'] + s['S2']

    vec = pl.BlockSpec((1, LANES), lambda i: (0, 0))
    return pl.pallas_call(
        body, name=name, out_shape=(_sds((T, SX), F32), _sds((nc, SX, N), F32)), grid=(nc,),
        in_specs=[pl.BlockSpec((L, XBC), lambda i: (i, 0)), pl.BlockSpec((L, LANES), lambda i: (i, 0)),
                  pl.BlockSpec((HR, L), lambda i: (0, i)), vec, pl.BlockSpec((HR, 1), lambda i: (0, 0)), vec],
        out_specs=(pl.BlockSpec((L, SX), lambda i: (i, 0)), pl.BlockSpec((1, SX, N), lambda i: (i, 0, 0))),
        scratch_shapes=[pltpu.VMEM((SX, N), F32), pltpu.VMEM((L, LANES), F32), pltpu.VMEM((HR, L), F32)],
        compiler_params=_cp("arbitrary"))(xbc, dt, dtT, a_col, a_row, d_col)


def _ssd_bwd(xbc, dt, dtT, a_col, a_row, d_col, hst, dy, *, H, name):
    T, XBC = xbc.shape
    L, N, G, P = SSM_CHUNK, SSM_STATE, SSM_GROUPS, SSM_HEAD_DIM
    SX = H * P
    HR = dtT.shape[0]
    nc = T // L
    heads_per_group = H // G

    def body(x_ref, dtc_ref, dtr_ref, acol_ref, arow_ref, d_ref, hst_ref, dy_ref,
             dx_ref, da_ref, ddtx_ref, dd_ref, dstate, cs_ref, csr_ref):
        ci = pl.program_id(0)

        @pl.when(ci == 0)
        def _():
            dstate[...] = jnp.zeros_like(dstate)
            dd_ref[...] = jnp.zeros_like(dd_ref)

        r, c = _ssd_consts()
        tri = r >= c
        lane_lo = c < P
        row_lo = r < P
        lane1 = lax.broadcasted_iota(jnp.int32, (1, LANES), 1)
        rowc = lax.broadcasted_iota(jnp.int32, (L, 1), 0)
        dtc, cs, e, ds, cd, triu = _ssd_chunk_decays(dtc_ref, dtr_ref, acol_ref, arow_ref, cs_ref, csr_ref, r, c)
        triu_b = triu.astype(BF16)
        dsk = d_ref[...]
        last_row = rowc == L - 1

        triT = r <= c

        def halves(v, axis):
            return jnp.concatenate([jnp.where(lane_lo, v, 0.0), jnp.where(lane_lo, 0.0, v)], axis=axis)

        def head_sum(v, h):
            lo = jnp.sum(jnp.where(lane1 < P, v, 0.0), axis=1, keepdims=True) * (lane1 == h).astype(F32)
            hi = jnp.sum(jnp.where(lane1 < P, 0.0, v), axis=1, keepdims=True) * (lane1 == h + 1).astype(F32)
            return lo + hi

        GW = heads_per_group * P
        gl = lax.broadcasted_iota(jnp.int32, (GW, LANES), 0)
        gc = lax.broadcasted_iota(jnp.int32, (GW, LANES), 1)
        wl = lax.broadcasted_iota(jnp.int32, (heads_per_group * L, LANES), 0)
        wc = lax.broadcasted_iota(jnp.int32, (heads_per_group * L, LANES), 1)
        wide_r = lax.broadcasted_iota(jnp.int32, (L, heads_per_group * L), 0)
        wide_c = lax.broadcasted_iota(jnp.int32, (L, heads_per_group * L), 1)
        below_diag = (wide_c % L) < wide_r
        sums = jnp.zeros((3 * L, LANES), F32)
        da_q = jnp.zeros((L, LANES), F32)
        dcd_acc = jnp.zeros((1, LANES), F32)
        dd_acc = jnp.zeros((1, LANES), F32)
        for g in range(G):
            bsl = slice(SX + g * N, SX + (g + 1) * N)
            csl = slice(SX + G * N + g * N, SX + G * N + (g + 1) * N)
            Bg = x_ref[:, bsl].astype(BF16)
            Cg = x_ref[:, csl].astype(BF16)
            Gm = _dot(Cg, Bg, _NT)
            GmT = _dot(Bg, Cg, _NT)
            dt_g, e_g, ds_g = _head_spread([dtc, e, ds], g, heads_per_group)
            st = []
            for j in range(g * heads_per_group // 2, (g + 1) * heads_per_group // 2):
                h0 = 2 * j
                sl = slice(2 * P * j, 2 * P * (j + 1))
                s = dict(h0=h0, sl=sl, x2=x_ref[:, sl], dy2=dy_ref[:, sl], H2=hst_ref[0, sl, :], dHn=dstate[sl, :])
                gsl = slice(sl.start - g * heads_per_group * P, sl.stop - g * heads_per_group * P)
                s['dt2'], s['e2'], s['ds2'] = dt_g[:, gsl], e_g[:, gsl], ds_g[:, gsl]
                s['X2'] = s['x2'] * s['dt2']
                s['H2b'], s['dHnb'] = s['H2'].astype(BF16), s['dHn'].astype(BF16)
                st.append(s)
            for s in st:
                s['R2'] = _dot(Cg, s['H2b'], _NT)
                s['dXd'] = _dot(Bg, s['dHnb'], _NT)
                s['dM2'] = _dot(s['dy2'].astype(BF16), halves(s['X2'], 0).astype(BF16), _NT)
            dG = jnp.zeros((L, L), F32)
            qs = []
            for s in st:
                mts = []
                for i, h in enumerate((s['h0'], s['h0'] + 1)):
                    z = cs[:, h:h + 1] - csr_ref[h:h + 1, :]
                    Dm = jnp.where(tri, jnp.exp(jnp.where(tri, z, 0.0)), 0.0)
                    DmT = jnp.where(triT, jnp.exp(jnp.where(triT, -z, 0.0)), 0.0)
                    dM = s['dM2'][:, i * L:(i + 1) * L]
                    dG = dG + dM * Dm
                    qs.append((dM * (Gm * Dm)).astype(BF16))
                    mts.append((GmT * DmT).astype(BF16))
                s['dXm'] = _dot(jnp.concatenate(mts, axis=1), halves(s['dy2'], 0).astype(BF16), _NN)
            Wg = _dot(triu_b, jnp.concatenate(qs, axis=1), _NN)
            place = (wc == g * heads_per_group + wl // L).astype(BF16)
            da_q = da_q + _dot(jnp.where(below_diag, Wg, 0.0).astype(BF16), place, _NN)
            dBg = jnp.zeros((L, N), F32)
            dCg = jnp.zeros((L, N), F32)
            des, ddss, dxxs = [], [], []
            for s in st:
                h0, sl, x2, dy2, X2 = s['h0'], s['sl'], s['x2'], s['dy2'], s['X2']
                dR2b = (s['e2'] * dy2).astype(BF16)
                dCg = dCg + _dot(dR2b, s['H2b'], _NN)
                dHr = _dot(dR2b, Cg, _TN)
                dBg = dBg + _dot((X2 * s['ds2']).astype(BF16), s['dHnb'], _NN)
                dX2 = s['ds2'] * s['dXd'] + s['dXm']
                des.append(dy2 * s['R2'])
                ddss.append(s['dXd'] * X2)
                dxxs.append(dX2 * x2)
                prod = s['dHn'] * s['H2']
                for i, h in enumerate((h0, h0 + 1)):
                    rows = jnp.sum(prod[i * P:(i + 1) * P], axis=0, keepdims=True)
                    dcd_acc = dcd_acc + jnp.sum(rows, axis=1, keepdims=True) * (lane1 == h).astype(F32)
                dd_acc = dd_acc + head_sum(jnp.sum(dy2 * x2, axis=0, keepdims=True), h0)
                dsk2 = jnp.where(lane1 < P, dsk[:, h0:h0 + 1], dsk[:, h0 + 1:h0 + 2])
                dx_ref[:, sl] = dX2 * s['dt2'] + dy2 * dsk2
                dstate[sl, :] = jnp.where(row_lo, cd[:, h0:h0 + 1], cd[:, h0 + 1:h0 + 2]) * s['dHn'] + dHr
            stack = jnp.concatenate([jnp.concatenate(v, axis=1) for v in (des, ddss, dxxs)], axis=0).astype(BF16)
            sums = sums + _dot(stack, (gc == g * heads_per_group + gl // P).astype(BF16), _NN)
            dGb = dG.astype(BF16)
            dx_ref[:, bsl] = dBg + _dot(dGb, Cg, _TN)
            dx_ref[:, csl] = dCg + _dot(dGb, Bg, _NN)
        t1 = sums[L:2 * L] * ds
        tail = jnp.sum(t1, axis=0, keepdims=True) + dcd_acc * cd
        dcs = sums[0:L] * e - t1 + jnp.where(last_row, tail, 0.0)
        da_ref[...] = jnp.dot(triu, dcs, precision=HIGHEST, preferred_element_type=F32) + da_q
        ddtx_ref[...] = sums[2 * L:3 * L]
        dd_ref[...] += dd_acc

    vec = pl.BlockSpec((1, LANES), lambda i: (0, 0))
    rev = lambda i: (nc - 1 - i, 0)
    return pl.pallas_call(
        body, name=name,
        out_shape=(_sds((T, XBC), F32), _sds((T, LANES), F32), _sds((T, LANES), F32), _sds((1, LANES), F32)),
        grid=(nc,),
        in_specs=[pl.BlockSpec((L, XBC), rev), pl.BlockSpec((L, LANES), rev),
                  pl.BlockSpec((HR, L), lambda i: (0, nc - 1 - i)), vec, pl.BlockSpec((HR, 1), lambda i: (0, 0)), vec,
                  pl.BlockSpec((1, SX, N), lambda i: (nc - 1 - i, 0, 0)), pl.BlockSpec((L, SX), rev)],
        out_specs=(pl.BlockSpec((L, XBC), rev), pl.BlockSpec((L, LANES), rev), pl.BlockSpec((L, LANES), rev), vec),
        scratch_shapes=[pltpu.VMEM((SX, N), F32), pltpu.VMEM((L, LANES), F32), pltpu.VMEM((HR, L), F32)],
        compiler_params=_cp("arbitrary"))(xbc, dt, dtT, a_col, a_row, d_col, hst, dy)


def _rms_fwd(y, u, zoff, w, *, name):
    T, SX = y.shape
    tt = _tile(T, 256)
    gs = SX // SSM_GROUPS
    zb = zoff // SX

    def body(y_ref, z_ref, w_ref, o_ref):
        for g in range(SSM_GROUPS):
            sl = slice(g * gs, (g + 1) * gs)
            z = _f32(z_ref[:, sl])
            yg = y_ref[:, sl] * (z * _sigmoid(z))
            rstd = lax.rsqrt(jnp.mean(yg * yg, axis=-1, keepdims=True) + RMS_EPS)
            o_ref[:, sl] = (yg * rstd * w_ref[:, sl]).astype(BF16)

    row = pl.BlockSpec((tt, SX), lambda i: (i, 0))
    return pl.pallas_call(body, name=name, out_shape=_sds((T, SX), BF16), grid=(T // tt,),
                          in_specs=[row, pl.BlockSpec((tt, SX), lambda i: (i, zb)), pl.BlockSpec((1, SX), lambda i: (0, 0))],
                          out_specs=row, compiler_params=_cp("parallel"))(y, u, w)


def _rms_bwd(y, u, zoff, w, dyn, *, name):
    T, SX = y.shape
    tt = _tile(T, 256)
    gs = SX // SSM_GROUPS
    zb = zoff // SX

    def body(y_ref, z_ref, w_ref, d_ref, dy_ref, dz_ref, dw_ref):
        i = pl.program_id(0)

        @pl.when(i == 0)
        def _():
            dw_ref[...] = jnp.zeros_like(dw_ref)

        for g in range(SSM_GROUPS):
            sl = slice(g * gs, (g + 1) * gs)
            z, yv, d = _f32(z_ref[:, sl]), y_ref[:, sl], d_ref[:, sl]
            sz, dsz = _silu_and_grad(z)
            yg = yv * sz
            rstd = lax.rsqrt(jnp.mean(yg * yg, axis=-1, keepdims=True) + RMS_EPS)
            t = yg * rstd
            dw_ref[:, sl] += jnp.sum(d * t, axis=0, keepdims=True)
            dt_ = d * w_ref[:, sl]
            dyg = rstd * (dt_ - t * jnp.mean(dt_ * t, axis=-1, keepdims=True))
            dy_ref[:, sl] = dyg * sz
            dz_ref[:, sl] = (dyg * yv * dsz).astype(BF16)

    row = pl.BlockSpec((tt, SX), lambda i: (i, 0))
    vec = pl.BlockSpec((1, SX), lambda i: (0, 0))
    return pl.pallas_call(body, name=name, out_shape=(_sds((T, SX), F32), _sds((T, SX), BF16), _sds((1, SX), F32)),
                          grid=(T // tt,), in_specs=[row, pl.BlockSpec((tt, SX), lambda i: (i, zb)), vec, row],
                          out_specs=(row, row, vec), compiler_params=_cp("arbitrary"))(y, u, w, dyn)


def _gate_fwd(u, goff, ya, yb, *, name):
    T, D = ya.shape
    tt = _tile(T, 512)
    gb = goff // D

    def body(ga_ref, gb_ref, ya_ref, yb_ref, o_ref):
        o_ref[...] = (_sigmoid(_f32(ga_ref[...])) * ya_ref[...] + _sigmoid(_f32(gb_ref[...])) * yb_ref[...]).astype(BF16)

    row = pl.BlockSpec((tt, D), lambda i: (i, 0))
    return pl.pallas_call(body, name=name, out_shape=_sds((T, D), BF16), grid=(T // tt,),
                          in_specs=[pl.BlockSpec((tt, D), lambda i: (i, gb)), pl.BlockSpec((tt, D), lambda i: (i, gb + 1)), row, row],
                          out_specs=row, compiler_params=_cp("parallel"))(u, u, ya, yb)


def _gate_bwd(u, goff, ya, yb, dm, *, name):
    T, D = ya.shape
    tt = _tile(T, 512)
    gb = goff // D

    def body(ga_ref, gb_ref, ya_ref, yb_ref, dm_ref, dya_ref, dyb_ref, dg_ref):
        d = dm_ref[...]
        sa, sb = _sigmoid(_f32(ga_ref[...])), _sigmoid(_f32(gb_ref[...]))
        dya_ref[...] = (d * sa).astype(BF16)
        dyb_ref[...] = (d * sb).astype(BF16)
        dg_ref[...] = jnp.concatenate([d * ya_ref[...] * sa * (1.0 - sa), d * yb_ref[...] * sb * (1.0 - sb)], axis=1).astype(BF16)

    row = pl.BlockSpec((tt, D), lambda i: (i, 0))
    return pl.pallas_call(body, name=name, out_shape=(_sds((T, D), BF16), _sds((T, D), BF16), _sds((T, 2 * D), BF16)),
                          grid=(T // tt,),
                          in_specs=[pl.BlockSpec((tt, D), lambda i: (i, gb)), pl.BlockSpec((tt, D), lambda i: (i, gb + 1)), row, row, row],
                          out_specs=(row, row, pl.BlockSpec((tt, 2 * D), lambda i: (i, 0))),
                          compiler_params=_cp("parallel"))(u, u, ya, yb, dm)


def _adamw_math(w, gg, m, v):
    nm = ADAM_B1 * m + (1.0 - ADAM_B1) * gg
    nv = ADAM_B2 * v + (1.0 - ADAM_B2) * (gg * gg)
    m_hat = nm / (1.0 - ADAM_B1 ** ADAM_STEP)
    v_hat = nv / (1.0 - ADAM_B2 ** ADAM_STEP)
    return -ADAM_LR * (m_hat / (jnp.sqrt(v_hat) + ADAM_EPS) + ADAM_WD * w), nm, nv


def _adamw(w, g, m, v, *, name):
    R, C = w.shape
    tr = _pick(R, 256, 8)

    def body(w_ref, g_ref, m_ref, v_ref, d_ref, nm_ref, nv_ref):
        d_ref[...], nm_ref[...], nv_ref[...] = _adamw_math(w_ref[...], g_ref[...], m_ref[...], v_ref[...])

    blk = pl.BlockSpec((tr, C), lambda i: (i, 0))
    out = _sds((R, C), F32)
    return pl.pallas_call(body, name=name, out_shape=(out, out, out), grid=(R // tr,), in_specs=[blk] * 4,
                          out_specs=(blk,) * 3, compiler_params=_cp("parallel"))(w, g, m, v)


def _sum_adamw(parts, w, m, v, *, name):
    depth, C = w.shape[0], w.shape[-1]
    R = w[0].size // C
    tr = _pick(R, 256, 8)

    def body(*refs):
        p_refs = refs[:depth]
        w_ref, m_ref, v_ref, g_ref, d_ref, nm_ref, nv_ref = refs[depth:]
        for l in range(depth):
            @pl.when(pl.program_id(0) == l)
            def _(l=l):
                gg = p_refs[l][0].astype(F32)
                for k in range(1, N_DEV):
                    gg = gg + p_refs[l][k].astype(F32)
                g_ref[...] = gg
                d_ref[...], nm_ref[...], nv_ref[...] = _adamw_math(w_ref[...], gg, m_ref[...], v_ref[...])

    p_specs = [pl.BlockSpec((N_DEV, tr, C), lambda l, i, ll=ll: (0, jnp.where(l == ll, i, 0), 0)) for ll in range(depth)]
    blk = pl.BlockSpec((None, tr, C), lambda l, i: (l, i, 0))
    out = _sds((depth, R, C), F32)
    res = pl.pallas_call(body, name=name, out_shape=(out,) * 4, grid=(depth, R // tr),
                         in_specs=p_specs + [blk, blk, blk], out_specs=(blk,) * 4,
                         compiler_params=_cp("parallel", "parallel"))(
        *[x.reshape(N_DEV, R, C) for x in parts], w.reshape(depth, R, C), m.reshape(depth, R, C), v.reshape(depth, R, C))
    return tuple(r.reshape(w.shape) for r in res)


def _sum_slots(x, *, name):
    n, R, C = x.shape
    tr = _tile(R, 512)

    def body(x_ref, o_ref):
        acc = x_ref[0].astype(F32)
        for k in range(1, n):
            acc = acc + x_ref[k].astype(F32)
        o_ref[...] = acc

    return pl.pallas_call(body, name=name, out_shape=_sds((R, C), F32), grid=(R // tr,),
                          in_specs=[pl.BlockSpec((n, tr, C), lambda i: (0, i, 0))],
                          out_specs=pl.BlockSpec((tr, C), lambda i: (i, 0)), compiler_params=_cp("parallel"))(x)


def _exchange(xs, *, scatter, name):
    n = len(xs)

    def body(*refs):
        x_refs, o_refs = refs[:n], refs[n:2 * n]
        send_sems, recv_sems, local_sems = refs[2 * n:]
        mx, my, mc = lax.axis_index("x"), lax.axis_index("y"), lax.axis_index("c")
        me = 4 * mx + 2 * my + mc

        def src(i, d):
            return x_refs[i].at[d] if scatter else x_refs[i]

        locals_ = [pltpu.make_async_copy(src(i, me), o_refs[i].at[me], local_sems.at[i]) for i in range(n)]
        for cp in locals_:
            cp.start()
        sends, recvs = [], []
        for k in range(1, N_DEV):
            px = 1 - mx if k & 4 else mx
            py = 1 - my if k & 2 else my
            pc = 1 - mc if k & 1 else mc
            peer = 4 * px + 2 * py + pc
            for i in range(n):
                common = dict(send_sem=send_sems.at[k - 1, i], recv_sem=recv_sems.at[k - 1, i],
                              device_id=(px, py, pc), device_id_type=pl.DeviceIdType.MESH)
                sends.append(pltpu.make_async_remote_copy(src_ref=src(i, peer), dst_ref=o_refs[i].at[me], **common))
                recvs.append(pltpu.make_async_remote_copy(src_ref=src(i, peer), dst_ref=o_refs[i].at[peer], **common))
        for cp in sends:
            cp.start()
        for cp in recvs:
            cp.wait_recv()
        for cp in sends:
            cp.wait_send()
        for cp in locals_:
            cp.wait()

    any_spec = pl.BlockSpec(memory_space=pl.ANY)
    out_shape = tuple(_sds(x.shape if scatter else (N_DEV,) + x.shape, x.dtype) for x in xs)
    return pl.pallas_call(
        body, name=name, out_shape=out_shape, in_specs=[any_spec] * n, out_specs=(any_spec,) * n,
        scratch_shapes=[pltpu.SemaphoreType.DMA((N_DEV - 1, n)), pltpu.SemaphoreType.DMA((N_DEV - 1, n)),
                        pltpu.SemaphoreType.DMA((n,))])(*xs)


def _to_rows(flat, row_mult):
    n = flat.shape[-1]
    per = row_mult * LANES
    pad = (-n) % per
    flat = jnp.pad(flat, [(0, 0)] * (flat.ndim - 1) + [(0, pad)])
    return flat.reshape(flat.shape[:-1] + ((n + pad) // LANES, LANES))


def _peers():
    mx, my, mc = lax.axis_index("x"), lax.axis_index("y"), lax.axis_index("c")
    out = []
    for k in range(1, N_DEV):
        px = 1 - mx if k & 4 else mx
        py = 1 - my if k & 2 else my
        pc = 1 - mc if k & 1 else mc
        out.append(((px, py, pc), 4 * px + 2 * py + pc))
    return 4 * mx + 2 * my + mc, out


_HBM = pl.BlockSpec(memory_space=pltpu.HBM)
_SEM = pl.BlockSpec(memory_space=pltpu.SEMAPHORE)


def _exchange_start(xs, *, scatter, name):
    n = len(xs)

    def body(*refs):
        x_refs, land_refs, send_sems, recv_sems, token = refs[:n], refs[n:2 * n], refs[2 * n], refs[2 * n + 1], refs[-1]
        me, peers = _peers()
        for k, (dev, peer) in enumerate(peers):
            for i in range(n):
                pltpu.make_async_remote_copy(
                    src_ref=x_refs[i].at[peer] if scatter else x_refs[i], dst_ref=land_refs[i].at[me],
                    send_sem=send_sems.at[k * n + i], recv_sem=recv_sems.at[k * n + i],
                    device_id=dev, device_id_type=pl.DeviceIdType.MESH).start()
        token[...] = jnp.zeros_like(token)

    land_shapes = [x.shape if scatter else (N_DEV,) + x.shape for x in xs]
    lands = [pltpu.with_memory_space_constraint(lax.empty(s, x.dtype), pltpu.HBM) for s, x in zip(land_shapes, xs)]
    srcs = [pltpu.with_memory_space_constraint(x, pltpu.HBM) for x in xs]
    out = pl.pallas_call(
        body, name=name,
        out_shape=(pltpu.SemaphoreType.DMA(((N_DEV - 1) * n,)), pltpu.SemaphoreType.DMA(((N_DEV - 1) * n,)),
                   *[pltpu.HBM(x.shape, x.dtype) for x in xs], *[pltpu.HBM(s, x.dtype) for s, x in zip(land_shapes, xs)],
                   _sds((8, LANES), F32)),
        in_specs=(_HBM,) * (2 * n), out_specs=(_SEM, _SEM) + (_HBM,) * (2 * n) + (pl.BlockSpec(memory_space=pltpu.VMEM),),
        input_output_aliases={i: 2 + i for i in range(2 * n)},
        compiler_params=pltpu.CompilerParams(has_side_effects=pltpu.SideEffectType.DATAFLOW_SIDE_EFFECTING))(*srcs, *lands)
    return (out[0], out[1], list(out[2:2 + n]), list(out[2 + n:2 + 2 * n])), out[-1]


def _exchange_wait(handle, after, *, scatter, name):
    send_sems, recv_sems, srcs, lands = handle
    n = len(srcs)

    def body(*refs):
        x_refs, land_refs, send_sems, recv_sems = refs[:n], refs[n:2 * n], refs[2 * n], refs[2 * n + 1]
        me, peers = _peers()
        for k, (dev, peer) in enumerate(peers):
            for i in range(n):
                cp = pltpu.make_async_remote_copy(
                    src_ref=x_refs[i].at[peer] if scatter else x_refs[i], dst_ref=land_refs[i].at[peer],
                    send_sem=send_sems.at[k * n + i], recv_sem=recv_sems.at[k * n + i],
                    device_id=dev, device_id_type=pl.DeviceIdType.MESH)
                cp.wait_send()
                cp.wait_recv()

    out = pl.pallas_call(
        body, name=name, out_shape=tuple(pltpu.HBM(a.shape, a.dtype) for a in srcs + lands),
        in_specs=(_HBM,) * (2 * n) + (_SEM, _SEM, pl.BlockSpec(memory_space=pl.ANY)), out_specs=(_HBM,) * (2 * n),
        input_output_aliases={i: i for i in range(2 * n)},
        compiler_params=pltpu.CompilerParams(has_side_effects=pltpu.SideEffectType.DATAFLOW_SIDE_EFFECTING))(
        *srcs, *lands, send_sems, recv_sems, after)
    return list(out[n:])


def _with_own_slot(lands, xs, *, scatter):
    me = 4 * lax.axis_index("x") + 2 * lax.axis_index("y") + lax.axis_index("c")
    out = []
    for land, x in zip(lands, xs):
        own = lax.dynamic_index_in_dim(x, me, 0, keepdims=True) if scatter else x[None]
        out.append(lax.dynamic_update_index_in_dim(land, own, me, 0))
    return out


def _wire_shards(p, l, names):
    return [p[k][l].astype(WIRE) if SHARDED[k][1] else p[k][l] for k in names]


def _full_weights(got, p, names):
    full = {}
    for k, g in zip(names, got):
        axis, shape = SHARDED[k][0] - 1, p[k].shape[1:]
        g = jnp.moveaxis(g, 0, axis)
        full[k] = g.reshape(shape[:axis] + (N_DEV * shape[axis],) + shape[axis + 1:])
    return full


def _grad_parts(grads, p, names):
    parts = []
    for k in names:
        axis, g = SHARDED[k][0] - 1, grads[k]
        g = g.reshape(g.shape[:axis] + (N_DEV, p[k].shape[1:][axis]) + g.shape[axis + 1:])
        parts.append(jnp.moveaxis(g, axis, 0).astype(WIRE))
    return parts


def _allreduce_small(vals):
    metas = [(k, v.shape, v.size) for k, v in vals.items()]
    flat = jnp.concatenate([v.reshape(-1) for v in vals.values()])
    got, = _exchange([_to_rows(flat, 8)], scatter=False, name="gather_small_grads")
    summed = _sum_slots(got, name="sum_small_grads").reshape(-1)
    out, off = {}, 0
    for k, shape, n in metas:
        out[k] = summed[off:off + n].reshape(shape)
        off += n
    return out


def _row(v):
    return v.reshape(1, -1).astype(F32)


def _lanes(v):
    return jnp.pad(v.astype(F32), (0, LANES - v.shape[0])).reshape(1, LANES)


EARLY = ('w_in', 'conv_dw_w', 'ssm_conv_w')
LATE = tuple(k for k in SHARDED if k not in EARLY)
FFN = ('w_ffn_up', 'ffn_dw_w', 'w_ffn_down')
NOT_FFN = tuple(k for k in SHARDED if k not in FFN)


def _layer_weights(p, full, l, dims):
    D, SX, XBC, H, FF = dims
    assert _tile(D, GLU_TILE) == D and _tile(FF, FFN_TILE) == FF
    W = {}
    if 'w_in' in full:
        w_in = full['w_in']
        o_dt = 2 * D + SX + XBC
        w_pieces = [w_in[:, :2 * D], w_in[:, 2 * D:2 * D + SX], w_in[:, 2 * D + SX:o_dt], w_in[:, o_dt + H:]]
        a_head = -jnp.exp(p['ssm_a_log'][l].astype(F32))
        hr = -(-H // 8) * 8
        W.update(
            w_main=jnp.concatenate(w_pieces, axis=1), w_pieces=w_pieces,
            w_dt=jnp.pad(w_in[:, o_dt:o_dt + H], ((0, 0), (0, LANES - H))),
            conv_w=full['conv_dw_w'], conv_b=_row(p['conv_dw_b'][l]), conv_g=_row(p['conv_ln_g'][l]),
            conv_beta=_row(p['conv_ln_b'][l]), ssm_w=full['ssm_conv_w'], ssm_b=_row(p['ssm_conv_b'][l]),
            dt_bias=_lanes(p['ssm_dt_bias'][l]), a_col=_lanes(a_head),
            a_row=jnp.pad(a_head, (0, hr - H)).reshape(hr, 1), d_col=_lanes(p['ssm_d'][l]),
            norm_w=_row(p['ssm_norm_w'][l]), ln1_g=_row(p['ln1_g'][l]), ln1_b=_row(p['ln1_b'][l]),
            ffn_b=_row(p['ffn_dw_b'][l]), ln2_g=_row(p['ln2_g'][l]), ln2_b=_row(p['ln2_b'][l]))
    if 'w_o' in full:
        W.update(w_conv_out=full['w_conv_out'], w_ssm_out=full['w_ssm_out'], w_o=full['w_o'], w_up=full['w_ffn_up'],
                 ffn_w=full['ffn_dw_w'], w_down=full['w_ffn_down'])
    return W


def _layer_fwd(h, hb, W, late, l, dims):
    D, SX, XBC, H, FF = dims
    o_z, o_xbc, o_gate = 2 * D, 2 * D + SX, 2 * D + SX + XBC
    hr = W['a_row'].shape[0]
    t = f"l{l}"
    u = _mm(hb, W['w_main'], mode='nn', out_dtype=BF16, name=t + "_in_proj")
    udt = _mm(hb, W['w_dt'], mode='nn', out_dtype=F32, name=t + "_dt_proj")
    v1, v3 = _conf_fwd(u, W['conv_w'], W['conv_b'], W['conv_g'], W['conv_beta'], D=D, name=t + "_conf_fwd")
    xbc = _conv_act_fwd(u, o_xbc, W['ssm_w'], W['ssm_b'], ffn=False, name=t + "_ssm_conv")
    dt = _dt_fwd(udt, W['dt_bias'], H=H, name=t + "_dt")
    dtT = jnp.pad(dt[:, :H].T, ((0, hr - H), (0, 0)))
    y, hst = _ssd_fwd(xbc, dt, dtT, W['a_col'], W['a_row'], W['d_col'], H=H, name=t + "_ssd_fwd")
    yn = _rms_fwd(y, u, o_z, W['norm_w'], name=t + "_rms_fwd")
    W = {**W, **late(yn)}
    ya = _mm(v3, W['w_conv_out'], mode='nn', out_dtype=F32, name=t + "_conv_out")
    yb = _mm(yn, W['w_ssm_out'], mode='nn', out_dtype=F32, name=t + "_ssm_out")
    m = _gate_fwd(u, o_gate, ya, yb, name=t + "_gate_fwd")
    h1, s1, h1b = _mm_res_ln(m, W['w_o'], h, W['ln1_g'], W['ln1_b'], name=t + "_mix_ln1")
    uf = _mm(h1b, W['w_up'], mode='nn', out_dtype=BF16, name=t + "_ffn_up")
    act = _conv_act_fwd(uf, 0, W['ffn_w'], W['ffn_b'], ffn=True, name=t + "_ffn_conv")
    h2, s2, h2b = _mm_res_ln(act, W['w_down'], h1, W['ln2_g'], W['ln2_b'], name=t + "_ffn_down_ln2")
    saved = dict(hb=hb, u=u, udt=udt, v1=v1, v3=v3, ya=ya, xbc=xbc, dt=dt, dtT=dtT, y=y, hst=hst, yn=yn, yb=yb, m=m,
                 h1b=h1b, s1=s1, uf=uf, act=act, s2=s2)
    return h2, h2b, saved, W


def _layer_bwd(dh2, W, S, l, dims, on_ffn_grads=None, on_all_grads=None):
    D, SX, XBC, H, FF = dims
    o_z, o_xbc, o_gate = 2 * D, 2 * D + SX, 2 * D + SX + XBC
    t = f"l{l}"
    g = {}
    ds2, g['ln2_g'], g['ln2_b'] = _ln_bwd(S['s2'], dh2, W['ln2_g'], W['ln2_b'], silu=False, name=t + "_ln2_bwd")
    g['w_down'] = _mm(S['act'], ds2, mode='tn', out_dtype=WIRE, name=t + "_dw_down")
    dact = _mm(ds2, W['w_down'], mode='nt', out_dtype=F32, name=t + "_dact")
    duf, g['ffn_w'], g['ffn_b'] = _conv_act_bwd(S['uf'], 0, W['ffn_w'], W['ffn_b'], dact, ffn=True, name=t + "_ffn_conv_bwd")
    g['w_up'] = _mm(S['h1b'], duf, mode='tn', out_dtype=WIRE, name=t + "_dw_up")
    dh1 = _mm(duf, W['w_up'], mode='nt', out_dtype=F32, res=ds2, res_scale=DN_ALPHA, name=t + "_dh1")
    ln1_g = W['ln1_g'] if on_ffn_grads is None else W['ln1_g'] + on_ffn_grads(g)[0:1, 0:1]
    ds1, g['ln1_g'], g['ln1_b'] = _ln_bwd(S['s1'], dh1, ln1_g, W['ln1_b'], silu=False, name=t + "_ln1_bwd")
    g['w_o'] = _mm(S['m'], ds1, mode='tn', out_dtype=WIRE, name=t + "_dw_o")
    dm = _mm(ds1, W['w_o'], mode='nt', out_dtype=F32, name=t + "_dm")
    dya, dyb, dgate = _gate_bwd(S['u'], o_gate, S['ya'], S['yb'], dm, name=t + "_gate_bwd")
    g['w_conv_out'] = _mm(S['v3'], dya, mode='tn', out_dtype=WIRE, name=t + "_dw_conv_out")
    dv3 = _mm(dya, W['w_conv_out'], mode='nt', out_dtype=F32, name=t + "_dv3")
    dv1, g['conv_g'], g['conv_beta'] = _ln_bwd(S['v1'], dv3, W['conv_g'], W['conv_beta'], silu=True, name=t + "_conv_ln_bwd")
    dglu, g['conv_w'], g['conv_b'] = _glu_conv_bwd(dv1, S['u'], W['conv_w'], name=t + "_conf_conv_bwd")
    g['w_ssm_out'] = _mm(S['yn'], dyb, mode='tn', out_dtype=WIRE, name=t + "_dw_ssm_out")
    dyn = _mm(dyb, W['w_ssm_out'], mode='nt', out_dtype=F32, name=t + "_dyn")
    dy, dz, g['norm_w'] = _rms_bwd(S['y'], S['u'], o_z, W['norm_w'], dyn, name=t + "_rms_bwd")
    dxbc_c, da, ddtx, g['d'] = _ssd_bwd(S['xbc'], S['dt'], S['dtT'], W['a_col'], W['a_row'], W['d_col'], S['hst'], dy,
                                        H=H, name=t + "_ssd_bwd")
    ddt_raw, g['dt_bias'], g['a_log'] = _dt_bwd(da, ddtx, S['dt'], S['udt'], W['dt_bias'], W['a_col'], H=H, name=t + "_dt_bwd")
    dxbc, g['ssm_w'], g['ssm_b'] = _conv_act_bwd(S['u'], o_xbc, W['ssm_w'], W['ssm_b'], dxbc_c, ffn=False, name=t + "_ssm_conv_bwd")
    du = [dglu, dz, dxbc, dgate]
    g['w_pieces'] = [_mm(S['hb'], d, mode='tn', out_dtype=WIRE, name=f"{t}_dw_in{i}") for i, d in enumerate(du)]
    g['w_dt'] = _mm(S['hb'], ddt_raw, mode='tn', out_dtype=WIRE, name=t + "_dw_dt")
    w_dt = W['w_dt'] if on_all_grads is None else W['w_dt'] + on_all_grads(g)[0:1, 0:1].astype(W['w_dt'].dtype)
    dh = _mm_nt_cat(du + [ddt_raw], W['w_pieces'] + [w_dt], ds1, DN_ALPHA, name=t + "_dh_in")
    return dh, g


def _layer_grads_to_params(g, dims):
    D, SX, XBC, H, FF = dims
    glu, dz, dxbc, dgate = g['w_pieces']
    w_in = jnp.concatenate([glu, dz, dxbc, g['w_dt'][:, :H], dgate], axis=1)
    return dict(
        w_in=w_in, conv_dw_w=g['conv_w'], conv_dw_b=g['conv_b'][0], conv_ln_g=g['conv_g'][0], conv_ln_b=g['conv_beta'][0],
        w_conv_out=g['w_conv_out'], ssm_conv_w=g['ssm_w'], ssm_conv_b=g['ssm_b'][0],
        ssm_dt_bias=g['dt_bias'][0, :H], ssm_a_log=g['a_log'][0, :H], ssm_d=g['d'][0, :H], ssm_norm_w=g['norm_w'][0],
        w_ssm_out=g['w_ssm_out'], w_o=g['w_o'], ln1_g=g['ln1_g'][0], ln1_b=g['ln1_b'][0],
        w_ffn_up=g['w_up'], ffn_dw_w=g['ffn_w'], ffn_dw_b=g['ffn_b'][0], w_ffn_down=g['w_down'], ln2_g=g['ln2_g'][0], ln2_b=g['ln2_b'][0])


def kernel(x, ln_in_g, ln_in_b, w_in, conv_dw_w, conv_dw_b, conv_ln_g, conv_ln_b, w_conv_out, ssm_conv_w, ssm_conv_b, ssm_dt_bias, ssm_a_log, ssm_d, ssm_norm_w, w_ssm_out, w_o, ln1_g, ln1_b, w_ffn_up, ffn_dw_w, ffn_dw_b, w_ffn_down, ln2_g, ln2_b, loss_target, m_ln_in_g, m_ln_in_b, m_w_in, m_conv_dw_w, m_conv_dw_b, m_conv_ln_g, m_conv_ln_b, m_w_conv_out, m_ssm_conv_w, m_ssm_conv_b, m_ssm_dt_bias, m_ssm_a_log, m_ssm_d, m_ssm_norm_w, m_w_ssm_out, m_w_o, m_ln1_g, m_ln1_b, m_w_ffn_up, m_ffn_dw_w, m_ffn_dw_b, m_w_ffn_down, m_ln2_g, m_ln2_b, v_ln_in_g, v_ln_in_b, v_w_in, v_conv_dw_w, v_conv_dw_b, v_conv_ln_g, v_conv_ln_b, v_w_conv_out, v_ssm_conv_w, v_ssm_conv_b, v_ssm_dt_bias, v_ssm_a_log, v_ssm_d, v_ssm_norm_w, v_w_ssm_out, v_w_o, v_ln1_g, v_ln1_b, v_w_ffn_up, v_ffn_dw_w, v_ffn_dw_b, v_w_ffn_down, v_ln2_g, v_ln2_b):
    weights = (ln_in_g, ln_in_b, w_in, conv_dw_w, conv_dw_b, conv_ln_g, conv_ln_b, w_conv_out, ssm_conv_w, ssm_conv_b,
               ssm_dt_bias, ssm_a_log, ssm_d, ssm_norm_w, w_ssm_out, w_o, ln1_g, ln1_b, w_ffn_up, ffn_dw_w, ffn_dw_b,
               w_ffn_down, ln2_g, ln2_b)
    moments_m = (m_ln_in_g, m_ln_in_b, m_w_in, m_conv_dw_w, m_conv_dw_b, m_conv_ln_g, m_conv_ln_b, m_w_conv_out,
                 m_ssm_conv_w, m_ssm_conv_b, m_ssm_dt_bias, m_ssm_a_log, m_ssm_d, m_ssm_norm_w, m_w_ssm_out, m_w_o,
                 m_ln1_g, m_ln1_b, m_w_ffn_up, m_ffn_dw_w, m_ffn_dw_b, m_w_ffn_down, m_ln2_g, m_ln2_b)
    moments_v = (v_ln_in_g, v_ln_in_b, v_w_in, v_conv_dw_w, v_conv_dw_b, v_conv_ln_g, v_conv_ln_b, v_w_conv_out,
                 v_ssm_conv_w, v_ssm_conv_b, v_ssm_dt_bias, v_ssm_a_log, v_ssm_d, v_ssm_norm_w, v_w_ssm_out, v_w_o,
                 v_ln1_g, v_ln1_b, v_w_ffn_up, v_ffn_dw_w, v_ffn_dw_b, v_w_ffn_down, v_ln2_g, v_ln2_b)
    p = dict(zip(PARAMS, weights))
    pm = dict(zip(PARAMS, moments_m))
    pv = dict(zip(PARAMS, moments_v))

    T, D = x.shape[1], x.shape[2]
    SX = w_ssm_out.shape[1] * N_DEV
    XBC = ssm_conv_b.shape[-1]
    H = ssm_d.shape[-1]
    FF = ffn_dw_b.shape[-1] // 2
    dims = (D, SX, XBC, H, FF)
    depth = w_in.shape[0]

    ALL = tuple(SHARDED)
    got = _exchange(_wire_shards(p, 0, EARLY), scatter=False, name="gather_l0_first")
    rest_shards, got = lax.optimization_barrier((_wire_shards(p, 0, LATE), got))
    rest_handle, tok = _exchange_start(rest_shards, scatter=False, name="gather_l0_rest_start")
    first = _layer_weights(p, _full_weights(got, p, EARLY), 0, dims)
    in_flight = {}

    def start_next_gather(l, behind):
        shards, _ = lax.optimization_barrier((_wire_shards(p, l, ALL), behind))
        in_flight[l], t = _exchange_start(shards, scatter=False, name=f"gather_l{l}_start")
        in_flight[l] = (in_flight[l], shards)
        return t

    def late_weights(l):
        def late(x):
            if l > 0:
                return {}
            got = _with_own_slot(_exchange_wait(rest_handle, x, scatter=False, name="gather_l0_rest_wait"), rest_shards, scatter=False)
            W = _layer_weights(p, _full_weights(got, p, LATE), 0, dims)
            if depth > 1:
                t = start_next_gather(1, got)
                W['w_conv_out'] = W['w_conv_out'] + t[0:1, 0:1].astype(W['w_conv_out'].dtype)
            return W
        return late

    xs = x.reshape(T, D)
    h, hb = _ln_fwd(xs, _row(ln_in_g) + tok[0:1, 0:1], _row(ln_in_b), name="ln_in_fwd")
    saved, layers = [], []
    for l in range(depth):
        if l > 0:
            handle, shards = in_flight.pop(l)
            got = _with_own_slot(_exchange_wait(handle, h, scatter=False, name=f"gather_l{l}_wait"), shards, scatter=False)
            first = _layer_weights(p, _full_weights(got, p, ALL), l, dims)
            if l + 1 < depth:
                t = start_next_gather(l + 1, got)
                first['w_dt'] = first['w_dt'] + t[0:1, 0:1].astype(first['w_dt'].dtype)
        h, hb, s, W = _layer_fwd(h, hb, first, late_weights(l), l, dims)
        saved.append(s)
        layers.append(W)
    loss_part, dh = _loss_fwd_bwd(h, loss_target.reshape(T, D), name="loss")

    layer_grads = [None] * depth
    flying, arrived = [], {}

    def wait_flying(after):
        while flying:
            l_, names, handle, parts = flying.pop(0)
            lands = _with_own_slot(_exchange_wait(handle, after, scatter=True, name=f"scatter_l{l_}_{names[0]}_wait"), parts, scatter=True)
            arrived.update({(l_, k): a for k, a in zip(names, lands)})
            after = lands[0]
        return after

    def start_scatter(l, names, grads_now, after):
        parts, _ = lax.optimization_barrier((_grad_parts(grads_now, p, names), wait_flying(after)))
        handle, t = _exchange_start(parts, scatter=True, name=f"scatter_l{l}_{names[0]}_start")
        flying.append((l, names, handle, parts))
        return t

    def on_ffn_grads(g):
        return start_scatter(0, FFN, dict(w_ffn_up=g['w_up'], ffn_dw_w=g['ffn_w'], w_ffn_down=g['w_down']), g['w_up'])

    def on_all_grads(g):
        layer_grads[0] = _layer_grads_to_params(g, dims)
        return start_scatter(0, NOT_FFN, layer_grads[0], g['w_dt'])

    tok = jnp.zeros((8, LANES), F32)
    for l in reversed(range(depth)):
        W = dict(layers[l], ln2_g=layers[l]['ln2_g'] + tok[0:1, 0:1])
        if l > 0:
            dh, g = _layer_bwd(dh, W, saved[l], l, dims)
            layer_grads[l] = _layer_grads_to_params(g, dims)
            tok = start_scatter(l, ALL, layer_grads[l], dh)
        else:
            dh, g = _layer_bwd(dh, W, saved[l], l, dims, on_ffn_grads, on_all_grads)
    grad_x, dg_in, db_in = _ln_bwd(xs, dh, _row(ln_in_g), _row(ln_in_b), silu=False, name="ln_in_bwd")
    wait_flying(grad_x)

    small = {k: jnp.stack([layer_grads[l][k] for l in range(depth)]) for k in layer_grads[0] if k not in SHARDED}
    small['ln_in_g'], small['ln_in_b'], small['loss'] = dg_in[0], db_in[0], loss_part
    grads = _allreduce_small(small)
    loss = grads.pop('loss').reshape(())

    delta, new_m, new_v = {}, {}, {}
    for k in SHARDED:
        grads[k], delta[k], new_m[k], new_v[k] = _sum_adamw([arrived[(l, k)] for l in range(depth)], p[k], pm[k], pv[k],
                                                            name="sum_adamw_" + k)
    rest = [k for k in PARAMS if k not in SHARDED]
    flat = lambda d: _to_rows(jnp.concatenate([d[k].reshape(-1) for k in rest]), 8)
    d_, m_, v_ = _adamw(flat(p), flat(grads), flat(pm), flat(pv), name="adamw_small")
    off = 0
    for k in rest:
        n, shp = p[k].size, p[k].shape
        delta[k] = d_.reshape(-1)[off:off + n].reshape(shp)
        new_m[k] = m_.reshape(-1)[off:off + n].reshape(shp)
        new_v[k] = v_.reshape(-1)[off:off + n].reshape(shp)
        off += n

    return (loss, grad_x.reshape(x.shape), *[grads[k] for k in PARAMS], *[delta[k] for k in PARAMS],
            *[new_m[k] for k in PARAMS], *[new_v[k] for k in PARAMS])
```

```python
import math

import jax
import jax.numpy as jnp
from jax import lax
from jax.experimental import pallas as pl
from jax.experimental.pallas import tpu as pltpu

F32 = jnp.float32
BF16 = jnp.bfloat16
WIRE = jnp.bfloat16
HIGHEST = lax.Precision.HIGHEST

DEPTH = 2
SSM_STATE = 128
SSM_CHUNK = 128
SSM_GROUPS = 4
SSM_HEAD_DIM = 64
DN_ALPHA = (2 * DEPTH) ** 0.25
LN_EPS = 1e-5
RMS_EPS = 1e-5
ADAM_LR = 0.001
ADAM_B1 = 0.9
ADAM_B2 = 0.999
ADAM_EPS = 1e-08
ADAM_WD = 0.01
ADAM_STEP = 10

N_DEV = 8
LANES = 128
VMEM_LIMIT = 48 * 1024 * 1024
WHOLE_K_VMEM = 40 * 1024 * 1024
GLU_TILE = 1024
FFN_TILE = 2816
CONV_ROWS = 256

PARAMS = ['ln_in_g', 'ln_in_b', 'w_in', 'conv_dw_w', 'conv_dw_b', 'conv_ln_g', 'conv_ln_b', 'w_conv_out',
          'ssm_conv_w', 'ssm_conv_b', 'ssm_dt_bias', 'ssm_a_log', 'ssm_d', 'ssm_norm_w', 'w_ssm_out', 'w_o',
          'ln1_g', 'ln1_b', 'w_ffn_up', 'ffn_dw_w', 'ffn_dw_b', 'w_ffn_down', 'ln2_g', 'ln2_b']
SHARDED = {'w_in': (2, True), 'conv_dw_w': (2, False), 'w_conv_out': (1, True), 'ssm_conv_w': (2, False),
           'w_ssm_out': (1, True), 'w_o': (1, True), 'w_ffn_up': (2, True), 'ffn_dw_w': (2, False),
           'w_ffn_down': (1, True)}


def _sds(shape, dtype):
    return jax.ShapeDtypeStruct(tuple(shape), dtype)


def _tile(dim, pref):
    return pref if dim % pref == 0 else dim


def _pick(dim, pref, mult):
    best = None
    for t in range(mult, min(dim, pref) + 1, mult):
        if dim % t == 0:
            best = t
    return best or dim


def _cp(*sem):
    return pltpu.CompilerParams(dimension_semantics=sem, vmem_limit_bytes=VMEM_LIMIT)


def _f32(x):
    return x.astype(F32)


def _sigmoid(x):
    return 1.0 / (1.0 + jnp.exp(-x))


def _silu_and_grad(x):
    s = _sigmoid(x)
    return x * s, s * (1.0 + x * (1.0 - s))


def _ln_stats(s):
    mu = jnp.mean(s, axis=-1, keepdims=True)
    xc = s - mu
    var = jnp.mean(xc * xc, axis=-1, keepdims=True)
    rstd = lax.rsqrt(var + LN_EPS)
    return xc * rstd, rstd


def _dot(a, b, dims):
    return lax.dot_general(a, b, (dims, ((), ())), preferred_element_type=F32)


_NN = ((1,), (0,))
_NT = ((1,), (1,))
_TN = ((0,), (0,))


def _mm(a, b, *, mode, out_dtype, name, res=None, res_scale=1.0, tm=1024, tn=1408, tk=1408):
    if mode == 'nn':
        (M, K), (_, N) = a.shape, b.shape
    elif mode == 'nt':
        (M, K), (N, _) = a.shape, b.shape
    else:
        (K, M), (_, N) = a.shape, b.shape
        tm, tk = 1408, 1024
    tm, tn, tk = _pick(M, tm, 8), _pick(N, tn, LANES), _pick(K, tk, LANES)
    if mode != 'tn':
        for rows in (tm, _pick(M, tm // 2, 8), _pick(M, tm // 4, 8)):
            blocks = rows * K * a.dtype.itemsize + tn * K * b.dtype.itemsize + rows * tn * (4 + (4 if res is not None else 0))
            if 2 * blocks <= WHOLE_K_VMEM:
                tm, tk = rows, K
                break
    nk = K // tk
    dims = {'nn': _NN, 'nt': _NT, 'tn': _TN}[mode]

    def body(*refs):
        if res is None:
            a_ref, b_ref, o_ref = refs[:3]
            r_ref = None
        else:
            a_ref, b_ref, r_ref, o_ref = refs[:4]
        p = _dot(a_ref[...].astype(BF16), b_ref[...].astype(BF16), dims)

        def finish(acc):
            if r_ref is not None:
                acc = acc + res_scale * r_ref[...]
            o_ref[...] = acc.astype(out_dtype)

        if nk == 1:
            finish(p)
        else:
            acc_ref = refs[-1]
            k = pl.program_id(2)

            @pl.when(k == 0)
            def _():
                acc_ref[...] = p

            @pl.when(k > 0)
            def _():
                acc_ref[...] += p

            @pl.when(k == nk - 1)
            def _():
                finish(acc_ref[...])

    if mode == 'nn':
        a_spec = pl.BlockSpec((tm, tk), lambda i, j, k: (i, k))
        b_spec = pl.BlockSpec((tk, tn), lambda i, j, k: (k, j))
    elif mode == 'nt':
        a_spec = pl.BlockSpec((tm, tk), lambda i, j, k: (i, k))
        b_spec = pl.BlockSpec((tn, tk), lambda i, j, k: (j, k))
    else:
        a_spec = pl.BlockSpec((tk, tm), lambda i, j, k: (k, i))
        b_spec = pl.BlockSpec((tk, tn), lambda i, j, k: (k, j))
    o_spec = pl.BlockSpec((tm, tn), lambda i, j, k: (i, j))
    in_specs = [a_spec, b_spec] + ([o_spec] if res is not None else [])
    args = (a, b) + ((res,) if res is not None else ())
    return pl.pallas_call(
        body, name=name, out_shape=_sds((M, N), out_dtype), grid=(M // tm, N // tn, nk),
        in_specs=in_specs, out_specs=o_spec,
        scratch_shapes=[pltpu.VMEM((tm, tn), F32)] if nk > 1 else [],
        compiler_params=_cp("parallel", "parallel", "arbitrary"))(*args)


def _mm_nt_cat(a_list, b_list, res, res_scale, *, name, tm=512, tn=512):
    M, N = a_list[0].shape[0], b_list[0].shape[0]
    tm, tn = _pick(M, tm, 8), _pick(N, tn, LANES)
    n = len(a_list)

    def body(*refs):
        a_refs, b_refs, r_ref, o_ref = refs[:n], refs[n:2 * n], refs[2 * n], refs[2 * n + 1]
        acc = res_scale * r_ref[...]
        for i in range(n):
            acc = acc + _dot(a_refs[i][...].astype(BF16), b_refs[i][...].astype(BF16), _NT)
        o_ref[...] = acc

    a_specs = [pl.BlockSpec((tm, a.shape[1]), lambda j, m: (m, 0)) for a in a_list]
    b_specs = [pl.BlockSpec((tn, b.shape[1]), lambda j, m: (j, 0)) for b in b_list]
    blk = pl.BlockSpec((tm, tn), lambda j, m: (m, j))
    return pl.pallas_call(
        body, name=name, out_shape=_sds((M, N), F32), grid=(N // tn, M // tm), in_specs=a_specs + b_specs + [blk],
        out_specs=blk, compiler_params=_cp("parallel", "parallel"))(*a_list, *b_list, res)


def _mm_res_ln(a, w, res, g, b, *, name, tm=512, tk=1408):
    (M, K), (_, N) = a.shape, w.shape
    tm, tk = _pick(M, tm, 8), _pick(K, tk, LANES)
    if 2 * (tm * K * a.dtype.itemsize + K * N * w.dtype.itemsize + tm * N * 14) <= WHOLE_K_VMEM:
        tk = K
    nk = K // tk

    def body(a_ref, w_ref, r_ref, g_ref, b_ref, h_ref, s_ref, hb_ref, acc_ref):
        k = pl.program_id(1)
        p = _dot(a_ref[...].astype(BF16), w_ref[...].astype(BF16), _NN)

        def finish(acc):
            s = DN_ALPHA * r_ref[...] + acc
            xhat, _ = _ln_stats(s)
            s_ref[...] = s
            h = xhat * g_ref[...] + b_ref[...]
            h_ref[...] = h
            hb_ref[...] = h.astype(BF16)

        if nk == 1:
            finish(p)
            return

        @pl.when(k == 0)
        def _():
            acc_ref[...] = p

        @pl.when(k > 0)
        def _():
            acc_ref[...] += p

        @pl.when(k == nk - 1)
        def _():
            finish(acc_ref[...])

    row = pl.BlockSpec((tm, N), lambda i, k: (i, 0))
    vec = pl.BlockSpec((1, N), lambda i, k: (0, 0))
    return pl.pallas_call(
        body, name=name, out_shape=(_sds((M, N), F32), _sds((M, N), F32), _sds((M, N), BF16)), grid=(M // tm, nk),
        in_specs=[pl.BlockSpec((tm, tk), lambda i, k: (i, k)), pl.BlockSpec((tk, N), lambda i, k: (k, 0)), row, vec, vec],
        out_specs=(row, row, row), scratch_shapes=[pltpu.VMEM((tm, N), F32)],
        compiler_params=_cp("parallel", "arbitrary"))(a, w, res, g, b)


def _ln_fwd(x, g, b, *, name):
    T, D = x.shape
    tt = _tile(T, 512)

    def body(x_ref, g_ref, b_ref, o_ref, ob_ref):
        xhat, _ = _ln_stats(x_ref[...])
        h = xhat * g_ref[...] + b_ref[...]
        o_ref[...] = h
        ob_ref[...] = h.astype(BF16)

    row = pl.BlockSpec((tt, D), lambda i: (i, 0))
    vec = pl.BlockSpec((1, D), lambda i: (0, 0))
    return pl.pallas_call(body, name=name, out_shape=(_sds((T, D), F32), _sds((T, D), BF16)), grid=(T // tt,),
                          in_specs=[row, vec, vec], out_specs=(row, row), compiler_params=_cp("parallel"))(x, g, b)


def _ln_bwd(s, dy, g, b, *, silu, name):
    T, D = s.shape
    tt = _tile(T, 512)

    def body(s_ref, dy_ref, g_ref, b_ref, ds_ref, dg_ref, db_ref):
        i = pl.program_id(0)
        xhat, rstd = _ln_stats(s_ref[...])
        gg = g_ref[...]
        dyl = dy_ref[...]
        if silu:
            _, dsilu = _silu_and_grad(xhat * gg + b_ref[...])
            dyl = dyl * dsilu
        dxh = dyl * gg
        m1 = jnp.mean(dxh, axis=-1, keepdims=True)
        m2 = jnp.mean(dxh * xhat, axis=-1, keepdims=True)
        ds_ref[...] = rstd * (dxh - m1 - xhat * m2)

        @pl.when(i == 0)
        def _():
            dg_ref[...] = jnp.zeros_like(dg_ref)
            db_ref[...] = jnp.zeros_like(db_ref)

        dg_ref[...] += jnp.sum(dyl * xhat, axis=0, keepdims=True)
        db_ref[...] += jnp.sum(dyl, axis=0, keepdims=True)

    row = pl.BlockSpec((tt, D), lambda i: (i, 0))
    vec = pl.BlockSpec((1, D), lambda i: (0, 0))
    return pl.pallas_call(body, name=name, out_shape=(_sds((T, D), F32), _sds((1, D), F32), _sds((1, D), F32)),
                          grid=(T // tt,), in_specs=[row, row, vec, vec], out_specs=(row, vec, vec),
                          compiler_params=_cp("arbitrary"))(s, dy, g, b)


def _loss_fwd_bwd(h, tgt, *, name):
    T, D = h.shape
    tt = _tile(T, 512)

    def body(h_ref, t_ref, dh_ref, l_ref):
        i = pl.program_id(0)
        e = h_ref[...] - t_ref[...]
        dh_ref[...] = e * (1.0 / D)

        @pl.when(i == 0)
        def _():
            l_ref[...] = jnp.zeros_like(l_ref)

        part = jnp.sum(jnp.sum(e * e, axis=1, keepdims=True), axis=0, keepdims=True) * (0.5 / D)
        l_ref[...] += jnp.broadcast_to(part, l_ref.shape)

    row = pl.BlockSpec((tt, D), lambda i: (i, 0))
    one = pl.BlockSpec((8, LANES), lambda i: (0, 0))
    dh, l = pl.pallas_call(body, name=name, out_shape=(_sds((T, D), F32), _sds((8, LANES), F32)), grid=(T // tt,),
                           in_specs=[row, row], out_specs=(row, one), compiler_params=_cp("arbitrary"))(h, tgt)
    return l[0:1, 0:1], dh


def _halo_rows(k):
    return 32 if k > 17 else 16


def _shifted(ext, K, first):
    rolled = {0: ext}
    out = []
    for k in range(K):
        r = (first + k) % 8
        if r not in rolled:
            rolled[r] = pltpu.roll(ext, ext.shape[0] - r, 0)
        out.append((rolled[r], first + k - r))
    return out


def _taps(ext, w_ref, sl, K, first, n, reverse=False):
    acc = None
    for k, (z, base) in enumerate(_shifted(ext, K, first)):
        kw = K - 1 - k if reverse else k
        term = w_ref[kw:kw + 1, sl] * z[base:base + n]
        acc = term if acc is None else acc + term
    return acc


def _tap_sums(d, ext, dw_ref, sl, K, first):
    n = d.shape[0]
    for k, (z, base) in enumerate(_shifted(ext, K, first)):
        dw_ref[k:k + 1, sl] += jnp.sum(d * z[base:base + n], axis=0, keepdims=True)


def _strip_width(width):
    return LANES if width % LANES == 0 else width


def _for_strips(width, fn):
    sw = _strip_width(width)

    def step(s, carry):
        fn(pl.ds(pl.multiple_of(s * sw, sw), sw), s)
        return carry

    lax.fori_loop(0, width // sw, step, 0)


def _prev_rows(tt, hb):
    return lambda t: jnp.maximum(t * (tt // hb) - 1, 0)


def _next_rows(tt, hb, T):
    return lambda t: jnp.minimum((t + 1) * (tt // hb), T // hb - 1)


def _conf_fwd(u, w, b, g, beta, *, D, name):
    T = u.shape[0]
    K = w.shape[0]
    tt, hb, tc = _tile(T, CONV_ROWS), _halo_rows(K), _tile(D, GLU_TILE)
    sw = _strip_width(tc)
    per_tile = tc // sw
    prev = _prev_rows(tt, hb)

    def body(u_ref, uh_ref, w_ref, b_ref, g_ref, beta_ref, v1_ref, v3_ref):
        i = pl.program_id(0)

        def strip(sl, s):
            a0 = (s // per_tile) * (2 * tc) + (s % per_tile) * sw
            a_sl, g_sl = pl.ds(pl.multiple_of(a0, sw), sw), pl.ds(pl.multiple_of(a0 + tc, sw), sw)
            halo = jnp.where(i > 0, _f32(uh_ref[:, a_sl]) * _sigmoid(_f32(uh_ref[:, g_sl])), 0.0)
            ext = jnp.concatenate([halo, _f32(u_ref[:, a_sl]) * _sigmoid(_f32(u_ref[:, g_sl]))], axis=0)
            v1_ref[:, sl] = _taps(ext, w_ref, sl, K, hb - (K - 1), tt) + b_ref[:, sl]

        _for_strips(D, strip)
        xhat, _ = _ln_stats(v1_ref[...])
        v2 = xhat * g_ref[...] + beta_ref[...]
        v3_ref[...] = (v2 * _sigmoid(v2)).astype(BF16)

    row = pl.BlockSpec((tt, D), lambda t: (t, 0))
    vec = pl.BlockSpec((1, D), lambda t: (0, 0))
    return pl.pallas_call(
        body, name=name, out_shape=(_sds((T, D), F32), _sds((T, D), BF16)), grid=(T // tt,),
        in_specs=[pl.BlockSpec((tt, 2 * D), lambda t: (t, 0)), pl.BlockSpec((hb, 2 * D), lambda t: (prev(t), 0)),
                  pl.BlockSpec((K, D), lambda t: (0, 0)), vec, vec, vec],
        out_specs=(row, row), compiler_params=_cp("parallel"))(u, u, w, b, g, beta)


def _conv_act_fwd(x, off, w, b, *, ffn, name):
    T = x.shape[0]
    K, Cw = w.shape
    tt, hb = _tile(T, CONV_ROWS), _halo_rows(K)
    tc = _tile(Cw // 2, FFN_TILE) if ffn else _pick(math.gcd(Cw, off), 1536, LANES)
    xw = 2 * tc if ffn else tc
    assert off % xw == 0
    ob = off // xw
    prev = _prev_rows(tt, hb)

    def body(x_ref, xh_ref, w_ref, b_ref, o_ref):
        first = pl.program_id(0) == 0

        def pre_of(sl):
            ext = jnp.concatenate([jnp.where(first, 0.0, _f32(xh_ref[:, sl])), _f32(x_ref[:, sl])], axis=0)
            return _taps(ext, w_ref, sl, K, hb - (K - 1), tt) + b_ref[:, sl]

        def strip(sl, s):
            if ffn:
                gate = pre_of(sl)
                val = pre_of(pl.ds(pl.multiple_of(tc + s * sw, sw), sw))
                o_ref[:, sl] = (gate * _sigmoid(gate) * val).astype(BF16)
            else:
                pre = pre_of(sl)
                o_ref[:, sl] = pre * _sigmoid(pre)

        _for_strips(tc, strip)

    sw = _strip_width(tc)
    return pl.pallas_call(
        body, name=name, out_shape=_sds((T, Cw // 2), BF16) if ffn else _sds((T, Cw), F32), grid=(T // tt, Cw // xw),
        in_specs=[pl.BlockSpec((tt, xw), lambda t, c: (t, c + ob)), pl.BlockSpec((hb, xw), lambda t, c: (prev(t), c + ob)),
                  pl.BlockSpec((K, xw), lambda t, c: (0, c)), pl.BlockSpec((1, xw), lambda t, c: (0, c))],
        out_specs=pl.BlockSpec((tt, tc), lambda t, c: (t, c)),
        compiler_params=_cp("parallel", "parallel"))(x, x, w, b)


def _conv_act_bwd(x, off, w, b, dout, *, ffn, name):
    T = x.shape[0]
    K, Cw = w.shape
    tt, hb = _tile(T, CONV_ROWS), _halo_rows(K)
    tc = _tile(Cw // 2, FFN_TILE) if ffn else _pick(math.gcd(Cw, off), 1536, LANES)
    xw = 2 * tc if ffn else tc
    assert off % xw == 0
    ob = off // xw
    nt = T // tt
    prev, nxt = _prev_rows(tt, hb), _next_rows(tt, hb, T)

    sw = _strip_width(tc)

    def body(x_ref, xp_ref, xn_ref, d_ref, dn_ref, w_ref, b_ref, dx_ref, dw_ref, db_ref):
        t = pl.program_id(1)

        @pl.when(t == 0)
        def _():
            dw_ref[...] = jnp.zeros_like(dw_ref)
            db_ref[...] = jnp.zeros_like(db_ref)

        def ext_of(sl):
            return jnp.concatenate([jnp.where(t == 0, 0.0, _f32(xp_ref[:, sl])), _f32(x_ref[:, sl]), _f32(xn_ref[:, sl])], axis=0)

        def pre_of(ext, sl):
            return _taps(ext, w_ref, sl, K, hb - (K - 1), tt + hb) + b_ref[:, sl]

        def finish(ext, dpre, sl):
            dx_ref[:, sl] = _taps(dpre, w_ref, sl, K, 0, tt, reverse=True).astype(BF16)
            dp = dpre[0:tt]
            _tap_sums(dp, ext, dw_ref, sl, K, hb - (K - 1))
            db_ref[:, sl] += jnp.sum(dp, axis=0, keepdims=True)

        def strip(sl, s):
            d = jnp.concatenate([d_ref[:, sl], jnp.where(t == nt - 1, 0.0, dn_ref[:, sl])], axis=0)
            if ffn:
                vsl = pl.ds(pl.multiple_of(tc + s * sw, sw), sw)
                eg, ev = ext_of(sl), ext_of(vsl)
                sg, dsg = _silu_and_grad(pre_of(eg, sl))
                val = pre_of(ev, vsl)
                finish(eg, d * val * dsg, sl)
                finish(ev, d * sg, vsl)
            else:
                ext = ext_of(sl)
                finish(ext, d * _silu_and_grad(pre_of(ext, sl))[1], sl)

        _for_strips(tc, strip)

    return pl.pallas_call(
        body, name=name, out_shape=(_sds((T, Cw), BF16), _sds((K, Cw), F32), _sds((1, Cw), F32)),
        grid=(Cw // xw, nt),
        in_specs=[pl.BlockSpec((tt, xw), lambda c, t: (t, c + ob)), pl.BlockSpec((hb, xw), lambda c, t: (prev(t), c + ob)),
                  pl.BlockSpec((hb, xw), lambda c, t: (nxt(t), c + ob)),
                  pl.BlockSpec((tt, tc), lambda c, t: (t, c)), pl.BlockSpec((hb, tc), lambda c, t: (nxt(t), c)),
                  pl.BlockSpec((K, xw), lambda c, t: (0, c)), pl.BlockSpec((1, xw), lambda c, t: (0, c))],
        out_specs=(pl.BlockSpec((tt, xw), lambda c, t: (t, c)), pl.BlockSpec((K, xw), lambda c, t: (0, c)),
                   pl.BlockSpec((1, xw), lambda c, t: (0, c))),
        compiler_params=_cp("parallel", "arbitrary"))(x, x, x, dout, dout, w, b)


def _glu_conv_bwd(dpre, u, w, *, name):
    T, C = dpre.shape
    K = w.shape[0]
    tt, hb, tc = _tile(T, CONV_ROWS), _halo_rows(K), _tile(C, GLU_TILE)
    nt = T // tt
    prev, nxt = _prev_rows(tt, hb), _next_rows(tt, hb, T)

    sw = _strip_width(tc)

    def body(d_ref, dn_ref, x_ref, xh_ref, w_ref, dx_ref, dw_ref, db_ref):
        t = pl.program_id(1)

        @pl.when(t == 0)
        def _():
            dw_ref[...] = jnp.zeros_like(dw_ref)
            db_ref[...] = jnp.zeros_like(db_ref)

        def strip(sl, s):
            gsl = pl.ds(pl.multiple_of(tc + s * sw, sw), sw)
            d = d_ref[:, sl]
            dext = jnp.concatenate([d, jnp.where(t == nt - 1, 0.0, dn_ref[:, sl])], axis=0)
            dxin = _taps(dext, w_ref, sl, K, 0, tt, reverse=True)
            a, sg = _f32(x_ref[:, sl]), _sigmoid(_f32(x_ref[:, gsl]))
            v0 = a * sg
            dx_ref[:, sl] = (dxin * sg).astype(BF16)
            dx_ref[:, gsl] = (dxin * v0 * (1.0 - sg)).astype(BF16)
            halo = jnp.where(t == 0, 0.0, _f32(xh_ref[:, sl]) * _sigmoid(_f32(xh_ref[:, gsl])))
            _tap_sums(d, jnp.concatenate([halo, v0], axis=0), dw_ref, sl, K, hb - (K - 1))
            db_ref[:, sl] += jnp.sum(d, axis=0, keepdims=True)

        _for_strips(tc, strip)

    return pl.pallas_call(
        body, name=name, out_shape=(_sds((T, 2 * C), BF16), _sds((K, C), F32), _sds((1, C), F32)), grid=(C // tc, nt),
        in_specs=[pl.BlockSpec((tt, tc), lambda c, t: (t, c)), pl.BlockSpec((hb, tc), lambda c, t: (nxt(t), c)),
                  pl.BlockSpec((tt, 2 * tc), lambda c, t: (t, c)), pl.BlockSpec((hb, 2 * tc), lambda c, t: (prev(t), c)),
                  pl.BlockSpec((K, tc), lambda c, t: (0, c))],
        out_specs=(pl.BlockSpec((tt, 2 * tc), lambda c, t: (t, c)), pl.BlockSpec((K, tc), lambda c, t: (0, c)),
                   pl.BlockSpec((1, tc), lambda c, t: (0, c))),
        compiler_params=_cp("parallel", "arbitrary"))(dpre, dpre, u, u, w)


def _softplus(x):
    t = jnp.exp(-jnp.abs(x))
    u = 1.0 + t
    log1p = jnp.where(u == 1.0, t, jnp.log(u) * t / jnp.where(u == 1.0, 1.0, u - 1.0))
    return jnp.maximum(x, 0.0) + log1p


def _dt_fwd(udt, bias, *, H, name):
    T = udt.shape[0]
    tt = _tile(T, 1024)

    def body(u_ref, b_ref, o_ref):
        lane = lax.broadcasted_iota(jnp.int32, (tt, LANES), 1)
        o_ref[...] = jnp.where(lane < H, _softplus(u_ref[...] + b_ref[...]), 0.0)

    row = pl.BlockSpec((tt, LANES), lambda i: (i, 0))
    return pl.pallas_call(body, name=name, out_shape=_sds((T, LANES), F32), grid=(T // tt,),
                          in_specs=[row, pl.BlockSpec((1, LANES), lambda i: (0, 0))], out_specs=row,
                          compiler_params=_cp("parallel"))(udt, bias)


def _dt_bwd(da, ddtx, dt, udt, bias, a_col, *, H, name):
    T = da.shape[0]
    tt = _tile(T, 1024)

    def body(da_ref, dx_ref, dt_ref, u_ref, b_ref, a_ref, draw_ref, dbias_ref, dalog_ref):
        i = pl.program_id(0)
        lane = lax.broadcasted_iota(jnp.int32, (tt, LANES), 1)
        dav = da_ref[...]
        ddt = dav * a_ref[...] + dx_ref[...]
        draw = jnp.where(lane < H, ddt * _sigmoid(u_ref[...] + b_ref[...]), 0.0)
        draw_ref[...] = draw.astype(BF16)

        @pl.when(i == 0)
        def _():
            dbias_ref[...] = jnp.zeros_like(dbias_ref)
            dalog_ref[...] = jnp.zeros_like(dalog_ref)

        dbias_ref[...] += jnp.sum(draw, axis=0, keepdims=True)
        dalog_ref[...] += jnp.sum(dav * dt_ref[...], axis=0, keepdims=True) * a_ref[...]

    row = pl.BlockSpec((tt, LANES), lambda i: (i, 0))
    vec = pl.BlockSpec((1, LANES), lambda i: (0, 0))
    return pl.pallas_call(body, name=name,
                          out_shape=(_sds((T, LANES), BF16), _sds((1, LANES), F32), _sds((1, LANES), F32)),
                          grid=(T // tt,), in_specs=[row, row, row, row, vec, vec], out_specs=(row, vec, vec),
                          compiler_params=_cp("arbitrary"))(da, ddtx, dt, udt, bias, a_col)


def _ssd_consts():
    L = SSM_CHUNK
    r = lax.broadcasted_iota(jnp.int32, (L, L), 0)
    c = lax.broadcasted_iota(jnp.int32, (L, L), 1)
    return r, c


def _ssd_chunk_decays(dtc_ref, dtr_ref, acol_ref, arow_ref, cs_ref, csr_ref, r, c):
    L = SSM_CHUNK
    dtc = dtc_ref[...]
    tril = (r >= c).astype(F32)
    triu = (r <= c).astype(F32)
    cs_ref[...] = jnp.dot(tril, dtc * acol_ref[...], precision=HIGHEST, preferred_element_type=F32)
    csr_ref[...] = jnp.dot(dtr_ref[...] * arow_ref[...], triu, precision=HIGHEST, preferred_element_type=F32)
    cs = cs_ref[...]
    cs_last = cs_ref[L - 1:L, :]
    return dtc, cs, jnp.exp(cs), jnp.exp(cs_last - cs), jnp.exp(cs_last), triu


def _head_spread(arrs, g, heads_per_group):
    P = SSM_HEAD_DIM
    gw = heads_per_group * P
    head = lax.broadcasted_iota(jnp.int32, (LANES, gw), 0)
    lane = lax.broadcasted_iota(jnp.int32, (LANES, gw), 1)
    spread = (head == g * heads_per_group + lane // P).astype(BF16)
    out = _dot(jnp.concatenate(arrs, axis=0).astype(BF16), spread, _NN)
    L = arrs[0].shape[0]
    return [out[i * L:(i + 1) * L] for i in range(len(arrs))]


def _ssd_fwd(xbc, dt, dtT, a_col, a_row, d_col, *, H, name):
    T, XBC = xbc.shape
    L, N, G, P = SSM_CHUNK, SSM_STATE, SSM_GROUPS, SSM_HEAD_DIM
    SX = H * P
    HR = dtT.shape[0]
    nc = T // L
    heads_per_group = H // G

    def body(x_ref, dtc_ref, dtr_ref, acol_ref, arow_ref, d_ref, y_ref, hst_ref, state, cs_ref, csr_ref):
        ci = pl.program_id(0)

        @pl.when(ci == 0)
        def _():
            state[...] = jnp.zeros_like(state)

        hst_ref[0] = state[...]
        r, c = _ssd_consts()
        tri = r >= c
        lane_lo = c < P
        row_lo = r < P
        lane1_lo = lax.broadcasted_iota(jnp.int32, (1, LANES), 1) < P
        dtc, cs, e, ds, cd, _ = _ssd_chunk_decays(dtc_ref, dtr_ref, acol_ref, arow_ref, cs_ref, csr_ref, r, c)
        dsk = d_ref[...]

        for g in range(G):
            Bg = x_ref[:, SX + g * N:SX + (g + 1) * N].astype(BF16)
            Cg = x_ref[:, SX + G * N + g * N:SX + G * N + (g + 1) * N].astype(BF16)
            Gm = _dot(Cg, Bg, _NT)
            dt_g, e_g, ds_g = _head_spread([dtc, e, ds], g, heads_per_group)
            st = []
            for j in range(g * heads_per_group // 2, (g + 1) * heads_per_group // 2):
                h0 = 2 * j
                sl = slice(2 * P * j, 2 * P * (j + 1))
                gl = slice(sl.start - g * heads_per_group * P, sl.stop - g * heads_per_group * P)
                x2 = x_ref[:, sl]
                X2 = x2 * dt_g[:, gl]
                H2 = state[sl, :]
                st.append(dict(h0=h0, sl=sl, x2=x2, X2=X2, H2=H2, e2=e_g[:, gl], R2=_dot(Cg, H2.astype(BF16), _NT),
                               S2=_dot((X2 * ds_g[:, gl]).astype(BF16), Bg, _TN)))
            for s in st:
                ms = []
                for h in (s['h0'], s['h0'] + 1):
                    seg = cs[:, h:h + 1] - csr_ref[h:h + 1, :]
                    ms.append((Gm * jnp.where(tri, jnp.exp(jnp.where(tri, seg, 0.0)), 0.0)).astype(BF16))
                xst = jnp.concatenate([jnp.where(lane_lo, s['X2'], 0.0), jnp.where(lane_lo, 0.0, s['X2'])], axis=0).astype(BF16)
                s['yd'] = _dot(jnp.concatenate(ms, axis=1), xst, _NN)
            for s in st:
                h0, sl = s['h0'], s['sl']
                dsk2 = jnp.where(lane1_lo, dsk[:, h0:h0 + 1], dsk[:, h0 + 1:h0 + 2])
                y_ref[:, sl] = s['yd'] + s['e2'] * s['R2'] + s['x2'] * dsk2
                state[sl, :] = jnp.where(row_lo, cd[:, h0:h0 + 1], cd[:, h0 + 1:h0 + 2]) * s['H2'] + s['S2']

    vec = pl.BlockSpec((1, LANES), lambda i: (0, 0))
    return pl.pallas_call(
        body, name=name, out_shape=(_sds((T, SX), F32), _sds((nc, SX, N), F32)), grid=(nc,),
        in_specs=[pl.BlockSpec((L, XBC), lambda i: (i, 0)), pl.BlockSpec((L, LANES), lambda i: (i, 0)),
                  pl.BlockSpec((HR, L), lambda i: (0, i)), vec, pl.BlockSpec((HR, 1), lambda i: (0, 0)), vec],
        out_specs=(pl.BlockSpec((L, SX), lambda i: (i, 0)), pl.BlockSpec((1, SX, N), lambda i: (i, 0, 0))),
        scratch_shapes=[pltpu.VMEM((SX, N), F32), pltpu.VMEM((L, LANES), F32), pltpu.VMEM((HR, L), F32)],
        compiler_params=_cp("arbitrary"))(xbc, dt, dtT, a_col, a_row, d_col)


def _ssd_bwd(xbc, dt, dtT, a_col, a_row, d_col, hst, dy, *, H, name):
    T, XBC = xbc.shape
    L, N, G, P = SSM_CHUNK, SSM_STATE, SSM_GROUPS, SSM_HEAD_DIM
    SX = H * P
    HR = dtT.shape[0]
    nc = T // L
    heads_per_group = H // G

    def body(x_ref, dtc_ref, dtr_ref, acol_ref, arow_ref, d_ref, hst_ref, dy_ref,
             dx_ref, da_ref, ddtx_ref, dd_ref, dstate, cs_ref, csr_ref):
        ci = pl.program_id(0)

        @pl.when(ci == 0)
        def _():
            dstate[...] = jnp.zeros_like(dstate)
            dd_ref[...] = jnp.zeros_like(dd_ref)

        r, c = _ssd_consts()
        tri = r >= c
        lane_lo = c < P
        row_lo = r < P
        lane1 = lax.broadcasted_iota(jnp.int32, (1, LANES), 1)
        rowc = lax.broadcasted_iota(jnp.int32, (L, 1), 0)
        dtc, cs, e, ds, cd, triu = _ssd_chunk_decays(dtc_ref, dtr_ref, acol_ref, arow_ref, cs_ref, csr_ref, r, c)
        triu_b = triu.astype(BF16)
        dsk = d_ref[...]
        last_row = rowc == L - 1

        triT = r <= c

        def halves(v, axis):
            return jnp.concatenate([jnp.where(lane_lo, v, 0.0), jnp.where(lane_lo, 0.0, v)], axis=axis)

        def head_sum(v, h):
            lo = jnp.sum(jnp.where(lane1 < P, v, 0.0), axis=1, keepdims=True) * (lane1 == h).astype(F32)
            hi = jnp.sum(jnp.where(lane1 < P, 0.0, v), axis=1, keepdims=True) * (lane1 == h + 1).astype(F32)
            return lo + hi

        GW = heads_per_group * P
        gl = lax.broadcasted_iota(jnp.int32, (GW, LANES), 0)
        gc = lax.broadcasted_iota(jnp.int32, (GW, LANES), 1)
        wl = lax.broadcasted_iota(jnp.int32, (heads_per_group * L, LANES), 0)
        wc = lax.broadcasted_iota(jnp.int32, (heads_per_group * L, LANES), 1)
        wide_r = lax.broadcasted_iota(jnp.int32, (L, heads_per_group * L), 0)
        wide_c = lax.broadcasted_iota(jnp.int32, (L, heads_per_group * L), 1)
        below_diag = (wide_c % L) < wide_r
        sums = jnp.zeros((3 * L, LANES), F32)
        da_q = jnp.zeros((L, LANES), F32)
        dcd_acc = jnp.zeros((1, LANES), F32)
        dd_acc = jnp.zeros((1, LANES), F32)
        for g in range(G):
            bsl = slice(SX + g * N, SX + (g + 1) * N)
            csl = slice(SX + G * N + g * N, SX + G * N + (g + 1) * N)
            Bg = x_ref[:, bsl].astype(BF16)
            Cg = x_ref[:, csl].astype(BF16)
            Gm = _dot(Cg, Bg, _NT)
            GmT = _dot(Bg, Cg, _NT)
            dt_g, e_g, ds_g = _head_spread([dtc, e, ds], g, heads_per_group)
            st = []
            for j in range(g * heads_per_group // 2, (g + 1) * heads_per_group // 2):
                h0 = 2 * j
                sl = slice(2 * P * j, 2 * P * (j + 1))
                s = dict(h0=h0, sl=sl, x2=x_ref[:, sl], dy2=dy_ref[:, sl], H2=hst_ref[0, sl, :], dHn=dstate[sl, :])
                gsl = slice(sl.start - g * heads_per_group * P, sl.stop - g * heads_per_group * P)
                s['dt2'], s['e2'], s['ds2'] = dt_g[:, gsl], e_g[:, gsl], ds_g[:, gsl]
                s['X2'] = s['x2'] * s['dt2']
                s['H2b'], s['dHnb'] = s['H2'].astype(BF16), s['dHn'].astype(BF16)
                st.append(s)
            for s in st:
                s['R2'] = _dot(Cg, s['H2b'], _NT)
                s['dXd'] = _dot(Bg, s['dHnb'], _NT)
                s['dM2'] = _dot(s['dy2'].astype(BF16), halves(s['X2'], 0).astype(BF16), _NT)
            dG = jnp.zeros((L, L), F32)
            qs = []
            for s in st:
                mts = []
                for i, h in enumerate((s['h0'], s['h0'] + 1)):
                    z = cs[:, h:h + 1] - csr_ref[h:h + 1, :]
                    Dm = jnp.where(tri, jnp.exp(jnp.where(tri, z, 0.0)), 0.0)
                    DmT = jnp.where(triT, jnp.exp(jnp.where(triT, -z, 0.0)), 0.0)
                    dM = s['dM2'][:, i * L:(i + 1) * L]
                    dG = dG + dM * Dm
                    qs.append((dM * (Gm * Dm)).astype(BF16))
                    mts.append((GmT * DmT).astype(BF16))
                s['dXm'] = _dot(jnp.concatenate(mts, axis=1), halves(s['dy2'], 0).astype(BF16), _NN)
            Wg = _dot(triu_b, jnp.concatenate(qs, axis=1), _NN)
            place = (wc == g * heads_per_group + wl // L).astype(BF16)
            da_q = da_q + _dot(jnp.where(below_diag, Wg, 0.0).astype(BF16), place, _NN)
            dBg = jnp.zeros((L, N), F32)
            dCg = jnp.zeros((L, N), F32)
            des, ddss, dxxs = [], [], []
            for s in st:
                h0, sl, x2, dy2, X2 = s['h0'], s['sl'], s['x2'], s['dy2'], s['X2']
                dR2b = (s['e2'] * dy2).astype(BF16)
                dCg = dCg + _dot(dR2b, s['H2b'], _NN)
                dHr = _dot(dR2b, Cg, _TN)
                dBg = dBg + _dot((X2 * s['ds2']).astype(BF16), s['dHnb'], _NN)
                dX2 = s['ds2'] * s['dXd'] + s['dXm']
                des.append(dy2 * s['R2'])
                ddss.append(s['dXd'] * X2)
                dxxs.append(dX2 * x2)
                prod = s['dHn'] * s['H2']
                for i, h in enumerate((h0, h0 + 1)):
                    rows = jnp.sum(prod[i * P:(i + 1) * P], axis=0, keepdims=True)
                    dcd_acc = dcd_acc + jnp.sum(rows, axis=1, keepdims=True) * (lane1 == h).astype(F32)
                dd_acc = dd_acc + head_sum(jnp.sum(dy2 * x2, axis=0, keepdims=True), h0)
                dsk2 = jnp.where(lane1 < P, dsk[:, h0:h0 + 1], dsk[:, h0 + 1:h0 + 2])
                dx_ref[:, sl] = dX2 * s['dt2'] + dy2 * dsk2
                dstate[sl, :] = jnp.where(row_lo, cd[:, h0:h0 + 1], cd[:, h0 + 1:h0 + 2]) * s['dHn'] + dHr
            stack = jnp.concatenate([jnp.concatenate(v, axis=1) for v in (des, ddss, dxxs)], axis=0).astype(BF16)
            sums = sums + _dot(stack, (gc == g * heads_per_group + gl // P).astype(BF16), _NN)
            dGb = dG.astype(BF16)
            dx_ref[:, bsl] = dBg + _dot(dGb, Cg, _TN)
            dx_ref[:, csl] = dCg + _dot(dGb, Bg, _NN)
        t1 = sums[L:2 * L] * ds
        tail = jnp.sum(t1, axis=0, keepdims=True) + dcd_acc * cd
        dcs = sums[0:L] * e - t1 + jnp.where(last_row, tail, 0.0)
        da_ref[...] = jnp.dot(triu, dcs, precision=HIGHEST, preferred_element_type=F32) + da_q
        ddtx_ref[...] = sums[2 * L:3 * L]
        dd_ref[...] += dd_acc

    vec = pl.BlockSpec((1, LANES), lambda i: (0, 0))
    rev = lambda i: (nc - 1 - i, 0)
    return pl.pallas_call(
        body, name=name,
        out_shape=(_sds((T, XBC), F32), _sds((T, LANES), F32), _sds((T, LANES), F32), _sds((1, LANES), F32)),
        grid=(nc,),
        in_specs=[pl.BlockSpec((L, XBC), rev), pl.BlockSpec((L, LANES), rev),
                  pl.BlockSpec((HR, L), lambda i: (0, nc - 1 - i)), vec, pl.BlockSpec((HR, 1), lambda i: (0, 0)), vec,
                  pl.BlockSpec((1, SX, N), lambda i: (nc - 1 - i, 0, 0)), pl.BlockSpec((L, SX), rev)],
        out_specs=(pl.BlockSpec((L, XBC), rev), pl.BlockSpec((L, LANES), rev), pl.BlockSpec((L, LANES), rev), vec),
        scratch_shapes=[pltpu.VMEM((SX, N), F32), pltpu.VMEM((L, LANES), F32), pltpu.VMEM((HR, L), F32)],
        compiler_params=_cp("arbitrary"))(xbc, dt, dtT, a_col, a_row, d_col, hst, dy)


def _rms_fwd(y, u, zoff, w, *, name):
    T, SX = y.shape
    tt = _tile(T, 256)
    gs = SX // SSM_GROUPS
    zb = zoff // SX

    def body(y_ref, z_ref, w_ref, o_ref):
        for g in range(SSM_GROUPS):
            sl = slice(g * gs, (g + 1) * gs)
            z = _f32(z_ref[:, sl])
            yg = y_ref[:, sl] * (z * _sigmoid(z))
            rstd = lax.rsqrt(jnp.mean(yg * yg, axis=-1, keepdims=True) + RMS_EPS)
            o_ref[:, sl] = (yg * rstd * w_ref[:, sl]).astype(BF16)

    row = pl.BlockSpec((tt, SX), lambda i: (i, 0))
    return pl.pallas_call(body, name=name, out_shape=_sds((T, SX), BF16), grid=(T // tt,),
                          in_specs=[row, pl.BlockSpec((tt, SX), lambda i: (i, zb)), pl.BlockSpec((1, SX), lambda i: (0, 0))],
                          out_specs=row, compiler_params=_cp("parallel"))(y, u, w)


def _rms_bwd(y, u, zoff, w, dyn, *, name):
    T, SX = y.shape
    tt = _tile(T, 256)
    gs = SX // SSM_GROUPS
    zb = zoff // SX

    def body(y_ref, z_ref, w_ref, d_ref, dy_ref, dz_ref, dw_ref):
        i = pl.program_id(0)

        @pl.when(i == 0)
        def _():
            dw_ref[...] = jnp.zeros_like(dw_ref)

        for g in range(SSM_GROUPS):
            sl = slice(g * gs, (g + 1) * gs)
            z, yv, d = _f32(z_ref[:, sl]), y_ref[:, sl], d_ref[:, sl]
            sz, dsz = _silu_and_grad(z)
            yg = yv * sz
            rstd = lax.rsqrt(jnp.mean(yg * yg, axis=-1, keepdims=True) + RMS_EPS)
            t = yg * rstd
            dw_ref[:, sl] += jnp.sum(d * t, axis=0, keepdims=True)
            dt_ = d * w_ref[:, sl]
            dyg = rstd * (dt_ - t * jnp.mean(dt_ * t, axis=-1, keepdims=True))
            dy_ref[:, sl] = dyg * sz
            dz_ref[:, sl] = (dyg * yv * dsz).astype(BF16)

    row = pl.BlockSpec((tt, SX), lambda i: (i, 0))
    vec = pl.BlockSpec((1, SX), lambda i: (0, 0))
    return pl.pallas_call(body, name=name, out_shape=(_sds((T, SX), F32), _sds((T, SX), BF16), _sds((1, SX), F32)),
                          grid=(T // tt,), in_specs=[row, pl.BlockSpec((tt, SX), lambda i: (i, zb)), vec, row],
                          out_specs=(row, row, vec), compiler_params=_cp("arbitrary"))(y, u, w, dyn)


def _gate_fwd(u, goff, ya, yb, *, name):
    T, D = ya.shape
    tt = _tile(T, 512)
    gb = goff // D

    def body(ga_ref, gb_ref, ya_ref, yb_ref, o_ref):
        o_ref[...] = (_sigmoid(_f32(ga_ref[...])) * ya_ref[...] + _sigmoid(_f32(gb_ref[...])) * yb_ref[...]).astype(BF16)

    row = pl.BlockSpec((tt, D), lambda i: (i, 0))
    return pl.pallas_call(body, name=name, out_shape=_sds((T, D), BF16), grid=(T // tt,),
                          in_specs=[pl.BlockSpec((tt, D), lambda i: (i, gb)), pl.BlockSpec((tt, D), lambda i: (i, gb + 1)), row, row],
                          out_specs=row, compiler_params=_cp("parallel"))(u, u, ya, yb)


def _gate_bwd(u, goff, ya, yb, dm, *, name):
    T, D = ya.shape
    tt = _tile(T, 512)
    gb = goff // D

    def body(ga_ref, gb_ref, ya_ref, yb_ref, dm_ref, dya_ref, dyb_ref, dg_ref):
        d = dm_ref[...]
        sa, sb = _sigmoid(_f32(ga_ref[...])), _sigmoid(_f32(gb_ref[...]))
        dya_ref[...] = (d * sa).astype(BF16)
        dyb_ref[...] = (d * sb).astype(BF16)
        dg_ref[...] = jnp.concatenate([d * ya_ref[...] * sa * (1.0 - sa), d * yb_ref[...] * sb * (1.0 - sb)], axis=1).astype(BF16)

    row = pl.BlockSpec((tt, D), lambda i: (i, 0))
    return pl.pallas_call(body, name=name, out_shape=(_sds((T, D), BF16), _sds((T, D), BF16), _sds((T, 2 * D), BF16)),
                          grid=(T // tt,),
                          in_specs=[pl.BlockSpec((tt, D), lambda i: (i, gb)), pl.BlockSpec((tt, D), lambda i: (i, gb + 1)), row, row, row],
                          out_specs=(row, row, pl.BlockSpec((tt, 2 * D), lambda i: (i, 0))),
                          compiler_params=_cp("parallel"))(u, u, ya, yb, dm)


def _adamw_math(w, gg, m, v):
    nm = ADAM_B1 * m + (1.0 - ADAM_B1) * gg
    nv = ADAM_B2 * v + (1.0 - ADAM_B2) * (gg * gg)
    m_hat = nm / (1.0 - ADAM_B1 ** ADAM_STEP)
    v_hat = nv / (1.0 - ADAM_B2 ** ADAM_STEP)
    return -ADAM_LR * (m_hat / (jnp.sqrt(v_hat) + ADAM_EPS) + ADAM_WD * w), nm, nv


def _adamw(w, g, m, v, *, name):
    R, C = w.shape
    tr = _pick(R, 256, 8)

    def body(w_ref, g_ref, m_ref, v_ref, d_ref, nm_ref, nv_ref):
        d_ref[...], nm_ref[...], nv_ref[...] = _adamw_math(w_ref[...], g_ref[...], m_ref[...], v_ref[...])

    blk = pl.BlockSpec((tr, C), lambda i: (i, 0))
    out = _sds((R, C), F32)
    return pl.pallas_call(body, name=name, out_shape=(out, out, out), grid=(R // tr,), in_specs=[blk] * 4,
                          out_specs=(blk,) * 3, compiler_params=_cp("parallel"))(w, g, m, v)


def _sum_adamw(parts, w, m, v, *, name):
    depth, C = w.shape[0], w.shape[-1]
    R = w[0].size // C
    tr = _pick(R, 256, 8)

    def body(*refs):
        p_refs = refs[:depth]
        w_ref, m_ref, v_ref, g_ref, d_ref, nm_ref, nv_ref = refs[depth:]
        for l in range(depth):
            @pl.when(pl.program_id(0) == l)
            def _(l=l):
                gg = p_refs[l][0].astype(F32)
                for k in range(1, N_DEV):
                    gg = gg + p_refs[l][k].astype(F32)
                g_ref[...] = gg
                d_ref[...], nm_ref[...], nv_ref[...] = _adamw_math(w_ref[...], gg, m_ref[...], v_ref[...])

    p_specs = [pl.BlockSpec((N_DEV, tr, C), lambda l, i, ll=ll: (0, jnp.where(l == ll, i, 0), 0)) for ll in range(depth)]
    blk = pl.BlockSpec((None, tr, C), lambda l, i: (l, i, 0))
    out = _sds((depth, R, C), F32)
    res = pl.pallas_call(body, name=name, out_shape=(out,) * 4, grid=(depth, R // tr),
                         in_specs=p_specs + [blk, blk, blk], out_specs=(blk,) * 4,
                         compiler_params=_cp("parallel", "parallel"))(
        *[x.reshape(N_DEV, R, C) for x in parts], w.reshape(depth, R, C), m.reshape(depth, R, C), v.reshape(depth, R, C))
    return tuple(r.reshape(w.shape) for r in res)


def _sum_slots(x, *, name):
    n, R, C = x.shape
    tr = _tile(R, 512)

    def body(x_ref, o_ref):
        acc = x_ref[0].astype(F32)
        for k in range(1, n):
            acc = acc + x_ref[k].astype(F32)
        o_ref[...] = acc

    return pl.pallas_call(body, name=name, out_shape=_sds((R, C), F32), grid=(R // tr,),
                          in_specs=[pl.BlockSpec((n, tr, C), lambda i: (0, i, 0))],
                          out_specs=pl.BlockSpec((tr, C), lambda i: (i, 0)), compiler_params=_cp("parallel"))(x)


def _exchange(xs, *, scatter, name):
    n = len(xs)

    def body(*refs):
        x_refs, o_refs = refs[:n], refs[n:2 * n]
        send_sems, recv_sems, local_sems = refs[2 * n:]
        mx, my, mc = lax.axis_index("x"), lax.axis_index("y"), lax.axis_index("c")
        me = 4 * mx + 2 * my + mc

        def src(i, d):
            return x_refs[i].at[d] if scatter else x_refs[i]

        locals_ = [pltpu.make_async_copy(src(i, me), o_refs[i].at[me], local_sems.at[i]) for i in range(n)]
        for cp in locals_:
            cp.start()
        sends, recvs = [], []
        for k in range(1, N_DEV):
            px = 1 - mx if k & 4 else mx
            py = 1 - my if k & 2 else my
            pc = 1 - mc if k & 1 else mc
            peer = 4 * px + 2 * py + pc
            for i in range(n):
                common = dict(send_sem=send_sems.at[k - 1, i], recv_sem=recv_sems.at[k - 1, i],
                              device_id=(px, py, pc), device_id_type=pl.DeviceIdType.MESH)
                sends.append(pltpu.make_async_remote_copy(src_ref=src(i, peer), dst_ref=o_refs[i].at[me], **common))
                recvs.append(pltpu.make_async_remote_copy(src_ref=src(i, peer), dst_ref=o_refs[i].at[peer], **common))
        for cp in sends:
            cp.start()
        for cp in recvs:
            cp.wait_recv()
        for cp in sends:
            cp.wait_send()
        for cp in locals_:
            cp.wait()

    any_spec = pl.BlockSpec(memory_space=pl.ANY)
    out_shape = tuple(_sds(x.shape if scatter else (N_DEV,) + x.shape, x.dtype) for x in xs)
    return pl.pallas_call(
        body, name=name, out_shape=out_shape, in_specs=[any_spec] * n, out_specs=(any_spec,) * n,
        scratch_shapes=[pltpu.SemaphoreType.DMA((N_DEV - 1, n)), pltpu.SemaphoreType.DMA((N_DEV - 1, n)),
                        pltpu.SemaphoreType.DMA((n,))])(*xs)


def _to_rows(flat, row_mult):
    n = flat.shape[-1]
    per = row_mult * LANES
    pad = (-n) % per
    flat = jnp.pad(flat, [(0, 0)] * (flat.ndim - 1) + [(0, pad)])
    return flat.reshape(flat.shape[:-1] + ((n + pad) // LANES, LANES))


def _peers():
    mx, my, mc = lax.axis_index("x"), lax.axis_index("y"), lax.axis_index("c")
    out = []
    for k in range(1, N_DEV):
        px = 1 - mx if k & 4 else mx
        py = 1 - my if k & 2 else my
        pc = 1 - mc if k & 1 else mc
        out.append(((px, py, pc), 4 * px + 2 * py + pc))
    return 4 * mx + 2 * my + mc, out


_HBM = pl.BlockSpec(memory_space=pltpu.HBM)
_SEM = pl.BlockSpec(memory_space=pltpu.SEMAPHORE)


def _exchange_start(xs, *, scatter, name):
    n = len(xs)

    def body(*refs):
        x_refs, land_refs, send_sems, recv_sems, token = refs[:n], refs[n:2 * n], refs[2 * n], refs[2 * n + 1], refs[-1]
        me, peers = _peers()
        for k, (dev, peer) in enumerate(peers):
            for i in range(n):
                pltpu.make_async_remote_copy(
                    src_ref=x_refs[i].at[peer] if scatter else x_refs[i], dst_ref=land_refs[i].at[me],
                    send_sem=send_sems.at[k * n + i], recv_sem=recv_sems.at[k * n + i],
                    device_id=dev, device_id_type=pl.DeviceIdType.MESH).start()
        token[...] = jnp.zeros_like(token)

    land_shapes = [x.shape if scatter else (N_DEV,) + x.shape for x in xs]
    lands = [pltpu.with_memory_space_constraint(lax.empty(s, x.dtype), pltpu.HBM) for s, x in zip(land_shapes, xs)]
    srcs = [pltpu.with_memory_space_constraint(x, pltpu.HBM) for x in xs]
    out = pl.pallas_call(
        body, name=name,
        out_shape=(pltpu.SemaphoreType.DMA(((N_DEV - 1) * n,)), pltpu.SemaphoreType.DMA(((N_DEV - 1) * n,)),
                   *[pltpu.HBM(x.shape, x.dtype) for x in xs], *[pltpu.HBM(s, x.dtype) for s, x in zip(land_shapes, xs)],
                   _sds((8, LANES), F32)),
        in_specs=(_HBM,) * (2 * n), out_specs=(_SEM, _SEM) + (_HBM,) * (2 * n) + (pl.BlockSpec(memory_space=pltpu.VMEM),),
        input_output_aliases={i: 2 + i for i in range(2 * n)},
        compiler_params=pltpu.CompilerParams(has_side_effects=pltpu.SideEffectType.DATAFLOW_SIDE_EFFECTING))(*srcs, *lands)
    return (out[0], out[1], list(out[2:2 + n]), list(out[2 + n:2 + 2 * n])), out[-1]


def _exchange_wait(handle, after, *, scatter, name):
    send_sems, recv_sems, srcs, lands = handle
    n = len(srcs)

    def body(*refs):
        x_refs, land_refs, send_sems, recv_sems = refs[:n], refs[n:2 * n], refs[2 * n], refs[2 * n + 1]
        me, peers = _peers()
        for k, (dev, peer) in enumerate(peers):
            for i in range(n):
                cp = pltpu.make_async_remote_copy(
                    src_ref=x_refs[i].at[peer] if scatter else x_refs[i], dst_ref=land_refs[i].at[peer],
                    send_sem=send_sems.at[k * n + i], recv_sem=recv_sems.at[k * n + i],
                    device_id=dev, device_id_type=pl.DeviceIdType.MESH)
                cp.wait_send()
                cp.wait_recv()

    out = pl.pallas_call(
        body, name=name, out_shape=tuple(pltpu.HBM(a.shape, a.dtype) for a in srcs + lands),
        in_specs=(_HBM,) * (2 * n) + (_SEM, _SEM, pl.BlockSpec(memory_space=pl.ANY)), out_specs=(_HBM,) * (2 * n),
        input_output_aliases={i: i for i in range(2 * n)},
        compiler_params=pltpu.CompilerParams(has_side_effects=pltpu.SideEffectType.DATAFLOW_SIDE_EFFECTING))(
        *srcs, *lands, send_sems, recv_sems, after)
    return list(out[n:])


def _with_own_slot(lands, xs, *, scatter):
    me = 4 * lax.axis_index("x") + 2 * lax.axis_index("y") + lax.axis_index("c")
    out = []
    for land, x in zip(lands, xs):
        own = lax.dynamic_index_in_dim(x, me, 0, keepdims=True) if scatter else x[None]
        out.append(lax.dynamic_update_index_in_dim(land, own, me, 0))
    return out


def _wire_shards(p, l, names):
    return [p[k][l].astype(WIRE) if SHARDED[k][1] else p[k][l] for k in names]


def _full_weights(got, p, names):
    full = {}
    for k, g in zip(names, got):
        axis, shape = SHARDED[k][0] - 1, p[k].shape[1:]
        g = jnp.moveaxis(g, 0, axis)
        full[k] = g.reshape(shape[:axis] + (N_DEV * shape[axis],) + shape[axis + 1:])
    return full


def _grad_parts(grads, p, names):
    parts = []
    for k in names:
        axis, g = SHARDED[k][0] - 1, grads[k]
        g = g.reshape(g.shape[:axis] + (N_DEV, p[k].shape[1:][axis]) + g.shape[axis + 1:])
        parts.append(jnp.moveaxis(g, axis, 0).astype(WIRE))
    return parts


def _allreduce_small(vals):
    metas = [(k, v.shape, v.size) for k, v in vals.items()]
    flat = jnp.concatenate([v.reshape(-1) for v in vals.values()])
    got, = _exchange([_to_rows(flat, 8)], scatter=False, name="gather_small_grads")
    summed = _sum_slots(got, name="sum_small_grads").reshape(-1)
    out, off = {}, 0
    for k, shape, n in metas:
        out[k] = summed[off:off + n].reshape(shape)
        off += n
    return out


def _row(v):
    return v.reshape(1, -1).astype(F32)


def _lanes(v):
    return jnp.pad(v.astype(F32), (0, LANES - v.shape[0])).reshape(1, LANES)


EARLY = ('w_in', 'conv_dw_w', 'ssm_conv_w')
LATE = tuple(k for k in SHARDED if k not in EARLY)
FFN = ('w_ffn_up', 'ffn_dw_w', 'w_ffn_down')
NOT_FFN = tuple(k for k in SHARDED if k not in FFN)


def _layer_weights(p, full, l, dims):
    D, SX, XBC, H, FF = dims
    assert _tile(D, GLU_TILE) == D and _tile(FF, FFN_TILE) == FF
    W = {}
    if 'w_in' in full:
        w_in = full['w_in']
        o_dt = 2 * D + SX + XBC
        w_pieces = [w_in[:, :2 * D], w_in[:, 2 * D:2 * D + SX], w_in[:, 2 * D + SX:o_dt], w_in[:, o_dt + H:]]
        a_head = -jnp.exp(p['ssm_a_log'][l].astype(F32))
        hr = -(-H // 8) * 8
        W.update(
            w_main=jnp.concatenate(w_pieces, axis=1), w_pieces=w_pieces,
            w_dt=jnp.pad(w_in[:, o_dt:o_dt + H], ((0, 0), (0, LANES - H))),
            conv_w=full['conv_dw_w'], conv_b=_row(p['conv_dw_b'][l]), conv_g=_row(p['conv_ln_g'][l]),
            conv_beta=_row(p['conv_ln_b'][l]), ssm_w=full['ssm_conv_w'], ssm_b=_row(p['ssm_conv_b'][l]),
            dt_bias=_lanes(p['ssm_dt_bias'][l]), a_col=_lanes(a_head),
            a_row=jnp.pad(a_head, (0, hr - H)).reshape(hr, 1), d_col=_lanes(p['ssm_d'][l]),
            norm_w=_row(p['ssm_norm_w'][l]), ln1_g=_row(p['ln1_g'][l]), ln1_b=_row(p['ln1_b'][l]),
            ffn_b=_row(p['ffn_dw_b'][l]), ln2_g=_row(p['ln2_g'][l]), ln2_b=_row(p['ln2_b'][l]))
    if 'w_o' in full:
        W.update(w_conv_out=full['w_conv_out'], w_ssm_out=full['w_ssm_out'], w_o=full['w_o'], w_up=full['w_ffn_up'],
                 ffn_w=full['ffn_dw_w'], w_down=full['w_ffn_down'])
    return W


def _layer_fwd(h, hb, W, late, l, dims):
    D, SX, XBC, H, FF = dims
    o_z, o_xbc, o_gate = 2 * D, 2 * D + SX, 2 * D + SX + XBC
    hr = W['a_row'].shape[0]
    t = f"l{l}"
    u = _mm(hb, W['w_main'], mode='nn', out_dtype=BF16, name=t + "_in_proj")
    udt = _mm(hb, W['w_dt'], mode='nn', out_dtype=F32, name=t + "_dt_proj")
    v1, v3 = _conf_fwd(u, W['conv_w'], W['conv_b'], W['conv_g'], W['conv_beta'], D=D, name=t + "_conf_fwd")
    xbc = _conv_act_fwd(u, o_xbc, W['ssm_w'], W['ssm_b'], ffn=False, name=t + "_ssm_conv")
    dt = _dt_fwd(udt, W['dt_bias'], H=H, name=t + "_dt")
    dtT = jnp.pad(dt[:, :H].T, ((0, hr - H), (0, 0)))
    y, hst = _ssd_fwd(xbc, dt, dtT, W['a_col'], W['a_row'], W['d_col'], H=H, name=t + "_ssd_fwd")
    yn = _rms_fwd(y, u, o_z, W['norm_w'], name=t + "_rms_fwd")
    W = {**W, **late(yn)}
    ya = _mm(v3, W['w_conv_out'], mode='nn', out_dtype=F32, name=t + "_conv_out")
    yb = _mm(yn, W['w_ssm_out'], mode='nn', out_dtype=F32, name=t + "_ssm_out")
    m = _gate_fwd(u, o_gate, ya, yb, name=t + "_gate_fwd")
    h1, s1, h1b = _mm_res_ln(m, W['w_o'], h, W['ln1_g'], W['ln1_b'], name=t + "_mix_ln1")
    uf = _mm(h1b, W['w_up'], mode='nn', out_dtype=BF16, name=t + "_ffn_up")
    act = _conv_act_fwd(uf, 0, W['ffn_w'], W['ffn_b'], ffn=True, name=t + "_ffn_conv")
    h2, s2, h2b = _mm_res_ln(act, W['w_down'], h1, W['ln2_g'], W['ln2_b'], name=t + "_ffn_down_ln2")
    saved = dict(hb=hb, u=u, udt=udt, v1=v1, v3=v3, ya=ya, xbc=xbc, dt=dt, dtT=dtT, y=y, hst=hst, yn=yn, yb=yb, m=m,
                 h1b=h1b, s1=s1, uf=uf, act=act, s2=s2)
    return h2, h2b, saved, W


def _layer_bwd(dh2, W, S, l, dims, on_ffn_grads=None, on_all_grads=None):
    D, SX, XBC, H, FF = dims
    o_z, o_xbc, o_gate = 2 * D, 2 * D + SX, 2 * D + SX + XBC
    t = f"l{l}"
    g = {}
    ds2, g['ln2_g'], g['ln2_b'] = _ln_bwd(S['s2'], dh2, W['ln2_g'], W['ln2_b'], silu=False, name=t + "_ln2_bwd")
    g['w_down'] = _mm(S['act'], ds2, mode='tn', out_dtype=WIRE, name=t + "_dw_down")
    dact = _mm(ds2, W['w_down'], mode='nt', out_dtype=F32, name=t + "_dact")
    duf, g['ffn_w'], g['ffn_b'] = _conv_act_bwd(S['uf'], 0, W['ffn_w'], W['ffn_b'], dact, ffn=True, name=t + "_ffn_conv_bwd")
    g['w_up'] = _mm(S['h1b'], duf, mode='tn', out_dtype=WIRE, name=t + "_dw_up")
    dh1 = _mm(duf, W['w_up'], mode='nt', out_dtype=F32, res=ds2, res_scale=DN_ALPHA, name=t + "_dh1")
    ln1_g = W['ln1_g'] if on_ffn_grads is None else W['ln1_g'] + on_ffn_grads(g)[0:1, 0:1]
    ds1, g['ln1_g'], g['ln1_b'] = _ln_bwd(S['s1'], dh1, ln1_g, W['ln1_b'], silu=False, name=t + "_ln1_bwd")
    g['w_o'] = _mm(S['m'], ds1, mode='tn', out_dtype=WIRE, name=t + "_dw_o")
    dm = _mm(ds1, W['w_o'], mode='nt', out_dtype=F32, name=t + "_dm")
    dya, dyb, dgate = _gate_bwd(S['u'], o_gate, S['ya'], S['yb'], dm, name=t + "_gate_bwd")
    g['w_conv_out'] = _mm(S['v3'], dya, mode='tn', out_dtype=WIRE, name=t + "_dw_conv_out")
    dv3 = _mm(dya, W['w_conv_out'], mode='nt', out_dtype=F32, name=t + "_dv3")
    dv1, g['conv_g'], g['conv_beta'] = _ln_bwd(S['v1'], dv3, W['conv_g'], W['conv_beta'], silu=True, name=t + "_conv_ln_bwd")
    dglu, g['conv_w'], g['conv_b'] = _glu_conv_bwd(dv1, S['u'], W['conv_w'], name=t + "_conf_conv_bwd")
    g['w_ssm_out'] = _mm(S['yn'], dyb, mode='tn', out_dtype=WIRE, name=t + "_dw_ssm_out")
    dyn = _mm(dyb, W['w_ssm_out'], mode='nt', out_dtype=F32, name=t + "_dyn")
    dy, dz, g['norm_w'] = _rms_bwd(S['y'], S['u'], o_z, W['norm_w'], dyn, name=t + "_rms_bwd")
    dxbc_c, da, ddtx, g['d'] = _ssd_bwd(S['xbc'], S['dt'], S['dtT'], W['a_col'], W['a_row'], W['d_col'], S['hst'], dy,
                                        H=H, name=t + "_ssd_bwd")
    ddt_raw, g['dt_bias'], g['a_log'] = _dt_bwd(da, ddtx, S['dt'], S['udt'], W['dt_bias'], W['a_col'], H=H, name=t + "_dt_bwd")
    dxbc, g['ssm_w'], g['ssm_b'] = _conv_act_bwd(S['u'], o_xbc, W['ssm_w'], W['ssm_b'], dxbc_c, ffn=False, name=t + "_ssm_conv_bwd")
    du = [dglu, dz, dxbc, dgate]
    g['w_pieces'] = [_mm(S['hb'], d, mode='tn', out_dtype=WIRE, name=f"{t}_dw_in{i}") for i, d in enumerate(du)]
    g['w_dt'] = _mm(S['hb'], ddt_raw, mode='tn', out_dtype=WIRE, name=t + "_dw_dt")
    w_dt = W['w_dt'] if on_all_grads is None else W['w_dt'] + on_all_grads(g)[0:1, 0:1].astype(W['w_dt'].dtype)
    dh = _mm_nt_cat(du + [ddt_raw], W['w_pieces'] + [w_dt], ds1, DN_ALPHA, name=t + "_dh_in")
    return dh, g


def _layer_grads_to_params(g, dims):
    D, SX, XBC, H, FF = dims
    glu, dz, dxbc, dgate = g['w_pieces']
    w_in = jnp.concatenate([glu, dz, dxbc, g['w_dt'][:, :H], dgate], axis=1)
    return dict(
        w_in=w_in, conv_dw_w=g['conv_w'], conv_dw_b=g['conv_b'][0], conv_ln_g=g['conv_g'][0], conv_ln_b=g['conv_beta'][0],
        w_conv_out=g['w_conv_out'], ssm_conv_w=g['ssm_w'], ssm_conv_b=g['ssm_b'][0],
        ssm_dt_bias=g['dt_bias'][0, :H], ssm_a_log=g['a_log'][0, :H], ssm_d=g['d'][0, :H], ssm_norm_w=g['norm_w'][0],
        w_ssm_out=g['w_ssm_out'], w_o=g['w_o'], ln1_g=g['ln1_g'][0], ln1_b=g['ln1_b'][0],
        w_ffn_up=g['w_up'], ffn_dw_w=g['ffn_w'], ffn_dw_b=g['ffn_b'][0], w_ffn_down=g['w_down'], ln2_g=g['ln2_g'][0], ln2_b=g['ln2_b'][0])


def kernel(x, ln_in_g, ln_in_b, w_in, conv_dw_w, conv_dw_b, conv_ln_g, conv_ln_b, w_conv_out, ssm_conv_w, ssm_conv_b, ssm_dt_bias, ssm_a_log, ssm_d, ssm_norm_w, w_ssm_out, w_o, ln1_g, ln1_b, w_ffn_up, ffn_dw_w, ffn_dw_b, w_ffn_down, ln2_g, ln2_b, loss_target, m_ln_in_g, m_ln_in_b, m_w_in, m_conv_dw_w, m_conv_dw_b, m_conv_ln_g, m_conv_ln_b, m_w_conv_out, m_ssm_conv_w, m_ssm_conv_b, m_ssm_dt_bias, m_ssm_a_log, m_ssm_d, m_ssm_norm_w, m_w_ssm_out, m_w_o, m_ln1_g, m_ln1_b, m_w_ffn_up, m_ffn_dw_w, m_ffn_dw_b, m_w_ffn_down, m_ln2_g, m_ln2_b, v_ln_in_g, v_ln_in_b, v_w_in, v_conv_dw_w, v_conv_dw_b, v_conv_ln_g, v_conv_ln_b, v_w_conv_out, v_ssm_conv_w, v_ssm_conv_b, v_ssm_dt_bias, v_ssm_a_log, v_ssm_d, v_ssm_norm_w, v_w_ssm_out, v_w_o, v_ln1_g, v_ln1_b, v_w_ffn_up, v_ffn_dw_w, v_ffn_dw_b, v_w_ffn_down, v_ln2_g, v_ln2_b):
    weights = (ln_in_g, ln_in_b, w_in, conv_dw_w, conv_dw_b, conv_ln_g, conv_ln_b, w_conv_out, ssm_conv_w, ssm_conv_b,
               ssm_dt_bias, ssm_a_log, ssm_d, ssm_norm_w, w_ssm_out, w_o, ln1_g, ln1_b, w_ffn_up, ffn_dw_w, ffn_dw_b,
               w_ffn_down, ln2_g, ln2_b)
    moments_m = (m_ln_in_g, m_ln_in_b, m_w_in, m_conv_dw_w, m_conv_dw_b, m_conv_ln_g, m_conv_ln_b, m_w_conv_out,
                 m_ssm_conv_w, m_ssm_conv_b, m_ssm_dt_bias, m_ssm_a_log, m_ssm_d, m_ssm_norm_w, m_w_ssm_out, m_w_o,
                 m_ln1_g, m_ln1_b, m_w_ffn_up, m_ffn_dw_w, m_ffn_dw_b, m_w_ffn_down, m_ln2_g, m_ln2_b)
    moments_v = (v_ln_in_g, v_ln_in_b, v_w_in, v_conv_dw_w, v_conv_dw_b, v_conv_ln_g, v_conv_ln_b, v_w_conv_out,
                 v_ssm_conv_w, v_ssm_conv_b, v_ssm_dt_bias, v_ssm_a_log, v_ssm_d, v_ssm_norm_w, v_w_ssm_out, v_w_o,
                 v_ln1_g, v_ln1_b, v_w_ffn_up, v_ffn_dw_w, v_ffn_dw_b, v_w_ffn_down, v_ln2_g, v_ln2_b)
    p = dict(zip(PARAMS, weights))
    pm = dict(zip(PARAMS, moments_m))
    pv = dict(zip(PARAMS, moments_v))

    T, D = x.shape[1], x.shape[2]
    SX = w_ssm_out.shape[1] * N_DEV
    XBC = ssm_conv_b.shape[-1]
    H = ssm_d.shape[-1]
    FF = ffn_dw_b.shape[-1] // 2
    dims = (D, SX, XBC, H, FF)
    depth = w_in.shape[0]

    ALL = tuple(SHARDED)
    got = _exchange(_wire_shards(p, 0, EARLY), scatter=False, name="gather_l0_first")
    rest_shards, got = lax.optimization_barrier((_wire_shards(p, 0, LATE), got))
    rest_handle, tok = _exchange_start(rest_shards, scatter=False, name="gather_l0_rest_start")
    first = _layer_weights(p, _full_weights(got, p, EARLY), 0, dims)
    in_flight = {}

    def start_next_gather(l, behind):
        shards, _ = lax.optimization_barrier((_wire_shards(p, l, ALL), behind))
        in_flight[l], t = _exchange_start(shards, scatter=False, name=f"gather_l{l}_start")
        in_flight[l] = (in_flight[l], shards)
        return t

    def late_weights(l):
        def late(x):
            if l > 0:
                return {}
            got = _with_own_slot(_exchange_wait(rest_handle, x, scatter=False, name="gather_l0_rest_wait"), rest_shards, scatter=False)
            W = _layer_weights(p, _full_weights(got, p, LATE), 0, dims)
            if depth > 1:
                t = start_next_gather(1, got)
                W['w_conv_out'] = W['w_conv_out'] + t[0:1, 0:1].astype(W['w_conv_out'].dtype)
            return W
        return late

    xs = x.reshape(T, D)
    h, hb = _ln_fwd(xs, _row(ln_in_g) + tok[0:1, 0:1], _row(ln_in_b), name="ln_in_fwd")
    saved, layers = [], []
    for l in range(depth):
        if l > 0:
            handle, shards = in_flight.pop(l)
            got = _with_own_slot(_exchange_wait(handle, h, scatter=False, name=f"gather_l{l}_wait"), shards, scatter=False)
            first = _layer_weights(p, _full_weights(got, p, ALL), l, dims)
            if l + 1 < depth:
                t = start_next_gather(l + 1, got)
                first['w_dt'] = first['w_dt'] + t[0:1, 0:1].astype(first['w_dt'].dtype)
        h, hb, s, W = _layer_fwd(h, hb, first, late_weights(l), l, dims)
        saved.append(s)
        layers.append(W)
    loss_part, dh = _loss_fwd_bwd(h, loss_target.reshape(T, D), name="loss")

    layer_grads = [None] * depth
    flying, arrived = [], {}

    def wait_flying(after):
        while flying:
            l_, names, handle, parts = flying.pop(0)
            lands = _with_own_slot(_exchange_wait(handle, after, scatter=True, name=f"scatter_l{l_}_{names[0]}_wait"), parts, scatter=True)
            arrived.update({(l_, k): a for k, a in zip(names, lands)})
            after = lands[0]
        return after

    def start_scatter(l, names, grads_now, after):
        parts, _ = lax.optimization_barrier((_grad_parts(grads_now, p, names), wait_flying(after)))
        handle, t = _exchange_start(parts, scatter=True, name=f"scatter_l{l}_{names[0]}_start")
        flying.append((l, names, handle, parts))
        return t

    def on_ffn_grads(g):
        return start_scatter(0, FFN, dict(w_ffn_up=g['w_up'], ffn_dw_w=g['ffn_w'], w_ffn_down=g['w_down']), g['w_up'])

    def on_all_grads(g):
        layer_grads[0] = _layer_grads_to_params(g, dims)
        return start_scatter(0, NOT_FFN, layer_grads[0], g['w_dt'])

    tok = jnp.zeros((8, LANES), F32)
    for l in reversed(range(depth)):
        W = dict(layers[l], ln2_g=layers[l]['ln2_g'] + tok[0:1, 0:1])
        if l > 0:
            dh, g = _layer_bwd(dh, W, saved[l], l, dims)
            layer_grads[l] = _layer_grads_to_params(g, dims)
            tok = start_scatter(l, ALL, layer_grads[l], dh)
        else:
            dh, g = _layer_bwd(dh, W, saved[l], l, dims, on_ffn_grads, on_all_grads)
    grad_x, dg_in, db_in = _ln_bwd(xs, dh, _row(ln_in_g), _row(ln_in_b), silu=False, name="ln_in_bwd")
    wait_flying(grad_x)

    small = {k: jnp.stack([layer_grads[l][k] for l in range(depth)]) for k in layer_grads[0] if k not in SHARDED}
    small['ln_in_g'], small['ln_in_b'], small['loss'] = dg_in[0], db_in[0], loss_part
    grads = _allreduce_small(small)
    loss = grads.pop('loss').reshape(())

    delta, new_m, new_v = {}, {}, {}
    for k in SHARDED:
        grads[k], delta[k], new_m[k], new_v[k] = _sum_adamw([arrived[(l, k)] for l in range(depth)], p[k], pm[k], pv[k],
                                                            name="sum_adamw_" + k)
    rest = [k for k in PARAMS if k not in SHARDED]
    flat = lambda d: _to_rows(jnp.concatenate([d[k].reshape(-1) for k in rest]), 8)
    d_, m_, v_ = _adamw(flat(p), flat(grads), flat(pm), flat(pv), name="adamw_small")
    off = 0
    for k in rest:
        n, shp = p[k].size, p[k].shape
        delta[k] = d_.reshape(-1)[off:off + n].reshape(shp)
        new_m[k] = m_.reshape(-1)[off:off + n].reshape(shp)
        new_v[k] = v_.reshape(-1)[off:off + n].reshape(shp)
        off += n

    return (loss, grad_x.reshape(x.shape), *[grads[k] for k in PARAMS], *[delta[k] for k in PARAMS],
            *[new_m[k] for k in PARAMS], *[new_v[k] for k in PARAMS])
```

```python
import math

import jax
import jax.numpy as jnp
from jax import lax
from jax.experimental import pallas as pl
from jax.experimental.pallas import tpu as pltpu

F32 = jnp.float32
BF16 = jnp.bfloat16
WIRE = jnp.bfloat16
HIGHEST = lax.Precision.HIGHEST

DEPTH = 2
SSM_STATE = 128
SSM_CHUNK = 128
SSM_GROUPS = 4
SSM_HEAD_DIM = 64
DN_ALPHA = (2 * DEPTH) ** 0.25
LN_EPS = 1e-5
RMS_EPS = 1e-5
ADAM_LR = 0.001
ADAM_B1 = 0.9
ADAM_B2 = 0.999
ADAM_EPS = 1e-08
ADAM_WD = 0.01
ADAM_STEP = 10

N_DEV = 8
LANES = 128
VMEM_LIMIT = 48 * 1024 * 1024
WHOLE_K_VMEM = 40 * 1024 * 1024
GLU_TILE = 1024
FFN_TILE = 2816
CONV_ROWS = 256

PARAMS = ['ln_in_g', 'ln_in_b', 'w_in', 'conv_dw_w', 'conv_dw_b', 'conv_ln_g', 'conv_ln_b', 'w_conv_out',
          'ssm_conv_w', 'ssm_conv_b', 'ssm_dt_bias', 'ssm_a_log', 'ssm_d', 'ssm_norm_w', 'w_ssm_out', 'w_o',
          'ln1_g', 'ln1_b', 'w_ffn_up', 'ffn_dw_w', 'ffn_dw_b', 'w_ffn_down', 'ln2_g', 'ln2_b']
SHARDED = {'w_in': (2, True), 'conv_dw_w': (2, False), 'w_conv_out': (1, True), 'ssm_conv_w': (2, False),
           'w_ssm_out': (1, True), 'w_o': (1, True), 'w_ffn_up': (2, True), 'ffn_dw_w': (2, False),
           'w_ffn_down': (1, True)}


def _sds(shape, dtype):
    return jax.ShapeDtypeStruct(tuple(shape), dtype)


def _tile(dim, pref):
    return pref if dim % pref == 0 else dim


def _pick(dim, pref, mult):
    best = None
    for t in range(mult, min(dim, pref) + 1, mult):
        if dim % t == 0:
            best = t
    return best or dim


def _cp(*sem):
    return pltpu.CompilerParams(dimension_semantics=sem, vmem_limit_bytes=VMEM_LIMIT)


def _f32(x):
    return x.astype(F32)


def _sigmoid(x):
    return 1.0 / (1.0 + jnp.exp(-x))


def _silu_and_grad(x):
    s = _sigmoid(x)
    return x * s, s * (1.0 + x * (1.0 - s))


def _ln_stats(s):
    mu = jnp.mean(s, axis=-1, keepdims=True)
    xc = s - mu
    var = jnp.mean(xc * xc, axis=-1, keepdims=True)
    rstd = lax.rsqrt(var + LN_EPS)
    return xc * rstd, rstd


def _dot(a, b, dims):
    return lax.dot_general(a, b, (dims, ((), ())), preferred_element_type=F32)


_NN = ((1,), (0,))
_NT = ((1,), (1,))
_TN = ((0,), (0,))


def _mm(a, b, *, mode, out_dtype, name, res=None, res_scale=1.0, tm=1024, tn=1408, tk=1408):
    if mode == 'nn':
        (M, K), (_, N) = a.shape, b.shape
    elif mode == 'nt':
        (M, K), (N, _) = a.shape, b.shape
    else:
        (K, M), (_, N) = a.shape, b.shape
        tm, tk = 1408, 1024
    tm, tn, tk = _pick(M, tm, 8), _pick(N, tn, LANES), _pick(K, tk, LANES)
    if mode != 'tn':
        for rows in (tm, _pick(M, tm // 2, 8), _pick(M, tm // 4, 8)):
            blocks = rows * K * a.dtype.itemsize + tn * K * b.dtype.itemsize + rows * tn * (4 + (4 if res is not None else 0))
            if 2 * blocks <= WHOLE_K_VMEM:
                tm, tk = rows, K
                break
    nk = K // tk
    dims = {'nn': _NN, 'nt': _NT, 'tn': _TN}[mode]

    def body(*refs):
        if res is None:
            a_ref, b_ref, o_ref = refs[:3]
            r_ref = None
        else:
            a_ref, b_ref, r_ref, o_ref = refs[:4]
        p = _dot(a_ref[...].astype(BF16), b_ref[...].astype(BF16), dims)

        def finish(acc):
            if r_ref is not None:
                acc = acc + res_scale * r_ref[...]
            o_ref[...] = acc.astype(out_dtype)

        if nk == 1:
            finish(p)
        else:
            acc_ref = refs[-1]
            k = pl.program_id(2)

            @pl.when(k == 0)
            def _():
                acc_ref[...] = p

            @pl.when(k > 0)
            def _():
                acc_ref[...] += p

            @pl.when(k == nk - 1)
            def _():
                finish(acc_ref[...])

    if mode == 'nn':
        a_spec = pl.BlockSpec((tm, tk), lambda i, j, k: (i, k))
        b_spec = pl.BlockSpec((tk, tn), lambda i, j, k: (k, j))
    elif mode == 'nt':
        a_spec = pl.BlockSpec((tm, tk), lambda i, j, k: (i, k))
        b_spec = pl.BlockSpec((tn, tk), lambda i, j, k: (j, k))
    else:
        a_spec = pl.BlockSpec((tk, tm), lambda i, j, k: (k, i))
        b_spec = pl.BlockSpec((tk, tn), lambda i, j, k: (k, j))
    o_spec = pl.BlockSpec((tm, tn), lambda i, j, k: (i, j))
    in_specs = [a_spec, b_spec] + ([o_spec] if res is not None else [])
    args = (a, b) + ((res,) if res is not None else ())
    return pl.pallas_call(
        body, name=name, out_shape=_sds((M, N), out_dtype), grid=(M // tm, N // tn, nk),
        in_specs=in_specs, out_specs=o_spec,
        scratch_shapes=[pltpu.VMEM((tm, tn), F32)] if nk > 1 else [],
        compiler_params=_cp("parallel", "parallel", "arbitrary"))(*args)


def _mm_nt_cat(a_list, b_list, res, res_scale, *, name, tm=512, tn=512):
    M, N = a_list[0].shape[0], b_list[0].shape[0]
    tm, tn = _pick(M, tm, 8), _pick(N, tn, LANES)
    n = len(a_list)

    def body(*refs):
        a_refs, b_refs, r_ref, o_ref = refs[:n], refs[n:2 * n], refs[2 * n], refs[2 * n + 1]
        acc = res_scale * r_ref[...]
        for i in range(n):
            acc = acc + _dot(a_refs[i][...].astype(BF16), b_refs[i][...].astype(BF16), _NT)
        o_ref[...] = acc

    a_specs = [pl.BlockSpec((tm, a.shape[1]), lambda j, m: (m, 0)) for a in a_list]
    b_specs = [pl.BlockSpec((tn, b.shape[1]), lambda j, m: (j, 0)) for b in b_list]
    blk = pl.BlockSpec((tm, tn), lambda j, m: (m, j))
    return pl.pallas_call(
        body, name=name, out_shape=_sds((M, N), F32), grid=(N // tn, M // tm), in_specs=a_specs + b_specs + [blk],
        out_specs=blk, compiler_params=_cp("parallel", "parallel"))(*a_list, *b_list, res)


def _mm_res_ln(a, w, res, g, b, *, name, tm=512, tk=1408):
    (M, K), (_, N) = a.shape, w.shape
    tm, tk = _pick(M, tm, 8), _pick(K, tk, LANES)
    if 2 * (tm * K * a.dtype.itemsize + K * N * w.dtype.itemsize + tm * N * 14) <= WHOLE_K_VMEM:
        tk = K
    nk = K // tk

    def body(a_ref, w_ref, r_ref, g_ref, b_ref, h_ref, s_ref, hb_ref, acc_ref):
        k = pl.program_id(1)
        p = _dot(a_ref[...].astype(BF16), w_ref[...].astype(BF16), _NN)

        def finish(acc):
            s = DN_ALPHA * r_ref[...] + acc
            xhat, _ = _ln_stats(s)
            s_ref[...] = s
            h = xhat * g_ref[...] + b_ref[...]
            h_ref[...] = h
            hb_ref[...] = h.astype(BF16)

        if nk == 1:
            finish(p)
            return

        @pl.when(k == 0)
        def _():
            acc_ref[...] = p

        @pl.when(k > 0)
        def _():
            acc_ref[...] += p

        @pl.when(k == nk - 1)
        def _():
            finish(acc_ref[...])

    row = pl.BlockSpec((tm, N), lambda i, k: (i, 0))
    vec = pl.BlockSpec((1, N), lambda i, k: (0, 0))
    return pl.pallas_call(
        body, name=name, out_shape=(_sds((M, N), F32), _sds((M, N), F32), _sds((M, N), BF16)), grid=(M // tm, nk),
        in_specs=[pl.BlockSpec((tm, tk), lambda i, k: (i, k)), pl.BlockSpec((tk, N), lambda i, k: (k, 0)), row, vec, vec],
        out_specs=(row, row, row), scratch_shapes=[pltpu.VMEM((tm, N), F32)],
        compiler_params=_cp("parallel", "arbitrary"))(a, w, res, g, b)


def _ln_fwd(x, g, b, *, name):
    T, D = x.shape
    tt = _tile(T, 512)

    def body(x_ref, g_ref, b_ref, o_ref, ob_ref):
        xhat, _ = _ln_stats(x_ref[...])
        h = xhat * g_ref[...] + b_ref[...]
        o_ref[...] = h
        ob_ref[...] = h.astype(BF16)

    row = pl.BlockSpec((tt, D), lambda i: (i, 0))
    vec = pl.BlockSpec((1, D), lambda i: (0, 0))
    return pl.pallas_call(body, name=name, out_shape=(_sds((T, D), F32), _sds((T, D), BF16)), grid=(T // tt,),
                          in_specs=[row, vec, vec], out_specs=(row, row), compiler_params=_cp("parallel"))(x, g, b)


def _ln_bwd(s, dy, g, b, *, silu, name):
    T, D = s.shape
    tt = _tile(T, 512)

    def body(s_ref, dy_ref, g_ref, b_ref, ds_ref, dg_ref, db_ref):
        i = pl.program_id(0)
        xhat, rstd = _ln_stats(s_ref[...])
        gg = g_ref[...]
        dyl = _f32(dy_ref[...])
        if silu:
            _, dsilu = _silu_and_grad(xhat * gg + b_ref[...])
            dyl = dyl * dsilu
        dxh = dyl * gg
        m1 = jnp.mean(dxh, axis=-1, keepdims=True)
        m2 = jnp.mean(dxh * xhat, axis=-1, keepdims=True)
        ds_ref[...] = rstd * (dxh - m1 - xhat * m2)

        @pl.when(i == 0)
        def _():
            dg_ref[...] = jnp.zeros_like(dg_ref)
            db_ref[...] = jnp.zeros_like(db_ref)

        dg_ref[...] += jnp.sum(dyl * xhat, axis=0, keepdims=True)
        db_ref[...] += jnp.sum(dyl, axis=0, keepdims=True)

    row = pl.BlockSpec((tt, D), lambda i: (i, 0))
    vec = pl.BlockSpec((1, D), lambda i: (0, 0))
    return pl.pallas_call(body, name=name, out_shape=(_sds((T, D), F32), _sds((1, D), F32), _sds((1, D), F32)),
                          grid=(T // tt,), in_specs=[row, row, vec, vec], out_specs=(row, vec, vec),
                          compiler_params=_cp("arbitrary"))(s, dy, g, b)


def _loss_fwd_bwd(h, tgt, *, name):
    T, D = h.shape
    tt = _tile(T, 512)

    def body(h_ref, t_ref, dh_ref, l_ref):
        i = pl.program_id(0)
        e = h_ref[...] - t_ref[...]
        dh_ref[...] = e * (1.0 / D)

        @pl.when(i == 0)
        def _():
            l_ref[...] = jnp.zeros_like(l_ref)

        part = jnp.sum(jnp.sum(e * e, axis=1, keepdims=True), axis=0, keepdims=True) * (0.5 / D)
        l_ref[...] += jnp.broadcast_to(part, l_ref.shape)

    row = pl.BlockSpec((tt, D), lambda i: (i, 0))
    one = pl.BlockSpec((8, LANES), lambda i: (0, 0))
    dh, l = pl.pallas_call(body, name=name, out_shape=(_sds((T, D), F32), _sds((8, LANES), F32)), grid=(T // tt,),
                           in_specs=[row, row], out_specs=(row, one), compiler_params=_cp("arbitrary"))(h, tgt)
    return l[0:1, 0:1], dh


def _halo_rows(k):
    return 32 if k > 17 else 16


def _shifted(ext, K, first):
    rolled = {0: ext}
    out = []
    for k in range(K):
        r = (first + k) % 8
        if r not in rolled:
            rolled[r] = pltpu.roll(ext, ext.shape[0] - r, 0)
        out.append((rolled[r], first + k - r))
    return out


def _taps(ext, w_ref, sl, K, first, n, reverse=False):
    acc = None
    for k, (z, base) in enumerate(_shifted(ext, K, first)):
        kw = K - 1 - k if reverse else k
        term = w_ref[kw:kw + 1, sl] * z[base:base + n]
        acc = term if acc is None else acc + term
    return acc


def _tap_sums(d, ext, dw_ref, sl, K, first):
    n = d.shape[0]
    for k, (z, base) in enumerate(_shifted(ext, K, first)):
        dw_ref[k:k + 1, sl] += jnp.sum(d * z[base:base + n], axis=0, keepdims=True)


def _strip_width(width):
    return LANES if width % LANES == 0 else width


def _for_strips(width, fn):
    sw = _strip_width(width)

    def step(s, carry):
        fn(pl.ds(pl.multiple_of(s * sw, sw), sw), s)
        return carry

    lax.fori_loop(0, width // sw, step, 0)


def _prev_rows(tt, hb):
    return lambda t: jnp.maximum(t * (tt // hb) - 1, 0)


def _next_rows(tt, hb, T):
    return lambda t: jnp.minimum((t + 1) * (tt // hb), T // hb - 1)


def _conf_fwd(u, w, b, g, beta, *, D, name):
    T = u.shape[0]
    K = w.shape[0]
    tt, hb, tc = _tile(T, CONV_ROWS), _halo_rows(K), _tile(D, GLU_TILE)
    sw = _strip_width(tc)
    per_tile = tc // sw
    prev = _prev_rows(tt, hb)

    def body(u_ref, uh_ref, w_ref, b_ref, g_ref, beta_ref, v1_ref, v3_ref):
        i = pl.program_id(0)

        def strip(sl, s):
            a0 = (s // per_tile) * (2 * tc) + (s % per_tile) * sw
            a_sl, g_sl = pl.ds(pl.multiple_of(a0, sw), sw), pl.ds(pl.multiple_of(a0 + tc, sw), sw)
            halo = jnp.where(i > 0, _f32(uh_ref[:, a_sl]) * _sigmoid(_f32(uh_ref[:, g_sl])), 0.0)
            ext = jnp.concatenate([halo, _f32(u_ref[:, a_sl]) * _sigmoid(_f32(u_ref[:, g_sl]))], axis=0)
            v1_ref[:, sl] = _taps(ext, w_ref, sl, K, hb - (K - 1), tt) + b_ref[:, sl]

        _for_strips(D, strip)
        xhat, _ = _ln_stats(v1_ref[...])
        v2 = xhat * g_ref[...] + beta_ref[...]
        v3_ref[...] = (v2 * _sigmoid(v2)).astype(BF16)

    row = pl.BlockSpec((tt, D), lambda t: (t, 0))
    vec = pl.BlockSpec((1, D), lambda t: (0, 0))
    return pl.pallas_call(
        body, name=name, out_shape=(_sds((T, D), F32), _sds((T, D), BF16)), grid=(T // tt,),
        in_specs=[pl.BlockSpec((tt, 2 * D), lambda t: (t, 0)), pl.BlockSpec((hb, 2 * D), lambda t: (prev(t), 0)),
                  pl.BlockSpec((K, D), lambda t: (0, 0)), vec, vec, vec],
        out_specs=(row, row), compiler_params=_cp("parallel"))(u, u, w, b, g, beta)


def _conv_act_fwd(x, off, w, b, *, ffn, name):
    T = x.shape[0]
    K, Cw = w.shape
    tt, hb = _tile(T, CONV_ROWS), _halo_rows(K)
    tc = _tile(Cw // 2, FFN_TILE) if ffn else _pick(math.gcd(Cw, off), 1536, LANES)
    xw = 2 * tc if ffn else tc
    assert off % xw == 0
    ob = off // xw
    prev = _prev_rows(tt, hb)

    def body(x_ref, xh_ref, w_ref, b_ref, o_ref):
        first = pl.program_id(0) == 0

        def pre_of(sl):
            ext = jnp.concatenate([jnp.where(first, 0.0, _f32(xh_ref[:, sl])), _f32(x_ref[:, sl])], axis=0)
            return _taps(ext, w_ref, sl, K, hb - (K - 1), tt) + b_ref[:, sl]

        def strip(sl, s):
            if ffn:
                gate = pre_of(sl)
                val = pre_of(pl.ds(pl.multiple_of(tc + s * sw, sw), sw))
                o_ref[:, sl] = (gate * _sigmoid(gate) * val).astype(BF16)
            else:
                pre = pre_of(sl)
                o_ref[:, sl] = pre * _sigmoid(pre)

        _for_strips(tc, strip)

    sw = _strip_width(tc)
    return pl.pallas_call(
        body, name=name, out_shape=_sds((T, Cw // 2), BF16) if ffn else _sds((T, Cw), F32), grid=(T // tt, Cw // xw),
        in_specs=[pl.BlockSpec((tt, xw), lambda t, c: (t, c + ob)), pl.BlockSpec((hb, xw), lambda t, c: (prev(t), c + ob)),
                  pl.BlockSpec((K, xw), lambda t, c: (0, c)), pl.BlockSpec((1, xw), lambda t, c: (0, c))],
        out_specs=pl.BlockSpec((tt, tc), lambda t, c: (t, c)),
        compiler_params=_cp("parallel", "parallel"))(x, x, w, b)


def _conv_act_bwd(x, off, w, b, dout, *, ffn, name):
    T = x.shape[0]
    K, Cw = w.shape
    tt, hb = _tile(T, CONV_ROWS), _halo_rows(K)
    tc = _tile(Cw // 2, FFN_TILE) if ffn else _pick(math.gcd(Cw, off), 1536, LANES)
    xw = 2 * tc if ffn else tc
    assert off % xw == 0
    ob = off // xw
    nt = T // tt
    prev, nxt = _prev_rows(tt, hb), _next_rows(tt, hb, T)

    sw = _strip_width(tc)

    def body(x_ref, xp_ref, xn_ref, d_ref, dn_ref, w_ref, b_ref, dx_ref, dw_ref, db_ref):
        t = pl.program_id(1)

        @pl.when(t == 0)
        def _():
            dw_ref[...] = jnp.zeros_like(dw_ref)
            db_ref[...] = jnp.zeros_like(db_ref)

        def ext_of(sl):
            return jnp.concatenate([jnp.where(t == 0, 0.0, _f32(xp_ref[:, sl])), _f32(x_ref[:, sl]), _f32(xn_ref[:, sl])], axis=0)

        def pre_of(ext, sl):
            return _taps(ext, w_ref, sl, K, hb - (K - 1), tt + hb) + b_ref[:, sl]

        def finish(ext, dpre, sl):
            dx_ref[:, sl] = _taps(dpre, w_ref, sl, K, 0, tt, reverse=True).astype(BF16)
            dp = dpre[0:tt]
            _tap_sums(dp, ext, dw_ref, sl, K, hb - (K - 1))
            db_ref[:, sl] += jnp.sum(dp, axis=0, keepdims=True)

        def strip(sl, s):
            d = jnp.concatenate([d_ref[:, sl], jnp.where(t == nt - 1, 0.0, dn_ref[:, sl])], axis=0)
            if ffn:
                vsl = pl.ds(pl.multiple_of(tc + s * sw, sw), sw)
                eg, ev = ext_of(sl), ext_of(vsl)
                sg, dsg = _silu_and_grad(pre_of(eg, sl))
                val = pre_of(ev, vsl)
                finish(eg, d * val * dsg, sl)
                finish(ev, d * sg, vsl)
            else:
                ext = ext_of(sl)
                finish(ext, d * _silu_and_grad(pre_of(ext, sl))[1], sl)

        _for_strips(tc, strip)

    return pl.pallas_call(
        body, name=name, out_shape=(_sds((T, Cw), BF16), _sds((K, Cw), F32), _sds((1, Cw), F32)),
        grid=(Cw // xw, nt),
        in_specs=[pl.BlockSpec((tt, xw), lambda c, t: (t, c + ob)), pl.BlockSpec((hb, xw), lambda c, t: (prev(t), c + ob)),
                  pl.BlockSpec((hb, xw), lambda c, t: (nxt(t), c + ob)),
                  pl.BlockSpec((tt, tc), lambda c, t: (t, c)), pl.BlockSpec((hb, tc), lambda c, t: (nxt(t), c)),
                  pl.BlockSpec((K, xw), lambda c, t: (0, c)), pl.BlockSpec((1, xw), lambda c, t: (0, c))],
        out_specs=(pl.BlockSpec((tt, xw), lambda c, t: (t, c)), pl.BlockSpec((K, xw), lambda c, t: (0, c)),
                   pl.BlockSpec((1, xw), lambda c, t: (0, c))),
        compiler_params=_cp("parallel", "arbitrary"))(x, x, x, dout, dout, w, b)


def _glu_conv_bwd(dpre, u, w, *, name):
    T, C = dpre.shape
    K = w.shape[0]
    tt, hb, tc = _tile(T, CONV_ROWS), _halo_rows(K), _tile(C, GLU_TILE)
    nt = T // tt
    prev, nxt = _prev_rows(tt, hb), _next_rows(tt, hb, T)

    sw = _strip_width(tc)

    def body(d_ref, dn_ref, x_ref, xh_ref, w_ref, dx_ref, dw_ref, db_ref):
        t = pl.program_id(1)

        @pl.when(t == 0)
        def _():
            dw_ref[...] = jnp.zeros_like(dw_ref)
            db_ref[...] = jnp.zeros_like(db_ref)

        def strip(sl, s):
            gsl = pl.ds(pl.multiple_of(tc + s * sw, sw), sw)
            d = d_ref[:, sl]
            dext = jnp.concatenate([d, jnp.where(t == nt - 1, 0.0, dn_ref[:, sl])], axis=0)
            dxin = _taps(dext, w_ref, sl, K, 0, tt, reverse=True)
            a, sg = _f32(x_ref[:, sl]), _sigmoid(_f32(x_ref[:, gsl]))
            v0 = a * sg
            dx_ref[:, sl] = (dxin * sg).astype(BF16)
            dx_ref[:, gsl] = (dxin * v0 * (1.0 - sg)).astype(BF16)
            halo = jnp.where(t == 0, 0.0, _f32(xh_ref[:, sl]) * _sigmoid(_f32(xh_ref[:, gsl])))
            _tap_sums(d, jnp.concatenate([halo, v0], axis=0), dw_ref, sl, K, hb - (K - 1))
            db_ref[:, sl] += jnp.sum(d, axis=0, keepdims=True)

        _for_strips(tc, strip)

    return pl.pallas_call(
        body, name=name, out_shape=(_sds((T, 2 * C), BF16), _sds((K, C), F32), _sds((1, C), F32)), grid=(C // tc, nt),
        in_specs=[pl.BlockSpec((tt, tc), lambda c, t: (t, c)), pl.BlockSpec((hb, tc), lambda c, t: (nxt(t), c)),
                  pl.BlockSpec((tt, 2 * tc), lambda c, t: (t, c)), pl.BlockSpec((hb, 2 * tc), lambda c, t: (prev(t), c)),
                  pl.BlockSpec((K, tc), lambda c, t: (0, c))],
        out_specs=(pl.BlockSpec((tt, 2 * tc), lambda c, t: (t, c)), pl.BlockSpec((K, tc), lambda c, t: (0, c)),
                   pl.BlockSpec((1, tc), lambda c, t: (0, c))),
        compiler_params=_cp("parallel", "arbitrary"))(dpre, dpre, u, u, w)


def _softplus(x):
    t = jnp.exp(-jnp.abs(x))
    u = 1.0 + t
    log1p = jnp.where(u == 1.0, t, jnp.log(u) * t / jnp.where(u == 1.0, 1.0, u - 1.0))
    return jnp.maximum(x, 0.0) + log1p


def _dt_fwd(udt, bias, *, H, name):
    T = udt.shape[0]
    tt = _tile(T, 1024)

    def body(u_ref, b_ref, o_ref):
        lane = lax.broadcasted_iota(jnp.int32, (tt, LANES), 1)
        o_ref[...] = jnp.where(lane < H, _softplus(u_ref[...] + b_ref[...]), 0.0)

    row = pl.BlockSpec((tt, LANES), lambda i: (i, 0))
    return pl.pallas_call(body, name=name, out_shape=_sds((T, LANES), F32), grid=(T // tt,),
                          in_specs=[row, pl.BlockSpec((1, LANES), lambda i: (0, 0))], out_specs=row,
                          compiler_params=_cp("parallel"))(udt, bias)


def _dt_bwd(da, ddtx, dt, udt, bias, a_col, *, H, name):
    T = da.shape[0]
    tt = _tile(T, 1024)

    def body(da_ref, dx_ref, dt_ref, u_ref, b_ref, a_ref, draw_ref, dbias_ref, dalog_ref):
        i = pl.program_id(0)
        lane = lax.broadcasted_iota(jnp.int32, (tt, LANES), 1)
        dav = da_ref[...]
        ddt = dav * a_ref[...] + dx_ref[...]
        draw = jnp.where(lane < H, ddt * _sigmoid(u_ref[...] + b_ref[...]), 0.0)
        draw_ref[...] = draw.astype(BF16)

        @pl.when(i == 0)
        def _():
            dbias_ref[...] = jnp.zeros_like(dbias_ref)
            dalog_ref[...] = jnp.zeros_like(dalog_ref)

        dbias_ref[...] += jnp.sum(draw, axis=0, keepdims=True)
        dalog_ref[...] += jnp.sum(dav * dt_ref[...], axis=0, keepdims=True) * a_ref[...]

    row = pl.BlockSpec((tt, LANES), lambda i: (i, 0))
    vec = pl.BlockSpec((1, LANES), lambda i: (0, 0))
    return pl.pallas_call(body, name=name,
                          out_shape=(_sds((T, LANES), BF16), _sds((1, LANES), F32), _sds((1, LANES), F32)),
                          grid=(T // tt,), in_specs=[row, row, row, row, vec, vec], out_specs=(row, vec, vec),
                          compiler_params=_cp("arbitrary"))(da, ddtx, dt, udt, bias, a_col)


def _ssd_consts():
    L = SSM_CHUNK
    r = lax.broadcasted_iota(jnp.int32, (L, L), 0)
    c = lax.broadcasted_iota(jnp.int32, (L, L), 1)
    return r, c


def _ssd_chunk_decays(dtc_ref, dtr_ref, acol_ref, arow_ref, cs_ref, csr_ref, r, c):
    L = SSM_CHUNK
    dtc = dtc_ref[...]
    tril = (r >= c).astype(F32)
    triu = (r <= c).astype(F32)
    cs_ref[...] = jnp.dot(tril, dtc * acol_ref[...], precision=HIGHEST, preferred_element_type=F32)
    csr_ref[...] = jnp.dot(dtr_ref[...] * arow_ref[...], triu, precision=HIGHEST, preferred_element_type=F32)
    cs = cs_ref[...]
    cs_last = cs_ref[L - 1:L, :]
    return dtc, cs, jnp.exp(cs), jnp.exp(cs_last - cs), jnp.exp(cs_last), triu


def _head_spread(arrs, g, heads_per_group):
    P = SSM_HEAD_DIM
    gw = heads_per_group * P
    head = lax.broadcasted_iota(jnp.int32, (LANES, gw), 0)
    lane = lax.broadcasted_iota(jnp.int32, (LANES, gw), 1)
    spread = (head == g * heads_per_group + lane // P).astype(BF16)
    out = _dot(jnp.concatenate(arrs, axis=0).astype(BF16), spread, _NN)
    L = arrs[0].shape[0]
    return [out[i * L:(i + 1) * L] for i in range(len(arrs))]


def _ssd_fwd(xbc, dt, dtT, a_col, a_row, d_col, *, H, name):
    T, XBC = xbc.shape
    L, N, G, P = SSM_CHUNK, SSM_STATE, SSM_GROUPS, SSM_HEAD_DIM
    SX = H * P
    HR = dtT.shape[0]
    nc = T // L
    heads_per_group = H // G

    def body(x_ref, dtc_ref, dtr_ref, acol_ref, arow_ref, d_ref, y_ref, hst_ref, state, cs_ref, csr_ref):
        ci = pl.program_id(0)

        @pl.when(ci == 0)
        def _():
            state[...] = jnp.zeros_like(state)

        hst_ref[0] = state[...]
        r, c = _ssd_consts()
        tri = r >= c
        lane_lo = c < P
        row_lo = r < P
        lane1_lo = lax.broadcasted_iota(jnp.int32, (1, LANES), 1) < P
        dtc, cs, e, ds, cd, _ = _ssd_chunk_decays(dtc_ref, dtr_ref, acol_ref, arow_ref, cs_ref, csr_ref, r, c)
        dsk = d_ref[...]

        for g in range(G):
            Bg = x_ref[:, SX + g * N:SX + (g + 1) * N].astype(BF16)
            Cg = x_ref[:, SX + G * N + g * N:SX + G * N + (g + 1) * N].astype(BF16)
            Gm = _dot(Cg, Bg, _NT)
            dt_g, e_g, ds_g = _head_spread([dtc, e, ds], g, heads_per_group)
            st = []
            for j in range(g * heads_per_group // 2, (g + 1) * heads_per_group // 2):
                h0 = 2 * j
                sl = slice(2 * P * j, 2 * P * (j + 1))
                gl = slice(sl.start - g * heads_per_group * P, sl.stop - g * heads_per_group * P)
                x2 = x_ref[:, sl]
                X2 = x2 * dt_g[:, gl]
                H2 = state[sl, :]
                st.append(dict(h0=h0, sl=sl, x2=x2, X2=X2, H2=H2, e2=e_g[:, gl], R2=_dot(Cg, H2.astype(BF16), _NT),
                               S2=_dot((X2 * ds_g[:, gl]).astype(BF16), Bg, _TN)))
            for s in st:
                ms = []
                for h in (s['h0'], s['h0'] + 1):
                    seg = cs[:, h:h + 1] - csr_ref[h:h + 1, :]
                    ms.append((Gm * jnp.where(tri, jnp.exp(jnp.where(tri, seg, 0.0)), 0.0)).astype(BF16))
                xst = jnp.concatenate([jnp.where(lane_lo, s['X2'], 0.0), jnp.where(lane_lo, 0.0, s['X2'])], axis=0).astype(BF16)
                s['yd'] = _dot(jnp.concatenate(ms, axis=1), xst, _NN)
            for s in st:
                h0, sl = s['h0'], s['sl']
                dsk2 = jnp.where(lane1_lo, dsk[:, h0:h0 + 1], dsk[:, h0 + 1:h0 + 2])
                y_ref[:, sl] = (s['yd'] + s['e2'] * s['R2'] + s['x2'] * dsk2).astype(BF16)
                state[sl, :] = jnp.where(row_lo, cd[:, h0:h0 + 1], cd[:, h0 + 1:h0 + 2]) * s['H2'] + s['S2']

    vec = pl.BlockSpec((1, LANES), lambda i: (0, 0))
    return pl.pallas_call(
        body, name=name, out_shape=(_sds((T, SX), BF16), _sds((nc, SX, N), F32)), grid=(nc,),
        in_specs=[pl.BlockSpec((L, XBC), lambda i: (i, 0)), pl.BlockSpec((L, LANES), lambda i: (i, 0)),
                  pl.BlockSpec((HR, L), lambda i: (0, i)), vec, pl.BlockSpec((HR, 1), lambda i: (0, 0)), vec],
        out_specs=(pl.BlockSpec((L, SX), lambda i: (i, 0)), pl.BlockSpec((1, SX, N), lambda i: (i, 0, 0))),
        scratch_shapes=[pltpu.VMEM((SX, N), F32), pltpu.VMEM((L, LANES), F32), pltpu.VMEM((HR, L), F32)],
        compiler_params=_cp("arbitrary"))(xbc, dt, dtT, a_col, a_row, d_col)


def _ssd_bwd(xbc, dt, dtT, a_col, a_row, d_col, hst, dy, *, H, name):
    T, XBC = xbc.shape
    L, N, G, P = SSM_CHUNK, SSM_STATE, SSM_GROUPS, SSM_HEAD_DIM
    SX = H * P
    HR = dtT.shape[0]
    nc = T // L
    heads_per_group = H // G

    def body(x_ref, dtc_ref, dtr_ref, acol_ref, arow_ref, d_ref, hst_ref, dy_ref,
             dx_ref, da_ref, ddtx_ref, dd_ref, dstate, cs_ref, csr_ref):
        ci = pl.program_id(0)

        @pl.when(ci == 0)
        def _():
            dstate[...] = jnp.zeros_like(dstate)
            dd_ref[...] = jnp.zeros_like(dd_ref)

        r, c = _ssd_consts()
        tri = r >= c
        lane_lo = c < P
        row_lo = r < P
        lane1 = lax.broadcasted_iota(jnp.int32, (1, LANES), 1)
        rowc = lax.broadcasted_iota(jnp.int32, (L, 1), 0)
        dtc, cs, e, ds, cd, triu = _ssd_chunk_decays(dtc_ref, dtr_ref, acol_ref, arow_ref, cs_ref, csr_ref, r, c)
        triu_b = triu.astype(BF16)
        dsk = d_ref[...]
        last_row = rowc == L - 1

        triT = r <= c

        def halves(v, axis):
            return jnp.concatenate([jnp.where(lane_lo, v, 0.0), jnp.where(lane_lo, 0.0, v)], axis=axis)

        def head_sum(v, h):
            lo = jnp.sum(jnp.where(lane1 < P, v, 0.0), axis=1, keepdims=True) * (lane1 == h).astype(F32)
            hi = jnp.sum(jnp.where(lane1 < P, 0.0, v), axis=1, keepdims=True) * (lane1 == h + 1).astype(F32)
            return lo + hi

        GW = heads_per_group * P
        gl = lax.broadcasted_iota(jnp.int32, (GW, LANES), 0)
        gc = lax.broadcasted_iota(jnp.int32, (GW, LANES), 1)
        wl = lax.broadcasted_iota(jnp.int32, (heads_per_group * L, LANES), 0)
        wc = lax.broadcasted_iota(jnp.int32, (heads_per_group * L, LANES), 1)
        wide_r = lax.broadcasted_iota(jnp.int32, (L, heads_per_group * L), 0)
        wide_c = lax.broadcasted_iota(jnp.int32, (L, heads_per_group * L), 1)
        below_diag = (wide_c % L) < wide_r
        sums = jnp.zeros((3 * L, LANES), F32)
        da_q = jnp.zeros((L, LANES), F32)
        dcd_acc = jnp.zeros((1, LANES), F32)
        dd_acc = jnp.zeros((1, LANES), F32)
        for g in range(G):
            bsl = slice(SX + g * N, SX + (g + 1) * N)
            csl = slice(SX + G * N + g * N, SX + G * N + (g + 1) * N)
            Bg = x_ref[:, bsl].astype(BF16)
            Cg = x_ref[:, csl].astype(BF16)
            Gm = _dot(Cg, Bg, _NT)
            GmT = _dot(Bg, Cg, _NT)
            dt_g, e_g, ds_g = _head_spread([dtc, e, ds], g, heads_per_group)
            st = []
            for j in range(g * heads_per_group // 2, (g + 1) * heads_per_group // 2):
                h0 = 2 * j
                sl = slice(2 * P * j, 2 * P * (j + 1))
                s = dict(h0=h0, sl=sl, x2=x_ref[:, sl], dy2=_f32(dy_ref[:, sl]), H2=hst_ref[0, sl, :], dHn=dstate[sl, :])
                gsl = slice(sl.start - g * heads_per_group * P, sl.stop - g * heads_per_group * P)
                s['dt2'], s['e2'], s['ds2'] = dt_g[:, gsl], e_g[:, gsl], ds_g[:, gsl]
                s['X2'] = s['x2'] * s['dt2']
                s['H2b'], s['dHnb'] = s['H2'].astype(BF16), s['dHn'].astype(BF16)
                st.append(s)
            for s in st:
                s['R2'] = _dot(Cg, s['H2b'], _NT)
                s['dXd'] = _dot(Bg, s['dHnb'], _NT)
                s['dM2'] = _dot(s['dy2'].astype(BF16), halves(s['X2'], 0).astype(BF16), _NT)
            dG = jnp.zeros((L, L), F32)
            qs = []
            for s in st:
                mts = []
                for i, h in enumerate((s['h0'], s['h0'] + 1)):
                    z = cs[:, h:h + 1] - csr_ref[h:h + 1, :]
                    Dm = jnp.where(tri, jnp.exp(jnp.where(tri, z, 0.0)), 0.0)
                    DmT = jnp.where(triT, jnp.exp(jnp.where(triT, -z, 0.0)), 0.0)
                    dM = s['dM2'][:, i * L:(i + 1) * L]
                    dG = dG + dM * Dm
                    qs.append((dM * (Gm * Dm)).astype(BF16))
                    mts.append((GmT * DmT).astype(BF16))
                s['dXm'] = _dot(jnp.concatenate(mts, axis=1), halves(s['dy2'], 0).astype(BF16), _NN)
            Wg = _dot(triu_b, jnp.concatenate(qs, axis=1), _NN)
            place = (wc == g * heads_per_group + wl // L).astype(BF16)
            da_q = da_q + _dot(jnp.where(below_diag, Wg, 0.0).astype(BF16), place, _NN)
            dBg = jnp.zeros((L, N), F32)
            dCg = jnp.zeros((L, N), F32)
            des, ddss, dxxs = [], [], []
            for s in st:
                h0, sl, x2, dy2, X2 = s['h0'], s['sl'], s['x2'], s['dy2'], s['X2']
                dR2b = (s['e2'] * dy2).astype(BF16)
                dCg = dCg + _dot(dR2b, s['H2b'], _NN)
                dHr = _dot(dR2b, Cg, _TN)
                dBg = dBg + _dot((X2 * s['ds2']).astype(BF16), s['dHnb'], _NN)
                dX2 = s['ds2'] * s['dXd'] + s['dXm']
                des.append(dy2 * s['R2'])
                ddss.append(s['dXd'] * X2)
                dxxs.append(dX2 * x2)
                prod = s['dHn'] * s['H2']
                for i, h in enumerate((h0, h0 + 1)):
                    rows = jnp.sum(prod[i * P:(i + 1) * P], axis=0, keepdims=True)
                    dcd_acc = dcd_acc + jnp.sum(rows, axis=1, keepdims=True) * (lane1 == h).astype(F32)
                dd_acc = dd_acc + head_sum(jnp.sum(dy2 * x2, axis=0, keepdims=True), h0)
                dsk2 = jnp.where(lane1 < P, dsk[:, h0:h0 + 1], dsk[:, h0 + 1:h0 + 2])
                dx_ref[:, sl] = dX2 * s['dt2'] + dy2 * dsk2
                dstate[sl, :] = jnp.where(row_lo, cd[:, h0:h0 + 1], cd[:, h0 + 1:h0 + 2]) * s['dHn'] + dHr
            stack = jnp.concatenate([jnp.concatenate(v, axis=1) for v in (des, ddss, dxxs)], axis=0).astype(BF16)
            sums = sums + _dot(stack, (gc == g * heads_per_group + gl // P).astype(BF16), _NN)
            dGb = dG.astype(BF16)
            dx_ref[:, bsl] = dBg + _dot(dGb, Cg, _TN)
            dx_ref[:, csl] = dCg + _dot(dGb, Bg, _NN)
        t1 = sums[L:2 * L] * ds
        tail = jnp.sum(t1, axis=0, keepdims=True) + dcd_acc * cd
        dcs = sums[0:L] * e - t1 + jnp.where(last_row, tail, 0.0)
        da_ref[...] = jnp.dot(triu, dcs, precision=HIGHEST, preferred_element_type=F32) + da_q
        ddtx_ref[...] = sums[2 * L:3 * L]
        dd_ref[...] += dd_acc

    vec = pl.BlockSpec((1, LANES), lambda i: (0, 0))
    rev = lambda i: (nc - 1 - i, 0)
    return pl.pallas_call(
        body, name=name,
        out_shape=(_sds((T, XBC), F32), _sds((T, LANES), F32), _sds((T, LANES), F32), _sds((1, LANES), F32)),
        grid=(nc,),
        in_specs=[pl.BlockSpec((L, XBC), rev), pl.BlockSpec((L, LANES), rev),
                  pl.BlockSpec((HR, L), lambda i: (0, nc - 1 - i)), vec, pl.BlockSpec((HR, 1), lambda i: (0, 0)), vec,
                  pl.BlockSpec((1, SX, N), lambda i: (nc - 1 - i, 0, 0)), pl.BlockSpec((L, SX), rev)],
        out_specs=(pl.BlockSpec((L, XBC), rev), pl.BlockSpec((L, LANES), rev), pl.BlockSpec((L, LANES), rev), vec),
        scratch_shapes=[pltpu.VMEM((SX, N), F32), pltpu.VMEM((L, LANES), F32), pltpu.VMEM((HR, L), F32)],
        compiler_params=_cp("arbitrary"))(xbc, dt, dtT, a_col, a_row, d_col, hst, dy)


def _rms_fwd(y, u, zoff, w, *, name):
    T, SX = y.shape
    tt = _tile(T, 256)
    gs = SX // SSM_GROUPS
    zb = zoff // SX

    def body(y_ref, z_ref, w_ref, o_ref):
        for g in range(SSM_GROUPS):
            sl = slice(g * gs, (g + 1) * gs)
            z = _f32(z_ref[:, sl])
            yg = _f32(y_ref[:, sl]) * (z * _sigmoid(z))
            rstd = lax.rsqrt(jnp.mean(yg * yg, axis=-1, keepdims=True) + RMS_EPS)
            o_ref[:, sl] = (yg * rstd * w_ref[:, sl]).astype(BF16)

    row = pl.BlockSpec((tt, SX), lambda i: (i, 0))
    return pl.pallas_call(body, name=name, out_shape=_sds((T, SX), BF16), grid=(T // tt,),
                          in_specs=[row, pl.BlockSpec((tt, SX), lambda i: (i, zb)), pl.BlockSpec((1, SX), lambda i: (0, 0))],
                          out_specs=row, compiler_params=_cp("parallel"))(y, u, w)


def _rms_bwd(y, u, zoff, w, dyn, *, name):
    T, SX = y.shape
    tt = _tile(T, 256)
    gs = SX // SSM_GROUPS
    zb = zoff // SX

    def body(y_ref, z_ref, w_ref, d_ref, dy_ref, dz_ref, dw_ref):
        i = pl.program_id(0)

        @pl.when(i == 0)
        def _():
            dw_ref[...] = jnp.zeros_like(dw_ref)

        for g in range(SSM_GROUPS):
            sl = slice(g * gs, (g + 1) * gs)
            z, yv, d = _f32(z_ref[:, sl]), _f32(y_ref[:, sl]), _f32(d_ref[:, sl])
            sz, dsz = _silu_and_grad(z)
            yg = yv * sz
            rstd = lax.rsqrt(jnp.mean(yg * yg, axis=-1, keepdims=True) + RMS_EPS)
            t = yg * rstd
            dw_ref[:, sl] += jnp.sum(d * t, axis=0, keepdims=True)
            dt_ = d * w_ref[:, sl]
            dyg = rstd * (dt_ - t * jnp.mean(dt_ * t, axis=-1, keepdims=True))
            dy_ref[:, sl] = (dyg * sz).astype(BF16)
            dz_ref[:, sl] = (dyg * yv * dsz).astype(BF16)

    row = pl.BlockSpec((tt, SX), lambda i: (i, 0))
    vec = pl.BlockSpec((1, SX), lambda i: (0, 0))
    return pl.pallas_call(body, name=name, out_shape=(_sds((T, SX), BF16), _sds((T, SX), BF16), _sds((1, SX), F32)),
                          grid=(T // tt,), in_specs=[row, pl.BlockSpec((tt, SX), lambda i: (i, zb)), vec, row],
                          out_specs=(row, row, vec), compiler_params=_cp("arbitrary"))(y, u, w, dyn)


def _gate_fwd(u, goff, ya, yb, *, name):
    T, D = ya.shape
    tt = _tile(T, 512)
    gb = goff // D

    def body(ga_ref, gb_ref, ya_ref, yb_ref, o_ref):
        o_ref[...] = (_sigmoid(_f32(ga_ref[...])) * _f32(ya_ref[...]) + _sigmoid(_f32(gb_ref[...])) * _f32(yb_ref[...])).astype(BF16)

    row = pl.BlockSpec((tt, D), lambda i: (i, 0))
    return pl.pallas_call(body, name=name, out_shape=_sds((T, D), BF16), grid=(T // tt,),
                          in_specs=[pl.BlockSpec((tt, D), lambda i: (i, gb)), pl.BlockSpec((tt, D), lambda i: (i, gb + 1)), row, row],
                          out_specs=row, compiler_params=_cp("parallel"))(u, u, ya, yb)


def _gate_bwd(u, goff, ya, yb, dm, *, name):
    T, D = ya.shape
    tt = _tile(T, 512)
    gb = goff // D

    def body(ga_ref, gb_ref, ya_ref, yb_ref, dm_ref, dya_ref, dyb_ref, dg_ref):
        d = _f32(dm_ref[...])
        sa, sb = _sigmoid(_f32(ga_ref[...])), _sigmoid(_f32(gb_ref[...]))
        dya_ref[...] = (d * sa).astype(BF16)
        dyb_ref[...] = (d * sb).astype(BF16)
        dg_ref[...] = jnp.concatenate([d * _f32(ya_ref[...]) * sa * (1.0 - sa), d * _f32(yb_ref[...]) * sb * (1.0 - sb)], axis=1).astype(BF16)

    row = pl.BlockSpec((tt, D), lambda i: (i, 0))
    return pl.pallas_call(body, name=name, out_shape=(_sds((T, D), BF16), _sds((T, D), BF16), _sds((T, 2 * D), BF16)),
                          grid=(T // tt,),
                          in_specs=[pl.BlockSpec((tt, D), lambda i: (i, gb)), pl.BlockSpec((tt, D), lambda i: (i, gb + 1)), row, row, row],
                          out_specs=(row, row, pl.BlockSpec((tt, 2 * D), lambda i: (i, 0))),
                          compiler_params=_cp("parallel"))(u, u, ya, yb, dm)


def _adamw_math(w, gg, m, v):
    nm = ADAM_B1 * m + (1.0 - ADAM_B1) * gg
    nv = ADAM_B2 * v + (1.0 - ADAM_B2) * (gg * gg)
    m_hat = nm / (1.0 - ADAM_B1 ** ADAM_STEP)
    v_hat = nv / (1.0 - ADAM_B2 ** ADAM_STEP)
    return -ADAM_LR * (m_hat / (jnp.sqrt(v_hat) + ADAM_EPS) + ADAM_WD * w), nm, nv


def _adamw(w, g, m, v, *, name):
    R, C = w.shape
    tr = _pick(R, 256, 8)

    def body(w_ref, g_ref, m_ref, v_ref, d_ref, nm_ref, nv_ref):
        d_ref[...], nm_ref[...], nv_ref[...] = _adamw_math(w_ref[...], g_ref[...], m_ref[...], v_ref[...])

    blk = pl.BlockSpec((tr, C), lambda i: (i, 0))
    out = _sds((R, C), F32)
    return pl.pallas_call(body, name=name, out_shape=(out, out, out), grid=(R // tr,), in_specs=[blk] * 4,
                          out_specs=(blk,) * 3, compiler_params=_cp("parallel"))(w, g, m, v)


def _sum_adamw(parts, w, m, v, *, name):
    depth, C = w.shape[0], w.shape[-1]
    R = w[0].size // C
    tr = _pick(R, 256, 8)

    def body(*refs):
        p_refs = refs[:depth]
        w_ref, m_ref, v_ref, g_ref, d_ref, nm_ref, nv_ref = refs[depth:]
        for l in range(depth):
            @pl.when(pl.program_id(0) == l)
            def _(l=l):
                gg = p_refs[l][0].astype(F32)
                for k in range(1, N_DEV):
                    gg = gg + p_refs[l][k].astype(F32)
                g_ref[...] = gg
                d_ref[...], nm_ref[...], nv_ref[...] = _adamw_math(w_ref[...], gg, m_ref[...], v_ref[...])

    p_specs = [pl.BlockSpec((N_DEV, tr, C), lambda l, i, ll=ll: (0, jnp.where(l == ll, i, 0), 0)) for ll in range(depth)]
    blk = pl.BlockSpec((None, tr, C), lambda l, i: (l, i, 0))
    out = _sds((depth, R, C), F32)
    res = pl.pallas_call(body, name=name, out_shape=(out,) * 4, grid=(depth, R // tr),
                         in_specs=p_specs + [blk, blk, blk], out_specs=(blk,) * 4,
                         compiler_params=_cp("parallel", "parallel"))(
        *[x.reshape(N_DEV, R, C) for x in parts], w.reshape(depth, R, C), m.reshape(depth, R, C), v.reshape(depth, R, C))
    return tuple(r.reshape(w.shape) for r in res)


def _sum_slots(x, *, name):
    n, R, C = x.shape
    tr = _tile(R, 512)

    def body(x_ref, o_ref):
        acc = x_ref[0].astype(F32)
        for k in range(1, n):
            acc = acc + x_ref[k].astype(F32)
        o_ref[...] = acc

    return pl.pallas_call(body, name=name, out_shape=_sds((R, C), F32), grid=(R // tr,),
                          in_specs=[pl.BlockSpec((n, tr, C), lambda i: (0, i, 0))],
                          out_specs=pl.BlockSpec((tr, C), lambda i: (i, 0)), compiler_params=_cp("parallel"))(x)


def _exchange(xs, *, scatter, name):
    n = len(xs)

    def body(*refs):
        x_refs, o_refs = refs[:n], refs[n:2 * n]
        send_sems, recv_sems, local_sems = refs[2 * n:]
        mx, my, mc = lax.axis_index("x"), lax.axis_index("y"), lax.axis_index("c")
        me = 4 * mx + 2 * my + mc

        def src(i, d):
            return x_refs[i].at[d] if scatter else x_refs[i]

        locals_ = [pltpu.make_async_copy(src(i, me), o_refs[i].at[me], local_sems.at[i]) for i in range(n)]
        for cp in locals_:
            cp.start()
        sends, recvs = [], []
        for k in range(1, N_DEV):
            px = 1 - mx if k & 4 else mx
            py = 1 - my if k & 2 else my
            pc = 1 - mc if k & 1 else mc
            peer = 4 * px + 2 * py + pc
            for i in range(n):
                common = dict(send_sem=send_sems.at[k - 1, i], recv_sem=recv_sems.at[k - 1, i],
                              device_id=(px, py, pc), device_id_type=pl.DeviceIdType.MESH)
                sends.append(pltpu.make_async_remote_copy(src_ref=src(i, peer), dst_ref=o_refs[i].at[me], **common))
                recvs.append(pltpu.make_async_remote_copy(src_ref=src(i, peer), dst_ref=o_refs[i].at[peer], **common))
        for cp in sends:
            cp.start()
        for cp in recvs:
            cp.wait_recv()
        for cp in sends:
            cp.wait_send()
        for cp in locals_:
            cp.wait()

    any_spec = pl.BlockSpec(memory_space=pl.ANY)
    out_shape = tuple(_sds(x.shape if scatter else (N_DEV,) + x.shape, x.dtype) for x in xs)
    return pl.pallas_call(
        body, name=name, out_shape=out_shape, in_specs=[any_spec] * n, out_specs=(any_spec,) * n,
        scratch_shapes=[pltpu.SemaphoreType.DMA((N_DEV - 1, n)), pltpu.SemaphoreType.DMA((N_DEV - 1, n)),
                        pltpu.SemaphoreType.DMA((n,))])(*xs)


def _to_rows(flat, row_mult):
    n = flat.shape[-1]
    per = row_mult * LANES
    pad = (-n) % per
    flat = jnp.pad(flat, [(0, 0)] * (flat.ndim - 1) + [(0, pad)])
    return flat.reshape(flat.shape[:-1] + ((n + pad) // LANES, LANES))


def _peers():
    mx, my, mc = lax.axis_index("x"), lax.axis_index("y"), lax.axis_index("c")
    out = []
    for k in range(1, N_DEV):
        px = 1 - mx if k & 4 else mx
        py = 1 - my if k & 2 else my
        pc = 1 - mc if k & 1 else mc
        out.append(((px, py, pc), 4 * px + 2 * py + pc))
    return 4 * mx + 2 * my + mc, out


_HBM = pl.BlockSpec(memory_space=pltpu.HBM)
_SEM = pl.BlockSpec(memory_space=pltpu.SEMAPHORE)


def _exchange_start(xs, *, scatter, name):
    n = len(xs)

    def body(*refs):
        x_refs, land_refs, send_sems, recv_sems, token = refs[:n], refs[n:2 * n], refs[2 * n], refs[2 * n + 1], refs[-1]
        me, peers = _peers()
        for k, (dev, peer) in enumerate(peers):
            for i in range(n):
                pltpu.make_async_remote_copy(
                    src_ref=x_refs[i].at[peer] if scatter else x_refs[i], dst_ref=land_refs[i].at[me],
                    send_sem=send_sems.at[k * n + i], recv_sem=recv_sems.at[k * n + i],
                    device_id=dev, device_id_type=pl.DeviceIdType.MESH).start()
        token[...] = jnp.zeros_like(token)

    land_shapes = [x.shape if scatter else (N_DEV,) + x.shape for x in xs]
    lands = [pltpu.with_memory_space_constraint(lax.empty(s, x.dtype), pltpu.HBM) for s, x in zip(land_shapes, xs)]
    srcs = [pltpu.with_memory_space_constraint(x, pltpu.HBM) for x in xs]
    out = pl.pallas_call(
        body, name=name,
        out_shape=(pltpu.SemaphoreType.DMA(((N_DEV - 1) * n,)), pltpu.SemaphoreType.DMA(((N_DEV - 1) * n,)),
                   *[pltpu.HBM(x.shape, x.dtype) for x in xs], *[pltpu.HBM(s, x.dtype) for s, x in zip(land_shapes, xs)],
                   _sds((8, LANES), F32)),
        in_specs=(_HBM,) * (2 * n), out_specs=(_SEM, _SEM) + (_HBM,) * (2 * n) + (pl.BlockSpec(memory_space=pltpu.VMEM),),
        input_output_aliases={i: 2 + i for i in range(2 * n)},
        compiler_params=pltpu.CompilerParams(has_side_effects=pltpu.SideEffectType.DATAFLOW_SIDE_EFFECTING))(*srcs, *lands)
    return (out[0], out[1], list(out[2:2 + n]), list(out[2 + n:2 + 2 * n])), out[-1]


def _exchange_wait(handle, after, *, scatter, name):
    send_sems, recv_sems, srcs, lands = handle
    n = len(srcs)

    def body(*refs):
        x_refs, land_refs, send_sems, recv_sems = refs[:n], refs[n:2 * n], refs[2 * n], refs[2 * n + 1]
        me, peers = _peers()
        for k, (dev, peer) in enumerate(peers):
            for i in range(n):
                cp = pltpu.make_async_remote_copy(
                    src_ref=x_refs[i].at[peer] if scatter else x_refs[i], dst_ref=land_refs[i].at[peer],
                    send_sem=send_sems.at[k * n + i], recv_sem=recv_sems.at[k * n + i],
                    device_id=dev, device_id_type=pl.DeviceIdType.MESH)
                cp.wait_send()
                cp.wait_recv()

    out = pl.pallas_call(
        body, name=name, out_shape=tuple(pltpu.HBM(a.shape, a.dtype) for a in srcs + lands),
        in_specs=(_HBM,) * (2 * n) + (_SEM, _SEM, pl.BlockSpec(memory_space=pl.ANY)), out_specs=(_HBM,) * (2 * n),
        input_output_aliases={i: i for i in range(2 * n)},
        compiler_params=pltpu.CompilerParams(has_side_effects=pltpu.SideEffectType.DATAFLOW_SIDE_EFFECTING))(
        *srcs, *lands, send_sems, recv_sems, after)
    return list(out[n:])


def _with_own_slot(lands, xs, *, scatter):
    me = 4 * lax.axis_index("x") + 2 * lax.axis_index("y") + lax.axis_index("c")
    out = []
    for land, x in zip(lands, xs):
        own = lax.dynamic_index_in_dim(x, me, 0, keepdims=True) if scatter else x[None]
        out.append(lax.dynamic_update_index_in_dim(land, own, me, 0))
    return out


def _wire_shards(p, l, names):
    return [p[k][l].astype(WIRE) if SHARDED[k][1] else p[k][l] for k in names]


def _full_weights(got, p, names):
    full = {}
    for k, g in zip(names, got):
        axis, shape = SHARDED[k][0] - 1, p[k].shape[1:]
        g = jnp.moveaxis(g, 0, axis)
        full[k] = g.reshape(shape[:axis] + (N_DEV * shape[axis],) + shape[axis + 1:])
    return full


def _grad_parts(grads, p, names):
    parts = []
    for k in names:
        axis, g = SHARDED[k][0] - 1, grads[k]
        g = g.reshape(g.shape[:axis] + (N_DEV, p[k].shape[1:][axis]) + g.shape[axis + 1:])
        parts.append(jnp.moveaxis(g, axis, 0).astype(WIRE))
    return parts


def _allreduce_small(vals):
    metas = [(k, v.shape, v.size) for k, v in vals.items()]
    flat = jnp.concatenate([v.reshape(-1) for v in vals.values()])
    got, = _exchange([_to_rows(flat, 8)], scatter=False, name="gather_small_grads")
    summed = _sum_slots(got, name="sum_small_grads").reshape(-1)
    out, off = {}, 0
    for k, shape, n in metas:
        out[k] = summed[off:off + n].reshape(shape)
        off += n
    return out


def _row(v):
    return v.reshape(1, -1).astype(F32)


def _lanes(v):
    return jnp.pad(v.astype(F32), (0, LANES - v.shape[0])).reshape(1, LANES)


EARLY = ('w_in', 'conv_dw_w', 'ssm_conv_w')
LATE = tuple(k for k in SHARDED if k not in EARLY)
FFN = ('w_ffn_up', 'ffn_dw_w', 'w_ffn_down')
NOT_FFN = tuple(k for k in SHARDED if k not in FFN)


def _layer_weights(p, full, l, dims):
    D, SX, XBC, H, FF = dims
    assert _tile(D, GLU_TILE) == D and _tile(FF, FFN_TILE) == FF
    W = {}
    if 'w_in' in full:
        w_in = full['w_in']
        o_dt = 2 * D + SX + XBC
        w_pieces = [w_in[:, :2 * D], w_in[:, 2 * D:2 * D + SX], w_in[:, 2 * D + SX:o_dt], w_in[:, o_dt + H:]]
        a_head = -jnp.exp(p['ssm_a_log'][l].astype(F32))
        hr = -(-H // 8) * 8
        W.update(
            w_main=jnp.concatenate(w_pieces, axis=1), w_pieces=w_pieces,
            w_dt=jnp.pad(w_in[:, o_dt:o_dt + H], ((0, 0), (0, LANES - H))),
            conv_w=full['conv_dw_w'], conv_b=_row(p['conv_dw_b'][l]), conv_g=_row(p['conv_ln_g'][l]),
            conv_beta=_row(p['conv_ln_b'][l]), ssm_w=full['ssm_conv_w'], ssm_b=_row(p['ssm_conv_b'][l]),
            dt_bias=_lanes(p['ssm_dt_bias'][l]), a_col=_lanes(a_head),
            a_row=jnp.pad(a_head, (0, hr - H)).reshape(hr, 1), d_col=_lanes(p['ssm_d'][l]),
            norm_w=_row(p['ssm_norm_w'][l]), ln1_g=_row(p['ln1_g'][l]), ln1_b=_row(p['ln1_b'][l]),
            ffn_b=_row(p['ffn_dw_b'][l]), ln2_g=_row(p['ln2_g'][l]), ln2_b=_row(p['ln2_b'][l]))
    if 'w_o' in full:
        W.update(w_conv_out=full['w_conv_out'], w_ssm_out=full['w_ssm_out'], w_o=full['w_o'], w_up=full['w_ffn_up'],
                 ffn_w=full['ffn_dw_w'], w_down=full['w_ffn_down'])
    return W


def _layer_fwd(h, hb, W, late, l, dims):
    D, SX, XBC, H, FF = dims
    o_z, o_xbc, o_gate = 2 * D, 2 * D + SX, 2 * D + SX + XBC
    hr = W['a_row'].shape[0]
    t = f"l{l}"
    u = _mm(hb, W['w_main'], mode='nn', out_dtype=BF16, name=t + "_in_proj")
    udt = _mm(hb, W['w_dt'], mode='nn', out_dtype=F32, name=t + "_dt_proj")
    v1, v3 = _conf_fwd(u, W['conv_w'], W['conv_b'], W['conv_g'], W['conv_beta'], D=D, name=t + "_conf_fwd")
    xbc = _conv_act_fwd(u, o_xbc, W['ssm_w'], W['ssm_b'], ffn=False, name=t + "_ssm_conv")
    dt = _dt_fwd(udt, W['dt_bias'], H=H, name=t + "_dt")
    dtT = jnp.pad(dt[:, :H].T, ((0, hr - H), (0, 0)))
    y, hst = _ssd_fwd(xbc, dt, dtT, W['a_col'], W['a_row'], W['d_col'], H=H, name=t + "_ssd_fwd")
    yn = _rms_fwd(y, u, o_z, W['norm_w'], name=t + "_rms_fwd")
    W = {**W, **late(yn)}
    ya = _mm(v3, W['w_conv_out'], mode='nn', out_dtype=BF16, name=t + "_conv_out")
    yb = _mm(yn, W['w_ssm_out'], mode='nn', out_dtype=BF16, name=t + "_ssm_out")
    m = _gate_fwd(u, o_gate, ya, yb, name=t + "_gate_fwd")
    h1, s1, h1b = _mm_res_ln(m, W['w_o'], h, W['ln1_g'], W['ln1_b'], name=t + "_mix_ln1")
    uf = _mm(h1b, W['w_up'], mode='nn', out_dtype=BF16, name=t + "_ffn_up")
    act = _conv_act_fwd(uf, 0, W['ffn_w'], W['ffn_b'], ffn=True, name=t + "_ffn_conv")
    h2, s2, h2b = _mm_res_ln(act, W['w_down'], h1, W['ln2_g'], W['ln2_b'], name=t + "_ffn_down_ln2")
    saved = dict(hb=hb, u=u, udt=udt, v1=v1, v3=v3, ya=ya, xbc=xbc, dt=dt, dtT=dtT, y=y, hst=hst, yn=yn, yb=yb, m=m,
                 h1b=h1b, s1=s1, uf=uf, act=act, s2=s2)
    return h2, h2b, saved, W


def _layer_bwd(dh2, W, S, l, dims, on_ffn_grads=None, on_all_grads=None):
    D, SX, XBC, H, FF = dims
    o_z, o_xbc, o_gate = 2 * D, 2 * D + SX, 2 * D + SX + XBC
    t = f"l{l}"
    g = {}
    ds2, g['ln2_g'], g['ln2_b'] = _ln_bwd(S['s2'], dh2, W['ln2_g'], W['ln2_b'], silu=False, name=t + "_ln2_bwd")
    g['w_down'] = _mm(S['act'], ds2, mode='tn', out_dtype=WIRE, name=t + "_dw_down")
    dact = _mm(ds2, W['w_down'], mode='nt', out_dtype=F32, name=t + "_dact")
    duf, g['ffn_w'], g['ffn_b'] = _conv_act_bwd(S['uf'], 0, W['ffn_w'], W['ffn_b'], dact, ffn=True, name=t + "_ffn_conv_bwd")
    g['w_up'] = _mm(S['h1b'], duf, mode='tn', out_dtype=WIRE, name=t + "_dw_up")
    dh1 = _mm(duf, W['w_up'], mode='nt', out_dtype=F32, res=ds2, res_scale=DN_ALPHA, name=t + "_dh1")
    ln1_g = W['ln1_g'] if on_ffn_grads is None else W['ln1_g'] + on_ffn_grads(g)[0:1, 0:1]
    ds1, g['ln1_g'], g['ln1_b'] = _ln_bwd(S['s1'], dh1, ln1_g, W['ln1_b'], silu=False, name=t + "_ln1_bwd")
    g['w_o'] = _mm(S['m'], ds1, mode='tn', out_dtype=WIRE, name=t + "_dw_o")
    dm = _mm(ds1, W['w_o'], mode='nt', out_dtype=BF16, name=t + "_dm")
    dya, dyb, dgate = _gate_bwd(S['u'], o_gate, S['ya'], S['yb'], dm, name=t + "_gate_bwd")
    g['w_conv_out'] = _mm(S['v3'], dya, mode='tn', out_dtype=WIRE, name=t + "_dw_conv_out")
    dv3 = _mm(dya, W['w_conv_out'], mode='nt', out_dtype=BF16, name=t + "_dv3")
    dv1, g['conv_g'], g['conv_beta'] = _ln_bwd(S['v1'], dv3, W['conv_g'], W['conv_beta'], silu=True, name=t + "_conv_ln_bwd")
    dglu, g['conv_w'], g['conv_b'] = _glu_conv_bwd(dv1, S['u'], W['conv_w'], name=t + "_conf_conv_bwd")
    g['w_ssm_out'] = _mm(S['yn'], dyb, mode='tn', out_dtype=WIRE, name=t + "_dw_ssm_out")
    dyn = _mm(dyb, W['w_ssm_out'], mode='nt', out_dtype=BF16, name=t + "_dyn")
    dy, dz, g['norm_w'] = _rms_bwd(S['y'], S['u'], o_z, W['norm_w'], dyn, name=t + "_rms_bwd")
    dxbc_c, da, ddtx, g['d'] = _ssd_bwd(S['xbc'], S['dt'], S['dtT'], W['a_col'], W['a_row'], W['d_col'], S['hst'], dy,
                                        H=H, name=t + "_ssd_bwd")
    ddt_raw, g['dt_bias'], g['a_log'] = _dt_bwd(da, ddtx, S['dt'], S['udt'], W['dt_bias'], W['a_col'], H=H, name=t + "_dt_bwd")
    dxbc, g['ssm_w'], g['ssm_b'] = _conv_act_bwd(S['u'], o_xbc, W['ssm_w'], W['ssm_b'], dxbc_c, ffn=False, name=t + "_ssm_conv_bwd")
    du = [dglu, dz, dxbc, dgate]
    g['w_pieces'] = [_mm(S['hb'], d, mode='tn', out_dtype=WIRE, name=f"{t}_dw_in{i}") for i, d in enumerate(du)]
    g['w_dt'] = _mm(S['hb'], ddt_raw, mode='tn', out_dtype=WIRE, name=t + "_dw_dt")
    w_dt = W['w_dt'] if on_all_grads is None else W['w_dt'] + on_all_grads(g)[0:1, 0:1].astype(W['w_dt'].dtype)
    dh = _mm_nt_cat(du + [ddt_raw], W['w_pieces'] + [w_dt], ds1, DN_ALPHA, name=t + "_dh_in")
    return dh, g


def _layer_grads_to_params(g, dims):
    D, SX, XBC, H, FF = dims
    glu, dz, dxbc, dgate = g['w_pieces']
    w_in = jnp.concatenate([glu, dz, dxbc, g['w_dt'][:, :H], dgate], axis=1)
    return dict(
        w_in=w_in, conv_dw_w=g['conv_w'], conv_dw_b=g['conv_b'][0], conv_ln_g=g['conv_g'][0], conv_ln_b=g['conv_beta'][0],
        w_conv_out=g['w_conv_out'], ssm_conv_w=g['ssm_w'], ssm_conv_b=g['ssm_b'][0],
        ssm_dt_bias=g['dt_bias'][0, :H], ssm_a_log=g['a_log'][0, :H], ssm_d=g['d'][0, :H], ssm_norm_w=g['norm_w'][0],
        w_ssm_out=g['w_ssm_out'], w_o=g['w_o'], ln1_g=g['ln1_g'][0], ln1_b=g['ln1_b'][0],
        w_ffn_up=g['w_up'], ffn_dw_w=g['ffn_w'], ffn_dw_b=g['ffn_b'][0], w_ffn_down=g['w_down'], ln2_g=g['ln2_g'][0], ln2_b=g['ln2_b'][0])


def kernel(x, ln_in_g, ln_in_b, w_in, conv_dw_w, conv_dw_b, conv_ln_g, conv_ln_b, w_conv_out, ssm_conv_w, ssm_conv_b, ssm_dt_bias, ssm_a_log, ssm_d, ssm_norm_w, w_ssm_out, w_o, ln1_g, ln1_b, w_ffn_up, ffn_dw_w, ffn_dw_b, w_ffn_down, ln2_g, ln2_b, loss_target, m_ln_in_g, m_ln_in_b, m_w_in, m_conv_dw_w, m_conv_dw_b, m_conv_ln_g, m_conv_ln_b, m_w_conv_out, m_ssm_conv_w, m_ssm_conv_b, m_ssm_dt_bias, m_ssm_a_log, m_ssm_d, m_ssm_norm_w, m_w_ssm_out, m_w_o, m_ln1_g, m_ln1_b, m_w_ffn_up, m_ffn_dw_w, m_ffn_dw_b, m_w_ffn_down, m_ln2_g, m_ln2_b, v_ln_in_g, v_ln_in_b, v_w_in, v_conv_dw_w, v_conv_dw_b, v_conv_ln_g, v_conv_ln_b, v_w_conv_out, v_ssm_conv_w, v_ssm_conv_b, v_ssm_dt_bias, v_ssm_a_log, v_ssm_d, v_ssm_norm_w, v_w_ssm_out, v_w_o, v_ln1_g, v_ln1_b, v_w_ffn_up, v_ffn_dw_w, v_ffn_dw_b, v_w_ffn_down, v_ln2_g, v_ln2_b):
    weights = (ln_in_g, ln_in_b, w_in, conv_dw_w, conv_dw_b, conv_ln_g, conv_ln_b, w_conv_out, ssm_conv_w, ssm_conv_b,
               ssm_dt_bias, ssm_a_log, ssm_d, ssm_norm_w, w_ssm_out, w_o, ln1_g, ln1_b, w_ffn_up, ffn_dw_w, ffn_dw_b,
               w_ffn_down, ln2_g, ln2_b)
    moments_m = (m_ln_in_g, m_ln_in_b, m_w_in, m_conv_dw_w, m_conv_dw_b, m_conv_ln_g, m_conv_ln_b, m_w_conv_out,
                 m_ssm_conv_w, m_ssm_conv_b, m_ssm_dt_bias, m_ssm_a_log, m_ssm_d, m_ssm_norm_w, m_w_ssm_out, m_w_o,
                 m_ln1_g, m_ln1_b, m_w_ffn_up, m_ffn_dw_w, m_ffn_dw_b, m_w_ffn_down, m_ln2_g, m_ln2_b)
    moments_v = (v_ln_in_g, v_ln_in_b, v_w_in, v_conv_dw_w, v_conv_dw_b, v_conv_ln_g, v_conv_ln_b, v_w_conv_out,
                 v_ssm_conv_w, v_ssm_conv_b, v_ssm_dt_bias, v_ssm_a_log, v_ssm_d, v_ssm_norm_w, v_w_ssm_out, v_w_o,
                 v_ln1_g, v_ln1_b, v_w_ffn_up, v_ffn_dw_w, v_ffn_dw_b, v_w_ffn_down, v_ln2_g, v_ln2_b)
    p = dict(zip(PARAMS, weights))
    pm = dict(zip(PARAMS, moments_m))
    pv = dict(zip(PARAMS, moments_v))

    T, D = x.shape[1], x.shape[2]
    SX = w_ssm_out.shape[1] * N_DEV
    XBC = ssm_conv_b.shape[-1]
    H = ssm_d.shape[-1]
    FF = ffn_dw_b.shape[-1] // 2
    dims = (D, SX, XBC, H, FF)
    depth = w_in.shape[0]

    ALL = tuple(SHARDED)
    got = _exchange(_wire_shards(p, 0, EARLY), scatter=False, name="gather_l0_first")
    rest_shards, got = lax.optimization_barrier((_wire_shards(p, 0, LATE), got))
    rest_handle, tok = _exchange_start(rest_shards, scatter=False, name="gather_l0_rest_start")
    first = _layer_weights(p, _full_weights(got, p, EARLY), 0, dims)
    in_flight = {}

    def start_next_gather(l, behind):
        shards, _ = lax.optimization_barrier((_wire_shards(p, l, ALL), behind))
        in_flight[l], t = _exchange_start(shards, scatter=False, name=f"gather_l{l}_start")
        in_flight[l] = (in_flight[l], shards)
        return t

    def late_weights(l):
        def late(x):
            if l > 0:
                return {}
            got = _with_own_slot(_exchange_wait(rest_handle, x, scatter=False, name="gather_l0_rest_wait"), rest_shards, scatter=False)
            W = _layer_weights(p, _full_weights(got, p, LATE), 0, dims)
            if depth > 1:
                t = start_next_gather(1, got)
                W['w_conv_out'] = W['w_conv_out'] + t[0:1, 0:1].astype(W['w_conv_out'].dtype)
            return W
        return late

    xs = x.reshape(T, D)
    h, hb = _ln_fwd(xs, _row(ln_in_g) + tok[0:1, 0:1], _row(ln_in_b), name="ln_in_fwd")
    saved, layers = [], []
    for l in range(depth):
        if l > 0:
            handle, shards = in_flight.pop(l)
            got = _with_own_slot(_exchange_wait(handle, h, scatter=False, name=f"gather_l{l}_wait"), shards, scatter=False)
            first = _layer_weights(p, _full_weights(got, p, ALL), l, dims)
            if l + 1 < depth:
                t = start_next_gather(l + 1, got)
                first['w_dt'] = first['w_dt'] + t[0:1, 0:1].astype(first['w_dt'].dtype)
        h, hb, s, W = _layer_fwd(h, hb, first, late_weights(l), l, dims)
        saved.append(s)
        layers.append(W)
    loss_part, dh = _loss_fwd_bwd(h, loss_target.reshape(T, D), name="loss")

    layer_grads = [None] * depth
    flying, arrived = [], {}

    def wait_flying(after):
        while flying:
            l_, names, handle, parts = flying.pop(0)
            lands = _with_own_slot(_exchange_wait(handle, after, scatter=True, name=f"scatter_l{l_}_{names[0]}_wait"), parts, scatter=True)
            arrived.update({(l_, k): a for k, a in zip(names, lands)})
            after = lands[0]
        return after

    def start_scatter(l, names, grads_now, after):
        parts, _ = lax.optimization_barrier((_grad_parts(grads_now, p, names), wait_flying(after)))
        handle, t = _exchange_start(parts, scatter=True, name=f"scatter_l{l}_{names[0]}_start")
        flying.append((l, names, handle, parts))
        return t

    def on_ffn_grads(g):
        return start_scatter(0, FFN, dict(w_ffn_up=g['w_up'], ffn_dw_w=g['ffn_w'], w_ffn_down=g['w_down']), g['w_up'])

    def on_all_grads(g):
        layer_grads[0] = _layer_grads_to_params(g, dims)
        return start_scatter(0, NOT_FFN, layer_grads[0], g['w_dt'])

    tok = jnp.zeros((8, LANES), F32)
    for l in reversed(range(depth)):
        W = dict(layers[l], ln2_g=layers[l]['ln2_g'] + tok[0:1, 0:1])
        if l > 0:
            dh, g = _layer_bwd(dh, W, saved[l], l, dims)
            layer_grads[l] = _layer_grads_to_params(g, dims)
            tok = start_scatter(l, ALL, layer_grads[l], dh)
        else:
            dh, g = _layer_bwd(dh, W, saved[l], l, dims, on_ffn_grads, on_all_grads)
    grad_x, dg_in, db_in = _ln_bwd(xs, dh, _row(ln_in_g), _row(ln_in_b), silu=False, name="ln_in_bwd")
    wait_flying(grad_x)

    small = {k: jnp.stack([layer_grads[l][k] for l in range(depth)]) for k in layer_grads[0] if k not in SHARDED}
    small['ln_in_g'], small['ln_in_b'], small['loss'] = dg_in[0], db_in[0], loss_part
    grads = _allreduce_small(small)
    loss = grads.pop('loss').reshape(())

    delta, new_m, new_v = {}, {}, {}
    for k in SHARDED:
        grads[k], delta[k], new_m[k], new_v[k] = _sum_adamw([arrived[(l, k)] for l in range(depth)], p[k], pm[k], pv[k],
                                                            name="sum_adamw_" + k)
    rest = [k for k in PARAMS if k not in SHARDED]
    flat = lambda d: _to_rows(jnp.concatenate([d[k].reshape(-1) for k in rest]), 8)
    d_, m_, v_ = _adamw(flat(p), flat(grads), flat(pm), flat(pv), name="adamw_small")
    off = 0
    for k in rest:
        n, shp = p[k].size, p[k].shape
        delta[k] = d_.reshape(-1)[off:off + n].reshape(shp)
        new_m[k] = m_.reshape(-1)[off:off + n].reshape(shp)
        new_v[k] = v_.reshape(-1)[off:off + n].reshape(shp)
        off += n

    return (loss, grad_x.reshape(x.shape), *[grads[k] for k in PARAMS], *[delta[k] for k in PARAMS],
            *[new_m[k] for k in PARAMS], *[new_v[k] for k in PARAMS])
```

```python
import math

import jax
import jax.numpy as jnp
from jax import lax
from jax.experimental import pallas as pl
from jax.experimental.pallas import tpu as pltpu

F32 = jnp.float32
BF16 = jnp.bfloat16
WIRE = jnp.bfloat16
HIGHEST = lax.Precision.HIGHEST

DEPTH = 2
SSM_STATE = 128
SSM_CHUNK = 128
SSM_GROUPS = 4
SSM_HEAD_DIM = 64
DN_ALPHA = (2 * DEPTH) ** 0.25
LN_EPS = 1e-5
RMS_EPS = 1e-5
ADAM_LR = 0.001
ADAM_B1 = 0.9
ADAM_B2 = 0.999
ADAM_EPS = 1e-08
ADAM_WD = 0.01
ADAM_STEP = 10

N_DEV = 8
LANES = 128
VMEM_LIMIT = 48 * 1024 * 1024
WHOLE_K_VMEM = 40 * 1024 * 1024
GLU_TILE = 1024
FFN_TILE = 2816
CONV_ROWS = 256

PARAMS = ['ln_in_g', 'ln_in_b', 'w_in', 'conv_dw_w', 'conv_dw_b', 'conv_ln_g', 'conv_ln_b', 'w_conv_out',
          'ssm_conv_w', 'ssm_conv_b', 'ssm_dt_bias', 'ssm_a_log', 'ssm_d', 'ssm_norm_w', 'w_ssm_out', 'w_o',
          'ln1_g', 'ln1_b', 'w_ffn_up', 'ffn_dw_w', 'ffn_dw_b', 'w_ffn_down', 'ln2_g', 'ln2_b']
SHARDED = {'w_in': (2, True), 'conv_dw_w': (2, False), 'w_conv_out': (1, True), 'ssm_conv_w': (2, False),
           'w_ssm_out': (1, True), 'w_o': (1, True), 'w_ffn_up': (2, True), 'ffn_dw_w': (2, False),
           'w_ffn_down': (1, True)}


def _sds(shape, dtype):
    return jax.ShapeDtypeStruct(tuple(shape), dtype)


def _tile(dim, pref):
    return pref if dim % pref == 0 else dim


def _pick(dim, pref, mult):
    best = None
    for t in range(mult, min(dim, pref) + 1, mult):
        if dim % t == 0:
            best = t
    return best or dim


def _cp(*sem):
    return pltpu.CompilerParams(dimension_semantics=sem, vmem_limit_bytes=VMEM_LIMIT)


def _f32(x):
    return x.astype(F32)


def _sigmoid(x):
    return 1.0 / (1.0 + jnp.exp(-x))


def _silu_and_grad(x):
    s = _sigmoid(x)
    return x * s, s * (1.0 + x * (1.0 - s))


def _ln_stats(s):
    mu = jnp.mean(s, axis=-1, keepdims=True)
    xc = s - mu
    var = jnp.mean(xc * xc, axis=-1, keepdims=True)
    rstd = lax.rsqrt(var + LN_EPS)
    return xc * rstd, rstd


def _dot(a, b, dims):
    return lax.dot_general(a, b, (dims, ((), ())), preferred_element_type=F32)


_NN = ((1,), (0,))
_NT = ((1,), (1,))
_TN = ((0,), (0,))


def _mm(a, b, *, mode, out_dtype, name, res=None, res_scale=1.0, tm=1024, tn=1408, tk=1408):
    if mode == 'nn':
        (M, K), (_, N) = a.shape, b.shape
    elif mode == 'nt':
        (M, K), (N, _) = a.shape, b.shape
    else:
        (K, M), (_, N) = a.shape, b.shape
        tm, tk = 1408, 1024
    tm, tn, tk = _pick(M, tm, 8), _pick(N, tn, LANES), _pick(K, tk, LANES)
    if mode != 'tn':
        for rows in (tm, _pick(M, tm // 2, 8), _pick(M, tm // 4, 8)):
            blocks = rows * K * a.dtype.itemsize + tn * K * b.dtype.itemsize + rows * tn * (4 + (4 if res is not None else 0))
            if 2 * blocks <= WHOLE_K_VMEM:
                tm, tk = rows, K
                break
    nk = K // tk
    dims = {'nn': _NN, 'nt': _NT, 'tn': _TN}[mode]

    def body(*refs):
        if res is None:
            a_ref, b_ref, o_ref = refs[:3]
            r_ref = None
        else:
            a_ref, b_ref, r_ref, o_ref = refs[:4]
        p = _dot(a_ref[...].astype(BF16), b_ref[...].astype(BF16), dims)

        def finish(acc):
            if r_ref is not None:
                acc = acc + res_scale * r_ref[...]
            o_ref[...] = acc.astype(out_dtype)

        if nk == 1:
            finish(p)
        else:
            acc_ref = refs[-1]
            k = pl.program_id(2)

            @pl.when(k == 0)
            def _():
                acc_ref[...] = p

            @pl.when(k > 0)
            def _():
                acc_ref[...] += p

            @pl.when(k == nk - 1)
            def _():
                finish(acc_ref[...])

    if mode == 'nn':
        a_spec = pl.BlockSpec((tm, tk), lambda i, j, k: (i, k))
        b_spec = pl.BlockSpec((tk, tn), lambda i, j, k: (k, j))
    elif mode == 'nt':
        a_spec = pl.BlockSpec((tm, tk), lambda i, j, k: (i, k))
        b_spec = pl.BlockSpec((tn, tk), lambda i, j, k: (j, k))
    else:
        a_spec = pl.BlockSpec((tk, tm), lambda i, j, k: (k, i))
        b_spec = pl.BlockSpec((tk, tn), lambda i, j, k: (k, j))
    o_spec = pl.BlockSpec((tm, tn), lambda i, j, k: (i, j))
    in_specs = [a_spec, b_spec] + ([o_spec] if res is not None else [])
    args = (a, b) + ((res,) if res is not None else ())
    return pl.pallas_call(
        body, name=name, out_shape=_sds((M, N), out_dtype), grid=(M // tm, N // tn, nk),
        in_specs=in_specs, out_specs=o_spec,
        scratch_shapes=[pltpu.VMEM((tm, tn), F32)] if nk > 1 else [],
        compiler_params=_cp("parallel", "parallel", "arbitrary"))(*args)


def _mm_nt_cat(a_list, b_list, res, res_scale, *, name, tm=512, tn=512):
    M, N = a_list[0].shape[0], b_list[0].shape[0]
    tm, tn = _pick(M, tm, 8), _pick(N, tn, LANES)
    n = len(a_list)

    def body(*refs):
        a_refs, b_refs, r_ref, o_ref = refs[:n], refs[n:2 * n], refs[2 * n], refs[2 * n + 1]
        acc = res_scale * r_ref[...]
        for i in range(n):
            acc = acc + _dot(a_refs[i][...].astype(BF16), b_refs[i][...].astype(BF16), _NT)
        o_ref[...] = acc

    a_specs = [pl.BlockSpec((tm, a.shape[1]), lambda j, m: (m, 0)) for a in a_list]
    b_specs = [pl.BlockSpec((tn, b.shape[1]), lambda j, m: (j, 0)) for b in b_list]
    blk = pl.BlockSpec((tm, tn), lambda j, m: (m, j))
    return pl.pallas_call(
        body, name=name, out_shape=_sds((M, N), F32), grid=(N // tn, M // tm), in_specs=a_specs + b_specs + [blk],
        out_specs=blk, compiler_params=_cp("parallel", "parallel"))(*a_list, *b_list, res)


def _mm_res_ln(a, w, res, g, b, *, name, tm=512, tk=1408):
    (M, K), (_, N) = a.shape, w.shape
    tm, tk = _pick(M, tm, 8), _pick(K, tk, LANES)
    if 2 * (tm * K * a.dtype.itemsize + K * N * w.dtype.itemsize + tm * N * 14) <= WHOLE_K_VMEM:
        tk = K
    nk = K // tk

    def body(a_ref, w_ref, r_ref, g_ref, b_ref, h_ref, s_ref, hb_ref, acc_ref):
        k = pl.program_id(1)
        p = _dot(a_ref[...].astype(BF16), w_ref[...].astype(BF16), _NN)

        def finish(acc):
            s = DN_ALPHA * r_ref[...] + acc
            xhat, _ = _ln_stats(s)
            s_ref[...] = s
            h = xhat * g_ref[...] + b_ref[...]
            h_ref[...] = h
            hb_ref[...] = h.astype(BF16)

        if nk == 1:
            finish(p)
            return

        @pl.when(k == 0)
        def _():
            acc_ref[...] = p

        @pl.when(k > 0)
        def _():
            acc_ref[...] += p

        @pl.when(k == nk - 1)
        def _():
            finish(acc_ref[...])

    row = pl.BlockSpec((tm, N), lambda i, k: (i, 0))
    vec = pl.BlockSpec((1, N), lambda i, k: (0, 0))
    return pl.pallas_call(
        body, name=name, out_shape=(_sds((M, N), F32), _sds((M, N), F32), _sds((M, N), BF16)), grid=(M // tm, nk),
        in_specs=[pl.BlockSpec((tm, tk), lambda i, k: (i, k)), pl.BlockSpec((tk, N), lambda i, k: (k, 0)), row, vec, vec],
        out_specs=(row, row, row), scratch_shapes=[pltpu.VMEM((tm, N), F32)],
        compiler_params=_cp("parallel", "arbitrary"))(a, w, res, g, b)


def _ln_fwd(x, g, b, *, name):
    T, D = x.shape
    tt = _tile(T, 512)

    def body(x_ref, g_ref, b_ref, o_ref, ob_ref):
        xhat, _ = _ln_stats(x_ref[...])
        h = xhat * g_ref[...] + b_ref[...]
        o_ref[...] = h
        ob_ref[...] = h.astype(BF16)

    row = pl.BlockSpec((tt, D), lambda i: (i, 0))
    vec = pl.BlockSpec((1, D), lambda i: (0, 0))
    return pl.pallas_call(body, name=name, out_shape=(_sds((T, D), F32), _sds((T, D), BF16)), grid=(T // tt,),
                          in_specs=[row, vec, vec], out_specs=(row, row), compiler_params=_cp("parallel"))(x, g, b)


def _ln_bwd(s, dy, g, b, *, silu, name):
    T, D = s.shape
    tt = _tile(T, 512)

    def body(s_ref, dy_ref, g_ref, b_ref, ds_ref, dg_ref, db_ref):
        i = pl.program_id(0)
        xhat, rstd = _ln_stats(s_ref[...])
        gg = g_ref[...]
        dyl = _f32(dy_ref[...])
        if silu:
            _, dsilu = _silu_and_grad(xhat * gg + b_ref[...])
            dyl = dyl * dsilu
        dxh = dyl * gg
        m1 = jnp.mean(dxh, axis=-1, keepdims=True)
        m2 = jnp.mean(dxh * xhat, axis=-1, keepdims=True)
        ds_ref[...] = rstd * (dxh - m1 - xhat * m2)

        @pl.when(i == 0)
        def _():
            dg_ref[...] = jnp.zeros_like(dg_ref)
            db_ref[...] = jnp.zeros_like(db_ref)

        dg_ref[...] += jnp.sum(dyl * xhat, axis=0, keepdims=True)
        db_ref[...] += jnp.sum(dyl, axis=0, keepdims=True)

    row = pl.BlockSpec((tt, D), lambda i: (i, 0))
    vec = pl.BlockSpec((1, D), lambda i: (0, 0))
    return pl.pallas_call(body, name=name, out_shape=(_sds((T, D), F32), _sds((1, D), F32), _sds((1, D), F32)),
                          grid=(T // tt,), in_specs=[row, row, vec, vec], out_specs=(row, vec, vec),
                          compiler_params=_cp("arbitrary"))(s, dy, g, b)


def _loss_fwd_bwd(h, tgt, *, name):
    T, D = h.shape
    tt = _tile(T, 512)

    def body(h_ref, t_ref, dh_ref, l_ref):
        i = pl.program_id(0)
        e = h_ref[...] - t_ref[...]
        dh_ref[...] = e * (1.0 / D)

        @pl.when(i == 0)
        def _():
            l_ref[...] = jnp.zeros_like(l_ref)

        part = jnp.sum(jnp.sum(e * e, axis=1, keepdims=True), axis=0, keepdims=True) * (0.5 / D)
        l_ref[...] += jnp.broadcast_to(part, l_ref.shape)

    row = pl.BlockSpec((tt, D), lambda i: (i, 0))
    one = pl.BlockSpec((8, LANES), lambda i: (0, 0))
    dh, l = pl.pallas_call(body, name=name, out_shape=(_sds((T, D), F32), _sds((8, LANES), F32)), grid=(T // tt,),
                           in_specs=[row, row], out_specs=(row, one), compiler_params=_cp("arbitrary"))(h, tgt)
    return l[0:1, 0:1], dh


def _halo_rows(k):
    return 32 if k > 17 else 16


def _shifted(ext, K, first):
    rolled = {0: ext}
    out = []
    for k in range(K):
        r = (first + k) % 8
        if r not in rolled:
            rolled[r] = pltpu.roll(ext, ext.shape[0] - r, 0)
        out.append((rolled[r], first + k - r))
    return out


def _taps(ext, w_ref, sl, K, first, n, reverse=False):
    acc = None
    for k, (z, base) in enumerate(_shifted(ext, K, first)):
        kw = K - 1 - k if reverse else k
        term = w_ref[kw:kw + 1, sl] * z[base:base + n]
        acc = term if acc is None else acc + term
    return acc


def _tap_sums(d, ext, dw_ref, sl, K, first):
    n = d.shape[0]
    for k, (z, base) in enumerate(_shifted(ext, K, first)):
        dw_ref[k:k + 1, sl] += jnp.sum(d * z[base:base + n], axis=0, keepdims=True)


def _strip_width(width):
    return LANES if width % LANES == 0 else width


def _for_strips(width, fn):
    sw = _strip_width(width)

    def step(s, carry):
        fn(pl.ds(pl.multiple_of(s * sw, sw), sw), s)
        return carry

    lax.fori_loop(0, width // sw, step, 0)


def _prev_rows(tt, hb):
    return lambda t: jnp.maximum(t * (tt // hb) - 1, 0)


def _next_rows(tt, hb, T):
    return lambda t: jnp.minimum((t + 1) * (tt // hb), T // hb - 1)


def _conf_fwd(u, w, b, g, beta, *, D, name):
    T = u.shape[0]
    K = w.shape[0]
    tt, hb, tc = _tile(T, CONV_ROWS), _halo_rows(K), _tile(D, GLU_TILE)
    sw = _strip_width(tc)
    per_tile = tc // sw
    prev = _prev_rows(tt, hb)

    def body(u_ref, uh_ref, w_ref, b_ref, g_ref, beta_ref, v1_ref, v3_ref):
        i = pl.program_id(0)

        def strip(sl, s):
            a0 = (s // per_tile) * (2 * tc) + (s % per_tile) * sw
            a_sl, g_sl = pl.ds(pl.multiple_of(a0, sw), sw), pl.ds(pl.multiple_of(a0 + tc, sw), sw)
            halo = jnp.where(i > 0, _f32(uh_ref[:, a_sl]) * _sigmoid(_f32(uh_ref[:, g_sl])), 0.0)
            ext = jnp.concatenate([halo, _f32(u_ref[:, a_sl]) * _sigmoid(_f32(u_ref[:, g_sl]))], axis=0)
            v1_ref[:, sl] = _taps(ext, w_ref, sl, K, hb - (K - 1), tt) + b_ref[:, sl]

        _for_strips(D, strip)
        xhat, _ = _ln_stats(v1_ref[...])
        v2 = xhat * g_ref[...] + beta_ref[...]
        v3_ref[...] = (v2 * _sigmoid(v2)).astype(BF16)

    row = pl.BlockSpec((tt, D), lambda t: (t, 0))
    vec = pl.BlockSpec((1, D), lambda t: (0, 0))
    return pl.pallas_call(
        body, name=name, out_shape=(_sds((T, D), F32), _sds((T, D), BF16)), grid=(T // tt,),
        in_specs=[pl.BlockSpec((tt, 2 * D), lambda t: (t, 0)), pl.BlockSpec((hb, 2 * D), lambda t: (prev(t), 0)),
                  pl.BlockSpec((K, D), lambda t: (0, 0)), vec, vec, vec],
        out_specs=(row, row), compiler_params=_cp("parallel"))(u, u, w, b, g, beta)


def _conv_act_fwd(x, off, w, b, *, ffn, name):
    T = x.shape[0]
    K, Cw = w.shape
    tt, hb = _tile(T, CONV_ROWS), _halo_rows(K)
    tc = _tile(Cw // 2, FFN_TILE) if ffn else _pick(math.gcd(Cw, off), 1536, LANES)
    xw = 2 * tc if ffn else tc
    assert off % xw == 0
    ob = off // xw
    prev = _prev_rows(tt, hb)

    def body(x_ref, xh_ref, w_ref, b_ref, o_ref):
        first = pl.program_id(0) == 0

        def pre_of(sl):
            ext = jnp.concatenate([jnp.where(first, 0.0, _f32(xh_ref[:, sl])), _f32(x_ref[:, sl])], axis=0)
            return _taps(ext, w_ref, sl, K, hb - (K - 1), tt) + b_ref[:, sl]

        def strip(sl, s):
            if ffn:
                gate = pre_of(sl)
                val = pre_of(pl.ds(pl.multiple_of(tc + s * sw, sw), sw))
                o_ref[:, sl] = (gate * _sigmoid(gate) * val).astype(BF16)
            else:
                pre = pre_of(sl)
                o_ref[:, sl] = pre * _sigmoid(pre)

        _for_strips(tc, strip)

    sw = _strip_width(tc)
    return pl.pallas_call(
        body, name=name, out_shape=_sds((T, Cw // 2), BF16) if ffn else _sds((T, Cw), F32), grid=(T // tt, Cw // xw),
        in_specs=[pl.BlockSpec((tt, xw), lambda t, c: (t, c + ob)), pl.BlockSpec((hb, xw), lambda t, c: (prev(t), c + ob)),
                  pl.BlockSpec((K, xw), lambda t, c: (0, c)), pl.BlockSpec((1, xw), lambda t, c: (0, c))],
        out_specs=pl.BlockSpec((tt, tc), lambda t, c: (t, c)),
        compiler_params=_cp("parallel", "parallel"))(x, x, w, b)


def _conv_act_bwd(x, off, w, b, dout, *, ffn, name):
    T = x.shape[0]
    K, Cw = w.shape
    tt, hb = _tile(T, CONV_ROWS), _halo_rows(K)
    tc = _tile(Cw // 2, FFN_TILE) if ffn else _pick(math.gcd(Cw, off), 1536, LANES)
    xw = 2 * tc if ffn else tc
    assert off % xw == 0
    ob = off // xw
    nt = T // tt
    prev, nxt = _prev_rows(tt, hb), _next_rows(tt, hb, T)

    sw = _strip_width(tc)

    def body(x_ref, xp_ref, xn_ref, d_ref, dn_ref, w_ref, b_ref, dx_ref, dw_ref, db_ref):
        t = pl.program_id(1)

        @pl.when(t == 0)
        def _():
            dw_ref[...] = jnp.zeros_like(dw_ref)
            db_ref[...] = jnp.zeros_like(db_ref)

        def ext_of(sl):
            return jnp.concatenate([jnp.where(t == 0, 0.0, _f32(xp_ref[:, sl])), _f32(x_ref[:, sl]), _f32(xn_ref[:, sl])], axis=0)

        def pre_of(ext, sl):
            return _taps(ext, w_ref, sl, K, hb - (K - 1), tt + hb) + b_ref[:, sl]

        def finish(ext, dpre, sl):
            dx_ref[:, sl] = _taps(dpre, w_ref, sl, K, 0, tt, reverse=True).astype(BF16)
            dp = dpre[0:tt]
            _tap_sums(dp, ext, dw_ref, sl, K, hb - (K - 1))
            db_ref[:, sl] += jnp.sum(dp, axis=0, keepdims=True)

        def strip(sl, s):
            d = jnp.concatenate([d_ref[:, sl], jnp.where(t == nt - 1, 0.0, dn_ref[:, sl])], axis=0)
            if ffn:
                vsl = pl.ds(pl.multiple_of(tc + s * sw, sw), sw)
                eg, ev = ext_of(sl), ext_of(vsl)
                sg, dsg = _silu_and_grad(pre_of(eg, sl))
                val = pre_of(ev, vsl)
                finish(eg, d * val * dsg, sl)
                finish(ev, d * sg, vsl)
            else:
                ext = ext_of(sl)
                finish(ext, d * _silu_and_grad(pre_of(ext, sl))[1], sl)

        _for_strips(tc, strip)

    return pl.pallas_call(
        body, name=name, out_shape=(_sds((T, Cw), BF16), _sds((K, Cw), F32), _sds((1, Cw), F32)),
        grid=(Cw // xw, nt),
        in_specs=[pl.BlockSpec((tt, xw), lambda c, t: (t, c + ob)), pl.BlockSpec((hb, xw), lambda c, t: (prev(t), c + ob)),
                  pl.BlockSpec((hb, xw), lambda c, t: (nxt(t), c + ob)),
                  pl.BlockSpec((tt, tc), lambda c, t: (t, c)), pl.BlockSpec((hb, tc), lambda c, t: (nxt(t), c)),
                  pl.BlockSpec((K, xw), lambda c, t: (0, c)), pl.BlockSpec((1, xw), lambda c, t: (0, c))],
        out_specs=(pl.BlockSpec((tt, xw), lambda c, t: (t, c)), pl.BlockSpec((K, xw), lambda c, t: (0, c)),
                   pl.BlockSpec((1, xw), lambda c, t: (0, c))),
        compiler_params=_cp("parallel", "arbitrary"))(x, x, x, dout, dout, w, b)


def _glu_conv_bwd(dpre, u, w, *, name):
    T, C = dpre.shape
    K = w.shape[0]
    tt, hb, tc = _tile(T, CONV_ROWS), _halo_rows(K), _tile(C, GLU_TILE)
    nt = T // tt
    prev, nxt = _prev_rows(tt, hb), _next_rows(tt, hb, T)

    sw = _strip_width(tc)

    def body(d_ref, dn_ref, x_ref, xh_ref, w_ref, dx_ref, dw_ref, db_ref):
        t = pl.program_id(1)

        @pl.when(t == 0)
        def _():
            dw_ref[...] = jnp.zeros_like(dw_ref)
            db_ref[...] = jnp.zeros_like(db_ref)

        def strip(sl, s):
            gsl = pl.ds(pl.multiple_of(tc + s * sw, sw), sw)
            d = d_ref[:, sl]
            dext = jnp.concatenate([d, jnp.where(t == nt - 1, 0.0, dn_ref[:, sl])], axis=0)
            dxin = _taps(dext, w_ref, sl, K, 0, tt, reverse=True)
            a, sg = _f32(x_ref[:, sl]), _sigmoid(_f32(x_ref[:, gsl]))
            v0 = a * sg
            dx_ref[:, sl] = (dxin * sg).astype(BF16)
            dx_ref[:, gsl] = (dxin * v0 * (1.0 - sg)).astype(BF16)
            halo = jnp.where(t == 0, 0.0, _f32(xh_ref[:, sl]) * _sigmoid(_f32(xh_ref[:, gsl])))
            _tap_sums(d, jnp.concatenate([halo, v0], axis=0), dw_ref, sl, K, hb - (K - 1))
            db_ref[:, sl] += jnp.sum(d, axis=0, keepdims=True)

        _for_strips(tc, strip)

    return pl.pallas_call(
        body, name=name, out_shape=(_sds((T, 2 * C), BF16), _sds((K, C), F32), _sds((1, C), F32)), grid=(C // tc, nt),
        in_specs=[pl.BlockSpec((tt, tc), lambda c, t: (t, c)), pl.BlockSpec((hb, tc), lambda c, t: (nxt(t), c)),
                  pl.BlockSpec((tt, 2 * tc), lambda c, t: (t, c)), pl.BlockSpec((hb, 2 * tc), lambda c, t: (prev(t), c)),
                  pl.BlockSpec((K, tc), lambda c, t: (0, c))],
        out_specs=(pl.BlockSpec((tt, 2 * tc), lambda c, t: (t, c)), pl.BlockSpec((K, tc), lambda c, t: (0, c)),
                   pl.BlockSpec((1, tc), lambda c, t: (0, c))),
        compiler_params=_cp("parallel", "arbitrary"))(dpre, dpre, u, u, w)


def _softplus(x):
    t = jnp.exp(-jnp.abs(x))
    u = 1.0 + t
    log1p = jnp.where(u == 1.0, t, jnp.log(u) * t / jnp.where(u == 1.0, 1.0, u - 1.0))
    return jnp.maximum(x, 0.0) + log1p


def _dt_fwd(udt, bias, *, H, name):
    T = udt.shape[0]
    tt = _tile(T, 1024)

    def body(u_ref, b_ref, o_ref):
        lane = lax.broadcasted_iota(jnp.int32, (tt, LANES), 1)
        o_ref[...] = jnp.where(lane < H, _softplus(u_ref[...] + b_ref[...]), 0.0)

    row = pl.BlockSpec((tt, LANES), lambda i: (i, 0))
    return pl.pallas_call(body, name=name, out_shape=_sds((T, LANES), F32), grid=(T // tt,),
                          in_specs=[row, pl.BlockSpec((1, LANES), lambda i: (0, 0))], out_specs=row,
                          compiler_params=_cp("parallel"))(udt, bias)


def _dt_bwd(da, ddtx, dt, udt, bias, a_col, *, H, name):
    T = da.shape[0]
    tt = _tile(T, 1024)

    def body(da_ref, dx_ref, dt_ref, u_ref, b_ref, a_ref, draw_ref, dbias_ref, dalog_ref):
        i = pl.program_id(0)
        lane = lax.broadcasted_iota(jnp.int32, (tt, LANES), 1)
        dav = da_ref[...]
        ddt = dav * a_ref[...] + dx_ref[...]
        draw = jnp.where(lane < H, ddt * _sigmoid(u_ref[...] + b_ref[...]), 0.0)
        draw_ref[...] = draw.astype(BF16)

        @pl.when(i == 0)
        def _():
            dbias_ref[...] = jnp.zeros_like(dbias_ref)
            dalog_ref[...] = jnp.zeros_like(dalog_ref)

        dbias_ref[...] += jnp.sum(draw, axis=0, keepdims=True)
        dalog_ref[...] += jnp.sum(dav * dt_ref[...], axis=0, keepdims=True) * a_ref[...]

    row = pl.BlockSpec((tt, LANES), lambda i: (i, 0))
    vec = pl.BlockSpec((1, LANES), lambda i: (0, 0))
    return pl.pallas_call(body, name=name,
                          out_shape=(_sds((T, LANES), BF16), _sds((1, LANES), F32), _sds((1, LANES), F32)),
                          grid=(T // tt,), in_specs=[row, row, row, row, vec, vec], out_specs=(row, vec, vec),
                          compiler_params=_cp("arbitrary"))(da, ddtx, dt, udt, bias, a_col)


def _ssd_consts():
    L = SSM_CHUNK
    r = lax.broadcasted_iota(jnp.int32, (L, L), 0)
    c = lax.broadcasted_iota(jnp.int32, (L, L), 1)
    return r, c


def _ssd_chunk_decays(dtc_ref, dtr_ref, acol_ref, arow_ref, cs_ref, csr_ref, r, c):
    L = SSM_CHUNK
    dtc = dtc_ref[...]
    tril = (r >= c).astype(F32)
    triu = (r <= c).astype(F32)
    cs_ref[...] = jnp.dot(tril, dtc * acol_ref[...], precision=HIGHEST, preferred_element_type=F32)
    csr_ref[...] = jnp.dot(dtr_ref[...] * arow_ref[...], triu, precision=HIGHEST, preferred_element_type=F32)
    cs = cs_ref[...]
    cs_last = cs_ref[L - 1:L, :]
    return dtc, cs, jnp.exp(cs), jnp.exp(cs_last - cs), jnp.exp(cs_last), triu


def _head_spread(arrs, g, heads_per_group):
    P = SSM_HEAD_DIM
    gw = heads_per_group * P
    head = lax.broadcasted_iota(jnp.int32, (LANES, gw), 0)
    lane = lax.broadcasted_iota(jnp.int32, (LANES, gw), 1)
    spread = (head == g * heads_per_group + lane // P).astype(BF16)
    out = _dot(jnp.concatenate(arrs, axis=0).astype(BF16), spread, _NN)
    L = arrs[0].shape[0]
    return [out[i * L:(i + 1) * L] for i in range(len(arrs))]


def _ssd_fwd(xbc, dt, dtT, a_col, a_row, d_col, *, H, name):
    T, XBC = xbc.shape
    L, N, G, P = SSM_CHUNK, SSM_STATE, SSM_GROUPS, SSM_HEAD_DIM
    SX = H * P
    HR = dtT.shape[0]
    nc = T // L
    heads_per_group = H // G

    def body(x_ref, dtc_ref, dtr_ref, acol_ref, arow_ref, d_ref, y_ref, hst_ref, state, cs_ref, csr_ref):
        ci = pl.program_id(0)

        @pl.when(ci == 0)
        def _():
            state[...] = jnp.zeros_like(state)

        hst_ref[0] = state[...]
        r, c = _ssd_consts()
        tri = r >= c
        lane_lo = c < P
        row_lo = r < P
        lane1_lo = lax.broadcasted_iota(jnp.int32, (1, LANES), 1) < P
        dtc, cs, e, ds, cd, _ = _ssd_chunk_decays(dtc_ref, dtr_ref, acol_ref, arow_ref, cs_ref, csr_ref, r, c)
        dsk = d_ref[...]

        for g in range(G):
            Bg = x_ref[:, SX + g * N:SX + (g + 1) * N].astype(BF16)
            Cg = x_ref[:, SX + G * N + g * N:SX + G * N + (g + 1) * N].astype(BF16)
            Gm = _dot(Cg, Bg, _NT)
            dt_g, e_g, ds_g = _head_spread([dtc, e, ds], g, heads_per_group)
            st = []
            for j in range(g * heads_per_group // 2, (g + 1) * heads_per_group // 2):
                h0 = 2 * j
                sl = slice(2 * P * j, 2 * P * (j + 1))
                gl = slice(sl.start - g * heads_per_group * P, sl.stop - g * heads_per_group * P)
                x2 = x_ref[:, sl]
                X2 = x2 * dt_g[:, gl]
                H2 = state[sl, :]
                st.append(dict(h0=h0, sl=sl, x2=x2, X2=X2, H2=H2, e2=e_g[:, gl], R2=_dot(Cg, H2.astype(BF16), _NT),
                               S2=_dot((X2 * ds_g[:, gl]).astype(BF16), Bg, _TN)))
            for s in st:
                ms = []
                for h in (s['h0'], s['h0'] + 1):
                    seg = cs[:, h:h + 1] - csr_ref[h:h + 1, :]
                    ms.append((Gm * jnp.where(tri, jnp.exp(jnp.where(tri, seg, 0.0)), 0.0)).astype(BF16))
                xst = jnp.concatenate([jnp.where(lane_lo, s['X2'], 0.0), jnp.where(lane_lo, 0.0, s['X2'])], axis=0).astype(BF16)
                s['yd'] = _dot(jnp.concatenate(ms, axis=1), xst, _NN)
            for s in st:
                h0, sl = s['h0'], s['sl']
                dsk2 = jnp.where(lane1_lo, dsk[:, h0:h0 + 1], dsk[:, h0 + 1:h0 + 2])
                y_ref[:, sl] = (s['yd'] + s['e2'] * s['R2'] + s['x2'] * dsk2).astype(BF16)
                state[sl, :] = jnp.where(row_lo, cd[:, h0:h0 + 1], cd[:, h0 + 1:h0 + 2]) * s['H2'] + s['S2']

    vec = pl.BlockSpec((1, LANES), lambda i: (0, 0))
    return pl.pallas_call(
        body, name=name, out_shape=(_sds((T, SX), BF16), _sds((nc, SX, N), F32)), grid=(nc,),
        in_specs=[pl.BlockSpec((L, XBC), lambda i: (i, 0)), pl.BlockSpec((L, LANES), lambda i: (i, 0)),
                  pl.BlockSpec((HR, L), lambda i: (0, i)), vec, pl.BlockSpec((HR, 1), lambda i: (0, 0)), vec],
        out_specs=(pl.BlockSpec((L, SX), lambda i: (i, 0)), pl.BlockSpec((1, SX, N), lambda i: (i, 0, 0))),
        scratch_shapes=[pltpu.VMEM((SX, N), F32), pltpu.VMEM((L, LANES), F32), pltpu.VMEM((HR, L), F32)],
        compiler_params=_cp("arbitrary"))(xbc, dt, dtT, a_col, a_row, d_col)


def _ssd_bwd(xbc, dt, dtT, a_col, a_row, d_col, hst, dy, *, H, name):
    T, XBC = xbc.shape
    L, N, G, P = SSM_CHUNK, SSM_STATE, SSM_GROUPS, SSM_HEAD_DIM
    SX = H * P
    HR = dtT.shape[0]
    nc = T // L
    heads_per_group = H // G

    def body(x_ref, dtc_ref, dtr_ref, acol_ref, arow_ref, d_ref, hst_ref, dy_ref,
             dx_ref, da_ref, ddtx_ref, dd_ref, dstate, cs_ref, csr_ref):
        ci = pl.program_id(0)

        @pl.when(ci == 0)
        def _():
            dstate[...] = jnp.zeros_like(dstate)
            dd_ref[...] = jnp.zeros_like(dd_ref)

        r, c = _ssd_consts()
        tri = r >= c
        lane_lo = c < P
        row_lo = r < P
        lane1 = lax.broadcasted_iota(jnp.int32, (1, LANES), 1)
        rowc = lax.broadcasted_iota(jnp.int32, (L, 1), 0)
        dtc, cs, e, ds, cd, triu = _ssd_chunk_decays(dtc_ref, dtr_ref, acol_ref, arow_ref, cs_ref, csr_ref, r, c)
        triu_b = triu.astype(BF16)
        dsk = d_ref[...]
        last_row = rowc == L - 1

        triT = r <= c

        def halves(v, axis):
            return jnp.concatenate([jnp.where(lane_lo, v, 0.0), jnp.where(lane_lo, 0.0, v)], axis=axis)

        def head_sum(v, h):
            lo = jnp.sum(jnp.where(lane1 < P, v, 0.0), axis=1, keepdims=True) * (lane1 == h).astype(F32)
            hi = jnp.sum(jnp.where(lane1 < P, 0.0, v), axis=1, keepdims=True) * (lane1 == h + 1).astype(F32)
            return lo + hi

        GW = heads_per_group * P
        gl = lax.broadcasted_iota(jnp.int32, (GW, LANES), 0)
        gc = lax.broadcasted_iota(jnp.int32, (GW, LANES), 1)
        wl = lax.broadcasted_iota(jnp.int32, (heads_per_group * L, LANES), 0)
        wc = lax.broadcasted_iota(jnp.int32, (heads_per_group * L, LANES), 1)
        wide_r = lax.broadcasted_iota(jnp.int32, (L, heads_per_group * L), 0)
        wide_c = lax.broadcasted_iota(jnp.int32, (L, heads_per_group * L), 1)
        below_diag = (wide_c % L) < wide_r
        sums = jnp.zeros((3 * L, LANES), F32)
        da_q = jnp.zeros((L, LANES), F32)
        dcd_acc = jnp.zeros((1, LANES), F32)
        dd_acc = jnp.zeros((1, LANES), F32)
        for g in range(G):
            bsl = slice(SX + g * N, SX + (g + 1) * N)
            csl = slice(SX + G * N + g * N, SX + G * N + (g + 1) * N)
            Bg = x_ref[:, bsl].astype(BF16)
            Cg = x_ref[:, csl].astype(BF16)
            Gm = _dot(Cg, Bg, _NT)
            GmT = _dot(Bg, Cg, _NT)
            dt_g, e_g, ds_g = _head_spread([dtc, e, ds], g, heads_per_group)
            st = []
            for j in range(g * heads_per_group // 2, (g + 1) * heads_per_group // 2):
                h0 = 2 * j
                sl = slice(2 * P * j, 2 * P * (j + 1))
                s = dict(h0=h0, sl=sl, x2=x_ref[:, sl], dy2=_f32(dy_ref[:, sl]), H2=hst_ref[0, sl, :], dHn=dstate[sl, :])
                gsl = slice(sl.start - g * heads_per_group * P, sl.stop - g * heads_per_group * P)
                s['dt2'], s['e2'], s['ds2'] = dt_g[:, gsl], e_g[:, gsl], ds_g[:, gsl]
                s['X2'] = s['x2'] * s['dt2']
                s['H2b'], s['dHnb'] = s['H2'].astype(BF16), s['dHn'].astype(BF16)
                st.append(s)
            for s in st:
                s['R2'] = _dot(Cg, s['H2b'], _NT)
                s['dXd'] = _dot(Bg, s['dHnb'], _NT)
                s['dM2'] = _dot(s['dy2'].astype(BF16), halves(s['X2'], 0).astype(BF16), _NT)
            dG = jnp.zeros((L, L), F32)
            qs = []
            for s in st:
                mts = []
                for i, h in enumerate((s['h0'], s['h0'] + 1)):
                    z = cs[:, h:h + 1] - csr_ref[h:h + 1, :]
                    Dm = jnp.where(tri, jnp.exp(jnp.where(tri, z, 0.0)), 0.0)
                    DmT = jnp.where(triT, jnp.exp(jnp.where(triT, -z, 0.0)), 0.0)
                    dM = s['dM2'][:, i * L:(i + 1) * L]
                    dG = dG + dM * Dm
                    qs.append((dM * (Gm * Dm)).astype(BF16))
                    mts.append((GmT * DmT).astype(BF16))
                s['dXm'] = _dot(jnp.concatenate(mts, axis=1), halves(s['dy2'], 0).astype(BF16), _NN)
            Wg = _dot(triu_b, jnp.concatenate(qs, axis=1), _NN)
            place = (wc == g * heads_per_group + wl // L).astype(BF16)
            da_q = da_q + _dot(jnp.where(below_diag, Wg, 0.0).astype(BF16), place, _NN)
            dBg = jnp.zeros((L, N), F32)
            dCg = jnp.zeros((L, N), F32)
            des, ddss, dxxs = [], [], []
            for s in st:
                h0, sl, x2, dy2, X2 = s['h0'], s['sl'], s['x2'], s['dy2'], s['X2']
                dR2b = (s['e2'] * dy2).astype(BF16)
                dCg = dCg + _dot(dR2b, s['H2b'], _NN)
                dHr = _dot(dR2b, Cg, _TN)
                dBg = dBg + _dot((X2 * s['ds2']).astype(BF16), s['dHnb'], _NN)
                dX2 = s['ds2'] * s['dXd'] + s['dXm']
                des.append(dy2 * s['R2'])
                ddss.append(s['dXd'] * X2)
                dxxs.append(dX2 * x2)
                prod = s['dHn'] * s['H2']
                for i, h in enumerate((h0, h0 + 1)):
                    rows = jnp.sum(prod[i * P:(i + 1) * P], axis=0, keepdims=True)
                    dcd_acc = dcd_acc + jnp.sum(rows, axis=1, keepdims=True) * (lane1 == h).astype(F32)
                dd_acc = dd_acc + head_sum(jnp.sum(dy2 * x2, axis=0, keepdims=True), h0)
                dsk2 = jnp.where(lane1 < P, dsk[:, h0:h0 + 1], dsk[:, h0 + 1:h0 + 2])
                dx_ref[:, sl] = dX2 * s['dt2'] + dy2 * dsk2
                dstate[sl, :] = jnp.where(row_lo, cd[:, h0:h0 + 1], cd[:, h0 + 1:h0 + 2]) * s['dHn'] + dHr
            stack = jnp.concatenate([jnp.concatenate(v, axis=1) for v in (des, ddss, dxxs)], axis=0).astype(BF16)
            sums = sums + _dot(stack, (gc == g * heads_per_group + gl // P).astype(BF16), _NN)
            dGb = dG.astype(BF16)
            dx_ref[:, bsl] = dBg + _dot(dGb, Cg, _TN)
            dx_ref[:, csl] = dCg + _dot(dGb, Bg, _NN)
        t1 = sums[L:2 * L] * ds
        tail = jnp.sum(t1, axis=0, keepdims=True) + dcd_acc * cd
        dcs = sums[0:L] * e - t1 + jnp.where(last_row, tail, 0.0)
        da_ref[...] = jnp.dot(triu, dcs, precision=HIGHEST, preferred_element_type=F32) + da_q
        ddtx_ref[...] = sums[2 * L:3 * L]
        dd_ref[...] += dd_acc

    vec = pl.BlockSpec((1, LANES), lambda i: (0, 0))
    rev = lambda i: (nc - 1 - i, 0)
    return pl.pallas_call(
        body, name=name,
        out_shape=(_sds((T, XBC), F32), _sds((T, LANES), F32), _sds((T, LANES), F32), _sds((1, LANES), F32)),
        grid=(nc,),
        in_specs=[pl.BlockSpec((L, XBC), rev), pl.BlockSpec((L, LANES), rev),
                  pl.BlockSpec((HR, L), lambda i: (0, nc - 1 - i)), vec, pl.BlockSpec((HR, 1), lambda i: (0, 0)), vec,
                  pl.BlockSpec((1, SX, N), lambda i: (nc - 1 - i, 0, 0)), pl.BlockSpec((L, SX), rev)],
        out_specs=(pl.BlockSpec((L, XBC), rev), pl.BlockSpec((L, LANES), rev), pl.BlockSpec((L, LANES), rev), vec),
        scratch_shapes=[pltpu.VMEM((SX, N), F32), pltpu.VMEM((L, LANES), F32), pltpu.VMEM((HR, L), F32)],
        compiler_params=_cp("arbitrary"))(xbc, dt, dtT, a_col, a_row, d_col, hst, dy)


def _rms_fwd(y, u, zoff, w, *, name):
    T, SX = y.shape
    tt = _tile(T, 256)
    gs = SX // SSM_GROUPS
    zb = zoff // SX

    def body(y_ref, z_ref, w_ref, o_ref):
        for g in range(SSM_GROUPS):
            sl = slice(g * gs, (g + 1) * gs)
            z = _f32(z_ref[:, sl])
            yg = _f32(y_ref[:, sl]) * (z * _sigmoid(z))
            rstd = lax.rsqrt(jnp.mean(yg * yg, axis=-1, keepdims=True) + RMS_EPS)
            o_ref[:, sl] = (yg * rstd * w_ref[:, sl]).astype(BF16)

    row = pl.BlockSpec((tt, SX), lambda i: (i, 0))
    return pl.pallas_call(body, name=name, out_shape=_sds((T, SX), BF16), grid=(T // tt,),
                          in_specs=[row, pl.BlockSpec((tt, SX), lambda i: (i, zb)), pl.BlockSpec((1, SX), lambda i: (0, 0))],
                          out_specs=row, compiler_params=_cp("parallel"))(y, u, w)


def _rms_bwd(y, u, zoff, w, dyn, *, name):
    T, SX = y.shape
    tt = _tile(T, 256)
    gs = SX // SSM_GROUPS
    zb = zoff // SX

    def body(y_ref, z_ref, w_ref, d_ref, dy_ref, dz_ref, dw_ref):
        i = pl.program_id(0)

        @pl.when(i == 0)
        def _():
            dw_ref[...] = jnp.zeros_like(dw_ref)

        for g in range(SSM_GROUPS):
            sl = slice(g * gs, (g + 1) * gs)
            z, yv, d = _f32(z_ref[:, sl]), _f32(y_ref[:, sl]), _f32(d_ref[:, sl])
            sz, dsz = _silu_and_grad(z)
            yg = yv * sz
            rstd = lax.rsqrt(jnp.mean(yg * yg, axis=-1, keepdims=True) + RMS_EPS)
            t = yg * rstd
            dw_ref[:, sl] += jnp.sum(d * t, axis=0, keepdims=True)
            dt_ = d * w_ref[:, sl]
            dyg = rstd * (dt_ - t * jnp.mean(dt_ * t, axis=-1, keepdims=True))
            dy_ref[:, sl] = (dyg * sz).astype(BF16)
            dz_ref[:, sl] = (dyg * yv * dsz).astype(BF16)

    row = pl.BlockSpec((tt, SX), lambda i: (i, 0))
    vec = pl.BlockSpec((1, SX), lambda i: (0, 0))
    return pl.pallas_call(body, name=name, out_shape=(_sds((T, SX), BF16), _sds((T, SX), BF16), _sds((1, SX), F32)),
                          grid=(T // tt,), in_specs=[row, pl.BlockSpec((tt, SX), lambda i: (i, zb)), vec, row],
                          out_specs=(row, row, vec), compiler_params=_cp("arbitrary"))(y, u, w, dyn)


def _gate_fwd(u, goff, ya, yb, *, name):
    T, D = ya.shape
    tt = _tile(T, 512)
    gb = goff // D

    def body(ga_ref, gb_ref, ya_ref, yb_ref, o_ref):
        o_ref[...] = (_sigmoid(_f32(ga_ref[...])) * _f32(ya_ref[...]) + _sigmoid(_f32(gb_ref[...])) * _f32(yb_ref[...])).astype(BF16)

    row = pl.BlockSpec((tt, D), lambda i: (i, 0))
    return pl.pallas_call(body, name=name, out_shape=_sds((T, D), BF16), grid=(T // tt,),
                          in_specs=[pl.BlockSpec((tt, D), lambda i: (i, gb)), pl.BlockSpec((tt, D), lambda i: (i, gb + 1)), row, row],
                          out_specs=row, compiler_params=_cp("parallel"))(u, u, ya, yb)


def _gate_bwd(u, goff, ya, yb, dm, *, name):
    T, D = ya.shape
    tt = _tile(T, 512)
    gb = goff // D

    def body(ga_ref, gb_ref, ya_ref, yb_ref, dm_ref, dya_ref, dyb_ref, dg_ref):
        d = _f32(dm_ref[...])
        sa, sb = _sigmoid(_f32(ga_ref[...])), _sigmoid(_f32(gb_ref[...]))
        dya_ref[...] = (d * sa).astype(BF16)
        dyb_ref[...] = (d * sb).astype(BF16)
        dg_ref[...] = jnp.concatenate([d * _f32(ya_ref[...]) * sa * (1.0 - sa), d * _f32(yb_ref[...]) * sb * (1.0 - sb)], axis=1).astype(BF16)

    row = pl.BlockSpec((tt, D), lambda i: (i, 0))
    return pl.pallas_call(body, name=name, out_shape=(_sds((T, D), BF16), _sds((T, D), BF16), _sds((T, 2 * D), BF16)),
                          grid=(T // tt,),
                          in_specs=[pl.BlockSpec((tt, D), lambda i: (i, gb)), pl.BlockSpec((tt, D), lambda i: (i, gb + 1)), row, row, row],
                          out_specs=(row, row, pl.BlockSpec((tt, 2 * D), lambda i: (i, 0))),
                          compiler_params=_cp("parallel"))(u, u, ya, yb, dm)


def _adamw_math(w, gg, m, v):
    nm = ADAM_B1 * m + (1.0 - ADAM_B1) * gg
    nv = ADAM_B2 * v + (1.0 - ADAM_B2) * (gg * gg)
    m_hat = nm / (1.0 - ADAM_B1 ** ADAM_STEP)
    v_hat = nv / (1.0 - ADAM_B2 ** ADAM_STEP)
    return -ADAM_LR * (m_hat / (jnp.sqrt(v_hat) + ADAM_EPS) + ADAM_WD * w), nm, nv


def _adamw(w, g, m, v, *, name):
    R, C = w.shape
    tr = _pick(R, 256, 8)

    def body(w_ref, g_ref, m_ref, v_ref, d_ref, nm_ref, nv_ref):
        d_ref[...], nm_ref[...], nv_ref[...] = _adamw_math(w_ref[...], g_ref[...], m_ref[...], v_ref[...])

    blk = pl.BlockSpec((tr, C), lambda i: (i, 0))
    out = _sds((R, C), F32)
    return pl.pallas_call(body, name=name, out_shape=(out, out, out), grid=(R // tr,), in_specs=[blk] * 4,
                          out_specs=(blk,) * 3, compiler_params=_cp("parallel"))(w, g, m, v)


def _sum_adamw(parts, w, m, v, *, name):
    depth, C = w.shape[0], w.shape[-1]
    R = w[0].size // C
    tr = _pick(R, 256, 8)

    def body(*refs):
        p_refs = refs[:depth]
        w_ref, m_ref, v_ref, g_ref, d_ref, nm_ref, nv_ref = refs[depth:]
        for l in range(depth):
            @pl.when(pl.program_id(0) == l)
            def _(l=l):
                gg = p_refs[l][0].astype(F32)
                for k in range(1, N_DEV):
                    gg = gg + p_refs[l][k].astype(F32)
                g_ref[...] = gg
                d_ref[...], nm_ref[...], nv_ref[...] = _adamw_math(w_ref[...], gg, m_ref[...], v_ref[...])

    p_specs = [pl.BlockSpec((N_DEV, tr, C), lambda l, i, ll=ll: (0, jnp.where(l == ll, i, 0), 0)) for ll in range(depth)]
    blk = pl.BlockSpec((None, tr, C), lambda l, i: (l, i, 0))
    out = _sds((depth, R, C), F32)
    res = pl.pallas_call(body, name=name, out_shape=(out,) * 4, grid=(depth, R // tr),
                         in_specs=p_specs + [blk, blk, blk], out_specs=(blk,) * 4,
                         compiler_params=_cp("parallel", "parallel"))(
        *[x.reshape(N_DEV, R, C) for x in parts], w.reshape(depth, R, C), m.reshape(depth, R, C), v.reshape(depth, R, C))
    return tuple(r.reshape(w.shape) for r in res)


def _sum_slots(x, *, name):
    n, R, C = x.shape
    tr = _tile(R, 512)

    def body(x_ref, o_ref):
        acc = x_ref[0].astype(F32)
        for k in range(1, n):
            acc = acc + x_ref[k].astype(F32)
        o_ref[...] = acc

    return pl.pallas_call(body, name=name, out_shape=_sds((R, C), F32), grid=(R // tr,),
                          in_specs=[pl.BlockSpec((n, tr, C), lambda i: (0, i, 0))],
                          out_specs=pl.BlockSpec((tr, C), lambda i: (i, 0)), compiler_params=_cp("parallel"))(x)


def _exchange(xs, *, scatter, name):
    n = len(xs)

    def body(*refs):
        x_refs, o_refs = refs[:n], refs[n:2 * n]
        send_sems, recv_sems, local_sems = refs[2 * n:]
        mx, my, mc = lax.axis_index("x"), lax.axis_index("y"), lax.axis_index("c")
        me = 4 * mx + 2 * my + mc

        def src(i, d):
            return x_refs[i].at[d] if scatter else x_refs[i]

        locals_ = [pltpu.make_async_copy(src(i, me), o_refs[i].at[me], local_sems.at[i]) for i in range(n)]
        for cp in locals_:
            cp.start()
        sends, recvs = [], []
        for k in range(1, N_DEV):
            px = 1 - mx if k & 4 else mx
            py = 1 - my if k & 2 else my
            pc = 1 - mc if k & 1 else mc
            peer = 4 * px + 2 * py + pc
            for i in range(n):
                common = dict(send_sem=send_sems.at[k - 1, i], recv_sem=recv_sems.at[k - 1, i],
                              device_id=(px, py, pc), device_id_type=pl.DeviceIdType.MESH)
                sends.append(pltpu.make_async_remote_copy(src_ref=src(i, peer), dst_ref=o_refs[i].at[me], **common))
                recvs.append(pltpu.make_async_remote_copy(src_ref=src(i, peer), dst_ref=o_refs[i].at[peer], **common))
        for cp in sends:
            cp.start()
        for cp in recvs:
            cp.wait_recv()
        for cp in sends:
            cp.wait_send()
        for cp in locals_:
            cp.wait()

    any_spec = pl.BlockSpec(memory_space=pl.ANY)
    out_shape = tuple(_sds(x.shape if scatter else (N_DEV,) + x.shape, x.dtype) for x in xs)
    return pl.pallas_call(
        body, name=name, out_shape=out_shape, in_specs=[any_spec] * n, out_specs=(any_spec,) * n,
        scratch_shapes=[pltpu.SemaphoreType.DMA((N_DEV - 1, n)), pltpu.SemaphoreType.DMA((N_DEV - 1, n)),
                        pltpu.SemaphoreType.DMA((n,))])(*xs)


def _gather_two_level(xs, *, name):
    n = len(xs)

    def body(*refs):
        x_refs, o_refs = refs[:n], refs[n:2 * n]
        send_sems, recv_sems, local_sems = refs[2 * n:]
        x, y, c = lax.axis_index("x"), lax.axis_index("y"), lax.axis_index("c")
        me, sibling = (x, y, c), (x, y, 1 - c)
        chips = [(1 - x, y), (x, 1 - y), (1 - x, 1 - y)]

        def copy(i, k, block, to, src=None):
            rows = o_refs[i].at[4 * block[0] + 2 * block[1] + block[2]]
            return pltpu.make_async_remote_copy(
                src_ref=rows if src is None else src, dst_ref=rows, send_sem=send_sems.at[k * n + i],
                recv_sem=recv_sems.at[k * n + i], device_id=to, device_id_type=pl.DeviceIdType.MESH)

        mine = [pltpu.make_async_copy(x_refs[i], o_refs[i].at[4 * x + 2 * y + c], local_sems.at[i]) for i in range(n)]
        for cp in mine:
            cp.start()
        first = [copy(i, 0, me, sibling, src=x_refs[i]) for i in range(n)]
        first += [copy(i, 1 + j, me, (*chip, c), src=x_refs[i]) for j, chip in enumerate(chips) for i in range(n)]
        for cp in first:
            cp.start()
        passed = []
        for j, chip in enumerate(chips):
            for i in range(n):
                copy(i, 1 + j, (*chip, c), me).wait_recv()
                passed.append(copy(i, 4 + j, (*chip, c), sibling))
                passed[-1].start()
        for i in range(n):
            copy(i, 0, sibling, me).wait_recv()
        for j, chip in enumerate(chips):
            for i in range(n):
                copy(i, 4 + j, (*chip, 1 - c), me).wait_recv()
        for cp in first + passed:
            cp.wait_send()
        for cp in mine:
            cp.wait()

    any_spec = pl.BlockSpec(memory_space=pl.ANY)
    return pl.pallas_call(
        body, name=name, out_shape=tuple(_sds((N_DEV,) + x.shape, x.dtype) for x in xs), in_specs=[any_spec] * n,
        out_specs=(any_spec,) * n,
        scratch_shapes=[pltpu.SemaphoreType.DMA(((N_DEV - 1) * n,)), pltpu.SemaphoreType.DMA(((N_DEV - 1) * n,)),
                        pltpu.SemaphoreType.DMA((n,))])(*xs)


def _to_rows(flat, row_mult):
    n = flat.shape[-1]
    per = row_mult * LANES
    pad = (-n) % per
    flat = jnp.pad(flat, [(0, 0)] * (flat.ndim - 1) + [(0, pad)])
    return flat.reshape(flat.shape[:-1] + ((n + pad) // LANES, LANES))


def _peers():
    mx, my, mc = lax.axis_index("x"), lax.axis_index("y"), lax.axis_index("c")
    out = []
    for k in range(1, N_DEV):
        px = 1 - mx if k & 4 else mx
        py = 1 - my if k & 2 else my
        pc = 1 - mc if k & 1 else mc
        out.append(((px, py, pc), 4 * px + 2 * py + pc))
    return 4 * mx + 2 * my + mc, out


_HBM = pl.BlockSpec(memory_space=pltpu.HBM)
_SEM = pl.BlockSpec(memory_space=pltpu.SEMAPHORE)


def _exchange_start(xs, *, scatter, name):
    n = len(xs)

    def body(*refs):
        x_refs, land_refs, send_sems, recv_sems, token = refs[:n], refs[n:2 * n], refs[2 * n], refs[2 * n + 1], refs[-1]
        me, peers = _peers()
        for k, (dev, peer) in enumerate(peers):
            for i in range(n):
                pltpu.make_async_remote_copy(
                    src_ref=x_refs[i].at[peer] if scatter else x_refs[i], dst_ref=land_refs[i].at[me],
                    send_sem=send_sems.at[k * n + i], recv_sem=recv_sems.at[k * n + i],
                    device_id=dev, device_id_type=pl.DeviceIdType.MESH).start()
        token[...] = jnp.zeros_like(token)

    land_shapes = [x.shape if scatter else (N_DEV,) + x.shape for x in xs]
    lands = [pltpu.with_memory_space_constraint(lax.empty(s, x.dtype), pltpu.HBM) for s, x in zip(land_shapes, xs)]
    srcs = [pltpu.with_memory_space_constraint(x, pltpu.HBM) for x in xs]
    out = pl.pallas_call(
        body, name=name,
        out_shape=(pltpu.SemaphoreType.DMA(((N_DEV - 1) * n,)), pltpu.SemaphoreType.DMA(((N_DEV - 1) * n,)),
                   *[pltpu.HBM(x.shape, x.dtype) for x in xs], *[pltpu.HBM(s, x.dtype) for s, x in zip(land_shapes, xs)],
                   _sds((8, LANES), F32)),
        in_specs=(_HBM,) * (2 * n), out_specs=(_SEM, _SEM) + (_HBM,) * (2 * n) + (pl.BlockSpec(memory_space=pltpu.VMEM),),
        input_output_aliases={i: 2 + i for i in range(2 * n)},
        compiler_params=pltpu.CompilerParams(has_side_effects=pltpu.SideEffectType.DATAFLOW_SIDE_EFFECTING))(*srcs, *lands)
    return (out[0], out[1], list(out[2:2 + n]), list(out[2 + n:2 + 2 * n])), out[-1]


def _exchange_wait(handle, after, *, scatter, name):
    send_sems, recv_sems, srcs, lands = handle
    n = len(srcs)

    def body(*refs):
        x_refs, land_refs, send_sems, recv_sems = refs[:n], refs[n:2 * n], refs[2 * n], refs[2 * n + 1]
        me, peers = _peers()
        for k, (dev, peer) in enumerate(peers):
            for i in range(n):
                cp = pltpu.make_async_remote_copy(
                    src_ref=x_refs[i].at[peer] if scatter else x_refs[i], dst_ref=land_refs[i].at[peer],
                    send_sem=send_sems.at[k * n + i], recv_sem=recv_sems.at[k * n + i],
                    device_id=dev, device_id_type=pl.DeviceIdType.MESH)
                cp.wait_send()
                cp.wait_recv()

    out = pl.pallas_call(
        body, name=name, out_shape=tuple(pltpu.HBM(a.shape, a.dtype) for a in srcs + lands),
        in_specs=(_HBM,) * (2 * n) + (_SEM, _SEM, pl.BlockSpec(memory_space=pl.ANY)), out_specs=(_HBM,) * (2 * n),
        input_output_aliases={i: i for i in range(2 * n)},
        compiler_params=pltpu.CompilerParams(has_side_effects=pltpu.SideEffectType.DATAFLOW_SIDE_EFFECTING))(
        *srcs, *lands, send_sems, recv_sems, after)
    return _with_own_slot(list(out[n:]), list(out[:n]), scatter=scatter)


def _with_own_slot(lands, xs, *, scatter):
    me = 4 * lax.axis_index("x") + 2 * lax.axis_index("y") + lax.axis_index("c")
    out = []
    for land, x in zip(lands, xs):
        own = lax.dynamic_index_in_dim(x, me, 0, keepdims=True) if scatter else x[None]
        out.append(lax.dynamic_update_index_in_dim(land, own, me, 0))
    return out


def _wire_shards(p, l, names):
    return [p[k][l].astype(WIRE) if SHARDED[k][1] else p[k][l] for k in names]


def _full_weights(got, p, names):
    full = {}
    for k, g in zip(names, got):
        axis, shape = SHARDED[k][0] - 1, p[k].shape[1:]
        g = jnp.moveaxis(g, 0, axis)
        full[k] = g.reshape(shape[:axis] + (N_DEV * shape[axis],) + shape[axis + 1:])
    return full


def _grad_parts(grads, p, names):
    parts = []
    for k in names:
        axis, g = SHARDED[k][0] - 1, grads[k]
        g = g.reshape(g.shape[:axis] + (N_DEV, p[k].shape[1:][axis]) + g.shape[axis + 1:])
        parts.append(jnp.moveaxis(g, axis, 0).astype(WIRE))
    return parts


def _allreduce_small(vals):
    metas = [(k, v.shape, v.size) for k, v in vals.items()]
    flat = jnp.concatenate([v.reshape(-1) for v in vals.values()])
    got, = _exchange([_to_rows(flat, 8)], scatter=False, name="gather_small_grads")
    summed = _sum_slots(got, name="sum_small_grads").reshape(-1)
    out, off = {}, 0
    for k, shape, n in metas:
        out[k] = summed[off:off + n].reshape(shape)
        off += n
    return out


def _row(v):
    return v.reshape(1, -1).astype(F32)


def _lanes(v):
    return jnp.pad(v.astype(F32), (0, LANES - v.shape[0])).reshape(1, LANES)


EARLY = ('w_in', 'conv_dw_w', 'ssm_conv_w')
LATE = tuple(k for k in SHARDED if k not in EARLY)
FFN = ('w_ffn_up', 'ffn_dw_w', 'w_ffn_down')
NOT_FFN = tuple(k for k in SHARDED if k not in FFN)


def _layer_weights(p, full, l, dims):
    D, SX, XBC, H, FF = dims
    assert _tile(D, GLU_TILE) == D and _tile(FF, FFN_TILE) == FF
    W = {}
    if 'w_in' in full:
        w_in = full['w_in']
        o_dt = 2 * D + SX + XBC
        w_pieces = [w_in[:, :2 * D], w_in[:, 2 * D:2 * D + SX], w_in[:, 2 * D + SX:o_dt], w_in[:, o_dt + H:]]
        a_head = -jnp.exp(p['ssm_a_log'][l].astype(F32))
        hr = -(-H // 8) * 8
        W.update(
            w_main=jnp.concatenate(w_pieces, axis=1), w_pieces=w_pieces,
            w_dt=jnp.pad(w_in[:, o_dt:o_dt + H], ((0, 0), (0, LANES - H))),
            conv_w=full['conv_dw_w'], conv_b=_row(p['conv_dw_b'][l]), conv_g=_row(p['conv_ln_g'][l]),
            conv_beta=_row(p['conv_ln_b'][l]), ssm_w=full['ssm_conv_w'], ssm_b=_row(p['ssm_conv_b'][l]),
            dt_bias=_lanes(p['ssm_dt_bias'][l]), a_col=_lanes(a_head),
            a_row=jnp.pad(a_head, (0, hr - H)).reshape(hr, 1), d_col=_lanes(p['ssm_d'][l]),
            norm_w=_row(p['ssm_norm_w'][l]), ln1_g=_row(p['ln1_g'][l]), ln1_b=_row(p['ln1_b'][l]),
            ffn_b=_row(p['ffn_dw_b'][l]), ln2_g=_row(p['ln2_g'][l]), ln2_b=_row(p['ln2_b'][l]))
    if 'w_o' in full:
        W.update(w_conv_out=full['w_conv_out'], w_ssm_out=full['w_ssm_out'], w_o=full['w_o'], w_up=full['w_ffn_up'],
                 ffn_w=full['ffn_dw_w'], w_down=full['w_ffn_down'])
    return W


def _layer_fwd(h, hb, W, late, l, dims):
    D, SX, XBC, H, FF = dims
    o_z, o_xbc, o_gate = 2 * D, 2 * D + SX, 2 * D + SX + XBC
    hr = W['a_row'].shape[0]
    t = f"l{l}"
    u = _mm(hb, W['w_main'], mode='nn', out_dtype=BF16, name=t + "_in_proj")
    udt = _mm(hb, W['w_dt'], mode='nn', out_dtype=F32, name=t + "_dt_proj")
    v1, v3 = _conf_fwd(u, W['conv_w'], W['conv_b'], W['conv_g'], W['conv_beta'], D=D, name=t + "_conf_fwd")
    xbc = _conv_act_fwd(u, o_xbc, W['ssm_w'], W['ssm_b'], ffn=False, name=t + "_ssm_conv")
    dt = _dt_fwd(udt, W['dt_bias'], H=H, name=t + "_dt")
    dtT = jnp.pad(dt[:, :H].T, ((0, hr - H), (0, 0)))
    y, hst = _ssd_fwd(xbc, dt, dtT, W['a_col'], W['a_row'], W['d_col'], H=H, name=t + "_ssd_fwd")
    yn = _rms_fwd(y, u, o_z, W['norm_w'], name=t + "_rms_fwd")
    W = {**W, **late(yn)}
    ya = _mm(v3, W['w_conv_out'], mode='nn', out_dtype=BF16, name=t + "_conv_out")
    yb = _mm(yn, W['w_ssm_out'], mode='nn', out_dtype=BF16, name=t + "_ssm_out")
    m = _gate_fwd(u, o_gate, ya, yb, name=t + "_gate_fwd")
    h1, s1, h1b = _mm_res_ln(m, W['w_o'], h, W['ln1_g'], W['ln1_b'], name=t + "_mix_ln1")
    uf = _mm(h1b, W['w_up'], mode='nn', out_dtype=BF16, name=t + "_ffn_up")
    act = _conv_act_fwd(uf, 0, W['ffn_w'], W['ffn_b'], ffn=True, name=t + "_ffn_conv")
    h2, s2, h2b = _mm_res_ln(act, W['w_down'], h1, W['ln2_g'], W['ln2_b'], name=t + "_ffn_down_ln2")
    saved = dict(hb=hb, u=u, udt=udt, v1=v1, v3=v3, ya=ya, xbc=xbc, dt=dt, dtT=dtT, y=y, hst=hst, yn=yn, yb=yb, m=m,
                 h1b=h1b, s1=s1, uf=uf, act=act, s2=s2)
    return h2, h2b, saved, W


def _layer_bwd(dh2, W, S, l, dims, on_ffn_grads=None, on_all_grads=None):
    D, SX, XBC, H, FF = dims
    o_z, o_xbc, o_gate = 2 * D, 2 * D + SX, 2 * D + SX + XBC
    t = f"l{l}"
    g = {}
    ds2, g['ln2_g'], g['ln2_b'] = _ln_bwd(S['s2'], dh2, W['ln2_g'], W['ln2_b'], silu=False, name=t + "_ln2_bwd")
    g['w_down'] = _mm(S['act'], ds2, mode='tn', out_dtype=WIRE, name=t + "_dw_down")
    dact = _mm(ds2, W['w_down'], mode='nt', out_dtype=F32, name=t + "_dact")
    duf, g['ffn_w'], g['ffn_b'] = _conv_act_bwd(S['uf'], 0, W['ffn_w'], W['ffn_b'], dact, ffn=True, name=t + "_ffn_conv_bwd")
    g['w_up'] = _mm(S['h1b'], duf, mode='tn', out_dtype=WIRE, name=t + "_dw_up")
    dh1 = _mm(duf, W['w_up'], mode='nt', out_dtype=F32, res=ds2, res_scale=DN_ALPHA, name=t + "_dh1")
    ln1_g = W['ln1_g'] if on_ffn_grads is None else W['ln1_g'] + on_ffn_grads(g)[0:1, 0:1]
    ds1, g['ln1_g'], g['ln1_b'] = _ln_bwd(S['s1'], dh1, ln1_g, W['ln1_b'], silu=False, name=t + "_ln1_bwd")
    g['w_o'] = _mm(S['m'], ds1, mode='tn', out_dtype=WIRE, name=t + "_dw_o")
    dm = _mm(ds1, W['w_o'], mode='nt', out_dtype=BF16, name=t + "_dm")
    dya, dyb, dgate = _gate_bwd(S['u'], o_gate, S['ya'], S['yb'], dm, name=t + "_gate_bwd")
    g['w_conv_out'] = _mm(S['v3'], dya, mode='tn', out_dtype=WIRE, name=t + "_dw_conv_out")
    dv3 = _mm(dya, W['w_conv_out'], mode='nt', out_dtype=BF16, name=t + "_dv3")
    dv1, g['conv_g'], g['conv_beta'] = _ln_bwd(S['v1'], dv3, W['conv_g'], W['conv_beta'], silu=True, name=t + "_conv_ln_bwd")
    dglu, g['conv_w'], g['conv_b'] = _glu_conv_bwd(dv1, S['u'], W['conv_w'], name=t + "_conf_conv_bwd")
    g['w_ssm_out'] = _mm(S['yn'], dyb, mode='tn', out_dtype=WIRE, name=t + "_dw_ssm_out")
    dyn = _mm(dyb, W['w_ssm_out'], mode='nt', out_dtype=BF16, name=t + "_dyn")
    dy, dz, g['norm_w'] = _rms_bwd(S['y'], S['u'], o_z, W['norm_w'], dyn, name=t + "_rms_bwd")
    dxbc_c, da, ddtx, g['d'] = _ssd_bwd(S['xbc'], S['dt'], S['dtT'], W['a_col'], W['a_row'], W['d_col'], S['hst'], dy,
                                        H=H, name=t + "_ssd_bwd")
    ddt_raw, g['dt_bias'], g['a_log'] = _dt_bwd(da, ddtx, S['dt'], S['udt'], W['dt_bias'], W['a_col'], H=H, name=t + "_dt_bwd")
    dxbc, g['ssm_w'], g['ssm_b'] = _conv_act_bwd(S['u'], o_xbc, W['ssm_w'], W['ssm_b'], dxbc_c, ffn=False, name=t + "_ssm_conv_bwd")
    du = [dglu, dz, dxbc, dgate]
    g['w_pieces'] = [_mm(S['hb'], d, mode='tn', out_dtype=WIRE, name=f"{t}_dw_in{i}") for i, d in enumerate(du)]
    g['w_dt'] = _mm(S['hb'], ddt_raw, mode='tn', out_dtype=WIRE, name=t + "_dw_dt")
    w_dt = W['w_dt'] if on_all_grads is None else W['w_dt'] + on_all_grads(g)[0:1, 0:1].astype(W['w_dt'].dtype)
    dh = _mm_nt_cat(du + [ddt_raw], W['w_pieces'] + [w_dt], ds1, DN_ALPHA, name=t + "_dh_in")
    return dh, g


def _layer_grads_to_params(g, dims):
    D, SX, XBC, H, FF = dims
    glu, dz, dxbc, dgate = g['w_pieces']
    w_in = jnp.concatenate([glu, dz, dxbc, g['w_dt'][:, :H], dgate], axis=1)
    return dict(
        w_in=w_in, conv_dw_w=g['conv_w'], conv_dw_b=g['conv_b'][0], conv_ln_g=g['conv_g'][0], conv_ln_b=g['conv_beta'][0],
        w_conv_out=g['w_conv_out'], ssm_conv_w=g['ssm_w'], ssm_conv_b=g['ssm_b'][0],
        ssm_dt_bias=g['dt_bias'][0, :H], ssm_a_log=g['a_log'][0, :H], ssm_d=g['d'][0, :H], ssm_norm_w=g['norm_w'][0],
        w_ssm_out=g['w_ssm_out'], w_o=g['w_o'], ln1_g=g['ln1_g'][0], ln1_b=g['ln1_b'][0],
        w_ffn_up=g['w_up'], ffn_dw_w=g['ffn_w'], ffn_dw_b=g['ffn_b'][0], w_ffn_down=g['w_down'], ln2_g=g['ln2_g'][0], ln2_b=g['ln2_b'][0])


def kernel(x, ln_in_g, ln_in_b, w_in, conv_dw_w, conv_dw_b, conv_ln_g, conv_ln_b, w_conv_out, ssm_conv_w, ssm_conv_b, ssm_dt_bias, ssm_a_log, ssm_d, ssm_norm_w, w_ssm_out, w_o, ln1_g, ln1_b, w_ffn_up, ffn_dw_w, ffn_dw_b, w_ffn_down, ln2_g, ln2_b, loss_target, m_ln_in_g, m_ln_in_b, m_w_in, m_conv_dw_w, m_conv_dw_b, m_conv_ln_g, m_conv_ln_b, m_w_conv_out, m_ssm_conv_w, m_ssm_conv_b, m_ssm_dt_bias, m_ssm_a_log, m_ssm_d, m_ssm_norm_w, m_w_ssm_out, m_w_o, m_ln1_g, m_ln1_b, m_w_ffn_up, m_ffn_dw_w, m_ffn_dw_b, m_w_ffn_down, m_ln2_g, m_ln2_b, v_ln_in_g, v_ln_in_b, v_w_in, v_conv_dw_w, v_conv_dw_b, v_conv_ln_g, v_conv_ln_b, v_w_conv_out, v_ssm_conv_w, v_ssm_conv_b, v_ssm_dt_bias, v_ssm_a_log, v_ssm_d, v_ssm_norm_w, v_w_ssm_out, v_w_o, v_ln1_g, v_ln1_b, v_w_ffn_up, v_ffn_dw_w, v_ffn_dw_b, v_w_ffn_down, v_ln2_g, v_ln2_b):
    weights = (ln_in_g, ln_in_b, w_in, conv_dw_w, conv_dw_b, conv_ln_g, conv_ln_b, w_conv_out, ssm_conv_w, ssm_conv_b,
               ssm_dt_bias, ssm_a_log, ssm_d, ssm_norm_w, w_ssm_out, w_o, ln1_g, ln1_b, w_ffn_up, ffn_dw_w, ffn_dw_b,
               w_ffn_down, ln2_g, ln2_b)
    moments_m = (m_ln_in_g, m_ln_in_b, m_w_in, m_conv_dw_w, m_conv_dw_b, m_conv_ln_g, m_conv_ln_b, m_w_conv_out,
                 m_ssm_conv_w, m_ssm_conv_b, m_ssm_dt_bias, m_ssm_a_log, m_ssm_d, m_ssm_norm_w, m_w_ssm_out, m_w_o,
                 m_ln1_g, m_ln1_b, m_w_ffn_up, m_ffn_dw_w, m_ffn_dw_b, m_w_ffn_down, m_ln2_g, m_ln2_b)
    moments_v = (v_ln_in_g, v_ln_in_b, v_w_in, v_conv_dw_w, v_conv_dw_b, v_conv_ln_g, v_conv_ln_b, v_w_conv_out,
                 v_ssm_conv_w, v_ssm_conv_b, v_ssm_dt_bias, v_ssm_a_log, v_ssm_d, v_ssm_norm_w, v_w_ssm_out, v_w_o,
                 v_ln1_g, v_ln1_b, v_w_ffn_up, v_ffn_dw_w, v_ffn_dw_b, v_w_ffn_down, v_ln2_g, v_ln2_b)
    p = dict(zip(PARAMS, weights))
    pm = dict(zip(PARAMS, moments_m))
    pv = dict(zip(PARAMS, moments_v))

    T, D = x.shape[1], x.shape[2]
    SX = w_ssm_out.shape[1] * N_DEV
    XBC = ssm_conv_b.shape[-1]
    H = ssm_d.shape[-1]
    FF = ffn_dw_b.shape[-1] // 2
    dims = (D, SX, XBC, H, FF)
    depth = w_in.shape[0]

    ALL = tuple(SHARDED)
    got = _gather_two_level(_wire_shards(p, 0, EARLY), name="gather_l0_first")
    rest_shards, got = lax.optimization_barrier((_wire_shards(p, 0, LATE), got))
    rest_handle, tok = _exchange_start(rest_shards, scatter=False, name="gather_l0_rest_start")
    first = _layer_weights(p, _full_weights(got, p, EARLY), 0, dims)
    in_flight = {}

    def start_next_gather(l, behind):
        shards, _ = lax.optimization_barrier((_wire_shards(p, l, ALL), behind))
        in_flight[l], t = _exchange_start(shards, scatter=False, name=f"gather_l{l}_start")
        in_flight[l] = (in_flight[l], shards)
        return t

    def late_weights(l):
        def late(x):
            if l > 0:
                return {}
            got = _exchange_wait(rest_handle, x, scatter=False, name="gather_l0_rest_wait")
            W = _layer_weights(p, _full_weights(got, p, LATE), 0, dims)
            if depth > 1:
                t = start_next_gather(1, got)
                W['w_conv_out'] = W['w_conv_out'] + t[0:1, 0:1].astype(W['w_conv_out'].dtype)
            return W
        return late

    xs = x.reshape(T, D)
    h, hb = _ln_fwd(xs, _row(ln_in_g) + tok[0:1, 0:1], _row(ln_in_b), name="ln_in_fwd")
    saved, layers = [], []
    for l in range(depth):
        if l > 0:
            handle, shards = in_flight.pop(l)
            got = _exchange_wait(handle, h, scatter=False, name=f"gather_l{l}_wait")
            first = _layer_weights(p, _full_weights(got, p, ALL), l, dims)
            if l + 1 < depth:
                t = start_next_gather(l + 1, got)
                first['w_dt'] = first['w_dt'] + t[0:1, 0:1].astype(first['w_dt'].dtype)
        h, hb, s, W = _layer_fwd(h, hb, first, late_weights(l), l, dims)
        saved.append(s)
        layers.append(W)
    loss_part, dh = _loss_fwd_bwd(h, loss_target.reshape(T, D), name="loss")

    layer_grads = [None] * depth
    flying, arrived = [], {}

    def wait_flying(after):
        while flying:
            l_, names, handle, parts = flying.pop(0)
            lands = _exchange_wait(handle, after, scatter=True, name=f"scatter_l{l_}_{names[0]}_wait")
            arrived.update({(l_, k): a for k, a in zip(names, lands)})
            after = lands[0]
        return after

    def start_scatter(l, names, grads_now, after):
        parts, _ = lax.optimization_barrier((_grad_parts(grads_now, p, names), wait_flying(after)))
        handle, t = _exchange_start(parts, scatter=True, name=f"scatter_l{l}_{names[0]}_start")
        flying.append((l, names, handle, parts))
        return t

    def on_ffn_grads(g):
        return start_scatter(0, FFN, dict(w_ffn_up=g['w_up'], ffn_dw_w=g['ffn_w'], w_ffn_down=g['w_down']), g['w_up'])

    def on_all_grads(g):
        layer_grads[0] = _layer_grads_to_params(g, dims)
        return start_scatter(0, NOT_FFN, layer_grads[0], g['w_dt'])

    tok = jnp.zeros((8, LANES), F32)
    for l in reversed(range(depth)):
        W = dict(layers[l], ln2_g=layers[l]['ln2_g'] + tok[0:1, 0:1])
        if l > 0:
            dh, g = _layer_bwd(dh, W, saved[l], l, dims)
            layer_grads[l] = _layer_grads_to_params(g, dims)
            tok = start_scatter(l, ALL, layer_grads[l], dh)
        else:
            dh, g = _layer_bwd(dh, W, saved[l], l, dims, on_ffn_grads, on_all_grads)
    grad_x, dg_in, db_in = _ln_bwd(xs, dh, _row(ln_in_g), _row(ln_in_b), silu=False, name="ln_in_bwd")
    wait_flying(grad_x)

    small = {k: jnp.stack([layer_grads[l][k] for l in range(depth)]) for k in layer_grads[0] if k not in SHARDED}
    small['ln_in_g'], small['ln_in_b'], small['loss'] = dg_in[0], db_in[0], loss_part
    grads = _allreduce_small(small)
    loss = grads.pop('loss').reshape(())

    delta, new_m, new_v = {}, {}, {}
    for k in SHARDED:
        grads[k], delta[k], new_m[k], new_v[k] = _sum_adamw([arrived[(l, k)] for l in range(depth)], p[k], pm[k], pv[k],
                                                            name="sum_adamw_" + k)
    rest = [k for k in PARAMS if k not in SHARDED]
    flat = lambda d: _to_rows(jnp.concatenate([d[k].reshape(-1) for k in rest]), 8)
    d_, m_, v_ = _adamw(flat(p), flat(grads), flat(pm), flat(pv), name="adamw_small")
    off = 0
    for k in rest:
        n, shp = p[k].size, p[k].shape
        delta[k] = d_.reshape(-1)[off:off + n].reshape(shp)
        new_m[k] = m_.reshape(-1)[off:off + n].reshape(shp)
        new_v[k] = v_.reshape(-1)[off:off + n].reshape(shp)
        off += n

    return (loss, grad_x.reshape(x.shape), *[grads[k] for k in PARAMS], *[delta[k] for k in PARAMS],
            *[new_m[k] for k in PARAMS], *[new_v[k] for k in PARAMS])
```

```python
import math

import jax
import jax.numpy as jnp
from jax import lax
from jax.experimental import pallas as pl
from jax.experimental.pallas import tpu as pltpu

F32 = jnp.float32
BF16 = jnp.bfloat16
WIRE = jnp.bfloat16
HIGHEST = lax.Precision.HIGHEST

DEPTH = 2
SSM_STATE = 128
SSM_CHUNK = 128
SSM_GROUPS = 4
SSM_HEAD_DIM = 64
DN_ALPHA = (2 * DEPTH) ** 0.25
LN_EPS = 1e-5
RMS_EPS = 1e-5
ADAM_LR = 0.001
ADAM_B1 = 0.9
ADAM_B2 = 0.999
ADAM_EPS = 1e-08
ADAM_WD = 0.01
ADAM_STEP = 10

N_DEV = 8
LANES = 128
VMEM_LIMIT = 48 * 1024 * 1024
WHOLE_K_VMEM = 40 * 1024 * 1024
GLU_TILE = 1024
FFN_TILE = 2816
CONV_ROWS = 256

PARAMS = ['ln_in_g', 'ln_in_b', 'w_in', 'conv_dw_w', 'conv_dw_b', 'conv_ln_g', 'conv_ln_b', 'w_conv_out',
          'ssm_conv_w', 'ssm_conv_b', 'ssm_dt_bias', 'ssm_a_log', 'ssm_d', 'ssm_norm_w', 'w_ssm_out', 'w_o',
          'ln1_g', 'ln1_b', 'w_ffn_up', 'ffn_dw_w', 'ffn_dw_b', 'w_ffn_down', 'ln2_g', 'ln2_b']
SHARDED = {'w_in': (2, True), 'conv_dw_w': (2, False), 'w_conv_out': (1, True), 'ssm_conv_w': (2, False),
           'w_ssm_out': (1, True), 'w_o': (1, True), 'w_ffn_up': (2, True), 'ffn_dw_w': (2, False),
           'w_ffn_down': (1, True)}


def _sds(shape, dtype):
    return jax.ShapeDtypeStruct(tuple(shape), dtype)


def _tile(dim, pref):
    return pref if dim % pref == 0 else dim


def _pick(dim, pref, mult):
    best = None
    for t in range(mult, min(dim, pref) + 1, mult):
        if dim % t == 0:
            best = t
    return best or dim


def _cp(*sem):
    return pltpu.CompilerParams(dimension_semantics=sem, vmem_limit_bytes=VMEM_LIMIT)


def _f32(x):
    return x.astype(F32)


def _sigmoid(x):
    return 1.0 / (1.0 + jnp.exp(-x))


def _silu_and_grad(x):
    s = _sigmoid(x)
    return x * s, s * (1.0 + x * (1.0 - s))


def _ln_stats(s):
    mu = jnp.mean(s, axis=-1, keepdims=True)
    xc = s - mu
    var = jnp.mean(xc * xc, axis=-1, keepdims=True)
    rstd = lax.rsqrt(var + LN_EPS)
    return xc * rstd, rstd


def _dot(a, b, dims):
    return lax.dot_general(a, b, (dims, ((), ())), preferred_element_type=F32)


_NN = ((1,), (0,))
_NT = ((1,), (1,))
_TN = ((0,), (0,))


def _mm(a, b, *, mode, out_dtype, name, res=None, res_scale=1.0, tm=1024, tn=1408, tk=1408):
    if mode == 'nn':
        (M, K), (_, N) = a.shape, b.shape
    elif mode == 'nt':
        (M, K), (N, _) = a.shape, b.shape
    else:
        (K, M), (_, N) = a.shape, b.shape
        tm, tk = 1408, 2048
    tm, tn, tk = _pick(M, tm, 8), _pick(N, tn, LANES), _pick(K, tk, LANES)
    if mode != 'tn':
        for rows in (tm, _pick(M, tm // 2, 8), _pick(M, tm // 4, 8)):
            blocks = rows * K * a.dtype.itemsize + tn * K * b.dtype.itemsize + rows * tn * (4 + (4 if res is not None else 0))
            if 2 * blocks <= WHOLE_K_VMEM:
                tm, tk = rows, K
                break
    nk = K // tk
    dims = {'nn': _NN, 'nt': _NT, 'tn': _TN}[mode]

    def body(*refs):
        if res is None:
            a_ref, b_ref, o_ref = refs[:3]
            r_ref = None
        else:
            a_ref, b_ref, r_ref, o_ref = refs[:4]
        p = _dot(a_ref[...].astype(BF16), b_ref[...].astype(BF16), dims)

        def finish(acc):
            if r_ref is not None:
                acc = acc + res_scale * r_ref[...]
            o_ref[...] = acc.astype(out_dtype)

        if nk == 1:
            finish(p)
        else:
            acc_ref = refs[-1]
            k = pl.program_id(2)

            @pl.when(k == 0)
            def _():
                acc_ref[...] = p

            @pl.when(k > 0)
            def _():
                acc_ref[...] += p

            @pl.when(k == nk - 1)
            def _():
                finish(acc_ref[...])

    if mode == 'nn':
        a_spec = pl.BlockSpec((tm, tk), lambda i, j, k: (i, k))
        b_spec = pl.BlockSpec((tk, tn), lambda i, j, k: (k, j))
    elif mode == 'nt':
        a_spec = pl.BlockSpec((tm, tk), lambda i, j, k: (i, k))
        b_spec = pl.BlockSpec((tn, tk), lambda i, j, k: (j, k))
    else:
        a_spec = pl.BlockSpec((tk, tm), lambda i, j, k: (k, i))
        b_spec = pl.BlockSpec((tk, tn), lambda i, j, k: (k, j))
    o_spec = pl.BlockSpec((tm, tn), lambda i, j, k: (i, j))
    in_specs = [a_spec, b_spec] + ([o_spec] if res is not None else [])
    args = (a, b) + ((res,) if res is not None else ())
    return pl.pallas_call(
        body, name=name, out_shape=_sds((M, N), out_dtype), grid=(M // tm, N // tn, nk),
        in_specs=in_specs, out_specs=o_spec,
        scratch_shapes=[pltpu.VMEM((tm, tn), F32)] if nk > 1 else [],
        compiler_params=_cp("parallel", "parallel", "arbitrary"))(*args)


def _mm_nt_cat(a_list, b_list, res, res_scale, *, name, tm=512, tn=512):
    M, N = a_list[0].shape[0], b_list[0].shape[0]
    tm, tn = _pick(M, tm, 8), _pick(N, tn, LANES)
    n = len(a_list)

    def body(*refs):
        a_refs, b_refs, r_ref, o_ref = refs[:n], refs[n:2 * n], refs[2 * n], refs[2 * n + 1]
        acc = res_scale * r_ref[...]
        for i in range(n):
            acc = acc + _dot(a_refs[i][...].astype(BF16), b_refs[i][...].astype(BF16), _NT)
        o_ref[...] = acc

    a_specs = [pl.BlockSpec((tm, a.shape[1]), lambda j, m: (m, 0)) for a in a_list]
    b_specs = [pl.BlockSpec((tn, b.shape[1]), lambda j, m: (j, 0)) for b in b_list]
    blk = pl.BlockSpec((tm, tn), lambda j, m: (m, j))
    return pl.pallas_call(
        body, name=name, out_shape=_sds((M, N), F32), grid=(N // tn, M // tm), in_specs=a_specs + b_specs + [blk],
        out_specs=blk, compiler_params=_cp("parallel", "parallel"))(*a_list, *b_list, res)


def _mm_res_ln(a, w, res, g, b, *, name, tm=512, tk=1408):
    (M, K), (_, N) = a.shape, w.shape
    tm, tk = _pick(M, tm, 8), _pick(K, tk, LANES)
    if 2 * (tm * K * a.dtype.itemsize + K * N * w.dtype.itemsize + tm * N * 14) <= WHOLE_K_VMEM:
        tk = K
    nk = K // tk

    def body(a_ref, w_ref, r_ref, g_ref, b_ref, h_ref, s_ref, hb_ref, acc_ref):
        k = pl.program_id(1)
        p = _dot(a_ref[...].astype(BF16), w_ref[...].astype(BF16), _NN)

        def finish(acc):
            s = DN_ALPHA * r_ref[...] + acc
            xhat, _ = _ln_stats(s)
            s_ref[...] = s
            h = xhat * g_ref[...] + b_ref[...]
            h_ref[...] = h
            hb_ref[...] = h.astype(BF16)

        if nk == 1:
            finish(p)
            return

        @pl.when(k == 0)
        def _():
            acc_ref[...] = p

        @pl.when(k > 0)
        def _():
            acc_ref[...] += p

        @pl.when(k == nk - 1)
        def _():
            finish(acc_ref[...])

    row = pl.BlockSpec((tm, N), lambda i, k: (i, 0))
    vec = pl.BlockSpec((1, N), lambda i, k: (0, 0))
    return pl.pallas_call(
        body, name=name, out_shape=(_sds((M, N), F32), _sds((M, N), F32), _sds((M, N), BF16)), grid=(M // tm, nk),
        in_specs=[pl.BlockSpec((tm, tk), lambda i, k: (i, k)), pl.BlockSpec((tk, N), lambda i, k: (k, 0)), row, vec, vec],
        out_specs=(row, row, row), scratch_shapes=[pltpu.VMEM((tm, N), F32)],
        compiler_params=_cp("parallel", "arbitrary"))(a, w, res, g, b)


def _ln_fwd(x, g, b, *, name):
    T, D = x.shape
    tt = _tile(T, 512)

    def body(x_ref, g_ref, b_ref, o_ref, ob_ref):
        xhat, _ = _ln_stats(x_ref[...])
        h = xhat * g_ref[...] + b_ref[...]
        o_ref[...] = h
        ob_ref[...] = h.astype(BF16)

    row = pl.BlockSpec((tt, D), lambda i: (i, 0))
    vec = pl.BlockSpec((1, D), lambda i: (0, 0))
    return pl.pallas_call(body, name=name, out_shape=(_sds((T, D), F32), _sds((T, D), BF16)), grid=(T // tt,),
                          in_specs=[row, vec, vec], out_specs=(row, row), compiler_params=_cp("parallel"))(x, g, b)


def _ln_bwd(s, dy, g, b, *, silu, name):
    T, D = s.shape
    tt = _tile(T, 512)

    def body(s_ref, dy_ref, g_ref, b_ref, ds_ref, dg_ref, db_ref):
        i = pl.program_id(0)
        xhat, rstd = _ln_stats(s_ref[...])
        gg = g_ref[...]
        dyl = _f32(dy_ref[...])
        if silu:
            _, dsilu = _silu_and_grad(xhat * gg + b_ref[...])
            dyl = dyl * dsilu
        dxh = dyl * gg
        m1 = jnp.mean(dxh, axis=-1, keepdims=True)
        m2 = jnp.mean(dxh * xhat, axis=-1, keepdims=True)
        ds_ref[...] = rstd * (dxh - m1 - xhat * m2)

        @pl.when(i == 0)
        def _():
            dg_ref[...] = jnp.zeros_like(dg_ref)
            db_ref[...] = jnp.zeros_like(db_ref)

        dg_ref[...] += jnp.sum(dyl * xhat, axis=0, keepdims=True)
        db_ref[...] += jnp.sum(dyl, axis=0, keepdims=True)

    row = pl.BlockSpec((tt, D), lambda i: (i, 0))
    vec = pl.BlockSpec((1, D), lambda i: (0, 0))
    return pl.pallas_call(body, name=name, out_shape=(_sds((T, D), F32), _sds((1, D), F32), _sds((1, D), F32)),
                          grid=(T // tt,), in_specs=[row, row, vec, vec], out_specs=(row, vec, vec),
                          compiler_params=_cp("arbitrary"))(s, dy, g, b)


def _loss_fwd_bwd(h, tgt, *, name):
    T, D = h.shape
    tt = _tile(T, 512)

    def body(h_ref, t_ref, dh_ref, l_ref):
        i = pl.program_id(0)
        e = h_ref[...] - t_ref[...]
        dh_ref[...] = e * (1.0 / D)

        @pl.when(i == 0)
        def _():
            l_ref[...] = jnp.zeros_like(l_ref)

        part = jnp.sum(jnp.sum(e * e, axis=1, keepdims=True), axis=0, keepdims=True) * (0.5 / D)
        l_ref[...] += jnp.broadcast_to(part, l_ref.shape)

    row = pl.BlockSpec((tt, D), lambda i: (i, 0))
    one = pl.BlockSpec((8, LANES), lambda i: (0, 0))
    dh, l = pl.pallas_call(body, name=name, out_shape=(_sds((T, D), F32), _sds((8, LANES), F32)), grid=(T // tt,),
                           in_specs=[row, row], out_specs=(row, one), compiler_params=_cp("arbitrary"))(h, tgt)
    return l[0:1, 0:1], dh


def _halo_rows(k):
    return 32 if k > 17 else 16


def _shifted(ext, K, first):
    rolled = {0: ext}
    out = []
    for k in range(K):
        r = (first + k) % 8
        if r not in rolled:
            rolled[r] = pltpu.roll(ext, ext.shape[0] - r, 0)
        out.append((rolled[r], first + k - r))
    return out


def _taps(ext, w_ref, sl, K, first, n, reverse=False):
    acc = None
    for k, (z, base) in enumerate(_shifted(ext, K, first)):
        kw = K - 1 - k if reverse else k
        term = w_ref[kw:kw + 1, sl] * z[base:base + n]
        acc = term if acc is None else acc + term
    return acc


def _tap_sums(d, ext, dw_ref, sl, K, first):
    n = d.shape[0]
    for k, (z, base) in enumerate(_shifted(ext, K, first)):
        dw_ref[k:k + 1, sl] += jnp.sum(d * z[base:base + n], axis=0, keepdims=True)


def _strip_width(width):
    return LANES if width % LANES == 0 else width


def _for_strips(width, fn):
    sw = _strip_width(width)

    def step(s, carry):
        fn(pl.ds(pl.multiple_of(s * sw, sw), sw), s)
        return carry

    lax.fori_loop(0, width // sw, step, 0)


def _prev_rows(tt, hb):
    return lambda t: jnp.maximum(t * (tt // hb) - 1, 0)


def _next_rows(tt, hb, T):
    return lambda t: jnp.minimum((t + 1) * (tt // hb), T // hb - 1)


def _conf_fwd(u, w, b, g, beta, *, D, name):
    T = u.shape[0]
    K = w.shape[0]
    tt, hb, tc = _tile(T, CONV_ROWS), _halo_rows(K), _tile(D, GLU_TILE)
    sw = _strip_width(tc)
    per_tile = tc // sw
    prev = _prev_rows(tt, hb)

    def body(u_ref, uh_ref, w_ref, b_ref, g_ref, beta_ref, v1_ref, v3_ref):
        i = pl.program_id(0)

        def strip(sl, s):
            a0 = (s // per_tile) * (2 * tc) + (s % per_tile) * sw
            a_sl, g_sl = pl.ds(pl.multiple_of(a0, sw), sw), pl.ds(pl.multiple_of(a0 + tc, sw), sw)
            halo = jnp.where(i > 0, _f32(uh_ref[:, a_sl]) * _sigmoid(_f32(uh_ref[:, g_sl])), 0.0)
            ext = jnp.concatenate([halo, _f32(u_ref[:, a_sl]) * _sigmoid(_f32(u_ref[:, g_sl]))], axis=0)
            v1_ref[:, sl] = _taps(ext, w_ref, sl, K, hb - (K - 1), tt) + b_ref[:, sl]

        _for_strips(D, strip)
        xhat, _ = _ln_stats(v1_ref[...])
        v2 = xhat * g_ref[...] + beta_ref[...]
        v3_ref[...] = (v2 * _sigmoid(v2)).astype(BF16)

    row = pl.BlockSpec((tt, D), lambda t: (t, 0))
    vec = pl.BlockSpec((1, D), lambda t: (0, 0))
    return pl.pallas_call(
        body, name=name, out_shape=(_sds((T, D), F32), _sds((T, D), BF16)), grid=(T // tt,),
        in_specs=[pl.BlockSpec((tt, 2 * D), lambda t: (t, 0)), pl.BlockSpec((hb, 2 * D), lambda t: (prev(t), 0)),
                  pl.BlockSpec((K, D), lambda t: (0, 0)), vec, vec, vec],
        out_specs=(row, row), compiler_params=_cp("parallel"))(u, u, w, b, g, beta)


def _conv_act_fwd(x, off, w, b, *, ffn, name):
    T = x.shape[0]
    K, Cw = w.shape
    tt, hb = _tile(T, CONV_ROWS), _halo_rows(K)
    tc = _tile(Cw // 2, FFN_TILE) if ffn else _pick(math.gcd(Cw, off), 1536, LANES)
    xw = 2 * tc if ffn else tc
    assert off % xw == 0
    ob = off // xw
    prev = _prev_rows(tt, hb)

    def body(x_ref, xh_ref, w_ref, b_ref, o_ref):
        first = pl.program_id(0) == 0

        def pre_of(sl):
            ext = jnp.concatenate([jnp.where(first, 0.0, _f32(xh_ref[:, sl])), _f32(x_ref[:, sl])], axis=0)
            return _taps(ext, w_ref, sl, K, hb - (K - 1), tt) + b_ref[:, sl]

        def strip(sl, s):
            if ffn:
                gate = pre_of(sl)
                val = pre_of(pl.ds(pl.multiple_of(tc + s * sw, sw), sw))
                o_ref[:, sl] = (gate * _sigmoid(gate) * val).astype(BF16)
            else:
                pre = pre_of(sl)
                o_ref[:, sl] = pre * _sigmoid(pre)

        _for_strips(tc, strip)

    sw = _strip_width(tc)
    return pl.pallas_call(
        body, name=name, out_shape=_sds((T, Cw // 2), BF16) if ffn else _sds((T, Cw), F32), grid=(T // tt, Cw // xw),
        in_specs=[pl.BlockSpec((tt, xw), lambda t, c: (t, c + ob)), pl.BlockSpec((hb, xw), lambda t, c: (prev(t), c + ob)),
                  pl.BlockSpec((K, xw), lambda t, c: (0, c)), pl.BlockSpec((1, xw), lambda t, c: (0, c))],
        out_specs=pl.BlockSpec((tt, tc), lambda t, c: (t, c)),
        compiler_params=_cp("parallel", "parallel"))(x, x, w, b)


def _conv_act_bwd(x, off, w, b, dout, *, ffn, name):
    T = x.shape[0]
    K, Cw = w.shape
    tt, hb = _tile(T, CONV_ROWS), _halo_rows(K)
    tc = _tile(Cw // 2, FFN_TILE) if ffn else _pick(math.gcd(Cw, off), 1536, LANES)
    xw = 2 * tc if ffn else tc
    assert off % xw == 0
    ob = off // xw
    nt = T // tt
    prev, nxt = _prev_rows(tt, hb), _next_rows(tt, hb, T)

    sw = _strip_width(tc)

    def body(x_ref, xp_ref, xn_ref, d_ref, dn_ref, w_ref, b_ref, dx_ref, dw_ref, db_ref):
        t = pl.program_id(1)

        @pl.when(t == 0)
        def _():
            dw_ref[...] = jnp.zeros_like(dw_ref)
            db_ref[...] = jnp.zeros_like(db_ref)

        def ext_of(sl):
            return jnp.concatenate([jnp.where(t == 0, 0.0, _f32(xp_ref[:, sl])), _f32(x_ref[:, sl]), _f32(xn_ref[:, sl])], axis=0)

        def pre_of(ext, sl):
            return _taps(ext, w_ref, sl, K, hb - (K - 1), tt + hb) + b_ref[:, sl]

        def finish(ext, dpre, sl):
            dx_ref[:, sl] = _taps(dpre, w_ref, sl, K, 0, tt, reverse=True).astype(BF16)
            dp = dpre[0:tt]
            _tap_sums(dp, ext, dw_ref, sl, K, hb - (K - 1))
            db_ref[:, sl] += jnp.sum(dp, axis=0, keepdims=True)

        def strip(sl, s):
            d = jnp.concatenate([d_ref[:, sl], jnp.where(t == nt - 1, 0.0, dn_ref[:, sl])], axis=0)
            if ffn:
                vsl = pl.ds(pl.multiple_of(tc + s * sw, sw), sw)
                eg, ev = ext_of(sl), ext_of(vsl)
                sg, dsg = _silu_and_grad(pre_of(eg, sl))
                val = pre_of(ev, vsl)
                finish(eg, d * val * dsg, sl)
                finish(ev, d * sg, vsl)
            else:
                ext = ext_of(sl)
                finish(ext, d * _silu_and_grad(pre_of(ext, sl))[1], sl)

        _for_strips(tc, strip)

    return pl.pallas_call(
        body, name=name, out_shape=(_sds((T, Cw), BF16), _sds((K, Cw), F32), _sds((1, Cw), F32)),
        grid=(Cw // xw, nt),
        in_specs=[pl.BlockSpec((tt, xw), lambda c, t: (t, c + ob)), pl.BlockSpec((hb, xw), lambda c, t: (prev(t), c + ob)),
                  pl.BlockSpec((hb, xw), lambda c, t: (nxt(t), c + ob)),
                  pl.BlockSpec((tt, tc), lambda c, t: (t, c)), pl.BlockSpec((hb, tc), lambda c, t: (nxt(t), c)),
                  pl.BlockSpec((K, xw), lambda c, t: (0, c)), pl.BlockSpec((1, xw), lambda c, t: (0, c))],
        out_specs=(pl.BlockSpec((tt, xw), lambda c, t: (t, c)), pl.BlockSpec((K, xw), lambda c, t: (0, c)),
                   pl.BlockSpec((1, xw), lambda c, t: (0, c))),
        compiler_params=_cp("parallel", "arbitrary"))(x, x, x, dout, dout, w, b)


def _glu_conv_bwd(dpre, u, w, *, name):
    T, C = dpre.shape
    K = w.shape[0]
    tt, hb, tc = _tile(T, CONV_ROWS), _halo_rows(K), _tile(C, GLU_TILE)
    nt = T // tt
    prev, nxt = _prev_rows(tt, hb), _next_rows(tt, hb, T)

    sw = _strip_width(tc)

    def body(d_ref, dn_ref, x_ref, xh_ref, w_ref, dx_ref, dw_ref, db_ref):
        t = pl.program_id(1)

        @pl.when(t == 0)
        def _():
            dw_ref[...] = jnp.zeros_like(dw_ref)
            db_ref[...] = jnp.zeros_like(db_ref)

        def strip(sl, s):
            gsl = pl.ds(pl.multiple_of(tc + s * sw, sw), sw)
            d = d_ref[:, sl]
            dext = jnp.concatenate([d, jnp.where(t == nt - 1, 0.0, dn_ref[:, sl])], axis=0)
            dxin = _taps(dext, w_ref, sl, K, 0, tt, reverse=True)
            a, sg = _f32(x_ref[:, sl]), _sigmoid(_f32(x_ref[:, gsl]))
            v0 = a * sg
            dx_ref[:, sl] = (dxin * sg).astype(BF16)
            dx_ref[:, gsl] = (dxin * v0 * (1.0 - sg)).astype(BF16)
            halo = jnp.where(t == 0, 0.0, _f32(xh_ref[:, sl]) * _sigmoid(_f32(xh_ref[:, gsl])))
            _tap_sums(d, jnp.concatenate([halo, v0], axis=0), dw_ref, sl, K, hb - (K - 1))
            db_ref[:, sl] += jnp.sum(d, axis=0, keepdims=True)

        _for_strips(tc, strip)

    return pl.pallas_call(
        body, name=name, out_shape=(_sds((T, 2 * C), BF16), _sds((K, C), F32), _sds((1, C), F32)), grid=(C // tc, nt),
        in_specs=[pl.BlockSpec((tt, tc), lambda c, t: (t, c)), pl.BlockSpec((hb, tc), lambda c, t: (nxt(t), c)),
                  pl.BlockSpec((tt, 2 * tc), lambda c, t: (t, c)), pl.BlockSpec((hb, 2 * tc), lambda c, t: (prev(t), c)),
                  pl.BlockSpec((K, tc), lambda c, t: (0, c))],
        out_specs=(pl.BlockSpec((tt, 2 * tc), lambda c, t: (t, c)), pl.BlockSpec((K, tc), lambda c, t: (0, c)),
                   pl.BlockSpec((1, tc), lambda c, t: (0, c))),
        compiler_params=_cp("parallel", "arbitrary"))(dpre, dpre, u, u, w)


def _softplus(x):
    t = jnp.exp(-jnp.abs(x))
    u = 1.0 + t
    log1p = jnp.where(u == 1.0, t, jnp.log(u) * t / jnp.where(u == 1.0, 1.0, u - 1.0))
    return jnp.maximum(x, 0.0) + log1p


def _dt_fwd(udt, bias, *, H, name):
    T = udt.shape[0]
    tt = _tile(T, 1024)

    def body(u_ref, b_ref, o_ref):
        lane = lax.broadcasted_iota(jnp.int32, (tt, LANES), 1)
        o_ref[...] = jnp.where(lane < H, _softplus(u_ref[...] + b_ref[...]), 0.0)

    row = pl.BlockSpec((tt, LANES), lambda i: (i, 0))
    return pl.pallas_call(body, name=name, out_shape=_sds((T, LANES), F32), grid=(T // tt,),
                          in_specs=[row, pl.BlockSpec((1, LANES), lambda i: (0, 0))], out_specs=row,
                          compiler_params=_cp("parallel"))(udt, bias)


def _dt_bwd(da, ddtx, dt, udt, bias, a_col, *, H, name):
    T = da.shape[0]
    tt = _tile(T, 1024)

    def body(da_ref, dx_ref, dt_ref, u_ref, b_ref, a_ref, draw_ref, dbias_ref, dalog_ref):
        i = pl.program_id(0)
        lane = lax.broadcasted_iota(jnp.int32, (tt, LANES), 1)
        dav = da_ref[...]
        ddt = dav * a_ref[...] + dx_ref[...]
        draw = jnp.where(lane < H, ddt * _sigmoid(u_ref[...] + b_ref[...]), 0.0)
        draw_ref[...] = draw.astype(BF16)

        @pl.when(i == 0)
        def _():
            dbias_ref[...] = jnp.zeros_like(dbias_ref)
            dalog_ref[...] = jnp.zeros_like(dalog_ref)

        dbias_ref[...] += jnp.sum(draw, axis=0, keepdims=True)
        dalog_ref[...] += jnp.sum(dav * dt_ref[...], axis=0, keepdims=True) * a_ref[...]

    row = pl.BlockSpec((tt, LANES), lambda i: (i, 0))
    vec = pl.BlockSpec((1, LANES), lambda i: (0, 0))
    return pl.pallas_call(body, name=name,
                          out_shape=(_sds((T, LANES), BF16), _sds((1, LANES), F32), _sds((1, LANES), F32)),
                          grid=(T // tt,), in_specs=[row, row, row, row, vec, vec], out_specs=(row, vec, vec),
                          compiler_params=_cp("arbitrary"))(da, ddtx, dt, udt, bias, a_col)


def _ssd_consts():
    L = SSM_CHUNK
    r = lax.broadcasted_iota(jnp.int32, (L, L), 0)
    c = lax.broadcasted_iota(jnp.int32, (L, L), 1)
    return r, c


def _ssd_chunk_decays(dtc_ref, dtr_ref, acol_ref, arow_ref, cs_ref, csr_ref, r, c):
    L = SSM_CHUNK
    dtc = dtc_ref[...]
    tril = (r >= c).astype(F32)
    triu = (r <= c).astype(F32)
    cs_ref[...] = jnp.dot(tril, dtc * acol_ref[...], precision=HIGHEST, preferred_element_type=F32)
    csr_ref[...] = jnp.dot(dtr_ref[...] * arow_ref[...], triu, precision=HIGHEST, preferred_element_type=F32)
    cs = cs_ref[...]
    cs_last = cs_ref[L - 1:L, :]
    return dtc, cs, jnp.exp(cs), jnp.exp(cs_last - cs), jnp.exp(cs_last), triu


def _head_spread(arrs, g, heads_per_group):
    P = SSM_HEAD_DIM
    gw = heads_per_group * P
    head = lax.broadcasted_iota(jnp.int32, (LANES, gw), 0)
    lane = lax.broadcasted_iota(jnp.int32, (LANES, gw), 1)
    spread = (head == g * heads_per_group + lane // P).astype(BF16)
    out = _dot(jnp.concatenate(arrs, axis=0).astype(BF16), spread, _NN)
    L = arrs[0].shape[0]
    return [out[i * L:(i + 1) * L] for i in range(len(arrs))]


def _ssd_fwd(xbc, dt, dtT, a_col, a_row, d_col, *, H, name):
    T, XBC = xbc.shape
    L, N, G, P = SSM_CHUNK, SSM_STATE, SSM_GROUPS, SSM_HEAD_DIM
    SX = H * P
    HR = dtT.shape[0]
    nc = T // L
    heads_per_group = H // G

    def body(x_ref, dtc_ref, dtr_ref, acol_ref, arow_ref, d_ref, y_ref, hst_ref, state, cs_ref, csr_ref):
        ci = pl.program_id(0)

        @pl.when(ci == 0)
        def _():
            state[...] = jnp.zeros_like(state)

        hst_ref[0] = state[...]
        r, c = _ssd_consts()
        tri = r >= c
        lane_lo = c < P
        row_lo = r < P
        lane1_lo = lax.broadcasted_iota(jnp.int32, (1, LANES), 1) < P
        dtc, cs, e, ds, cd, _ = _ssd_chunk_decays(dtc_ref, dtr_ref, acol_ref, arow_ref, cs_ref, csr_ref, r, c)
        dsk = d_ref[...]

        for g in range(G):
            Bg = x_ref[:, SX + g * N:SX + (g + 1) * N].astype(BF16)
            Cg = x_ref[:, SX + G * N + g * N:SX + G * N + (g + 1) * N].astype(BF16)
            Gm = _dot(Cg, Bg, _NT)
            dt_g, e_g, ds_g = _head_spread([dtc, e, ds], g, heads_per_group)
            st = []
            for j in range(g * heads_per_group // 2, (g + 1) * heads_per_group // 2):
                h0 = 2 * j
                sl = slice(2 * P * j, 2 * P * (j + 1))
                gl = slice(sl.start - g * heads_per_group * P, sl.stop - g * heads_per_group * P)
                x2 = x_ref[:, sl]
                X2 = x2 * dt_g[:, gl]
                H2 = state[sl, :]
                st.append(dict(h0=h0, sl=sl, x2=x2, X2=X2, H2=H2, e2=e_g[:, gl], R2=_dot(Cg, H2.astype(BF16), _NT),
                               S2=_dot((X2 * ds_g[:, gl]).astype(BF16), Bg, _TN)))
            for s in st:
                ms = []
                for h in (s['h0'], s['h0'] + 1):
                    seg = cs[:, h:h + 1] - csr_ref[h:h + 1, :]
                    ms.append((Gm * jnp.where(tri, jnp.exp(jnp.where(tri, seg, 0.0)), 0.0)).astype(BF16))
                xst = jnp.concatenate([jnp.where(lane_lo, s['X2'], 0.0), jnp.where(lane_lo, 0.0, s['X2'])], axis=0).astype(BF16)
                s['yd'] = _dot(jnp.concatenate(ms, axis=1), xst, _NN)
            for s in st:
                h0, sl = s['h0'], s['sl']
                dsk2 = jnp.where(lane1_lo, dsk[:, h0:h0 + 1], dsk[:, h0 + 1:h0 + 2])
                y_ref[:, sl] = (s['yd'] + s['e2'] * s['R2'] + s['x2'] * dsk2).astype(BF16)
                state[sl, :] = jnp.where(row_lo, cd[:, h0:h0 + 1], cd[:, h0 + 1:h0 + 2]) * s['H2'] + s['S2']

    vec = pl.BlockSpec((1, LANES), lambda i: (0, 0))
    return pl.pallas_call(
        body, name=name, out_shape=(_sds((T, SX), BF16), _sds((nc, SX, N), F32)), grid=(nc,),
        in_specs=[pl.BlockSpec((L, XBC), lambda i: (i, 0)), pl.BlockSpec((L, LANES), lambda i: (i, 0)),
                  pl.BlockSpec((HR, L), lambda i: (0, i)), vec, pl.BlockSpec((HR, 1), lambda i: (0, 0)), vec],
        out_specs=(pl.BlockSpec((L, SX), lambda i: (i, 0)), pl.BlockSpec((1, SX, N), lambda i: (i, 0, 0))),
        scratch_shapes=[pltpu.VMEM((SX, N), F32), pltpu.VMEM((L, LANES), F32), pltpu.VMEM((HR, L), F32)],
        compiler_params=_cp("arbitrary"))(xbc, dt, dtT, a_col, a_row, d_col)


def _ssd_bwd(xbc, dt, dtT, a_col, a_row, d_col, hst, dy, *, H, name):
    T, XBC = xbc.shape
    L, N, G, P = SSM_CHUNK, SSM_STATE, SSM_GROUPS, SSM_HEAD_DIM
    SX = H * P
    HR = dtT.shape[0]
    nc = T // L
    heads_per_group = H // G

    def body(x_ref, dtc_ref, dtr_ref, acol_ref, arow_ref, d_ref, hst_ref, dy_ref,
             dx_ref, da_ref, ddtx_ref, dd_ref, dstate, cs_ref, csr_ref):
        ci = pl.program_id(0)

        @pl.when(ci == 0)
        def _():
            dstate[...] = jnp.zeros_like(dstate)
            dd_ref[...] = jnp.zeros_like(dd_ref)

        r, c = _ssd_consts()
        tri = r >= c
        lane_lo = c < P
        row_lo = r < P
        lane1 = lax.broadcasted_iota(jnp.int32, (1, LANES), 1)
        rowc = lax.broadcasted_iota(jnp.int32, (L, 1), 0)
        dtc, cs, e, ds, cd, triu = _ssd_chunk_decays(dtc_ref, dtr_ref, acol_ref, arow_ref, cs_ref, csr_ref, r, c)
        triu_b = triu.astype(BF16)
        dsk = d_ref[...]
        last_row = rowc == L - 1

        triT = r <= c

        def halves(v, axis):
            return jnp.concatenate([jnp.where(lane_lo, v, 0.0), jnp.where(lane_lo, 0.0, v)], axis=axis)

        def head_sum(v, h):
            lo = jnp.sum(jnp.where(lane1 < P, v, 0.0), axis=1, keepdims=True) * (lane1 == h).astype(F32)
            hi = jnp.sum(jnp.where(lane1 < P, 0.0, v), axis=1, keepdims=True) * (lane1 == h + 1).astype(F32)
            return lo + hi

        GW = heads_per_group * P
        gl = lax.broadcasted_iota(jnp.int32, (GW, LANES), 0)
        gc = lax.broadcasted_iota(jnp.int32, (GW, LANES), 1)
        wl = lax.broadcasted_iota(jnp.int32, (heads_per_group * L, LANES), 0)
        wc = lax.broadcasted_iota(jnp.int32, (heads_per_group * L, LANES), 1)
        wide_r = lax.broadcasted_iota(jnp.int32, (L, heads_per_group * L), 0)
        wide_c = lax.broadcasted_iota(jnp.int32, (L, heads_per_group * L), 1)
        below_diag = (wide_c % L) < wide_r
        sums = jnp.zeros((3 * L, LANES), F32)
        da_q = jnp.zeros((L, LANES), F32)
        dcd_acc = jnp.zeros((1, LANES), F32)
        dd_acc = jnp.zeros((1, LANES), F32)
        for g in range(G):
            bsl = slice(SX + g * N, SX + (g + 1) * N)
            csl = slice(SX + G * N + g * N, SX + G * N + (g + 1) * N)
            Bg = x_ref[:, bsl].astype(BF16)
            Cg = x_ref[:, csl].astype(BF16)
            Gm = _dot(Cg, Bg, _NT)
            GmT = _dot(Bg, Cg, _NT)
            dt_g, e_g, ds_g = _head_spread([dtc, e, ds], g, heads_per_group)
            st = []
            for j in range(g * heads_per_group // 2, (g + 1) * heads_per_group // 2):
                h0 = 2 * j
                sl = slice(2 * P * j, 2 * P * (j + 1))
                s = dict(h0=h0, sl=sl, x2=x_ref[:, sl], dy2=_f32(dy_ref[:, sl]), H2=hst_ref[0, sl, :], dHn=dstate[sl, :])
                gsl = slice(sl.start - g * heads_per_group * P, sl.stop - g * heads_per_group * P)
                s['dt2'], s['e2'], s['ds2'] = dt_g[:, gsl], e_g[:, gsl], ds_g[:, gsl]
                s['X2'] = s['x2'] * s['dt2']
                s['H2b'], s['dHnb'] = s['H2'].astype(BF16), s['dHn'].astype(BF16)
                st.append(s)
            for s in st:
                s['R2'] = _dot(Cg, s['H2b'], _NT)
                s['dXd'] = _dot(Bg, s['dHnb'], _NT)
                s['dM2'] = _dot(s['dy2'].astype(BF16), halves(s['X2'], 0).astype(BF16), _NT)
            dG = jnp.zeros((L, L), F32)
            qs = []
            for s in st:
                mts = []
                for i, h in enumerate((s['h0'], s['h0'] + 1)):
                    z = cs[:, h:h + 1] - csr_ref[h:h + 1, :]
                    Dm = jnp.where(tri, jnp.exp(jnp.where(tri, z, 0.0)), 0.0)
                    DmT = jnp.where(triT, jnp.exp(jnp.where(triT, -z, 0.0)), 0.0)
                    dM = s['dM2'][:, i * L:(i + 1) * L]
                    dG = dG + dM * Dm
                    qs.append((dM * (Gm * Dm)).astype(BF16))
                    mts.append((GmT * DmT).astype(BF16))
                s['dXm'] = _dot(jnp.concatenate(mts, axis=1), halves(s['dy2'], 0).astype(BF16), _NN)
            Wg = _dot(triu_b, jnp.concatenate(qs, axis=1), _NN)
            place = (wc == g * heads_per_group + wl // L).astype(BF16)
            da_q = da_q + _dot(jnp.where(below_diag, Wg, 0.0).astype(BF16), place, _NN)
            dBg = jnp.zeros((L, N), F32)
            dCg = jnp.zeros((L, N), F32)
            des, ddss, dxxs = [], [], []
            for s in st:
                h0, sl, x2, dy2, X2 = s['h0'], s['sl'], s['x2'], s['dy2'], s['X2']
                dR2b = (s['e2'] * dy2).astype(BF16)
                dCg = dCg + _dot(dR2b, s['H2b'], _NN)
                dHr = _dot(dR2b, Cg, _TN)
                dBg = dBg + _dot((X2 * s['ds2']).astype(BF16), s['dHnb'], _NN)
                dX2 = s['ds2'] * s['dXd'] + s['dXm']
                des.append(dy2 * s['R2'])
                ddss.append(s['dXd'] * X2)
                dxxs.append(dX2 * x2)
                prod = s['dHn'] * s['H2']
                for i, h in enumerate((h0, h0 + 1)):
                    rows = jnp.sum(prod[i * P:(i + 1) * P], axis=0, keepdims=True)
                    dcd_acc = dcd_acc + jnp.sum(rows, axis=1, keepdims=True) * (lane1 == h).astype(F32)
                dd_acc = dd_acc + head_sum(jnp.sum(dy2 * x2, axis=0, keepdims=True), h0)
                dsk2 = jnp.where(lane1 < P, dsk[:, h0:h0 + 1], dsk[:, h0 + 1:h0 + 2])
                dx_ref[:, sl] = dX2 * s['dt2'] + dy2 * dsk2
                dstate[sl, :] = jnp.where(row_lo, cd[:, h0:h0 + 1], cd[:, h0 + 1:h0 + 2]) * s['dHn'] + dHr
            stack = jnp.concatenate([jnp.concatenate(v, axis=1) for v in (des, ddss, dxxs)], axis=0).astype(BF16)
            sums = sums + _dot(stack, (gc == g * heads_per_group + gl // P).astype(BF16), _NN)
            dGb = dG.astype(BF16)
            dx_ref[:, bsl] = dBg + _dot(dGb, Cg, _TN)
            dx_ref[:, csl] = dCg + _dot(dGb, Bg, _NN)
        t1 = sums[L:2 * L] * ds
        tail = jnp.sum(t1, axis=0, keepdims=True) + dcd_acc * cd
        dcs = sums[0:L] * e - t1 + jnp.where(last_row, tail, 0.0)
        da_ref[...] = jnp.dot(triu, dcs, precision=HIGHEST, preferred_element_type=F32) + da_q
        ddtx_ref[...] = sums[2 * L:3 * L]
        dd_ref[...] += dd_acc

    vec = pl.BlockSpec((1, LANES), lambda i: (0, 0))
    rev = lambda i: (nc - 1 - i, 0)
    return pl.pallas_call(
        body, name=name,
        out_shape=(_sds((T, XBC), F32), _sds((T, LANES), F32), _sds((T, LANES), F32), _sds((1, LANES), F32)),
        grid=(nc,),
        in_specs=[pl.BlockSpec((L, XBC), rev), pl.BlockSpec((L, LANES), rev),
                  pl.BlockSpec((HR, L), lambda i: (0, nc - 1 - i)), vec, pl.BlockSpec((HR, 1), lambda i: (0, 0)), vec,
                  pl.BlockSpec((1, SX, N), lambda i: (nc - 1 - i, 0, 0)), pl.BlockSpec((L, SX), rev)],
        out_specs=(pl.BlockSpec((L, XBC), rev), pl.BlockSpec((L, LANES), rev), pl.BlockSpec((L, LANES), rev), vec),
        scratch_shapes=[pltpu.VMEM((SX, N), F32), pltpu.VMEM((L, LANES), F32), pltpu.VMEM((HR, L), F32)],
        compiler_params=_cp("arbitrary"))(xbc, dt, dtT, a_col, a_row, d_col, hst, dy)


def _rms_fwd(y, u, zoff, w, *, name):
    T, SX = y.shape
    tt = _tile(T, 256)
    gs = SX // SSM_GROUPS
    zb = zoff // SX

    def body(y_ref, z_ref, w_ref, o_ref):
        for g in range(SSM_GROUPS):
            sl = slice(g * gs, (g + 1) * gs)
            z = _f32(z_ref[:, sl])
            yg = _f32(y_ref[:, sl]) * (z * _sigmoid(z))
            rstd = lax.rsqrt(jnp.mean(yg * yg, axis=-1, keepdims=True) + RMS_EPS)
            o_ref[:, sl] = (yg * rstd * w_ref[:, sl]).astype(BF16)

    row = pl.BlockSpec((tt, SX), lambda i: (i, 0))
    return pl.pallas_call(body, name=name, out_shape=_sds((T, SX), BF16), grid=(T // tt,),
                          in_specs=[row, pl.BlockSpec((tt, SX), lambda i: (i, zb)), pl.BlockSpec((1, SX), lambda i: (0, 0))],
                          out_specs=row, compiler_params=_cp("parallel"))(y, u, w)


def _rms_bwd(y, u, zoff, w, dyn, *, name):
    T, SX = y.shape
    tt = _tile(T, 256)
    gs = SX // SSM_GROUPS
    zb = zoff // SX

    def body(y_ref, z_ref, w_ref, d_ref, dy_ref, dz_ref, dw_ref):
        i = pl.program_id(0)

        @pl.when(i == 0)
        def _():
            dw_ref[...] = jnp.zeros_like(dw_ref)

        for g in range(SSM_GROUPS):
            sl = slice(g * gs, (g + 1) * gs)
            z, yv, d = _f32(z_ref[:, sl]), _f32(y_ref[:, sl]), _f32(d_ref[:, sl])
            sz, dsz = _silu_and_grad(z)
            yg = yv * sz
            rstd = lax.rsqrt(jnp.mean(yg * yg, axis=-1, keepdims=True) + RMS_EPS)
            t = yg * rstd
            dw_ref[:, sl] += jnp.sum(d * t, axis=0, keepdims=True)
            dt_ = d * w_ref[:, sl]
            dyg = rstd * (dt_ - t * jnp.mean(dt_ * t, axis=-1, keepdims=True))
            dy_ref[:, sl] = (dyg * sz).astype(BF16)
            dz_ref[:, sl] = (dyg * yv * dsz).astype(BF16)

    row = pl.BlockSpec((tt, SX), lambda i: (i, 0))
    vec = pl.BlockSpec((1, SX), lambda i: (0, 0))
    return pl.pallas_call(body, name=name, out_shape=(_sds((T, SX), BF16), _sds((T, SX), BF16), _sds((1, SX), F32)),
                          grid=(T // tt,), in_specs=[row, pl.BlockSpec((tt, SX), lambda i: (i, zb)), vec, row],
                          out_specs=(row, row, vec), compiler_params=_cp("arbitrary"))(y, u, w, dyn)


def _gate_fwd(u, goff, ya, yb, *, name):
    T, D = ya.shape
    tt = _tile(T, 512)
    gb = goff // D

    def body(ga_ref, gb_ref, ya_ref, yb_ref, o_ref):
        o_ref[...] = (_sigmoid(_f32(ga_ref[...])) * _f32(ya_ref[...]) + _sigmoid(_f32(gb_ref[...])) * _f32(yb_ref[...])).astype(BF16)

    row = pl.BlockSpec((tt, D), lambda i: (i, 0))
    return pl.pallas_call(body, name=name, out_shape=_sds((T, D), BF16), grid=(T // tt,),
                          in_specs=[pl.BlockSpec((tt, D), lambda i: (i, gb)), pl.BlockSpec((tt, D), lambda i: (i, gb + 1)), row, row],
                          out_specs=row, compiler_params=_cp("parallel"))(u, u, ya, yb)


def _gate_bwd(u, goff, ya, yb, dm, *, name):
    T, D = ya.shape
    tt = _tile(T, 512)
    gb = goff // D

    def body(ga_ref, gb_ref, ya_ref, yb_ref, dm_ref, dya_ref, dyb_ref, dg_ref):
        d = _f32(dm_ref[...])
        sa, sb = _sigmoid(_f32(ga_ref[...])), _sigmoid(_f32(gb_ref[...]))
        dya_ref[...] = (d * sa).astype(BF16)
        dyb_ref[...] = (d * sb).astype(BF16)
        dg_ref[...] = jnp.concatenate([d * _f32(ya_ref[...]) * sa * (1.0 - sa), d * _f32(yb_ref[...]) * sb * (1.0 - sb)], axis=1).astype(BF16)

    row = pl.BlockSpec((tt, D), lambda i: (i, 0))
    return pl.pallas_call(body, name=name, out_shape=(_sds((T, D), BF16), _sds((T, D), BF16), _sds((T, 2 * D), BF16)),
                          grid=(T // tt,),
                          in_specs=[pl.BlockSpec((tt, D), lambda i: (i, gb)), pl.BlockSpec((tt, D), lambda i: (i, gb + 1)), row, row, row],
                          out_specs=(row, row, pl.BlockSpec((tt, 2 * D), lambda i: (i, 0))),
                          compiler_params=_cp("parallel"))(u, u, ya, yb, dm)


def _adamw_math(w, gg, m, v):
    nm = ADAM_B1 * m + (1.0 - ADAM_B1) * gg
    nv = ADAM_B2 * v + (1.0 - ADAM_B2) * (gg * gg)
    m_hat = nm / (1.0 - ADAM_B1 ** ADAM_STEP)
    v_hat = nv / (1.0 - ADAM_B2 ** ADAM_STEP)
    return -ADAM_LR * (m_hat / (jnp.sqrt(v_hat) + ADAM_EPS) + ADAM_WD * w), nm, nv


def _adamw(w, g, m, v, *, name):
    R, C = w.shape
    tr = _pick(R, 256, 8)

    def body(w_ref, g_ref, m_ref, v_ref, d_ref, nm_ref, nv_ref):
        d_ref[...], nm_ref[...], nv_ref[...] = _adamw_math(w_ref[...], g_ref[...], m_ref[...], v_ref[...])

    blk = pl.BlockSpec((tr, C), lambda i: (i, 0))
    out = _sds((R, C), F32)
    return pl.pallas_call(body, name=name, out_shape=(out, out, out), grid=(R // tr,), in_specs=[blk] * 4,
                          out_specs=(blk,) * 3, compiler_params=_cp("parallel"))(w, g, m, v)


def _sum_adamw(parts, w, m, v, *, name):
    depth, C = w.shape[0], w.shape[-1]
    R = w[0].size // C
    tr = _pick(R, 256, 8)

    def body(*refs):
        p_refs = refs[:depth]
        w_ref, m_ref, v_ref, g_ref, d_ref, nm_ref, nv_ref = refs[depth:]
        for l in range(depth):
            @pl.when(pl.program_id(0) == l)
            def _(l=l):
                gg = p_refs[l][0].astype(F32)
                for k in range(1, N_DEV):
                    gg = gg + p_refs[l][k].astype(F32)
                g_ref[...] = gg
                d_ref[...], nm_ref[...], nv_ref[...] = _adamw_math(w_ref[...], gg, m_ref[...], v_ref[...])

    p_specs = [pl.BlockSpec((N_DEV, tr, C), lambda l, i, ll=ll: (0, jnp.where(l == ll, i, 0), 0)) for ll in range(depth)]
    blk = pl.BlockSpec((None, tr, C), lambda l, i: (l, i, 0))
    out = _sds((depth, R, C), F32)
    res = pl.pallas_call(body, name=name, out_shape=(out,) * 4, grid=(depth, R // tr),
                         in_specs=p_specs + [blk, blk, blk], out_specs=(blk,) * 4,
                         compiler_params=_cp("parallel", "parallel"))(
        *[x.reshape(N_DEV, R, C) for x in parts], w.reshape(depth, R, C), m.reshape(depth, R, C), v.reshape(depth, R, C))
    return tuple(r.reshape(w.shape) for r in res)


def _sum_slots(x, *, name):
    n, R, C = x.shape
    tr = _tile(R, 512)

    def body(x_ref, o_ref):
        acc = x_ref[0].astype(F32)
        for k in range(1, n):
            acc = acc + x_ref[k].astype(F32)
        o_ref[...] = acc

    return pl.pallas_call(body, name=name, out_shape=_sds((R, C), F32), grid=(R // tr,),
                          in_specs=[pl.BlockSpec((n, tr, C), lambda i: (0, i, 0))],
                          out_specs=pl.BlockSpec((tr, C), lambda i: (i, 0)), compiler_params=_cp("parallel"))(x)


def _exchange(xs, *, scatter, name):
    n = len(xs)

    def body(*refs):
        x_refs, o_refs = refs[:n], refs[n:2 * n]
        send_sems, recv_sems, local_sems = refs[2 * n:]
        mx, my, mc = lax.axis_index("x"), lax.axis_index("y"), lax.axis_index("c")
        me = 4 * mx + 2 * my + mc

        def src(i, d):
            return x_refs[i].at[d] if scatter else x_refs[i]

        locals_ = [pltpu.make_async_copy(src(i, me), o_refs[i].at[me], local_sems.at[i]) for i in range(n)]
        for cp in locals_:
            cp.start()
        sends, recvs = [], []
        for k in range(1, N_DEV):
            px = 1 - mx if k & 4 else mx
            py = 1 - my if k & 2 else my
            pc = 1 - mc if k & 1 else mc
            peer = 4 * px + 2 * py + pc
            for i in range(n):
                common = dict(send_sem=send_sems.at[k - 1, i], recv_sem=recv_sems.at[k - 1, i],
                              device_id=(px, py, pc), device_id_type=pl.DeviceIdType.MESH)
                sends.append(pltpu.make_async_remote_copy(src_ref=src(i, peer), dst_ref=o_refs[i].at[me], **common))
                recvs.append(pltpu.make_async_remote_copy(src_ref=src(i, peer), dst_ref=o_refs[i].at[peer], **common))
        for cp in sends:
            cp.start()
        for cp in recvs:
            cp.wait_recv()
        for cp in sends:
            cp.wait_send()
        for cp in locals_:
            cp.wait()

    any_spec = pl.BlockSpec(memory_space=pl.ANY)
    out_shape = tuple(_sds(x.shape if scatter else (N_DEV,) + x.shape, x.dtype) for x in xs)
    return pl.pallas_call(
        body, name=name, out_shape=out_shape, in_specs=[any_spec] * n, out_specs=(any_spec,) * n,
        scratch_shapes=[pltpu.SemaphoreType.DMA((N_DEV - 1, n)), pltpu.SemaphoreType.DMA((N_DEV - 1, n)),
                        pltpu.SemaphoreType.DMA((n,))])(*xs)


def _gather_two_level(xs, *, name):
    n = len(xs)

    def body(*refs):
        x_refs, o_refs = refs[:n], refs[n:2 * n]
        send_sems, recv_sems, local_sems = refs[2 * n:]
        x, y, c = lax.axis_index("x"), lax.axis_index("y"), lax.axis_index("c")
        me, sibling = (x, y, c), (x, y, 1 - c)
        chips = [(1 - x, y), (x, 1 - y), (1 - x, 1 - y)]

        def copy(i, k, block, to, src=None):
            rows = o_refs[i].at[4 * block[0] + 2 * block[1] + block[2]]
            return pltpu.make_async_remote_copy(
                src_ref=rows if src is None else src, dst_ref=rows, send_sem=send_sems.at[k * n + i],
                recv_sem=recv_sems.at[k * n + i], device_id=to, device_id_type=pl.DeviceIdType.MESH)

        mine = [pltpu.make_async_copy(x_refs[i], o_refs[i].at[4 * x + 2 * y + c], local_sems.at[i]) for i in range(n)]
        for cp in mine:
            cp.start()
        first = [copy(i, 0, me, sibling, src=x_refs[i]) for i in range(n)]
        first += [copy(i, 1 + j, me, (*chip, c), src=x_refs[i]) for j, chip in enumerate(chips) for i in range(n)]
        for cp in first:
            cp.start()
        passed = []
        for j, chip in enumerate(chips):
            for i in range(n):
                copy(i, 1 + j, (*chip, c), me).wait_recv()
                passed.append(copy(i, 4 + j, (*chip, c), sibling))
                passed[-1].start()
        for i in range(n):
            copy(i, 0, sibling, me).wait_recv()
        for j, chip in enumerate(chips):
            for i in range(n):
                copy(i, 4 + j, (*chip, 1 - c), me).wait_recv()
        for cp in first + passed:
            cp.wait_send()
        for cp in mine:
            cp.wait()

    any_spec = pl.BlockSpec(memory_space=pl.ANY)
    return pl.pallas_call(
        body, name=name, out_shape=tuple(_sds((N_DEV,) + x.shape, x.dtype) for x in xs), in_specs=[any_spec] * n,
        out_specs=(any_spec,) * n,
        scratch_shapes=[pltpu.SemaphoreType.DMA(((N_DEV - 1) * n,)), pltpu.SemaphoreType.DMA(((N_DEV - 1) * n,)),
                        pltpu.SemaphoreType.DMA((n,))])(*xs)


def _to_rows(flat, row_mult):
    n = flat.shape[-1]
    per = row_mult * LANES
    pad = (-n) % per
    flat = jnp.pad(flat, [(0, 0)] * (flat.ndim - 1) + [(0, pad)])
    return flat.reshape(flat.shape[:-1] + ((n + pad) // LANES, LANES))


def _peers():
    mx, my, mc = lax.axis_index("x"), lax.axis_index("y"), lax.axis_index("c")
    out = []
    for k in range(1, N_DEV):
        px = 1 - mx if k & 4 else mx
        py = 1 - my if k & 2 else my
        pc = 1 - mc if k & 1 else mc
        out.append(((px, py, pc), 4 * px + 2 * py + pc))
    return 4 * mx + 2 * my + mc, out


_HBM = pl.BlockSpec(memory_space=pltpu.HBM)
_SEM = pl.BlockSpec(memory_space=pltpu.SEMAPHORE)


def _exchange_start(xs, *, scatter, name):
    n = len(xs)

    def body(*refs):
        x_refs, land_refs, send_sems, recv_sems, token = refs[:n], refs[n:2 * n], refs[2 * n], refs[2 * n + 1], refs[-1]
        me, peers = _peers()
        for k, (dev, peer) in enumerate(peers):
            for i in range(n):
                pltpu.make_async_remote_copy(
                    src_ref=x_refs[i].at[peer] if scatter else x_refs[i], dst_ref=land_refs[i].at[me],
                    send_sem=send_sems.at[k * n + i], recv_sem=recv_sems.at[k * n + i],
                    device_id=dev, device_id_type=pl.DeviceIdType.MESH).start()
        token[...] = jnp.zeros_like(token)

    land_shapes = [x.shape if scatter else (N_DEV,) + x.shape for x in xs]
    lands = [pltpu.with_memory_space_constraint(lax.empty(s, x.dtype), pltpu.HBM) for s, x in zip(land_shapes, xs)]
    srcs = [pltpu.with_memory_space_constraint(x, pltpu.HBM) for x in xs]
    out = pl.pallas_call(
        body, name=name,
        out_shape=(pltpu.SemaphoreType.DMA(((N_DEV - 1) * n,)), pltpu.SemaphoreType.DMA(((N_DEV - 1) * n,)),
                   *[pltpu.HBM(x.shape, x.dtype) for x in xs], *[pltpu.HBM(s, x.dtype) for s, x in zip(land_shapes, xs)],
                   _sds((8, LANES), F32)),
        in_specs=(_HBM,) * (2 * n), out_specs=(_SEM, _SEM) + (_HBM,) * (2 * n) + (pl.BlockSpec(memory_space=pltpu.VMEM),),
        input_output_aliases={i: 2 + i for i in range(2 * n)},
        compiler_params=pltpu.CompilerParams(has_side_effects=pltpu.SideEffectType.DATAFLOW_SIDE_EFFECTING))(*srcs, *lands)
    return (out[0], out[1], list(out[2:2 + n]), list(out[2 + n:2 + 2 * n])), out[-1]


def _exchange_wait(handle, after, *, scatter, name):
    send_sems, recv_sems, srcs, lands = handle
    n = len(srcs)

    def body(*refs):
        x_refs, land_refs, send_sems, recv_sems = refs[:n], refs[n:2 * n], refs[2 * n], refs[2 * n + 1]
        me, peers = _peers()
        for k, (dev, peer) in enumerate(peers):
            for i in range(n):
                cp = pltpu.make_async_remote_copy(
                    src_ref=x_refs[i].at[peer] if scatter else x_refs[i], dst_ref=land_refs[i].at[peer],
                    send_sem=send_sems.at[k * n + i], recv_sem=recv_sems.at[k * n + i],
                    device_id=dev, device_id_type=pl.DeviceIdType.MESH)
                cp.wait_send()
                cp.wait_recv()

    out = pl.pallas_call(
        body, name=name, out_shape=tuple(pltpu.HBM(a.shape, a.dtype) for a in srcs + lands),
        in_specs=(_HBM,) * (2 * n) + (_SEM, _SEM, pl.BlockSpec(memory_space=pl.ANY)), out_specs=(_HBM,) * (2 * n),
        input_output_aliases={i: i for i in range(2 * n)},
        compiler_params=pltpu.CompilerParams(has_side_effects=pltpu.SideEffectType.DATAFLOW_SIDE_EFFECTING))(
        *srcs, *lands, send_sems, recv_sems, after)
    return _with_own_slot(list(out[n:]), list(out[:n]), scatter=scatter)


def _with_own_slot(lands, xs, *, scatter):
    me = 4 * lax.axis_index("x") + 2 * lax.axis_index("y") + lax.axis_index("c")
    out = []
    for land, x in zip(lands, xs):
        own = lax.dynamic_index_in_dim(x, me, 0, keepdims=True) if scatter else x[None]
        out.append(lax.dynamic_update_index_in_dim(land, own, me, 0))
    return out


def _wire_shards(p, l, names):
    return [p[k][l].astype(WIRE) if SHARDED[k][1] else p[k][l] for k in names]


def _full_weights(got, p, names):
    full = {}
    for k, g in zip(names, got):
        axis, shape = SHARDED[k][0] - 1, p[k].shape[1:]
        g = jnp.moveaxis(g, 0, axis)
        full[k] = g.reshape(shape[:axis] + (N_DEV * shape[axis],) + shape[axis + 1:])
    return full


def _grad_parts(grads, p, names):
    parts = []
    for k in names:
        axis, g = SHARDED[k][0] - 1, grads[k]
        g = g.reshape(g.shape[:axis] + (N_DEV, p[k].shape[1:][axis]) + g.shape[axis + 1:])
        parts.append(jnp.moveaxis(g, axis, 0).astype(WIRE))
    return parts


def _allreduce_small(vals):
    metas = [(k, v.shape, v.size) for k, v in vals.items()]
    flat = jnp.concatenate([v.reshape(-1) for v in vals.values()])
    got, = _exchange([_to_rows(flat, 8)], scatter=False, name="gather_small_grads")
    summed = _sum_slots(got, name="sum_small_grads").reshape(-1)
    out, off = {}, 0
    for k, shape, n in metas:
        out[k] = summed[off:off + n].reshape(shape)
        off += n
    return out


def _row(v):
    return v.reshape(1, -1).astype(F32)


def _lanes(v):
    return jnp.pad(v.astype(F32), (0, LANES - v.shape[0])).reshape(1, LANES)


EARLY = ('w_in', 'conv_dw_w', 'ssm_conv_w')
LATE = tuple(k for k in SHARDED if k not in EARLY)
FFN = ('w_ffn_up', 'ffn_dw_w', 'w_ffn_down')
NOT_FFN = tuple(k for k in SHARDED if k not in FFN)


def _layer_weights(p, full, l, dims):
    D, SX, XBC, H, FF = dims
    assert _tile(D, GLU_TILE) == D and _tile(FF, FFN_TILE) == FF
    W = {}
    if 'w_in' in full:
        w_in = full['w_in']
        o_dt = 2 * D + SX + XBC
        w_pieces = [w_in[:, :2 * D], w_in[:, 2 * D:2 * D + SX], w_in[:, 2 * D + SX:o_dt], w_in[:, o_dt + H:]]
        a_head = -jnp.exp(p['ssm_a_log'][l].astype(F32))
        hr = -(-H // 8) * 8
        W.update(
            w_main=jnp.concatenate(w_pieces, axis=1), w_pieces=w_pieces,
            w_dt=jnp.pad(w_in[:, o_dt:o_dt + H], ((0, 0), (0, LANES - H))),
            conv_w=full['conv_dw_w'], conv_b=_row(p['conv_dw_b'][l]), conv_g=_row(p['conv_ln_g'][l]),
            conv_beta=_row(p['conv_ln_b'][l]), ssm_w=full['ssm_conv_w'], ssm_b=_row(p['ssm_conv_b'][l]),
            dt_bias=_lanes(p['ssm_dt_bias'][l]), a_col=_lanes(a_head),
            a_row=jnp.pad(a_head, (0, hr - H)).reshape(hr, 1), d_col=_lanes(p['ssm_d'][l]),
            norm_w=_row(p['ssm_norm_w'][l]), ln1_g=_row(p['ln1_g'][l]), ln1_b=_row(p['ln1_b'][l]),
            ffn_b=_row(p['ffn_dw_b'][l]), ln2_g=_row(p['ln2_g'][l]), ln2_b=_row(p['ln2_b'][l]))
    if 'w_o' in full:
        W.update(w_conv_out=full['w_conv_out'], w_ssm_out=full['w_ssm_out'], w_o=full['w_o'], w_up=full['w_ffn_up'],
                 ffn_w=full['ffn_dw_w'], w_down=full['w_ffn_down'])
    return W


def _layer_fwd(h, hb, W, late, l, dims):
    D, SX, XBC, H, FF = dims
    o_z, o_xbc, o_gate = 2 * D, 2 * D + SX, 2 * D + SX + XBC
    hr = W['a_row'].shape[0]
    t = f"l{l}"
    u = _mm(hb, W['w_main'], mode='nn', out_dtype=BF16, name=t + "_in_proj")
    udt = _mm(hb, W['w_dt'], mode='nn', out_dtype=F32, name=t + "_dt_proj")
    v1, v3 = _conf_fwd(u, W['conv_w'], W['conv_b'], W['conv_g'], W['conv_beta'], D=D, name=t + "_conf_fwd")
    xbc = _conv_act_fwd(u, o_xbc, W['ssm_w'], W['ssm_b'], ffn=False, name=t + "_ssm_conv")
    dt = _dt_fwd(udt, W['dt_bias'], H=H, name=t + "_dt")
    dtT = jnp.pad(dt[:, :H].T, ((0, hr - H), (0, 0)))
    y, hst = _ssd_fwd(xbc, dt, dtT, W['a_col'], W['a_row'], W['d_col'], H=H, name=t + "_ssd_fwd")
    yn = _rms_fwd(y, u, o_z, W['norm_w'], name=t + "_rms_fwd")
    W = {**W, **late(yn)}
    ya = _mm(v3, W['w_conv_out'], mode='nn', out_dtype=BF16, name=t + "_conv_out")
    yb = _mm(yn, W['w_ssm_out'], mode='nn', out_dtype=BF16, name=t + "_ssm_out")
    m = _gate_fwd(u, o_gate, ya, yb, name=t + "_gate_fwd")
    h1, s1, h1b = _mm_res_ln(m, W['w_o'], h, W['ln1_g'], W['ln1_b'], name=t + "_mix_ln1")
    uf = _mm(h1b, W['w_up'], mode='nn', out_dtype=BF16, name=t + "_ffn_up")
    act = _conv_act_fwd(uf, 0, W['ffn_w'], W['ffn_b'], ffn=True, name=t + "_ffn_conv")
    h2, s2, h2b = _mm_res_ln(act, W['w_down'], h1, W['ln2_g'], W['ln2_b'], name=t + "_ffn_down_ln2")
    saved = dict(hb=hb, u=u, udt=udt, v1=v1, v3=v3, ya=ya, xbc=xbc, dt=dt, dtT=dtT, y=y, hst=hst, yn=yn, yb=yb, m=m,
                 h1b=h1b, s1=s1, uf=uf, act=act, s2=s2)
    return h2, h2b, saved, W


def _layer_bwd(dh2, W, S, l, dims, on_ffn_grads=None, on_all_grads=None):
    D, SX, XBC, H, FF = dims
    o_z, o_xbc, o_gate = 2 * D, 2 * D + SX, 2 * D + SX + XBC
    t = f"l{l}"
    g = {}
    ds2, g['ln2_g'], g['ln2_b'] = _ln_bwd(S['s2'], dh2, W['ln2_g'], W['ln2_b'], silu=False, name=t + "_ln2_bwd")
    g['w_down'] = _mm(S['act'], ds2, mode='tn', out_dtype=WIRE, name=t + "_dw_down")
    dact = _mm(ds2, W['w_down'], mode='nt', out_dtype=F32, name=t + "_dact")
    duf, g['ffn_w'], g['ffn_b'] = _conv_act_bwd(S['uf'], 0, W['ffn_w'], W['ffn_b'], dact, ffn=True, name=t + "_ffn_conv_bwd")
    g['w_up'] = _mm(S['h1b'], duf, mode='tn', out_dtype=WIRE, name=t + "_dw_up")
    dh1 = _mm(duf, W['w_up'], mode='nt', out_dtype=F32, res=ds2, res_scale=DN_ALPHA, name=t + "_dh1")
    ln1_g = W['ln1_g'] if on_ffn_grads is None else W['ln1_g'] + on_ffn_grads(g)[0:1, 0:1]
    ds1, g['ln1_g'], g['ln1_b'] = _ln_bwd(S['s1'], dh1, ln1_g, W['ln1_b'], silu=False, name=t + "_ln1_bwd")
    g['w_o'] = _mm(S['m'], ds1, mode='tn', out_dtype=WIRE, name=t + "_dw_o")
    dm = _mm(ds1, W['w_o'], mode='nt', out_dtype=BF16, name=t + "_dm")
    dya, dyb, dgate = _gate_bwd(S['u'], o_gate, S['ya'], S['yb'], dm, name=t + "_gate_bwd")
    g['w_conv_out'] = _mm(S['v3'], dya, mode='tn', out_dtype=WIRE, name=t + "_dw_conv_out")
    dv3 = _mm(dya, W['w_conv_out'], mode='nt', out_dtype=BF16, name=t + "_dv3")
    dv1, g['conv_g'], g['conv_beta'] = _ln_bwd(S['v1'], dv3, W['conv_g'], W['conv_beta'], silu=True, name=t + "_conv_ln_bwd")
    dglu, g['conv_w'], g['conv_b'] = _glu_conv_bwd(dv1, S['u'], W['conv_w'], name=t + "_conf_conv_bwd")
    g['w_ssm_out'] = _mm(S['yn'], dyb, mode='tn', out_dtype=WIRE, name=t + "_dw_ssm_out")
    dyn = _mm(dyb, W['w_ssm_out'], mode='nt', out_dtype=BF16, name=t + "_dyn")
    dy, dz, g['norm_w'] = _rms_bwd(S['y'], S['u'], o_z, W['norm_w'], dyn, name=t + "_rms_bwd")
    dxbc_c, da, ddtx, g['d'] = _ssd_bwd(S['xbc'], S['dt'], S['dtT'], W['a_col'], W['a_row'], W['d_col'], S['hst'], dy,
                                        H=H, name=t + "_ssd_bwd")
    ddt_raw, g['dt_bias'], g['a_log'] = _dt_bwd(da, ddtx, S['dt'], S['udt'], W['dt_bias'], W['a_col'], H=H, name=t + "_dt_bwd")
    dxbc, g['ssm_w'], g['ssm_b'] = _conv_act_bwd(S['u'], o_xbc, W['ssm_w'], W['ssm_b'], dxbc_c, ffn=False, name=t + "_ssm_conv_bwd")
    du = [dglu, dz, dxbc, dgate]
    g['w_pieces'] = [_mm(S['hb'], d, mode='tn', out_dtype=WIRE, name=f"{t}_dw_in{i}") for i, d in enumerate(du)]
    g['w_dt'] = _mm(S['hb'], ddt_raw, mode='tn', out_dtype=WIRE, name=t + "_dw_dt")
    w_dt = W['w_dt'] if on_all_grads is None else W['w_dt'] + on_all_grads(g)[0:1, 0:1].astype(W['w_dt'].dtype)
    dh = _mm_nt_cat(du + [ddt_raw], W['w_pieces'] + [w_dt], ds1, DN_ALPHA, name=t + "_dh_in")
    return dh, g


def _layer_grads_to_params(g, dims):
    D, SX, XBC, H, FF = dims
    glu, dz, dxbc, dgate = g['w_pieces']
    w_in = jnp.concatenate([glu, dz, dxbc, g['w_dt'][:, :H], dgate], axis=1)
    return dict(
        w_in=w_in, conv_dw_w=g['conv_w'], conv_dw_b=g['conv_b'][0], conv_ln_g=g['conv_g'][0], conv_ln_b=g['conv_beta'][0],
        w_conv_out=g['w_conv_out'], ssm_conv_w=g['ssm_w'], ssm_conv_b=g['ssm_b'][0],
        ssm_dt_bias=g['dt_bias'][0, :H], ssm_a_log=g['a_log'][0, :H], ssm_d=g['d'][0, :H], ssm_norm_w=g['norm_w'][0],
        w_ssm_out=g['w_ssm_out'], w_o=g['w_o'], ln1_g=g['ln1_g'][0], ln1_b=g['ln1_b'][0],
        w_ffn_up=g['w_up'], ffn_dw_w=g['ffn_w'], ffn_dw_b=g['ffn_b'][0], w_ffn_down=g['w_down'], ln2_g=g['ln2_g'][0], ln2_b=g['ln2_b'][0])


def kernel(x, ln_in_g, ln_in_b, w_in, conv_dw_w, conv_dw_b, conv_ln_g, conv_ln_b, w_conv_out, ssm_conv_w, ssm_conv_b, ssm_dt_bias, ssm_a_log, ssm_d, ssm_norm_w, w_ssm_out, w_o, ln1_g, ln1_b, w_ffn_up, ffn_dw_w, ffn_dw_b, w_ffn_down, ln2_g, ln2_b, loss_target, m_ln_in_g, m_ln_in_b, m_w_in, m_conv_dw_w, m_conv_dw_b, m_conv_ln_g, m_conv_ln_b, m_w_conv_out, m_ssm_conv_w, m_ssm_conv_b, m_ssm_dt_bias, m_ssm_a_log, m_ssm_d, m_ssm_norm_w, m_w_ssm_out, m_w_o, m_ln1_g, m_ln1_b, m_w_ffn_up, m_ffn_dw_w, m_ffn_dw_b, m_w_ffn_down, m_ln2_g, m_ln2_b, v_ln_in_g, v_ln_in_b, v_w_in, v_conv_dw_w, v_conv_dw_b, v_conv_ln_g, v_conv_ln_b, v_w_conv_out, v_ssm_conv_w, v_ssm_conv_b, v_ssm_dt_bias, v_ssm_a_log, v_ssm_d, v_ssm_norm_w, v_w_ssm_out, v_w_o, v_ln1_g, v_ln1_b, v_w_ffn_up, v_ffn_dw_w, v_ffn_dw_b, v_w_ffn_down, v_ln2_g, v_ln2_b):
    weights = (ln_in_g, ln_in_b, w_in, conv_dw_w, conv_dw_b, conv_ln_g, conv_ln_b, w_conv_out, ssm_conv_w, ssm_conv_b,
               ssm_dt_bias, ssm_a_log, ssm_d, ssm_norm_w, w_ssm_out, w_o, ln1_g, ln1_b, w_ffn_up, ffn_dw_w, ffn_dw_b,
               w_ffn_down, ln2_g, ln2_b)
    moments_m = (m_ln_in_g, m_ln_in_b, m_w_in, m_conv_dw_w, m_conv_dw_b, m_conv_ln_g, m_conv_ln_b, m_w_conv_out,
                 m_ssm_conv_w, m_ssm_conv_b, m_ssm_dt_bias, m_ssm_a_log, m_ssm_d, m_ssm_norm_w, m_w_ssm_out, m_w_o,
                 m_ln1_g, m_ln1_b, m_w_ffn_up, m_ffn_dw_w, m_ffn_dw_b, m_w_ffn_down, m_ln2_g, m_ln2_b)
    moments_v = (v_ln_in_g, v_ln_in_b, v_w_in, v_conv_dw_w, v_conv_dw_b, v_conv_ln_g, v_conv_ln_b, v_w_conv_out,
                 v_ssm_conv_w, v_ssm_conv_b, v_ssm_dt_bias, v_ssm_a_log, v_ssm_d, v_ssm_norm_w, v_w_ssm_out, v_w_o,
                 v_ln1_g, v_ln1_b, v_w_ffn_up, v_ffn_dw_w, v_ffn_dw_b, v_w_ffn_down, v_ln2_g, v_ln2_b)
    p = dict(zip(PARAMS, weights))
    pm = dict(zip(PARAMS, moments_m))
    pv = dict(zip(PARAMS, moments_v))

    T, D = x.shape[1], x.shape[2]
    SX = w_ssm_out.shape[1] * N_DEV
    XBC = ssm_conv_b.shape[-1]
    H = ssm_d.shape[-1]
    FF = ffn_dw_b.shape[-1] // 2
    dims = (D, SX, XBC, H, FF)
    depth = w_in.shape[0]

    ALL = tuple(SHARDED)
    got = _gather_two_level(_wire_shards(p, 0, EARLY), name="gather_l0_first")
    rest_shards, got = lax.optimization_barrier((_wire_shards(p, 0, LATE), got))
    rest_handle, tok = _exchange_start(rest_shards, scatter=False, name="gather_l0_rest_start")
    first = _layer_weights(p, _full_weights(got, p, EARLY), 0, dims)
    in_flight = {}

    def start_next_gather(l, behind):
        shards, _ = lax.optimization_barrier((_wire_shards(p, l, ALL), behind))
        in_flight[l], t = _exchange_start(shards, scatter=False, name=f"gather_l{l}_start")
        in_flight[l] = (in_flight[l], shards)
        return t

    def late_weights(l):
        def late(x):
            if l > 0:
                return {}
            got = _exchange_wait(rest_handle, x, scatter=False, name="gather_l0_rest_wait")
            W = _layer_weights(p, _full_weights(got, p, LATE), 0, dims)
            if depth > 1:
                t = start_next_gather(1, got)
                W['w_conv_out'] = W['w_conv_out'] + t[0:1, 0:1].astype(W['w_conv_out'].dtype)
            return W
        return late

    xs = x.reshape(T, D)
    h, hb = _ln_fwd(xs, _row(ln_in_g) + tok[0:1, 0:1], _row(ln_in_b), name="ln_in_fwd")
    saved, layers = [], []
    for l in range(depth):
        if l > 0:
            handle, shards = in_flight.pop(l)
            got = _exchange_wait(handle, h, scatter=False, name=f"gather_l{l}_wait")
            first = _layer_weights(p, _full_weights(got, p, ALL), l, dims)
            if l + 1 < depth:
                t = start_next_gather(l + 1, got)
                first['w_dt'] = first['w_dt'] + t[0:1, 0:1].astype(first['w_dt'].dtype)
        h, hb, s, W = _layer_fwd(h, hb, first, late_weights(l), l, dims)
        saved.append(s)
        layers.append(W)
    loss_part, dh = _loss_fwd_bwd(h, loss_target.reshape(T, D), name="loss")

    layer_grads = [None] * depth
    flying, arrived = [], {}

    def wait_flying(after):
        while flying:
            l_, names, handle, parts = flying.pop(0)
            lands = _exchange_wait(handle, after, scatter=True, name=f"scatter_l{l_}_{names[0]}_wait")
            arrived.update({(l_, k): a for k, a in zip(names, lands)})
            after = lands[0]
        return after

    def start_scatter(l, names, grads_now, after):
        parts, _ = lax.optimization_barrier((_grad_parts(grads_now, p, names), wait_flying(after)))
        handle, t = _exchange_start(parts, scatter=True, name=f"scatter_l{l}_{names[0]}_start")
        flying.append((l, names, handle, parts))
        return t

    def on_ffn_grads(g):
        return start_scatter(0, FFN, dict(w_ffn_up=g['w_up'], ffn_dw_w=g['ffn_w'], w_ffn_down=g['w_down']), g['w_up'])

    def on_all_grads(g):
        layer_grads[0] = _layer_grads_to_params(g, dims)
        return start_scatter(0, NOT_FFN, layer_grads[0], g['w_dt'])

    tok = jnp.zeros((8, LANES), F32)
    for l in reversed(range(depth)):
        W = dict(layers[l], ln2_g=layers[l]['ln2_g'] + tok[0:1, 0:1])
        if l > 0:
            dh, g = _layer_bwd(dh, W, saved[l], l, dims)
            layer_grads[l] = _layer_grads_to_params(g, dims)
            tok = start_scatter(l, ALL, layer_grads[l], dh)
        else:
            dh, g = _layer_bwd(dh, W, saved[l], l, dims, on_ffn_grads, on_all_grads)
    grad_x, dg_in, db_in = _ln_bwd(xs, dh, _row(ln_in_g), _row(ln_in_b), silu=False, name="ln_in_bwd")
    wait_flying(grad_x)

    small = {k: jnp.stack([layer_grads[l][k] for l in range(depth)]) for k in layer_grads[0] if k not in SHARDED}
    small['ln_in_g'], small['ln_in_b'], small['loss'] = dg_in[0], db_in[0], loss_part
    grads = _allreduce_small(small)
    loss = grads.pop('loss').reshape(())

    delta, new_m, new_v = {}, {}, {}
    for k in SHARDED:
        grads[k], delta[k], new_m[k], new_v[k] = _sum_adamw([arrived[(l, k)] for l in range(depth)], p[k], pm[k], pv[k],
                                                            name="sum_adamw_" + k)
    rest = [k for k in PARAMS if k not in SHARDED]
    flat = lambda d: _to_rows(jnp.concatenate([d[k].reshape(-1) for k in rest]), 8)
    d_, m_, v_ = _adamw(flat(p), flat(grads), flat(pm), flat(pv), name="adamw_small")
    off = 0
    for k in rest:
        n, shp = p[k].size, p[k].shape
        delta[k] = d_.reshape(-1)[off:off + n].reshape(shp)
        new_m[k] = m_.reshape(-1)[off:off + n].reshape(shp)
        new_v[k] = v_.reshape(-1)[off:off + n].reshape(shp)
        off += n

    return (loss, grad_x.reshape(x.shape), *[grads[k] for k in PARAMS], *[delta[k] for k in PARAMS],
            *[new_m[k] for k in PARAMS], *[new_v[k] for k in PARAMS])
```

```python
import math

import jax
import jax.numpy as jnp
from jax import lax
from jax.experimental import pallas as pl
from jax.experimental.pallas import tpu as pltpu

F32 = jnp.float32
BF16 = jnp.bfloat16
WIRE = jnp.bfloat16
HIGHEST = lax.Precision.HIGHEST

DEPTH = 2
SSM_STATE = 128
SSM_CHUNK = 128
SSM_GROUPS = 4
SSM_HEAD_DIM = 64
DN_ALPHA = (2 * DEPTH) ** 0.25
LN_EPS = 1e-5
RMS_EPS = 1e-5
ADAM_LR = 0.001
ADAM_B1 = 0.9
ADAM_B2 = 0.999
ADAM_EPS = 1e-08
ADAM_WD = 0.01
ADAM_STEP = 10

N_DEV = 8
LANES = 128
MXU_COLS = 256
VMEM_LIMIT = 48 * 1024 * 1024
WHOLE_K_VMEM = 40 * 1024 * 1024
GLU_TILE = 1024
FFN_TILE = 2816
CONV_ROWS = 256

PARAMS = ['ln_in_g', 'ln_in_b', 'w_in', 'conv_dw_w', 'conv_dw_b', 'conv_ln_g', 'conv_ln_b', 'w_conv_out',
          'ssm_conv_w', 'ssm_conv_b', 'ssm_dt_bias', 'ssm_a_log', 'ssm_d', 'ssm_norm_w', 'w_ssm_out', 'w_o',
          'ln1_g', 'ln1_b', 'w_ffn_up', 'ffn_dw_w', 'ffn_dw_b', 'w_ffn_down', 'ln2_g', 'ln2_b']
SHARDED = {'w_in': (2, True), 'conv_dw_w': (2, False), 'w_conv_out': (1, True), 'ssm_conv_w': (2, False),
           'w_ssm_out': (1, True), 'w_o': (1, True), 'w_ffn_up': (2, True), 'ffn_dw_w': (2, False),
           'w_ffn_down': (1, True)}


def _sds(shape, dtype):
    return jax.ShapeDtypeStruct(tuple(shape), dtype)


def _tile(dim, pref):
    return pref if dim % pref == 0 else dim


def _pick(dim, pref, mult):
    best = None
    for t in range(mult, min(dim, pref) + 1, mult):
        if dim % t == 0:
            best = t
    return best or dim


def _cp(*sem):
    return pltpu.CompilerParams(dimension_semantics=sem, vmem_limit_bytes=VMEM_LIMIT)


def _f32(x):
    return x.astype(F32)


def _sigmoid(x):
    return 1.0 / (1.0 + jnp.exp(-x))


def _silu_and_grad(x):
    s = _sigmoid(x)
    return x * s, s * (1.0 + x * (1.0 - s))


def _ln_stats(s):
    mu = jnp.mean(s, axis=-1, keepdims=True)
    xc = s - mu
    var = jnp.mean(xc * xc, axis=-1, keepdims=True)
    rstd = lax.rsqrt(var + LN_EPS)
    return xc * rstd, rstd


def _dot(a, b, dims):
    return lax.dot_general(a, b, (dims, ((), ())), preferred_element_type=F32)


_NN = ((1,), (0,))
_NT = ((1,), (1,))
_TN = ((0,), (0,))


def _mm(a, b, *, mode, out_dtype, name, res=None, res_scale=1.0, tm=1024, tn=1408, tk=1408):
    if mode == 'nn':
        (M, K), (_, N) = a.shape, b.shape
    elif mode == 'nt':
        (M, K), (N, _) = a.shape, b.shape
    else:
        (K, M), (_, N) = a.shape, b.shape
        tm, tk = 1408, 2048
    tm, tn, tk = _pick(M, tm, 8), _pick(N, tn, LANES), _pick(K, tk, LANES)
    if N % MXU_COLS == 0 and _pick(N, tn, MXU_COLS) >= 1024:
        tn = _pick(N, tn, MXU_COLS)
    if mode != 'tn':
        for rows in (tm, _pick(M, tm // 2, 8), _pick(M, tm // 4, 8)):
            blocks = rows * K * a.dtype.itemsize + tn * K * b.dtype.itemsize + rows * tn * (4 + (4 if res is not None else 0))
            if 2 * blocks <= WHOLE_K_VMEM:
                tm, tk = rows, K
                break
    nk = K // tk
    dims = {'nn': _NN, 'nt': _NT, 'tn': _TN}[mode]

    def body(*refs):
        if res is None:
            a_ref, b_ref, o_ref = refs[:3]
            r_ref = None
        else:
            a_ref, b_ref, r_ref, o_ref = refs[:4]
        p = _dot(a_ref[...].astype(BF16), b_ref[...].astype(BF16), dims)

        def finish(acc):
            if r_ref is not None:
                acc = acc + res_scale * r_ref[...]
            o_ref[...] = acc.astype(out_dtype)

        if nk == 1:
            finish(p)
        else:
            acc_ref = refs[-1]
            k = pl.program_id(2)

            @pl.when(k == 0)
            def _():
                acc_ref[...] = p

            @pl.when(k > 0)
            def _():
                acc_ref[...] += p

            @pl.when(k == nk - 1)
            def _():
                finish(acc_ref[...])

    if mode == 'nn':
        a_spec = pl.BlockSpec((tm, tk), lambda i, j, k: (i, k))
        b_spec = pl.BlockSpec((tk, tn), lambda i, j, k: (k, j))
    elif mode == 'nt':
        a_spec = pl.BlockSpec((tm, tk), lambda i, j, k: (i, k))
        b_spec = pl.BlockSpec((tn, tk), lambda i, j, k: (j, k))
    else:
        a_spec = pl.BlockSpec((tk, tm), lambda i, j, k: (k, i))
        b_spec = pl.BlockSpec((tk, tn), lambda i, j, k: (k, j))
    o_spec = pl.BlockSpec((tm, tn), lambda i, j, k: (i, j))
    in_specs = [a_spec, b_spec] + ([o_spec] if res is not None else [])
    args = (a, b) + ((res,) if res is not None else ())
    return pl.pallas_call(
        body, name=name, out_shape=_sds((M, N), out_dtype), grid=(M // tm, N // tn, nk),
        in_specs=in_specs, out_specs=o_spec,
        scratch_shapes=[pltpu.VMEM((tm, tn), F32)] if nk > 1 else [],
        compiler_params=_cp("parallel", "parallel", "arbitrary"))(*args)


def _mm_nt_cat(a_list, b_list, res, res_scale, *, name, tm=512, tn=512):
    M, N = a_list[0].shape[0], b_list[0].shape[0]
    tm, tn = _pick(M, tm, 8), _pick(N, tn, LANES)
    n = len(a_list)

    def body(*refs):
        a_refs, b_refs, r_ref, o_ref = refs[:n], refs[n:2 * n], refs[2 * n], refs[2 * n + 1]
        acc = res_scale * r_ref[...]
        for i in range(n):
            acc = acc + _dot(a_refs[i][...].astype(BF16), b_refs[i][...].astype(BF16), _NT)
        o_ref[...] = acc

    a_specs = [pl.BlockSpec((tm, a.shape[1]), lambda j, m: (m, 0)) for a in a_list]
    b_specs = [pl.BlockSpec((tn, b.shape[1]), lambda j, m: (j, 0)) for b in b_list]
    blk = pl.BlockSpec((tm, tn), lambda j, m: (m, j))
    return pl.pallas_call(
        body, name=name, out_shape=_sds((M, N), F32), grid=(N // tn, M // tm), in_specs=a_specs + b_specs + [blk],
        out_specs=blk, compiler_params=_cp("parallel", "parallel"))(*a_list, *b_list, res)


def _mm_res_ln(a, w, res, g, b, *, name, tm=512, tk=1408):
    (M, K), (_, N) = a.shape, w.shape
    tm, tk = _pick(M, tm, 8), _pick(K, tk, LANES)
    if 2 * (tm * K * a.dtype.itemsize + K * N * w.dtype.itemsize + tm * N * 14) <= WHOLE_K_VMEM:
        tk = K
    nk = K // tk

    def body(a_ref, w_ref, r_ref, g_ref, b_ref, h_ref, s_ref, hb_ref, acc_ref):
        k = pl.program_id(1)
        p = _dot(a_ref[...].astype(BF16), w_ref[...].astype(BF16), _NN)

        def finish(acc):
            s = DN_ALPHA * r_ref[...] + acc
            xhat, _ = _ln_stats(s)
            s_ref[...] = s
            h = xhat * g_ref[...] + b_ref[...]
            h_ref[...] = h
            hb_ref[...] = h.astype(BF16)

        if nk == 1:
            finish(p)
            return

        @pl.when(k == 0)
        def _():
            acc_ref[...] = p

        @pl.when(k > 0)
        def _():
            acc_ref[...] += p

        @pl.when(k == nk - 1)
        def _():
            finish(acc_ref[...])

    row = pl.BlockSpec((tm, N), lambda i, k: (i, 0))
    vec = pl.BlockSpec((1, N), lambda i, k: (0, 0))
    return pl.pallas_call(
        body, name=name, out_shape=(_sds((M, N), F32), _sds((M, N), F32), _sds((M, N), BF16)), grid=(M // tm, nk),
        in_specs=[pl.BlockSpec((tm, tk), lambda i, k: (i, k)), pl.BlockSpec((tk, N), lambda i, k: (k, 0)), row, vec, vec],
        out_specs=(row, row, row), scratch_shapes=[pltpu.VMEM((tm, N), F32)],
        compiler_params=_cp("parallel", "arbitrary"))(a, w, res, g, b)


def _ln_fwd(x, g, b, *, name):
    T, D = x.shape
    tt = _tile(T, 512)

    def body(x_ref, g_ref, b_ref, o_ref, ob_ref):
        xhat, _ = _ln_stats(x_ref[...])
        h = xhat * g_ref[...] + b_ref[...]
        o_ref[...] = h
        ob_ref[...] = h.astype(BF16)

    row = pl.BlockSpec((tt, D), lambda i: (i, 0))
    vec = pl.BlockSpec((1, D), lambda i: (0, 0))
    return pl.pallas_call(body, name=name, out_shape=(_sds((T, D), F32), _sds((T, D), BF16)), grid=(T // tt,),
                          in_specs=[row, vec, vec], out_specs=(row, row), compiler_params=_cp("parallel"))(x, g, b)


def _ln_bwd(s, dy, g, b, *, silu, name):
    T, D = s.shape
    tt = _tile(T, 512)

    def body(s_ref, dy_ref, g_ref, b_ref, ds_ref, dg_ref, db_ref):
        i = pl.program_id(0)
        xhat, rstd = _ln_stats(s_ref[...])
        gg = g_ref[...]
        dyl = _f32(dy_ref[...])
        if silu:
            _, dsilu = _silu_and_grad(xhat * gg + b_ref[...])
            dyl = dyl * dsilu
        dxh = dyl * gg
        m1 = jnp.mean(dxh, axis=-1, keepdims=True)
        m2 = jnp.mean(dxh * xhat, axis=-1, keepdims=True)
        ds_ref[...] = rstd * (dxh - m1 - xhat * m2)

        @pl.when(i == 0)
        def _():
            dg_ref[...] = jnp.zeros_like(dg_ref)
            db_ref[...] = jnp.zeros_like(db_ref)

        dg_ref[...] += jnp.sum(dyl * xhat, axis=0, keepdims=True)
        db_ref[...] += jnp.sum(dyl, axis=0, keepdims=True)

    row = pl.BlockSpec((tt, D), lambda i: (i, 0))
    vec = pl.BlockSpec((1, D), lambda i: (0, 0))
    return pl.pallas_call(body, name=name, out_shape=(_sds((T, D), F32), _sds((1, D), F32), _sds((1, D), F32)),
                          grid=(T // tt,), in_specs=[row, row, vec, vec], out_specs=(row, vec, vec),
                          compiler_params=_cp("arbitrary"))(s, dy, g, b)


def _loss_fwd_bwd(h, tgt, *, name):
    T, D = h.shape
    tt = _tile(T, 512)

    def body(h_ref, t_ref, dh_ref, l_ref):
        i = pl.program_id(0)
        e = h_ref[...] - t_ref[...]
        dh_ref[...] = e * (1.0 / D)

        @pl.when(i == 0)
        def _():
            l_ref[...] = jnp.zeros_like(l_ref)

        part = jnp.sum(jnp.sum(e * e, axis=1, keepdims=True), axis=0, keepdims=True) * (0.5 / D)
        l_ref[...] += jnp.broadcast_to(part, l_ref.shape)

    row = pl.BlockSpec((tt, D), lambda i: (i, 0))
    one = pl.BlockSpec((8, LANES), lambda i: (0, 0))
    dh, l = pl.pallas_call(body, name=name, out_shape=(_sds((T, D), F32), _sds((8, LANES), F32)), grid=(T // tt,),
                           in_specs=[row, row], out_specs=(row, one), compiler_params=_cp("arbitrary"))(h, tgt)
    return l[0:1, 0:1], dh


def _halo_rows(k):
    return 32 if k > 17 else 16


def _shifted(ext, K, first):
    rolled = {0: ext}
    out = []
    for k in range(K):
        r = (first + k) % 8
        if r not in rolled:
            rolled[r] = pltpu.roll(ext, ext.shape[0] - r, 0)
        out.append((rolled[r], first + k - r))
    return out


def _taps(ext, w_ref, sl, K, first, n, reverse=False):
    acc = None
    for k, (z, base) in enumerate(_shifted(ext, K, first)):
        kw = K - 1 - k if reverse else k
        term = w_ref[kw:kw + 1, sl] * z[base:base + n]
        acc = term if acc is None else acc + term
    return acc


def _tap_sums(d, ext, dw_ref, sl, K, first):
    n = d.shape[0]
    for k, (z, base) in enumerate(_shifted(ext, K, first)):
        dw_ref[k:k + 1, sl] += jnp.sum(d * z[base:base + n], axis=0, keepdims=True)


def _strip_width(width):
    return LANES if width % LANES == 0 else width


def _for_strips(width, fn):
    sw = _strip_width(width)

    def step(s, carry):
        fn(pl.ds(pl.multiple_of(s * sw, sw), sw), s)
        return carry

    lax.fori_loop(0, width // sw, step, 0)


def _prev_rows(tt, hb):
    return lambda t: jnp.maximum(t * (tt // hb) - 1, 0)


def _next_rows(tt, hb, T):
    return lambda t: jnp.minimum((t + 1) * (tt // hb), T // hb - 1)


def _conf_fwd(u, w, b, g, beta, *, D, name):
    T = u.shape[0]
    K = w.shape[0]
    tt, hb, tc = _tile(T, CONV_ROWS), _halo_rows(K), _tile(D, GLU_TILE)
    sw = _strip_width(tc)
    per_tile = tc // sw
    prev = _prev_rows(tt, hb)

    def body(u_ref, uh_ref, w_ref, b_ref, g_ref, beta_ref, v1_ref, v3_ref):
        i = pl.program_id(0)

        def strip(sl, s):
            a0 = (s // per_tile) * (2 * tc) + (s % per_tile) * sw
            a_sl, g_sl = pl.ds(pl.multiple_of(a0, sw), sw), pl.ds(pl.multiple_of(a0 + tc, sw), sw)
            halo = jnp.where(i > 0, _f32(uh_ref[:, a_sl]) * _sigmoid(_f32(uh_ref[:, g_sl])), 0.0)
            ext = jnp.concatenate([halo, _f32(u_ref[:, a_sl]) * _sigmoid(_f32(u_ref[:, g_sl]))], axis=0)
            v1_ref[:, sl] = _taps(ext, w_ref, sl, K, hb - (K - 1), tt) + b_ref[:, sl]

        _for_strips(D, strip)
        xhat, _ = _ln_stats(v1_ref[...])
        v2 = xhat * g_ref[...] + beta_ref[...]
        v3_ref[...] = (v2 * _sigmoid(v2)).astype(BF16)

    row = pl.BlockSpec((tt, D), lambda t: (t, 0))
    vec = pl.BlockSpec((1, D), lambda t: (0, 0))
    return pl.pallas_call(
        body, name=name, out_shape=(_sds((T, D), F32), _sds((T, D), BF16)), grid=(T // tt,),
        in_specs=[pl.BlockSpec((tt, 2 * D), lambda t: (t, 0)), pl.BlockSpec((hb, 2 * D), lambda t: (prev(t), 0)),
                  pl.BlockSpec((K, D), lambda t: (0, 0)), vec, vec, vec],
        out_specs=(row, row), compiler_params=_cp("parallel"))(u, u, w, b, g, beta)


def _conv_act_fwd(x, off, w, b, *, ffn, name):
    T = x.shape[0]
    K, Cw = w.shape
    tt, hb = _tile(T, CONV_ROWS), _halo_rows(K)
    tc = _tile(Cw // 2, FFN_TILE) if ffn else _pick(math.gcd(Cw, off), 1536, LANES)
    xw = 2 * tc if ffn else tc
    assert off % xw == 0
    ob = off // xw
    prev = _prev_rows(tt, hb)

    def body(x_ref, xh_ref, w_ref, b_ref, o_ref):
        first = pl.program_id(0) == 0

        def pre_of(sl):
            ext = jnp.concatenate([jnp.where(first, 0.0, _f32(xh_ref[:, sl])), _f32(x_ref[:, sl])], axis=0)
            return _taps(ext, w_ref, sl, K, hb - (K - 1), tt) + b_ref[:, sl]

        def strip(sl, s):
            if ffn:
                gate = pre_of(sl)
                val = pre_of(pl.ds(pl.multiple_of(tc + s * sw, sw), sw))
                o_ref[:, sl] = (gate * _sigmoid(gate) * val).astype(BF16)
            else:
                pre = pre_of(sl)
                o_ref[:, sl] = pre * _sigmoid(pre)

        _for_strips(tc, strip)

    sw = _strip_width(tc)
    return pl.pallas_call(
        body, name=name, out_shape=_sds((T, Cw // 2), BF16) if ffn else _sds((T, Cw), F32), grid=(T // tt, Cw // xw),
        in_specs=[pl.BlockSpec((tt, xw), lambda t, c: (t, c + ob)), pl.BlockSpec((hb, xw), lambda t, c: (prev(t), c + ob)),
                  pl.BlockSpec((K, xw), lambda t, c: (0, c)), pl.BlockSpec((1, xw), lambda t, c: (0, c))],
        out_specs=pl.BlockSpec((tt, tc), lambda t, c: (t, c)),
        compiler_params=_cp("parallel", "parallel"))(x, x, w, b)


def _conv_act_bwd(x, off, w, b, dout, *, ffn, name):
    T = x.shape[0]
    K, Cw = w.shape
    tt, hb = _tile(T, CONV_ROWS), _halo_rows(K)
    tc = _tile(Cw // 2, FFN_TILE) if ffn else _pick(math.gcd(Cw, off), 1536, LANES)
    xw = 2 * tc if ffn else tc
    assert off % xw == 0
    ob = off // xw
    nt = T // tt
    prev, nxt = _prev_rows(tt, hb), _next_rows(tt, hb, T)

    sw = _strip_width(tc)

    def body(x_ref, xp_ref, xn_ref, d_ref, dn_ref, w_ref, b_ref, dx_ref, dw_ref, db_ref):
        t = pl.program_id(1)

        @pl.when(t == 0)
        def _():
            dw_ref[...] = jnp.zeros_like(dw_ref)
            db_ref[...] = jnp.zeros_like(db_ref)

        def ext_of(sl):
            return jnp.concatenate([jnp.where(t == 0, 0.0, _f32(xp_ref[:, sl])), _f32(x_ref[:, sl]), _f32(xn_ref[:, sl])], axis=0)

        def pre_of(ext, sl):
            return _taps(ext, w_ref, sl, K, hb - (K - 1), tt + hb) + b_ref[:, sl]

        def finish(ext, dpre, sl):
            dx_ref[:, sl] = _taps(dpre, w_ref, sl, K, 0, tt, reverse=True).astype(BF16)
            dp = dpre[0:tt]
            _tap_sums(dp, ext, dw_ref, sl, K, hb - (K - 1))
            db_ref[:, sl] += jnp.sum(dp, axis=0, keepdims=True)

        def strip(sl, s):
            d = jnp.concatenate([d_ref[:, sl], jnp.where(t == nt - 1, 0.0, dn_ref[:, sl])], axis=0)
            if ffn:
                vsl = pl.ds(pl.multiple_of(tc + s * sw, sw), sw)
                eg, ev = ext_of(sl), ext_of(vsl)
                sg, dsg = _silu_and_grad(pre_of(eg, sl))
                val = pre_of(ev, vsl)
                finish(eg, d * val * dsg, sl)
                finish(ev, d * sg, vsl)
            else:
                ext = ext_of(sl)
                finish(ext, d * _silu_and_grad(pre_of(ext, sl))[1], sl)

        _for_strips(tc, strip)

    return pl.pallas_call(
        body, name=name, out_shape=(_sds((T, Cw), BF16), _sds((K, Cw), F32), _sds((1, Cw), F32)),
        grid=(Cw // xw, nt),
        in_specs=[pl.BlockSpec((tt, xw), lambda c, t: (t, c + ob)), pl.BlockSpec((hb, xw), lambda c, t: (prev(t), c + ob)),
                  pl.BlockSpec((hb, xw), lambda c, t: (nxt(t), c + ob)),
                  pl.BlockSpec((tt, tc), lambda c, t: (t, c)), pl.BlockSpec((hb, tc), lambda c, t: (nxt(t), c)),
                  pl.BlockSpec((K, xw), lambda c, t: (0, c)), pl.BlockSpec((1, xw), lambda c, t: (0, c))],
        out_specs=(pl.BlockSpec((tt, xw), lambda c, t: (t, c)), pl.BlockSpec((K, xw), lambda c, t: (0, c)),
                   pl.BlockSpec((1, xw), lambda c, t: (0, c))),
        compiler_params=_cp("parallel", "arbitrary"))(x, x, x, dout, dout, w, b)


def _glu_conv_bwd(dpre, u, w, *, name):
    T, C = dpre.shape
    K = w.shape[0]
    tt, hb, tc = _tile(T, CONV_ROWS), _halo_rows(K), _tile(C, GLU_TILE)
    nt = T // tt
    prev, nxt = _prev_rows(tt, hb), _next_rows(tt, hb, T)

    sw = _strip_width(tc)

    def body(d_ref, dn_ref, x_ref, xh_ref, w_ref, dx_ref, dw_ref, db_ref):
        t = pl.program_id(1)

        @pl.when(t == 0)
        def _():
            dw_ref[...] = jnp.zeros_like(dw_ref)
            db_ref[...] = jnp.zeros_like(db_ref)

        def strip(sl, s):
            gsl = pl.ds(pl.multiple_of(tc + s * sw, sw), sw)
            d = d_ref[:, sl]
            dext = jnp.concatenate([d, jnp.where(t == nt - 1, 0.0, dn_ref[:, sl])], axis=0)
            dxin = _taps(dext, w_ref, sl, K, 0, tt, reverse=True)
            a, sg = _f32(x_ref[:, sl]), _sigmoid(_f32(x_ref[:, gsl]))
            v0 = a * sg
            dx_ref[:, sl] = (dxin * sg).astype(BF16)
            dx_ref[:, gsl] = (dxin * v0 * (1.0 - sg)).astype(BF16)
            halo = jnp.where(t == 0, 0.0, _f32(xh_ref[:, sl]) * _sigmoid(_f32(xh_ref[:, gsl])))
            _tap_sums(d, jnp.concatenate([halo, v0], axis=0), dw_ref, sl, K, hb - (K - 1))
            db_ref[:, sl] += jnp.sum(d, axis=0, keepdims=True)

        _for_strips(tc, strip)

    return pl.pallas_call(
        body, name=name, out_shape=(_sds((T, 2 * C), BF16), _sds((K, C), F32), _sds((1, C), F32)), grid=(C // tc, nt),
        in_specs=[pl.BlockSpec((tt, tc), lambda c, t: (t, c)), pl.BlockSpec((hb, tc), lambda c, t: (nxt(t), c)),
                  pl.BlockSpec((tt, 2 * tc), lambda c, t: (t, c)), pl.BlockSpec((hb, 2 * tc), lambda c, t: (prev(t), c)),
                  pl.BlockSpec((K, tc), lambda c, t: (0, c))],
        out_specs=(pl.BlockSpec((tt, 2 * tc), lambda c, t: (t, c)), pl.BlockSpec((K, tc), lambda c, t: (0, c)),
                   pl.BlockSpec((1, tc), lambda c, t: (0, c))),
        compiler_params=_cp("parallel", "arbitrary"))(dpre, dpre, u, u, w)


def _softplus(x):
    t = jnp.exp(-jnp.abs(x))
    u = 1.0 + t
    log1p = jnp.where(u == 1.0, t, jnp.log(u) * t / jnp.where(u == 1.0, 1.0, u - 1.0))
    return jnp.maximum(x, 0.0) + log1p


def _dt_fwd(udt, bias, *, H, name):
    T = udt.shape[0]
    tt = _tile(T, 1024)

    def body(u_ref, b_ref, o_ref):
        lane = lax.broadcasted_iota(jnp.int32, (tt, LANES), 1)
        o_ref[...] = jnp.where(lane < H, _softplus(u_ref[...] + b_ref[...]), 0.0)

    row = pl.BlockSpec((tt, LANES), lambda i: (i, 0))
    return pl.pallas_call(body, name=name, out_shape=_sds((T, LANES), F32), grid=(T // tt,),
                          in_specs=[row, pl.BlockSpec((1, LANES), lambda i: (0, 0))], out_specs=row,
                          compiler_params=_cp("parallel"))(udt, bias)


def _dt_bwd(da, ddtx, dt, udt, bias, a_col, *, H, name):
    T = da.shape[0]
    tt = _tile(T, 1024)

    def body(da_ref, dx_ref, dt_ref, u_ref, b_ref, a_ref, draw_ref, dbias_ref, dalog_ref):
        i = pl.program_id(0)
        lane = lax.broadcasted_iota(jnp.int32, (tt, LANES), 1)
        dav = da_ref[...]
        ddt = dav * a_ref[...] + dx_ref[...]
        draw = jnp.where(lane < H, ddt * _sigmoid(u_ref[...] + b_ref[...]), 0.0)
        draw_ref[...] = draw.astype(BF16)

        @pl.when(i == 0)
        def _():
            dbias_ref[...] = jnp.zeros_like(dbias_ref)
            dalog_ref[...] = jnp.zeros_like(dalog_ref)

        dbias_ref[...] += jnp.sum(draw, axis=0, keepdims=True)
        dalog_ref[...] += jnp.sum(dav * dt_ref[...], axis=0, keepdims=True) * a_ref[...]

    row = pl.BlockSpec((tt, LANES), lambda i: (i, 0))
    vec = pl.BlockSpec((1, LANES), lambda i: (0, 0))
    return pl.pallas_call(body, name=name,
                          out_shape=(_sds((T, LANES), BF16), _sds((1, LANES), F32), _sds((1, LANES), F32)),
                          grid=(T // tt,), in_specs=[row, row, row, row, vec, vec], out_specs=(row, vec, vec),
                          compiler_params=_cp("arbitrary"))(da, ddtx, dt, udt, bias, a_col)


def _ssd_consts():
    L = SSM_CHUNK
    r = lax.broadcasted_iota(jnp.int32, (L, L), 0)
    c = lax.broadcasted_iota(jnp.int32, (L, L), 1)
    return r, c


def _ssd_chunk_decays(dtc_ref, dtr_ref, acol_ref, arow_ref, cs_ref, csr_ref, r, c):
    L = SSM_CHUNK
    dtc = dtc_ref[...]
    tril = (r >= c).astype(F32)
    triu = (r <= c).astype(F32)
    cs_ref[...] = jnp.dot(tril, dtc * acol_ref[...], precision=HIGHEST, preferred_element_type=F32)
    csr_ref[...] = jnp.dot(dtr_ref[...] * arow_ref[...], triu, precision=HIGHEST, preferred_element_type=F32)
    cs = cs_ref[...]
    cs_last = cs_ref[L - 1:L, :]
    return dtc, cs, jnp.exp(cs), jnp.exp(cs_last - cs), jnp.exp(cs_last), triu


def _head_spread(arrs, g, heads_per_group):
    P = SSM_HEAD_DIM
    gw = heads_per_group * P
    head = lax.broadcasted_iota(jnp.int32, (LANES, gw), 0)
    lane = lax.broadcasted_iota(jnp.int32, (LANES, gw), 1)
    spread = (head == g * heads_per_group + lane // P).astype(BF16)
    out = _dot(jnp.concatenate(arrs, axis=0).astype(BF16), spread, _NN)
    L = arrs[0].shape[0]
    return [out[i * L:(i + 1) * L] for i in range(len(arrs))]


def _ssd_fwd(xbc, dt, dtT, a_col, a_row, d_col, *, H, name):
    T, XBC = xbc.shape
    L, N, G, P = SSM_CHUNK, SSM_STATE, SSM_GROUPS, SSM_HEAD_DIM
    SX = H * P
    HR = dtT.shape[0]
    nc = T // L
    heads_per_group = H // G

    def body(x_ref, dtc_ref, dtr_ref, acol_ref, arow_ref, d_ref, y_ref, hst_ref, state, cs_ref, csr_ref):
        ci = pl.program_id(0)

        @pl.when(ci == 0)
        def _():
            state[...] = jnp.zeros_like(state)

        hst_ref[0] = state[...]
        r, c = _ssd_consts()
        tri = r >= c
        lane_lo = c < P
        row_lo = r < P
        lane1_lo = lax.broadcasted_iota(jnp.int32, (1, LANES), 1) < P
        dtc, cs, e, ds, cd, _ = _ssd_chunk_decays(dtc_ref, dtr_ref, acol_ref, arow_ref, cs_ref, csr_ref, r, c)
        dsk = d_ref[...]

        for g in range(G):
            Bg = x_ref[:, SX + g * N:SX + (g + 1) * N].astype(BF16)
            Cg = x_ref[:, SX + G * N + g * N:SX + G * N + (g + 1) * N].astype(BF16)
            Gm = _dot(Cg, Bg, _NT)
            dt_g, e_g, ds_g = _head_spread([dtc, e, ds], g, heads_per_group)
            st = []
            for j in range(g * heads_per_group // 2, (g + 1) * heads_per_group // 2):
                h0 = 2 * j
                sl = slice(2 * P * j, 2 * P * (j + 1))
                gl = slice(sl.start - g * heads_per_group * P, sl.stop - g * heads_per_group * P)
                x2 = x_ref[:, sl]
                X2 = x2 * dt_g[:, gl]
                H2 = state[sl, :]
                st.append(dict(h0=h0, sl=sl, x2=x2, X2=X2, H2=H2, e2=e_g[:, gl], R2=_dot(Cg, H2.astype(BF16), _NT),
                               S2=_dot((X2 * ds_g[:, gl]).astype(BF16), Bg, _TN)))
            for s in st:
                ms = []
                for h in (s['h0'], s['h0'] + 1):
                    seg = cs[:, h:h + 1] - csr_ref[h:h + 1, :]
                    ms.append((Gm * jnp.where(tri, jnp.exp(jnp.where(tri, seg, 0.0)), 0.0)).astype(BF16))
                xst = jnp.concatenate([jnp.where(lane_lo, s['X2'], 0.0), jnp.where(lane_lo, 0.0, s['X2'])], axis=0).astype(BF16)
                s['yd'] = _dot(jnp.concatenate(ms, axis=1), xst, _NN)
            for s in st:
                h0, sl = s['h0'], s['sl']
                dsk2 = jnp.where(lane1_lo, dsk[:, h0:h0 + 1], dsk[:, h0 + 1:h0 + 2])
                y_ref[:, sl] = (s['yd'] + s['e2'] * s['R2'] + s['x2'] * dsk2).astype(BF16)
                state[sl, :] = jnp.where(row_lo, cd[:, h0:h0 + 1], cd[:, h0 + 1:h0 + 2]) * s['H2'] + s['S2']

    vec = pl.BlockSpec((1, LANES), lambda i: (0, 0))
    return pl.pallas_call(
        body, name=name, out_shape=(_sds((T, SX), BF16), _sds((nc, SX, N), F32)), grid=(nc,),
        in_specs=[pl.BlockSpec((L, XBC), lambda i: (i, 0)), pl.BlockSpec((L, LANES), lambda i: (i, 0)),
                  pl.BlockSpec((HR, L), lambda i: (0, i)), vec, pl.BlockSpec((HR, 1), lambda i: (0, 0)), vec],
        out_specs=(pl.BlockSpec((L, SX), lambda i: (i, 0)), pl.BlockSpec((1, SX, N), lambda i: (i, 0, 0))),
        scratch_shapes=[pltpu.VMEM((SX, N), F32), pltpu.VMEM((L, LANES), F32), pltpu.VMEM((HR, L), F32)],
        compiler_params=_cp("arbitrary"))(xbc, dt, dtT, a_col, a_row, d_col)


def _ssd_bwd(xbc, dt, dtT, a_col, a_row, d_col, hst, dy, *, H, name):
    T, XBC = xbc.shape
    L, N, G, P = SSM_CHUNK, SSM_STATE, SSM_GROUPS, SSM_HEAD_DIM
    SX = H * P
    HR = dtT.shape[0]
    nc = T // L
    heads_per_group = H // G

    def body(x_ref, dtc_ref, dtr_ref, acol_ref, arow_ref, d_ref, hst_ref, dy_ref,
             dx_ref, da_ref, ddtx_ref, dd_ref, dstate, cs_ref, csr_ref):
        ci = pl.program_id(0)

        @pl.when(ci == 0)
        def _():
            dstate[...] = jnp.zeros_like(dstate)
            dd_ref[...] = jnp.zeros_like(dd_ref)

        r, c = _ssd_consts()
        tri = r >= c
        lane_lo = c < P
        row_lo = r < P
        lane1 = lax.broadcasted_iota(jnp.int32, (1, LANES), 1)
        rowc = lax.broadcasted_iota(jnp.int32, (L, 1), 0)
        dtc, cs, e, ds, cd, triu = _ssd_chunk_decays(dtc_ref, dtr_ref, acol_ref, arow_ref, cs_ref, csr_ref, r, c)
        triu_b = triu.astype(BF16)
        dsk = d_ref[...]
        last_row = rowc == L - 1

        triT = r <= c

        def halves(v, axis):
            return jnp.concatenate([jnp.where(lane_lo, v, 0.0), jnp.where(lane_lo, 0.0, v)], axis=axis)

        def head_sum(v, h):
            lo = jnp.sum(jnp.where(lane1 < P, v, 0.0), axis=1, keepdims=True) * (lane1 == h).astype(F32)
            hi = jnp.sum(jnp.where(lane1 < P, 0.0, v), axis=1, keepdims=True) * (lane1 == h + 1).astype(F32)
            return lo + hi

        GW = heads_per_group * P
        gl = lax.broadcasted_iota(jnp.int32, (GW, LANES), 0)
        gc = lax.broadcasted_iota(jnp.int32, (GW, LANES), 1)
        wl = lax.broadcasted_iota(jnp.int32, (heads_per_group * L, LANES), 0)
        wc = lax.broadcasted_iota(jnp.int32, (heads_per_group * L, LANES), 1)
        wide_r = lax.broadcasted_iota(jnp.int32, (L, heads_per_group * L), 0)
        wide_c = lax.broadcasted_iota(jnp.int32, (L, heads_per_group * L), 1)
        below_diag = (wide_c % L) < wide_r
        sums = jnp.zeros((3 * L, LANES), F32)
        da_q = jnp.zeros((L, LANES), F32)
        dcd_acc = jnp.zeros((1, LANES), F32)
        dd_acc = jnp.zeros((1, LANES), F32)
        for g in range(G):
            bsl = slice(SX + g * N, SX + (g + 1) * N)
            csl = slice(SX + G * N + g * N, SX + G * N + (g + 1) * N)
            Bg = x_ref[:, bsl].astype(BF16)
            Cg = x_ref[:, csl].astype(BF16)
            Gm = _dot(Cg, Bg, _NT)
            GmT = _dot(Bg, Cg, _NT)
            dt_g, e_g, ds_g = _head_spread([dtc, e, ds], g, heads_per_group)
            st = []
            for j in range(g * heads_per_group // 2, (g + 1) * heads_per_group // 2):
                h0 = 2 * j
                sl = slice(2 * P * j, 2 * P * (j + 1))
                s = dict(h0=h0, sl=sl, x2=x_ref[:, sl], dy2=_f32(dy_ref[:, sl]), H2=hst_ref[0, sl, :], dHn=dstate[sl, :])
                gsl = slice(sl.start - g * heads_per_group * P, sl.stop - g * heads_per_group * P)
                s['dt2'], s['e2'], s['ds2'] = dt_g[:, gsl], e_g[:, gsl], ds_g[:, gsl]
                s['X2'] = s['x2'] * s['dt2']
                s['H2b'], s['dHnb'] = s['H2'].astype(BF16), s['dHn'].astype(BF16)
                st.append(s)
            for s in st:
                s['R2'] = _dot(Cg, s['H2b'], _NT)
                s['dXd'] = _dot(Bg, s['dHnb'], _NT)
                s['dM2'] = _dot(s['dy2'].astype(BF16), halves(s['X2'], 0).astype(BF16), _NT)
            dG = jnp.zeros((L, L), F32)
            qs = []
            for s in st:
                mts = []
                for i, h in enumerate((s['h0'], s['h0'] + 1)):
                    z = cs[:, h:h + 1] - csr_ref[h:h + 1, :]
                    Dm = jnp.where(tri, jnp.exp(jnp.where(tri, z, 0.0)), 0.0)
                    DmT = jnp.where(triT, jnp.exp(jnp.where(triT, -z, 0.0)), 0.0)
                    dM = s['dM2'][:, i * L:(i + 1) * L]
                    dG = dG + dM * Dm
                    qs.append((dM * (Gm * Dm)).astype(BF16))
                    mts.append((GmT * DmT).astype(BF16))
                s['dXm'] = _dot(jnp.concatenate(mts, axis=1), halves(s['dy2'], 0).astype(BF16), _NN)
            Wg = _dot(triu_b, jnp.concatenate(qs, axis=1), _NN)
            place = (wc == g * heads_per_group + wl // L).astype(BF16)
            da_q = da_q + _dot(jnp.where(below_diag, Wg, 0.0).astype(BF16), place, _NN)
            dBg = jnp.zeros((L, N), F32)
            dCg = jnp.zeros((L, N), F32)
            des, ddss, dxxs = [], [], []
            for s in st:
                h0, sl, x2, dy2, X2 = s['h0'], s['sl'], s['x2'], s['dy2'], s['X2']
                dR2b = (s['e2'] * dy2).astype(BF16)
                dCg = dCg + _dot(dR2b, s['H2b'], _NN)
                dHr = _dot(dR2b, Cg, _TN)
                dBg = dBg + _dot((X2 * s['ds2']).astype(BF16), s['dHnb'], _NN)
                dX2 = s['ds2'] * s['dXd'] + s['dXm']
                des.append(dy2 * s['R2'])
                ddss.append(s['dXd'] * X2)
                dxxs.append(dX2 * x2)
                prod = s['dHn'] * s['H2']
                for i, h in enumerate((h0, h0 + 1)):
                    rows = jnp.sum(prod[i * P:(i + 1) * P], axis=0, keepdims=True)
                    dcd_acc = dcd_acc + jnp.sum(rows, axis=1, keepdims=True) * (lane1 == h).astype(F32)
                dd_acc = dd_acc + head_sum(jnp.sum(dy2 * x2, axis=0, keepdims=True), h0)
                dsk2 = jnp.where(lane1 < P, dsk[:, h0:h0 + 1], dsk[:, h0 + 1:h0 + 2])
                dx_ref[:, sl] = dX2 * s['dt2'] + dy2 * dsk2
                dstate[sl, :] = jnp.where(row_lo, cd[:, h0:h0 + 1], cd[:, h0 + 1:h0 + 2]) * s['dHn'] + dHr
            stack = jnp.concatenate([jnp.concatenate(v, axis=1) for v in (des, ddss, dxxs)], axis=0).astype(BF16)
            sums = sums + _dot(stack, (gc == g * heads_per_group + gl // P).astype(BF16), _NN)
            dGb = dG.astype(BF16)
            dx_ref[:, bsl] = dBg + _dot(dGb, Cg, _TN)
            dx_ref[:, csl] = dCg + _dot(dGb, Bg, _NN)
        t1 = sums[L:2 * L] * ds
        tail = jnp.sum(t1, axis=0, keepdims=True) + dcd_acc * cd
        dcs = sums[0:L] * e - t1 + jnp.where(last_row, tail, 0.0)
        da_ref[...] = jnp.dot(triu, dcs, precision=HIGHEST, preferred_element_type=F32) + da_q
        ddtx_ref[...] = sums[2 * L:3 * L]
        dd_ref[...] += dd_acc

    vec = pl.BlockSpec((1, LANES), lambda i: (0, 0))
    rev = lambda i: (nc - 1 - i, 0)
    return pl.pallas_call(
        body, name=name,
        out_shape=(_sds((T, XBC), F32), _sds((T, LANES), F32), _sds((T, LANES), F32), _sds((1, LANES), F32)),
        grid=(nc,),
        in_specs=[pl.BlockSpec((L, XBC), rev), pl.BlockSpec((L, LANES), rev),
                  pl.BlockSpec((HR, L), lambda i: (0, nc - 1 - i)), vec, pl.BlockSpec((HR, 1), lambda i: (0, 0)), vec,
                  pl.BlockSpec((1, SX, N), lambda i: (nc - 1 - i, 0, 0)), pl.BlockSpec((L, SX), rev)],
        out_specs=(pl.BlockSpec((L, XBC), rev), pl.BlockSpec((L, LANES), rev), pl.BlockSpec((L, LANES), rev), vec),
        scratch_shapes=[pltpu.VMEM((SX, N), F32), pltpu.VMEM((L, LANES), F32), pltpu.VMEM((HR, L), F32)],
        compiler_params=_cp("arbitrary"))(xbc, dt, dtT, a_col, a_row, d_col, hst, dy)


def _rms_fwd(y, u, zoff, w, *, name):
    T, SX = y.shape
    tt = _tile(T, 256)
    gs = SX // SSM_GROUPS
    zb = zoff // SX

    def body(y_ref, z_ref, w_ref, o_ref):
        for g in range(SSM_GROUPS):
            sl = slice(g * gs, (g + 1) * gs)
            z = _f32(z_ref[:, sl])
            yg = _f32(y_ref[:, sl]) * (z * _sigmoid(z))
            rstd = lax.rsqrt(jnp.mean(yg * yg, axis=-1, keepdims=True) + RMS_EPS)
            o_ref[:, sl] = (yg * rstd * w_ref[:, sl]).astype(BF16)

    row = pl.BlockSpec((tt, SX), lambda i: (i, 0))
    return pl.pallas_call(body, name=name, out_shape=_sds((T, SX), BF16), grid=(T // tt,),
                          in_specs=[row, pl.BlockSpec((tt, SX), lambda i: (i, zb)), pl.BlockSpec((1, SX), lambda i: (0, 0))],
                          out_specs=row, compiler_params=_cp("parallel"))(y, u, w)


def _rms_bwd(y, u, zoff, w, dyn, *, name):
    T, SX = y.shape
    tt = _tile(T, 256)
    gs = SX // SSM_GROUPS
    zb = zoff // SX

    def body(y_ref, z_ref, w_ref, d_ref, dy_ref, dz_ref, dw_ref):
        i = pl.program_id(0)

        @pl.when(i == 0)
        def _():
            dw_ref[...] = jnp.zeros_like(dw_ref)

        for g in range(SSM_GROUPS):
            sl = slice(g * gs, (g + 1) * gs)
            z, yv, d = _f32(z_ref[:, sl]), _f32(y_ref[:, sl]), _f32(d_ref[:, sl])
            sz, dsz = _silu_and_grad(z)
            yg = yv * sz
            rstd = lax.rsqrt(jnp.mean(yg * yg, axis=-1, keepdims=True) + RMS_EPS)
            t = yg * rstd
            dw_ref[:, sl] += jnp.sum(d * t, axis=0, keepdims=True)
            dt_ = d * w_ref[:, sl]
            dyg = rstd * (dt_ - t * jnp.mean(dt_ * t, axis=-1, keepdims=True))
            dy_ref[:, sl] = (dyg * sz).astype(BF16)
            dz_ref[:, sl] = (dyg * yv * dsz).astype(BF16)

    row = pl.BlockSpec((tt, SX), lambda i: (i, 0))
    vec = pl.BlockSpec((1, SX), lambda i: (0, 0))
    return pl.pallas_call(body, name=name, out_shape=(_sds((T, SX), BF16), _sds((T, SX), BF16), _sds((1, SX), F32)),
                          grid=(T // tt,), in_specs=[row, pl.BlockSpec((tt, SX), lambda i: (i, zb)), vec, row],
                          out_specs=(row, row, vec), compiler_params=_cp("arbitrary"))(y, u, w, dyn)


def _gate_fwd(u, goff, ya, yb, *, name):
    T, D = ya.shape
    tt = _tile(T, 512)
    gb = goff // D

    def body(ga_ref, gb_ref, ya_ref, yb_ref, o_ref):
        o_ref[...] = (_sigmoid(_f32(ga_ref[...])) * _f32(ya_ref[...]) + _sigmoid(_f32(gb_ref[...])) * _f32(yb_ref[...])).astype(BF16)

    row = pl.BlockSpec((tt, D), lambda i: (i, 0))
    return pl.pallas_call(body, name=name, out_shape=_sds((T, D), BF16), grid=(T // tt,),
                          in_specs=[pl.BlockSpec((tt, D), lambda i: (i, gb)), pl.BlockSpec((tt, D), lambda i: (i, gb + 1)), row, row],
                          out_specs=row, compiler_params=_cp("parallel"))(u, u, ya, yb)


def _gate_bwd(u, goff, ya, yb, dm, *, name):
    T, D = ya.shape
    tt = _tile(T, 512)
    gb = goff // D

    def body(ga_ref, gb_ref, ya_ref, yb_ref, dm_ref, dya_ref, dyb_ref, dg_ref):
        d = _f32(dm_ref[...])
        sa, sb = _sigmoid(_f32(ga_ref[...])), _sigmoid(_f32(gb_ref[...]))
        dya_ref[...] = (d * sa).astype(BF16)
        dyb_ref[...] = (d * sb).astype(BF16)
        dg_ref[...] = jnp.concatenate([d * _f32(ya_ref[...]) * sa * (1.0 - sa), d * _f32(yb_ref[...]) * sb * (1.0 - sb)], axis=1).astype(BF16)

    row = pl.BlockSpec((tt, D), lambda i: (i, 0))
    return pl.pallas_call(body, name=name, out_shape=(_sds((T, D), BF16), _sds((T, D), BF16), _sds((T, 2 * D), BF16)),
                          grid=(T // tt,),
                          in_specs=[pl.BlockSpec((tt, D), lambda i: (i, gb)), pl.BlockSpec((tt, D), lambda i: (i, gb + 1)), row, row, row],
                          out_specs=(row, row, pl.BlockSpec((tt, 2 * D), lambda i: (i, 0))),
                          compiler_params=_cp("parallel"))(u, u, ya, yb, dm)


def _adamw_math(w, gg, m, v):
    nm = ADAM_B1 * m + (1.0 - ADAM_B1) * gg
    nv = ADAM_B2 * v + (1.0 - ADAM_B2) * (gg * gg)
    m_hat = nm / (1.0 - ADAM_B1 ** ADAM_STEP)
    v_hat = nv / (1.0 - ADAM_B2 ** ADAM_STEP)
    return -ADAM_LR * (m_hat / (jnp.sqrt(v_hat) + ADAM_EPS) + ADAM_WD * w), nm, nv


def _adamw(w, g, m, v, *, name):
    R, C = w.shape
    tr = _pick(R, 256, 8)

    def body(w_ref, g_ref, m_ref, v_ref, d_ref, nm_ref, nv_ref):
        d_ref[...], nm_ref[...], nv_ref[...] = _adamw_math(w_ref[...], g_ref[...], m_ref[...], v_ref[...])

    blk = pl.BlockSpec((tr, C), lambda i: (i, 0))
    out = _sds((R, C), F32)
    return pl.pallas_call(body, name=name, out_shape=(out, out, out), grid=(R // tr,), in_specs=[blk] * 4,
                          out_specs=(blk,) * 3, compiler_params=_cp("parallel"))(w, g, m, v)


def _sum_adamw(parts, w, m, v, *, name):
    depth, C = w.shape[0], w.shape[-1]
    R = w[0].size // C
    tr = _pick(R, 256, 8)

    def body(*refs):
        p_refs = refs[:depth]
        w_ref, m_ref, v_ref, g_ref, d_ref, nm_ref, nv_ref = refs[depth:]
        for l in range(depth):
            @pl.when(pl.program_id(0) == l)
            def _(l=l):
                gg = p_refs[l][0].astype(F32)
                for k in range(1, N_DEV):
                    gg = gg + p_refs[l][k].astype(F32)
                g_ref[...] = gg
                d_ref[...], nm_ref[...], nv_ref[...] = _adamw_math(w_ref[...], gg, m_ref[...], v_ref[...])

    p_specs = [pl.BlockSpec((N_DEV, tr, C), lambda l, i, ll=ll: (0, jnp.where(l == ll, i, 0), 0)) for ll in range(depth)]
    blk = pl.BlockSpec((None, tr, C), lambda l, i: (l, i, 0))
    out = _sds((depth, R, C), F32)
    res = pl.pallas_call(body, name=name, out_shape=(out,) * 4, grid=(depth, R // tr),
                         in_specs=p_specs + [blk, blk, blk], out_specs=(blk,) * 4,
                         compiler_params=_cp("parallel", "parallel"))(
        *[x.reshape(N_DEV, R, C) for x in parts], w.reshape(depth, R, C), m.reshape(depth, R, C), v.reshape(depth, R, C))
    return tuple(r.reshape(w.shape) for r in res)


def _sum_slots(x, *, name):
    n, R, C = x.shape
    tr = _tile(R, 512)

    def body(x_ref, o_ref):
        acc = x_ref[0].astype(F32)
        for k in range(1, n):
            acc = acc + x_ref[k].astype(F32)
        o_ref[...] = acc

    return pl.pallas_call(body, name=name, out_shape=_sds((R, C), F32), grid=(R // tr,),
                          in_specs=[pl.BlockSpec((n, tr, C), lambda i: (0, i, 0))],
                          out_specs=pl.BlockSpec((tr, C), lambda i: (i, 0)), compiler_params=_cp("parallel"))(x)


def _exchange(xs, *, scatter, name):
    n = len(xs)

    def body(*refs):
        x_refs, o_refs = refs[:n], refs[n:2 * n]
        send_sems, recv_sems, local_sems = refs[2 * n:]
        mx, my, mc = lax.axis_index("x"), lax.axis_index("y"), lax.axis_index("c")
        me = 4 * mx + 2 * my + mc

        def src(i, d):
            return x_refs[i].at[d] if scatter else x_refs[i]

        locals_ = [pltpu.make_async_copy(src(i, me), o_refs[i].at[me], local_sems.at[i]) for i in range(n)]
        for cp in locals_:
            cp.start()
        sends, recvs = [], []
        for k in range(1, N_DEV):
            px = 1 - mx if k & 4 else mx
            py = 1 - my if k & 2 else my
            pc = 1 - mc if k & 1 else mc
            peer = 4 * px + 2 * py + pc
            for i in range(n):
                common = dict(send_sem=send_sems.at[k - 1, i], recv_sem=recv_sems.at[k - 1, i],
                              device_id=(px, py, pc), device_id_type=pl.DeviceIdType.MESH)
                sends.append(pltpu.make_async_remote_copy(src_ref=src(i, peer), dst_ref=o_refs[i].at[me], **common))
                recvs.append(pltpu.make_async_remote_copy(src_ref=src(i, peer), dst_ref=o_refs[i].at[peer], **common))
        for cp in sends:
            cp.start()
        for cp in recvs:
            cp.wait_recv()
        for cp in sends:
            cp.wait_send()
        for cp in locals_:
            cp.wait()

    any_spec = pl.BlockSpec(memory_space=pl.ANY)
    out_shape = tuple(_sds(x.shape if scatter else (N_DEV,) + x.shape, x.dtype) for x in xs)
    return pl.pallas_call(
        body, name=name, out_shape=out_shape, in_specs=[any_spec] * n, out_specs=(any_spec,) * n,
        scratch_shapes=[pltpu.SemaphoreType.DMA((N_DEV - 1, n)), pltpu.SemaphoreType.DMA((N_DEV - 1, n)),
                        pltpu.SemaphoreType.DMA((n,))])(*xs)


def _gather_two_level(xs, *, name):
    n = len(xs)

    def body(*refs):
        x_refs, o_refs = refs[:n], refs[n:2 * n]
        send_sems, recv_sems, local_sems = refs[2 * n:]
        x, y, c = lax.axis_index("x"), lax.axis_index("y"), lax.axis_index("c")
        me, sibling = (x, y, c), (x, y, 1 - c)
        chips = [(1 - x, y), (x, 1 - y), (1 - x, 1 - y)]

        def copy(i, k, block, to, src=None):
            rows = o_refs[i].at[4 * block[0] + 2 * block[1] + block[2]]
            return pltpu.make_async_remote_copy(
                src_ref=rows if src is None else src, dst_ref=rows, send_sem=send_sems.at[k * n + i],
                recv_sem=recv_sems.at[k * n + i], device_id=to, device_id_type=pl.DeviceIdType.MESH)

        mine = [pltpu.make_async_copy(x_refs[i], o_refs[i].at[4 * x + 2 * y + c], local_sems.at[i]) for i in range(n)]
        for cp in mine:
            cp.start()
        first = [copy(i, 0, me, sibling, src=x_refs[i]) for i in range(n)]
        first += [copy(i, 1 + j, me, (*chip, c), src=x_refs[i]) for j, chip in enumerate(chips) for i in range(n)]
        for cp in first:
            cp.start()
        passed = []
        for j, chip in enumerate(chips):
            for i in range(n):
                copy(i, 1 + j, (*chip, c), me).wait_recv()
                passed.append(copy(i, 4 + j, (*chip, c), sibling))
                passed[-1].start()
        for i in range(n):
            copy(i, 0, sibling, me).wait_recv()
        for j, chip in enumerate(chips):
            for i in range(n):
                copy(i, 4 + j, (*chip, 1 - c), me).wait_recv()
        for cp in first + passed:
            cp.wait_send()
        for cp in mine:
            cp.wait()

    any_spec = pl.BlockSpec(memory_space=pl.ANY)
    return pl.pallas_call(
        body, name=name, out_shape=tuple(_sds((N_DEV,) + x.shape, x.dtype) for x in xs), in_specs=[any_spec] * n,
        out_specs=(any_spec,) * n,
        scratch_shapes=[pltpu.SemaphoreType.DMA(((N_DEV - 1) * n,)), pltpu.SemaphoreType.DMA(((N_DEV - 1) * n,)),
                        pltpu.SemaphoreType.DMA((n,))])(*xs)


def _to_rows(flat, row_mult):
    n = flat.shape[-1]
    per = row_mult * LANES
    pad = (-n) % per
    flat = jnp.pad(flat, [(0, 0)] * (flat.ndim - 1) + [(0, pad)])
    return flat.reshape(flat.shape[:-1] + ((n + pad) // LANES, LANES))


def _peers():
    mx, my, mc = lax.axis_index("x"), lax.axis_index("y"), lax.axis_index("c")
    out = []
    for k in range(1, N_DEV):
        px = 1 - mx if k & 4 else mx
        py = 1 - my if k & 2 else my
        pc = 1 - mc if k & 1 else mc
        out.append(((px, py, pc), 4 * px + 2 * py + pc))
    return 4 * mx + 2 * my + mc, out


_HBM = pl.BlockSpec(memory_space=pltpu.HBM)
_SEM = pl.BlockSpec(memory_space=pltpu.SEMAPHORE)


def _exchange_start(xs, *, scatter, name):
    n = len(xs)

    def body(*refs):
        x_refs, land_refs, send_sems, recv_sems, token = refs[:n], refs[n:2 * n], refs[2 * n], refs[2 * n + 1], refs[-1]
        me, peers = _peers()
        for k, (dev, peer) in enumerate(peers):
            for i in range(n):
                pltpu.make_async_remote_copy(
                    src_ref=x_refs[i].at[peer] if scatter else x_refs[i], dst_ref=land_refs[i].at[me],
                    send_sem=send_sems.at[k * n + i], recv_sem=recv_sems.at[k * n + i],
                    device_id=dev, device_id_type=pl.DeviceIdType.MESH).start()
        token[...] = jnp.zeros_like(token)

    land_shapes = [x.shape if scatter else (N_DEV,) + x.shape for x in xs]
    lands = [pltpu.with_memory_space_constraint(lax.empty(s, x.dtype), pltpu.HBM) for s, x in zip(land_shapes, xs)]
    srcs = [pltpu.with_memory_space_constraint(x, pltpu.HBM) for x in xs]
    out = pl.pallas_call(
        body, name=name,
        out_shape=(pltpu.SemaphoreType.DMA(((N_DEV - 1) * n,)), pltpu.SemaphoreType.DMA(((N_DEV - 1) * n,)),
                   *[pltpu.HBM(x.shape, x.dtype) for x in xs], *[pltpu.HBM(s, x.dtype) for s, x in zip(land_shapes, xs)],
                   _sds((8, LANES), F32)),
        in_specs=(_HBM,) * (2 * n), out_specs=(_SEM, _SEM) + (_HBM,) * (2 * n) + (pl.BlockSpec(memory_space=pltpu.VMEM),),
        input_output_aliases={i: 2 + i for i in range(2 * n)},
        compiler_params=pltpu.CompilerParams(has_side_effects=pltpu.SideEffectType.DATAFLOW_SIDE_EFFECTING))(*srcs, *lands)
    return (out[0], out[1], list(out[2:2 + n]), list(out[2 + n:2 + 2 * n])), out[-1]


def _exchange_wait(handle, after, *, scatter, name):
    send_sems, recv_sems, srcs, lands = handle
    n = len(srcs)

    def body(*refs):
        x_refs, land_refs, send_sems, recv_sems = refs[:n], refs[n:2 * n], refs[2 * n], refs[2 * n + 1]
        me, peers = _peers()
        for k, (dev, peer) in enumerate(peers):
            for i in range(n):
                cp = pltpu.make_async_remote_copy(
                    src_ref=x_refs[i].at[peer] if scatter else x_refs[i], dst_ref=land_refs[i].at[peer],
                    send_sem=send_sems.at[k * n + i], recv_sem=recv_sems.at[k * n + i],
                    device_id=dev, device_id_type=pl.DeviceIdType.MESH)
                cp.wait_send()
                cp.wait_recv()

    out = pl.pallas_call(
        body, name=name, out_shape=tuple(pltpu.HBM(a.shape, a.dtype) for a in srcs + lands),
        in_specs=(_HBM,) * (2 * n) + (_SEM, _SEM, pl.BlockSpec(memory_space=pl.ANY)), out_specs=(_HBM,) * (2 * n),
        input_output_aliases={i: i for i in range(2 * n)},
        compiler_params=pltpu.CompilerParams(has_side_effects=pltpu.SideEffectType.DATAFLOW_SIDE_EFFECTING))(
        *srcs, *lands, send_sems, recv_sems, after)
    return _with_own_slot(list(out[n:]), list(out[:n]), scatter=scatter)


def _with_own_slot(lands, xs, *, scatter):
    me = 4 * lax.axis_index("x") + 2 * lax.axis_index("y") + lax.axis_index("c")
    out = []
    for land, x in zip(lands, xs):
        own = lax.dynamic_index_in_dim(x, me, 0, keepdims=True) if scatter else x[None]
        out.append(lax.dynamic_update_index_in_dim(land, own, me, 0))
    return out


def _wire_shards(p, l, names):
    return [p[k][l].astype(WIRE) if SHARDED[k][1] else p[k][l] for k in names]


def _full_weights(got, p, names):
    full = {}
    for k, g in zip(names, got):
        axis, shape = SHARDED[k][0] - 1, p[k].shape[1:]
        g = jnp.moveaxis(g, 0, axis)
        full[k] = g.reshape(shape[:axis] + (N_DEV * shape[axis],) + shape[axis + 1:])
    return full


def _grad_parts(grads, p, names):
    parts = []
    for k in names:
        axis, g = SHARDED[k][0] - 1, grads[k]
        g = g.reshape(g.shape[:axis] + (N_DEV, p[k].shape[1:][axis]) + g.shape[axis + 1:])
        parts.append(jnp.moveaxis(g, axis, 0).astype(WIRE))
    return parts


def _allreduce_small(vals):
    metas = [(k, v.shape, v.size) for k, v in vals.items()]
    flat = jnp.concatenate([v.reshape(-1) for v in vals.values()])
    got, = _exchange([_to_rows(flat, 8)], scatter=False, name="gather_small_grads")
    summed = _sum_slots(got, name="sum_small_grads").reshape(-1)
    out, off = {}, 0
    for k, shape, n in metas:
        out[k] = summed[off:off + n].reshape(shape)
        off += n
    return out


def _row(v):
    return v.reshape(1, -1).astype(F32)


def _lanes(v):
    return jnp.pad(v.astype(F32), (0, LANES - v.shape[0])).reshape(1, LANES)


EARLY = ('w_in', 'conv_dw_w', 'ssm_conv_w')
LATE = tuple(k for k in SHARDED if k not in EARLY)
FFN = ('w_ffn_up', 'ffn_dw_w', 'w_ffn_down')
NOT_FFN = tuple(k for k in SHARDED if k not in FFN)


def _layer_weights(p, full, l, dims):
    D, SX, XBC, H, FF = dims
    assert _tile(D, GLU_TILE) == D and _tile(FF, FFN_TILE) == FF
    W = {}
    if 'w_in' in full:
        w_in = full['w_in']
        o_dt = 2 * D + SX + XBC
        w_pieces = [w_in[:, :2 * D], w_in[:, 2 * D:2 * D + SX], w_in[:, 2 * D + SX:o_dt], w_in[:, o_dt + H:]]
        a_head = -jnp.exp(p['ssm_a_log'][l].astype(F32))
        hr = -(-H // 8) * 8
        W.update(
            w_main=jnp.concatenate(w_pieces, axis=1), w_pieces=w_pieces,
            w_dt=jnp.pad(w_in[:, o_dt:o_dt + H], ((0, 0), (0, LANES - H))),
            conv_w=full['conv_dw_w'], conv_b=_row(p['conv_dw_b'][l]), conv_g=_row(p['conv_ln_g'][l]),
            conv_beta=_row(p['conv_ln_b'][l]), ssm_w=full['ssm_conv_w'], ssm_b=_row(p['ssm_conv_b'][l]),
            dt_bias=_lanes(p['ssm_dt_bias'][l]), a_col=_lanes(a_head),
            a_row=jnp.pad(a_head, (0, hr - H)).reshape(hr, 1), d_col=_lanes(p['ssm_d'][l]),
            norm_w=_row(p['ssm_norm_w'][l]), ln1_g=_row(p['ln1_g'][l]), ln1_b=_row(p['ln1_b'][l]),
            ffn_b=_row(p['ffn_dw_b'][l]), ln2_g=_row(p['ln2_g'][l]), ln2_b=_row(p['ln2_b'][l]))
    if 'w_o' in full:
        W.update(w_conv_out=full['w_conv_out'], w_ssm_out=full['w_ssm_out'], w_o=full['w_o'], w_up=full['w_ffn_up'],
                 ffn_w=full['ffn_dw_w'], w_down=full['w_ffn_down'])
    return W


def _layer_fwd(h, hb, W, late, l, dims):
    D, SX, XBC, H, FF = dims
    o_z, o_xbc, o_gate = 2 * D, 2 * D + SX, 2 * D + SX + XBC
    hr = W['a_row'].shape[0]
    t = f"l{l}"
    u = _mm(hb, W['w_main'], mode='nn', out_dtype=BF16, name=t + "_in_proj")
    udt = _mm(hb, W['w_dt'], mode='nn', out_dtype=F32, name=t + "_dt_proj")
    v1, v3 = _conf_fwd(u, W['conv_w'], W['conv_b'], W['conv_g'], W['conv_beta'], D=D, name=t + "_conf_fwd")
    xbc = _conv_act_fwd(u, o_xbc, W['ssm_w'], W['ssm_b'], ffn=False, name=t + "_ssm_conv")
    dt = _dt_fwd(udt, W['dt_bias'], H=H, name=t + "_dt")
    dtT = jnp.pad(dt[:, :H].T, ((0, hr - H), (0, 0)))
    y, hst = _ssd_fwd(xbc, dt, dtT, W['a_col'], W['a_row'], W['d_col'], H=H, name=t + "_ssd_fwd")
    yn = _rms_fwd(y, u, o_z, W['norm_w'], name=t + "_rms_fwd")
    W = {**W, **late(yn)}
    ya = _mm(v3, W['w_conv_out'], mode='nn', out_dtype=BF16, name=t + "_conv_out")
    yb = _mm(yn, W['w_ssm_out'], mode='nn', out_dtype=BF16, name=t + "_ssm_out")
    m = _gate_fwd(u, o_gate, ya, yb, name=t + "_gate_fwd")
    h1, s1, h1b = _mm_res_ln(m, W['w_o'], h, W['ln1_g'], W['ln1_b'], name=t + "_mix_ln1")
    uf = _mm(h1b, W['w_up'], mode='nn', out_dtype=BF16, name=t + "_ffn_up")
    act = _conv_act_fwd(uf, 0, W['ffn_w'], W['ffn_b'], ffn=True, name=t + "_ffn_conv")
    h2, s2, h2b = _mm_res_ln(act, W['w_down'], h1, W['ln2_g'], W['ln2_b'], name=t + "_ffn_down_ln2")
    saved = dict(hb=hb, u=u, udt=udt, v1=v1, v3=v3, ya=ya, xbc=xbc, dt=dt, dtT=dtT, y=y, hst=hst, yn=yn, yb=yb, m=m,
                 h1b=h1b, s1=s1, uf=uf, act=act, s2=s2)
    return h2, h2b, saved, W


def _layer_bwd(dh2, W, S, l, dims, on_ffn_grads=None, on_all_grads=None):
    D, SX, XBC, H, FF = dims
    o_z, o_xbc, o_gate = 2 * D, 2 * D + SX, 2 * D + SX + XBC
    t = f"l{l}"
    g = {}
    ds2, g['ln2_g'], g['ln2_b'] = _ln_bwd(S['s2'], dh2, W['ln2_g'], W['ln2_b'], silu=False, name=t + "_ln2_bwd")
    g['w_down'] = _mm(S['act'], ds2, mode='tn', out_dtype=WIRE, name=t + "_dw_down")
    dact = _mm(ds2, W['w_down'], mode='nt', out_dtype=F32, name=t + "_dact")
    duf, g['ffn_w'], g['ffn_b'] = _conv_act_bwd(S['uf'], 0, W['ffn_w'], W['ffn_b'], dact, ffn=True, name=t + "_ffn_conv_bwd")
    g['w_up'] = _mm(S['h1b'], duf, mode='tn', out_dtype=WIRE, name=t + "_dw_up")
    dh1 = _mm(duf, W['w_up'], mode='nt', out_dtype=F32, res=ds2, res_scale=DN_ALPHA, name=t + "_dh1")
    ln1_g = W['ln1_g'] if on_ffn_grads is None else W['ln1_g'] + on_ffn_grads(g)[0:1, 0:1]
    ds1, g['ln1_g'], g['ln1_b'] = _ln_bwd(S['s1'], dh1, ln1_g, W['ln1_b'], silu=False, name=t + "_ln1_bwd")
    g['w_o'] = _mm(S['m'], ds1, mode='tn', out_dtype=WIRE, name=t + "_dw_o")
    dm = _mm(ds1, W['w_o'], mode='nt', out_dtype=BF16, name=t + "_dm")
    dya, dyb, dgate = _gate_bwd(S['u'], o_gate, S['ya'], S['yb'], dm, name=t + "_gate_bwd")
    g['w_conv_out'] = _mm(S['v3'], dya, mode='tn', out_dtype=WIRE, name=t + "_dw_conv_out")
    dv3 = _mm(dya, W['w_conv_out'], mode='nt', out_dtype=BF16, name=t + "_dv3")
    dv1, g['conv_g'], g['conv_beta'] = _ln_bwd(S['v1'], dv3, W['conv_g'], W['conv_beta'], silu=True, name=t + "_conv_ln_bwd")
    dglu, g['conv_w'], g['conv_b'] = _glu_conv_bwd(dv1, S['u'], W['conv_w'], name=t + "_conf_conv_bwd")
    g['w_ssm_out'] = _mm(S['yn'], dyb, mode='tn', out_dtype=WIRE, name=t + "_dw_ssm_out")
    dyn = _mm(dyb, W['w_ssm_out'], mode='nt', out_dtype=BF16, name=t + "_dyn")
    dy, dz, g['norm_w'] = _rms_bwd(S['y'], S['u'], o_z, W['norm_w'], dyn, name=t + "_rms_bwd")
    dxbc_c, da, ddtx, g['d'] = _ssd_bwd(S['xbc'], S['dt'], S['dtT'], W['a_col'], W['a_row'], W['d_col'], S['hst'], dy,
                                        H=H, name=t + "_ssd_bwd")
    ddt_raw, g['dt_bias'], g['a_log'] = _dt_bwd(da, ddtx, S['dt'], S['udt'], W['dt_bias'], W['a_col'], H=H, name=t + "_dt_bwd")
    dxbc, g['ssm_w'], g['ssm_b'] = _conv_act_bwd(S['u'], o_xbc, W['ssm_w'], W['ssm_b'], dxbc_c, ffn=False, name=t + "_ssm_conv_bwd")
    du = [dglu, dz, dxbc, dgate]
    g['w_pieces'] = [_mm(S['hb'], d, mode='tn', out_dtype=WIRE, name=f"{t}_dw_in{i}") for i, d in enumerate(du)]
    g['w_dt'] = _mm(S['hb'], ddt_raw, mode='tn', out_dtype=WIRE, name=t + "_dw_dt")
    w_dt = W['w_dt'] if on_all_grads is None else W['w_dt'] + on_all_grads(g)[0:1, 0:1].astype(W['w_dt'].dtype)
    dh = _mm_nt_cat(du + [ddt_raw], W['w_pieces'] + [w_dt], ds1, DN_ALPHA, name=t + "_dh_in")
    return dh, g


def _layer_grads_to_params(g, dims):
    D, SX, XBC, H, FF = dims
    glu, dz, dxbc, dgate = g['w_pieces']
    w_in = jnp.concatenate([glu, dz, dxbc, g['w_dt'][:, :H], dgate], axis=1)
    return dict(
        w_in=w_in, conv_dw_w=g['conv_w'], conv_dw_b=g['conv_b'][0], conv_ln_g=g['conv_g'][0], conv_ln_b=g['conv_beta'][0],
        w_conv_out=g['w_conv_out'], ssm_conv_w=g['ssm_w'], ssm_conv_b=g['ssm_b'][0],
        ssm_dt_bias=g['dt_bias'][0, :H], ssm_a_log=g['a_log'][0, :H], ssm_d=g['d'][0, :H], ssm_norm_w=g['norm_w'][0],
        w_ssm_out=g['w_ssm_out'], w_o=g['w_o'], ln1_g=g['ln1_g'][0], ln1_b=g['ln1_b'][0],
        w_ffn_up=g['w_up'], ffn_dw_w=g['ffn_w'], ffn_dw_b=g['ffn_b'][0], w_ffn_down=g['w_down'], ln2_g=g['ln2_g'][0], ln2_b=g['ln2_b'][0])


def kernel(x, ln_in_g, ln_in_b, w_in, conv_dw_w, conv_dw_b, conv_ln_g, conv_ln_b, w_conv_out, ssm_conv_w, ssm_conv_b, ssm_dt_bias, ssm_a_log, ssm_d, ssm_norm_w, w_ssm_out, w_o, ln1_g, ln1_b, w_ffn_up, ffn_dw_w, ffn_dw_b, w_ffn_down, ln2_g, ln2_b, loss_target, m_ln_in_g, m_ln_in_b, m_w_in, m_conv_dw_w, m_conv_dw_b, m_conv_ln_g, m_conv_ln_b, m_w_conv_out, m_ssm_conv_w, m_ssm_conv_b, m_ssm_dt_bias, m_ssm_a_log, m_ssm_d, m_ssm_norm_w, m_w_ssm_out, m_w_o, m_ln1_g, m_ln1_b, m_w_ffn_up, m_ffn_dw_w, m_ffn_dw_b, m_w_ffn_down, m_ln2_g, m_ln2_b, v_ln_in_g, v_ln_in_b, v_w_in, v_conv_dw_w, v_conv_dw_b, v_conv_ln_g, v_conv_ln_b, v_w_conv_out, v_ssm_conv_w, v_ssm_conv_b, v_ssm_dt_bias, v_ssm_a_log, v_ssm_d, v_ssm_norm_w, v_w_ssm_out, v_w_o, v_ln1_g, v_ln1_b, v_w_ffn_up, v_ffn_dw_w, v_ffn_dw_b, v_w_ffn_down, v_ln2_g, v_ln2_b):
    weights = (ln_in_g, ln_in_b, w_in, conv_dw_w, conv_dw_b, conv_ln_g, conv_ln_b, w_conv_out, ssm_conv_w, ssm_conv_b,
               ssm_dt_bias, ssm_a_log, ssm_d, ssm_norm_w, w_ssm_out, w_o, ln1_g, ln1_b, w_ffn_up, ffn_dw_w, ffn_dw_b,
               w_ffn_down, ln2_g, ln2_b)
    moments_m = (m_ln_in_g, m_ln_in_b, m_w_in, m_conv_dw_w, m_conv_dw_b, m_conv_ln_g, m_conv_ln_b, m_w_conv_out,
                 m_ssm_conv_w, m_ssm_conv_b, m_ssm_dt_bias, m_ssm_a_log, m_ssm_d, m_ssm_norm_w, m_w_ssm_out, m_w_o,
                 m_ln1_g, m_ln1_b, m_w_ffn_up, m_ffn_dw_w, m_ffn_dw_b, m_w_ffn_down, m_ln2_g, m_ln2_b)
    moments_v = (v_ln_in_g, v_ln_in_b, v_w_in, v_conv_dw_w, v_conv_dw_b, v_conv_ln_g, v_conv_ln_b, v_w_conv_out,
                 v_ssm_conv_w, v_ssm_conv_b, v_ssm_dt_bias, v_ssm_a_log, v_ssm_d, v_ssm_norm_w, v_w_ssm_out, v_w_o,
                 v_ln1_g, v_ln1_b, v_w_ffn_up, v_ffn_dw_w, v_ffn_dw_b, v_w_ffn_down, v_ln2_g, v_ln2_b)
    p = dict(zip(PARAMS, weights))
    pm = dict(zip(PARAMS, moments_m))
    pv = dict(zip(PARAMS, moments_v))

    T, D = x.shape[1], x.shape[2]
    SX = w_ssm_out.shape[1] * N_DEV
    XBC = ssm_conv_b.shape[-1]
    H = ssm_d.shape[-1]
    FF = ffn_dw_b.shape[-1] // 2
    dims = (D, SX, XBC, H, FF)
    depth = w_in.shape[0]

    ALL = tuple(SHARDED)
    got = _gather_two_level(_wire_shards(p, 0, EARLY), name="gather_l0_first")
    rest_shards, got = lax.optimization_barrier((_wire_shards(p, 0, LATE), got))
    rest_handle, tok = _exchange_start(rest_shards, scatter=False, name="gather_l0_rest_start")
    first = _layer_weights(p, _full_weights(got, p, EARLY), 0, dims)
    in_flight = {}

    def start_next_gather(l, behind):
        shards, _ = lax.optimization_barrier((_wire_shards(p, l, ALL), behind))
        in_flight[l], t = _exchange_start(shards, scatter=False, name=f"gather_l{l}_start")
        in_flight[l] = (in_flight[l], shards)
        return t

    def late_weights(l):
        def late(x):
            if l > 0:
                return {}
            got = _exchange_wait(rest_handle, x, scatter=False, name="gather_l0_rest_wait")
            W = _layer_weights(p, _full_weights(got, p, LATE), 0, dims)
            if depth > 1:
                t = start_next_gather(1, got)
                W['w_conv_out'] = W['w_conv_out'] + t[0:1, 0:1].astype(W['w_conv_out'].dtype)
            return W
        return late

    xs = x.reshape(T, D)
    h, hb = _ln_fwd(xs, _row(ln_in_g) + tok[0:1, 0:1], _row(ln_in_b), name="ln_in_fwd")
    saved, layers = [], []
    for l in range(depth):
        if l > 0:
            handle, shards = in_flight.pop(l)
            got = _exchange_wait(handle, h, scatter=False, name=f"gather_l{l}_wait")
            first = _layer_weights(p, _full_weights(got, p, ALL), l, dims)
            if l + 1 < depth:
                t = start_next_gather(l + 1, got)
                first['w_dt'] = first['w_dt'] + t[0:1, 0:1].astype(first['w_dt'].dtype)
        h, hb, s, W = _layer_fwd(h, hb, first, late_weights(l), l, dims)
        saved.append(s)
        layers.append(W)
    loss_part, dh = _loss_fwd_bwd(h, loss_target.reshape(T, D), name="loss")

    layer_grads = [None] * depth
    flying, arrived = [], {}

    def wait_flying(after):
        while flying:
            l_, names, handle, parts = flying.pop(0)
            lands = _exchange_wait(handle, after, scatter=True, name=f"scatter_l{l_}_{names[0]}_wait")
            arrived.update({(l_, k): a for k, a in zip(names, lands)})
            after = lands[0]
        return after

    def start_scatter(l, names, grads_now, after):
        parts, _ = lax.optimization_barrier((_grad_parts(grads_now, p, names), wait_flying(after)))
        handle, t = _exchange_start(parts, scatter=True, name=f"scatter_l{l}_{names[0]}_start")
        flying.append((l, names, handle, parts))
        return t

    def on_ffn_grads(g):
        return start_scatter(0, FFN, dict(w_ffn_up=g['w_up'], ffn_dw_w=g['ffn_w'], w_ffn_down=g['w_down']), g['w_up'])

    def on_all_grads(g):
        layer_grads[0] = _layer_grads_to_params(g, dims)
        return start_scatter(0, NOT_FFN, layer_grads[0], g['w_dt'])

    tok = jnp.zeros((8, LANES), F32)
    for l in reversed(range(depth)):
        W = dict(layers[l], ln2_g=layers[l]['ln2_g'] + tok[0:1, 0:1])
        if l > 0:
            dh, g = _layer_bwd(dh, W, saved[l], l, dims)
            layer_grads[l] = _layer_grads_to_params(g, dims)
            tok = start_scatter(l, ALL, layer_grads[l], dh)
        else:
            dh, g = _layer_bwd(dh, W, saved[l], l, dims, on_ffn_grads, on_all_grads)
    grad_x, dg_in, db_in = _ln_bwd(xs, dh, _row(ln_in_g), _row(ln_in_b), silu=False, name="ln_in_bwd")
    wait_flying(grad_x)

    small = {k: jnp.stack([layer_grads[l][k] for l in range(depth)]) for k in layer_grads[0] if k not in SHARDED}
    small['ln_in_g'], small['ln_in_b'], small['loss'] = dg_in[0], db_in[0], loss_part
    grads = _allreduce_small(small)
    loss = grads.pop('loss').reshape(())

    delta, new_m, new_v = {}, {}, {}
    for k in SHARDED:
        grads[k], delta[k], new_m[k], new_v[k] = _sum_adamw([arrived[(l, k)] for l in range(depth)], p[k], pm[k], pv[k],
                                                            name="sum_adamw_" + k)
    rest = [k for k in PARAMS if k not in SHARDED]
    flat = lambda d: _to_rows(jnp.concatenate([d[k].reshape(-1) for k in rest]), 8)
    d_, m_, v_ = _adamw(flat(p), flat(grads), flat(pm), flat(pv), name="adamw_small")
    off = 0
    for k in rest:
        n, shp = p[k].size, p[k].shape
        delta[k] = d_.reshape(-1)[off:off + n].reshape(shp)
        new_m[k] = m_.reshape(-1)[off:off + n].reshape(shp)
        new_v[k] = v_.reshape(-1)[off:off + n].reshape(shp)
        off += n

    return (loss, grad_x.reshape(x.shape), *[grads[k] for k in PARAMS], *[delta[k] for k in PARAMS],
            *[new_m[k] for k in PARAMS], *[new_v[k] for k in PARAMS])
```

```python
import math

import jax
import jax.numpy as jnp
from jax import lax
from jax.experimental import pallas as pl
from jax.experimental.pallas import tpu as pltpu

F32 = jnp.float32
BF16 = jnp.bfloat16
WIRE = jnp.bfloat16
HIGHEST = lax.Precision.HIGHEST

DEPTH = 2
SSM_STATE = 128
SSM_CHUNK = 128
SSM_GROUPS = 4
SSM_HEAD_DIM = 64
DN_ALPHA = (2 * DEPTH) ** 0.25
LN_EPS = 1e-5
RMS_EPS = 1e-5
ADAM_LR = 0.001
ADAM_B1 = 0.9
ADAM_B2 = 0.999
ADAM_EPS = 1e-08
ADAM_WD = 0.01
ADAM_STEP = 10

N_DEV = 8
LANES = 128
MXU_COLS = 256
VMEM_LIMIT = 48 * 1024 * 1024
WHOLE_K_VMEM = 40 * 1024 * 1024
GLU_TILE = 1024
FFN_TILE = 2816
CONV_ROWS = 256

PARAMS = ['ln_in_g', 'ln_in_b', 'w_in', 'conv_dw_w', 'conv_dw_b', 'conv_ln_g', 'conv_ln_b', 'w_conv_out',
          'ssm_conv_w', 'ssm_conv_b', 'ssm_dt_bias', 'ssm_a_log', 'ssm_d', 'ssm_norm_w', 'w_ssm_out', 'w_o',
          'ln1_g', 'ln1_b', 'w_ffn_up', 'ffn_dw_w', 'ffn_dw_b', 'w_ffn_down', 'ln2_g', 'ln2_b']
SHARDED = {'w_in': (2, True), 'conv_dw_w': (2, False), 'w_conv_out': (1, True), 'ssm_conv_w': (2, False),
           'w_ssm_out': (1, True), 'w_o': (1, True), 'w_ffn_up': (2, True), 'ffn_dw_w': (2, False),
           'w_ffn_down': (1, True)}


def _sds(shape, dtype):
    return jax.ShapeDtypeStruct(tuple(shape), dtype)


def _tile(dim, pref):
    return pref if dim % pref == 0 else dim


def _pick(dim, pref, mult):
    best = None
    for t in range(mult, min(dim, pref) + 1, mult):
        if dim % t == 0:
            best = t
    return best or dim


def _cp(*sem):
    return pltpu.CompilerParams(dimension_semantics=sem, vmem_limit_bytes=VMEM_LIMIT)


def _f32(x):
    return x.astype(F32)


def _sigmoid(x):
    return 1.0 / (1.0 + jnp.exp(-x))


def _silu_and_grad(x):
    s = _sigmoid(x)
    return x * s, s * (1.0 + x * (1.0 - s))


def _ln_stats(s):
    mu = jnp.mean(s, axis=-1, keepdims=True)
    xc = s - mu
    var = jnp.mean(xc * xc, axis=-1, keepdims=True)
    rstd = lax.rsqrt(var + LN_EPS)
    return xc * rstd, rstd


def _dot(a, b, dims):
    return lax.dot_general(a, b, (dims, ((), ())), preferred_element_type=F32)


_NN = ((1,), (0,))
_NT = ((1,), (1,))
_TN = ((0,), (0,))


def _mm(a, b, *, mode, out_dtype, name, res=None, res_scale=1.0, tm=1024, tn=1408, tk=1408):
    if mode == 'nn':
        (M, K), (_, N) = a.shape, b.shape
    elif mode == 'nt':
        (M, K), (N, _) = a.shape, b.shape
    else:
        (K, M), (_, N) = a.shape, b.shape
        tm, tk = 1408, 2048
    tm, tn, tk = _pick(M, tm, 8), _pick(N, tn, LANES), _pick(K, tk, LANES)
    if N % MXU_COLS == 0 and _pick(N, tn, MXU_COLS) >= 1024:
        tn = _pick(N, tn, MXU_COLS)
    if mode != 'tn':
        for rows in (tm, _pick(M, tm // 2, 8), _pick(M, tm // 4, 8)):
            blocks = rows * K * a.dtype.itemsize + tn * K * b.dtype.itemsize + rows * tn * (4 + (4 if res is not None else 0))
            if 2 * blocks <= WHOLE_K_VMEM:
                tm, tk = rows, K
                break
    nk = K // tk
    dims = {'nn': _NN, 'nt': _NT, 'tn': _TN}[mode]

    def body(*refs):
        if res is None:
            a_ref, b_ref, o_ref = refs[:3]
            r_ref = None
        else:
            a_ref, b_ref, r_ref, o_ref = refs[:4]
        p = _dot(a_ref[...].astype(BF16), b_ref[...].astype(BF16), dims)

        def finish(acc):
            if r_ref is not None:
                acc = acc + res_scale * r_ref[...]
            o_ref[...] = acc.astype(out_dtype)

        if nk == 1:
            finish(p)
        else:
            acc_ref = refs[-1]
            k = pl.program_id(2)

            @pl.when(k == 0)
            def _():
                acc_ref[...] = p

            @pl.when(k > 0)
            def _():
                acc_ref[...] += p

            @pl.when(k == nk - 1)
            def _():
                finish(acc_ref[...])

    if mode == 'nn':
        a_spec = pl.BlockSpec((tm, tk), lambda i, j, k: (i, k))
        b_spec = pl.BlockSpec((tk, tn), lambda i, j, k: (k, j))
    elif mode == 'nt':
        a_spec = pl.BlockSpec((tm, tk), lambda i, j, k: (i, k))
        b_spec = pl.BlockSpec((tn, tk), lambda i, j, k: (j, k))
    else:
        a_spec = pl.BlockSpec((tk, tm), lambda i, j, k: (k, i))
        b_spec = pl.BlockSpec((tk, tn), lambda i, j, k: (k, j))
    o_spec = pl.BlockSpec((tm, tn), lambda i, j, k: (i, j))
    in_specs = [a_spec, b_spec] + ([o_spec] if res is not None else [])
    args = (a, b) + ((res,) if res is not None else ())
    return pl.pallas_call(
        body, name=name, out_shape=_sds((M, N), out_dtype), grid=(M // tm, N // tn, nk),
        in_specs=in_specs, out_specs=o_spec,
        scratch_shapes=[pltpu.VMEM((tm, tn), F32)] if nk > 1 else [],
        compiler_params=_cp("parallel", "parallel", "arbitrary"))(*args)


def _mm_nt_cat(a_list, b_list, res, res_scale, *, name, tm=512, tn=512):
    M, N = a_list[0].shape[0], b_list[0].shape[0]
    tm, tn = _pick(M, tm, 8), _pick(N, tn, LANES)
    n = len(a_list)

    def body(*refs):
        a_refs, b_refs, r_ref, o_ref = refs[:n], refs[n:2 * n], refs[2 * n], refs[2 * n + 1]
        acc = res_scale * r_ref[...]
        for i in range(n):
            acc = acc + _dot(a_refs[i][...].astype(BF16), b_refs[i][...].astype(BF16), _NT)
        o_ref[...] = acc

    a_specs = [pl.BlockSpec((tm, a.shape[1]), lambda j, m: (m, 0)) for a in a_list]
    b_specs = [pl.BlockSpec((tn, b.shape[1]), lambda j, m: (j, 0)) for b in b_list]
    blk = pl.BlockSpec((tm, tn), lambda j, m: (m, j))
    return pl.pallas_call(
        body, name=name, out_shape=_sds((M, N), F32), grid=(N // tn, M // tm), in_specs=a_specs + b_specs + [blk],
        out_specs=blk, compiler_params=_cp("parallel", "parallel"))(*a_list, *b_list, res)


def _mm_res_ln(a, w, res, g, b, *, name, tm=512, tk=1408):
    (M, K), (_, N) = a.shape, w.shape
    tm, tk = _pick(M, tm, 8), _pick(K, tk, LANES)
    if 2 * (tm * K * a.dtype.itemsize + K * N * w.dtype.itemsize + tm * N * 14) <= WHOLE_K_VMEM:
        tk = K
    nk = K // tk

    chunk = _pick(tm, 128, 16)

    def body(a_ref, w_ref, r_ref, g_ref, b_ref, h_ref, s_ref, hb_ref, acc_ref):
        k = pl.program_id(1)

        def finish(acc, rows=slice(None)):
            s = DN_ALPHA * r_ref[rows, :] + acc
            xhat, _ = _ln_stats(s)
            s_ref[rows, :] = s
            h = xhat * g_ref[...] + b_ref[...]
            h_ref[rows, :] = h
            hb_ref[rows, :] = h.astype(BF16)

        if nk == 1:
            wb = w_ref[...].astype(BF16)
            for c in range(tm // chunk):
                rows = slice(c * chunk, (c + 1) * chunk)
                finish(_dot(a_ref[rows, :].astype(BF16), wb, _NN), rows)
            return

        p = _dot(a_ref[...].astype(BF16), w_ref[...].astype(BF16), _NN)

        @pl.when(k == 0)
        def _():
            acc_ref[...] = p

        @pl.when(k > 0)
        def _():
            acc_ref[...] += p

        @pl.when(k == nk - 1)
        def _():
            finish(acc_ref[...])

    row = pl.BlockSpec((tm, N), lambda i, k: (i, 0))
    vec = pl.BlockSpec((1, N), lambda i, k: (0, 0))
    return pl.pallas_call(
        body, name=name, out_shape=(_sds((M, N), F32), _sds((M, N), F32), _sds((M, N), BF16)), grid=(M // tm, nk),
        in_specs=[pl.BlockSpec((tm, tk), lambda i, k: (i, k)), pl.BlockSpec((tk, N), lambda i, k: (k, 0)), row, vec, vec],
        out_specs=(row, row, row), scratch_shapes=[pltpu.VMEM((tm, N), F32)],
        compiler_params=_cp("parallel", "arbitrary"))(a, w, res, g, b)


def _ln_fwd(x, g, b, *, name):
    T, D = x.shape
    tt = _tile(T, 512)

    def body(x_ref, g_ref, b_ref, o_ref, ob_ref):
        xhat, _ = _ln_stats(x_ref[...])
        h = xhat * g_ref[...] + b_ref[...]
        o_ref[...] = h
        ob_ref[...] = h.astype(BF16)

    row = pl.BlockSpec((tt, D), lambda i: (i, 0))
    vec = pl.BlockSpec((1, D), lambda i: (0, 0))
    return pl.pallas_call(body, name=name, out_shape=(_sds((T, D), F32), _sds((T, D), BF16)), grid=(T // tt,),
                          in_specs=[row, vec, vec], out_specs=(row, row), compiler_params=_cp("parallel"))(x, g, b)


def _ln_bwd(s, dy, g, b, *, silu, name):
    T, D = s.shape
    tt = _tile(T, 512)

    def body(s_ref, dy_ref, g_ref, b_ref, ds_ref, dg_ref, db_ref):
        i = pl.program_id(0)
        xhat, rstd = _ln_stats(s_ref[...])
        gg = g_ref[...]
        dyl = _f32(dy_ref[...])
        if silu:
            _, dsilu = _silu_and_grad(xhat * gg + b_ref[...])
            dyl = dyl * dsilu
        dxh = dyl * gg
        m1 = jnp.mean(dxh, axis=-1, keepdims=True)
        m2 = jnp.mean(dxh * xhat, axis=-1, keepdims=True)
        ds_ref[...] = rstd * (dxh - m1 - xhat * m2)

        @pl.when(i == 0)
        def _():
            dg_ref[...] = jnp.zeros_like(dg_ref)
            db_ref[...] = jnp.zeros_like(db_ref)

        dg_ref[...] += jnp.sum(dyl * xhat, axis=0, keepdims=True)
        db_ref[...] += jnp.sum(dyl, axis=0, keepdims=True)

    row = pl.BlockSpec((tt, D), lambda i: (i, 0))
    vec = pl.BlockSpec((1, D), lambda i: (0, 0))
    return pl.pallas_call(body, name=name, out_shape=(_sds((T, D), F32), _sds((1, D), F32), _sds((1, D), F32)),
                          grid=(T // tt,), in_specs=[row, row, vec, vec], out_specs=(row, vec, vec),
                          compiler_params=_cp("arbitrary"))(s, dy, g, b)


def _loss_fwd_bwd(h, tgt, *, name):
    T, D = h.shape
    tt = _tile(T, 512)

    def body(h_ref, t_ref, dh_ref, l_ref):
        i = pl.program_id(0)
        e = h_ref[...] - t_ref[...]
        dh_ref[...] = e * (1.0 / D)

        @pl.when(i == 0)
        def _():
            l_ref[...] = jnp.zeros_like(l_ref)

        part = jnp.sum(jnp.sum(e * e, axis=1, keepdims=True), axis=0, keepdims=True) * (0.5 / D)
        l_ref[...] += jnp.broadcast_to(part, l_ref.shape)

    row = pl.BlockSpec((tt, D), lambda i: (i, 0))
    one = pl.BlockSpec((8, LANES), lambda i: (0, 0))
    dh, l = pl.pallas_call(body, name=name, out_shape=(_sds((T, D), F32), _sds((8, LANES), F32)), grid=(T // tt,),
                           in_specs=[row, row], out_specs=(row, one), compiler_params=_cp("arbitrary"))(h, tgt)
    return l[0:1, 0:1], dh


def _halo_rows(k):
    return 32 if k > 17 else 16


def _shifted(ext, K, first):
    rolled = {0: ext}
    out = []
    for k in range(K):
        r = (first + k) % 8
        if r not in rolled:
            rolled[r] = pltpu.roll(ext, ext.shape[0] - r, 0)
        out.append((rolled[r], first + k - r))
    return out


def _taps(ext, w_ref, sl, K, first, n, reverse=False):
    acc = None
    for k, (z, base) in enumerate(_shifted(ext, K, first)):
        kw = K - 1 - k if reverse else k
        term = w_ref[kw:kw + 1, sl] * z[base:base + n]
        acc = term if acc is None else acc + term
    return acc


def _tap_sums(d, ext, dw_ref, sl, K, first):
    n = d.shape[0]
    for k, (z, base) in enumerate(_shifted(ext, K, first)):
        dw_ref[k:k + 1, sl] += jnp.sum(d * z[base:base + n], axis=0, keepdims=True)


def _strip_width(width):
    return LANES if width % LANES == 0 else width


def _for_strips(width, fn):
    sw = _strip_width(width)

    def step(s, carry):
        fn(pl.ds(pl.multiple_of(s * sw, sw), sw), s)
        return carry

    lax.fori_loop(0, width // sw, step, 0)


def _prev_rows(tt, hb):
    return lambda t: jnp.maximum(t * (tt // hb) - 1, 0)


def _next_rows(tt, hb, T):
    return lambda t: jnp.minimum((t + 1) * (tt // hb), T // hb - 1)


def _conf_fwd(u, w, b, g, beta, *, D, name):
    T = u.shape[0]
    K = w.shape[0]
    tt, hb, tc = _tile(T, CONV_ROWS), _halo_rows(K), _tile(D, GLU_TILE)
    sw = _strip_width(tc)
    per_tile = tc // sw
    prev = _prev_rows(tt, hb)

    def body(u_ref, uh_ref, w_ref, b_ref, g_ref, beta_ref, v1_ref, v3_ref):
        i = pl.program_id(0)

        def strip(sl, s):
            a0 = (s // per_tile) * (2 * tc) + (s % per_tile) * sw
            a_sl, g_sl = pl.ds(pl.multiple_of(a0, sw), sw), pl.ds(pl.multiple_of(a0 + tc, sw), sw)
            halo = jnp.where(i > 0, _f32(uh_ref[:, a_sl]) * _sigmoid(_f32(uh_ref[:, g_sl])), 0.0)
            ext = jnp.concatenate([halo, _f32(u_ref[:, a_sl]) * _sigmoid(_f32(u_ref[:, g_sl]))], axis=0)
            v1_ref[:, sl] = _taps(ext, w_ref, sl, K, hb - (K - 1), tt) + b_ref[:, sl]

        _for_strips(D, strip)
        xhat, _ = _ln_stats(v1_ref[...])
        v2 = xhat * g_ref[...] + beta_ref[...]
        v3_ref[...] = (v2 * _sigmoid(v2)).astype(BF16)

    row = pl.BlockSpec((tt, D), lambda t: (t, 0))
    vec = pl.BlockSpec((1, D), lambda t: (0, 0))
    return pl.pallas_call(
        body, name=name, out_shape=(_sds((T, D), F32), _sds((T, D), BF16)), grid=(T // tt,),
        in_specs=[pl.BlockSpec((tt, 2 * D), lambda t: (t, 0)), pl.BlockSpec((hb, 2 * D), lambda t: (prev(t), 0)),
                  pl.BlockSpec((K, D), lambda t: (0, 0)), vec, vec, vec],
        out_specs=(row, row), compiler_params=_cp("parallel"))(u, u, w, b, g, beta)


def _conv_act_fwd(x, off, w, b, *, ffn, name):
    T = x.shape[0]
    K, Cw = w.shape
    tt, hb = _tile(T, CONV_ROWS), _halo_rows(K)
    tc = _tile(Cw // 2, FFN_TILE) if ffn else _pick(math.gcd(Cw, off), 1536, LANES)
    xw = 2 * tc if ffn else tc
    assert off % xw == 0
    ob = off // xw
    prev = _prev_rows(tt, hb)

    def body(x_ref, xh_ref, w_ref, b_ref, o_ref):
        first = pl.program_id(0) == 0

        def pre_of(sl):
            ext = jnp.concatenate([jnp.where(first, 0.0, _f32(xh_ref[:, sl])), _f32(x_ref[:, sl])], axis=0)
            return _taps(ext, w_ref, sl, K, hb - (K - 1), tt) + b_ref[:, sl]

        def strip(sl, s):
            if ffn:
                gate = pre_of(sl)
                val = pre_of(pl.ds(pl.multiple_of(tc + s * sw, sw), sw))
                o_ref[:, sl] = (gate * _sigmoid(gate) * val).astype(BF16)
            else:
                pre = pre_of(sl)
                o_ref[:, sl] = pre * _sigmoid(pre)

        _for_strips(tc, strip)

    sw = _strip_width(tc)
    return pl.pallas_call(
        body, name=name, out_shape=_sds((T, Cw // 2), BF16) if ffn else _sds((T, Cw), F32), grid=(T // tt, Cw // xw),
        in_specs=[pl.BlockSpec((tt, xw), lambda t, c: (t, c + ob)), pl.BlockSpec((hb, xw), lambda t, c: (prev(t), c + ob)),
                  pl.BlockSpec((K, xw), lambda t, c: (0, c)), pl.BlockSpec((1, xw), lambda t, c: (0, c))],
        out_specs=pl.BlockSpec((tt, tc), lambda t, c: (t, c)),
        compiler_params=_cp("parallel", "parallel"))(x, x, w, b)


def _conv_act_bwd(x, off, w, b, dout, *, ffn, name):
    T = x.shape[0]
    K, Cw = w.shape
    tt, hb = _tile(T, CONV_ROWS), _halo_rows(K)
    tc = _tile(Cw // 2, FFN_TILE) if ffn else _pick(math.gcd(Cw, off), 1536, LANES)
    xw = 2 * tc if ffn else tc
    assert off % xw == 0
    ob = off // xw
    nt = T // tt
    prev, nxt = _prev_rows(tt, hb), _next_rows(tt, hb, T)

    sw = _strip_width(tc)

    def body(x_ref, xp_ref, xn_ref, d_ref, dn_ref, w_ref, b_ref, dx_ref, dw_ref, db_ref):
        t = pl.program_id(1)

        @pl.when(t == 0)
        def _():
            dw_ref[...] = jnp.zeros_like(dw_ref)
            db_ref[...] = jnp.zeros_like(db_ref)

        def ext_of(sl):
            return jnp.concatenate([jnp.where(t == 0, 0.0, _f32(xp_ref[:, sl])), _f32(x_ref[:, sl]), _f32(xn_ref[:, sl])], axis=0)

        def pre_of(ext, sl):
            return _taps(ext, w_ref, sl, K, hb - (K - 1), tt + hb) + b_ref[:, sl]

        def finish(ext, dpre, sl):
            dx_ref[:, sl] = _taps(dpre, w_ref, sl, K, 0, tt, reverse=True).astype(BF16)
            dp = dpre[0:tt]
            _tap_sums(dp, ext, dw_ref, sl, K, hb - (K - 1))
            db_ref[:, sl] += jnp.sum(dp, axis=0, keepdims=True)

        def strip(sl, s):
            d = jnp.concatenate([d_ref[:, sl], jnp.where(t == nt - 1, 0.0, dn_ref[:, sl])], axis=0)
            if ffn:
                vsl = pl.ds(pl.multiple_of(tc + s * sw, sw), sw)
                eg, ev = ext_of(sl), ext_of(vsl)
                sg, dsg = _silu_and_grad(pre_of(eg, sl))
                val = pre_of(ev, vsl)
                finish(eg, d * val * dsg, sl)
                finish(ev, d * sg, vsl)
            else:
                ext = ext_of(sl)
                finish(ext, d * _silu_and_grad(pre_of(ext, sl))[1], sl)

        _for_strips(tc, strip)

    return pl.pallas_call(
        body, name=name, out_shape=(_sds((T, Cw), BF16), _sds((K, Cw), F32), _sds((1, Cw), F32)),
        grid=(Cw // xw, nt),
        in_specs=[pl.BlockSpec((tt, xw), lambda c, t: (t, c + ob)), pl.BlockSpec((hb, xw), lambda c, t: (prev(t), c + ob)),
                  pl.BlockSpec((hb, xw), lambda c, t: (nxt(t), c + ob)),
                  pl.BlockSpec((tt, tc), lambda c, t: (t, c)), pl.BlockSpec((hb, tc), lambda c, t: (nxt(t), c)),
                  pl.BlockSpec((K, xw), lambda c, t: (0, c)), pl.BlockSpec((1, xw), lambda c, t: (0, c))],
        out_specs=(pl.BlockSpec((tt, xw), lambda c, t: (t, c)), pl.BlockSpec((K, xw), lambda c, t: (0, c)),
                   pl.BlockSpec((1, xw), lambda c, t: (0, c))),
        compiler_params=_cp("parallel", "arbitrary"))(x, x, x, dout, dout, w, b)


def _glu_conv_bwd(dpre, u, w, *, name):
    T, C = dpre.shape
    K = w.shape[0]
    tt, hb, tc = _tile(T, CONV_ROWS), _halo_rows(K), _tile(C, GLU_TILE)
    nt = T // tt
    prev, nxt = _prev_rows(tt, hb), _next_rows(tt, hb, T)

    sw = _strip_width(tc)

    def body(d_ref, dn_ref, x_ref, xh_ref, w_ref, dx_ref, dw_ref, db_ref):
        t = pl.program_id(1)

        @pl.when(t == 0)
        def _():
            dw_ref[...] = jnp.zeros_like(dw_ref)
            db_ref[...] = jnp.zeros_like(db_ref)

        def strip(sl, s):
            gsl = pl.ds(pl.multiple_of(tc + s * sw, sw), sw)
            d = d_ref[:, sl]
            dext = jnp.concatenate([d, jnp.where(t == nt - 1, 0.0, dn_ref[:, sl])], axis=0)
            dxin = _taps(dext, w_ref, sl, K, 0, tt, reverse=True)
            a, sg = _f32(x_ref[:, sl]), _sigmoid(_f32(x_ref[:, gsl]))
            v0 = a * sg
            dx_ref[:, sl] = (dxin * sg).astype(BF16)
            dx_ref[:, gsl] = (dxin * v0 * (1.0 - sg)).astype(BF16)
            halo = jnp.where(t == 0, 0.0, _f32(xh_ref[:, sl]) * _sigmoid(_f32(xh_ref[:, gsl])))
            _tap_sums(d, jnp.concatenate([halo, v0], axis=0), dw_ref, sl, K, hb - (K - 1))
            db_ref[:, sl] += jnp.sum(d, axis=0, keepdims=True)

        _for_strips(tc, strip)

    return pl.pallas_call(
        body, name=name, out_shape=(_sds((T, 2 * C), BF16), _sds((K, C), F32), _sds((1, C), F32)), grid=(C // tc, nt),
        in_specs=[pl.BlockSpec((tt, tc), lambda c, t: (t, c)), pl.BlockSpec((hb, tc), lambda c, t: (nxt(t), c)),
                  pl.BlockSpec((tt, 2 * tc), lambda c, t: (t, c)), pl.BlockSpec((hb, 2 * tc), lambda c, t: (prev(t), c)),
                  pl.BlockSpec((K, tc), lambda c, t: (0, c))],
        out_specs=(pl.BlockSpec((tt, 2 * tc), lambda c, t: (t, c)), pl.BlockSpec((K, tc), lambda c, t: (0, c)),
                   pl.BlockSpec((1, tc), lambda c, t: (0, c))),
        compiler_params=_cp("parallel", "arbitrary"))(dpre, dpre, u, u, w)


def _softplus(x):
    t = jnp.exp(-jnp.abs(x))
    u = 1.0 + t
    log1p = jnp.where(u == 1.0, t, jnp.log(u) * t / jnp.where(u == 1.0, 1.0, u - 1.0))
    return jnp.maximum(x, 0.0) + log1p


def _dt_fwd(udt, bias, *, H, name):
    T = udt.shape[0]
    tt = _tile(T, 1024)

    def body(u_ref, b_ref, o_ref):
        lane = lax.broadcasted_iota(jnp.int32, (tt, LANES), 1)
        o_ref[...] = jnp.where(lane < H, _softplus(u_ref[...] + b_ref[...]), 0.0)

    row = pl.BlockSpec((tt, LANES), lambda i: (i, 0))
    return pl.pallas_call(body, name=name, out_shape=_sds((T, LANES), F32), grid=(T // tt,),
                          in_specs=[row, pl.BlockSpec((1, LANES), lambda i: (0, 0))], out_specs=row,
                          compiler_params=_cp("parallel"))(udt, bias)


def _dt_bwd(da, ddtx, dt, udt, bias, a_col, *, H, name):
    T = da.shape[0]
    tt = _tile(T, 1024)

    def body(da_ref, dx_ref, dt_ref, u_ref, b_ref, a_ref, draw_ref, dbias_ref, dalog_ref):
        i = pl.program_id(0)
        lane = lax.broadcasted_iota(jnp.int32, (tt, LANES), 1)
        dav = da_ref[...]
        ddt = dav * a_ref[...] + dx_ref[...]
        draw = jnp.where(lane < H, ddt * _sigmoid(u_ref[...] + b_ref[...]), 0.0)
        draw_ref[...] = draw.astype(BF16)

        @pl.when(i == 0)
        def _():
            dbias_ref[...] = jnp.zeros_like(dbias_ref)
            dalog_ref[...] = jnp.zeros_like(dalog_ref)

        dbias_ref[...] += jnp.sum(draw, axis=0, keepdims=True)
        dalog_ref[...] += jnp.sum(dav * dt_ref[...], axis=0, keepdims=True) * a_ref[...]

    row = pl.BlockSpec((tt, LANES), lambda i: (i, 0))
    vec = pl.BlockSpec((1, LANES), lambda i: (0, 0))
    return pl.pallas_call(body, name=name,
                          out_shape=(_sds((T, LANES), BF16), _sds((1, LANES), F32), _sds((1, LANES), F32)),
                          grid=(T // tt,), in_specs=[row, row, row, row, vec, vec], out_specs=(row, vec, vec),
                          compiler_params=_cp("arbitrary"))(da, ddtx, dt, udt, bias, a_col)


def _ssd_consts():
    L = SSM_CHUNK
    r = lax.broadcasted_iota(jnp.int32, (L, L), 0)
    c = lax.broadcasted_iota(jnp.int32, (L, L), 1)
    return r, c


def _ssd_chunk_decays(dtc_ref, dtr_ref, acol_ref, arow_ref, cs_ref, csr_ref, r, c):
    L = SSM_CHUNK
    dtc = dtc_ref[...]
    tril = (r >= c).astype(F32)
    triu = (r <= c).astype(F32)
    cs_ref[...] = jnp.dot(tril, dtc * acol_ref[...], precision=HIGHEST, preferred_element_type=F32)
    csr_ref[...] = jnp.dot(dtr_ref[...] * arow_ref[...], triu, precision=HIGHEST, preferred_element_type=F32)
    cs = cs_ref[...]
    cs_last = cs_ref[L - 1:L, :]
    return dtc, cs, jnp.exp(cs), jnp.exp(cs_last - cs), jnp.exp(cs_last), triu


def _head_spread(arrs, g, heads_per_group):
    P = SSM_HEAD_DIM
    gw = heads_per_group * P
    head = lax.broadcasted_iota(jnp.int32, (LANES, gw), 0)
    lane = lax.broadcasted_iota(jnp.int32, (LANES, gw), 1)
    spread = (head == g * heads_per_group + lane // P).astype(BF16)
    out = _dot(jnp.concatenate(arrs, axis=0).astype(BF16), spread, _NN)
    L = arrs[0].shape[0]
    return [out[i * L:(i + 1) * L] for i in range(len(arrs))]


def _ssd_fwd(xbc, dt, dtT, a_col, a_row, d_col, *, H, name):
    T, XBC = xbc.shape
    L, N, G, P = SSM_CHUNK, SSM_STATE, SSM_GROUPS, SSM_HEAD_DIM
    SX = H * P
    HR = dtT.shape[0]
    nc = T // L
    heads_per_group = H // G

    def body(x_ref, dtc_ref, dtr_ref, acol_ref, arow_ref, d_ref, y_ref, hst_ref, state, cs_ref, csr_ref):
        ci = pl.program_id(0)

        @pl.when(ci == 0)
        def _():
            state[...] = jnp.zeros_like(state)

        hst_ref[0] = state[...]
        r, c = _ssd_consts()
        tri = r >= c
        lane_lo = c < P
        row_lo = r < P
        lane1_lo = lax.broadcasted_iota(jnp.int32, (1, LANES), 1) < P
        dtc, cs, e, ds, cd, _ = _ssd_chunk_decays(dtc_ref, dtr_ref, acol_ref, arow_ref, cs_ref, csr_ref, r, c)
        dsk = d_ref[...]

        for g in range(G):
            Bg = x_ref[:, SX + g * N:SX + (g + 1) * N].astype(BF16)
            Cg = x_ref[:, SX + G * N + g * N:SX + G * N + (g + 1) * N].astype(BF16)
            Gm = _dot(Cg, Bg, _NT)
            dt_g, e_g, ds_g = _head_spread([dtc, e, ds], g, heads_per_group)
            st = []
            for j in range(g * heads_per_group // 2, (g + 1) * heads_per_group // 2):
                h0 = 2 * j
                sl = slice(2 * P * j, 2 * P * (j + 1))
                gl = slice(sl.start - g * heads_per_group * P, sl.stop - g * heads_per_group * P)
                x2 = x_ref[:, sl]
                X2 = x2 * dt_g[:, gl]
                H2 = state[sl, :]
                st.append(dict(h0=h0, sl=sl, x2=x2, X2=X2, H2=H2, e2=e_g[:, gl], R2=_dot(Cg, H2.astype(BF16), _NT),
                               S2=_dot((X2 * ds_g[:, gl]).astype(BF16), Bg, _TN)))
            for s in st:
                ms = []
                for h in (s['h0'], s['h0'] + 1):
                    seg = cs[:, h:h + 1] - csr_ref[h:h + 1, :]
                    ms.append((Gm * jnp.where(tri, jnp.exp(jnp.where(tri, seg, 0.0)), 0.0)).astype(BF16))
                xst = jnp.concatenate([jnp.where(lane_lo, s['X2'], 0.0), jnp.where(lane_lo, 0.0, s['X2'])], axis=0).astype(BF16)
                s['yd'] = _dot(jnp.concatenate(ms, axis=1), xst, _NN)
            for s in st:
                h0, sl = s['h0'], s['sl']
                dsk2 = jnp.where(lane1_lo, dsk[:, h0:h0 + 1], dsk[:, h0 + 1:h0 + 2])
                y_ref[:, sl] = (s['yd'] + s['e2'] * s['R2'] + s['x2'] * dsk2).astype(BF16)
                state[sl, :] = jnp.where(row_lo, cd[:, h0:h0 + 1], cd[:, h0 + 1:h0 + 2]) * s['H2'] + s['S2']

    vec = pl.BlockSpec((1, LANES), lambda i: (0, 0))
    return pl.pallas_call(
        body, name=name, out_shape=(_sds((T, SX), BF16), _sds((nc, SX, N), F32)), grid=(nc,),
        in_specs=[pl.BlockSpec((L, XBC), lambda i: (i, 0)), pl.BlockSpec((L, LANES), lambda i: (i, 0)),
                  pl.BlockSpec((HR, L), lambda i: (0, i)), vec, pl.BlockSpec((HR, 1), lambda i: (0, 0)), vec],
        out_specs=(pl.BlockSpec((L, SX), lambda i: (i, 0)), pl.BlockSpec((1, SX, N), lambda i: (i, 0, 0))),
        scratch_shapes=[pltpu.VMEM((SX, N), F32), pltpu.VMEM((L, LANES), F32), pltpu.VMEM((HR, L), F32)],
        compiler_params=_cp("arbitrary"))(xbc, dt, dtT, a_col, a_row, d_col)


def _ssd_bwd(xbc, dt, dtT, a_col, a_row, d_col, hst, dy, *, H, name):
    T, XBC = xbc.shape
    L, N, G, P = SSM_CHUNK, SSM_STATE, SSM_GROUPS, SSM_HEAD_DIM
    SX = H * P
    HR = dtT.shape[0]
    nc = T // L
    heads_per_group = H // G

    def body(x_ref, dtc_ref, dtr_ref, acol_ref, arow_ref, d_ref, hst_ref, dy_ref,
             dx_ref, da_ref, ddtx_ref, dd_ref, dstate, cs_ref, csr_ref):
        ci = pl.program_id(0)

        @pl.when(ci == 0)
        def _():
            dstate[...] = jnp.zeros_like(dstate)
            dd_ref[...] = jnp.zeros_like(dd_ref)

        r, c = _ssd_consts()
        tri = r >= c
        lane_lo = c < P
        row_lo = r < P
        lane1 = lax.broadcasted_iota(jnp.int32, (1, LANES), 1)
        rowc = lax.broadcasted_iota(jnp.int32, (L, 1), 0)
        dtc, cs, e, ds, cd, triu = _ssd_chunk_decays(dtc_ref, dtr_ref, acol_ref, arow_ref, cs_ref, csr_ref, r, c)
        triu_b = triu.astype(BF16)
        dsk = d_ref[...]
        last_row = rowc == L - 1

        triT = r <= c

        def halves(v, axis):
            return jnp.concatenate([jnp.where(lane_lo, v, 0.0), jnp.where(lane_lo, 0.0, v)], axis=axis)

        def head_sum(v, h):
            lo = jnp.sum(jnp.where(lane1 < P, v, 0.0), axis=1, keepdims=True) * (lane1 == h).astype(F32)
            hi = jnp.sum(jnp.where(lane1 < P, 0.0, v), axis=1, keepdims=True) * (lane1 == h + 1).astype(F32)
            return lo + hi

        GW = heads_per_group * P
        gl = lax.broadcasted_iota(jnp.int32, (GW, LANES), 0)
        gc = lax.broadcasted_iota(jnp.int32, (GW, LANES), 1)
        wl = lax.broadcasted_iota(jnp.int32, (heads_per_group * L, LANES), 0)
        wc = lax.broadcasted_iota(jnp.int32, (heads_per_group * L, LANES), 1)
        wide_r = lax.broadcasted_iota(jnp.int32, (L, heads_per_group * L), 0)
        wide_c = lax.broadcasted_iota(jnp.int32, (L, heads_per_group * L), 1)
        below_diag = (wide_c % L) < wide_r
        sums = jnp.zeros((3 * L, LANES), F32)
        da_q = jnp.zeros((L, LANES), F32)
        dcd_acc = jnp.zeros((1, LANES), F32)
        dd_acc = jnp.zeros((1, LANES), F32)
        for g in range(G):
            bsl = slice(SX + g * N, SX + (g + 1) * N)
            csl = slice(SX + G * N + g * N, SX + G * N + (g + 1) * N)
            Bg = x_ref[:, bsl].astype(BF16)
            Cg = x_ref[:, csl].astype(BF16)
            Gm = _dot(Cg, Bg, _NT)
            GmT = _dot(Bg, Cg, _NT)
            dt_g, e_g, ds_g = _head_spread([dtc, e, ds], g, heads_per_group)
            st = []
            for j in range(g * heads_per_group // 2, (g + 1) * heads_per_group // 2):
                h0 = 2 * j
                sl = slice(2 * P * j, 2 * P * (j + 1))
                s = dict(h0=h0, sl=sl, x2=x_ref[:, sl], dy2=_f32(dy_ref[:, sl]), H2=hst_ref[0, sl, :], dHn=dstate[sl, :])
                gsl = slice(sl.start - g * heads_per_group * P, sl.stop - g * heads_per_group * P)
                s['dt2'], s['e2'], s['ds2'] = dt_g[:, gsl], e_g[:, gsl], ds_g[:, gsl]
                s['X2'] = s['x2'] * s['dt2']
                s['H2b'], s['dHnb'] = s['H2'].astype(BF16), s['dHn'].astype(BF16)
                st.append(s)
            for s in st:
                s['R2'] = _dot(Cg, s['H2b'], _NT)
                s['dXd'] = _dot(Bg, s['dHnb'], _NT)
                s['dM2'] = _dot(s['dy2'].astype(BF16), halves(s['X2'], 0).astype(BF16), _NT)
            dG = jnp.zeros((L, L), F32)
            qs = []
            for s in st:
                mts = []
                for i, h in enumerate((s['h0'], s['h0'] + 1)):
                    z = cs[:, h:h + 1] - csr_ref[h:h + 1, :]
                    Dm = jnp.where(tri, jnp.exp(jnp.where(tri, z, 0.0)), 0.0)
                    DmT = jnp.where(triT, jnp.exp(jnp.where(triT, -z, 0.0)), 0.0)
                    dM = s['dM2'][:, i * L:(i + 1) * L]
                    dG = dG + dM * Dm
                    qs.append((dM * (Gm * Dm)).astype(BF16))
                    mts.append((GmT * DmT).astype(BF16))
                s['dXm'] = _dot(jnp.concatenate(mts, axis=1), halves(s['dy2'], 0).astype(BF16), _NN)
            Wg = _dot(triu_b, jnp.concatenate(qs, axis=1), _NN)
            place = (wc == g * heads_per_group + wl // L).astype(BF16)
            da_q = da_q + _dot(jnp.where(below_diag, Wg, 0.0).astype(BF16), place, _NN)
            dBg = jnp.zeros((L, N), F32)
            dCg = jnp.zeros((L, N), F32)
            des, ddss, dxxs = [], [], []
            for s in st:
                h0, sl, x2, dy2, X2 = s['h0'], s['sl'], s['x2'], s['dy2'], s['X2']
                dR2b = (s['e2'] * dy2).astype(BF16)
                dCg = dCg + _dot(dR2b, s['H2b'], _NN)
                dHr = _dot(dR2b, Cg, _TN)
                dBg = dBg + _dot((X2 * s['ds2']).astype(BF16), s['dHnb'], _NN)
                dX2 = s['ds2'] * s['dXd'] + s['dXm']
                des.append(dy2 * s['R2'])
                ddss.append(s['dXd'] * X2)
                dxxs.append(dX2 * x2)
                prod = s['dHn'] * s['H2']
                for i, h in enumerate((h0, h0 + 1)):
                    rows = jnp.sum(prod[i * P:(i + 1) * P], axis=0, keepdims=True)
                    dcd_acc = dcd_acc + jnp.sum(rows, axis=1, keepdims=True) * (lane1 == h).astype(F32)
                dd_acc = dd_acc + head_sum(jnp.sum(dy2 * x2, axis=0, keepdims=True), h0)
                dsk2 = jnp.where(lane1 < P, dsk[:, h0:h0 + 1], dsk[:, h0 + 1:h0 + 2])
                dx_ref[:, sl] = dX2 * s['dt2'] + dy2 * dsk2
                dstate[sl, :] = jnp.where(row_lo, cd[:, h0:h0 + 1], cd[:, h0 + 1:h0 + 2]) * s['dHn'] + dHr
            stack = jnp.concatenate([jnp.concatenate(v, axis=1) for v in (des, ddss, dxxs)], axis=0).astype(BF16)
            sums = sums + _dot(stack, (gc == g * heads_per_group + gl // P).astype(BF16), _NN)
            dGb = dG.astype(BF16)
            dx_ref[:, bsl] = dBg + _dot(dGb, Cg, _TN)
            dx_ref[:, csl] = dCg + _dot(dGb, Bg, _NN)
        t1 = sums[L:2 * L] * ds
        tail = jnp.sum(t1, axis=0, keepdims=True) + dcd_acc * cd
        dcs = sums[0:L] * e - t1 + jnp.where(last_row, tail, 0.0)
        da_ref[...] = jnp.dot(triu, dcs, precision=HIGHEST, preferred_element_type=F32) + da_q
        ddtx_ref[...] = sums[2 * L:3 * L]
        dd_ref[...] += dd_acc

    vec = pl.BlockSpec((1, LANES), lambda i: (0, 0))
    rev = lambda i: (nc - 1 - i, 0)
    return pl.pallas_call(
        body, name=name,
        out_shape=(_sds((T, XBC), F32), _sds((T, LANES), F32), _sds((T, LANES), F32), _sds((1, LANES), F32)),
        grid=(nc,),
        in_specs=[pl.BlockSpec((L, XBC), rev), pl.BlockSpec((L, LANES), rev),
                  pl.BlockSpec((HR, L), lambda i: (0, nc - 1 - i)), vec, pl.BlockSpec((HR, 1), lambda i: (0, 0)), vec,
                  pl.BlockSpec((1, SX, N), lambda i: (nc - 1 - i, 0, 0)), pl.BlockSpec((L, SX), rev)],
        out_specs=(pl.BlockSpec((L, XBC), rev), pl.BlockSpec((L, LANES), rev), pl.BlockSpec((L, LANES), rev), vec),
        scratch_shapes=[pltpu.VMEM((SX, N), F32), pltpu.VMEM((L, LANES), F32), pltpu.VMEM((HR, L), F32)],
        compiler_params=_cp("arbitrary"))(xbc, dt, dtT, a_col, a_row, d_col, hst, dy)


def _rms_fwd(y, u, zoff, w, *, name):
    T, SX = y.shape
    tt = _tile(T, 256)
    gs = SX // SSM_GROUPS
    zb = zoff // SX

    def body(y_ref, z_ref, w_ref, o_ref):
        for g in range(SSM_GROUPS):
            sl = slice(g * gs, (g + 1) * gs)
            z = _f32(z_ref[:, sl])
            yg = _f32(y_ref[:, sl]) * (z * _sigmoid(z))
            rstd = lax.rsqrt(jnp.mean(yg * yg, axis=-1, keepdims=True) + RMS_EPS)
            o_ref[:, sl] = (yg * rstd * w_ref[:, sl]).astype(BF16)

    row = pl.BlockSpec((tt, SX), lambda i: (i, 0))
    return pl.pallas_call(body, name=name, out_shape=_sds((T, SX), BF16), grid=(T // tt,),
                          in_specs=[row, pl.BlockSpec((tt, SX), lambda i: (i, zb)), pl.BlockSpec((1, SX), lambda i: (0, 0))],
                          out_specs=row, compiler_params=_cp("parallel"))(y, u, w)


def _rms_bwd(y, u, zoff, w, dyn, *, name):
    T, SX = y.shape
    tt = _tile(T, 256)
    gs = SX // SSM_GROUPS
    zb = zoff // SX

    def body(y_ref, z_ref, w_ref, d_ref, dy_ref, dz_ref, dw_ref):
        i = pl.program_id(0)

        @pl.when(i == 0)
        def _():
            dw_ref[...] = jnp.zeros_like(dw_ref)

        for g in range(SSM_GROUPS):
            sl = slice(g * gs, (g + 1) * gs)
            z, yv, d = _f32(z_ref[:, sl]), _f32(y_ref[:, sl]), _f32(d_ref[:, sl])
            sz, dsz = _silu_and_grad(z)
            yg = yv * sz
            rstd = lax.rsqrt(jnp.mean(yg * yg, axis=-1, keepdims=True) + RMS_EPS)
            t = yg * rstd
            dw_ref[:, sl] += jnp.sum(d * t, axis=0, keepdims=True)
            dt_ = d * w_ref[:, sl]
            dyg = rstd * (dt_ - t * jnp.mean(dt_ * t, axis=-1, keepdims=True))
            dy_ref[:, sl] = (dyg * sz).astype(BF16)
            dz_ref[:, sl] = (dyg * yv * dsz).astype(BF16)

    row = pl.BlockSpec((tt, SX), lambda i: (i, 0))
    vec = pl.BlockSpec((1, SX), lambda i: (0, 0))
    return pl.pallas_call(body, name=name, out_shape=(_sds((T, SX), BF16), _sds((T, SX), BF16), _sds((1, SX), F32)),
                          grid=(T // tt,), in_specs=[row, pl.BlockSpec((tt, SX), lambda i: (i, zb)), vec, row],
                          out_specs=(row, row, vec), compiler_params=_cp("arbitrary"))(y, u, w, dyn)


def _gate_fwd(u, goff, ya, yb, *, name):
    T, D = ya.shape
    tt = _tile(T, 512)
    gb = goff // D

    def body(ga_ref, gb_ref, ya_ref, yb_ref, o_ref):
        o_ref[...] = (_sigmoid(_f32(ga_ref[...])) * _f32(ya_ref[...]) + _sigmoid(_f32(gb_ref[...])) * _f32(yb_ref[...])).astype(BF16)

    row = pl.BlockSpec((tt, D), lambda i: (i, 0))
    return pl.pallas_call(body, name=name, out_shape=_sds((T, D), BF16), grid=(T // tt,),
                          in_specs=[pl.BlockSpec((tt, D), lambda i: (i, gb)), pl.BlockSpec((tt, D), lambda i: (i, gb + 1)), row, row],
                          out_specs=row, compiler_params=_cp("parallel"))(u, u, ya, yb)


def _gate_bwd(u, goff, ya, yb, dm, *, name):
    T, D = ya.shape
    tt = _tile(T, 512)
    gb = goff // D

    def body(ga_ref, gb_ref, ya_ref, yb_ref, dm_ref, dya_ref, dyb_ref, dg_ref):
        d = _f32(dm_ref[...])
        sa, sb = _sigmoid(_f32(ga_ref[...])), _sigmoid(_f32(gb_ref[...]))
        dya_ref[...] = (d * sa).astype(BF16)
        dyb_ref[...] = (d * sb).astype(BF16)
        dg_ref[...] = jnp.concatenate([d * _f32(ya_ref[...]) * sa * (1.0 - sa), d * _f32(yb_ref[...]) * sb * (1.0 - sb)], axis=1).astype(BF16)

    row = pl.BlockSpec((tt, D), lambda i: (i, 0))
    return pl.pallas_call(body, name=name, out_shape=(_sds((T, D), BF16), _sds((T, D), BF16), _sds((T, 2 * D), BF16)),
                          grid=(T // tt,),
                          in_specs=[pl.BlockSpec((tt, D), lambda i: (i, gb)), pl.BlockSpec((tt, D), lambda i: (i, gb + 1)), row, row, row],
                          out_specs=(row, row, pl.BlockSpec((tt, 2 * D), lambda i: (i, 0))),
                          compiler_params=_cp("parallel"))(u, u, ya, yb, dm)


def _adamw_math(w, gg, m, v):
    nm = ADAM_B1 * m + (1.0 - ADAM_B1) * gg
    nv = ADAM_B2 * v + (1.0 - ADAM_B2) * (gg * gg)
    m_hat = nm / (1.0 - ADAM_B1 ** ADAM_STEP)
    v_hat = nv / (1.0 - ADAM_B2 ** ADAM_STEP)
    return -ADAM_LR * (m_hat / (jnp.sqrt(v_hat) + ADAM_EPS) + ADAM_WD * w), nm, nv


def _adamw(w, g, m, v, *, name):
    R, C = w.shape
    tr = _pick(R, 256, 8)

    def body(w_ref, g_ref, m_ref, v_ref, d_ref, nm_ref, nv_ref):
        d_ref[...], nm_ref[...], nv_ref[...] = _adamw_math(w_ref[...], g_ref[...], m_ref[...], v_ref[...])

    blk = pl.BlockSpec((tr, C), lambda i: (i, 0))
    out = _sds((R, C), F32)
    return pl.pallas_call(body, name=name, out_shape=(out, out, out), grid=(R // tr,), in_specs=[blk] * 4,
                          out_specs=(blk,) * 3, compiler_params=_cp("parallel"))(w, g, m, v)


def _sum_adamw(parts, w, m, v, *, name):
    depth, C = w.shape[0], w.shape[-1]
    R = w[0].size // C
    tr = _pick(R, 256, 8)

    def body(*refs):
        p_refs = refs[:depth]
        w_ref, m_ref, v_ref, g_ref, d_ref, nm_ref, nv_ref = refs[depth:]
        for l in range(depth):
            @pl.when(pl.program_id(0) == l)
            def _(l=l):
                gg = p_refs[l][0].astype(F32)
                for k in range(1, N_DEV):
                    gg = gg + p_refs[l][k].astype(F32)
                g_ref[...] = gg
                d_ref[...], nm_ref[...], nv_ref[...] = _adamw_math(w_ref[...], gg, m_ref[...], v_ref[...])

    p_specs = [pl.BlockSpec((N_DEV, tr, C), lambda l, i, ll=ll: (0, jnp.where(l == ll, i, 0), 0)) for ll in range(depth)]
    blk = pl.BlockSpec((None, tr, C), lambda l, i: (l, i, 0))
    out = _sds((depth, R, C), F32)
    res = pl.pallas_call(body, name=name, out_shape=(out,) * 4, grid=(depth, R // tr),
                         in_specs=p_specs + [blk, blk, blk], out_specs=(blk,) * 4,
                         compiler_params=_cp("parallel", "parallel"))(
        *[x.reshape(N_DEV, R, C) for x in parts], w.reshape(depth, R, C), m.reshape(depth, R, C), v.reshape(depth, R, C))
    return tuple(r.reshape(w.shape) for r in res)


def _sum_slots(x, *, name):
    n, R, C = x.shape
    tr = _tile(R, 512)

    def body(x_ref, o_ref):
        acc = x_ref[0].astype(F32)
        for k in range(1, n):
            acc = acc + x_ref[k].astype(F32)
        o_ref[...] = acc

    return pl.pallas_call(body, name=name, out_shape=_sds((R, C), F32), grid=(R // tr,),
                          in_specs=[pl.BlockSpec((n, tr, C), lambda i: (0, i, 0))],
                          out_specs=pl.BlockSpec((tr, C), lambda i: (i, 0)), compiler_params=_cp("parallel"))(x)


def _exchange(xs, *, scatter, name):
    n = len(xs)

    def body(*refs):
        x_refs, o_refs = refs[:n], refs[n:2 * n]
        send_sems, recv_sems, local_sems = refs[2 * n:]
        mx, my, mc = lax.axis_index("x"), lax.axis_index("y"), lax.axis_index("c")
        me = 4 * mx + 2 * my + mc

        def src(i, d):
            return x_refs[i].at[d] if scatter else x_refs[i]

        locals_ = [pltpu.make_async_copy(src(i, me), o_refs[i].at[me], local_sems.at[i]) for i in range(n)]
        for cp in locals_:
            cp.start()
        sends, recvs = [], []
        for k in range(1, N_DEV):
            px = 1 - mx if k & 4 else mx
            py = 1 - my if k & 2 else my
            pc = 1 - mc if k & 1 else mc
            peer = 4 * px + 2 * py + pc
            for i in range(n):
                common = dict(send_sem=send_sems.at[k - 1, i], recv_sem=recv_sems.at[k - 1, i],
                              device_id=(px, py, pc), device_id_type=pl.DeviceIdType.MESH)
                sends.append(pltpu.make_async_remote_copy(src_ref=src(i, peer), dst_ref=o_refs[i].at[me], **common))
                recvs.append(pltpu.make_async_remote_copy(src_ref=src(i, peer), dst_ref=o_refs[i].at[peer], **common))
        for cp in sends:
            cp.start()
        for cp in recvs:
            cp.wait_recv()
        for cp in sends:
            cp.wait_send()
        for cp in locals_:
            cp.wait()

    any_spec = pl.BlockSpec(memory_space=pl.ANY)
    out_shape = tuple(_sds(x.shape if scatter else (N_DEV,) + x.shape, x.dtype) for x in xs)
    return pl.pallas_call(
        body, name=name, out_shape=out_shape, in_specs=[any_spec] * n, out_specs=(any_spec,) * n,
        scratch_shapes=[pltpu.SemaphoreType.DMA((N_DEV - 1, n)), pltpu.SemaphoreType.DMA((N_DEV - 1, n)),
                        pltpu.SemaphoreType.DMA((n,))])(*xs)


def _gather_two_level(xs, *, name):
    n = len(xs)

    def body(*refs):
        x_refs, o_refs = refs[:n], refs[n:2 * n]
        send_sems, recv_sems, local_sems = refs[2 * n:]
        x, y, c = lax.axis_index("x"), lax.axis_index("y"), lax.axis_index("c")
        me, sibling = (x, y, c), (x, y, 1 - c)
        chips = [(1 - x, y), (x, 1 - y), (1 - x, 1 - y)]

        def copy(i, k, block, to, src=None):
            rows = o_refs[i].at[4 * block[0] + 2 * block[1] + block[2]]
            return pltpu.make_async_remote_copy(
                src_ref=rows if src is None else src, dst_ref=rows, send_sem=send_sems.at[k * n + i],
                recv_sem=recv_sems.at[k * n + i], device_id=to, device_id_type=pl.DeviceIdType.MESH)

        mine = [pltpu.make_async_copy(x_refs[i], o_refs[i].at[4 * x + 2 * y + c], local_sems.at[i]) for i in range(n)]
        for cp in mine:
            cp.start()
        first = [copy(i, 0, me, sibling, src=x_refs[i]) for i in range(n)]
        first += [copy(i, 1 + j, me, (*chip, c), src=x_refs[i]) for j, chip in enumerate(chips) for i in range(n)]
        for cp in first:
            cp.start()
        passed = []
        for j, chip in enumerate(chips):
            for i in range(n):
                copy(i, 1 + j, (*chip, c), me).wait_recv()
                passed.append(copy(i, 4 + j, (*chip, c), sibling))
                passed[-1].start()
        for i in range(n):
            copy(i, 0, sibling, me).wait_recv()
        for j, chip in enumerate(chips):
            for i in range(n):
                copy(i, 4 + j, (*chip, 1 - c), me).wait_recv()
        for cp in first + passed:
            cp.wait_send()
        for cp in mine:
            cp.wait()

    any_spec = pl.BlockSpec(memory_space=pl.ANY)
    return pl.pallas_call(
        body, name=name, out_shape=tuple(_sds((N_DEV,) + x.shape, x.dtype) for x in xs), in_specs=[any_spec] * n,
        out_specs=(any_spec,) * n,
        scratch_shapes=[pltpu.SemaphoreType.DMA(((N_DEV - 1) * n,)), pltpu.SemaphoreType.DMA(((N_DEV - 1) * n,)),
                        pltpu.SemaphoreType.DMA((n,))])(*xs)


def _to_rows(flat, row_mult):
    n = flat.shape[-1]
    per = row_mult * LANES
    pad = (-n) % per
    flat = jnp.pad(flat, [(0, 0)] * (flat.ndim - 1) + [(0, pad)])
    return flat.reshape(flat.shape[:-1] + ((n + pad) // LANES, LANES))


def _peers():
    mx, my, mc = lax.axis_index("x"), lax.axis_index("y"), lax.axis_index("c")
    out = []
    for k in range(1, N_DEV):
        px = 1 - mx if k & 4 else mx
        py = 1 - my if k & 2 else my
        pc = 1 - mc if k & 1 else mc
        out.append(((px, py, pc), 4 * px + 2 * py + pc))
    return 4 * mx + 2 * my + mc, out


_HBM = pl.BlockSpec(memory_space=pltpu.HBM)
_SEM = pl.BlockSpec(memory_space=pltpu.SEMAPHORE)


def _exchange_start(xs, *, scatter, name):
    n = len(xs)

    def body(*refs):
        x_refs, land_refs, send_sems, recv_sems, token = refs[:n], refs[n:2 * n], refs[2 * n], refs[2 * n + 1], refs[-1]
        me, peers = _peers()
        for k, (dev, peer) in enumerate(peers):
            for i in range(n):
                pltpu.make_async_remote_copy(
                    src_ref=x_refs[i].at[peer] if scatter else x_refs[i], dst_ref=land_refs[i].at[me],
                    send_sem=send_sems.at[k * n + i], recv_sem=recv_sems.at[k * n + i],
                    device_id=dev, device_id_type=pl.DeviceIdType.MESH).start()
        token[...] = jnp.zeros_like(token)

    land_shapes = [x.shape if scatter else (N_DEV,) + x.shape for x in xs]
    lands = [pltpu.with_memory_space_constraint(lax.empty(s, x.dtype), pltpu.HBM) for s, x in zip(land_shapes, xs)]
    srcs = [pltpu.with_memory_space_constraint(x, pltpu.HBM) for x in xs]
    out = pl.pallas_call(
        body, name=name,
        out_shape=(pltpu.SemaphoreType.DMA(((N_DEV - 1) * n,)), pltpu.SemaphoreType.DMA(((N_DEV - 1) * n,)),
                   *[pltpu.HBM(x.shape, x.dtype) for x in xs], *[pltpu.HBM(s, x.dtype) for s, x in zip(land_shapes, xs)],
                   _sds((8, LANES), F32)),
        in_specs=(_HBM,) * (2 * n), out_specs=(_SEM, _SEM) + (_HBM,) * (2 * n) + (pl.BlockSpec(memory_space=pltpu.VMEM),),
        input_output_aliases={i: 2 + i for i in range(2 * n)},
        compiler_params=pltpu.CompilerParams(has_side_effects=pltpu.SideEffectType.DATAFLOW_SIDE_EFFECTING))(*srcs, *lands)
    return (out[0], out[1], list(out[2:2 + n]), list(out[2 + n:2 + 2 * n])), out[-1]


def _exchange_wait(handle, after, *, scatter, name):
    send_sems, recv_sems, srcs, lands = handle
    n = len(srcs)

    def body(*refs):
        x_refs, land_refs, send_sems, recv_sems = refs[:n], refs[n:2 * n], refs[2 * n], refs[2 * n + 1]
        me, peers = _peers()
        for k, (dev, peer) in enumerate(peers):
            for i in range(n):
                cp = pltpu.make_async_remote_copy(
                    src_ref=x_refs[i].at[peer] if scatter else x_refs[i], dst_ref=land_refs[i].at[peer],
                    send_sem=send_sems.at[k * n + i], recv_sem=recv_sems.at[k * n + i],
                    device_id=dev, device_id_type=pl.DeviceIdType.MESH)
                cp.wait_send()
                cp.wait_recv()

    out = pl.pallas_call(
        body, name=name, out_shape=tuple(pltpu.HBM(a.shape, a.dtype) for a in srcs + lands),
        in_specs=(_HBM,) * (2 * n) + (_SEM, _SEM, pl.BlockSpec(memory_space=pl.ANY)), out_specs=(_HBM,) * (2 * n),
        input_output_aliases={i: i for i in range(2 * n)},
        compiler_params=pltpu.CompilerParams(has_side_effects=pltpu.SideEffectType.DATAFLOW_SIDE_EFFECTING))(
        *srcs, *lands, send_sems, recv_sems, after)
    return _with_own_slot(list(out[n:]), list(out[:n]), scatter=scatter)


def _with_own_slot(lands, xs, *, scatter):
    me = 4 * lax.axis_index("x") + 2 * lax.axis_index("y") + lax.axis_index("c")
    out = []
    for land, x in zip(lands, xs):
        own = lax.dynamic_index_in_dim(x, me, 0, keepdims=True) if scatter else x[None]
        out.append(lax.dynamic_update_index_in_dim(land, own, me, 0))
    return out


def _wire_shards(p, l, names):
    return [p[k][l].astype(WIRE) if SHARDED[k][1] else p[k][l] for k in names]


def _full_weights(got, p, names):
    full = {}
    for k, g in zip(names, got):
        axis, shape = SHARDED[k][0] - 1, p[k].shape[1:]
        g = jnp.moveaxis(g, 0, axis)
        full[k] = g.reshape(shape[:axis] + (N_DEV * shape[axis],) + shape[axis + 1:])
    return full


def _grad_parts(grads, p, names):
    parts = []
    for k in names:
        axis, g = SHARDED[k][0] - 1, grads[k]
        g = g.reshape(g.shape[:axis] + (N_DEV, p[k].shape[1:][axis]) + g.shape[axis + 1:])
        parts.append(jnp.moveaxis(g, axis, 0).astype(WIRE))
    return parts


def _allreduce_small(vals):
    metas = [(k, v.shape, v.size) for k, v in vals.items()]
    flat = jnp.concatenate([v.reshape(-1) for v in vals.values()])
    got, = _exchange([_to_rows(flat, 8)], scatter=False, name="gather_small_grads")
    summed = _sum_slots(got, name="sum_small_grads").reshape(-1)
    out, off = {}, 0
    for k, shape, n in metas:
        out[k] = summed[off:off + n].reshape(shape)
        off += n
    return out


def _row(v):
    return v.reshape(1, -1).astype(F32)


def _lanes(v):
    return jnp.pad(v.astype(F32), (0, LANES - v.shape[0])).reshape(1, LANES)


EARLY = ('w_in', 'conv_dw_w', 'ssm_conv_w')
LATE = tuple(k for k in SHARDED if k not in EARLY)
FFN = ('w_ffn_up', 'ffn_dw_w', 'w_ffn_down')
NOT_FFN = tuple(k for k in SHARDED if k not in FFN)


def _layer_weights(p, full, l, dims):
    D, SX, XBC, H, FF = dims
    assert _tile(D, GLU_TILE) == D and _tile(FF, FFN_TILE) == FF
    W = {}
    if 'w_in' in full:
        w_in = full['w_in']
        o_dt = 2 * D + SX + XBC
        w_pieces = [w_in[:, :2 * D], w_in[:, 2 * D:2 * D + SX], w_in[:, 2 * D + SX:o_dt], w_in[:, o_dt + H:]]
        a_head = -jnp.exp(p['ssm_a_log'][l].astype(F32))
        hr = -(-H // 8) * 8
        W.update(
            w_main=jnp.concatenate(w_pieces, axis=1), w_pieces=w_pieces,
            w_dt=jnp.pad(w_in[:, o_dt:o_dt + H], ((0, 0), (0, LANES - H))),
            conv_w=full['conv_dw_w'], conv_b=_row(p['conv_dw_b'][l]), conv_g=_row(p['conv_ln_g'][l]),
            conv_beta=_row(p['conv_ln_b'][l]), ssm_w=full['ssm_conv_w'], ssm_b=_row(p['ssm_conv_b'][l]),
            dt_bias=_lanes(p['ssm_dt_bias'][l]), a_col=_lanes(a_head),
            a_row=jnp.pad(a_head, (0, hr - H)).reshape(hr, 1), d_col=_lanes(p['ssm_d'][l]),
            norm_w=_row(p['ssm_norm_w'][l]), ln1_g=_row(p['ln1_g'][l]), ln1_b=_row(p['ln1_b'][l]),
            ffn_b=_row(p['ffn_dw_b'][l]), ln2_g=_row(p['ln2_g'][l]), ln2_b=_row(p['ln2_b'][l]))
    if 'w_o' in full:
        W.update(w_conv_out=full['w_conv_out'], w_ssm_out=full['w_ssm_out'], w_o=full['w_o'], w_up=full['w_ffn_up'],
                 ffn_w=full['ffn_dw_w'], w_down=full['w_ffn_down'])
    return W


def _layer_fwd(h, hb, W, late, l, dims):
    D, SX, XBC, H, FF = dims
    o_z, o_xbc, o_gate = 2 * D, 2 * D + SX, 2 * D + SX + XBC
    hr = W['a_row'].shape[0]
    t = f"l{l}"
    u = _mm(hb, W['w_main'], mode='nn', out_dtype=BF16, name=t + "_in_proj")
    udt = _mm(hb, W['w_dt'], mode='nn', out_dtype=F32, name=t + "_dt_proj")
    v1, v3 = _conf_fwd(u, W['conv_w'], W['conv_b'], W['conv_g'], W['conv_beta'], D=D, name=t + "_conf_fwd")
    xbc = _conv_act_fwd(u, o_xbc, W['ssm_w'], W['ssm_b'], ffn=False, name=t + "_ssm_conv")
    dt = _dt_fwd(udt, W['dt_bias'], H=H, name=t + "_dt")
    dtT = jnp.pad(dt[:, :H].T, ((0, hr - H), (0, 0)))
    y, hst = _ssd_fwd(xbc, dt, dtT, W['a_col'], W['a_row'], W['d_col'], H=H, name=t + "_ssd_fwd")
    yn = _rms_fwd(y, u, o_z, W['norm_w'], name=t + "_rms_fwd")
    W = {**W, **late(yn)}
    ya = _mm(v3, W['w_conv_out'], mode='nn', out_dtype=BF16, name=t + "_conv_out")
    yb = _mm(yn, W['w_ssm_out'], mode='nn', out_dtype=BF16, name=t + "_ssm_out")
    m = _gate_fwd(u, o_gate, ya, yb, name=t + "_gate_fwd")
    h1, s1, h1b = _mm_res_ln(m, W['w_o'], h, W['ln1_g'], W['ln1_b'], name=t + "_mix_ln1")
    uf = _mm(h1b, W['w_up'], mode='nn', out_dtype=BF16, name=t + "_ffn_up")
    act = _conv_act_fwd(uf, 0, W['ffn_w'], W['ffn_b'], ffn=True, name=t + "_ffn_conv")
    h2, s2, h2b = _mm_res_ln(act, W['w_down'], h1, W['ln2_g'], W['ln2_b'], name=t + "_ffn_down_ln2")
    saved = dict(hb=hb, u=u, udt=udt, v1=v1, v3=v3, ya=ya, xbc=xbc, dt=dt, dtT=dtT, y=y, hst=hst, yn=yn, yb=yb, m=m,
                 h1b=h1b, s1=s1, uf=uf, act=act, s2=s2)
    return h2, h2b, saved, W


def _layer_bwd(dh2, W, S, l, dims, on_ffn_grads=None, on_all_grads=None):
    D, SX, XBC, H, FF = dims
    o_z, o_xbc, o_gate = 2 * D, 2 * D + SX, 2 * D + SX + XBC
    t = f"l{l}"
    g = {}
    ds2, g['ln2_g'], g['ln2_b'] = _ln_bwd(S['s2'], dh2, W['ln2_g'], W['ln2_b'], silu=False, name=t + "_ln2_bwd")
    g['w_down'] = _mm(S['act'], ds2, mode='tn', out_dtype=WIRE, name=t + "_dw_down")
    dact = _mm(ds2, W['w_down'], mode='nt', out_dtype=F32, name=t + "_dact")
    duf, g['ffn_w'], g['ffn_b'] = _conv_act_bwd(S['uf'], 0, W['ffn_w'], W['ffn_b'], dact, ffn=True, name=t + "_ffn_conv_bwd")
    g['w_up'] = _mm(S['h1b'], duf, mode='tn', out_dtype=WIRE, name=t + "_dw_up")
    dh1 = _mm(duf, W['w_up'], mode='nt', out_dtype=F32, res=ds2, res_scale=DN_ALPHA, name=t + "_dh1")
    ln1_g = W['ln1_g'] if on_ffn_grads is None else W['ln1_g'] + on_ffn_grads(g)[0:1, 0:1]
    ds1, g['ln1_g'], g['ln1_b'] = _ln_bwd(S['s1'], dh1, ln1_g, W['ln1_b'], silu=False, name=t + "_ln1_bwd")
    g['w_o'] = _mm(S['m'], ds1, mode='tn', out_dtype=WIRE, name=t + "_dw_o")
    dm = _mm(ds1, W['w_o'], mode='nt', out_dtype=BF16, name=t + "_dm")
    dya, dyb, dgate = _gate_bwd(S['u'], o_gate, S['ya'], S['yb'], dm, name=t + "_gate_bwd")
    g['w_conv_out'] = _mm(S['v3'], dya, mode='tn', out_dtype=WIRE, name=t + "_dw_conv_out")
    dv3 = _mm(dya, W['w_conv_out'], mode='nt', out_dtype=BF16, name=t + "_dv3")
    dv1, g['conv_g'], g['conv_beta'] = _ln_bwd(S['v1'], dv3, W['conv_g'], W['conv_beta'], silu=True, name=t + "_conv_ln_bwd")
    dglu, g['conv_w'], g['conv_b'] = _glu_conv_bwd(dv1, S['u'], W['conv_w'], name=t + "_conf_conv_bwd")
    g['w_ssm_out'] = _mm(S['yn'], dyb, mode='tn', out_dtype=WIRE, name=t + "_dw_ssm_out")
    dyn = _mm(dyb, W['w_ssm_out'], mode='nt', out_dtype=BF16, name=t + "_dyn")
    dy, dz, g['norm_w'] = _rms_bwd(S['y'], S['u'], o_z, W['norm_w'], dyn, name=t + "_rms_bwd")
    dxbc_c, da, ddtx, g['d'] = _ssd_bwd(S['xbc'], S['dt'], S['dtT'], W['a_col'], W['a_row'], W['d_col'], S['hst'], dy,
                                        H=H, name=t + "_ssd_bwd")
    ddt_raw, g['dt_bias'], g['a_log'] = _dt_bwd(da, ddtx, S['dt'], S['udt'], W['dt_bias'], W['a_col'], H=H, name=t + "_dt_bwd")
    dxbc, g['ssm_w'], g['ssm_b'] = _conv_act_bwd(S['u'], o_xbc, W['ssm_w'], W['ssm_b'], dxbc_c, ffn=False, name=t + "_ssm_conv_bwd")
    du = [dglu, dz, dxbc, dgate]
    g['w_pieces'] = [_mm(S['hb'], d, mode='tn', out_dtype=WIRE, name=f"{t}_dw_in{i}") for i, d in enumerate(du)]
    g['w_dt'] = _mm(S['hb'], ddt_raw, mode='tn', out_dtype=WIRE, name=t + "_dw_dt")
    w_dt = W['w_dt'] if on_all_grads is None else W['w_dt'] + on_all_grads(g)[0:1, 0:1].astype(W['w_dt'].dtype)
    dh = _mm_nt_cat(du + [ddt_raw], W['w_pieces'] + [w_dt], ds1, DN_ALPHA, name=t + "_dh_in")
    return dh, g


def _layer_grads_to_params(g, dims):
    D, SX, XBC, H, FF = dims
    glu, dz, dxbc, dgate = g['w_pieces']
    w_in = jnp.concatenate([glu, dz, dxbc, g['w_dt'][:, :H], dgate], axis=1)
    return dict(
        w_in=w_in, conv_dw_w=g['conv_w'], conv_dw_b=g['conv_b'][0], conv_ln_g=g['conv_g'][0], conv_ln_b=g['conv_beta'][0],
        w_conv_out=g['w_conv_out'], ssm_conv_w=g['ssm_w'], ssm_conv_b=g['ssm_b'][0],
        ssm_dt_bias=g['dt_bias'][0, :H], ssm_a_log=g['a_log'][0, :H], ssm_d=g['d'][0, :H], ssm_norm_w=g['norm_w'][0],
        w_ssm_out=g['w_ssm_out'], w_o=g['w_o'], ln1_g=g['ln1_g'][0], ln1_b=g['ln1_b'][0],
        w_ffn_up=g['w_up'], ffn_dw_w=g['ffn_w'], ffn_dw_b=g['ffn_b'][0], w_ffn_down=g['w_down'], ln2_g=g['ln2_g'][0], ln2_b=g['ln2_b'][0])


def kernel(x, ln_in_g, ln_in_b, w_in, conv_dw_w, conv_dw_b, conv_ln_g, conv_ln_b, w_conv_out, ssm_conv_w, ssm_conv_b, ssm_dt_bias, ssm_a_log, ssm_d, ssm_norm_w, w_ssm_out, w_o, ln1_g, ln1_b, w_ffn_up, ffn_dw_w, ffn_dw_b, w_ffn_down, ln2_g, ln2_b, loss_target, m_ln_in_g, m_ln_in_b, m_w_in, m_conv_dw_w, m_conv_dw_b, m_conv_ln_g, m_conv_ln_b, m_w_conv_out, m_ssm_conv_w, m_ssm_conv_b, m_ssm_dt_bias, m_ssm_a_log, m_ssm_d, m_ssm_norm_w, m_w_ssm_out, m_w_o, m_ln1_g, m_ln1_b, m_w_ffn_up, m_ffn_dw_w, m_ffn_dw_b, m_w_ffn_down, m_ln2_g, m_ln2_b, v_ln_in_g, v_ln_in_b, v_w_in, v_conv_dw_w, v_conv_dw_b, v_conv_ln_g, v_conv_ln_b, v_w_conv_out, v_ssm_conv_w, v_ssm_conv_b, v_ssm_dt_bias, v_ssm_a_log, v_ssm_d, v_ssm_norm_w, v_w_ssm_out, v_w_o, v_ln1_g, v_ln1_b, v_w_ffn_up, v_ffn_dw_w, v_ffn_dw_b, v_w_ffn_down, v_ln2_g, v_ln2_b):
    weights = (ln_in_g, ln_in_b, w_in, conv_dw_w, conv_dw_b, conv_ln_g, conv_ln_b, w_conv_out, ssm_conv_w, ssm_conv_b,
               ssm_dt_bias, ssm_a_log, ssm_d, ssm_norm_w, w_ssm_out, w_o, ln1_g, ln1_b, w_ffn_up, ffn_dw_w, ffn_dw_b,
               w_ffn_down, ln2_g, ln2_b)
    moments_m = (m_ln_in_g, m_ln_in_b, m_w_in, m_conv_dw_w, m_conv_dw_b, m_conv_ln_g, m_conv_ln_b, m_w_conv_out,
                 m_ssm_conv_w, m_ssm_conv_b, m_ssm_dt_bias, m_ssm_a_log, m_ssm_d, m_ssm_norm_w, m_w_ssm_out, m_w_o,
                 m_ln1_g, m_ln1_b, m_w_ffn_up, m_ffn_dw_w, m_ffn_dw_b, m_w_ffn_down, m_ln2_g, m_ln2_b)
    moments_v = (v_ln_in_g, v_ln_in_b, v_w_in, v_conv_dw_w, v_conv_dw_b, v_conv_ln_g, v_conv_ln_b, v_w_conv_out,
                 v_ssm_conv_w, v_ssm_conv_b, v_ssm_dt_bias, v_ssm_a_log, v_ssm_d, v_ssm_norm_w, v_w_ssm_out, v_w_o,
                 v_ln1_g, v_ln1_b, v_w_ffn_up, v_ffn_dw_w, v_ffn_dw_b, v_w_ffn_down, v_ln2_g, v_ln2_b)
    p = dict(zip(PARAMS, weights))
    pm = dict(zip(PARAMS, moments_m))
    pv = dict(zip(PARAMS, moments_v))

    T, D = x.shape[1], x.shape[2]
    SX = w_ssm_out.shape[1] * N_DEV
    XBC = ssm_conv_b.shape[-1]
    H = ssm_d.shape[-1]
    FF = ffn_dw_b.shape[-1] // 2
    dims = (D, SX, XBC, H, FF)
    depth = w_in.shape[0]

    ALL = tuple(SHARDED)
    got = _gather_two_level(_wire_shards(p, 0, EARLY), name="gather_l0_first")
    rest_shards, got = lax.optimization_barrier((_wire_shards(p, 0, LATE), got))
    rest_handle, tok = _exchange_start(rest_shards, scatter=False, name="gather_l0_rest_start")
    first = _layer_weights(p, _full_weights(got, p, EARLY), 0, dims)
    in_flight = {}

    def start_next_gather(l, behind):
        shards, _ = lax.optimization_barrier((_wire_shards(p, l, ALL), behind))
        in_flight[l], t = _exchange_start(shards, scatter=False, name=f"gather_l{l}_start")
        in_flight[l] = (in_flight[l], shards)
        return t

    def late_weights(l):
        def late(x):
            if l > 0:
                return {}
            got = _exchange_wait(rest_handle, x, scatter=False, name="gather_l0_rest_wait")
            W = _layer_weights(p, _full_weights(got, p, LATE), 0, dims)
            if depth > 1:
                t = start_next_gather(1, got)
                W['w_conv_out'] = W['w_conv_out'] + t[0:1, 0:1].astype(W['w_conv_out'].dtype)
            return W
        return late

    xs = x.reshape(T, D)
    h, hb = _ln_fwd(xs, _row(ln_in_g) + tok[0:1, 0:1], _row(ln_in_b), name="ln_in_fwd")
    saved, layers = [], []
    for l in range(depth):
        if l > 0:
            handle, shards = in_flight.pop(l)
            got = _exchange_wait(handle, h, scatter=False, name=f"gather_l{l}_wait")
            first = _layer_weights(p, _full_weights(got, p, ALL), l, dims)
            if l + 1 < depth:
                t = start_next_gather(l + 1, got)
                first['w_dt'] = first['w_dt'] + t[0:1, 0:1].astype(first['w_dt'].dtype)
        h, hb, s, W = _layer_fwd(h, hb, first, late_weights(l), l, dims)
        saved.append(s)
        layers.append(W)
    loss_part, dh = _loss_fwd_bwd(h, loss_target.reshape(T, D), name="loss")

    layer_grads = [None] * depth
    flying, arrived = [], {}

    def wait_flying(after):
        while flying:
            l_, names, handle, parts = flying.pop(0)
            lands = _exchange_wait(handle, after, scatter=True, name=f"scatter_l{l_}_{names[0]}_wait")
            arrived.update({(l_, k): a for k, a in zip(names, lands)})
            after = lands[0]
        return after

    def start_scatter(l, names, grads_now, after):
        parts, _ = lax.optimization_barrier((_grad_parts(grads_now, p, names), wait_flying(after)))
        handle, t = _exchange_start(parts, scatter=True, name=f"scatter_l{l}_{names[0]}_start")
        flying.append((l, names, handle, parts))
        return t

    def on_ffn_grads(g):
        return start_scatter(0, FFN, dict(w_ffn_up=g['w_up'], ffn_dw_w=g['ffn_w'], w_ffn_down=g['w_down']), g['w_up'])

    def on_all_grads(g):
        layer_grads[0] = _layer_grads_to_params(g, dims)
        return start_scatter(0, NOT_FFN, layer_grads[0], g['w_dt'])

    tok = jnp.zeros((8, LANES), F32)
    for l in reversed(range(depth)):
        W = dict(layers[l], ln2_g=layers[l]['ln2_g'] + tok[0:1, 0:1])
        if l > 0:
            dh, g = _layer_bwd(dh, W, saved[l], l, dims)
            layer_grads[l] = _layer_grads_to_params(g, dims)
            tok = start_scatter(l, ALL, layer_grads[l], dh)
        else:
            dh, g = _layer_bwd(dh, W, saved[l], l, dims, on_ffn_grads, on_all_grads)
    grad_x, dg_in, db_in = _ln_bwd(xs, dh, _row(ln_in_g), _row(ln_in_b), silu=False, name="ln_in_bwd")
    wait_flying(grad_x)

    small = {k: jnp.stack([layer_grads[l][k] for l in range(depth)]) for k in layer_grads[0] if k not in SHARDED}
    small['ln_in_g'], small['ln_in_b'], small['loss'] = dg_in[0], db_in[0], loss_part
    grads = _allreduce_small(small)
    loss = grads.pop('loss').reshape(())

    delta, new_m, new_v = {}, {}, {}
    for k in SHARDED:
        grads[k], delta[k], new_m[k], new_v[k] = _sum_adamw([arrived[(l, k)] for l in range(depth)], p[k], pm[k], pv[k],
                                                            name="sum_adamw_" + k)
    rest = [k for k in PARAMS if k not in SHARDED]
    flat = lambda d: _to_rows(jnp.concatenate([d[k].reshape(-1) for k in rest]), 8)
    d_, m_, v_ = _adamw(flat(p), flat(grads), flat(pm), flat(pv), name="adamw_small")
    off = 0
    for k in rest:
        n, shp = p[k].size, p[k].shape
        delta[k] = d_.reshape(-1)[off:off + n].reshape(shp)
        new_m[k] = m_.reshape(-1)[off:off + n].reshape(shp)
        new_v[k] = v_.reshape(-1)[off:off + n].reshape(shp)
        off += n

    return (loss, grad_x.reshape(x.shape), *[grads[k] for k in PARAMS], *[delta[k] for k in PARAMS],
            *[new_m[k] for k in PARAMS], *[new_v[k] for k in PARAMS])
```
